```python
import jax, jax.numpy as jnp
from jax import lax
import numpy as np

D_MODEL = 1024
BATCH = 8
SEQ = 2048
DEPTH = 2

SG_WIDTH = D_MODEL // 2
SG_CHUNK = 128
SG_GROUPS = 4
CV_WIDTH = D_MODEL // 2
CV_KERNEL = 31
HEAD_DIM = 64
N_Q_HEADS = D_MODEL // 2 // HEAD_DIM
N_KV_HEADS = N_Q_HEADS // 4
Q_WIDTH = N_Q_HEADS * HEAD_DIM
KV_WIDTH = N_KV_HEADS * HEAD_DIM
WINDOW = 128
ROPE_THETA = 10000.0
SC_WIDTH = D_MODEL // 2
SC_KERNEL = 3
N_BRANCH = 4
BRANCH_WIDTH = D_MODEL // 2
D_FF = -(-8 * D_MODEL // (3 * 256)) * 256
EPS = 1e-6

_PROJ_WIDTHS = (2 * SG_WIDTH, 2 * CV_WIDTH, Q_WIDTH, KV_WIDTH, KV_WIDTH, 3 * SC_WIDTH, N_BRANCH * D_MODEL)
PROJ_WIDTH = sum(_PROJ_WIDTHS)
PROJ_SPLITS = tuple(sum(_PROJ_WIDTHS[:i + 1]) for i in range(len(_PROJ_WIDTHS) - 1))

kernel_name = "hybrid_gated_four_mixer_block"


def rmsnorm(x, g):
    xf = x.astype(jnp.float32)
    y = xf * lax.rsqrt(jnp.mean(xf * xf, axis=-1, keepdims=True) + EPS)
    return (y * g.astype(jnp.float32)).astype(x.dtype)


def layernorm(x, g, b):
    xf = x.astype(jnp.float32)
    mu = jnp.mean(xf, axis=-1, keepdims=True)
    var = jnp.mean(jnp.square(xf - mu), axis=-1, keepdims=True)
    y = (xf - mu) * lax.rsqrt(var + EPS)
    return (y * g.astype(jnp.float32) + b.astype(jnp.float32)).astype(x.dtype)


def causal_depthwise_conv(x, w):
    k = w.shape[0]
    return lax.conv_general_dilated(
        x, w[:, None, :].astype(x.dtype), window_strides=(1,), padding=[(k - 1, 0)],
        dimension_numbers=('NWC', 'WIO', 'NWC'), feature_group_count=x.shape[-1])


def rotary(t, cos, sin):
    t1, t2 = jnp.split(t, 2, axis=-1)
    c = cos[None, :, None, :]
    s = sin[None, :, None, :]
    return jnp.concatenate([t1 * c - t2 * s, t2 * c + t1 * s], axis=-1)


def spatial_gating(z, ln_g, ln_b, w_s, b_s):
    u, v = jnp.split(z, 2, axis=-1)
    v = layernorm(v, ln_g, ln_b)
    b, s, _ = v.shape
    nc = s // SG_CHUNK
    vc = v.reshape(b, nc, SG_CHUNK, SG_GROUPS, SG_WIDTH // SG_GROUPS)
    tril = jnp.tril(jnp.ones((SG_CHUNK, SG_CHUNK), dtype=bool))
    w = jnp.where(tril[None], w_s, jnp.zeros((), w_s.dtype))
    mixed = jnp.einsum('gts,bnsgc->bntgc', w, vc) + b_s.T[None, None, :, :, None]
    return u * mixed.reshape(b, s, SG_WIDTH)


def conformer_conv(z, w_dw, b_dw, ln_g, ln_b):
    a, gate = jnp.split(z, 2, axis=-1)
    y = a * jax.nn.sigmoid(gate)
    y = causal_depthwise_conv(y, w_dw) + b_dw
    y = layernorm(y, ln_g, ln_b)
    return jax.nn.silu(y)


def short_gated_conv(z, w_sc):
    bg, cg, h = jnp.split(z, 3, axis=-1)
    return bg * causal_depthwise_conv(cg * h, w_sc)


def sliding_window_attention(q, k, v, sinks):
    b, s, _, _ = q.shape
    nb = s // WINDOW
    g = N_Q_HEADS // N_KV_HEADS
    qb = q.reshape(b, nb, WINDOW, N_KV_HEADS, g, HEAD_DIM)

    def band(t):
        tp = jnp.pad(t, ((0, 0), (WINDOW, 0), (0, 0), (0, 0)))
        prev = tp[:, :s].reshape(b, nb, WINDOW, N_KV_HEADS, HEAD_DIM)
        cur = t.reshape(b, nb, WINDOW, N_KV_HEADS, HEAD_DIM)
        return jnp.concatenate([prev, cur], axis=2)

    kb, vb = band(k), band(v)
    scores = jnp.einsum('bnqhgd,bnkhd->bnhgqk', qb, kb).astype(jnp.float32) * (HEAD_DIM ** -0.5)
    qi = jnp.arange(WINDOW)[None, :, None]
    kj = jnp.arange(2 * WINDOW)[None, None, :]
    blk = jnp.arange(nb)[:, None, None]
    delta = qi + WINDOW - kj
    valid = (delta >= 0) & (delta < WINDOW) & (blk * WINDOW + kj - WINDOW >= 0)
    scores = jnp.where(valid[None, :, None, None], scores, -jnp.inf)
    sink = sinks.astype(jnp.float32).reshape(N_KV_HEADS, g)[None, None, :, :, None, None]
    m = jnp.maximum(jnp.max(scores, axis=-1, keepdims=True), sink)
    p = jnp.exp(scores - m)
    probs = (p / (jnp.sum(p, axis=-1, keepdims=True) + jnp.exp(sink - m))).astype(v.dtype)
    out = jnp.einsum('bnhgqk,bnkhd->bnqhgd', probs, vb)
    return out.reshape(b, s, Q_WIDTH)


def _fwd_setup_inputs(seed: int = 0) -> dict:
    key = jax.random.key(seed)
    ks = jax.random.split(key, 24)
    f32 = jnp.float32
    nrm = lambda k, shape, scale: jax.random.normal(k, shape, f32) * scale
    gain = lambda k, shape: 1.0 + 0.02 * jax.random.normal(k, shape, f32)
    return {
        "x": nrm(ks[0], (BATCH, SEQ, D_MODEL), 1.0),
        "norm_mix": gain(ks[1], (DEPTH, D_MODEL)),
        "w_in": nrm(ks[2], (DEPTH, D_MODEL, PROJ_WIDTH), D_MODEL ** -0.5),
        "sg_ln_g": gain(ks[3], (DEPTH, SG_WIDTH)),
        "sg_ln_b": nrm(ks[4], (DEPTH, SG_WIDTH), 0.02),
        "sg_w": nrm(ks[5], (DEPTH, SG_GROUPS, SG_CHUNK, SG_CHUNK), SG_CHUNK ** -0.5),
        "sg_b": 1.0 + nrm(ks[6], (DEPTH, SG_GROUPS, SG_CHUNK), 0.1),
        "cv_w": nrm(ks[7], (DEPTH, CV_KERNEL, CV_WIDTH), CV_KERNEL ** -0.5),
        "cv_b": nrm(ks[8], (DEPTH, CV_WIDTH), 0.02),
        "cv_ln_g": gain(ks[9], (DEPTH, CV_WIDTH)),
        "cv_ln_b": nrm(ks[10], (DEPTH, CV_WIDTH), 0.02),
        "attn_sinks": nrm(ks[11], (DEPTH, N_Q_HEADS), 1.0),
        "sc_w": nrm(ks[12], (DEPTH, SC_KERNEL, SC_WIDTH), SC_KERNEL ** -0.5),
        "w_branch": nrm(ks[13], (DEPTH, N_BRANCH, BRANCH_WIDTH, D_MODEL), BRANCH_WIDTH ** -0.5),
        "w_out": nrm(ks[14], (DEPTH, D_MODEL, D_MODEL), 0.5 * D_MODEL ** -0.5),
        "norm_ffn": gain(ks[15], (DEPTH, D_MODEL)),
        "w_gate_up": nrm(ks[16], (DEPTH, D_MODEL, 2 * D_FF), D_MODEL ** -0.5),
        "w_down": nrm(ks[17], (DEPTH, D_FF, D_MODEL), D_FF ** -0.5),
        "norm_final": gain(ks[18], (D_MODEL,)),
    }


def _fwd_reference(x, norm_mix, w_in, sg_ln_g, sg_ln_b, sg_w, sg_b, cv_w, cv_b, cv_ln_g, cv_ln_b,
              attn_sinks, sc_w, w_branch, w_out, norm_ffn, w_gate_up, w_down, norm_final):
    b, s, _ = x.shape
    pos = jnp.arange(s, dtype=jnp.float32)
    inv_freq = 1.0 / (ROPE_THETA ** (jnp.arange(0, HEAD_DIM, 2, dtype=jnp.float32) / HEAD_DIM))
    ang = pos[:, None] * inv_freq[None, :]
    cos = jnp.cos(ang).astype(x.dtype)
    sin = jnp.sin(ang).astype(x.dtype)

    for l in range(DEPTH):
        xn = rmsnorm(x, norm_mix[l])
        proj = xn @ w_in[l]
        z_a, z_b, q, k, v, z_d, z_g = jnp.split(proj, PROJ_SPLITS, axis=-1)

        y_a = spatial_gating(jax.nn.gelu(z_a, approximate=False), sg_ln_g[l], sg_ln_b[l], sg_w[l], sg_b[l])
        y_b = conformer_conv(z_b, cv_w[l], cv_b[l], cv_ln_g[l], cv_ln_b[l])
        q = rotary(q.reshape(b, s, N_Q_HEADS, HEAD_DIM), cos, sin)
        k = rotary(k.reshape(b, s, N_KV_HEADS, HEAD_DIM), cos, sin)
        v = v.reshape(b, s, N_KV_HEADS, HEAD_DIM)
        y_c = sliding_window_attention(q, k, v, attn_sinks[l])
        y_d = short_gated_conv(z_d, sc_w[l])

        ys = jnp.stack([y_a, y_b, y_c, y_d], axis=0)
        branch = jnp.einsum('nbsc,ncd->nbsd', ys, w_branch[l])
        gates = jax.nn.sigmoid(z_g.reshape(b, s, N_BRANCH, D_MODEL))
        merged = jnp.einsum('bsnd,nbsd->bsd', gates, branch)
        x = x + merged @ w_out[l]

        hn = rmsnorm(x, norm_ffn[l])
        gate, up = jnp.split(hn @ w_gate_up[l], 2, axis=-1)
        x = x + (jax.nn.silu(gate) * up) @ w_down[l]

    return rmsnorm(x, norm_final)


import jax as _jax
import jax.numpy as _jnp

TWIN_FORMAT = 'train_step'
FWD_PARAMS = ['x', 'norm_mix', 'w_in', 'sg_ln_g', 'sg_ln_b', 'sg_w', 'sg_b', 'cv_w', 'cv_b', 'cv_ln_g', 'cv_ln_b', 'attn_sinks', 'sc_w', 'w_branch', 'w_out', 'norm_ffn', 'w_gate_up', 'w_down', 'norm_final']
TWIN_WEIGHTS = ['norm_mix', 'w_in', 'sg_ln_g', 'sg_ln_b', 'sg_w', 'sg_b', 'cv_w', 'cv_b', 'cv_ln_g', 'cv_ln_b', 'attn_sinks', 'sc_w', 'w_branch', 'w_out', 'norm_ffn', 'w_gate_up', 'w_down', 'norm_final']
TWIN_DIFF_INPUT = 'x'
TWIN_INPUTS = ['x', 'norm_mix', 'w_in', 'sg_ln_g', 'sg_ln_b', 'sg_w', 'sg_b', 'cv_w', 'cv_b', 'cv_ln_g', 'cv_ln_b', 'attn_sinks', 'sc_w', 'w_branch', 'w_out', 'norm_ffn', 'w_gate_up', 'w_down', 'norm_final', 'loss_target', 'm_norm_mix', 'm_w_in', 'm_sg_ln_g', 'm_sg_ln_b', 'm_sg_w', 'm_sg_b', 'm_cv_w', 'm_cv_b', 'm_cv_ln_g', 'm_cv_ln_b', 'm_attn_sinks', 'm_sc_w', 'm_w_branch', 'm_w_out', 'm_norm_ffn', 'm_w_gate_up', 'm_w_down', 'm_norm_final', 'v_norm_mix', 'v_w_in', 'v_sg_ln_g', 'v_sg_ln_b', 'v_sg_w', 'v_sg_b', 'v_cv_w', 'v_cv_b', 'v_cv_ln_g', 'v_cv_ln_b', 'v_attn_sinks', 'v_sc_w', 'v_w_branch', 'v_w_out', 'v_norm_ffn', 'v_w_gate_up', 'v_w_down', 'v_norm_final']
TWIN_OUTPUTS = ['loss', 'grad_x', 'grad_norm_mix', 'grad_w_in', 'grad_sg_ln_g', 'grad_sg_ln_b', 'grad_sg_w', 'grad_sg_b', 'grad_cv_w', 'grad_cv_b', 'grad_cv_ln_g', 'grad_cv_ln_b', 'grad_attn_sinks', 'grad_sc_w', 'grad_w_branch', 'grad_w_out', 'grad_norm_ffn', 'grad_w_gate_up', 'grad_w_down', 'grad_norm_final', 'delta_norm_mix', 'delta_w_in', 'delta_sg_ln_g', 'delta_sg_ln_b', 'delta_sg_w', 'delta_sg_b', 'delta_cv_w', 'delta_cv_b', 'delta_cv_ln_g', 'delta_cv_ln_b', 'delta_attn_sinks', 'delta_sc_w', 'delta_w_branch', 'delta_w_out', 'delta_norm_ffn', 'delta_w_gate_up', 'delta_w_down', 'delta_norm_final', 'new_m_norm_mix', 'new_m_w_in', 'new_m_sg_ln_g', 'new_m_sg_ln_b', 'new_m_sg_w', 'new_m_sg_b', 'new_m_cv_w', 'new_m_cv_b', 'new_m_cv_ln_g', 'new_m_cv_ln_b', 'new_m_attn_sinks', 'new_m_sc_w', 'new_m_w_branch', 'new_m_w_out', 'new_m_norm_ffn', 'new_m_w_gate_up', 'new_m_w_down', 'new_m_norm_final', 'new_v_norm_mix', 'new_v_w_in', 'new_v_sg_ln_g', 'new_v_sg_ln_b', 'new_v_sg_w', 'new_v_sg_b', 'new_v_cv_w', 'new_v_cv_b', 'new_v_cv_ln_g', 'new_v_cv_ln_b', 'new_v_attn_sinks', 'new_v_sc_w', 'new_v_w_branch', 'new_v_w_out', 'new_v_norm_ffn', 'new_v_w_gate_up', 'new_v_w_down', 'new_v_norm_final']
TWIN_LEAF_KINDS = {'loss': 'loss', 'grad_x': 'grad_x', 'grad_norm_mix': 'grad_w', 'grad_w_in': 'grad_w', 'grad_sg_ln_g': 'grad_w', 'grad_sg_ln_b': 'grad_w', 'grad_sg_w': 'grad_w', 'grad_sg_b': 'grad_w', 'grad_cv_w': 'grad_w', 'grad_cv_b': 'grad_w', 'grad_cv_ln_g': 'grad_w', 'grad_cv_ln_b': 'grad_w', 'grad_attn_sinks': 'grad_w', 'grad_sc_w': 'grad_w', 'grad_w_branch': 'grad_w', 'grad_w_out': 'grad_w', 'grad_norm_ffn': 'grad_w', 'grad_w_gate_up': 'grad_w', 'grad_w_down': 'grad_w', 'grad_norm_final': 'grad_w', 'delta_norm_mix': 'delta_w', 'delta_w_in': 'delta_w', 'delta_sg_ln_g': 'delta_w', 'delta_sg_ln_b': 'delta_w', 'delta_sg_w': 'delta_w', 'delta_sg_b': 'delta_w', 'delta_cv_w': 'delta_w', 'delta_cv_b': 'delta_w', 'delta_cv_ln_g': 'delta_w', 'delta_cv_ln_b': 'delta_w', 'delta_attn_sinks': 'delta_w', 'delta_sc_w': 'delta_w', 'delta_w_branch': 'delta_w', 'delta_w_out': 'delta_w', 'delta_norm_ffn': 'delta_w', 'delta_w_gate_up': 'delta_w', 'delta_w_down': 'delta_w', 'delta_norm_final': 'delta_w', 'new_m_norm_mix': 'new_m', 'new_m_w_in': 'new_m', 'new_m_sg_ln_g': 'new_m', 'new_m_sg_ln_b': 'new_m', 'new_m_sg_w': 'new_m', 'new_m_sg_b': 'new_m', 'new_m_cv_w': 'new_m', 'new_m_cv_b': 'new_m', 'new_m_cv_ln_g': 'new_m', 'new_m_cv_ln_b': 'new_m', 'new_m_attn_sinks': 'new_m', 'new_m_sc_w': 'new_m', 'new_m_w_branch': 'new_m', 'new_m_w_out': 'new_m', 'new_m_norm_ffn': 'new_m', 'new_m_w_gate_up': 'new_m', 'new_m_w_down': 'new_m', 'new_m_norm_final': 'new_m', 'new_v_norm_mix': 'new_v', 'new_v_w_in': 'new_v', 'new_v_sg_ln_g': 'new_v', 'new_v_sg_ln_b': 'new_v', 'new_v_sg_w': 'new_v', 'new_v_sg_b': 'new_v', 'new_v_cv_w': 'new_v', 'new_v_cv_b': 'new_v', 'new_v_cv_ln_g': 'new_v', 'new_v_cv_ln_b': 'new_v', 'new_v_attn_sinks': 'new_v', 'new_v_sc_w': 'new_v', 'new_v_w_branch': 'new_v', 'new_v_w_out': 'new_v', 'new_v_norm_ffn': 'new_v', 'new_v_w_gate_up': 'new_v', 'new_v_w_down': 'new_v', 'new_v_norm_final': 'new_v'}


def _forward(args):
    return _fwd_reference(*[args[k] for k in FWD_PARAMS])


def _output_shape():
    out = _jax.eval_shape(lambda: _forward(_fwd_setup_inputs(0)))
    return out.shape, out.dtype

N_MICROBATCH = 1
ADAM_LR = 0.001
ADAM_B1 = 0.9
ADAM_B2 = 0.999
ADAM_EPS = 1e-08
ADAM_WD = 0.01
ADAM_STEP = 10
PER_EXAMPLE_BATCH_AXIS = {'x': 0, 'loss_target': 0}
SHARED_INPUTS = []
_WEIGHT_DTYPES = {'norm_mix': _jnp.float32, 'w_in': _jnp.float32, 'sg_ln_g': _jnp.float32, 'sg_ln_b': _jnp.float32, 'sg_w': _jnp.float32, 'sg_b': _jnp.float32, 'cv_w': _jnp.float32, 'cv_b': _jnp.float32, 'cv_ln_g': _jnp.float32, 'cv_ln_b': _jnp.float32, 'attn_sinks': _jnp.float32, 'sc_w': _jnp.float32, 'w_branch': _jnp.float32, 'w_out': _jnp.float32, 'norm_ffn': _jnp.float32, 'w_gate_up': _jnp.float32, 'w_down': _jnp.float32, 'norm_final': _jnp.float32}
MOMENT_SCALE = {'norm_mix': 8.087254e-02, 'w_in': 2.675932e-02, 'sg_ln_g': 2.394480e-02, 'sg_ln_b': 2.073724e-02, 'sg_w': 2.344911e-02, 'sg_b': 3.246554e-02, 'cv_w': 3.118806e-02, 'cv_b': 6.939503e-02, 'cv_ln_g': 3.800905e-02, 'cv_ln_b': 3.133546e-02, 'attn_sinks': 9.672546e-03, 'sc_w': 4.998494e-02, 'w_branch': 2.508293e-02, 'w_out': 1.004216e-01, 'norm_ffn': 9.282265e-02, 'w_gate_up': 3.808460e-02, 'w_down': 6.205777e-02, 'norm_final': 1.602443e+01}


def _to_microbatches(a, axis):
    t = _jnp.moveaxis(a, axis, 0)
    t = t.reshape((N_MICROBATCH, t.shape[0] // N_MICROBATCH) + t.shape[1:])
    return _jnp.moveaxis(t, 1, axis + 1)


def setup_inputs(seed: int = 0) -> dict:
    inp = _fwd_setup_inputs(seed)
    key = _jax.random.fold_in(_jax.random.key(seed), 7919)
    shape, _ = _output_shape()
    out = dict(inp)
    out["loss_target"] = _jax.random.normal(_jax.random.fold_in(key, 0), shape, _jnp.float32)
    for i, name in enumerate(TWIN_WEIGHTS):
        w = inp[name].astype(_jnp.float32)
        if MOMENT_SCALE is None:
            s = _jnp.sqrt(_jnp.mean(_jnp.square(w)) + 1e-30)
        else:
            s = MOMENT_SCALE[name]
        km, kv = _jax.random.split(_jax.random.fold_in(key, i + 1))
        out[name] = w
        out["m_" + name] = s * _jax.random.normal(km, w.shape, _jnp.float32)
        out["v_" + name] = (s * s) * _jax.random.uniform(kv, w.shape, _jnp.float32, 0.5, 1.5)
    if N_MICROBATCH > 1:
        for name, axis in PER_EXAMPLE_BATCH_AXIS.items():
            out[name] = _to_microbatches(out[name], axis)
    return {'x': out['x'], 'norm_mix': out['norm_mix'], 'w_in': out['w_in'], 'sg_ln_g': out['sg_ln_g'], 'sg_ln_b': out['sg_ln_b'], 'sg_w': out['sg_w'], 'sg_b': out['sg_b'], 'cv_w': out['cv_w'], 'cv_b': out['cv_b'], 'cv_ln_g': out['cv_ln_g'], 'cv_ln_b': out['cv_ln_b'], 'attn_sinks': out['attn_sinks'], 'sc_w': out['sc_w'], 'w_branch': out['w_branch'], 'w_out': out['w_out'], 'norm_ffn': out['norm_ffn'], 'w_gate_up': out['w_gate_up'], 'w_down': out['w_down'], 'norm_final': out['norm_final'], 'loss_target': out['loss_target'], 'm_norm_mix': out['m_norm_mix'], 'm_w_in': out['m_w_in'], 'm_sg_ln_g': out['m_sg_ln_g'], 'm_sg_ln_b': out['m_sg_ln_b'], 'm_sg_w': out['m_sg_w'], 'm_sg_b': out['m_sg_b'], 'm_cv_w': out['m_cv_w'], 'm_cv_b': out['m_cv_b'], 'm_cv_ln_g': out['m_cv_ln_g'], 'm_cv_ln_b': out['m_cv_ln_b'], 'm_attn_sinks': out['m_attn_sinks'], 'm_sc_w': out['m_sc_w'], 'm_w_branch': out['m_w_branch'], 'm_w_out': out['m_w_out'], 'm_norm_ffn': out['m_norm_ffn'], 'm_w_gate_up': out['m_w_gate_up'], 'm_w_down': out['m_w_down'], 'm_norm_final': out['m_norm_final'], 'v_norm_mix': out['v_norm_mix'], 'v_w_in': out['v_w_in'], 'v_sg_ln_g': out['v_sg_ln_g'], 'v_sg_ln_b': out['v_sg_ln_b'], 'v_sg_w': out['v_sg_w'], 'v_sg_b': out['v_sg_b'], 'v_cv_w': out['v_cv_w'], 'v_cv_b': out['v_cv_b'], 'v_cv_ln_g': out['v_cv_ln_g'], 'v_cv_ln_b': out['v_cv_ln_b'], 'v_attn_sinks': out['v_attn_sinks'], 'v_sc_w': out['v_sc_w'], 'v_w_branch': out['v_w_branch'], 'v_w_out': out['v_w_out'], 'v_norm_ffn': out['v_norm_ffn'], 'v_w_gate_up': out['v_w_gate_up'], 'v_w_down': out['v_w_down'], 'v_norm_final': out['v_norm_final']}


def _loss(weights, diff, rest, loss_target):
    with _jax.named_scope("forward"):
        args = {**rest, TWIN_DIFF_INPUT: diff, **{k: w.astype(_WEIGHT_DTYPES[k]) for k, w in weights.items()}}
        y = _forward(args)
    with _jax.named_scope("loss_head"):
        err = _jnp.square(y.astype(_jnp.float32) - loss_target)
        return 0.5 * _jnp.sum(_jnp.mean(err, axis=-1)) if err.ndim else 0.5 * err


def _adamw(w, g, m, v):
    m = ADAM_B1 * m + (1.0 - ADAM_B1) * g
    v = ADAM_B2 * v + (1.0 - ADAM_B2) * _jnp.square(g)
    m_hat = m / (1.0 - ADAM_B1 ** ADAM_STEP)
    v_hat = v / (1.0 - ADAM_B2 ** ADAM_STEP)
    delta = -ADAM_LR * (m_hat / (_jnp.sqrt(v_hat) + ADAM_EPS) + ADAM_WD * w)
    return delta, m, v


def reference(x, norm_mix, w_in, sg_ln_g, sg_ln_b, sg_w, sg_b, cv_w, cv_b, cv_ln_g, cv_ln_b, attn_sinks, sc_w, w_branch, w_out, norm_ffn, w_gate_up, w_down, norm_final, loss_target, m_norm_mix, m_w_in, m_sg_ln_g, m_sg_ln_b, m_sg_w, m_sg_b, m_cv_w, m_cv_b, m_cv_ln_g, m_cv_ln_b, m_attn_sinks, m_sc_w, m_w_branch, m_w_out, m_norm_ffn, m_w_gate_up, m_w_down, m_norm_final, v_norm_mix, v_w_in, v_sg_ln_g, v_sg_ln_b, v_sg_w, v_sg_b, v_cv_w, v_cv_b, v_cv_ln_g, v_cv_ln_b, v_attn_sinks, v_sc_w, v_w_branch, v_w_out, v_norm_ffn, v_w_gate_up, v_w_down, v_norm_final):
    given = dict(x=x, norm_mix=norm_mix, w_in=w_in, sg_ln_g=sg_ln_g, sg_ln_b=sg_ln_b, sg_w=sg_w, sg_b=sg_b, cv_w=cv_w, cv_b=cv_b, cv_ln_g=cv_ln_g, cv_ln_b=cv_ln_b, attn_sinks=attn_sinks, sc_w=sc_w, w_branch=w_branch, w_out=w_out, norm_ffn=norm_ffn, w_gate_up=w_gate_up, w_down=w_down, norm_final=norm_final, loss_target=loss_target, m_norm_mix=m_norm_mix, m_w_in=m_w_in, m_sg_ln_g=m_sg_ln_g, m_sg_ln_b=m_sg_ln_b, m_sg_w=m_sg_w, m_sg_b=m_sg_b, m_cv_w=m_cv_w, m_cv_b=m_cv_b, m_cv_ln_g=m_cv_ln_g, m_cv_ln_b=m_cv_ln_b, m_attn_sinks=m_attn_sinks, m_sc_w=m_sc_w, m_w_branch=m_w_branch, m_w_out=m_w_out, m_norm_ffn=m_norm_ffn, m_w_gate_up=m_w_gate_up, m_w_down=m_w_down, m_norm_final=m_norm_final, v_norm_mix=v_norm_mix, v_w_in=v_w_in, v_sg_ln_g=v_sg_ln_g, v_sg_ln_b=v_sg_ln_b, v_sg_w=v_sg_w, v_sg_b=v_sg_b, v_cv_w=v_cv_w, v_cv_b=v_cv_b, v_cv_ln_g=v_cv_ln_g, v_cv_ln_b=v_cv_ln_b, v_attn_sinks=v_attn_sinks, v_sc_w=v_sc_w, v_w_branch=v_w_branch, v_w_out=v_w_out, v_norm_ffn=v_norm_ffn, v_w_gate_up=v_w_gate_up, v_w_down=v_w_down, v_norm_final=v_norm_final)
    weights = {n: given[n] for n in TWIN_WEIGHTS}
    shared = {n: given[n] for n in SHARED_INPUTS}
    per_example = {n: given[n] for n in ['x']}
    grad_fn = _jax.value_and_grad(_loss, argnums=(0, 1))

    def one_microbatch(ex, loss_target):
        ex = dict(ex)
        diff = ex.pop(TWIN_DIFF_INPUT)
        return grad_fn(weights, diff, {**shared, **ex}, loss_target)

    if N_MICROBATCH == 1:
        loss, (grad_w, grad_x) = one_microbatch(per_example, given["loss_target"])
    else:
        def body(carry, xs):
            loss_sum, grad_sum = carry
            l_k, (gw_k, gx_k) = one_microbatch(xs[0], xs[1])
            with _jax.named_scope("update"):
                return (loss_sum + l_k, _jax.tree.map(_jnp.add, grad_sum, gw_k)), gx_k

        init = (_jnp.zeros((), _jnp.float32), _jax.tree.map(_jnp.zeros_like, weights))
        (loss, grad_w), grad_x = _jax.lax.scan(body, init, (per_example, given["loss_target"]))
    with _jax.named_scope("update"):
        delta_w, new_m, new_v = {}, {}, {}
        for n in TWIN_WEIGHTS:
            delta_w[n], new_m[n], new_v[n] = _adamw(weights[n], grad_w[n], given["m_" + n], given["v_" + n])
    return (loss, grad_x, *[grad_w[n] for n in TWIN_WEIGHTS], *[delta_w[n] for n in TWIN_WEIGHTS],
            *[new_m[n] for n in TWIN_WEIGHTS], *[new_v[n] for n in TWIN_WEIGHTS])
```

```python
import functools
import math

import jax
import jax.numpy as jnp
from jax import lax
from jax.experimental import pallas as pl
from jax.experimental.pallas import tpu as pltpu

F32 = jnp.float32
BF16 = jnp.bfloat16

D_MODEL = 1024
DEPTH = 2
HALF = 512
SG_CHUNK = 128
SG_GROUPS = 4
CV_KERNEL = 31
HEAD_DIM = 64
N_Q_HEADS = 8
N_KV_HEADS = 2
Q_PER_KV = N_Q_HEADS // N_KV_HEADS
WINDOW = 128
ROPE_THETA = 10000.0
SC_KERNEL = 3
N_BRANCH = 4
D_FF = 2816
EPS = 1e-6
N_CHIPS = 4
N_DEV = 8

MIX_W = 4352
GATE_W = N_BRANCH * D_MODEL
PROJ_PAD = 2 * MIX_W
W_IN_SHARD = 2112
GU_SHARD = 1408
HALO = 128

ADAM_LR = 0.001
ADAM_B1 = 0.9
ADAM_B2 = 0.999
ADAM_EPS = 1e-08
ADAM_WD = 0.01
ADAM_STEP = 10

VMEM_LIMIT = 56 * 1024 * 1024
INV_SQRT2 = 1.0 / math.sqrt(2.0)
INV_SQRT_2PI = 1.0 / math.sqrt(2.0 * math.pi)
NEG_BIG = -1e30
MESH = pl.DeviceIdType.MESH

C_ZA, C_ZB, C_Q, C_K, C_V, C_ZD = 0, 1024, 2048, 2560, 2688, 2816


def _params(*sem):
    return pltpu.CompilerParams(dimension_semantics=sem, vmem_limit_bytes=VMEM_LIMIT)


def _sig(v):
    return 1.0 / (1.0 + jnp.exp(-v))


def _dot(a, b):
    return jnp.dot(a, b, preferred_element_type=F32)


def _dot_nt(a, b):
    return lax.dot_general(a, b, (((1,), (1,)), ((), ())), preferred_element_type=F32)


def _dot_tn(a, b):
    return lax.dot_general(a, b, (((0,), (0,)), ((), ())), preferred_element_type=F32)


def _full(shape):
    nd = len(shape)
    return pl.BlockSpec(shape, lambda *_: (0,) * nd)


def _rms_mm(x, g, w, tm, tn, name):
    T = x.shape[0]
    if w.ndim == 3:
        tn = w.shape[2]
        N = w.shape[0] * tn
        wspec = pl.BlockSpec((None, D_MODEL, tn), lambda i, j: (j, 0, 0))
    else:
        N = w.shape[1]
        wspec = pl.BlockSpec((D_MODEL, tn), lambda i, j: (0, j))

    def body(x_ref, g_ref, w_ref, o_ref, xn_ref):
        @pl.when(pl.program_id(1) == 0)
        def _():
            xv = x_ref[...]
            r = lax.rsqrt(jnp.mean(xv * xv, axis=-1, keepdims=True) + EPS)
            xn_ref[...] = (xv * r * g_ref[...]).astype(BF16)

        o_ref[...] = _dot(xn_ref[...], w_ref[...]).astype(BF16)

    return pl.pallas_call(
        body, name=name, grid=(T // tm, N // tn),
        in_specs=[pl.BlockSpec((tm, D_MODEL), lambda i, j: (i, 0)), _full((1, D_MODEL)), wspec],
        out_specs=[pl.BlockSpec((tm, tn), lambda i, j: (i, j)), pl.BlockSpec((tm, D_MODEL), lambda i, j: (i, 0))],
        out_shape=[jax.ShapeDtypeStruct((T, N), BF16), jax.ShapeDtypeStruct((T, D_MODEL), BF16)],
        compiler_params=_params("parallel", "arbitrary"),
    )(x, g, w)


def _merge_fwd(x, ys, proj, wb, wo, tm, name):
    T = x.shape[0]

    def body(x_ref, ys_ref, zg_ref, wb_ref, wo_ref, xo_ref, mg_ref):
        merged = None
        for n in range(N_BRANCH):
            yn = ys_ref[:, n * HALF:(n + 1) * HALF]
            br = jnp.concatenate([_dot(yn, wb_ref[s, n]) for s in range(N_CHIPS)], axis=1)
            t = _sig(zg_ref[:, n * D_MODEL:(n + 1) * D_MODEL].astype(F32)) * br
            merged = t if merged is None else merged + t
        mb = merged.astype(BF16)
        mg_ref[...] = mb
        xo_ref[...] = x_ref[...] + _dot(mb, wo_ref[...])

    return pl.pallas_call(
        body, name=name, grid=(T // tm,),
        in_specs=[pl.BlockSpec((tm, D_MODEL), lambda i: (i, 0)), pl.BlockSpec((tm, N_BRANCH * HALF), lambda i: (i, 0)),
                  pl.BlockSpec((tm, GATE_W), lambda i: (i, 0)), _full(wb.shape), _full(wo.shape)],
        out_specs=[pl.BlockSpec((tm, D_MODEL), lambda i: (i, 0)), pl.BlockSpec((tm, D_MODEL), lambda i: (i, 0))],
        out_shape=[jax.ShapeDtypeStruct((T, D_MODEL), F32), jax.ShapeDtypeStruct((T, D_MODEL), BF16)],
        compiler_params=_params("parallel"),
    )(x, ys, proj, wb, wo)


def _ffn_down(xm, gu, wd, tm, name):
    T = xm.shape[0]

    def body(x_ref, gu_ref, wd_ref, o_ref):
        g = gu_ref[:, :D_FF].astype(F32)
        u = gu_ref[:, D_FF:].astype(F32)
        act = (g * _sig(g) * u).astype(BF16)
        o_ref[...] = x_ref[...] + _dot(act, wd_ref[...])

    return pl.pallas_call(
        body, name=name, grid=(T // tm,),
        in_specs=[pl.BlockSpec((tm, D_MODEL), lambda i: (i, 0)), pl.BlockSpec((tm, 2 * D_FF), lambda i: (i, 0)), _full(wd.shape)],
        out_specs=pl.BlockSpec((tm, D_MODEL), lambda i: (i, 0)),
        out_shape=jax.ShapeDtypeStruct((T, D_MODEL), F32),
        compiler_params=_params("parallel"),
    )(xm, gu, wd)


def _final_loss(x, g, tgt, tm, name):
    T = x.shape[0]

    def body(x_ref, g_ref, t_ref, dx_ref, dg_ref, ls_ref):
        @pl.when(pl.program_id(0) == 0)
        def _():
            dg_ref[...] = jnp.zeros_like(dg_ref)
            ls_ref[...] = jnp.zeros_like(ls_ref)

        xv = x_ref[...]
        gv = g_ref[...]
        r = lax.rsqrt(jnp.mean(xv * xv, axis=-1, keepdims=True) + EPS)
        xh = xv * r
        diff = xh * gv - t_ref[...]
        ls_ref[...] += jnp.full(ls_ref.shape, 0.5 / D_MODEL, F32) * jnp.sum(diff * diff)
        dy = diff * (1.0 / D_MODEL)
        dxh = dy * gv
        dx_ref[...] = r * (dxh - xh * jnp.mean(dxh * xh, axis=-1, keepdims=True))
        dg_ref[...] += jnp.sum(dy * xh, axis=0, keepdims=True)

    return pl.pallas_call(
        body, name=name, grid=(T // tm,),
        in_specs=[pl.BlockSpec((tm, D_MODEL), lambda i: (i, 0)), _full((1, D_MODEL)), pl.BlockSpec((tm, D_MODEL), lambda i: (i, 0))],
        out_specs=[pl.BlockSpec((tm, D_MODEL), lambda i: (i, 0)), _full((1, D_MODEL)), _full((1, 128))],
        out_shape=[jax.ShapeDtypeStruct((T, D_MODEL), F32), jax.ShapeDtypeStruct((1, D_MODEL), F32), jax.ShapeDtypeStruct((1, 128), F32)],
        compiler_params=_params("arbitrary"),
    )(x, g, tgt)


def _swiglu_bwd(dx, gu, wd, tm, name):
    T = dx.shape[0]

    def body(dx_ref, gu_ref, wd_ref, dgu_ref, act_ref):
        dact = _dot_nt(dx_ref[...].astype(BF16), wd_ref[...])
        g = gu_ref[:, :D_FF].astype(F32)
        u = gu_ref[:, D_FF:].astype(F32)
        s = _sig(g)
        silu = g * s
        act_ref[...] = (silu * u).astype(BF16)
        dgu_ref[:, :D_FF] = (dact * u * (s + silu * (1.0 - s))).astype(BF16)
        dgu_ref[:, D_FF:] = (dact * silu).astype(BF16)

    return pl.pallas_call(
        body, name=name, grid=(T // tm,),
        in_specs=[pl.BlockSpec((tm, D_MODEL), lambda i: (i, 0)), pl.BlockSpec((tm, 2 * D_FF), lambda i: (i, 0)), _full(wd.shape)],
        out_specs=[pl.BlockSpec((tm, 2 * D_FF), lambda i: (i, 0)), pl.BlockSpec((tm, D_FF), lambda i: (i, 0))],
        out_shape=[jax.ShapeDtypeStruct((T, 2 * D_FF), BF16), jax.ShapeDtypeStruct((T, D_FF), BF16)],
        compiler_params=_params("parallel"),
    )(dx, gu, wd)


def _mm_tn(a, b, grid, a_block, a_map, b_block, b_map, o_shape, o_block, o_map, name):
    gk = grid[2]
    tm = [d for d in a_block if d is not None][-1]
    tn = [d for d in b_block if d is not None][-1]

    def body(a_ref, b_ref, o_ref, acc_ref):
        k = pl.program_id(2)
        p = _dot_tn(a_ref[...].astype(BF16), b_ref[...].astype(BF16))

        @pl.when(k == 0)
        def _():
            acc_ref[...] = p

        @pl.when(k > 0)
        def _():
            acc_ref[...] += p

        @pl.when(k == gk - 1)
        def _():
            o_ref[...] = acc_ref[...].astype(o_ref.dtype)

    return pl.pallas_call(
        body, name=name, grid=grid,
        in_specs=[pl.BlockSpec(a_block, a_map), pl.BlockSpec(b_block, b_map)],
        out_specs=pl.BlockSpec(o_block, o_map),
        out_shape=jax.ShapeDtypeStruct(o_shape, BF16),
        scratch_shapes=[pltpu.VMEM((tm, tn), F32)],
        compiler_params=_params("parallel", "parallel", "arbitrary"),
    )(a, b)


def _mm_nt_rmsbwd(a, w, x, g, dres, tm, tk, name):
    T = x.shape[0]
    if w.ndim == 3:
        tk = w.shape[2]
        gk = w.shape[0]
        wspec = pl.BlockSpec((None, D_MODEL, tk), lambda i, k: (k, 0, 0))
    else:
        gk = w.shape[1] // tk
        wspec = pl.BlockSpec((D_MODEL, tk), lambda i, k: (0, k))

    def body(a_ref, w_ref, x_ref, g_ref, r_ref, dx_ref, dg_ref, acc_ref):
        i, k = pl.program_id(0), pl.program_id(1)
        p = _dot_nt(a_ref[...], w_ref[...])

        @pl.when(k == 0)
        def _():
            acc_ref[...] = p

        @pl.when(k > 0)
        def _():
            acc_ref[...] += p

        @pl.when(jnp.logical_and(i == 0, k == 0))
        def _():
            dg_ref[...] = jnp.zeros_like(dg_ref)

        @pl.when(k == gk - 1)
        def _():
            dh = acc_ref[...]
            xv = x_ref[...]
            r = lax.rsqrt(jnp.mean(xv * xv, axis=-1, keepdims=True) + EPS)
            xh = xv * r
            dxh = dh * g_ref[...]
            dx_ref[...] = r_ref[...] + r * (dxh - xh * jnp.mean(dxh * xh, axis=-1, keepdims=True))
            dg_ref[...] += jnp.sum(dh * xh, axis=0, keepdims=True)

    return pl.pallas_call(
        body, name=name, grid=(T // tm, gk),
        in_specs=[pl.BlockSpec((tm, tk), lambda i, k: (i, k)), wspec, pl.BlockSpec((tm, D_MODEL), lambda i, k: (i, 0)),
                  _full((1, D_MODEL)), pl.BlockSpec((tm, D_MODEL), lambda i, k: (i, 0))],
        out_specs=[pl.BlockSpec((tm, D_MODEL), lambda i, k: (i, 0)), _full((1, D_MODEL))],
        out_shape=[jax.ShapeDtypeStruct((T, D_MODEL), F32), jax.ShapeDtypeStruct((1, D_MODEL), F32)],
        scratch_shapes=[pltpu.VMEM((tm, D_MODEL), F32)],
        compiler_params=_params("arbitrary", "arbitrary"),
    )(a, w, x, g, dres)


def _merge_bwd(dxm, ys, proj, wb, wo, tm, name):
    T = dxm.shape[0]

    def body(dx_ref, ys_ref, zg_ref, wb_ref, wo_ref, dys_ref, dbr_ref, dp_ref):
        dmerged = _dot_nt(dx_ref[...].astype(BF16), wo_ref[...])
        for n in range(N_BRANCH):
            yn = ys_ref[:, n * HALF:(n + 1) * HALF]
            br = jnp.concatenate([_dot(yn, wb_ref[s, n]) for s in range(N_CHIPS)], axis=1)
            gt = _sig(zg_ref[:, n * D_MODEL:(n + 1) * D_MODEL].astype(F32))
            dbr = (gt * dmerged).astype(BF16)
            dbr_ref[:, n * D_MODEL:(n + 1) * D_MODEL] = dbr
            dp_ref[:, n * D_MODEL:(n + 1) * D_MODEL] = (dmerged * br * gt * (1.0 - gt)).astype(BF16)
            dy = None
            for s in range(N_CHIPS):
                t = _dot_nt(dbr[:, s * 256:(s + 1) * 256], wb_ref[s, n])
                dy = t if dy is None else dy + t
            dys_ref[:, n * HALF:(n + 1) * HALF] = dy.astype(BF16)
        dp_ref[:, GATE_W:] = jnp.zeros((tm, MIX_W - GATE_W), BF16)

    return pl.pallas_call(
        body, name=name, grid=(T // tm,),
        in_specs=[pl.BlockSpec((tm, D_MODEL), lambda i: (i, 0)), pl.BlockSpec((tm, N_BRANCH * HALF), lambda i: (i, 0)),
                  pl.BlockSpec((tm, GATE_W), lambda i: (i, 0)), _full(wb.shape), _full(wo.shape)],
        out_specs=[pl.BlockSpec((tm, N_BRANCH * HALF), lambda i: (i, 0)), pl.BlockSpec((tm, GATE_W), lambda i: (i, 0)),
                   pl.BlockSpec((tm, MIX_W), lambda i: (i, 0))],
        out_shape=[jax.ShapeDtypeStruct((T, N_BRANCH * HALF), BF16), jax.ShapeDtypeStruct((T, GATE_W), BF16),
                   jax.ShapeDtypeStruct((T, PROJ_PAD), BF16)],
        compiler_params=_params("parallel"),
    )(dxm, ys, proj, wb, wo)


def _gelu(v):
    return 0.5 * v * (1.0 + lax.erf(v * INV_SQRT2))


def _gelu_grad(v):
    return 0.5 * (1.0 + lax.erf(v * INV_SQRT2)) + v * jnp.exp(-0.5 * v * v) * INV_SQRT_2PI


def _rot_half(t):
    w = t.shape[1]
    lane = lax.broadcasted_iota(jnp.int32, t.shape, 1)
    return jnp.where((lane % HEAD_DIM) < HEAD_DIM // 2, pltpu.roll(t, w - HEAD_DIM // 2, 1), pltpu.roll(t, HEAD_DIM // 2, 1))


def _rope(t, cos, sin_signed):
    return t * cos + _rot_half(t) * sin_signed


def _rope_t(d, cos, sin_signed):
    return d * cos + _rot_half(d * sin_signed)


def _ln_fwd(v, g, b):
    mu = jnp.mean(v, axis=-1, keepdims=True)
    vc = v - mu
    r = lax.rsqrt(jnp.mean(vc * vc, axis=-1, keepdims=True) + EPS)
    vh = vc * r
    return vh * g + b, vh, r


def _ln_bwd(dn, vh, r, g):
    dvh = dn * g
    return r * (dvh - jnp.mean(dvh, axis=-1, keepdims=True) - vh * jnp.mean(dvh * vh, axis=-1, keepdims=True))


def _tril_mask():
    return lax.broadcasted_iota(jnp.int32, (SG_CHUNK, SG_CHUNK), 0) >= lax.broadcasted_iota(jnp.int32, (SG_CHUNK, SG_CHUNK), 1)


def _attn_probs(qs, kh, sink_col, first_ok):
    s = _dot_nt(qs, kh) * (HEAD_DIM ** -0.5)
    row = lax.broadcasted_iota(jnp.int32, s.shape, 0) % WINDOW
    col = lax.broadcasted_iota(jnp.int32, s.shape, 1)
    valid = (col > row) & (col <= row + WINDOW) & ((col >= WINDOW) | first_ok)
    s = jnp.where(valid, s, NEG_BIG)
    m = jnp.maximum(jnp.max(s, axis=-1, keepdims=True), sink_col)
    p = jnp.where(valid, jnp.exp(s - m), 0.0)
    es = jnp.exp(sink_col - m)
    inv = 1.0 / (jnp.sum(p, axis=-1, keepdims=True) + es)
    return p * inv, es * inv


def _sink_col(sinks_ref, h):
    return jnp.concatenate([jnp.broadcast_to(sinks_ref[:, h * Q_PER_KV + g:h * Q_PER_KV + g + 1], (WINDOW, 1))
                            for g in range(Q_PER_KV)], axis=0)


def _mixer_in_specs(TB, nb):
    r = TB // HALO
    last = nb * r - 1
    cur = pl.BlockSpec((TB, MIX_W), lambda i: (i, 1))
    prev = pl.BlockSpec((HALO, MIX_W), lambda i: (jnp.maximum(i * r - 1, 0), 1))
    nxt = pl.BlockSpec((HALO, MIX_W), lambda i: (jnp.minimum((i + 1) * r, last), 1))
    tcur = pl.BlockSpec((TB, 128), lambda i: (i, 0))
    tprev = pl.BlockSpec((HALO, 128), lambda i: (jnp.maximum(i * r - 1, 0), 0))
    tnxt = pl.BlockSpec((HALO, 128), lambda i: (jnp.minimum((i + 1) * r, last), 0))
    return cur, prev, nxt, tcur, tprev, tnxt


def _mixer_param_specs():
    return [_full((1, HALF)), _full((1, HALF)), _full((SG_GROUPS, SG_CHUNK, SG_CHUNK)), _full((SG_CHUNK, 128)),
            _full((32, HALF)), _full((1, HALF)), _full((1, HALF)), _full((1, HALF)), _full((1, 128)), _full((8, HALF))]


def _mixers_fwd(proj, cos_t, sin_t, mp, TB, name):
    T = proj.shape[0]
    nb = T // TB
    r = TB // HALO
    cur, prev, _, tcur, tprev, _ = _mixer_in_specs(TB, nb)

    def body(zc_ref, zp_ref, cc_ref, sc_ref, cp_ref, sp_ref,
             lg_ref, lb_ref, sgw_ref, sgb_ref, cvw_ref, cvb_ref, cvg_ref, cvbb_ref, sinks_ref, scw_ref,
             ys_ref, scr_ref, k_ref, v_ref):
        i = pl.program_id(0)
        pm = (i > 0).astype(F32)

        def colsE(c0, c1):
            return jnp.concatenate([zp_ref[:, c0:c1].astype(F32) * pm, zc_ref[:, c0:c1].astype(F32)], axis=0)

        a = _gelu(zc_ref[:, C_ZA:C_ZA + 2 * HALF].astype(F32))
        u = a[:, :HALF]
        vn, _, _ = _ln_fwd(a[:, HALF:], lg_ref[...], lb_ref[...])
        vnb = vn.astype(BF16)
        tril = _tril_mask()
        for g in range(SG_GROUPS):
            wt = jnp.where(tril, sgw_ref[g], 0.0).astype(BF16)
            for ci in range(r):
                rows = slice(ci * SG_CHUNK, (ci + 1) * SG_CHUNK)
                cols = slice(g * 128, (g + 1) * 128)
                mixed = _dot(wt, vnb[rows, cols]) + sgb_ref[:, g:g + 1]
                ys_ref[rows, g * 128:(g + 1) * 128] = (u[rows, cols] * mixed).astype(BF16)

        zb = colsE(C_ZB, C_ZB + 2 * HALF)
        scr_ref[...] = zb[:, :HALF] * _sig(zb[:, HALF:])
        c = jnp.broadcast_to(cvb_ref[...], (TB, HALF))
        for k in range(CV_KERNEL):
            c = c + cvw_ref[k:k + 1, :] * scr_ref[pl.ds(HALO - (CV_KERNEL - 1) + k, TB), :]
        n, _, _ = _ln_fwd(c, cvg_ref[...], cvbb_ref[...])
        ys_ref[:, HALF:2 * HALF] = (n * _sig(n)).astype(BF16)

        zd = colsE(C_ZD + HALF, C_ZD + 3 * HALF)
        scr_ref[...] = zd[:, :HALF] * zd[:, HALF:]
        cv = None
        for k in range(SC_KERNEL):
            t = scw_ref[k:k + 1, :] * scr_ref[pl.ds(HALO - (SC_KERNEL - 1) + k, TB), :]
            cv = t if cv is None else cv + t
        ys_ref[:, 3 * HALF:4 * HALF] = (zc_ref[:, C_ZD:C_ZD + HALF].astype(F32) * cv).astype(BF16)

        cosE = jnp.concatenate([cp_ref[...], cc_ref[...]], axis=0)
        sinE = jnp.concatenate([sp_ref[...], sc_ref[...]], axis=0)
        k_ref[...] = _rope(colsE(C_K, C_K + 128), cosE, sinE).astype(BF16)
        v_ref[...] = colsE(C_V, C_V + 128).astype(BF16)
        cosC, sinC = cc_ref[...], sc_ref[...]
        q = jnp.concatenate([_rope(zc_ref[:, C_Q + 128 * j:C_Q + 128 * (j + 1)].astype(F32), cosC, sinC)
                             for j in range(4)], axis=1).astype(BF16)
        for qb in range(r):
            first_ok = (i * r + qb) > 0
            for h in range(N_KV_HEADS):
                hc = slice(h * HEAD_DIM, (h + 1) * HEAD_DIM)
                kh = k_ref[qb * WINDOW:qb * WINDOW + 2 * WINDOW, hc]
                vh = v_ref[qb * WINDOW:qb * WINDOW + 2 * WINDOW, hc]
                qs = jnp.concatenate([q[qb * WINDOW:(qb + 1) * WINDOW, (h * Q_PER_KV + g) * HEAD_DIM:(h * Q_PER_KV + g + 1) * HEAD_DIM]
                                      for g in range(Q_PER_KV)], axis=0)
                probs, _ = _attn_probs(qs, kh, _sink_col(sinks_ref, h), first_ok)
                o = _dot(probs.astype(BF16), vh)
                for g in range(Q_PER_KV):
                    c0 = 2 * HALF + (h * Q_PER_KV + g) * HEAD_DIM
                    ys_ref[qb * WINDOW:(qb + 1) * WINDOW, c0:c0 + HEAD_DIM] = o[g * WINDOW:(g + 1) * WINDOW].astype(BF16)

    return pl.pallas_call(
        body, name=name, grid=(nb,),
        in_specs=[cur, prev, tcur, tcur, tprev, tprev] + _mixer_param_specs(),
        out_specs=pl.BlockSpec((TB, 4 * HALF), lambda i: (i, 0)),
        out_shape=jax.ShapeDtypeStruct((T, 4 * HALF), BF16),
        scratch_shapes=[pltpu.VMEM((TB + HALO, HALF), F32), pltpu.VMEM((TB + HALO, 128), BF16), pltpu.VMEM((TB + HALO, 128), BF16)],
        compiler_params=_params("parallel"),
    )(proj, proj, cos_t, sin_t, cos_t, sin_t, *mp)


def _mixers_bwd(proj, dys, dproj, cos_t, sin_t, mp, TB, name):
    T = proj.shape[0]
    nb = T // TB
    r = TB // HALO
    RE = TB + 2 * HALO
    RC = TB + HALO
    cur, prev, nxt, tcur, tprev, tnxt = _mixer_in_specs(TB, nb)
    dcur = pl.BlockSpec((TB, 4 * HALF), lambda i: (i, 0))
    dnxt = pl.BlockSpec((HALO, 4 * HALF), lambda i: (jnp.minimum((i + 1) * r, nb * r - 1), 0))

    def body(zc_ref, zp_ref, zn_ref, dyc_ref, dyn_ref, cc_ref, sc_ref, cp_ref, sp_ref, cn_ref, sn_ref,
             lg_ref, lb_ref, sgw_ref, sgb_ref, cvw_ref, cvb_ref, cvg_ref, cvbb_ref, sinks_ref, scw_ref, dp_in_ref,
             dz_ref, dlg_ref, dlb_ref, dsgw_ref, dsgb_ref, dcvw_ref, dcvb_ref, dcvg_ref, dcvbb_ref, dsink_ref, dscw_ref,
             scr_ref, scr2_ref, k_ref, v_ref, dk_ref, dv_ref, dq_ref):
        del dp_in_ref
        i = pl.program_id(0)
        pm = (i > 0).astype(F32)
        nm = (i < nb - 1).astype(F32)

        @pl.when(i == 0)
        def _():
            for ref in (dlg_ref, dlb_ref, dsgw_ref, dsgb_ref, dcvw_ref, dcvb_ref, dcvg_ref, dcvbb_ref, dsink_ref, dscw_ref):
                ref[...] = jnp.zeros_like(ref)

        def colsE(c0, c1):
            return jnp.concatenate([zp_ref[:, c0:c1].astype(F32) * pm, zc_ref[:, c0:c1].astype(F32),
                                    zn_ref[:, c0:c1].astype(F32)], axis=0)

        def colsC(c0, c1):
            return jnp.concatenate([zc_ref[:, c0:c1].astype(F32), zn_ref[:, c0:c1].astype(F32)], axis=0)

        def dyC(c0, c1):
            return jnp.concatenate([dyc_ref[:, c0:c1].astype(F32), dyn_ref[:, c0:c1].astype(F32) * nm], axis=0)

        own = (lax.broadcasted_iota(jnp.int32, (RC, 1), 0) < TB).astype(F32)

        za = zc_ref[:, C_ZA:C_ZA + 2 * HALF].astype(F32)
        a = _gelu(za)
        u = a[:, :HALF]
        lg = lg_ref[...]
        vn, vh, rs = _ln_fwd(a[:, HALF:], lg, lb_ref[...])
        vnb = vn.astype(BF16)
        dya = dyc_ref[:, 0:HALF].astype(F32)
        tril = _tril_mask()
        lane128 = lax.broadcasted_iota(jnp.int32, (SG_CHUNK, 128), 1)
        du_parts, dvn_parts = [], []
        for ci in range(r):
            rows = slice(ci * SG_CHUNK, (ci + 1) * SG_CHUNK)
            du_g, dvn_g = [], []
            for g in range(SG_GROUPS):
                cols = slice(g * 128, (g + 1) * 128)
                wt = jnp.where(tril, sgw_ref[g], 0.0).astype(BF16)
                vb = vnb[rows, cols]
                mixed = _dot(wt, vb) + sgb_ref[:, g:g + 1]
                dy_blk = dya[rows, cols]
                du_g.append(dy_blk * mixed)
                dmix = dy_blk * u[rows, cols]
                dmb = dmix.astype(BF16)
                dvn_g.append(_dot_tn(wt, dmb))
                dsgw_ref[g] += jnp.where(tril, _dot_nt(dmb, vb), 0.0)
                dsgb_ref[...] += jnp.where(lane128 == g, jnp.sum(dmix, axis=1, keepdims=True), 0.0)
            du_parts.append(jnp.concatenate(du_g, axis=1))
            dvn_parts.append(jnp.concatenate(dvn_g, axis=1))
        du = jnp.concatenate(du_parts, axis=0) if r > 1 else du_parts[0]
        dvn = jnp.concatenate(dvn_parts, axis=0) if r > 1 else dvn_parts[0]
        dlg_ref[...] += jnp.sum(dvn * vh, axis=0, keepdims=True)
        dlb_ref[...] += jnp.sum(dvn, axis=0, keepdims=True)
        dvv = _ln_bwd(dvn, vh, rs, lg)
        gg = _gelu_grad(za)
        dz_ref[:, C_ZA:C_ZA + HALF] = (du * gg[:, :HALF]).astype(BF16)
        dz_ref[:, C_ZA + HALF:C_ZA + 2 * HALF] = (dvv * gg[:, HALF:]).astype(BF16)

        zb = colsE(C_ZB, C_ZB + 2 * HALF)
        scr_ref[...] = zb[:, :HALF] * _sig(zb[:, HALF:])
        c = jnp.broadcast_to(cvb_ref[...], (RC, HALF))
        for k in range(CV_KERNEL):
            c = c + cvw_ref[k:k + 1, :] * scr_ref[pl.ds(HALO - (CV_KERNEL - 1) + k, RC), :]
        cvg = cvg_ref[...]
        n, ch, rc = _ln_fwd(c, cvg, cvbb_ref[...])
        sn = _sig(n)
        dn = dyC(HALF, 2 * HALF) * (sn + n * sn * (1.0 - sn))
        dno = dn * own
        dcvg_ref[...] += jnp.sum(dno * ch, axis=0, keepdims=True)
        dcvbb_ref[...] += jnp.sum(dno, axis=0, keepdims=True)
        dc = _ln_bwd(dn, ch, rc, cvg)
        scr2_ref[...] = dc
        dcvb_ref[...] += jnp.sum(dc[:TB], axis=0, keepdims=True)
        dy0 = None
        for k in range(CV_KERNEL):
            wk = cvw_ref[k:k + 1, :]
            t = wk * scr2_ref[pl.ds(CV_KERNEL - 1 - k, TB), :]
            dy0 = t if dy0 is None else dy0 + t
            dcvw_ref[k:k + 1, :] += jnp.sum(dc[:TB] * scr_ref[pl.ds(HALO - (CV_KERNEL - 1) + k, TB), :], axis=0, keepdims=True)
        ab = zc_ref[:, C_ZB:C_ZB + HALF].astype(F32)
        sg = _sig(zc_ref[:, C_ZB + HALF:C_ZB + 2 * HALF].astype(F32))
        dz_ref[:, C_ZB:C_ZB + HALF] = (dy0 * sg).astype(BF16)
        dz_ref[:, C_ZB + HALF:C_ZB + 2 * HALF] = (dy0 * ab * sg * (1.0 - sg)).astype(BF16)

        zd = colsE(C_ZD + HALF, C_ZD + 3 * HALF)
        scr_ref[...] = zd[:, :HALF] * zd[:, HALF:]
        dcv = dyC(3 * HALF, 4 * HALF) * colsC(C_ZD, C_ZD + HALF)
        scr2_ref[...] = dcv
        cv = None
        dud = None
        for k in range(SC_KERNEL):
            wk = scw_ref[k:k + 1, :]
            us = scr_ref[pl.ds(HALO - (SC_KERNEL - 1) + k, TB), :]
            t = wk * us
            cv = t if cv is None else cv + t
            t2 = wk * scr2_ref[pl.ds(SC_KERNEL - 1 - k, TB), :]
            dud = t2 if dud is None else dud + t2
            dscw_ref[k:k + 1, :] += jnp.sum(dcv[:TB] * us, axis=0, keepdims=True)
        dz_ref[:, C_ZD:C_ZD + HALF] = (dyc_ref[:, 3 * HALF:4 * HALF].astype(F32) * cv).astype(BF16)
        dz_ref[:, C_ZD + HALF:C_ZD + 2 * HALF] = (dud * zc_ref[:, C_ZD + 2 * HALF:C_ZD + 3 * HALF].astype(F32)).astype(BF16)
        dz_ref[:, C_ZD + 2 * HALF:C_ZD + 3 * HALF] = (dud * zc_ref[:, C_ZD + HALF:C_ZD + 2 * HALF].astype(F32)).astype(BF16)

        cosE = jnp.concatenate([cp_ref[...], cc_ref[...], cn_ref[...]], axis=0)
        sinE = jnp.concatenate([sp_ref[...], sc_ref[...], sn_ref[...]], axis=0)
        k_ref[...] = _rope(colsE(C_K, C_K + 128), cosE, sinE).astype(BF16)
        v_ref[...] = colsE(C_V, C_V + 128).astype(BF16)
        dk_ref[...] = jnp.zeros_like(dk_ref)
        dv_ref[...] = jnp.zeros_like(dv_ref)
        q = jnp.concatenate([_rope(colsC(C_Q + 128 * j, C_Q + 128 * (j + 1)), cosE[HALO:], sinE[HALO:])
                             for j in range(4)], axis=1).astype(BF16)
        dO = dyC(2 * HALF, 3 * HALF).astype(BF16)
        lane_s = lax.broadcasted_iota(jnp.int32, (1, 128), 1)
        for qb in range(r + 1):
            first_ok = (i * r + qb) > 0
            rows = slice(qb * WINDOW, (qb + 1) * WINDOW)
            band = slice(qb * WINDOW, qb * WINDOW + 2 * WINDOW)
            for h in range(N_KV_HEADS):
                hc = slice(h * HEAD_DIM, (h + 1) * HEAD_DIM)
                kh = k_ref[band, hc]
                vh_ = v_ref[band, hc]
                heads = [slice((h * Q_PER_KV + g) * HEAD_DIM, (h * Q_PER_KV + g + 1) * HEAD_DIM) for g in range(Q_PER_KV)]
                qs = jnp.concatenate([q[rows, hs] for hs in heads], axis=0)
                dos = jnp.concatenate([dO[rows, hs] for hs in heads], axis=0)
                probs, p_sink = _attn_probs(qs, kh, _sink_col(sinks_ref, h), first_ok)
                dP = _dot_nt(dos, vh_)
                rsum = jnp.sum(probs * dP, axis=-1, keepdims=True)
                dS = (probs * (dP - rsum) * (HEAD_DIM ** -0.5)).astype(BF16)
                dk_ref[band, hc] += _dot_tn(dS, qs)
                dv_ref[band, hc] += _dot_tn(probs.astype(BF16), dos)
                if qb < r:
                    dqs = _dot(dS, kh)
                    dsk = -p_sink * rsum
                    for g in range(Q_PER_KV):
                        dq_ref[rows, heads[g]] = dqs[g * WINDOW:(g + 1) * WINDOW]
                        dsink_ref[...] += jnp.where(lane_s == h * Q_PER_KV + g, jnp.sum(dsk[g * WINDOW:(g + 1) * WINDOW]), 0.0)
        cosC, sinC = cc_ref[...], sc_ref[...]
        for j in range(4):
            dz_ref[:, C_Q + 128 * j:C_Q + 128 * (j + 1)] = _rope_t(dq_ref[:, 128 * j:128 * (j + 1)], cosC, sinC).astype(BF16)
        dz_ref[:, C_K:C_K + 128] = _rope_t(dk_ref[HALO:HALO + TB, :], cosC, sinC).astype(BF16)
        dz_ref[:, C_V:C_V + 128] = dv_ref[HALO:HALO + TB, :].astype(BF16)

    small = [((1, HALF), F32), ((1, HALF), F32), ((SG_GROUPS, SG_CHUNK, SG_CHUNK), F32), ((SG_CHUNK, 128), F32),
             ((32, HALF), F32), ((1, HALF), F32), ((1, HALF), F32), ((1, HALF), F32), ((1, 128), F32), ((8, HALF), F32)]
    outs = pl.pallas_call(
        body, name=name, grid=(nb,),
        in_specs=[cur, prev, nxt, dcur, dnxt, tcur, tcur, tprev, tprev, tnxt, tnxt] + _mixer_param_specs()
                 + [pl.BlockSpec(memory_space=pl.ANY)],
        out_specs=[pl.BlockSpec((TB, MIX_W), lambda i: (i, 1))] + [_full(s) for s, _ in small],
        out_shape=[jax.ShapeDtypeStruct((T, PROJ_PAD), BF16)] + [jax.ShapeDtypeStruct(s, d) for s, d in small],
        scratch_shapes=[pltpu.VMEM((RE, HALF), F32), pltpu.VMEM((RC, HALF), F32), pltpu.VMEM((RE, 128), BF16), pltpu.VMEM((RE, 128), BF16),
                        pltpu.VMEM((RE, 128), F32), pltpu.VMEM((RE, 128), F32), pltpu.VMEM((TB, HALF), F32)],
        input_output_aliases={21: 0},
        compiler_params=_params("arbitrary"),
    )(proj, proj, proj, dys, dys, cos_t, sin_t, cos_t, sin_t, cos_t, sin_t, *mp, dproj)
    return outs


def _rope_tables(T):
    pos = jnp.arange(T, dtype=F32)
    inv_freq = 1.0 / (ROPE_THETA ** (jnp.arange(0, HEAD_DIM, 2, dtype=F32) / HEAD_DIM))
    ang = pos[:, None] * inv_freq[None, :]
    cos, sin = jnp.cos(ang), jnp.sin(ang)
    cos_t = jnp.concatenate([cos, cos, cos, cos], axis=1)
    sin_t = jnp.concatenate([-sin, sin, -sin, sin], axis=1)
    return cos_t, sin_t


def _mixer_params(l, sg_ln_g, sg_ln_b, sg_w, sg_b, cv_w, cv_b, cv_ln_g, cv_ln_b, attn_sinks, sc_w):
    sgb_t = jnp.zeros((SG_CHUNK, 128), F32).at[:, :SG_GROUPS].set(sg_b[l].T)
    cvw = jnp.zeros((32, HALF), F32).at[:CV_KERNEL].set(cv_w[l])
    scw = jnp.zeros((8, HALF), F32).at[:SC_KERNEL].set(sc_w[l])
    sinks = jnp.zeros((1, 128), F32).at[0, :N_Q_HEADS].set(attn_sinks[l])
    return [sg_ln_g[l][None], sg_ln_b[l][None], sg_w[l], sgb_t, cvw, cv_b[l][None], cv_ln_g[l][None], cv_ln_b[l][None], sinks, scw]


def _w_in_layout(w_in_g):
    wmix = jnp.concatenate([w_in_g[0], w_in_g[1], w_in_g[2][:, :MIX_W - 2 * W_IN_SHARD]], axis=1)
    wg = jnp.concatenate([w_in_g[2][:, MIX_W - 2 * W_IN_SHARD:], w_in_g[3]], axis=1)
    return jnp.concatenate([wg, jnp.zeros((D_MODEL, MIX_W - GATE_W), w_in_g.dtype), wmix], axis=1)


def _w_in_unlayout(dw):
    cut = MIX_W - 2 * W_IN_SHARD
    return jnp.stack([dw[:, MIX_W:MIX_W + W_IN_SHARD], dw[:, MIX_W + W_IN_SHARD:MIX_W + 2 * W_IN_SHARD],
                      jnp.concatenate([dw[:, MIX_W + 2 * W_IN_SHARD:], dw[:, :W_IN_SHARD - cut]], axis=1),
                      dw[:, W_IN_SHARD - cut:GATE_W]], axis=0)


def _device_step(x, tgt, norm_mix, norm_ffn, norm_final, mixer_params, w_in_p, wb_g, wo_g, wgu_g, wd_g):
    T = x.shape[0]
    TM = min(T, 1024)
    TB = 256
    cos_t, sin_t = _rope_tables(T)
    saved = []
    for l in range(DEPTH):
        proj, xn = _rms_mm(x, norm_mix[l][None], w_in_p[l], TM, 2176, f"proj{l}")
        ys = _mixers_fwd(proj, cos_t, sin_t, mixer_params[l], TB, f"mixers_fwd{l}")
        xm, merged = _merge_fwd(x, ys, proj, wb_g[l], wo_g[l], 256, f"merge_fwd{l}")
        gu, hn = _rms_mm(xm, norm_ffn[l][None], wgu_g[l], TM, GU_SHARD, f"ffn_up{l}")
        x_out = _ffn_down(xm, gu, wd_g[l], 256, f"ffn_down{l}")
        saved.append((x, proj, xn, ys, xm, merged, gu, hn))
        x = x_out
    dx, dnf, loss = _final_loss(x, norm_final[None], tgt, 256, "final_loss")

    tkk = min(T, 1024)
    gk = T // tkk
    grads = [None] * DEPTH
    for l in reversed(range(DEPTH)):
        x_in, proj, xn, ys, xm, merged, gu, hn = saved[l]
        dgu, act = _swiglu_bwd(dx, gu, wd_g[l], 256, f"swiglu_bwd{l}")
        d_wd = _mm_tn(act, dx, (2, 1, gk), (tkk, D_FF // 2), lambda i, j, k: (k, i), (tkk, D_MODEL), lambda i, j, k: (k, 0),
                      (D_FF, D_MODEL), (D_FF // 2, D_MODEL), lambda i, j, k: (i, 0), f"dw_down{l}")
        d_wgu = _mm_tn(hn, dgu, (1, N_CHIPS, gk), (tkk, D_MODEL), lambda i, j, k: (k, 0), (tkk, GU_SHARD), lambda i, j, k: (k, j),
                       (N_CHIPS, D_MODEL, GU_SHARD), (None, D_MODEL, GU_SHARD), lambda i, j, k: (j, 0, 0), f"dw_gate_up{l}")
        dxm, d_nffn = _mm_nt_rmsbwd(dgu, wgu_g[l], xm, norm_ffn[l][None], dx, min(T, 512), GU_SHARD, f"ffn_up_bwd{l}")
        dys, dbr, dproj = _merge_bwd(dxm, ys, proj, wb_g[l], wo_g[l], 256, f"merge_bwd{l}")
        d_wo = _mm_tn(merged, dxm, (2, 1, gk), (tkk, 512), lambda i, j, k: (k, i), (tkk, D_MODEL), lambda i, j, k: (k, 0),
                      (D_MODEL, D_MODEL), (512, D_MODEL), lambda i, j, k: (i, 0), f"dw_out{l}")
        d_wb = _mm_tn(ys, dbr, (N_BRANCH, N_CHIPS, gk), (tkk, HALF), lambda i, j, k: (k, i), (tkk, 256), lambda i, j, k: (k, i * N_CHIPS + j),
                      (N_CHIPS, N_BRANCH, HALF, 256), (None, None, HALF, 256), lambda i, j, k: (j, i, 0, 0), f"dw_branch{l}")
        mb = _mixers_bwd(proj, dys, dproj, cos_t, sin_t, mixer_params[l], TB, f"mixers_bwd{l}")
        dproj = mb[0]
        d_win = _mm_tn(xn, dproj, (1, PROJ_PAD // 2176, gk), (tkk, D_MODEL), lambda i, j, k: (k, 0), (tkk, 2176), lambda i, j, k: (k, j),
                       (D_MODEL, PROJ_PAD), (D_MODEL, 2176), lambda i, j, k: (0, j), f"dw_in{l}")
        dx, d_nmix = _mm_nt_rmsbwd(dproj, w_in_p[l], x_in, norm_mix[l][None], dxm, min(T, 512), 2176, f"proj_bwd{l}")
        grads[l] = dict(w_in=d_win, w_branch=d_wb, w_out=d_wo, w_gate_up=d_wgu, w_down=d_wd,
                        norm_mix=d_nmix[0], norm_ffn=d_nffn[0],
                        sg_ln_g=mb[1][0], sg_ln_b=mb[2][0], sg_w=mb[3], sg_b=mb[4][:, :SG_GROUPS].T,
                        cv_w=mb[5][:CV_KERNEL], cv_b=mb[6][0], cv_ln_g=mb[7][0], cv_ln_b=mb[8][0],
                        attn_sinks=mb[9][0, :N_Q_HEADS], sc_w=mb[10][:SC_KERNEL])
    return loss, dx, dnf[0], grads


ANY = pl.BlockSpec(memory_space=pl.ANY)
BIG = ("w_in", "w_branch", "w_out", "w_gate_up", "w_down")
HALF_SHAPE = {"w_in": (2, 512, W_IN_SHARD), "w_branch": (2, 1024, 256), "w_out": (2, 128, D_MODEL),
              "w_gate_up": (2, 512, GU_SHARD), "w_down": (2, 352, D_MODEL)}
NB = len(BIG)


def _place():
    x, y, c = lax.axis_index("x"), lax.axis_index("y"), lax.axis_index("c")
    chips = [(1 - x, y), (x, 1 - y), (1 - x, 1 - y)]
    return x, y, c, 2 * x + y, chips, [2 * px + py for px, py in chips]


def _remote(src, dst, ssem, rsem, dev):
    return pltpu.make_async_remote_copy(src_ref=src, dst_ref=dst, send_sem=ssem, recv_sem=rsem, device_id=dev, device_id_type=MESH)


def _ag_weights(shards):
    def body(*refs):
        ins, outs = refs[:NB], refs[NB:2 * NB]
        s_ici, r_ici, s_d2d, r_d2d, loc = refs[2 * NB:]
        x, y, c, chip, chips, chip_ids = _place()
        sib = (x, y, 1 - c)
        local, sends, fwd = [], [], []
        for l in range(DEPTH):
            for a in range(NB):
                cp = pltpu.make_async_copy(ins[a].at[l], outs[a].at[l, chip], loc.at[a, l])
                cp.start()
                local.append(cp)
                for j in range(3):
                    cp = _remote(ins[a].at[l, c], outs[a].at[l, chip, c], s_ici.at[a, l, j], r_ici.at[a, l, j], (*chips[j], c))
                    cp.start()
                    sends.append(cp)
        for l in range(DEPTH):
            for a in range(NB):
                for j in range(3):
                    got = outs[a].at[l, chip_ids[j], c]
                    _remote(got, got, s_ici.at[a, l, j], r_ici.at[a, l, j], sib).wait_recv()
                    cp = _remote(got, got, s_d2d.at[a, l, j], r_d2d.at[a, l, j], sib)
                    cp.start()
                    fwd.append(cp)
        for l in range(DEPTH):
            for a in range(NB):
                for j in range(3):
                    got = outs[a].at[l, chip_ids[j], 1 - c]
                    _remote(got, got, s_d2d.at[a, l, j], r_d2d.at[a, l, j], sib).wait_recv()
        for cp in sends + fwd:
            cp.wait_send()
        for cp in local:
            cp.wait()

    out_shape = [jax.ShapeDtypeStruct((DEPTH, N_CHIPS) + HALF_SHAPE[n], BF16) for n in BIG]
    sem = pltpu.SemaphoreType.DMA((NB, DEPTH, 3))
    return pl.pallas_call(
        body, name="ag_weights", out_shape=out_shape, in_specs=[ANY] * NB, out_specs=[ANY] * NB,
        scratch_shapes=[sem, sem, sem, sem, pltpu.SemaphoreType.DMA((NB, DEPTH))],
        compiler_params=pltpu.CompilerParams(has_side_effects=True),
    )(*shards)


def _rs_pair(grads):
    def body(*refs):
        ins = [refs[l * NB:(l + 1) * NB] for l in range(DEPTH)]
        own, got = refs[DEPTH * NB:(DEPTH + 1) * NB], refs[(DEPTH + 1) * NB:(DEPTH + 2) * NB]
        ssem, rsem, loc = refs[(DEPTH + 2) * NB:]
        x, y, c, _, _, _ = _place()
        sib = (x, y, 1 - c)
        local, sends = [], []
        for l in reversed(range(DEPTH)):
            for a in range(NB):
                for q in range(N_CHIPS):
                    cp = _remote(ins[l][a].at[q, 1 - c], got[a].at[l, q], ssem.at[a, l, q], rsem.at[a, l, q], sib)
                    cp.start()
                    sends.append(cp)
                    cp = pltpu.make_async_copy(ins[l][a].at[q, c], own[a].at[l, q], loc.at[a, l, q])
                    cp.start()
                    local.append(cp)
        for l in range(DEPTH):
            for a in range(NB):
                for q in range(N_CHIPS):
                    dst = got[a].at[l, q]
                    _remote(dst, dst, ssem.at[a, l, q], rsem.at[a, l, q], sib).wait_recv()
        for cp in sends:
            cp.wait_send()
        for cp in local:
            cp.wait()

    shp = [jax.ShapeDtypeStruct((DEPTH, N_CHIPS) + HALF_SHAPE[n][1:], BF16) for n in BIG]
    sem = pltpu.SemaphoreType.DMA((NB, DEPTH, N_CHIPS))
    outs = pl.pallas_call(
        body, name="rs_pair", out_shape=shp + shp, in_specs=[ANY] * (DEPTH * NB), out_specs=[ANY] * (2 * NB),
        scratch_shapes=[sem, sem, sem], compiler_params=pltpu.CompilerParams(has_side_effects=True),
    )(*[grads[l][a] for l in range(DEPTH) for a in range(NB)])
    return outs[:NB], outs[NB:]


def _rs_chips(partials):
    def body(*refs):
        ins, outs = refs[:NB], refs[NB:2 * NB]
        ssem, rsem, loc = refs[2 * NB:]
        x, y, c, chip, chips, chip_ids = _place()
        local, sends = [], []
        for l in reversed(range(DEPTH)):
            for a in range(NB):
                for j in range(3):
                    cp = _remote(ins[a].at[l, chip_ids[j]], outs[a].at[l, chip], ssem.at[a, l, j], rsem.at[a, l, j], (*chips[j], c))
                    cp.start()
                    sends.append(cp)
                cp = pltpu.make_async_copy(ins[a].at[l, chip], outs[a].at[l, chip], loc.at[a, l])
                cp.start()
                local.append(cp)
        for l in range(DEPTH):
            for a in range(NB):
                for j in range(3):
                    dst = outs[a].at[l, chip_ids[j]]
                    _remote(dst, dst, ssem.at[a, l, j], rsem.at[a, l, j], (*chips[j], c)).wait_recv()
        for cp in sends:
            cp.wait_send()
        for cp in local:
            cp.wait()

    shp = [jax.ShapeDtypeStruct((DEPTH, N_CHIPS) + HALF_SHAPE[n][1:], BF16) for n in BIG]
    sem = pltpu.SemaphoreType.DMA((NB, DEPTH, 3))
    return pl.pallas_call(
        body, name="rs_chips", out_shape=shp, in_specs=[ANY] * NB, out_specs=[ANY] * NB,
        scratch_shapes=[sem, sem, pltpu.SemaphoreType.DMA((NB, DEPTH))], compiler_params=pltpu.CompilerParams(has_side_effects=True),
    )(*partials)


def _rs_share(halves):
    def body(*refs):
        ins, outs = refs[:NB], refs[NB:2 * NB]
        ssem, rsem, loc = refs[2 * NB:]
        x, y, c, _, _, _ = _place()
        sib = (x, y, 1 - c)
        local, sends = [], []
        for a in range(NB):
            for l in range(DEPTH):
                cp = _remote(ins[a].at[l], outs[a].at[l, c], ssem.at[a, l], rsem.at[a, l], sib)
                cp.start()
                sends.append(cp)
                cp = pltpu.make_async_copy(ins[a].at[l], outs[a].at[l, c], loc.at[a, l])
                cp.start()
                local.append(cp)
        for a in range(NB):
            for l in range(DEPTH):
                dst = outs[a].at[l, 1 - c]
                _remote(dst, dst, ssem.at[a, l], rsem.at[a, l], sib).wait_recv()
        for cp in sends:
            cp.wait_send()
        for cp in local:
            cp.wait()

    shp = [jax.ShapeDtypeStruct((DEPTH,) + HALF_SHAPE[n], F32) for n in BIG]
    sem = pltpu.SemaphoreType.DMA((NB, DEPTH))
    return pl.pallas_call(
        body, name="rs_share", out_shape=shp, in_specs=[ANY] * NB, out_specs=[ANY] * NB,
        scratch_shapes=[sem, sem, sem], compiler_params=pltpu.CompilerParams(has_side_effects=True),
    )(*halves)


def _all_reduce_small(buf, name):
    R = buf.shape[0]
    offs = [(dx, dy, dc) for dx in (0, 1) for dy in (0, 1) for dc in (0, 1)][1:]

    def body(in_ref, out_ref, gather_ref, ssem, rsem):
        x, y, c = lax.axis_index("x"), lax.axis_index("y"), lax.axis_index("c")
        me = 4 * x + 2 * y + c
        gather_ref[me] = in_ref[...]
        flip = lambda v, d: 1 - v if d else v
        peers = [(flip(x, dx), flip(y, dy), flip(c, dc)) for dx, dy, dc in offs]
        cps = [_remote(in_ref, gather_ref.at[me], ssem.at[k], rsem.at[k], peers[k]) for k in range(N_DEV - 1)]
        for cp in cps:
            cp.start()
        for k, (px, py, pc) in enumerate(peers):
            _remote(in_ref, gather_ref.at[4 * px + 2 * py + pc], ssem.at[k], rsem.at[k], peers[k]).wait_recv()
        acc = gather_ref[0]
        for s in range(1, N_DEV):
            acc = acc + gather_ref[s]
        out_ref[...] = acc
        for cp in cps:
            cp.wait_send()

    vm = pl.BlockSpec(memory_space=pltpu.VMEM)
    return pl.pallas_call(
        body, name=name, out_shape=jax.ShapeDtypeStruct((R, 128), F32), in_specs=[vm], out_specs=vm,
        scratch_shapes=[pltpu.VMEM((N_DEV, R, 128), F32), pltpu.SemaphoreType.DMA((N_DEV - 1,)), pltpu.SemaphoreType.DMA((N_DEV - 1,))],
        compiler_params=pltpu.CompilerParams(vmem_limit_bytes=VMEM_LIMIT),
    )(buf)


def _row_tile(rows, cols, n_arrays):
    budget = 20 * 1024 * 1024 // (n_arrays * 2 * cols * 4)
    tr = rows
    while tr > budget or tr % 16:
        assert tr % 2 == 0, (rows, cols)
        tr //= 2
    return tr


def _add_pairs(a, b, name):
    shape = a.shape
    a2, b2 = a.reshape(-1, shape[-1]), b.reshape(-1, shape[-1])
    rows, cols = a2.shape
    tr = _row_tile(rows, cols, 3)

    def body(a_ref, b_ref, o_ref):
        o_ref[...] = (a_ref[...].astype(F32) + b_ref[...].astype(F32)).astype(BF16)

    spec = pl.BlockSpec((tr, cols), lambda i: (i, 0))
    return pl.pallas_call(body, name=name, grid=(rows // tr,), in_specs=[spec, spec], out_specs=spec,
                          out_shape=jax.ShapeDtypeStruct((rows, cols), BF16), compiler_params=_params("parallel"))(a2, b2).reshape(shape)


def _sum_chips(p, name):
    _, _, rows, cols = p.shape
    tr = _row_tile(rows, cols, 4)

    def body(p_ref, o_ref):
        acc = p_ref[0].astype(F32)
        for q in range(1, N_CHIPS):
            acc = acc + p_ref[q].astype(F32)
        o_ref[...] = acc

    return pl.pallas_call(
        body, name=name, grid=(DEPTH, rows // tr),
        in_specs=[pl.BlockSpec((None, N_CHIPS, tr, cols), lambda l, i: (l, 0, i, 0))],
        out_specs=pl.BlockSpec((None, tr, cols), lambda l, i: (l, i, 0)),
        out_shape=jax.ShapeDtypeStruct((DEPTH, rows, cols), F32), compiler_params=_params("parallel", "parallel"))(p)


def _adamw(w, g, m, v, name):
    shape = w.shape
    cols = shape[-1]
    w2, g2, m2, v2 = (t.reshape(-1, cols) for t in (w, g, m, v))
    rows = w2.shape[0]
    tr = _row_tile(rows, cols, 7)

    def body(w_ref, g_ref, m_ref, v_ref, d_ref, mo_ref, vo_ref):
        gv = g_ref[...]
        mn = ADAM_B1 * m_ref[...] + (1.0 - ADAM_B1) * gv
        vn = ADAM_B2 * v_ref[...] + (1.0 - ADAM_B2) * (gv * gv)
        m_hat = mn / (1.0 - ADAM_B1 ** ADAM_STEP)
        v_hat = vn / (1.0 - ADAM_B2 ** ADAM_STEP)
        d_ref[...] = -ADAM_LR * (m_hat / (jnp.sqrt(v_hat) + ADAM_EPS) + ADAM_WD * w_ref[...])
        mo_ref[...] = mn
        vo_ref[...] = vn

    spec = pl.BlockSpec((tr, cols), lambda i: (i, 0))
    outs = pl.pallas_call(body, name=name, grid=(rows // tr,), in_specs=[spec] * 4, out_specs=[spec] * 3,
                          out_shape=[jax.ShapeDtypeStruct((rows, cols), F32)] * 3, compiler_params=_params("parallel"))(w2, g2, m2, v2)
    return [o.reshape(shape) for o in outs]


def _pack(arrays):
    parts = []
    for t in arrays:
        f = t.reshape(-1)
        parts.append(jnp.pad(f, (0, (-f.shape[0]) % 128)))
    f = jnp.concatenate(parts)
    f = jnp.pad(f, (0, (-f.shape[0]) % 2048))
    return f.reshape(-1, 128)


def _unpack(buf, shapes):
    f = buf.reshape(-1)
    out, off = [], 0
    for s in shapes:
        n = math.prod(s)
        out.append(f[off:off + n].reshape(s))
        off += n + (-n) % 128
    return out


SMALL = ("norm_mix", "sg_ln_g", "sg_ln_b", "sg_w", "sg_b", "cv_w", "cv_b", "cv_ln_g", "cv_ln_b", "attn_sinks", "sc_w", "norm_ffn", "norm_final")
ORDER = ("norm_mix", "w_in", "sg_ln_g", "sg_ln_b", "sg_w", "sg_b", "cv_w", "cv_b", "cv_ln_g", "cv_ln_b", "attn_sinks", "sc_w",
         "w_branch", "w_out", "norm_ffn", "w_gate_up", "w_down", "norm_final")


def kernel(x, norm_mix, w_in, sg_ln_g, sg_ln_b, sg_w, sg_b, cv_w, cv_b, cv_ln_g, cv_ln_b, attn_sinks, sc_w, w_branch, w_out, norm_ffn, w_gate_up, w_down, norm_final, loss_target, m_norm_mix, m_w_in, m_sg_ln_g, m_sg_ln_b, m_sg_w, m_sg_b, m_cv_w, m_cv_b, m_cv_ln_g, m_cv_ln_b, m_attn_sinks, m_sc_w, m_w_branch, m_w_out, m_norm_ffn, m_w_gate_up, m_w_down, m_norm_final, v_norm_mix, v_w_in, v_sg_ln_g, v_sg_ln_b, v_sg_w, v_sg_b, v_cv_w, v_cv_b, v_cv_ln_g, v_cv_ln_b, v_attn_sinks, v_sc_w, v_w_branch, v_w_out, v_norm_ffn, v_w_gate_up, v_w_down, v_norm_final):
    W = dict(norm_mix=norm_mix, w_in=w_in, sg_ln_g=sg_ln_g, sg_ln_b=sg_ln_b, sg_w=sg_w, sg_b=sg_b, cv_w=cv_w, cv_b=cv_b, cv_ln_g=cv_ln_g,
             cv_ln_b=cv_ln_b, attn_sinks=attn_sinks, sc_w=sc_w, w_branch=w_branch, w_out=w_out, norm_ffn=norm_ffn, w_gate_up=w_gate_up,
             w_down=w_down, norm_final=norm_final)
    M = dict(norm_mix=m_norm_mix, w_in=m_w_in, sg_ln_g=m_sg_ln_g, sg_ln_b=m_sg_ln_b, sg_w=m_sg_w, sg_b=m_sg_b, cv_w=m_cv_w, cv_b=m_cv_b,
             cv_ln_g=m_cv_ln_g, cv_ln_b=m_cv_ln_b, attn_sinks=m_attn_sinks, sc_w=m_sc_w, w_branch=m_w_branch, w_out=m_w_out,
             norm_ffn=m_norm_ffn, w_gate_up=m_w_gate_up, w_down=m_w_down, norm_final=m_norm_final)
    V = dict(norm_mix=v_norm_mix, w_in=v_w_in, sg_ln_g=v_sg_ln_g, sg_ln_b=v_sg_ln_b, sg_w=v_sg_w, sg_b=v_sg_b, cv_w=v_cv_w, cv_b=v_cv_b,
             cv_ln_g=v_cv_ln_g, cv_ln_b=v_cv_ln_b, attn_sinks=v_attn_sinks, sc_w=v_sc_w, w_branch=v_w_branch, w_out=v_w_out,
             norm_ffn=v_norm_ffn, w_gate_up=v_w_gate_up, w_down=v_w_down, norm_final=v_norm_final)
    mx, my, mc = lax.axis_index("x"), lax.axis_index("y"), lax.axis_index("c")
    chip = 2 * mx + my

    gathered = _ag_weights([W[n].astype(BF16).reshape((DEPTH,) + HALF_SHAPE[n]) for n in BIG])
    G = dict(zip(BIG, gathered))
    south = (mc == 0).astype(F32)
    cvw_z = lax.dynamic_update_slice(jnp.zeros((DEPTH, CV_KERNEL, HALF), F32), cv_w * south, (0, 0, chip * 128))
    scw_z = lax.dynamic_update_slice(jnp.zeros((DEPTH, SC_KERNEL, HALF), F32), sc_w * south, (0, 0, chip * 128))
    cvw_full, scw_full = _unpack(_all_reduce_small(_pack([cvw_z, scw_z]), "ag_small"), [cvw_z.shape, scw_z.shape])

    mixer_params = [_mixer_params(l, sg_ln_g, sg_ln_b, sg_w, sg_b, cvw_full, cv_b, cv_ln_g, cv_ln_b, attn_sinks, scw_full) for l in range(DEPTH)]
    w_in_p = [_w_in_layout(G["w_in"][l].reshape(N_CHIPS, D_MODEL, W_IN_SHARD)) for l in range(DEPTH)]
    wb_g = [G["w_branch"][l].reshape(N_CHIPS, N_BRANCH, HALF, 256) for l in range(DEPTH)]
    wo_g = [G["w_out"][l].reshape(D_MODEL, D_MODEL) for l in range(DEPTH)]
    wgu_g = [G["w_gate_up"][l].reshape(N_CHIPS, D_MODEL, GU_SHARD) for l in range(DEPTH)]
    wd_g = [G["w_down"][l].reshape(D_FF, D_MODEL) for l in range(DEPTH)]

    loss, dx, d_nfinal, grads = _device_step(x[0], loss_target[0], norm_mix, norm_ffn, norm_final, mixer_params, w_in_p, wb_g, wo_g, wgu_g, wd_g)

    small_g = {n: jnp.stack([grads[l][n] for l in range(DEPTH)]) for n in SMALL if n != "norm_final"}
    small_g["norm_final"] = d_nfinal
    red = _unpack(_all_reduce_small(_pack([small_g[n] for n in SMALL] + [loss]), "ar_small"),
                  [small_g[n].shape for n in SMALL] + [loss.shape])
    small_red = dict(zip(SMALL, red[:-1]))
    loss_out = red[-1][0, 0]
    for n in ("cv_w", "sc_w"):
        small_red[n] = lax.dynamic_slice_in_dim(small_red[n], chip * 128, 128, axis=2)

    per_layer = []
    for l in range(DEPTH):
        g = grads[l]
        per_layer.append([_w_in_unlayout(g["w_in"]).reshape((N_CHIPS,) + HALF_SHAPE["w_in"]),
                          g["w_branch"].reshape((N_CHIPS,) + HALF_SHAPE["w_branch"]),
                          g["w_out"].reshape((N_CHIPS,) + HALF_SHAPE["w_out"]),
                          g["w_gate_up"].reshape((N_CHIPS,) + HALF_SHAPE["w_gate_up"]),
                          g["w_down"].reshape((N_CHIPS,) + HALF_SHAPE["w_down"])])
    own, got = _rs_pair(per_layer)
    partial = [_add_pairs(own[a], got[a], f"rs_add_{BIG[a]}") for a in range(NB)]
    from_chips = _rs_chips(partial)
    halves = [_sum_chips(from_chips[a], f"rs_sum_{BIG[a]}") for a in range(NB)]
    shared = _rs_share(halves)
    big_red = {n: shared[a].reshape(W[n].shape) for a, n in enumerate(BIG)}

    upd = {}
    for n in BIG:
        upd[n] = [big_red[n]] + _adamw(W[n], big_red[n], M[n], V[n], f"adamw_{n}")
    sw, sg, sm, sv = (_pack([t[n] for n in SMALL]) for t in (W, small_red, M, V))
    packed = _adamw(sw, sg, sm, sv, "adamw_small")
    shapes = [W[n].shape for n in SMALL]
    for n, d, mo, vo in zip(SMALL, *(_unpack(p, shapes) for p in packed)):
        upd[n] = [small_red[n], d, mo, vo]

    out = [loss_out, dx[None]]
    for k in range(4):
        out += [upd[n][k] for n in ORDER]
    return tuple(out)
```

```python
import functools
import math

import jax
import jax.numpy as jnp
from jax import lax
from jax.experimental import pallas as pl
from jax.experimental.pallas import tpu as pltpu

F32 = jnp.float32
BF16 = jnp.bfloat16

D_MODEL = 1024
DEPTH = 2
HALF = 512
SG_CHUNK = 128
SG_GROUPS = 4
CV_KERNEL = 31
HEAD_DIM = 64
N_Q_HEADS = 8
N_KV_HEADS = 2
Q_PER_KV = N_Q_HEADS // N_KV_HEADS
WINDOW = 128
ROPE_THETA = 10000.0
SC_KERNEL = 3
N_BRANCH = 4
D_FF = 2816
EPS = 1e-6
N_CHIPS = 4
N_DEV = 8

MIX_W = 4352
GATE_W = N_BRANCH * D_MODEL
PROJ_PAD = 2 * MIX_W
W_IN_SHARD = 2112
GU_SHARD = 1408
HALO = 128
CV_PAD = 32

ADAM_LR = 0.001
ADAM_B1 = 0.9
ADAM_B2 = 0.999
ADAM_EPS = 1e-08
ADAM_WD = 0.01
ADAM_STEP = 10

VMEM_LIMIT = 56 * 1024 * 1024
INV_SQRT2 = 1.0 / math.sqrt(2.0)
INV_SQRT_2PI = 1.0 / math.sqrt(2.0 * math.pi)
NEG_BIG = -1e30
MESH = pl.DeviceIdType.MESH

C_ZA, C_ZB, C_Q, C_K, C_V, C_ZD = 0, 1024, 2048, 2560, 2688, 2816


def _params(*sem):
    return pltpu.CompilerParams(dimension_semantics=sem, vmem_limit_bytes=VMEM_LIMIT)


def _sig(v):
    return 1.0 / (1.0 + jnp.exp(-v))


def _dot(a, b):
    return jnp.dot(a, b, preferred_element_type=F32)


def _dot_nt(a, b):
    return lax.dot_general(a, b, (((1,), (1,)), ((), ())), preferred_element_type=F32)


def _dot_tn(a, b):
    return lax.dot_general(a, b, (((0,), (0,)), ((), ())), preferred_element_type=F32)


def _full(shape):
    nd = len(shape)
    return pl.BlockSpec(shape, lambda *_: (0,) * nd)


def _rms_mm(x, g, w, tm, tn, name):
    T = x.shape[0]
    if w.ndim == 3:
        tn = w.shape[2]
        N = w.shape[0] * tn
        wspec = pl.BlockSpec((None, D_MODEL, tn), lambda i, j: (j, 0, 0))
    else:
        N = w.shape[1]
        wspec = pl.BlockSpec((D_MODEL, tn), lambda i, j: (0, j))

    def body(x_ref, g_ref, w_ref, o_ref, xn_ref):
        @pl.when(pl.program_id(1) == 0)
        def _():
            xv = x_ref[...]
            r = lax.rsqrt(jnp.mean(xv * xv, axis=-1, keepdims=True) + EPS)
            xn_ref[...] = (xv * r * g_ref[...]).astype(BF16)

        o_ref[...] = _dot(xn_ref[...], w_ref[...]).astype(BF16)

    return pl.pallas_call(
        body, name=name, grid=(T // tm, N // tn),
        in_specs=[pl.BlockSpec((tm, D_MODEL), lambda i, j: (i, 0)), _full((1, D_MODEL)), wspec],
        out_specs=[pl.BlockSpec((tm, tn), lambda i, j: (i, j)), pl.BlockSpec((tm, D_MODEL), lambda i, j: (i, 0))],
        out_shape=[jax.ShapeDtypeStruct((T, N), BF16), jax.ShapeDtypeStruct((T, D_MODEL), BF16)],
        compiler_params=_params("parallel", "arbitrary"),
    )(x, g, w)


def _merge_fwd(x, ys, proj, wb, wo, tm, name):
    T = x.shape[0]

    def body(x_ref, ys_ref, zg_ref, wb_ref, wo_ref, xo_ref, mg_ref):
        merged = None
        for n in range(N_BRANCH):
            yn = ys_ref[:, n * HALF:(n + 1) * HALF]
            br = jnp.concatenate([_dot(yn, wb_ref[s, n]) for s in range(N_CHIPS)], axis=1)
            t = _sig(zg_ref[:, n * D_MODEL:(n + 1) * D_MODEL].astype(F32)) * br
            merged = t if merged is None else merged + t
        mb = merged.astype(BF16)
        mg_ref[...] = mb
        xo_ref[...] = x_ref[...] + _dot(mb, wo_ref[...])

    return pl.pallas_call(
        body, name=name, grid=(T // tm,),
        in_specs=[pl.BlockSpec((tm, D_MODEL), lambda i: (i, 0)), pl.BlockSpec((tm, N_BRANCH * HALF), lambda i: (i, 0)),
                  pl.BlockSpec((tm, GATE_W), lambda i: (i, 0)), _full(wb.shape), _full(wo.shape)],
        out_specs=[pl.BlockSpec((tm, D_MODEL), lambda i: (i, 0)), pl.BlockSpec((tm, D_MODEL), lambda i: (i, 0))],
        out_shape=[jax.ShapeDtypeStruct((T, D_MODEL), F32), jax.ShapeDtypeStruct((T, D_MODEL), BF16)],
        compiler_params=_params("parallel"),
    )(x, ys, proj, wb, wo)


def _ffn_down(xm, gu, wd, tm, name):
    T = xm.shape[0]

    def body(x_ref, gu_ref, wd_ref, o_ref):
        g = gu_ref[:, :D_FF].astype(F32)
        u = gu_ref[:, D_FF:].astype(F32)
        act = (g * _sig(g) * u).astype(BF16)
        o_ref[...] = x_ref[...] + _dot(act, wd_ref[...])

    return pl.pallas_call(
        body, name=name, grid=(T // tm,),
        in_specs=[pl.BlockSpec((tm, D_MODEL), lambda i: (i, 0)), pl.BlockSpec((tm, 2 * D_FF), lambda i: (i, 0)), _full(wd.shape)],
        out_specs=pl.BlockSpec((tm, D_MODEL), lambda i: (i, 0)),
        out_shape=jax.ShapeDtypeStruct((T, D_MODEL), F32),
        compiler_params=_params("parallel"),
    )(xm, gu, wd)


def _final_loss(x, g, tgt, tm, name):
    T = x.shape[0]

    def body(x_ref, g_ref, t_ref, dx_ref, dg_ref, ls_ref):
        @pl.when(pl.program_id(0) == 0)
        def _():
            dg_ref[...] = jnp.zeros_like(dg_ref)
            ls_ref[...] = jnp.zeros_like(ls_ref)

        xv = x_ref[...]
        gv = g_ref[...]
        r = lax.rsqrt(jnp.mean(xv * xv, axis=-1, keepdims=True) + EPS)
        xh = xv * r
        diff = xh * gv - t_ref[...]
        ls_ref[...] += jnp.full(ls_ref.shape, 0.5 / D_MODEL, F32) * jnp.sum(diff * diff)
        dy = diff * (1.0 / D_MODEL)
        dxh = dy * gv
        dx_ref[...] = r * (dxh - xh * jnp.mean(dxh * xh, axis=-1, keepdims=True))
        dg_ref[...] += jnp.sum(dy * xh, axis=0, keepdims=True)

    return pl.pallas_call(
        body, name=name, grid=(T // tm,),
        in_specs=[pl.BlockSpec((tm, D_MODEL), lambda i: (i, 0)), _full((1, D_MODEL)), pl.BlockSpec((tm, D_MODEL), lambda i: (i, 0))],
        out_specs=[pl.BlockSpec((tm, D_MODEL), lambda i: (i, 0)), _full((1, D_MODEL)), _full((1, 128))],
        out_shape=[jax.ShapeDtypeStruct((T, D_MODEL), F32), jax.ShapeDtypeStruct((1, D_MODEL), F32), jax.ShapeDtypeStruct((1, 128), F32)],
        compiler_params=_params("arbitrary"),
    )(x, g, tgt)


def _swiglu_bwd(dx, gu, wd, tm, name):
    T = dx.shape[0]

    def body(dx_ref, gu_ref, wd_ref, dgu_ref, act_ref):
        dact = _dot_nt(dx_ref[...].astype(BF16), wd_ref[...])
        g = gu_ref[:, :D_FF].astype(F32)
        u = gu_ref[:, D_FF:].astype(F32)
        s = _sig(g)
        silu = g * s
        act_ref[...] = (silu * u).astype(BF16)
        dgu_ref[:, :D_FF] = (dact * u * (s + silu * (1.0 - s))).astype(BF16)
        dgu_ref[:, D_FF:] = (dact * silu).astype(BF16)

    return pl.pallas_call(
        body, name=name, grid=(T // tm,),
        in_specs=[pl.BlockSpec((tm, D_MODEL), lambda i: (i, 0)), pl.BlockSpec((tm, 2 * D_FF), lambda i: (i, 0)), _full(wd.shape)],
        out_specs=[pl.BlockSpec((tm, 2 * D_FF), lambda i: (i, 0)), pl.BlockSpec((tm, D_FF), lambda i: (i, 0))],
        out_shape=[jax.ShapeDtypeStruct((T, 2 * D_FF), BF16), jax.ShapeDtypeStruct((T, D_FF), BF16)],
        compiler_params=_params("parallel"),
    )(dx, gu, wd)


def _mm_tn(a, b, grid, a_block, a_map, b_block, b_map, o_shape, o_block, o_map, name):
    gk = grid[2]
    tm = [d for d in a_block if d is not None][-1]
    tn = [d for d in b_block if d is not None][-1]

    def body(a_ref, b_ref, o_ref, acc_ref):
        k = pl.program_id(2)
        p = _dot_tn(a_ref[...].astype(BF16), b_ref[...].astype(BF16))

        @pl.when(k == 0)
        def _():
            acc_ref[...] = p

        @pl.when(k > 0)
        def _():
            acc_ref[...] += p

        @pl.when(k == gk - 1)
        def _():
            o_ref[...] = acc_ref[...].astype(o_ref.dtype)

    return pl.pallas_call(
        body, name=name, grid=grid,
        in_specs=[pl.BlockSpec(a_block, a_map), pl.BlockSpec(b_block, b_map)],
        out_specs=pl.BlockSpec(o_block, o_map),
        out_shape=jax.ShapeDtypeStruct(o_shape, BF16),
        scratch_shapes=[pltpu.VMEM((tm, tn), F32)],
        compiler_params=_params("parallel", "parallel", "arbitrary"),
    )(a, b)


def _mm_nt_rmsbwd(a, w, x, g, dres, tm, tk, name):
    T = x.shape[0]
    if w.ndim == 3:
        tk = w.shape[2]
        gk = w.shape[0]
        wspec = pl.BlockSpec((None, D_MODEL, tk), lambda i, k: (k, 0, 0))
    else:
        gk = w.shape[1] // tk
        wspec = pl.BlockSpec((D_MODEL, tk), lambda i, k: (0, k))

    def body(a_ref, w_ref, x_ref, g_ref, r_ref, dx_ref, dg_ref, acc_ref):
        i, k = pl.program_id(0), pl.program_id(1)
        p = _dot_nt(a_ref[...], w_ref[...])

        @pl.when(k == 0)
        def _():
            acc_ref[...] = p

        @pl.when(k > 0)
        def _():
            acc_ref[...] += p

        @pl.when(jnp.logical_and(i == 0, k == 0))
        def _():
            dg_ref[...] = jnp.zeros_like(dg_ref)

        @pl.when(k == gk - 1)
        def _():
            dh = acc_ref[...]
            xv = x_ref[...]
            r = lax.rsqrt(jnp.mean(xv * xv, axis=-1, keepdims=True) + EPS)
            xh = xv * r
            dxh = dh * g_ref[...]
            dx_ref[...] = r_ref[...] + r * (dxh - xh * jnp.mean(dxh * xh, axis=-1, keepdims=True))
            dg_ref[...] += jnp.sum(dh * xh, axis=0, keepdims=True)

    return pl.pallas_call(
        body, name=name, grid=(T // tm, gk),
        in_specs=[pl.BlockSpec((tm, tk), lambda i, k: (i, k)), wspec, pl.BlockSpec((tm, D_MODEL), lambda i, k: (i, 0)),
                  _full((1, D_MODEL)), pl.BlockSpec((tm, D_MODEL), lambda i, k: (i, 0))],
        out_specs=[pl.BlockSpec((tm, D_MODEL), lambda i, k: (i, 0)), _full((1, D_MODEL))],
        out_shape=[jax.ShapeDtypeStruct((T, D_MODEL), F32), jax.ShapeDtypeStruct((1, D_MODEL), F32)],
        scratch_shapes=[pltpu.VMEM((tm, D_MODEL), F32)],
        compiler_params=_params("arbitrary", "arbitrary"),
    )(a, w, x, g, dres)


def _merge_bwd(dxm, ys, proj, wb, wo, tm, name):
    T = dxm.shape[0]

    def body(dx_ref, ys_ref, zg_ref, wb_ref, wo_ref, dys_ref, dbr_ref, dp_ref):
        dmerged = _dot_nt(dx_ref[...].astype(BF16), wo_ref[...])
        for n in range(N_BRANCH):
            yn = ys_ref[:, n * HALF:(n + 1) * HALF]
            br = jnp.concatenate([_dot(yn, wb_ref[s, n]) for s in range(N_CHIPS)], axis=1)
            gt = _sig(zg_ref[:, n * D_MODEL:(n + 1) * D_MODEL].astype(F32))
            dbr = (gt * dmerged).astype(BF16)
            dbr_ref[:, n * D_MODEL:(n + 1) * D_MODEL] = dbr
            dp_ref[:, n * D_MODEL:(n + 1) * D_MODEL] = (dmerged * br * gt * (1.0 - gt)).astype(BF16)
            dy = None
            for s in range(N_CHIPS):
                t = _dot_nt(dbr[:, s * 256:(s + 1) * 256], wb_ref[s, n])
                dy = t if dy is None else dy + t
            dys_ref[:, n * HALF:(n + 1) * HALF] = dy.astype(BF16)
        dp_ref[:, GATE_W:] = jnp.zeros((tm, MIX_W - GATE_W), BF16)

    return pl.pallas_call(
        body, name=name, grid=(T // tm,),
        in_specs=[pl.BlockSpec((tm, D_MODEL), lambda i: (i, 0)), pl.BlockSpec((tm, N_BRANCH * HALF), lambda i: (i, 0)),
                  pl.BlockSpec((tm, GATE_W), lambda i: (i, 0)), _full(wb.shape), _full(wo.shape)],
        out_specs=[pl.BlockSpec((tm, N_BRANCH * HALF), lambda i: (i, 0)), pl.BlockSpec((tm, GATE_W), lambda i: (i, 0)),
                   pl.BlockSpec((tm, MIX_W), lambda i: (i, 0))],
        out_shape=[jax.ShapeDtypeStruct((T, N_BRANCH * HALF), BF16), jax.ShapeDtypeStruct((T, GATE_W), BF16),
                   jax.ShapeDtypeStruct((T, PROJ_PAD), BF16)],
        compiler_params=_params("parallel"),
    )(dxm, ys, proj, wb, wo)


def _gelu(v):
    return 0.5 * v * (1.0 + lax.erf(v * INV_SQRT2))


def _gelu_grad(v):
    return 0.5 * (1.0 + lax.erf(v * INV_SQRT2)) + v * jnp.exp(-0.5 * v * v) * INV_SQRT_2PI


def _rot_half(t):
    w = t.shape[1]
    lane = lax.broadcasted_iota(jnp.int32, t.shape, 1)
    return jnp.where((lane % HEAD_DIM) < HEAD_DIM // 2, pltpu.roll(t, w - HEAD_DIM // 2, 1), pltpu.roll(t, HEAD_DIM // 2, 1))


def _rope(t, cos, sin_signed):
    return t * cos + _rot_half(t) * sin_signed


def _rope_t(d, cos, sin_signed):
    return d * cos + _rot_half(d * sin_signed)


def _ln_fwd(v, g, b):
    mu = jnp.mean(v, axis=-1, keepdims=True)
    vc = v - mu
    r = lax.rsqrt(jnp.mean(vc * vc, axis=-1, keepdims=True) + EPS)
    vh = vc * r
    return vh * g + b, vh, r


def _ln_bwd(dn, vh, r, g):
    dvh = dn * g
    return r * (dvh - jnp.mean(dvh, axis=-1, keepdims=True) - vh * jnp.mean(dvh * vh, axis=-1, keepdims=True))


def _sublane_shifts(sh_ref, rows):
    for b in range(1, 8):
        sh_ref[b, 0:rows - 8, :] = sh_ref[0, pl.ds(b, rows - 8), :]


def _tap(sh_ref, off, n):
    return sh_ref[off % 8, pl.ds(off - off % 8, n), :]


def _tril_mask():
    return lax.broadcasted_iota(jnp.int32, (SG_CHUNK, SG_CHUNK), 0) >= lax.broadcasted_iota(jnp.int32, (SG_CHUNK, SG_CHUNK), 1)


def _attn_probs(qs, kh, sink_col, first_ok):
    s = _dot_nt(qs, kh) * (HEAD_DIM ** -0.5)
    row = lax.broadcasted_iota(jnp.int32, s.shape, 0) % WINDOW
    col = lax.broadcasted_iota(jnp.int32, s.shape, 1)
    valid = (col > row) & (col <= row + WINDOW) & ((col >= WINDOW) | first_ok)
    s = jnp.where(valid, s, NEG_BIG)
    m = jnp.maximum(jnp.max(s, axis=-1, keepdims=True), sink_col)
    p = jnp.where(valid, jnp.exp(s - m), 0.0)
    es = jnp.exp(sink_col - m)
    inv = 1.0 / (jnp.sum(p, axis=-1, keepdims=True) + es)
    return p * inv, es * inv


def _sink_col(sinks_ref, h):
    return jnp.concatenate([jnp.broadcast_to(sinks_ref[:, h * Q_PER_KV + g:h * Q_PER_KV + g + 1], (WINDOW, 1))
                            for g in range(Q_PER_KV)], axis=0)


def _mixer_in_specs(TB, nb):
    r = TB // HALO
    last = nb * r - 1
    cur = pl.BlockSpec((TB, MIX_W), lambda i: (i, 1))
    prev = pl.BlockSpec((HALO, MIX_W), lambda i: (jnp.maximum(i * r - 1, 0), 1))
    nxt = pl.BlockSpec((HALO, MIX_W), lambda i: (jnp.minimum((i + 1) * r, last), 1))
    tcur = pl.BlockSpec((TB, 128), lambda i: (i, 0))
    tprev = pl.BlockSpec((HALO, 128), lambda i: (jnp.maximum(i * r - 1, 0), 0))
    tnxt = pl.BlockSpec((HALO, 128), lambda i: (jnp.minimum((i + 1) * r, last), 0))
    return cur, prev, nxt, tcur, tprev, tnxt


def _mixer_param_specs():
    return [_full((1, HALF)), _full((1, HALF)), _full((SG_GROUPS, SG_CHUNK, SG_CHUNK)), _full((SG_CHUNK, 128)),
            _full((32, HALF)), _full((1, HALF)), _full((1, HALF)), _full((1, HALF)), _full((1, 128)), _full((8, HALF))]


def _mixers_fwd(proj, cos_t, sin_t, mp, TB, name):
    T = proj.shape[0]
    nb = T // TB
    r = TB // HALO
    cur, prev, _, tcur, tprev, _ = _mixer_in_specs(TB, nb)

    def body(zc_ref, zp_ref, cc_ref, sc_ref, cp_ref, sp_ref,
             lg_ref, lb_ref, sgw_ref, sgb_ref, cvw_ref, cvb_ref, cvg_ref, cvbb_ref, sinks_ref, scw_ref,
             ys_ref, scr_ref, k_ref, v_ref, sh_ref):
        i = pl.program_id(0)
        pm = (i > 0).astype(F32)

        def colsE(c0, c1):
            return jnp.concatenate([zp_ref[:, c0:c1].astype(F32) * pm, zc_ref[:, c0:c1].astype(F32)], axis=0)

        a = _gelu(zc_ref[:, C_ZA:C_ZA + 2 * HALF].astype(F32))
        u = a[:, :HALF]
        vn, _, _ = _ln_fwd(a[:, HALF:], lg_ref[...], lb_ref[...])
        vnb = vn.astype(BF16)
        tril = _tril_mask()
        for g in range(SG_GROUPS):
            wt = jnp.where(tril, sgw_ref[g], 0.0).astype(BF16)
            for ci in range(r):
                rows = slice(ci * SG_CHUNK, (ci + 1) * SG_CHUNK)
                cols = slice(g * 128, (g + 1) * 128)
                mixed = _dot(wt, vnb[rows, cols]) + sgb_ref[:, g:g + 1]
                ys_ref[rows, g * 128:(g + 1) * 128] = (u[rows, cols] * mixed).astype(BF16)

        def colsB(c0, c1):
            return jnp.concatenate([zp_ref[HALO - CV_PAD:, c0:c1].astype(F32) * pm, zc_ref[:, c0:c1].astype(F32)], axis=0)

        sh_ref[0] = colsB(C_ZB, C_ZB + HALF) * _sig(colsB(C_ZB + HALF, C_ZB + 2 * HALF))
        _sublane_shifts(sh_ref, TB + CV_PAD)
        c = jnp.broadcast_to(cvb_ref[...], (TB, HALF))
        for k in range(CV_KERNEL):
            c = c + cvw_ref[k:k + 1, :] * _tap(sh_ref, CV_PAD - (CV_KERNEL - 1) + k, TB)
        n, _, _ = _ln_fwd(c, cvg_ref[...], cvbb_ref[...])
        ys_ref[:, HALF:2 * HALF] = (n * _sig(n)).astype(BF16)

        zd = colsE(C_ZD + HALF, C_ZD + 3 * HALF)
        scr_ref[...] = zd[:, :HALF] * zd[:, HALF:]
        cv = None
        for k in range(SC_KERNEL):
            t = scw_ref[k:k + 1, :] * scr_ref[pl.ds(HALO - (SC_KERNEL - 1) + k, TB), :]
            cv = t if cv is None else cv + t
        ys_ref[:, 3 * HALF:4 * HALF] = (zc_ref[:, C_ZD:C_ZD + HALF].astype(F32) * cv).astype(BF16)

        cosE = jnp.concatenate([cp_ref[...], cc_ref[...]], axis=0)
        sinE = jnp.concatenate([sp_ref[...], sc_ref[...]], axis=0)
        k_ref[...] = _rope(colsE(C_K, C_K + 128), cosE, sinE).astype(BF16)
        v_ref[...] = colsE(C_V, C_V + 128).astype(BF16)
        cosC, sinC = cc_ref[...], sc_ref[...]
        q = jnp.concatenate([_rope(zc_ref[:, C_Q + 128 * j:C_Q + 128 * (j + 1)].astype(F32), cosC, sinC)
                             for j in range(4)], axis=1).astype(BF16)
        for qb in range(r):
            first_ok = (i * r + qb) > 0
            for h in range(N_KV_HEADS):
                hc = slice(h * HEAD_DIM, (h + 1) * HEAD_DIM)
                kh = k_ref[qb * WINDOW:qb * WINDOW + 2 * WINDOW, hc]
                vh = v_ref[qb * WINDOW:qb * WINDOW + 2 * WINDOW, hc]
                qs = jnp.concatenate([q[qb * WINDOW:(qb + 1) * WINDOW, (h * Q_PER_KV + g) * HEAD_DIM:(h * Q_PER_KV + g + 1) * HEAD_DIM]
                                      for g in range(Q_PER_KV)], axis=0)
                probs, _ = _attn_probs(qs, kh, _sink_col(sinks_ref, h), first_ok)
                o = _dot(probs.astype(BF16), vh)
                for g in range(Q_PER_KV):
                    c0 = 2 * HALF + (h * Q_PER_KV + g) * HEAD_DIM
                    ys_ref[qb * WINDOW:(qb + 1) * WINDOW, c0:c0 + HEAD_DIM] = o[g * WINDOW:(g + 1) * WINDOW].astype(BF16)

    return pl.pallas_call(
        body, name=name, grid=(nb,),
        in_specs=[cur, prev, tcur, tcur, tprev, tprev] + _mixer_param_specs(),
        out_specs=pl.BlockSpec((TB, 4 * HALF), lambda i: (i, 0)),
        out_shape=jax.ShapeDtypeStruct((T, 4 * HALF), BF16),
        scratch_shapes=[pltpu.VMEM((TB + HALO, HALF), F32), pltpu.VMEM((TB + HALO, 128), BF16), pltpu.VMEM((TB + HALO, 128), BF16),
                        pltpu.VMEM((8, TB + CV_PAD, HALF), F32)],
        compiler_params=_params("parallel"),
    )(proj, proj, cos_t, sin_t, cos_t, sin_t, *mp)


def _mixers_bwd(proj, dys, dproj, cos_t, sin_t, mp, TB, name):
    T = proj.shape[0]
    nb = T // TB
    r = TB // HALO
    RE = TB + 2 * HALO
    RC = TB + HALO
    cur, prev, nxt, tcur, tprev, tnxt = _mixer_in_specs(TB, nb)
    dcur = pl.BlockSpec((TB, 4 * HALF), lambda i: (i, 0))
    dnxt = pl.BlockSpec((HALO, 4 * HALF), lambda i: (jnp.minimum((i + 1) * r, nb * r - 1), 0))

    def body(zc_ref, zp_ref, zn_ref, dyc_ref, dyn_ref, cc_ref, sc_ref, cp_ref, sp_ref, cn_ref, sn_ref,
             lg_ref, lb_ref, sgw_ref, sgb_ref, cvw_ref, cvb_ref, cvg_ref, cvbb_ref, sinks_ref, scw_ref, dp_in_ref,
             dz_ref, dlg_ref, dlb_ref, dsgw_ref, dsgb_ref, dcvw_ref, dcvb_ref, dcvg_ref, dcvbb_ref, dsink_ref, dscw_ref,
             scr_ref, scr2_ref, k_ref, v_ref, dk_ref, dv_ref, dq_ref, sh_ref, sh2_ref):
        del dp_in_ref
        i = pl.program_id(0)
        pm = (i > 0).astype(F32)
        nm = (i < nb - 1).astype(F32)

        @pl.when(i == 0)
        def _():
            for ref in (dlg_ref, dlb_ref, dsgw_ref, dsgb_ref, dcvw_ref, dcvb_ref, dcvg_ref, dcvbb_ref, dsink_ref, dscw_ref):
                ref[...] = jnp.zeros_like(ref)

        def colsE(c0, c1):
            return jnp.concatenate([zp_ref[:, c0:c1].astype(F32) * pm, zc_ref[:, c0:c1].astype(F32),
                                    zn_ref[:, c0:c1].astype(F32)], axis=0)

        def colsC(c0, c1):
            return jnp.concatenate([zc_ref[:, c0:c1].astype(F32), zn_ref[:, c0:c1].astype(F32)], axis=0)

        def dyC(c0, c1):
            return jnp.concatenate([dyc_ref[:, c0:c1].astype(F32), dyn_ref[:, c0:c1].astype(F32) * nm], axis=0)

        za = zc_ref[:, C_ZA:C_ZA + 2 * HALF].astype(F32)
        a = _gelu(za)
        u = a[:, :HALF]
        lg = lg_ref[...]
        vn, vh, rs = _ln_fwd(a[:, HALF:], lg, lb_ref[...])
        vnb = vn.astype(BF16)
        dya = dyc_ref[:, 0:HALF].astype(F32)
        tril = _tril_mask()
        lane128 = lax.broadcasted_iota(jnp.int32, (SG_CHUNK, 128), 1)
        du_parts, dvn_parts = [], []
        for ci in range(r):
            rows = slice(ci * SG_CHUNK, (ci + 1) * SG_CHUNK)
            du_g, dvn_g = [], []
            for g in range(SG_GROUPS):
                cols = slice(g * 128, (g + 1) * 128)
                wt = jnp.where(tril, sgw_ref[g], 0.0).astype(BF16)
                vb = vnb[rows, cols]
                mixed = _dot(wt, vb) + sgb_ref[:, g:g + 1]
                dy_blk = dya[rows, cols]
                du_g.append(dy_blk * mixed)
                dmix = dy_blk * u[rows, cols]
                dmb = dmix.astype(BF16)
                dvn_g.append(_dot_tn(wt, dmb))
                dsgw_ref[g] += jnp.where(tril, _dot_nt(dmb, vb), 0.0)
                dsgb_ref[...] += jnp.where(lane128 == g, jnp.sum(dmix, axis=1, keepdims=True), 0.0)
            du_parts.append(jnp.concatenate(du_g, axis=1))
            dvn_parts.append(jnp.concatenate(dvn_g, axis=1))
        du = jnp.concatenate(du_parts, axis=0) if r > 1 else du_parts[0]
        dvn = jnp.concatenate(dvn_parts, axis=0) if r > 1 else dvn_parts[0]
        dlg_ref[...] += jnp.sum(dvn * vh, axis=0, keepdims=True)
        dlb_ref[...] += jnp.sum(dvn, axis=0, keepdims=True)
        dvv = _ln_bwd(dvn, vh, rs, lg)
        gg = _gelu_grad(za)
        dz_ref[:, C_ZA:C_ZA + HALF] = (du * gg[:, :HALF]).astype(BF16)
        dz_ref[:, C_ZA + HALF:C_ZA + 2 * HALF] = (dvv * gg[:, HALF:]).astype(BF16)

        RB = TB + CV_PAD

        def colsB(c0, c1):
            return jnp.concatenate([zp_ref[HALO - CV_PAD:, c0:c1].astype(F32) * pm, zc_ref[:, c0:c1].astype(F32),
                                    zn_ref[:CV_PAD, c0:c1].astype(F32)], axis=0)

        sh_ref[0] = colsB(C_ZB, C_ZB + HALF) * _sig(colsB(C_ZB + HALF, C_ZB + 2 * HALF))
        _sublane_shifts(sh_ref, RB + CV_PAD)
        c = jnp.broadcast_to(cvb_ref[...], (RB, HALF))
        for k in range(CV_KERNEL):
            c = c + cvw_ref[k:k + 1, :] * _tap(sh_ref, CV_PAD - (CV_KERNEL - 1) + k, RB)
        cvg = cvg_ref[...]
        n, ch, rc = _ln_fwd(c, cvg, cvbb_ref[...])
        sn = _sig(n)
        dyb = jnp.concatenate([dyc_ref[:, HALF:2 * HALF].astype(F32), dyn_ref[:CV_PAD, HALF:2 * HALF].astype(F32) * nm], axis=0)
        dn = dyb * (sn + n * sn * (1.0 - sn))
        dno = dn[:TB]
        dcvg_ref[...] += jnp.sum(dno * ch[:TB], axis=0, keepdims=True)
        dcvbb_ref[...] += jnp.sum(dno, axis=0, keepdims=True)
        dc = _ln_bwd(dn, ch, rc, cvg)
        sh2_ref[0] = dc
        _sublane_shifts(sh2_ref, RB)
        dcvb_ref[...] += jnp.sum(dc[:TB], axis=0, keepdims=True)
        dy0 = None
        for k in range(CV_KERNEL):
            wk = cvw_ref[k:k + 1, :]
            t = wk * _tap(sh2_ref, CV_KERNEL - 1 - k, TB)
            dy0 = t if dy0 is None else dy0 + t
            dcvw_ref[k:k + 1, :] += jnp.sum(dc[:TB] * _tap(sh_ref, CV_PAD - (CV_KERNEL - 1) + k, TB), axis=0, keepdims=True)
        ab = zc_ref[:, C_ZB:C_ZB + HALF].astype(F32)
        sg = _sig(zc_ref[:, C_ZB + HALF:C_ZB + 2 * HALF].astype(F32))
        dz_ref[:, C_ZB:C_ZB + HALF] = (dy0 * sg).astype(BF16)
        dz_ref[:, C_ZB + HALF:C_ZB + 2 * HALF] = (dy0 * ab * sg * (1.0 - sg)).astype(BF16)

        zd = colsE(C_ZD + HALF, C_ZD + 3 * HALF)
        scr_ref[...] = zd[:, :HALF] * zd[:, HALF:]
        dcv = dyC(3 * HALF, 4 * HALF) * colsC(C_ZD, C_ZD + HALF)
        scr2_ref[...] = dcv
        cv = None
        dud = None
        for k in range(SC_KERNEL):
            wk = scw_ref[k:k + 1, :]
            us = scr_ref[pl.ds(HALO - (SC_KERNEL - 1) + k, TB), :]
            t = wk * us
            cv = t if cv is None else cv + t
            t2 = wk * scr2_ref[pl.ds(SC_KERNEL - 1 - k, TB), :]
            dud = t2 if dud is None else dud + t2
            dscw_ref[k:k + 1, :] += jnp.sum(dcv[:TB] * us, axis=0, keepdims=True)
        dz_ref[:, C_ZD:C_ZD + HALF] = (dyc_ref[:, 3 * HALF:4 * HALF].astype(F32) * cv).astype(BF16)
        dz_ref[:, C_ZD + HALF:C_ZD + 2 * HALF] = (dud * zc_ref[:, C_ZD + 2 * HALF:C_ZD + 3 * HALF].astype(F32)).astype(BF16)
        dz_ref[:, C_ZD + 2 * HALF:C_ZD + 3 * HALF] = (dud * zc_ref[:, C_ZD + HALF:C_ZD + 2 * HALF].astype(F32)).astype(BF16)

        cosE = jnp.concatenate([cp_ref[...], cc_ref[...], cn_ref[...]], axis=0)
        sinE = jnp.concatenate([sp_ref[...], sc_ref[...], sn_ref[...]], axis=0)
        k_ref[...] = _rope(colsE(C_K, C_K + 128), cosE, sinE).astype(BF16)
        v_ref[...] = colsE(C_V, C_V + 128).astype(BF16)
        dk_ref[...] = jnp.zeros_like(dk_ref)
        dv_ref[...] = jnp.zeros_like(dv_ref)
        q = jnp.concatenate([_rope(colsC(C_Q + 128 * j, C_Q + 128 * (j + 1)), cosE[HALO:], sinE[HALO:])
                             for j in range(4)], axis=1).astype(BF16)
        dO = dyC(2 * HALF, 3 * HALF).astype(BF16)
        lane_s = lax.broadcasted_iota(jnp.int32, (1, 128), 1)
        for qb in range(r + 1):
            first_ok = (i * r + qb) > 0
            rows = slice(qb * WINDOW, (qb + 1) * WINDOW)
            band = slice(qb * WINDOW, qb * WINDOW + 2 * WINDOW)
            for h in range(N_KV_HEADS):
                hc = slice(h * HEAD_DIM, (h + 1) * HEAD_DIM)
                kh = k_ref[band, hc]
                vh_ = v_ref[band, hc]
                heads = [slice((h * Q_PER_KV + g) * HEAD_DIM, (h * Q_PER_KV + g + 1) * HEAD_DIM) for g in range(Q_PER_KV)]
                qs = jnp.concatenate([q[rows, hs] for hs in heads], axis=0)
                dos = jnp.concatenate([dO[rows, hs] for hs in heads], axis=0)
                probs, p_sink = _attn_probs(qs, kh, _sink_col(sinks_ref, h), first_ok)
                dP = _dot_nt(dos, vh_)
                rsum = jnp.sum(probs * dP, axis=-1, keepdims=True)
                dS = (probs * (dP - rsum) * (HEAD_DIM ** -0.5)).astype(BF16)
                dk_ref[band, hc] += _dot_tn(dS, qs)
                dv_ref[band, hc] += _dot_tn(probs.astype(BF16), dos)
                if qb < r:
                    dqs = _dot(dS, kh)
                    dsk = -p_sink * rsum
                    for g in range(Q_PER_KV):
                        dq_ref[rows, heads[g]] = dqs[g * WINDOW:(g + 1) * WINDOW]
                        dsink_ref[...] += jnp.where(lane_s == h * Q_PER_KV + g, jnp.sum(dsk[g * WINDOW:(g + 1) * WINDOW]), 0.0)
        cosC, sinC = cc_ref[...], sc_ref[...]
        for j in range(4):
            dz_ref[:, C_Q + 128 * j:C_Q + 128 * (j + 1)] = _rope_t(dq_ref[:, 128 * j:128 * (j + 1)], cosC, sinC).astype(BF16)
        dz_ref[:, C_K:C_K + 128] = _rope_t(dk_ref[HALO:HALO + TB, :], cosC, sinC).astype(BF16)
        dz_ref[:, C_V:C_V + 128] = dv_ref[HALO:HALO + TB, :].astype(BF16)

    small = [((1, HALF), F32), ((1, HALF), F32), ((SG_GROUPS, SG_CHUNK, SG_CHUNK), F32), ((SG_CHUNK, 128), F32),
             ((32, HALF), F32), ((1, HALF), F32), ((1, HALF), F32), ((1, HALF), F32), ((1, 128), F32), ((8, HALF), F32)]
    outs = pl.pallas_call(
        body, name=name, grid=(nb,),
        in_specs=[cur, prev, nxt, dcur, dnxt, tcur, tcur, tprev, tprev, tnxt, tnxt] + _mixer_param_specs()
                 + [pl.BlockSpec(memory_space=pl.ANY)],
        out_specs=[pl.BlockSpec((TB, MIX_W), lambda i: (i, 1))] + [_full(s) for s, _ in small],
        out_shape=[jax.ShapeDtypeStruct((T, PROJ_PAD), BF16)] + [jax.ShapeDtypeStruct(s, d) for s, d in small],
        scratch_shapes=[pltpu.VMEM((RE, HALF), F32), pltpu.VMEM((RC, HALF), F32), pltpu.VMEM((RE, 128), BF16), pltpu.VMEM((RE, 128), BF16),
                        pltpu.VMEM((RE, 128), F32), pltpu.VMEM((RE, 128), F32), pltpu.VMEM((TB, HALF), F32),
                        pltpu.VMEM((8, TB + 2 * CV_PAD, HALF), F32), pltpu.VMEM((8, TB + CV_PAD, HALF), F32)],
        input_output_aliases={21: 0},
        compiler_params=_params("arbitrary"),
    )(proj, proj, proj, dys, dys, cos_t, sin_t, cos_t, sin_t, cos_t, sin_t, *mp, dproj)
    return outs


def _rope_tables(T):
    pos = jnp.arange(T, dtype=F32)
    inv_freq = 1.0 / (ROPE_THETA ** (jnp.arange(0, HEAD_DIM, 2, dtype=F32) / HEAD_DIM))
    ang = pos[:, None] * inv_freq[None, :]
    cos, sin = jnp.cos(ang), jnp.sin(ang)
    cos_t = jnp.concatenate([cos, cos, cos, cos], axis=1)
    sin_t = jnp.concatenate([-sin, sin, -sin, sin], axis=1)
    return cos_t, sin_t


def _mixer_params(l, sg_ln_g, sg_ln_b, sg_w, sg_b, cv_w, cv_b, cv_ln_g, cv_ln_b, attn_sinks, sc_w):
    sgb_t = jnp.zeros((SG_CHUNK, 128), F32).at[:, :SG_GROUPS].set(sg_b[l].T)
    cvw = jnp.zeros((32, HALF), F32).at[:CV_KERNEL].set(cv_w[l])
    scw = jnp.zeros((8, HALF), F32).at[:SC_KERNEL].set(sc_w[l])
    sinks = jnp.zeros((1, 128), F32).at[0, :N_Q_HEADS].set(attn_sinks[l])
    return [sg_ln_g[l][None], sg_ln_b[l][None], sg_w[l], sgb_t, cvw, cv_b[l][None], cv_ln_g[l][None], cv_ln_b[l][None], sinks, scw]


def _w_in_layout(w_in_g):
    wmix = jnp.concatenate([w_in_g[0], w_in_g[1], w_in_g[2][:, :MIX_W - 2 * W_IN_SHARD]], axis=1)
    wg = jnp.concatenate([w_in_g[2][:, MIX_W - 2 * W_IN_SHARD:], w_in_g[3]], axis=1)
    return jnp.concatenate([wg, jnp.zeros((D_MODEL, MIX_W - GATE_W), w_in_g.dtype), wmix], axis=1)


def _w_in_unlayout(dw):
    cut = MIX_W - 2 * W_IN_SHARD
    return jnp.stack([dw[:, MIX_W:MIX_W + W_IN_SHARD], dw[:, MIX_W + W_IN_SHARD:MIX_W + 2 * W_IN_SHARD],
                      jnp.concatenate([dw[:, MIX_W + 2 * W_IN_SHARD:], dw[:, :W_IN_SHARD - cut]], axis=1),
                      dw[:, W_IN_SHARD - cut:GATE_W]], axis=0)


def _device_step(x, tgt, norm_mix, norm_ffn, norm_final, mixer_params, w_in_p, wb_g, wo_g, wgu_g, wd_g):
    T = x.shape[0]
    TM = min(T, 1024)
    TB = 256
    cos_t, sin_t = _rope_tables(T)
    saved = []
    for l in range(DEPTH):
        proj, xn = _rms_mm(x, norm_mix[l][None], w_in_p[l], TM, 2176, f"proj{l}")
        ys = _mixers_fwd(proj, cos_t, sin_t, mixer_params[l], TB, f"mixers_fwd{l}")
        xm, merged = _merge_fwd(x, ys, proj, wb_g[l], wo_g[l], 256, f"merge_fwd{l}")
        gu, hn = _rms_mm(xm, norm_ffn[l][None], wgu_g[l], TM, GU_SHARD, f"ffn_up{l}")
        x_out = _ffn_down(xm, gu, wd_g[l], 256, f"ffn_down{l}")
        saved.append((x, proj, xn, ys, xm, merged, gu, hn))
        x = x_out
    dx, dnf, loss = _final_loss(x, norm_final[None], tgt, 256, "final_loss")

    tkk = min(T, 1024)
    gk = T // tkk
    grads = [None] * DEPTH
    for l in reversed(range(DEPTH)):
        x_in, proj, xn, ys, xm, merged, gu, hn = saved[l]
        dgu, act = _swiglu_bwd(dx, gu, wd_g[l], 256, f"swiglu_bwd{l}")
        d_wd = _mm_tn(act, dx, (2, 1, gk), (tkk, D_FF // 2), lambda i, j, k: (k, i), (tkk, D_MODEL), lambda i, j, k: (k, 0),
                      (D_FF, D_MODEL), (D_FF // 2, D_MODEL), lambda i, j, k: (i, 0), f"dw_down{l}")
        d_wgu = _mm_tn(hn, dgu, (1, N_CHIPS, gk), (tkk, D_MODEL), lambda i, j, k: (k, 0), (tkk, GU_SHARD), lambda i, j, k: (k, j),
                       (N_CHIPS, D_MODEL, GU_SHARD), (None, D_MODEL, GU_SHARD), lambda i, j, k: (j, 0, 0), f"dw_gate_up{l}")
        dxm, d_nffn = _mm_nt_rmsbwd(dgu, wgu_g[l], xm, norm_ffn[l][None], dx, min(T, 512), GU_SHARD, f"ffn_up_bwd{l}")
        dys, dbr, dproj = _merge_bwd(dxm, ys, proj, wb_g[l], wo_g[l], 256, f"merge_bwd{l}")
        d_wo = _mm_tn(merged, dxm, (2, 1, gk), (tkk, 512), lambda i, j, k: (k, i), (tkk, D_MODEL), lambda i, j, k: (k, 0),
                      (D_MODEL, D_MODEL), (512, D_MODEL), lambda i, j, k: (i, 0), f"dw_out{l}")
        d_wb = _mm_tn(ys, dbr, (N_BRANCH, N_CHIPS, gk), (tkk, HALF), lambda i, j, k: (k, i), (tkk, 256), lambda i, j, k: (k, i * N_CHIPS + j),
                      (N_CHIPS, N_BRANCH, HALF, 256), (None, None, HALF, 256), lambda i, j, k: (j, i, 0, 0), f"dw_branch{l}")
        mb = _mixers_bwd(proj, dys, dproj, cos_t, sin_t, mixer_params[l], TB, f"mixers_bwd{l}")
        dproj = mb[0]
        d_win = _mm_tn(xn, dproj, (1, PROJ_PAD // 2176, gk), (tkk, D_MODEL), lambda i, j, k: (k, 0), (tkk, 2176), lambda i, j, k: (k, j),
                       (D_MODEL, PROJ_PAD), (D_MODEL, 2176), lambda i, j, k: (0, j), f"dw_in{l}")
        dx, d_nmix = _mm_nt_rmsbwd(dproj, w_in_p[l], x_in, norm_mix[l][None], dxm, min(T, 512), 2176, f"proj_bwd{l}")
        grads[l] = dict(w_in=d_win, w_branch=d_wb, w_out=d_wo, w_gate_up=d_wgu, w_down=d_wd,
                        norm_mix=d_nmix[0], norm_ffn=d_nffn[0],
                        sg_ln_g=mb[1][0], sg_ln_b=mb[2][0], sg_w=mb[3], sg_b=mb[4][:, :SG_GROUPS].T,
                        cv_w=mb[5][:CV_KERNEL], cv_b=mb[6][0], cv_ln_g=mb[7][0], cv_ln_b=mb[8][0],
                        attn_sinks=mb[9][0, :N_Q_HEADS], sc_w=mb[10][:SC_KERNEL])
    return loss, dx, dnf[0], grads


ANY = pl.BlockSpec(memory_space=pl.ANY)
BIG = ("w_in", "w_branch", "w_out", "w_gate_up", "w_down")
HALF_SHAPE = {"w_in": (2, 512, W_IN_SHARD), "w_branch": (2, 1024, 256), "w_out": (2, 128, D_MODEL),
              "w_gate_up": (2, 512, GU_SHARD), "w_down": (2, 352, D_MODEL)}
NB = len(BIG)


def _place():
    x, y, c = lax.axis_index("x"), lax.axis_index("y"), lax.axis_index("c")
    chips = [(1 - x, y), (x, 1 - y), (1 - x, 1 - y)]
    return x, y, c, 2 * x + y, chips, [2 * px + py for px, py in chips]


def _remote(src, dst, ssem, rsem, dev):
    return pltpu.make_async_remote_copy(src_ref=src, dst_ref=dst, send_sem=ssem, recv_sem=rsem, device_id=dev, device_id_type=MESH)


def _ag_weights(shards):
    def body(*refs):
        ins, outs = refs[:NB], refs[NB:2 * NB]
        s_ici, r_ici, s_d2d, r_d2d, s_own, r_own = refs[2 * NB:]
        x, y, c, chip, chips, chip_ids = _place()
        sib = (x, y, 1 - c)
        mine, sends, fwd = [], [], []
        for l in range(DEPTH):
            for a in range(NB):
                cp = _remote(ins[a].at[l], outs[a].at[l, chip], s_own.at[a, l], r_own.at[a, l], sib)
                cp.start()
                mine.append(cp)
                for j in range(3):
                    cp = _remote(ins[a].at[l, c], outs[a].at[l, chip, c], s_ici.at[a, l, j], r_ici.at[a, l, j], (*chips[j], c))
                    cp.start()
                    sends.append(cp)
        for l in range(DEPTH):
            for a in range(NB):
                for j in range(3):
                    got = outs[a].at[l, chip_ids[j], c]
                    _remote(got, got, s_ici.at[a, l, j], r_ici.at[a, l, j], sib).wait_recv()
                    cp = _remote(got, got, s_d2d.at[a, l, j], r_d2d.at[a, l, j], sib)
                    cp.start()
                    fwd.append(cp)
        for l in range(DEPTH):
            for a in range(NB):
                for j in range(3):
                    got = outs[a].at[l, chip_ids[j], 1 - c]
                    _remote(got, got, s_d2d.at[a, l, j], r_d2d.at[a, l, j], sib).wait_recv()
        for cp in mine:
            cp.wait()
        for cp in sends + fwd:
            cp.wait_send()

    out_shape = [jax.ShapeDtypeStruct((DEPTH, N_CHIPS) + HALF_SHAPE[n], BF16) for n in BIG]
    sem = pltpu.SemaphoreType.DMA((NB, DEPTH, 3))
    sem2 = pltpu.SemaphoreType.DMA((NB, DEPTH))
    return pl.pallas_call(
        body, name="ag_weights", out_shape=out_shape, in_specs=[ANY] * NB, out_specs=[ANY] * NB,
        scratch_shapes=[sem, sem, sem, sem, sem2, sem2],
        compiler_params=pltpu.CompilerParams(has_side_effects=True),
    )(*shards)


def _rs_pair(grads):
    n_arr = DEPTH * NB

    def body(*refs):
        ins, got = refs[:n_arr], refs[n_arr:2 * n_arr]
        ssem, rsem = refs[2 * n_arr:]
        x, y, c, _, _, _ = _place()
        sib = (x, y, 1 - c)
        sends = []
        for k in reversed(range(n_arr)):
            for q in range(N_CHIPS):
                cp = _remote(ins[k].at[q, 1 - c], got[k].at[q], ssem.at[k, q], rsem.at[k, q], sib)
                cp.start()
                sends.append(cp)
        for k in range(n_arr):
            for q in range(N_CHIPS):
                _remote(got[k].at[q], got[k].at[q], ssem.at[k, q], rsem.at[k, q], sib).wait_recv()
        for cp in sends:
            cp.wait_send()

    shp = [jax.ShapeDtypeStruct((N_CHIPS,) + HALF_SHAPE[n][1:], BF16) for _ in range(DEPTH) for n in BIG]
    sem = pltpu.SemaphoreType.DMA((n_arr, N_CHIPS))
    outs = pl.pallas_call(
        body, name="rs_pair", out_shape=shp, in_specs=[ANY] * n_arr, out_specs=[ANY] * n_arr,
        scratch_shapes=[sem, sem], compiler_params=pltpu.CompilerParams(has_side_effects=True),
    )(*[grads[l][a] for l in range(DEPTH) for a in range(NB)])
    return [outs[l * NB:(l + 1) * NB] for l in range(DEPTH)]


def _rs_chips(partials):
    n_arr = DEPTH * NB

    def body(*refs):
        ins, outs = refs[:n_arr], refs[n_arr:2 * n_arr]
        ssem, rsem = refs[2 * n_arr:]
        x, y, c, chip, chips, chip_ids = _place()
        sends = []
        for k in reversed(range(n_arr)):
            for j in range(3):
                cp = _remote(ins[k].at[chip_ids[j]], outs[k].at[chip], ssem.at[k, j], rsem.at[k, j], (*chips[j], c))
                cp.start()
                sends.append(cp)
        for k in range(n_arr):
            for j in range(3):
                dst = outs[k].at[chip_ids[j]]
                _remote(dst, dst, ssem.at[k, j], rsem.at[k, j], (*chips[j], c)).wait_recv()
        for cp in sends:
            cp.wait_send()

    shp = [jax.ShapeDtypeStruct((N_CHIPS,) + HALF_SHAPE[n][1:], BF16) for _ in range(DEPTH) for n in BIG]
    sem = pltpu.SemaphoreType.DMA((n_arr, 3))
    outs = pl.pallas_call(
        body, name="rs_chips", out_shape=shp, in_specs=[ANY] * n_arr, out_specs=[ANY] * n_arr,
        scratch_shapes=[sem, sem], compiler_params=pltpu.CompilerParams(has_side_effects=True),
    )(*[partials[l][a] for l in range(DEPTH) for a in range(NB)])
    return [outs[l * NB:(l + 1) * NB] for l in range(DEPTH)]


def _rs_share(bufs):
    def body(*refs):
        ins, outs = refs[:NB], refs[NB:2 * NB]
        ssem, rsem = refs[2 * NB:]
        del ins
        x, y, c, _, _, _ = _place()
        sib = (x, y, 1 - c)
        sends = []
        for a in range(NB):
            for l in range(DEPTH):
                cp = _remote(outs[a].at[l, c], outs[a].at[l, c], ssem.at[a, l], rsem.at[a, l], sib)
                cp.start()
                sends.append(cp)
        for a in range(NB):
            for l in range(DEPTH):
                dst = outs[a].at[l, 1 - c]
                _remote(dst, dst, ssem.at[a, l], rsem.at[a, l], sib).wait_recv()
        for cp in sends:
            cp.wait_send()

    shp = [jax.ShapeDtypeStruct((DEPTH,) + HALF_SHAPE[n], F32) for n in BIG]
    sem = pltpu.SemaphoreType.DMA((NB, DEPTH))
    return pl.pallas_call(
        body, name="rs_share", out_shape=shp, in_specs=[ANY] * NB, out_specs=[ANY] * NB,
        input_output_aliases={a: a for a in range(NB)},
        scratch_shapes=[sem, sem], compiler_params=pltpu.CompilerParams(has_side_effects=True),
    )(*bufs)


def _all_reduce_small(buf, name):
    R = buf.shape[0]
    offs = [(dx, dy, dc) for dx in (0, 1) for dy in (0, 1) for dc in (0, 1)][1:]

    def body(in_ref, out_ref, gather_ref, ssem, rsem):
        x, y, c = lax.axis_index("x"), lax.axis_index("y"), lax.axis_index("c")
        me = 4 * x + 2 * y + c
        gather_ref[me] = in_ref[...]
        flip = lambda v, d: 1 - v if d else v
        peers = [(flip(x, dx), flip(y, dy), flip(c, dc)) for dx, dy, dc in offs]
        cps = [_remote(in_ref, gather_ref.at[me], ssem.at[k], rsem.at[k], peers[k]) for k in range(N_DEV - 1)]
        for cp in cps:
            cp.start()
        for k, (px, py, pc) in enumerate(peers):
            _remote(in_ref, gather_ref.at[4 * px + 2 * py + pc], ssem.at[k], rsem.at[k], peers[k]).wait_recv()
        acc = gather_ref[0]
        for s in range(1, N_DEV):
            acc = acc + gather_ref[s]
        out_ref[...] = acc
        for cp in cps:
            cp.wait_send()

    vm = pl.BlockSpec(memory_space=pltpu.VMEM)
    return pl.pallas_call(
        body, name=name, out_shape=jax.ShapeDtypeStruct((R, 128), F32), in_specs=[vm], out_specs=vm,
        scratch_shapes=[pltpu.VMEM((N_DEV, R, 128), F32), pltpu.SemaphoreType.DMA((N_DEV - 1,)), pltpu.SemaphoreType.DMA((N_DEV - 1,))],
        compiler_params=pltpu.CompilerParams(vmem_limit_bytes=VMEM_LIMIT),
    )(buf)


def _row_tile(rows, cols, n_arrays):
    budget = 20 * 1024 * 1024 // (n_arrays * 2 * cols * 4)
    tr = rows
    while tr > budget or tr % 16:
        assert tr % 2 == 0, (rows, cols)
        tr //= 2
    return tr


def _add_pairs(g, got, place, name):
    _, _, rows, cols = g.shape
    tr = _row_tile(rows, cols, 3)

    def body(place_ref, a_ref, b_ref, o_ref):
        del place_ref
        o_ref[...] = (a_ref[...].astype(F32) + b_ref[...].astype(F32)).astype(BF16)

    spec = pl.BlockSpec((None, tr, cols), lambda q, i, p: (q, i, 0))
    grid_spec = pltpu.PrefetchScalarGridSpec(
        num_scalar_prefetch=1, grid=(N_CHIPS, rows // tr),
        in_specs=[pl.BlockSpec((None, None, tr, cols), lambda q, i, p: (q, p[1], i, 0)), spec], out_specs=spec)
    return pl.pallas_call(body, name=name, grid_spec=grid_spec, out_shape=jax.ShapeDtypeStruct((N_CHIPS, rows, cols), BF16),
                          compiler_params=_params("parallel", "parallel"))(place, g, got)


def _sum_chips(own, recv, place, l, buf, name):
    _, rows, cols = own.shape
    tr = _row_tile(rows, cols, 4)

    def body(place_ref, own_ref, recv_ref, *rest):
        chip = place_ref[0]
        acc = own_ref[...].astype(F32)
        for j in range(1, N_CHIPS):
            acc = acc + recv_ref[lax.rem(chip + j, N_CHIPS)].astype(F32)
        rest[-1][...] = acc

    in_specs = [pl.BlockSpec((None, tr, cols), lambda i, p: (p[0], i, 0)), pl.BlockSpec((N_CHIPS, tr, cols), lambda i, p: (0, i, 0))]
    args = [place, own, recv]
    aliases = {}
    if buf is not None:
        in_specs.append(ANY)
        args.append(buf)
        aliases = {3: 0}
    grid_spec = pltpu.PrefetchScalarGridSpec(
        num_scalar_prefetch=1, grid=(rows // tr,), in_specs=in_specs,
        out_specs=pl.BlockSpec((None, None, tr, cols), lambda i, p: (l, p[1], i, 0)))
    return pl.pallas_call(body, name=name, grid_spec=grid_spec, out_shape=jax.ShapeDtypeStruct((DEPTH, 2, rows, cols), F32),
                          input_output_aliases=aliases, compiler_params=_params("parallel"))(*args)


def _adamw(w, g, m, v, name):
    shape = w.shape
    cols = shape[-1]
    w2, g2, m2, v2 = (t.reshape(-1, cols) for t in (w, g, m, v))
    rows = w2.shape[0]
    tr = _row_tile(rows, cols, 7)

    def body(w_ref, g_ref, m_ref, v_ref, d_ref, mo_ref, vo_ref):
        gv = g_ref[...]
        mn = ADAM_B1 * m_ref[...] + (1.0 - ADAM_B1) * gv
        vn = ADAM_B2 * v_ref[...] + (1.0 - ADAM_B2) * (gv * gv)
        m_hat = mn / (1.0 - ADAM_B1 ** ADAM_STEP)
        v_hat = vn / (1.0 - ADAM_B2 ** ADAM_STEP)
        d_ref[...] = -ADAM_LR * (m_hat / (jnp.sqrt(v_hat) + ADAM_EPS) + ADAM_WD * w_ref[...])
        mo_ref[...] = mn
        vo_ref[...] = vn

    spec = pl.BlockSpec((tr, cols), lambda i: (i, 0))
    outs = pl.pallas_call(body, name=name, grid=(rows // tr,), in_specs=[spec] * 4, out_specs=[spec] * 3,
                          out_shape=[jax.ShapeDtypeStruct((rows, cols), F32)] * 3, compiler_params=_params("parallel"))(w2, g2, m2, v2)
    return [o.reshape(shape) for o in outs]


def _pack(arrays):
    parts = []
    for t in arrays:
        f = t.reshape(-1)
        parts.append(jnp.pad(f, (0, (-f.shape[0]) % 128)))
    f = jnp.concatenate(parts)
    f = jnp.pad(f, (0, (-f.shape[0]) % 2048))
    return f.reshape(-1, 128)


def _unpack(buf, shapes):
    f = buf.reshape(-1)
    out, off = [], 0
    for s in shapes:
        n = math.prod(s)
        out.append(f[off:off + n].reshape(s))
        off += n + (-n) % 128
    return out


SMALL = ("norm_mix", "sg_ln_g", "sg_ln_b", "sg_w", "sg_b", "cv_w", "cv_b", "cv_ln_g", "cv_ln_b", "attn_sinks", "sc_w", "norm_ffn", "norm_final")
ORDER = ("norm_mix", "w_in", "sg_ln_g", "sg_ln_b", "sg_w", "sg_b", "cv_w", "cv_b", "cv_ln_g", "cv_ln_b", "attn_sinks", "sc_w",
         "w_branch", "w_out", "norm_ffn", "w_gate_up", "w_down", "norm_final")


def kernel(x, norm_mix, w_in, sg_ln_g, sg_ln_b, sg_w, sg_b, cv_w, cv_b, cv_ln_g, cv_ln_b, attn_sinks, sc_w, w_branch, w_out, norm_ffn, w_gate_up, w_down, norm_final, loss_target, m_norm_mix, m_w_in, m_sg_ln_g, m_sg_ln_b, m_sg_w, m_sg_b, m_cv_w, m_cv_b, m_cv_ln_g, m_cv_ln_b, m_attn_sinks, m_sc_w, m_w_branch, m_w_out, m_norm_ffn, m_w_gate_up, m_w_down, m_norm_final, v_norm_mix, v_w_in, v_sg_ln_g, v_sg_ln_b, v_sg_w, v_sg_b, v_cv_w, v_cv_b, v_cv_ln_g, v_cv_ln_b, v_attn_sinks, v_sc_w, v_w_branch, v_w_out, v_norm_ffn, v_w_gate_up, v_w_down, v_norm_final):
    W = dict(norm_mix=norm_mix, w_in=w_in, sg_ln_g=sg_ln_g, sg_ln_b=sg_ln_b, sg_w=sg_w, sg_b=sg_b, cv_w=cv_w, cv_b=cv_b, cv_ln_g=cv_ln_g,
             cv_ln_b=cv_ln_b, attn_sinks=attn_sinks, sc_w=sc_w, w_branch=w_branch, w_out=w_out, norm_ffn=norm_ffn, w_gate_up=w_gate_up,
             w_down=w_down, norm_final=norm_final)
    M = dict(norm_mix=m_norm_mix, w_in=m_w_in, sg_ln_g=m_sg_ln_g, sg_ln_b=m_sg_ln_b, sg_w=m_sg_w, sg_b=m_sg_b, cv_w=m_cv_w, cv_b=m_cv_b,
             cv_ln_g=m_cv_ln_g, cv_ln_b=m_cv_ln_b, attn_sinks=m_attn_sinks, sc_w=m_sc_w, w_branch=m_w_branch, w_out=m_w_out,
             norm_ffn=m_norm_ffn, w_gate_up=m_w_gate_up, w_down=m_w_down, norm_final=m_norm_final)
    V = dict(norm_mix=v_norm_mix, w_in=v_w_in, sg_ln_g=v_sg_ln_g, sg_ln_b=v_sg_ln_b, sg_w=v_sg_w, sg_b=v_sg_b, cv_w=v_cv_w, cv_b=v_cv_b,
             cv_ln_g=v_cv_ln_g, cv_ln_b=v_cv_ln_b, attn_sinks=v_attn_sinks, sc_w=v_sc_w, w_branch=v_w_branch, w_out=v_w_out,
             norm_ffn=v_norm_ffn, w_gate_up=v_w_gate_up, w_down=v_w_down, norm_final=v_norm_final)
    mx, my, mc = lax.axis_index("x"), lax.axis_index("y"), lax.axis_index("c")
    chip = 2 * mx + my

    gathered = _ag_weights([W[n].astype(BF16).reshape((DEPTH,) + HALF_SHAPE[n]) for n in BIG])
    G = dict(zip(BIG, gathered))
    south = (mc == 0).astype(F32)
    cvw_z = lax.dynamic_update_slice(jnp.zeros((DEPTH, CV_KERNEL, HALF), F32), cv_w * south, (0, 0, chip * 128))
    scw_z = lax.dynamic_update_slice(jnp.zeros((DEPTH, SC_KERNEL, HALF), F32), sc_w * south, (0, 0, chip * 128))
    cvw_full, scw_full = _unpack(_all_reduce_small(_pack([cvw_z, scw_z]), "ag_small"), [cvw_z.shape, scw_z.shape])

    mixer_params = [_mixer_params(l, sg_ln_g, sg_ln_b, sg_w, sg_b, cvw_full, cv_b, cv_ln_g, cv_ln_b, attn_sinks, scw_full) for l in range(DEPTH)]
    w_in_p = [_w_in_layout(G["w_in"][l].reshape(N_CHIPS, D_MODEL, W_IN_SHARD)) for l in range(DEPTH)]
    wb_g = [G["w_branch"][l].reshape(N_CHIPS, N_BRANCH, HALF, 256) for l in range(DEPTH)]
    wo_g = [G["w_out"][l].reshape(D_MODEL, D_MODEL) for l in range(DEPTH)]
    wgu_g = [G["w_gate_up"][l].reshape(N_CHIPS, D_MODEL, GU_SHARD) for l in range(DEPTH)]
    wd_g = [G["w_down"][l].reshape(D_FF, D_MODEL) for l in range(DEPTH)]

    loss, dx, d_nfinal, grads = _device_step(x[0], loss_target[0], norm_mix, norm_ffn, norm_final, mixer_params, w_in_p, wb_g, wo_g, wgu_g, wd_g)

    small_g = {n: jnp.stack([grads[l][n] for l in range(DEPTH)]) for n in SMALL if n != "norm_final"}
    small_g["norm_final"] = d_nfinal
    red = _unpack(_all_reduce_small(_pack([small_g[n] for n in SMALL] + [loss]), "ar_small"),
                  [small_g[n].shape for n in SMALL] + [loss.shape])
    small_red = dict(zip(SMALL, red[:-1]))
    loss_out = red[-1][0, 0]
    for n in ("cv_w", "sc_w"):
        small_red[n] = lax.dynamic_slice_in_dim(small_red[n], chip * 128, 128, axis=2)

    per_layer = []
    for l in range(DEPTH):
        g = grads[l]
        per_layer.append([_w_in_unlayout(g["w_in"]).reshape((N_CHIPS,) + HALF_SHAPE["w_in"]),
                          g["w_branch"].reshape((N_CHIPS,) + HALF_SHAPE["w_branch"]),
                          g["w_out"].reshape((N_CHIPS,) + HALF_SHAPE["w_out"]),
                          g["w_gate_up"].reshape((N_CHIPS,) + HALF_SHAPE["w_gate_up"]),
                          g["w_down"].reshape((N_CHIPS,) + HALF_SHAPE["w_down"])])
    place = jnp.stack([chip, mc]).astype(jnp.int32)
    got = _rs_pair(per_layer)
    partial = [[_add_pairs(per_layer[l][a], got[l][a], place, f"rs_add{l}_{BIG[a]}") for a in range(NB)] for l in range(DEPTH)]
    recv = _rs_chips(partial)
    bufs = []
    for a in range(NB):
        buf = None
        for l in range(DEPTH):
            buf = _sum_chips(partial[l][a], recv[l][a], place, l, buf, f"rs_sum{l}_{BIG[a]}")
        bufs.append(buf)
    shared = _rs_share(bufs)
    big_red = {n: shared[a].reshape(W[n].shape) for a, n in enumerate(BIG)}

    upd = {}
    for n in BIG:
        upd[n] = [big_red[n]] + _adamw(W[n], big_red[n], M[n], V[n], f"adamw_{n}")
    sw, sg, sm, sv = (_pack([t[n] for n in SMALL]) for t in (W, small_red, M, V))
    packed = _adamw(sw, sg, sm, sv, "adamw_small")
    shapes = [W[n].shape for n in SMALL]
    for n, d, mo, vo in zip(SMALL, *(_unpack(p, shapes) for p in packed)):
        upd[n] = [small_red[n], d, mo, vo]

    out = [loss_out, dx[None]]
    for k in range(4):
        out += [upd[n][k] for n in ORDER]
    return tuple(out)
```

```python
import functools
import math

import jax
import jax.numpy as jnp
from jax import lax
from jax.experimental import pallas as pl
from jax.experimental.pallas import tpu as pltpu

F32 = jnp.float32
BF16 = jnp.bfloat16

D_MODEL = 1024
DEPTH = 2
HALF = 512
SG_CHUNK = 128
SG_GROUPS = 4
CV_KERNEL = 31
HEAD_DIM = 64
N_Q_HEADS = 8
N_KV_HEADS = 2
Q_PER_KV = N_Q_HEADS // N_KV_HEADS
WINDOW = 128
ROPE_THETA = 10000.0
SC_KERNEL = 3
N_BRANCH = 4
D_FF = 2816
EPS = 1e-6
N_CHIPS = 4
N_DEV = 8

MIX_W = 4352
GATE_W = N_BRANCH * D_MODEL
PROJ_PAD = 2 * MIX_W
W_IN_SHARD = 2112
GU_SHARD = 1408
HALO = 128
CV_PAD = 32

ADAM_LR = 0.001
ADAM_B1 = 0.9
ADAM_B2 = 0.999
ADAM_EPS = 1e-08
ADAM_WD = 0.01
ADAM_STEP = 10

VMEM_LIMIT = 56 * 1024 * 1024
INV_SQRT2 = 1.0 / math.sqrt(2.0)
INV_SQRT_2PI = 1.0 / math.sqrt(2.0 * math.pi)
NEG_BIG = -1e30
MESH = pl.DeviceIdType.MESH

C_ZA, C_ZB, C_Q, C_K, C_V, C_ZD = 0, 1024, 2048, 2560, 2688, 2816


def _params(*sem):
    return pltpu.CompilerParams(dimension_semantics=sem, vmem_limit_bytes=VMEM_LIMIT)


def _sig(v):
    return 1.0 / (1.0 + jnp.exp(-v))


def _dot(a, b):
    return jnp.dot(a, b, preferred_element_type=F32)


def _dot_nt(a, b):
    return lax.dot_general(a, b, (((1,), (1,)), ((), ())), preferred_element_type=F32)


def _dot_tn(a, b):
    return lax.dot_general(a, b, (((0,), (0,)), ((), ())), preferred_element_type=F32)


def _full(shape):
    nd = len(shape)
    return pl.BlockSpec(shape, lambda *_: (0,) * nd)


def _rms_mm(x, g, w, tm, tn, name):
    T = x.shape[0]
    if w.ndim == 3:
        tn = w.shape[2]
        N = w.shape[0] * tn
        wspec = pl.BlockSpec((None, D_MODEL, tn), lambda i, j: (j, 0, 0))
    else:
        N = w.shape[1]
        wspec = pl.BlockSpec((D_MODEL, tn), lambda i, j: (0, j))

    def body(x_ref, g_ref, w_ref, o_ref, xn_ref):
        @pl.when(pl.program_id(1) == 0)
        def _():
            xv = x_ref[...]
            r = lax.rsqrt(jnp.mean(xv * xv, axis=-1, keepdims=True) + EPS)
            xn_ref[...] = (xv * r * g_ref[...]).astype(BF16)

        o_ref[...] = _dot(xn_ref[...], w_ref[...]).astype(BF16)

    return pl.pallas_call(
        body, name=name, grid=(T // tm, N // tn),
        in_specs=[pl.BlockSpec((tm, D_MODEL), lambda i, j: (i, 0)), _full((1, D_MODEL)), wspec],
        out_specs=[pl.BlockSpec((tm, tn), lambda i, j: (i, j)), pl.BlockSpec((tm, D_MODEL), lambda i, j: (i, 0))],
        out_shape=[jax.ShapeDtypeStruct((T, N), BF16), jax.ShapeDtypeStruct((T, D_MODEL), BF16)],
        compiler_params=_params("parallel", "arbitrary"),
    )(x, g, w)


def _merge_fwd(x, ys, proj, wb, wo, tm, name):
    T = x.shape[0]

    def body(x_ref, ys_ref, zg_ref, wb_ref, wo_ref, xo_ref, mg_ref):
        merged = None
        for n in range(N_BRANCH):
            yn = ys_ref[:, n * HALF:(n + 1) * HALF]
            br = jnp.concatenate([_dot(yn, wb_ref[s, n]) for s in range(N_CHIPS)], axis=1)
            t = _sig(zg_ref[:, n * D_MODEL:(n + 1) * D_MODEL].astype(F32)) * br
            merged = t if merged is None else merged + t
        mb = merged.astype(BF16)
        mg_ref[...] = mb
        xo_ref[...] = x_ref[...] + _dot(mb, wo_ref[...])

    return pl.pallas_call(
        body, name=name, grid=(T // tm,),
        in_specs=[pl.BlockSpec((tm, D_MODEL), lambda i: (i, 0)), pl.BlockSpec((tm, N_BRANCH * HALF), lambda i: (i, 0)),
                  pl.BlockSpec((tm, GATE_W), lambda i: (i, 0)), _full(wb.shape), _full(wo.shape)],
        out_specs=[pl.BlockSpec((tm, D_MODEL), lambda i: (i, 0)), pl.BlockSpec((tm, D_MODEL), lambda i: (i, 0))],
        out_shape=[jax.ShapeDtypeStruct((T, D_MODEL), F32), jax.ShapeDtypeStruct((T, D_MODEL), BF16)],
        compiler_params=_params("parallel"),
    )(x, ys, proj, wb, wo)


def _ffn_down(xm, gu, wd, tm, name):
    T = xm.shape[0]

    def body(x_ref, gu_ref, wd_ref, o_ref):
        g = gu_ref[:, :D_FF].astype(F32)
        u = gu_ref[:, D_FF:].astype(F32)
        act = (g * _sig(g) * u).astype(BF16)
        o_ref[...] = x_ref[...] + _dot(act, wd_ref[...])

    return pl.pallas_call(
        body, name=name, grid=(T // tm,),
        in_specs=[pl.BlockSpec((tm, D_MODEL), lambda i: (i, 0)), pl.BlockSpec((tm, 2 * D_FF), lambda i: (i, 0)), _full(wd.shape)],
        out_specs=pl.BlockSpec((tm, D_MODEL), lambda i: (i, 0)),
        out_shape=jax.ShapeDtypeStruct((T, D_MODEL), F32),
        compiler_params=_params("parallel"),
    )(xm, gu, wd)


def _final_loss(x, g, tgt, tm, name):
    T = x.shape[0]

    def body(x_ref, g_ref, t_ref, dx_ref, dg_ref, ls_ref):
        @pl.when(pl.program_id(0) == 0)
        def _():
            dg_ref[...] = jnp.zeros_like(dg_ref)
            ls_ref[...] = jnp.zeros_like(ls_ref)

        xv = x_ref[...]
        gv = g_ref[...]
        r = lax.rsqrt(jnp.mean(xv * xv, axis=-1, keepdims=True) + EPS)
        xh = xv * r
        diff = xh * gv - t_ref[...]
        ls_ref[...] += jnp.full(ls_ref.shape, 0.5 / D_MODEL, F32) * jnp.sum(diff * diff)
        dy = diff * (1.0 / D_MODEL)
        dxh = dy * gv
        dx_ref[...] = r * (dxh - xh * jnp.mean(dxh * xh, axis=-1, keepdims=True))
        dg_ref[...] += jnp.sum(dy * xh, axis=0, keepdims=True)

    return pl.pallas_call(
        body, name=name, grid=(T // tm,),
        in_specs=[pl.BlockSpec((tm, D_MODEL), lambda i: (i, 0)), _full((1, D_MODEL)), pl.BlockSpec((tm, D_MODEL), lambda i: (i, 0))],
        out_specs=[pl.BlockSpec((tm, D_MODEL), lambda i: (i, 0)), _full((1, D_MODEL)), _full((1, 128))],
        out_shape=[jax.ShapeDtypeStruct((T, D_MODEL), F32), jax.ShapeDtypeStruct((1, D_MODEL), F32), jax.ShapeDtypeStruct((1, 128), F32)],
        compiler_params=_params("arbitrary"),
    )(x, g, tgt)


def _swiglu_bwd(dx, gu, wd, tm, name, after):
    T = dx.shape[0]

    def body(dx_ref, gu_ref, wd_ref, after_ref, dgu_ref, act_ref):
        del after_ref
        dact = _dot_nt(dx_ref[...].astype(BF16), wd_ref[...])
        g = gu_ref[:, :D_FF].astype(F32)
        u = gu_ref[:, D_FF:].astype(F32)
        s = _sig(g)
        silu = g * s
        act_ref[...] = (silu * u).astype(BF16)
        dgu_ref[:, :D_FF] = (dact * u * (s + silu * (1.0 - s))).astype(BF16)
        dgu_ref[:, D_FF:] = (dact * silu).astype(BF16)

    return pl.pallas_call(
        body, name=name, grid=(T // tm,),
        in_specs=[pl.BlockSpec((tm, D_MODEL), lambda i: (i, 0)), pl.BlockSpec((tm, 2 * D_FF), lambda i: (i, 0)), _full(wd.shape),
                  pl.BlockSpec(memory_space=pl.ANY)],
        out_specs=[pl.BlockSpec((tm, 2 * D_FF), lambda i: (i, 0)), pl.BlockSpec((tm, D_FF), lambda i: (i, 0))],
        out_shape=[jax.ShapeDtypeStruct((T, 2 * D_FF), BF16), jax.ShapeDtypeStruct((T, D_FF), BF16)],
        compiler_params=_params("parallel"),
    )(dx, gu, wd, after)


def _mm_tn(a, b, grid, a_block, a_map, b_block, b_map, o_shape, o_block, o_map, name):
    gk = grid[2]
    tm = [d for d in a_block if d is not None][-1]
    tn = [d for d in b_block if d is not None][-1]

    def body(a_ref, b_ref, o_ref, acc_ref):
        k = pl.program_id(2)
        p = _dot_tn(a_ref[...].astype(BF16), b_ref[...].astype(BF16))

        @pl.when(k == 0)
        def _():
            acc_ref[...] = p

        @pl.when(k > 0)
        def _():
            acc_ref[...] += p

        @pl.when(k == gk - 1)
        def _():
            o_ref[...] = acc_ref[...].astype(o_ref.dtype)

    return pl.pallas_call(
        body, name=name, grid=grid,
        in_specs=[pl.BlockSpec(a_block, a_map), pl.BlockSpec(b_block, b_map)],
        out_specs=pl.BlockSpec(o_block, o_map),
        out_shape=jax.ShapeDtypeStruct(o_shape, BF16),
        scratch_shapes=[pltpu.VMEM((tm, tn), F32)],
        compiler_params=_params("parallel", "parallel", "arbitrary"),
    )(a, b)


def _mm_nt_rmsbwd(a, w, x, g, dres, tm, tk, name):
    T = x.shape[0]
    if w.ndim == 3:
        tk = w.shape[2]
        gk = w.shape[0]
        wspec = pl.BlockSpec((None, D_MODEL, tk), lambda i, k: (k, 0, 0))
    else:
        gk = w.shape[1] // tk
        wspec = pl.BlockSpec((D_MODEL, tk), lambda i, k: (0, k))

    def body(a_ref, w_ref, x_ref, g_ref, r_ref, dx_ref, dg_ref, acc_ref):
        i, k = pl.program_id(0), pl.program_id(1)
        p = _dot_nt(a_ref[...], w_ref[...])

        @pl.when(k == 0)
        def _():
            acc_ref[...] = p

        @pl.when(k > 0)
        def _():
            acc_ref[...] += p

        @pl.when(jnp.logical_and(i == 0, k == 0))
        def _():
            dg_ref[...] = jnp.zeros_like(dg_ref)

        @pl.when(k == gk - 1)
        def _():
            dh = acc_ref[...]
            xv = x_ref[...]
            r = lax.rsqrt(jnp.mean(xv * xv, axis=-1, keepdims=True) + EPS)
            xh = xv * r
            dxh = dh * g_ref[...]
            dx_ref[...] = r_ref[...] + r * (dxh - xh * jnp.mean(dxh * xh, axis=-1, keepdims=True))
            dg_ref[...] += jnp.sum(dh * xh, axis=0, keepdims=True)

    return pl.pallas_call(
        body, name=name, grid=(T // tm, gk),
        in_specs=[pl.BlockSpec((tm, tk), lambda i, k: (i, k)), wspec, pl.BlockSpec((tm, D_MODEL), lambda i, k: (i, 0)),
                  _full((1, D_MODEL)), pl.BlockSpec((tm, D_MODEL), lambda i, k: (i, 0))],
        out_specs=[pl.BlockSpec((tm, D_MODEL), lambda i, k: (i, 0)), _full((1, D_MODEL))],
        out_shape=[jax.ShapeDtypeStruct((T, D_MODEL), F32), jax.ShapeDtypeStruct((1, D_MODEL), F32)],
        scratch_shapes=[pltpu.VMEM((tm, D_MODEL), F32)],
        compiler_params=_params("arbitrary", "arbitrary"),
    )(a, w, x, g, dres)


def _merge_bwd(dxm, ys, proj, wb, wo, tm, name):
    T = dxm.shape[0]

    def body(dx_ref, ys_ref, zg_ref, wb_ref, wo_ref, dys_ref, dbr_ref, dp_ref):
        dmerged = _dot_nt(dx_ref[...].astype(BF16), wo_ref[...])
        for n in range(N_BRANCH):
            yn = ys_ref[:, n * HALF:(n + 1) * HALF]
            br = jnp.concatenate([_dot(yn, wb_ref[s, n]) for s in range(N_CHIPS)], axis=1)
            gt = _sig(zg_ref[:, n * D_MODEL:(n + 1) * D_MODEL].astype(F32))
            dbr = (gt * dmerged).astype(BF16)
            dbr_ref[:, n * D_MODEL:(n + 1) * D_MODEL] = dbr
            dp_ref[:, n * D_MODEL:(n + 1) * D_MODEL] = (dmerged * br * gt * (1.0 - gt)).astype(BF16)
            dy = None
            for s in range(N_CHIPS):
                t = _dot_nt(dbr[:, s * 256:(s + 1) * 256], wb_ref[s, n])
                dy = t if dy is None else dy + t
            dys_ref[:, n * HALF:(n + 1) * HALF] = dy.astype(BF16)
        dp_ref[:, GATE_W:] = jnp.zeros((tm, MIX_W - GATE_W), BF16)

    return pl.pallas_call(
        body, name=name, grid=(T // tm,),
        in_specs=[pl.BlockSpec((tm, D_MODEL), lambda i: (i, 0)), pl.BlockSpec((tm, N_BRANCH * HALF), lambda i: (i, 0)),
                  pl.BlockSpec((tm, GATE_W), lambda i: (i, 0)), _full(wb.shape), _full(wo.shape)],
        out_specs=[pl.BlockSpec((tm, N_BRANCH * HALF), lambda i: (i, 0)), pl.BlockSpec((tm, GATE_W), lambda i: (i, 0)),
                   pl.BlockSpec((tm, MIX_W), lambda i: (i, 0))],
        out_shape=[jax.ShapeDtypeStruct((T, N_BRANCH * HALF), BF16), jax.ShapeDtypeStruct((T, GATE_W), BF16),
                   jax.ShapeDtypeStruct((T, PROJ_PAD), BF16)],
        compiler_params=_params("parallel"),
    )(dxm, ys, proj, wb, wo)


def _gelu(v):
    return 0.5 * v * (1.0 + lax.erf(v * INV_SQRT2))


def _gelu_grad(v):
    return 0.5 * (1.0 + lax.erf(v * INV_SQRT2)) + v * jnp.exp(-0.5 * v * v) * INV_SQRT_2PI


def _rot_half(t):
    w = t.shape[1]
    lane = lax.broadcasted_iota(jnp.int32, t.shape, 1)
    return jnp.where((lane % HEAD_DIM) < HEAD_DIM // 2, pltpu.roll(t, w - HEAD_DIM // 2, 1), pltpu.roll(t, HEAD_DIM // 2, 1))


def _rope(t, cos, sin_signed):
    return t * cos + _rot_half(t) * sin_signed


def _rope_t(d, cos, sin_signed):
    return d * cos + _rot_half(d * sin_signed)


def _ln_fwd(v, g, b):
    mu = jnp.mean(v, axis=-1, keepdims=True)
    vc = v - mu
    r = lax.rsqrt(jnp.mean(vc * vc, axis=-1, keepdims=True) + EPS)
    vh = vc * r
    return vh * g + b, vh, r


def _ln_bwd(dn, vh, r, g):
    dvh = dn * g
    return r * (dvh - jnp.mean(dvh, axis=-1, keepdims=True) - vh * jnp.mean(dvh * vh, axis=-1, keepdims=True))


def _sublane_shifts(sh_ref, rows):
    for b in range(1, 8):
        sh_ref[b, 0:rows - 8, :] = sh_ref[0, pl.ds(b, rows - 8), :]


def _tap(sh_ref, off, n):
    return sh_ref[off % 8, pl.ds(off - off % 8, n), :]


def _tril_mask():
    return lax.broadcasted_iota(jnp.int32, (SG_CHUNK, SG_CHUNK), 0) >= lax.broadcasted_iota(jnp.int32, (SG_CHUNK, SG_CHUNK), 1)


def _attn_probs(qs, kh, sink_col, first_ok):
    s = _dot_nt(qs, kh) * (HEAD_DIM ** -0.5)
    row = lax.broadcasted_iota(jnp.int32, s.shape, 0) % WINDOW
    col = lax.broadcasted_iota(jnp.int32, s.shape, 1)
    valid = (col > row) & (col <= row + WINDOW) & ((col >= WINDOW) | first_ok)
    s = jnp.where(valid, s, NEG_BIG)
    m = jnp.maximum(jnp.max(s, axis=-1, keepdims=True), sink_col)
    p = jnp.where(valid, jnp.exp(s - m), 0.0)
    es = jnp.exp(sink_col - m)
    inv = 1.0 / (jnp.sum(p, axis=-1, keepdims=True) + es)
    return p * inv, es * inv


def _sink_col(sinks_ref, h):
    return jnp.concatenate([jnp.broadcast_to(sinks_ref[:, h * Q_PER_KV + g:h * Q_PER_KV + g + 1], (WINDOW, 1))
                            for g in range(Q_PER_KV)], axis=0)


def _mixer_in_specs(TB, nb):
    r = TB // HALO
    last = nb * r - 1
    cur = pl.BlockSpec((TB, MIX_W), lambda i: (i, 1))
    prev = pl.BlockSpec((HALO, MIX_W), lambda i: (jnp.maximum(i * r - 1, 0), 1))
    nxt = pl.BlockSpec((HALO, MIX_W), lambda i: (jnp.minimum((i + 1) * r, last), 1))
    tcur = pl.BlockSpec((TB, 128), lambda i: (i, 0))
    tprev = pl.BlockSpec((HALO, 128), lambda i: (jnp.maximum(i * r - 1, 0), 0))
    tnxt = pl.BlockSpec((HALO, 128), lambda i: (jnp.minimum((i + 1) * r, last), 0))
    return cur, prev, nxt, tcur, tprev, tnxt


def _mixer_param_specs():
    return [_full((1, HALF)), _full((1, HALF)), _full((SG_GROUPS, SG_CHUNK, SG_CHUNK)), _full((SG_CHUNK, 128)),
            _full((32, HALF)), _full((1, HALF)), _full((1, HALF)), _full((1, HALF)), _full((1, 128)), _full((8, HALF))]


def _mixers_fwd(proj, cos_t, sin_t, mp, TB, name):
    T = proj.shape[0]
    nb = T // TB
    r = TB // HALO
    cur, prev, _, tcur, tprev, _ = _mixer_in_specs(TB, nb)

    def body(zc_ref, zp_ref, cc_ref, sc_ref, cp_ref, sp_ref,
             lg_ref, lb_ref, sgw_ref, sgb_ref, cvw_ref, cvb_ref, cvg_ref, cvbb_ref, sinks_ref, scw_ref,
             ys_ref, scr_ref, k_ref, v_ref, sh_ref):
        i = pl.program_id(0)
        pm = (i > 0).astype(F32)

        def colsE(c0, c1):
            return jnp.concatenate([zp_ref[:, c0:c1].astype(F32) * pm, zc_ref[:, c0:c1].astype(F32)], axis=0)

        a = _gelu(zc_ref[:, C_ZA:C_ZA + 2 * HALF].astype(F32))
        u = a[:, :HALF]
        vn, _, _ = _ln_fwd(a[:, HALF:], lg_ref[...], lb_ref[...])
        vnb = vn.astype(BF16)
        tril = _tril_mask()
        for g in range(SG_GROUPS):
            wt = jnp.where(tril, sgw_ref[g], 0.0).astype(BF16)
            for ci in range(r):
                rows = slice(ci * SG_CHUNK, (ci + 1) * SG_CHUNK)
                cols = slice(g * 128, (g + 1) * 128)
                mixed = _dot(wt, vnb[rows, cols]) + sgb_ref[:, g:g + 1]
                ys_ref[rows, g * 128:(g + 1) * 128] = (u[rows, cols] * mixed).astype(BF16)

        def colsB(c0, c1):
            return jnp.concatenate([zp_ref[HALO - CV_PAD:, c0:c1].astype(F32) * pm, zc_ref[:, c0:c1].astype(F32)], axis=0)

        sh_ref[0] = colsB(C_ZB, C_ZB + HALF) * _sig(colsB(C_ZB + HALF, C_ZB + 2 * HALF))
        _sublane_shifts(sh_ref, TB + CV_PAD)
        c = jnp.broadcast_to(cvb_ref[...], (TB, HALF))
        for k in range(CV_KERNEL):
            c = c + cvw_ref[k:k + 1, :] * _tap(sh_ref, CV_PAD - (CV_KERNEL - 1) + k, TB)
        n, _, _ = _ln_fwd(c, cvg_ref[...], cvbb_ref[...])
        ys_ref[:, HALF:2 * HALF] = (n * _sig(n)).astype(BF16)

        zd = colsE(C_ZD + HALF, C_ZD + 3 * HALF)
        scr_ref[...] = zd[:, :HALF] * zd[:, HALF:]
        cv = None
        for k in range(SC_KERNEL):
            t = scw_ref[k:k + 1, :] * scr_ref[pl.ds(HALO - (SC_KERNEL - 1) + k, TB), :]
            cv = t if cv is None else cv + t
        ys_ref[:, 3 * HALF:4 * HALF] = (zc_ref[:, C_ZD:C_ZD + HALF].astype(F32) * cv).astype(BF16)

        cosE = jnp.concatenate([cp_ref[...], cc_ref[...]], axis=0)
        sinE = jnp.concatenate([sp_ref[...], sc_ref[...]], axis=0)
        k_ref[...] = _rope(colsE(C_K, C_K + 128), cosE, sinE).astype(BF16)
        v_ref[...] = colsE(C_V, C_V + 128).astype(BF16)
        cosC, sinC = cc_ref[...], sc_ref[...]
        q = jnp.concatenate([_rope(zc_ref[:, C_Q + 128 * j:C_Q + 128 * (j + 1)].astype(F32), cosC, sinC)
                             for j in range(4)], axis=1).astype(BF16)
        for qb in range(r):
            first_ok = (i * r + qb) > 0
            for h in range(N_KV_HEADS):
                hc = slice(h * HEAD_DIM, (h + 1) * HEAD_DIM)
                kh = k_ref[qb * WINDOW:qb * WINDOW + 2 * WINDOW, hc]
                vh = v_ref[qb * WINDOW:qb * WINDOW + 2 * WINDOW, hc]
                qs = jnp.concatenate([q[qb * WINDOW:(qb + 1) * WINDOW, (h * Q_PER_KV + g) * HEAD_DIM:(h * Q_PER_KV + g + 1) * HEAD_DIM]
                                      for g in range(Q_PER_KV)], axis=0)
                probs, _ = _attn_probs(qs, kh, _sink_col(sinks_ref, h), first_ok)
                o = _dot(probs.astype(BF16), vh)
                for g in range(Q_PER_KV):
                    c0 = 2 * HALF + (h * Q_PER_KV + g) * HEAD_DIM
                    ys_ref[qb * WINDOW:(qb + 1) * WINDOW, c0:c0 + HEAD_DIM] = o[g * WINDOW:(g + 1) * WINDOW].astype(BF16)

    return pl.pallas_call(
        body, name=name, grid=(nb,),
        in_specs=[cur, prev, tcur, tcur, tprev, tprev] + _mixer_param_specs(),
        out_specs=pl.BlockSpec((TB, 4 * HALF), lambda i: (i, 0)),
        out_shape=jax.ShapeDtypeStruct((T, 4 * HALF), BF16),
        scratch_shapes=[pltpu.VMEM((TB + HALO, HALF), F32), pltpu.VMEM((TB + HALO, 128), BF16), pltpu.VMEM((TB + HALO, 128), BF16),
                        pltpu.VMEM((8, TB + CV_PAD, HALF), F32)],
        compiler_params=_params("parallel"),
    )(proj, proj, cos_t, sin_t, cos_t, sin_t, *mp)


def _mixers_bwd(proj, dys, dproj, cos_t, sin_t, mp, TB, name):
    T = proj.shape[0]
    nb = T // TB
    r = TB // HALO
    RE = TB + 2 * HALO
    RC = TB + HALO
    cur, prev, nxt, tcur, tprev, tnxt = _mixer_in_specs(TB, nb)
    dcur = pl.BlockSpec((TB, 4 * HALF), lambda i: (i, 0))
    dnxt = pl.BlockSpec((HALO, 4 * HALF), lambda i: (jnp.minimum((i + 1) * r, nb * r - 1), 0))

    def body(zc_ref, zp_ref, zn_ref, dyc_ref, dyn_ref, cc_ref, sc_ref, cp_ref, sp_ref, cn_ref, sn_ref,
             lg_ref, lb_ref, sgw_ref, sgb_ref, cvw_ref, cvb_ref, cvg_ref, cvbb_ref, sinks_ref, scw_ref, dp_in_ref,
             dz_ref, dlg_ref, dlb_ref, dsgw_ref, dsgb_ref, dcvw_ref, dcvb_ref, dcvg_ref, dcvbb_ref, dsink_ref, dscw_ref,
             scr_ref, scr2_ref, k_ref, v_ref, dk_ref, dv_ref, dq_ref, sh_ref, sh2_ref):
        del dp_in_ref
        i = pl.program_id(0)
        pm = (i > 0).astype(F32)
        nm = (i < nb - 1).astype(F32)

        @pl.when(i == 0)
        def _():
            for ref in (dlg_ref, dlb_ref, dsgw_ref, dsgb_ref, dcvw_ref, dcvb_ref, dcvg_ref, dcvbb_ref, dsink_ref, dscw_ref):
                ref[...] = jnp.zeros_like(ref)

        def colsE(c0, c1):
            return jnp.concatenate([zp_ref[:, c0:c1].astype(F32) * pm, zc_ref[:, c0:c1].astype(F32),
                                    zn_ref[:, c0:c1].astype(F32)], axis=0)

        def colsC(c0, c1):
            return jnp.concatenate([zc_ref[:, c0:c1].astype(F32), zn_ref[:, c0:c1].astype(F32)], axis=0)

        def dyC(c0, c1):
            return jnp.concatenate([dyc_ref[:, c0:c1].astype(F32), dyn_ref[:, c0:c1].astype(F32) * nm], axis=0)

        za = zc_ref[:, C_ZA:C_ZA + 2 * HALF].astype(F32)
        a = _gelu(za)
        u = a[:, :HALF]
        lg = lg_ref[...]
        vn, vh, rs = _ln_fwd(a[:, HALF:], lg, lb_ref[...])
        vnb = vn.astype(BF16)
        dya = dyc_ref[:, 0:HALF].astype(F32)
        tril = _tril_mask()
        lane128 = lax.broadcasted_iota(jnp.int32, (SG_CHUNK, 128), 1)
        du_parts, dvn_parts = [], []
        for ci in range(r):
            rows = slice(ci * SG_CHUNK, (ci + 1) * SG_CHUNK)
            du_g, dvn_g = [], []
            for g in range(SG_GROUPS):
                cols = slice(g * 128, (g + 1) * 128)
                wt = jnp.where(tril, sgw_ref[g], 0.0).astype(BF16)
                vb = vnb[rows, cols]
                mixed = _dot(wt, vb) + sgb_ref[:, g:g + 1]
                dy_blk = dya[rows, cols]
                du_g.append(dy_blk * mixed)
                dmix = dy_blk * u[rows, cols]
                dmb = dmix.astype(BF16)
                dvn_g.append(_dot_tn(wt, dmb))
                dsgw_ref[g] += jnp.where(tril, _dot_nt(dmb, vb), 0.0)
                dsgb_ref[...] += jnp.where(lane128 == g, jnp.sum(dmix, axis=1, keepdims=True), 0.0)
            du_parts.append(jnp.concatenate(du_g, axis=1))
            dvn_parts.append(jnp.concatenate(dvn_g, axis=1))
        du = jnp.concatenate(du_parts, axis=0) if r > 1 else du_parts[0]
        dvn = jnp.concatenate(dvn_parts, axis=0) if r > 1 else dvn_parts[0]
        dlg_ref[...] += jnp.sum(dvn * vh, axis=0, keepdims=True)
        dlb_ref[...] += jnp.sum(dvn, axis=0, keepdims=True)
        dvv = _ln_bwd(dvn, vh, rs, lg)
        gg = _gelu_grad(za)
        dz_ref[:, C_ZA:C_ZA + HALF] = (du * gg[:, :HALF]).astype(BF16)
        dz_ref[:, C_ZA + HALF:C_ZA + 2 * HALF] = (dvv * gg[:, HALF:]).astype(BF16)

        RB = TB + CV_PAD

        def colsB(c0, c1):
            return jnp.concatenate([zp_ref[HALO - CV_PAD:, c0:c1].astype(F32) * pm, zc_ref[:, c0:c1].astype(F32),
                                    zn_ref[:CV_PAD, c0:c1].astype(F32)], axis=0)

        sh_ref[0] = colsB(C_ZB, C_ZB + HALF) * _sig(colsB(C_ZB + HALF, C_ZB + 2 * HALF))
        _sublane_shifts(sh_ref, RB + CV_PAD)
        c = jnp.broadcast_to(cvb_ref[...], (RB, HALF))
        for k in range(CV_KERNEL):
            c = c + cvw_ref[k:k + 1, :] * _tap(sh_ref, CV_PAD - (CV_KERNEL - 1) + k, RB)
        cvg = cvg_ref[...]
        n, ch, rc = _ln_fwd(c, cvg, cvbb_ref[...])
        sn = _sig(n)
        dyb = jnp.concatenate([dyc_ref[:, HALF:2 * HALF].astype(F32), dyn_ref[:CV_PAD, HALF:2 * HALF].astype(F32) * nm], axis=0)
        dn = dyb * (sn + n * sn * (1.0 - sn))
        dno = dn[:TB]
        dcvg_ref[...] += jnp.sum(dno * ch[:TB], axis=0, keepdims=True)
        dcvbb_ref[...] += jnp.sum(dno, axis=0, keepdims=True)
        dc = _ln_bwd(dn, ch, rc, cvg)
        sh2_ref[0] = dc
        _sublane_shifts(sh2_ref, RB)
        dcvb_ref[...] += jnp.sum(dc[:TB], axis=0, keepdims=True)
        dy0 = None
        for k in range(CV_KERNEL):
            wk = cvw_ref[k:k + 1, :]
            t = wk * _tap(sh2_ref, CV_KERNEL - 1 - k, TB)
            dy0 = t if dy0 is None else dy0 + t
            dcvw_ref[k:k + 1, :] += jnp.sum(dc[:TB] * _tap(sh_ref, CV_PAD - (CV_KERNEL - 1) + k, TB), axis=0, keepdims=True)
        ab = zc_ref[:, C_ZB:C_ZB + HALF].astype(F32)
        sg = _sig(zc_ref[:, C_ZB + HALF:C_ZB + 2 * HALF].astype(F32))
        dz_ref[:, C_ZB:C_ZB + HALF] = (dy0 * sg).astype(BF16)
        dz_ref[:, C_ZB + HALF:C_ZB + 2 * HALF] = (dy0 * ab * sg * (1.0 - sg)).astype(BF16)

        zd = colsE(C_ZD + HALF, C_ZD + 3 * HALF)
        scr_ref[...] = zd[:, :HALF] * zd[:, HALF:]
        dcv = dyC(3 * HALF, 4 * HALF) * colsC(C_ZD, C_ZD + HALF)
        scr2_ref[...] = dcv
        cv = None
        dud = None
        for k in range(SC_KERNEL):
            wk = scw_ref[k:k + 1, :]
            us = scr_ref[pl.ds(HALO - (SC_KERNEL - 1) + k, TB), :]
            t = wk * us
            cv = t if cv is None else cv + t
            t2 = wk * scr2_ref[pl.ds(SC_KERNEL - 1 - k, TB), :]
            dud = t2 if dud is None else dud + t2
            dscw_ref[k:k + 1, :] += jnp.sum(dcv[:TB] * us, axis=0, keepdims=True)
        dz_ref[:, C_ZD:C_ZD + HALF] = (dyc_ref[:, 3 * HALF:4 * HALF].astype(F32) * cv).astype(BF16)
        dz_ref[:, C_ZD + HALF:C_ZD + 2 * HALF] = (dud * zc_ref[:, C_ZD + 2 * HALF:C_ZD + 3 * HALF].astype(F32)).astype(BF16)
        dz_ref[:, C_ZD + 2 * HALF:C_ZD + 3 * HALF] = (dud * zc_ref[:, C_ZD + HALF:C_ZD + 2 * HALF].astype(F32)).astype(BF16)

        cosE = jnp.concatenate([cp_ref[...], cc_ref[...], cn_ref[...]], axis=0)
        sinE = jnp.concatenate([sp_ref[...], sc_ref[...], sn_ref[...]], axis=0)
        k_ref[...] = _rope(colsE(C_K, C_K + 128), cosE, sinE).astype(BF16)
        v_ref[...] = colsE(C_V, C_V + 128).astype(BF16)
        dk_ref[...] = jnp.zeros_like(dk_ref)
        dv_ref[...] = jnp.zeros_like(dv_ref)
        q = jnp.concatenate([_rope(colsC(C_Q + 128 * j, C_Q + 128 * (j + 1)), cosE[HALO:], sinE[HALO:])
                             for j in range(4)], axis=1).astype(BF16)
        dO = dyC(2 * HALF, 3 * HALF).astype(BF16)
        lane_s = lax.broadcasted_iota(jnp.int32, (1, 128), 1)
        for qb in range(r + 1):
            first_ok = (i * r + qb) > 0
            rows = slice(qb * WINDOW, (qb + 1) * WINDOW)
            band = slice(qb * WINDOW, qb * WINDOW + 2 * WINDOW)
            for h in range(N_KV_HEADS):
                hc = slice(h * HEAD_DIM, (h + 1) * HEAD_DIM)
                kh = k_ref[band, hc]
                vh_ = v_ref[band, hc]
                heads = [slice((h * Q_PER_KV + g) * HEAD_DIM, (h * Q_PER_KV + g + 1) * HEAD_DIM) for g in range(Q_PER_KV)]
                qs = jnp.concatenate([q[rows, hs] for hs in heads], axis=0)
                dos = jnp.concatenate([dO[rows, hs] for hs in heads], axis=0)
                probs, p_sink = _attn_probs(qs, kh, _sink_col(sinks_ref, h), first_ok)
                dP = _dot_nt(dos, vh_)
                rsum = jnp.sum(probs * dP, axis=-1, keepdims=True)
                dS = (probs * (dP - rsum) * (HEAD_DIM ** -0.5)).astype(BF16)
                dk_ref[band, hc] += _dot_tn(dS, qs)
                dv_ref[band, hc] += _dot_tn(probs.astype(BF16), dos)
                if qb < r:
                    dqs = _dot(dS, kh)
                    dsk = -p_sink * rsum
                    for g in range(Q_PER_KV):
                        dq_ref[rows, heads[g]] = dqs[g * WINDOW:(g + 1) * WINDOW]
                        dsink_ref[...] += jnp.where(lane_s == h * Q_PER_KV + g, jnp.sum(dsk[g * WINDOW:(g + 1) * WINDOW]), 0.0)
        cosC, sinC = cc_ref[...], sc_ref[...]
        for j in range(4):
            dz_ref[:, C_Q + 128 * j:C_Q + 128 * (j + 1)] = _rope_t(dq_ref[:, 128 * j:128 * (j + 1)], cosC, sinC).astype(BF16)
        dz_ref[:, C_K:C_K + 128] = _rope_t(dk_ref[HALO:HALO + TB, :], cosC, sinC).astype(BF16)
        dz_ref[:, C_V:C_V + 128] = dv_ref[HALO:HALO + TB, :].astype(BF16)

    small = [((1, HALF), F32), ((1, HALF), F32), ((SG_GROUPS, SG_CHUNK, SG_CHUNK), F32), ((SG_CHUNK, 128), F32),
             ((32, HALF), F32), ((1, HALF), F32), ((1, HALF), F32), ((1, HALF), F32), ((1, 128), F32), ((8, HALF), F32)]
    outs = pl.pallas_call(
        body, name=name, grid=(nb,),
        in_specs=[cur, prev, nxt, dcur, dnxt, tcur, tcur, tprev, tprev, tnxt, tnxt] + _mixer_param_specs()
                 + [pl.BlockSpec(memory_space=pl.ANY)],
        out_specs=[pl.BlockSpec((TB, MIX_W), lambda i: (i, 1))] + [_full(s) for s, _ in small],
        out_shape=[jax.ShapeDtypeStruct((T, PROJ_PAD), BF16)] + [jax.ShapeDtypeStruct(s, d) for s, d in small],
        scratch_shapes=[pltpu.VMEM((RE, HALF), F32), pltpu.VMEM((RC, HALF), F32), pltpu.VMEM((RE, 128), BF16), pltpu.VMEM((RE, 128), BF16),
                        pltpu.VMEM((RE, 128), F32), pltpu.VMEM((RE, 128), F32), pltpu.VMEM((TB, HALF), F32),
                        pltpu.VMEM((8, TB + 2 * CV_PAD, HALF), F32), pltpu.VMEM((8, TB + CV_PAD, HALF), F32)],
        input_output_aliases={21: 0},
        compiler_params=_params("arbitrary"),
    )(proj, proj, proj, dys, dys, cos_t, sin_t, cos_t, sin_t, cos_t, sin_t, *mp, dproj)
    return outs


def _rope_tables(T):
    pos = jnp.arange(T, dtype=F32)
    inv_freq = 1.0 / (ROPE_THETA ** (jnp.arange(0, HEAD_DIM, 2, dtype=F32) / HEAD_DIM))
    ang = pos[:, None] * inv_freq[None, :]
    cos, sin = jnp.cos(ang), jnp.sin(ang)
    cos_t = jnp.concatenate([cos, cos, cos, cos], axis=1)
    sin_t = jnp.concatenate([-sin, sin, -sin, sin], axis=1)
    return cos_t, sin_t


def _mixer_params(l, sg_ln_g, sg_ln_b, sg_w, sg_b, cv_w, cv_b, cv_ln_g, cv_ln_b, attn_sinks, sc_w):
    sgb_t = jnp.zeros((SG_CHUNK, 128), F32).at[:, :SG_GROUPS].set(sg_b[l].T)
    cvw = jnp.zeros((32, HALF), F32).at[:CV_KERNEL].set(cv_w[l])
    scw = jnp.zeros((8, HALF), F32).at[:SC_KERNEL].set(sc_w[l])
    sinks = jnp.zeros((1, 128), F32).at[0, :N_Q_HEADS].set(attn_sinks[l])
    return [sg_ln_g[l][None], sg_ln_b[l][None], sg_w[l], sgb_t, cvw, cv_b[l][None], cv_ln_g[l][None], cv_ln_b[l][None], sinks, scw]


def _w_in_layout(w_in_g):
    wmix = jnp.concatenate([w_in_g[0], w_in_g[1], w_in_g[2][:, :MIX_W - 2 * W_IN_SHARD]], axis=1)
    wg = jnp.concatenate([w_in_g[2][:, MIX_W - 2 * W_IN_SHARD:], w_in_g[3]], axis=1)
    return jnp.concatenate([wg, jnp.zeros((D_MODEL, MIX_W - GATE_W), w_in_g.dtype), wmix], axis=1)


def _w_in_unlayout(dw):
    cut = MIX_W - 2 * W_IN_SHARD
    return jnp.stack([dw[:, MIX_W:MIX_W + W_IN_SHARD], dw[:, MIX_W + W_IN_SHARD:MIX_W + 2 * W_IN_SHARD],
                      jnp.concatenate([dw[:, MIX_W + 2 * W_IN_SHARD:], dw[:, :W_IN_SHARD - cut]], axis=1),
                      dw[:, W_IN_SHARD - cut:GATE_W]], axis=0)


def _device_step(x, tgt, norm_mix, norm_ffn, norm_final, mixer_params, w_in_p, wb_g, wo_g, wgu_g, wd_g):
    T = x.shape[0]
    tables = _rope_tables(T)
    saved = []
    for l in range(DEPTH):
        lw = dict(w_in=w_in_p[l], w_branch=wb_g[l], w_out=wo_g[l], w_gate_up=wgu_g[l], w_down=wd_g[l],
                  norm_mix=norm_mix[l][None], norm_ffn=norm_ffn[l][None], mixer=mixer_params[l], after=jnp.zeros((8, 128), F32))
        x, sv = _fwd_layer(l, x, lw, tables)
        saved.append((lw, sv))
    dx, dnf, loss = _final_loss(x, norm_final[None], tgt, 256, "final_loss")
    grads = [None] * DEPTH
    for l in reversed(range(DEPTH)):
        lw, sv = saved[l]
        dxm, g_ffn = _bwd_layer_ffn(l, dx, lw, sv)
        dx, g_mix = _bwd_layer_mix(l, dxm, lw, sv, tables)
        grads[l] = {**g_ffn, **g_mix}
    return loss, dx, dnf[0], grads


MIX_BLOCK = 256


def _fwd_layer(l, x, lw, tables):
    T = x.shape[0]
    TM = min(T, 1024)
    proj, xn = _rms_mm(x, lw["norm_mix"], lw["w_in"], TM, 2176, f"proj{l}")
    ys = _mixers_fwd(proj, *tables, lw["mixer"], MIX_BLOCK, f"mixers_fwd{l}")
    xm, merged = _merge_fwd(x, ys, proj, lw["w_branch"], lw["w_out"], 256, f"merge_fwd{l}")
    gu, hn = _rms_mm(xm, lw["norm_ffn"], lw["w_gate_up"], TM, GU_SHARD, f"ffn_up{l}")
    x_out = _ffn_down(xm, gu, lw["w_down"], 256, f"ffn_down{l}")
    return x_out, (x, proj, xn, ys, xm, merged, gu, hn)


def _bwd_layer_ffn(l, dx, lw, sv):
    x_in, proj, xn, ys, xm, merged, gu, hn = sv
    T = dx.shape[0]
    tkk = min(T, 1024)
    gk = T // tkk
    dgu, act = _swiglu_bwd(dx, gu, lw["w_down"], 256, f"swiglu_bwd{l}", lw["after"])
    d_wd = _mm_tn(act, dx, (2, 1, gk), (tkk, D_FF // 2), lambda i, j, k: (k, i), (tkk, D_MODEL), lambda i, j, k: (k, 0),
                  (D_FF, D_MODEL), (D_FF // 2, D_MODEL), lambda i, j, k: (i, 0), f"dw_down{l}")
    d_wgu = _mm_tn(hn, dgu, (1, N_CHIPS, gk), (tkk, D_MODEL), lambda i, j, k: (k, 0), (tkk, GU_SHARD), lambda i, j, k: (k, j),
                   (N_CHIPS, D_MODEL, GU_SHARD), (None, D_MODEL, GU_SHARD), lambda i, j, k: (j, 0, 0), f"dw_gate_up{l}")
    dxm, d_nffn = _mm_nt_rmsbwd(dgu, lw["w_gate_up"], xm, lw["norm_ffn"], dx, min(T, 512), GU_SHARD, f"ffn_up_bwd{l}")
    dys, dbr, dproj = _merge_bwd(dxm, ys, proj, lw["w_branch"], lw["w_out"], 256, f"merge_bwd{l}")
    d_wo = _mm_tn(merged, dxm, (2, 1, gk), (tkk, 512), lambda i, j, k: (k, i), (tkk, D_MODEL), lambda i, j, k: (k, 0),
                  (D_MODEL, D_MODEL), (512, D_MODEL), lambda i, j, k: (i, 0), f"dw_out{l}")
    d_wb = _mm_tn(ys, dbr, (N_BRANCH, N_CHIPS, gk), (tkk, HALF), lambda i, j, k: (k, i), (tkk, 256), lambda i, j, k: (k, i * N_CHIPS + j),
                  (N_CHIPS, N_BRANCH, HALF, 256), (None, None, HALF, 256), lambda i, j, k: (j, i, 0, 0), f"dw_branch{l}")
    return (dxm, dys, dproj), dict(w_branch=d_wb, w_out=d_wo, w_gate_up=d_wgu, w_down=d_wd, norm_ffn=d_nffn[0])


def _bwd_layer_mix(l, carry, lw, sv, tables):
    dxm, dys, dproj = carry
    x_in, proj, xn, ys, xm, merged, gu, hn = sv
    T = dxm.shape[0]
    tkk = min(T, 1024)
    gk = T // tkk
    mb = _mixers_bwd(proj, dys, dproj, *tables, lw["mixer"], MIX_BLOCK, f"mixers_bwd{l}")
    dproj = mb[0]
    d_win = _mm_tn(xn, dproj, (1, PROJ_PAD // 2176, gk), (tkk, D_MODEL), lambda i, j, k: (k, 0), (tkk, 2176), lambda i, j, k: (k, j),
                   (D_MODEL, PROJ_PAD), (D_MODEL, 2176), lambda i, j, k: (0, j), f"dw_in{l}")
    dx, d_nmix = _mm_nt_rmsbwd(dproj, lw["w_in"], x_in, lw["norm_mix"], dxm, min(T, 512), 2176, f"proj_bwd{l}")
    return dx, dict(w_in=d_win, norm_mix=d_nmix[0],
                    sg_ln_g=mb[1][0], sg_ln_b=mb[2][0], sg_w=mb[3], sg_b=mb[4][:, :SG_GROUPS].T,
                    cv_w=mb[5][:CV_KERNEL], cv_b=mb[6][0], cv_ln_g=mb[7][0], cv_ln_b=mb[8][0],
                    attn_sinks=mb[9][0, :N_Q_HEADS], sc_w=mb[10][:SC_KERNEL])


ANY = pl.BlockSpec(memory_space=pl.ANY)
BIG = ("w_in", "w_branch", "w_out", "w_gate_up", "w_down")
HALF_SHAPE = {"w_in": (2, 512, W_IN_SHARD), "w_branch": (2, 1024, 256), "w_out": (2, 128, D_MODEL),
              "w_gate_up": (2, 512, GU_SHARD), "w_down": (2, 352, D_MODEL)}
NB = len(BIG)


def _place():
    x, y, c = lax.axis_index("x"), lax.axis_index("y"), lax.axis_index("c")
    chips = [(1 - x, y), (x, 1 - y), (1 - x, 1 - y)]
    return x, y, c, 2 * x + y, chips, [2 * px + py for px, py in chips]


def _remote(src, dst, ssem, rsem, dev):
    return pltpu.make_async_remote_copy(src_ref=src, dst_ref=dst, send_sem=ssem, recv_sem=rsem, device_id=dev, device_id_type=MESH)


HBM_SPEC = pl.BlockSpec(memory_space=pltpu.HBM)
SEM_SPEC = pl.BlockSpec(memory_space=pltpu.SEMAPHORE)
DATAFLOW = pltpu.SideEffectType.DATAFLOW_SIDE_EFFECTING


def _ici_ends(kind, src, land, j, c, chip, chip_ids):
    if kind == "gather":
        return src.at[c], land.at[chip, c], land.at[chip_ids[j], c]
    return src.at[chip_ids[j]], land.at[chip], land.at[chip_ids[j]]


def _ici_start(kind, srcs, land_shapes, name):
    n = len(srcs)

    def body(*refs):
        src, land = refs[:n], refs[n:2 * n]
        ssem, rsem, token = refs[2 * n], refs[2 * n + 1], refs[-1]
        x, y, c, chip, chips, chip_ids = _place()
        for k in range(n):
            for j in range(3):
                s, d, _ = _ici_ends(kind, src[k], land[k], j, c, chip, chip_ids)
                _remote(s, d, ssem.at[3 * k + j], rsem.at[3 * k + j], (*chips[j], c)).start()
        token[...] = jnp.zeros_like(token)

    sem = pltpu.SemaphoreType.DMA((3 * n,))
    outs = pl.pallas_call(
        body, name=name,
        out_shape=(sem, sem, *[pltpu.HBM(s.shape, s.dtype) for s in srcs], *[pltpu.HBM(sh, BF16) for sh in land_shapes],
                   jax.ShapeDtypeStruct((8, 128), F32)),
        in_specs=[HBM_SPEC] * (2 * n),
        out_specs=(SEM_SPEC, SEM_SPEC, *[HBM_SPEC] * (2 * n), pl.BlockSpec(memory_space=pltpu.VMEM)),
        input_output_aliases={i: 2 + i for i in range(2 * n)},
        compiler_params=pltpu.CompilerParams(has_side_effects=DATAFLOW),
    )(*[pltpu.with_memory_space_constraint(s, pltpu.HBM) for s in srcs],
      *[pltpu.with_memory_space_constraint(lax.empty(sh, BF16), pltpu.HBM) for sh in land_shapes])
    return (kind, outs[0], outs[1], list(outs[2:2 + n]), list(outs[2 + n:2 + 2 * n])), outs[-1]


def _ici_wait(handle, after, name):
    kind, ssem_in, rsem_in, srcs, lands = handle
    n = len(srcs)

    def body(*refs):
        src, land = refs[:n], refs[n:2 * n]
        ssem, rsem = refs[2 * n], refs[2 * n + 1]
        x, y, c, chip, chips, chip_ids = _place()
        for k in range(n):
            for j in range(3):
                s, _, mine = _ici_ends(kind, src[k], land[k], j, c, chip, chip_ids)
                cp = _remote(s, mine, ssem.at[3 * k + j], rsem.at[3 * k + j], (*chips[j], c))
                cp.wait_send()
                cp.wait_recv()

    outs = pl.pallas_call(
        body, name=name, out_shape=[pltpu.HBM(t.shape, t.dtype) for t in srcs + lands],
        in_specs=[HBM_SPEC] * (2 * n) + [SEM_SPEC, SEM_SPEC, ANY], out_specs=[HBM_SPEC] * (2 * n),
        input_output_aliases={i: i for i in range(2 * n)},
        compiler_params=pltpu.CompilerParams(has_side_effects=DATAFLOW),
    )(*srcs, *lands, ssem_in, rsem_in, after)
    return list(outs[:n]), list(outs[n:])


def _ag_pair(shards, lands, name):
    n = len(shards)

    def body(*refs):
        ins, outs = refs[:n], refs[2 * n:3 * n]
        token = refs[3 * n]
        s_fwd, r_fwd, s_own, r_own = refs[3 * n + 1:]
        x, y, c, chip, chips, chip_ids = _place()
        sib = (x, y, 1 - c)
        cps = []
        for k in range(n):
            cp = _remote(ins[k], outs[k].at[chip], s_own.at[k], r_own.at[k], sib)
            cp.start()
            cps.append(cp)
            for j in range(3):
                got = outs[k].at[chip_ids[j], c]
                cp = _remote(got, got, s_fwd.at[k, j], r_fwd.at[k, j], sib)
                cp.start()
                cps.append(cp)
        for k in range(n):
            _remote(ins[k], outs[k].at[chip], s_own.at[k], r_own.at[k], sib).wait_recv()
            for j in range(3):
                got = outs[k].at[chip_ids[j], 1 - c]
                _remote(got, got, s_fwd.at[k, j], r_fwd.at[k, j], sib).wait_recv()
        for cp in cps:
            cp.wait_send()
        token[...] = jnp.zeros_like(token)

    sem, sem1 = pltpu.SemaphoreType.DMA((n, 3)), pltpu.SemaphoreType.DMA((n,))
    outs = pl.pallas_call(
        body, name=name, out_shape=[jax.ShapeDtypeStruct(t.shape, t.dtype) for t in lands] + [jax.ShapeDtypeStruct((8, 128), F32)],
        in_specs=[ANY] * (2 * n), out_specs=[ANY] * n + [pl.BlockSpec(memory_space=pltpu.VMEM)],
        input_output_aliases={n + k: k for k in range(n)},
        scratch_shapes=[sem, sem, sem1, sem1], compiler_params=pltpu.CompilerParams(has_side_effects=True),
    )(*shards, *lands)
    return list(outs[:n]), outs[n]


def _rs_pair(grads, name):
    n_arr = len(grads)

    def body(*refs):
        ins, got = refs[:n_arr], refs[n_arr:2 * n_arr]
        ssem, rsem = refs[2 * n_arr:]
        x, y, c, _, _, _ = _place()
        sib = (x, y, 1 - c)
        sends = []
        for k in reversed(range(n_arr)):
            for q in range(N_CHIPS):
                cp = _remote(ins[k].at[q, 1 - c], got[k].at[q], ssem.at[k, q], rsem.at[k, q], sib)
                cp.start()
                sends.append(cp)
        for k in range(n_arr):
            for q in range(N_CHIPS):
                _remote(got[k].at[q], got[k].at[q], ssem.at[k, q], rsem.at[k, q], sib).wait_recv()
        for cp in sends:
            cp.wait_send()

    shp = [jax.ShapeDtypeStruct((N_CHIPS,) + g.shape[2:], BF16) for g in grads]
    sem = pltpu.SemaphoreType.DMA((n_arr, N_CHIPS))
    outs = pl.pallas_call(
        body, name=name, out_shape=shp, in_specs=[ANY] * n_arr, out_specs=[ANY] * n_arr,
        scratch_shapes=[sem, sem], compiler_params=pltpu.CompilerParams(has_side_effects=True),
    )(*grads)
    return list(outs)


def _rs_share(bufs, name):
    n = len(bufs)

    def body(*refs):
        outs = refs[n:2 * n]
        ssem, rsem = refs[2 * n:]
        x, y, c, _, _, _ = _place()
        sib = (x, y, 1 - c)
        sends = []
        for k in range(n):
            for l in range(DEPTH):
                cp = _remote(outs[k].at[l, c], outs[k].at[l, c], ssem.at[k, l], rsem.at[k, l], sib)
                cp.start()
                sends.append(cp)
        for k in range(n):
            for l in range(DEPTH):
                dst = outs[k].at[l, 1 - c]
                _remote(dst, dst, ssem.at[k, l], rsem.at[k, l], sib).wait_recv()
        for cp in sends:
            cp.wait_send()

    sem = pltpu.SemaphoreType.DMA((n, DEPTH))
    outs = pl.pallas_call(
        body, name=name, out_shape=[jax.ShapeDtypeStruct(b.shape, b.dtype) for b in bufs], in_specs=[ANY] * n, out_specs=[ANY] * n,
        input_output_aliases={k: k for k in range(n)},
        scratch_shapes=[sem, sem], compiler_params=pltpu.CompilerParams(has_side_effects=True),
    )(*bufs)
    return list(outs)


def _all_reduce_small(buf, name):
    R = buf.shape[0]
    offs = [(dx, dy, dc) for dx in (0, 1) for dy in (0, 1) for dc in (0, 1)][1:]

    def body(in_ref, out_ref, gather_ref, ssem, rsem):
        x, y, c = lax.axis_index("x"), lax.axis_index("y"), lax.axis_index("c")
        me = 4 * x + 2 * y + c
        gather_ref[me] = in_ref[...]
        flip = lambda v, d: 1 - v if d else v
        peers = [(flip(x, dx), flip(y, dy), flip(c, dc)) for dx, dy, dc in offs]
        cps = [_remote(in_ref, gather_ref.at[me], ssem.at[k], rsem.at[k], peers[k]) for k in range(N_DEV - 1)]
        for cp in cps:
            cp.start()
        for k, (px, py, pc) in enumerate(peers):
            _remote(in_ref, gather_ref.at[4 * px + 2 * py + pc], ssem.at[k], rsem.at[k], peers[k]).wait_recv()
        acc = gather_ref[0]
        for s in range(1, N_DEV):
            acc = acc + gather_ref[s]
        out_ref[...] = acc
        for cp in cps:
            cp.wait_send()

    vm = pl.BlockSpec(memory_space=pltpu.VMEM)
    return pl.pallas_call(
        body, name=name, out_shape=jax.ShapeDtypeStruct((R, 128), F32), in_specs=[vm], out_specs=vm,
        scratch_shapes=[pltpu.VMEM((N_DEV, R, 128), F32), pltpu.SemaphoreType.DMA((N_DEV - 1,)), pltpu.SemaphoreType.DMA((N_DEV - 1,))],
        compiler_params=pltpu.CompilerParams(vmem_limit_bytes=VMEM_LIMIT),
    )(buf)


def _row_tile(rows, cols, n_arrays):
    budget = 20 * 1024 * 1024 // (n_arrays * 2 * cols * 4)
    tr = rows
    while tr > budget or tr % 16:
        assert tr % 2 == 0, (rows, cols)
        tr //= 2
    return tr


def _add_pairs(g, got, place, name):
    _, _, rows, cols = g.shape
    tr = _row_tile(rows, cols, 3)

    def body(place_ref, a_ref, b_ref, o_ref):
        del place_ref
        o_ref[...] = (a_ref[...].astype(F32) + b_ref[...].astype(F32)).astype(BF16)

    spec = pl.BlockSpec((None, tr, cols), lambda q, i, p: (q, i, 0))
    grid_spec = pltpu.PrefetchScalarGridSpec(
        num_scalar_prefetch=1, grid=(N_CHIPS, rows // tr),
        in_specs=[pl.BlockSpec((None, None, tr, cols), lambda q, i, p: (q, p[1], i, 0)), spec], out_specs=spec)
    return pl.pallas_call(body, name=name, grid_spec=grid_spec, out_shape=jax.ShapeDtypeStruct((N_CHIPS, rows, cols), BF16),
                          compiler_params=_params("parallel", "parallel"))(place, g, got)


def _sum_chips(own, recv, place, l, buf, name, after):
    _, rows, cols = own.shape
    tr = _row_tile(rows, cols, 4)

    def body(place_ref, own_ref, recv_ref, *rest):
        chip = place_ref[0]
        acc = own_ref[...].astype(F32)
        for j in range(1, N_CHIPS):
            acc = acc + recv_ref[lax.rem(chip + j, N_CHIPS)].astype(F32)
        rest[-1][...] = acc

    in_specs = [pl.BlockSpec((None, tr, cols), lambda i, p: (p[0], i, 0)), pl.BlockSpec((N_CHIPS, tr, cols), lambda i, p: (0, i, 0)), ANY]
    args = [place, own, recv, after]
    aliases = {}
    if buf is not None:
        in_specs.append(ANY)
        args.append(buf)
        aliases = {4: 0}
    grid_spec = pltpu.PrefetchScalarGridSpec(
        num_scalar_prefetch=1, grid=(rows // tr,), in_specs=in_specs,
        out_specs=pl.BlockSpec((None, None, tr, cols), lambda i, p: (l, p[1], i, 0)))
    return pl.pallas_call(body, name=name, grid_spec=grid_spec, out_shape=jax.ShapeDtypeStruct((DEPTH, 2, rows, cols), F32),
                          input_output_aliases=aliases, compiler_params=_params("parallel"))(*args)


def _adamw(w, g, m, v, name):
    shape = w.shape
    cols = shape[-1]
    w2, g2, m2, v2 = (t.reshape(-1, cols) for t in (w, g, m, v))
    rows = w2.shape[0]
    tr = _row_tile(rows, cols, 7)

    def body(w_ref, g_ref, m_ref, v_ref, d_ref, mo_ref, vo_ref):
        gv = g_ref[...]
        mn = ADAM_B1 * m_ref[...] + (1.0 - ADAM_B1) * gv
        vn = ADAM_B2 * v_ref[...] + (1.0 - ADAM_B2) * (gv * gv)
        m_hat = mn / (1.0 - ADAM_B1 ** ADAM_STEP)
        v_hat = vn / (1.0 - ADAM_B2 ** ADAM_STEP)
        d_ref[...] = -ADAM_LR * (m_hat / (jnp.sqrt(v_hat) + ADAM_EPS) + ADAM_WD * w_ref[...])
        mo_ref[...] = mn
        vo_ref[...] = vn

    spec = pl.BlockSpec((tr, cols), lambda i: (i, 0))
    outs = pl.pallas_call(body, name=name, grid=(rows // tr,), in_specs=[spec] * 4, out_specs=[spec] * 3,
                          out_shape=[jax.ShapeDtypeStruct((rows, cols), F32)] * 3, compiler_params=_params("parallel"))(w2, g2, m2, v2)
    return [o.reshape(shape) for o in outs]


def _pack(arrays):
    parts = []
    for t in arrays:
        f = t.reshape(-1)
        parts.append(jnp.pad(f, (0, (-f.shape[0]) % 128)))
    f = jnp.concatenate(parts)
    f = jnp.pad(f, (0, (-f.shape[0]) % 2048))
    return f.reshape(-1, 128)


def _unpack(buf, shapes):
    f = buf.reshape(-1)
    out, off = [], 0
    for s in shapes:
        n = math.prod(s)
        out.append(f[off:off + n].reshape(s))
        off += n + (-n) % 128
    return out


SMALL = ("norm_mix", "sg_ln_g", "sg_ln_b", "sg_w", "sg_b", "cv_w", "cv_b", "cv_ln_g", "cv_ln_b", "attn_sinks", "sc_w", "norm_ffn", "norm_final")
ORDER = ("norm_mix", "w_in", "sg_ln_g", "sg_ln_b", "sg_w", "sg_b", "cv_w", "cv_b", "cv_ln_g", "cv_ln_b", "attn_sinks", "sc_w",
         "w_branch", "w_out", "norm_ffn", "w_gate_up", "w_down", "norm_final")


def kernel(x, norm_mix, w_in, sg_ln_g, sg_ln_b, sg_w, sg_b, cv_w, cv_b, cv_ln_g, cv_ln_b, attn_sinks, sc_w, w_branch, w_out, norm_ffn, w_gate_up, w_down, norm_final, loss_target, m_norm_mix, m_w_in, m_sg_ln_g, m_sg_ln_b, m_sg_w, m_sg_b, m_cv_w, m_cv_b, m_cv_ln_g, m_cv_ln_b, m_attn_sinks, m_sc_w, m_w_branch, m_w_out, m_norm_ffn, m_w_gate_up, m_w_down, m_norm_final, v_norm_mix, v_w_in, v_sg_ln_g, v_sg_ln_b, v_sg_w, v_sg_b, v_cv_w, v_cv_b, v_cv_ln_g, v_cv_ln_b, v_attn_sinks, v_sc_w, v_w_branch, v_w_out, v_norm_ffn, v_w_gate_up, v_w_down, v_norm_final):
    W = dict(norm_mix=norm_mix, w_in=w_in, sg_ln_g=sg_ln_g, sg_ln_b=sg_ln_b, sg_w=sg_w, sg_b=sg_b, cv_w=cv_w, cv_b=cv_b, cv_ln_g=cv_ln_g,
             cv_ln_b=cv_ln_b, attn_sinks=attn_sinks, sc_w=sc_w, w_branch=w_branch, w_out=w_out, norm_ffn=norm_ffn, w_gate_up=w_gate_up,
             w_down=w_down, norm_final=norm_final)
    M = dict(norm_mix=m_norm_mix, w_in=m_w_in, sg_ln_g=m_sg_ln_g, sg_ln_b=m_sg_ln_b, sg_w=m_sg_w, sg_b=m_sg_b, cv_w=m_cv_w, cv_b=m_cv_b,
             cv_ln_g=m_cv_ln_g, cv_ln_b=m_cv_ln_b, attn_sinks=m_attn_sinks, sc_w=m_sc_w, w_branch=m_w_branch, w_out=m_w_out,
             norm_ffn=m_norm_ffn, w_gate_up=m_w_gate_up, w_down=m_w_down, norm_final=m_norm_final)
    V = dict(norm_mix=v_norm_mix, w_in=v_w_in, sg_ln_g=v_sg_ln_g, sg_ln_b=v_sg_ln_b, sg_w=v_sg_w, sg_b=v_sg_b, cv_w=v_cv_w, cv_b=v_cv_b,
             cv_ln_g=v_cv_ln_g, cv_ln_b=v_cv_ln_b, attn_sinks=v_attn_sinks, sc_w=v_sc_w, w_branch=v_w_branch, w_out=v_w_out,
             norm_ffn=v_norm_ffn, w_gate_up=v_w_gate_up, w_down=v_w_down, norm_final=v_norm_final)
    mx, my, mc = lax.axis_index("x"), lax.axis_index("y"), lax.axis_index("c")
    chip = 2 * mx + my

    place = jnp.stack([chip, mc]).astype(jnp.int32)
    tables = _rope_tables(x.shape[1])
    land_shapes = [(N_CHIPS,) + HALF_SHAPE[n] for n in BIG]
    part_shapes = {n: (N_CHIPS,) + HALF_SHAPE[n][1:] for n in BIG}

    def shards_of(l, tok):
        return [(W[n][l] + tok[0, 0]).astype(BF16).reshape(HALF_SHAPE[n]) for n in BIG]

    def layer_weights(l, gathered, cvw, scw, tok):
        G = dict(zip(BIG, gathered))
        return dict(w_in=_w_in_layout(G["w_in"].reshape(N_CHIPS, D_MODEL, W_IN_SHARD)), w_branch=G["w_branch"].reshape(N_CHIPS, N_BRANCH, HALF, 256),
                    w_out=G["w_out"].reshape(D_MODEL, D_MODEL), w_gate_up=G["w_gate_up"].reshape(N_CHIPS, D_MODEL, GU_SHARD),
                    w_down=G["w_down"].reshape(D_FF, D_MODEL), norm_mix=norm_mix[l][None] + tok[0, 0], norm_ffn=norm_ffn[l][None],
                    mixer=_mixer_params(l, sg_ln_g, sg_ln_b, sg_w, sg_b, cvw, cv_b, cv_ln_g, cv_ln_b, attn_sinks, scw))

    def shard_major(g):
        t = dict(g)
        if "w_in" in t:
            t["w_in"] = _w_in_unlayout(t["w_in"])
        return {n: t[n].reshape((N_CHIPS,) + HALF_SHAPE[n]) for n in BIG if n in t}

    def pair_sums(tag, g):
        names = list(g)
        got = _rs_pair([g[n] for n in names], f"rs_pair{tag}")
        return names, [_add_pairs(g[n], got[k], place, f"rs_add{tag}_{n}") for k, n in enumerate(names)]

    zero_tok = jnp.zeros((8, 128), F32)
    h0, tok = _ici_start("gather", shards_of(0, zero_tok), land_shapes, "ag_start0")
    south = (mc == 0).astype(F32) + tok[0, 0]
    cvw_z = lax.dynamic_update_slice(jnp.zeros((DEPTH, CV_KERNEL, HALF), F32), cv_w * south, (0, 0, chip * 128))
    scw_z = lax.dynamic_update_slice(jnp.zeros((DEPTH, SC_KERNEL, HALF), F32), sc_w * south, (0, 0, chip * 128))
    cvw_full, scw_full = _unpack(_all_reduce_small(_pack([cvw_z, scw_z]), "ag_small"), [cvw_z.shape, scw_z.shape])
    srcs, lands = _ici_wait(h0, cvw_full, "ag_wait0")
    g0, tok = _ag_pair(srcs, lands, "ag_pair0")

    h1, tok = _ici_start("gather", shards_of(1, tok), land_shapes, "ag_start1")
    lw0 = layer_weights(0, g0, cvw_full, scw_full, tok)
    x1, sv0 = _fwd_layer(0, x[0], lw0, tables)
    srcs, lands = _ici_wait(h1, x1, "ag_wait1")
    g1, tok = _ag_pair(srcs, lands, "ag_pair1")
    lw1 = layer_weights(1, g1, cvw_full, scw_full, tok)
    x2, sv1 = _fwd_layer(1, x1, lw1, tables)
    dx, d_nfinal, loss = _final_loss(x2, norm_final[None], loss_target[0], 256, "final_loss")

    lw1["after"] = zero_tok
    carry, g_ffn1 = _bwd_layer_ffn(1, dx, lw1, sv1)
    dx, g_mix1 = _bwd_layer_mix(1, carry, lw1, sv1, tables)
    names1, part1 = pair_sums("1", shard_major({**g_ffn1, **g_mix1}))
    hr1, tok = _ici_start("scatter", part1, [part_shapes[n] for n in names1], "rs_start1")

    lw0["after"] = tok
    carry, g_ffn0 = _bwd_layer_ffn(0, dx, lw0, sv0)
    names_a, part_a = pair_sums("0a", shard_major(g_ffn0))
    _, recv1 = _ici_wait(hr1, part_a[0], "rs_wait1")
    hra, tok = _ici_start("scatter", part_a, [part_shapes[n] for n in names_a], "rs_start0a")

    lw0["mixer"] = [lw0["mixer"][0] + tok[0, 0]] + lw0["mixer"][1:]
    dx, g_mix0 = _bwd_layer_mix(0, carry, lw0, sv0, tables)
    names_b, part_b = pair_sums("0b", shard_major(g_mix0))
    _, recv_a = _ici_wait(hra, part_b[0], "rs_wait0a")
    hrb, tok = _ici_start("scatter", part_b, [part_shapes[n] for n in names_b], "rs_start0b")

    bufs = {n: _sum_chips(part1[k], recv1[k], place, 1, None, f"rs_sum1_{n}", tok) for k, n in enumerate(names1)}
    for k, n in enumerate(names_a):
        bufs[n] = _sum_chips(part_a[k], recv_a[k], place, 0, bufs[n], f"rs_sum0_{n}", tok)
    shared = dict(zip(names_a, _rs_share([bufs[n] for n in names_a], "rs_share_a")))
    upd = {}
    for n in names_a:
        red = shared[n].reshape(W[n].shape)
        upd[n] = [red] + _adamw(W[n], red, M[n], V[n], f"adamw_{n}")

    grads = [{**g_ffn0, **g_mix0}, {**g_ffn1, **g_mix1}]
    small_g = {n: jnp.stack([grads[l][n] for l in range(DEPTH)]) for n in SMALL if n != "norm_final"}
    small_g["norm_final"] = d_nfinal[0]
    red = _unpack(_all_reduce_small(_pack([small_g[n] for n in SMALL] + [loss]), "ar_small"),
                  [small_g[n].shape for n in SMALL] + [loss.shape])
    small_red = dict(zip(SMALL, red[:-1]))
    loss_out = red[-1][0, 0]
    for n in ("cv_w", "sc_w"):
        small_red[n] = lax.dynamic_slice_in_dim(small_red[n], chip * 128, 128, axis=2)
    sw, sg, sm, sv = (_pack([t[n] for n in SMALL]) for t in (W, small_red, M, V))
    packed = _adamw(sw, sg, sm, sv, "adamw_small")
    shapes = [W[n].shape for n in SMALL]
    for n, d, mo, vo in zip(SMALL, *(_unpack(p, shapes) for p in packed)):
        upd[n] = [small_red[n], d, mo, vo]

    _, recv_b = _ici_wait(hrb, packed[0], "rs_wait0b")
    for k, n in enumerate(names_b):
        bufs[n] = _sum_chips(part_b[k], recv_b[k], place, 0, bufs[n], f"rs_sum0_{n}", tok)
    shared = dict(zip(names_b, _rs_share([bufs[n] for n in names_b], "rs_share_b")))
    for n in names_b:
        red = shared[n].reshape(W[n].shape)
        upd[n] = [red] + _adamw(W[n], red, M[n], V[n], f"adamw_{n}")

    out = [loss_out, dx[None]]
    for k in range(4):
        out += [upd[n][k] for n in ORDER]
    return tuple(out)
```

```python
import functools
import math

import jax
import jax.numpy as jnp
from jax import lax
from jax.experimental import pallas as pl
from jax.experimental.pallas import tpu as pltpu

F32 = jnp.float32
BF16 = jnp.bfloat16

D_MODEL = 1024
DEPTH = 2
HALF = 512
SG_CHUNK = 128
SG_GROUPS = 4
CV_KERNEL = 31
HEAD_DIM = 64
N_Q_HEADS = 8
N_KV_HEADS = 2
Q_PER_KV = N_Q_HEADS // N_KV_HEADS
WINDOW = 128
ROPE_THETA = 10000.0
SC_KERNEL = 3
N_BRANCH = 4
D_FF = 2816
EPS = 1e-6
N_CHIPS = 4
N_DEV = 8

MIX_W = 4352
GATE_W = N_BRANCH * D_MODEL
PROJ_PAD = 2 * MIX_W
W_IN_SHARD = 2112
GU_SHARD = 1408
HALO = 128
CV_PAD = 32

ADAM_LR = 0.001
ADAM_B1 = 0.9
ADAM_B2 = 0.999
ADAM_EPS = 1e-08
ADAM_WD = 0.01
ADAM_STEP = 10

VMEM_LIMIT = 56 * 1024 * 1024
INV_SQRT2 = 1.0 / math.sqrt(2.0)
INV_SQRT_2PI = 1.0 / math.sqrt(2.0 * math.pi)
NEG_BIG = -1e30
MESH = pl.DeviceIdType.MESH

C_ZA, C_ZB, C_Q, C_K, C_V, C_ZD = 0, 1024, 2048, 2560, 2688, 2816


def _params(*sem):
    return pltpu.CompilerParams(dimension_semantics=sem, vmem_limit_bytes=VMEM_LIMIT)


def _sig(v):
    return 1.0 / (1.0 + jnp.exp(-v))


def _dot(a, b):
    return jnp.dot(a, b, preferred_element_type=F32)


def _dot_nt(a, b):
    return lax.dot_general(a, b, (((1,), (1,)), ((), ())), preferred_element_type=F32)


def _dot_tn(a, b):
    return lax.dot_general(a, b, (((0,), (0,)), ((), ())), preferred_element_type=F32)


def _full(shape):
    nd = len(shape)
    return pl.BlockSpec(shape, lambda *_: (0,) * nd)


def _rms_mm(x, g, w, tm, tn, name):
    T = x.shape[0]
    if w.ndim == 3:
        tn = w.shape[2]
        N = w.shape[0] * tn
        wspec = pl.BlockSpec((None, D_MODEL, tn), lambda i, j: (j, 0, 0))
    else:
        N = w.shape[1]
        wspec = pl.BlockSpec((D_MODEL, tn), lambda i, j: (0, j))

    def body(x_ref, g_ref, w_ref, o_ref, xn_ref):
        @pl.when(pl.program_id(1) == 0)
        def _():
            xv = x_ref[...]
            r = lax.rsqrt(jnp.mean(xv * xv, axis=-1, keepdims=True) + EPS)
            xn_ref[...] = (xv * r * g_ref[...]).astype(BF16)

        o_ref[...] = _dot(xn_ref[...], w_ref[...]).astype(BF16)

    return pl.pallas_call(
        body, name=name, grid=(T // tm, N // tn),
        in_specs=[pl.BlockSpec((tm, D_MODEL), lambda i, j: (i, 0)), _full((1, D_MODEL)), wspec],
        out_specs=[pl.BlockSpec((tm, tn), lambda i, j: (i, j)), pl.BlockSpec((tm, D_MODEL), lambda i, j: (i, 0))],
        out_shape=[jax.ShapeDtypeStruct((T, N), BF16), jax.ShapeDtypeStruct((T, D_MODEL), BF16)],
        compiler_params=_params("parallel", "arbitrary"),
    )(x, g, w)


def _merge_fwd(x, ys, proj, wb, wo, tm, name):
    T = x.shape[0]

    def body(x_ref, ys_ref, zg_ref, wb_ref, wo_ref, xo_ref, mg_ref):
        merged = None
        for n in range(N_BRANCH):
            yn = ys_ref[:, n * HALF:(n + 1) * HALF]
            br = jnp.concatenate([_dot(yn, wb_ref[s, n]) for s in range(N_CHIPS)], axis=1)
            t = _sig(zg_ref[:, n * D_MODEL:(n + 1) * D_MODEL].astype(F32)) * br
            merged = t if merged is None else merged + t
        mb = merged.astype(BF16)
        mg_ref[...] = mb
        xo_ref[...] = x_ref[...] + _dot(mb, wo_ref[...])

    return pl.pallas_call(
        body, name=name, grid=(T // tm,),
        in_specs=[pl.BlockSpec((tm, D_MODEL), lambda i: (i, 0)), pl.BlockSpec((tm, N_BRANCH * HALF), lambda i: (i, 0)),
                  pl.BlockSpec((tm, GATE_W), lambda i: (i, 0)), _full(wb.shape), _full(wo.shape)],
        out_specs=[pl.BlockSpec((tm, D_MODEL), lambda i: (i, 0)), pl.BlockSpec((tm, D_MODEL), lambda i: (i, 0))],
        out_shape=[jax.ShapeDtypeStruct((T, D_MODEL), F32), jax.ShapeDtypeStruct((T, D_MODEL), BF16)],
        compiler_params=_params("parallel"),
    )(x, ys, proj, wb, wo)


def _ffn_down(xm, gu, wd, tm, name):
    T = xm.shape[0]

    def body(x_ref, gu_ref, wd_ref, o_ref):
        g = gu_ref[:, :D_FF].astype(F32)
        u = gu_ref[:, D_FF:].astype(F32)
        act = (g * _sig(g) * u).astype(BF16)
        o_ref[...] = x_ref[...] + _dot(act, wd_ref[...])

    return pl.pallas_call(
        body, name=name, grid=(T // tm,),
        in_specs=[pl.BlockSpec((tm, D_MODEL), lambda i: (i, 0)), pl.BlockSpec((tm, 2 * D_FF), lambda i: (i, 0)), _full(wd.shape)],
        out_specs=pl.BlockSpec((tm, D_MODEL), lambda i: (i, 0)),
        out_shape=jax.ShapeDtypeStruct((T, D_MODEL), F32),
        compiler_params=_params("parallel"),
    )(xm, gu, wd)


def _final_loss(x, g, tgt, tm, name):
    T = x.shape[0]

    def body(x_ref, g_ref, t_ref, dx_ref, dg_ref, ls_ref):
        @pl.when(pl.program_id(0) == 0)
        def _():
            dg_ref[...] = jnp.zeros_like(dg_ref)
            ls_ref[...] = jnp.zeros_like(ls_ref)

        xv = x_ref[...]
        gv = g_ref[...]
        r = lax.rsqrt(jnp.mean(xv * xv, axis=-1, keepdims=True) + EPS)
        xh = xv * r
        diff = xh * gv - t_ref[...]
        ls_ref[...] += jnp.full(ls_ref.shape, 0.5 / D_MODEL, F32) * jnp.sum(diff * diff)
        dy = diff * (1.0 / D_MODEL)
        dxh = dy * gv
        dx_ref[...] = r * (dxh - xh * jnp.mean(dxh * xh, axis=-1, keepdims=True))
        dg_ref[...] += jnp.sum(dy * xh, axis=0, keepdims=True)

    return pl.pallas_call(
        body, name=name, grid=(T // tm,),
        in_specs=[pl.BlockSpec((tm, D_MODEL), lambda i: (i, 0)), _full((1, D_MODEL)), pl.BlockSpec((tm, D_MODEL), lambda i: (i, 0))],
        out_specs=[pl.BlockSpec((tm, D_MODEL), lambda i: (i, 0)), _full((1, D_MODEL)), _full((1, 128))],
        out_shape=[jax.ShapeDtypeStruct((T, D_MODEL), F32), jax.ShapeDtypeStruct((1, D_MODEL), F32), jax.ShapeDtypeStruct((1, 128), F32)],
        compiler_params=_params("arbitrary"),
    )(x, g, tgt)


def _swiglu_bwd(dx, gu, wd, tm, name, after):
    T = dx.shape[0]

    def body(dx_ref, gu_ref, wd_ref, after_ref, dgu_ref, act_ref):
        del after_ref
        dact = _dot_nt(dx_ref[...].astype(BF16), wd_ref[...])
        g = gu_ref[:, :D_FF].astype(F32)
        u = gu_ref[:, D_FF:].astype(F32)
        s = _sig(g)
        silu = g * s
        act_ref[...] = (silu * u).astype(BF16)
        dgu_ref[:, :D_FF] = (dact * u * (s + silu * (1.0 - s))).astype(BF16)
        dgu_ref[:, D_FF:] = (dact * silu).astype(BF16)

    return pl.pallas_call(
        body, name=name, grid=(T // tm,),
        in_specs=[pl.BlockSpec((tm, D_MODEL), lambda i: (i, 0)), pl.BlockSpec((tm, 2 * D_FF), lambda i: (i, 0)), _full(wd.shape),
                  pl.BlockSpec(memory_space=pl.ANY)],
        out_specs=[pl.BlockSpec((tm, 2 * D_FF), lambda i: (i, 0)), pl.BlockSpec((tm, D_FF), lambda i: (i, 0))],
        out_shape=[jax.ShapeDtypeStruct((T, 2 * D_FF), BF16), jax.ShapeDtypeStruct((T, D_FF), BF16)],
        compiler_params=_params("parallel"),
    )(dx, gu, wd, after)


def _mm_tn(a, b, grid, a_block, a_map, b_block, b_map, o_shape, o_block, o_map, name):
    gk = grid[2]
    tm = [d for d in a_block if d is not None][-1]
    tn = [d for d in b_block if d is not None][-1]

    def body(a_ref, b_ref, o_ref, acc_ref):
        k = pl.program_id(2)
        p = _dot_tn(a_ref[...].astype(BF16), b_ref[...].astype(BF16))

        @pl.when(k == 0)
        def _():
            acc_ref[...] = p

        @pl.when(k > 0)
        def _():
            acc_ref[...] += p

        @pl.when(k == gk - 1)
        def _():
            o_ref[...] = acc_ref[...].astype(o_ref.dtype)

    return pl.pallas_call(
        body, name=name, grid=grid,
        in_specs=[pl.BlockSpec(a_block, a_map), pl.BlockSpec(b_block, b_map)],
        out_specs=pl.BlockSpec(o_block, o_map),
        out_shape=jax.ShapeDtypeStruct(o_shape, BF16),
        scratch_shapes=[pltpu.VMEM((tm, tn), F32)],
        compiler_params=_params("parallel", "parallel", "arbitrary"),
    )(a, b)


def _mm_nt_rmsbwd(a, w, x, g, dres, tm, tk, name):
    T = x.shape[0]
    if w.ndim == 3:
        tk = w.shape[2]
        gk = w.shape[0]
        wspec = pl.BlockSpec((None, D_MODEL, tk), lambda i, k: (k, 0, 0))
    else:
        gk = w.shape[1] // tk
        wspec = pl.BlockSpec((D_MODEL, tk), lambda i, k: (0, k))

    def body(a_ref, w_ref, x_ref, g_ref, r_ref, dx_ref, dg_ref, acc_ref):
        i, k = pl.program_id(0), pl.program_id(1)
        p = _dot_nt(a_ref[...], w_ref[...])

        @pl.when(k == 0)
        def _():
            acc_ref[...] = p

        @pl.when(k > 0)
        def _():
            acc_ref[...] += p

        @pl.when(jnp.logical_and(i == 0, k == 0))
        def _():
            dg_ref[...] = jnp.zeros_like(dg_ref)

        @pl.when(k == gk - 1)
        def _():
            dh = acc_ref[...]
            xv = x_ref[...]
            r = lax.rsqrt(jnp.mean(xv * xv, axis=-1, keepdims=True) + EPS)
            xh = xv * r
            dxh = dh * g_ref[...]
            dx_ref[...] = r_ref[...] + r * (dxh - xh * jnp.mean(dxh * xh, axis=-1, keepdims=True))
            dg_ref[...] += jnp.sum(dh * xh, axis=0, keepdims=True)

    return pl.pallas_call(
        body, name=name, grid=(T // tm, gk),
        in_specs=[pl.BlockSpec((tm, tk), lambda i, k: (i, k)), wspec, pl.BlockSpec((tm, D_MODEL), lambda i, k: (i, 0)),
                  _full((1, D_MODEL)), pl.BlockSpec((tm, D_MODEL), lambda i, k: (i, 0))],
        out_specs=[pl.BlockSpec((tm, D_MODEL), lambda i, k: (i, 0)), _full((1, D_MODEL))],
        out_shape=[jax.ShapeDtypeStruct((T, D_MODEL), F32), jax.ShapeDtypeStruct((1, D_MODEL), F32)],
        scratch_shapes=[pltpu.VMEM((tm, D_MODEL), F32)],
        compiler_params=_params("arbitrary", "arbitrary"),
    )(a, w, x, g, dres)


def _merge_bwd(dxm, ys, proj, wb, wo, tm, name):
    T = dxm.shape[0]

    def body(dx_ref, ys_ref, zg_ref, wb_ref, wo_ref, dys_ref, dbr_ref, dp_ref):
        dmerged = _dot_nt(dx_ref[...].astype(BF16), wo_ref[...])
        for n in range(N_BRANCH):
            yn = ys_ref[:, n * HALF:(n + 1) * HALF]
            br = jnp.concatenate([_dot(yn, wb_ref[s, n]) for s in range(N_CHIPS)], axis=1)
            gt = _sig(zg_ref[:, n * D_MODEL:(n + 1) * D_MODEL].astype(F32))
            dbr = (gt * dmerged).astype(BF16)
            dbr_ref[:, n * D_MODEL:(n + 1) * D_MODEL] = dbr
            dp_ref[:, n * D_MODEL:(n + 1) * D_MODEL] = (dmerged * br * gt * (1.0 - gt)).astype(BF16)
            dy = None
            for s in range(N_CHIPS):
                t = _dot_nt(dbr[:, s * 256:(s + 1) * 256], wb_ref[s, n])
                dy = t if dy is None else dy + t
            dys_ref[:, n * HALF:(n + 1) * HALF] = dy.astype(BF16)
        dp_ref[:, GATE_W:] = jnp.zeros((tm, MIX_W - GATE_W), BF16)

    return pl.pallas_call(
        body, name=name, grid=(T // tm,),
        in_specs=[pl.BlockSpec((tm, D_MODEL), lambda i: (i, 0)), pl.BlockSpec((tm, N_BRANCH * HALF), lambda i: (i, 0)),
                  pl.BlockSpec((tm, GATE_W), lambda i: (i, 0)), _full(wb.shape), _full(wo.shape)],
        out_specs=[pl.BlockSpec((tm, N_BRANCH * HALF), lambda i: (i, 0)), pl.BlockSpec((tm, GATE_W), lambda i: (i, 0)),
                   pl.BlockSpec((tm, MIX_W), lambda i: (i, 0))],
        out_shape=[jax.ShapeDtypeStruct((T, N_BRANCH * HALF), BF16), jax.ShapeDtypeStruct((T, GATE_W), BF16),
                   jax.ShapeDtypeStruct((T, PROJ_PAD), BF16)],
        compiler_params=_params("parallel"),
    )(dxm, ys, proj, wb, wo)


def _gelu(v):
    return 0.5 * v * (1.0 + lax.erf(v * INV_SQRT2))


def _gelu_grad(v):
    return 0.5 * (1.0 + lax.erf(v * INV_SQRT2)) + v * jnp.exp(-0.5 * v * v) * INV_SQRT_2PI


def _rot_half(t):
    w = t.shape[1]
    lane = lax.broadcasted_iota(jnp.int32, t.shape, 1)
    return jnp.where((lane % HEAD_DIM) < HEAD_DIM // 2, pltpu.roll(t, w - HEAD_DIM // 2, 1), pltpu.roll(t, HEAD_DIM // 2, 1))


def _rope(t, cos, sin_signed):
    return t * cos + _rot_half(t) * sin_signed


def _rope_t(d, cos, sin_signed):
    return d * cos + _rot_half(d * sin_signed)


def _ln_fwd(v, g, b):
    mu = jnp.mean(v, axis=-1, keepdims=True)
    vc = v - mu
    r = lax.rsqrt(jnp.mean(vc * vc, axis=-1, keepdims=True) + EPS)
    vh = vc * r
    return vh * g + b, vh, r


def _ln_bwd(dn, vh, r, g):
    dvh = dn * g
    return r * (dvh - jnp.mean(dvh, axis=-1, keepdims=True) - vh * jnp.mean(dvh * vh, axis=-1, keepdims=True))


def _sublane_shifts(sh_ref, rows):
    for b in range(1, 8):
        sh_ref[b, 0:rows - 8, :] = sh_ref[0, pl.ds(b, rows - 8), :]


def _tap(sh_ref, off, n):
    return sh_ref[off % 8, pl.ds(off - off % 8, n), :]


def _tril_mask():
    return lax.broadcasted_iota(jnp.int32, (SG_CHUNK, SG_CHUNK), 0) >= lax.broadcasted_iota(jnp.int32, (SG_CHUNK, SG_CHUNK), 1)


def _attn_probs(qs, kh, sink_col, first_ok):
    s = _dot_nt(qs, kh) * (HEAD_DIM ** -0.5)
    row = lax.broadcasted_iota(jnp.int32, s.shape, 0) % WINDOW
    col = lax.broadcasted_iota(jnp.int32, s.shape, 1)
    valid = (col > row) & (col <= row + WINDOW) & ((col >= WINDOW) | first_ok)
    s = jnp.where(valid, s, NEG_BIG)
    m = jnp.maximum(jnp.max(s, axis=-1, keepdims=True), sink_col)
    p = jnp.where(valid, jnp.exp(s - m), 0.0)
    es = jnp.exp(sink_col - m)
    inv = 1.0 / (jnp.sum(p, axis=-1, keepdims=True) + es)
    return p * inv, es * inv


def _sink_col(sinks_ref, h):
    return jnp.concatenate([jnp.broadcast_to(sinks_ref[:, h * Q_PER_KV + g:h * Q_PER_KV + g + 1], (WINDOW, 1))
                            for g in range(Q_PER_KV)], axis=0)


def _mixer_in_specs(TB, nb):
    r = TB // HALO
    last = nb * r - 1
    cur = pl.BlockSpec((TB, MIX_W), lambda i: (i, 1))
    prev = pl.BlockSpec((HALO, MIX_W), lambda i: (jnp.maximum(i * r - 1, 0), 1))
    nxt = pl.BlockSpec((HALO, MIX_W), lambda i: (jnp.minimum((i + 1) * r, last), 1))
    tcur = pl.BlockSpec((TB, 128), lambda i: (i, 0))
    tprev = pl.BlockSpec((HALO, 128), lambda i: (jnp.maximum(i * r - 1, 0), 0))
    tnxt = pl.BlockSpec((HALO, 128), lambda i: (jnp.minimum((i + 1) * r, last), 0))
    return cur, prev, nxt, tcur, tprev, tnxt


def _mixer_param_specs():
    return [_full((1, HALF)), _full((1, HALF)), _full((SG_GROUPS, SG_CHUNK, SG_CHUNK)), _full((SG_CHUNK, 128)),
            _full((32, HALF)), _full((1, HALF)), _full((1, HALF)), _full((1, HALF)), _full((1, 128)), _full((8, HALF))]


def _mixers_fwd(proj, cos_t, sin_t, mp, TB, name):
    T = proj.shape[0]
    nb = T // TB
    r = TB // HALO
    cur, prev, _, tcur, tprev, _ = _mixer_in_specs(TB, nb)

    def body(zc_ref, zp_ref, cc_ref, sc_ref, cp_ref, sp_ref,
             lg_ref, lb_ref, sgw_ref, sgb_ref, cvw_ref, cvb_ref, cvg_ref, cvbb_ref, sinks_ref, scw_ref,
             ys_ref, scr_ref, k_ref, v_ref, sh_ref):
        i = pl.program_id(0)
        pm = (i > 0).astype(F32)

        def colsE(c0, c1):
            return jnp.concatenate([zp_ref[:, c0:c1].astype(F32) * pm, zc_ref[:, c0:c1].astype(F32)], axis=0)

        a = _gelu(zc_ref[:, C_ZA:C_ZA + 2 * HALF].astype(F32))
        u = a[:, :HALF]
        vn, _, _ = _ln_fwd(a[:, HALF:], lg_ref[...], lb_ref[...])
        vnb = vn.astype(BF16)
        tril = _tril_mask()
        for g in range(SG_GROUPS):
            wt = jnp.where(tril, sgw_ref[g], 0.0).astype(BF16)
            for ci in range(r):
                rows = slice(ci * SG_CHUNK, (ci + 1) * SG_CHUNK)
                cols = slice(g * 128, (g + 1) * 128)
                mixed = _dot(wt, vnb[rows, cols]) + sgb_ref[:, g:g + 1]
                ys_ref[rows, g * 128:(g + 1) * 128] = (u[rows, cols] * mixed).astype(BF16)

        def colsB(c0, c1):
            return jnp.concatenate([zp_ref[HALO - CV_PAD:, c0:c1].astype(F32) * pm, zc_ref[:, c0:c1].astype(F32)], axis=0)

        sh_ref[0] = colsB(C_ZB, C_ZB + HALF) * _sig(colsB(C_ZB + HALF, C_ZB + 2 * HALF))
        _sublane_shifts(sh_ref, TB + CV_PAD)
        c = jnp.broadcast_to(cvb_ref[...], (TB, HALF))
        for k in range(CV_KERNEL):
            c = c + cvw_ref[k:k + 1, :] * _tap(sh_ref, CV_PAD - (CV_KERNEL - 1) + k, TB)
        n, _, _ = _ln_fwd(c, cvg_ref[...], cvbb_ref[...])
        ys_ref[:, HALF:2 * HALF] = (n * _sig(n)).astype(BF16)

        zd = colsE(C_ZD + HALF, C_ZD + 3 * HALF)
        scr_ref[...] = zd[:, :HALF] * zd[:, HALF:]
        cv = None
        for k in range(SC_KERNEL):
            t = scw_ref[k:k + 1, :] * scr_ref[pl.ds(HALO - (SC_KERNEL - 1) + k, TB), :]
            cv = t if cv is None else cv + t
        ys_ref[:, 3 * HALF:4 * HALF] = (zc_ref[:, C_ZD:C_ZD + HALF].astype(F32) * cv).astype(BF16)

        cosE = jnp.concatenate([cp_ref[...], cc_ref[...]], axis=0)
        sinE = jnp.concatenate([sp_ref[...], sc_ref[...]], axis=0)
        k_ref[...] = _rope(colsE(C_K, C_K + 128), cosE, sinE).astype(BF16)
        v_ref[...] = colsE(C_V, C_V + 128).astype(BF16)
        cosC, sinC = cc_ref[...], sc_ref[...]
        q = jnp.concatenate([_rope(zc_ref[:, C_Q + 128 * j:C_Q + 128 * (j + 1)].astype(F32), cosC, sinC)
                             for j in range(4)], axis=1).astype(BF16)
        for qb in range(r):
            first_ok = (i * r + qb) > 0
            for h in range(N_KV_HEADS):
                hc = slice(h * HEAD_DIM, (h + 1) * HEAD_DIM)
                kh = k_ref[qb * WINDOW:qb * WINDOW + 2 * WINDOW, hc]
                vh = v_ref[qb * WINDOW:qb * WINDOW + 2 * WINDOW, hc]
                qs = jnp.concatenate([q[qb * WINDOW:(qb + 1) * WINDOW, (h * Q_PER_KV + g) * HEAD_DIM:(h * Q_PER_KV + g + 1) * HEAD_DIM]
                                      for g in range(Q_PER_KV)], axis=0)
                probs, _ = _attn_probs(qs, kh, _sink_col(sinks_ref, h), first_ok)
                o = _dot(probs.astype(BF16), vh)
                for g in range(Q_PER_KV):
                    c0 = 2 * HALF + (h * Q_PER_KV + g) * HEAD_DIM
                    ys_ref[qb * WINDOW:(qb + 1) * WINDOW, c0:c0 + HEAD_DIM] = o[g * WINDOW:(g + 1) * WINDOW].astype(BF16)

    return pl.pallas_call(
        body, name=name, grid=(nb,),
        in_specs=[cur, prev, tcur, tcur, tprev, tprev] + _mixer_param_specs(),
        out_specs=pl.BlockSpec((TB, 4 * HALF), lambda i: (i, 0)),
        out_shape=jax.ShapeDtypeStruct((T, 4 * HALF), BF16),
        scratch_shapes=[pltpu.VMEM((TB + HALO, HALF), F32), pltpu.VMEM((TB + HALO, 128), BF16), pltpu.VMEM((TB + HALO, 128), BF16),
                        pltpu.VMEM((8, TB + CV_PAD, HALF), F32)],
        compiler_params=_params("parallel"),
    )(proj, proj, cos_t, sin_t, cos_t, sin_t, *mp)


def _mixers_bwd(proj, dys, dproj, cos_t, sin_t, mp, TB, name):
    T = proj.shape[0]
    nb = T // TB
    r = TB // HALO
    RE = TB + 2 * HALO
    RC = TB + HALO
    cur, prev, nxt, tcur, tprev, tnxt = _mixer_in_specs(TB, nb)
    dcur = pl.BlockSpec((TB, 4 * HALF), lambda i: (i, 0))
    dnxt = pl.BlockSpec((HALO, 4 * HALF), lambda i: (jnp.minimum((i + 1) * r, nb * r - 1), 0))

    def body(zc_ref, zp_ref, zn_ref, dyc_ref, dyn_ref, cc_ref, sc_ref, cp_ref, sp_ref, cn_ref, sn_ref,
             lg_ref, lb_ref, sgw_ref, sgb_ref, cvw_ref, cvb_ref, cvg_ref, cvbb_ref, sinks_ref, scw_ref, dp_in_ref,
             dz_ref, dlg_ref, dlb_ref, dsgw_ref, dsgb_ref, dcvw_ref, dcvb_ref, dcvg_ref, dcvbb_ref, dsink_ref, dscw_ref,
             scr_ref, scr2_ref, k_ref, v_ref, dk_ref, dv_ref, dq_ref, sh_ref, sh2_ref):
        del dp_in_ref
        i = pl.program_id(0)
        pm = (i > 0).astype(F32)
        nm = (i < nb - 1).astype(F32)

        @pl.when(i == 0)
        def _():
            for ref in (dlg_ref, dlb_ref, dsgw_ref, dsgb_ref, dcvw_ref, dcvb_ref, dcvg_ref, dcvbb_ref, dsink_ref, dscw_ref):
                ref[...] = jnp.zeros_like(ref)

        def colsE(c0, c1):
            return jnp.concatenate([zp_ref[:, c0:c1].astype(F32) * pm, zc_ref[:, c0:c1].astype(F32),
                                    zn_ref[:, c0:c1].astype(F32)], axis=0)

        def colsC(c0, c1):
            return jnp.concatenate([zc_ref[:, c0:c1].astype(F32), zn_ref[:, c0:c1].astype(F32)], axis=0)

        def dyC(c0, c1):
            return jnp.concatenate([dyc_ref[:, c0:c1].astype(F32), dyn_ref[:, c0:c1].astype(F32) * nm], axis=0)

        za = zc_ref[:, C_ZA:C_ZA + 2 * HALF].astype(F32)
        a = _gelu(za)
        u = a[:, :HALF]
        lg = lg_ref[...]
        vn, vh, rs = _ln_fwd(a[:, HALF:], lg, lb_ref[...])
        vnb = vn.astype(BF16)
        dya = dyc_ref[:, 0:HALF].astype(F32)
        tril = _tril_mask()
        lane128 = lax.broadcasted_iota(jnp.int32, (SG_CHUNK, 128), 1)
        du_parts, dvn_parts = [], []
        for ci in range(r):
            rows = slice(ci * SG_CHUNK, (ci + 1) * SG_CHUNK)
            du_g, dvn_g = [], []
            for g in range(SG_GROUPS):
                cols = slice(g * 128, (g + 1) * 128)
                wt = jnp.where(tril, sgw_ref[g], 0.0).astype(BF16)
                vb = vnb[rows, cols]
                mixed = _dot(wt, vb) + sgb_ref[:, g:g + 1]
                dy_blk = dya[rows, cols]
                du_g.append(dy_blk * mixed)
                dmix = dy_blk * u[rows, cols]
                dmb = dmix.astype(BF16)
                dvn_g.append(_dot_tn(wt, dmb))
                dsgw_ref[g] += jnp.where(tril, _dot_nt(dmb, vb), 0.0)
                dsgb_ref[...] += jnp.where(lane128 == g, jnp.sum(dmix, axis=1, keepdims=True), 0.0)
            du_parts.append(jnp.concatenate(du_g, axis=1))
            dvn_parts.append(jnp.concatenate(dvn_g, axis=1))
        du = jnp.concatenate(du_parts, axis=0) if r > 1 else du_parts[0]
        dvn = jnp.concatenate(dvn_parts, axis=0) if r > 1 else dvn_parts[0]
        dlg_ref[...] += jnp.sum(dvn * vh, axis=0, keepdims=True)
        dlb_ref[...] += jnp.sum(dvn, axis=0, keepdims=True)
        dvv = _ln_bwd(dvn, vh, rs, lg)
        gg = _gelu_grad(za)
        dz_ref[:, C_ZA:C_ZA + HALF] = (du * gg[:, :HALF]).astype(BF16)
        dz_ref[:, C_ZA + HALF:C_ZA + 2 * HALF] = (dvv * gg[:, HALF:]).astype(BF16)

        RB = TB + CV_PAD

        def colsB(c0, c1):
            return jnp.concatenate([zp_ref[HALO - CV_PAD:, c0:c1].astype(F32) * pm, zc_ref[:, c0:c1].astype(F32),
                                    zn_ref[:CV_PAD, c0:c1].astype(F32)], axis=0)

        sh_ref[0] = colsB(C_ZB, C_ZB + HALF) * _sig(colsB(C_ZB + HALF, C_ZB + 2 * HALF))
        _sublane_shifts(sh_ref, RB + CV_PAD)
        c = jnp.broadcast_to(cvb_ref[...], (RB, HALF))
        for k in range(CV_KERNEL):
            c = c + cvw_ref[k:k + 1, :] * _tap(sh_ref, CV_PAD - (CV_KERNEL - 1) + k, RB)
        cvg = cvg_ref[...]
        n, ch, rc = _ln_fwd(c, cvg, cvbb_ref[...])
        sn = _sig(n)
        dyb = jnp.concatenate([dyc_ref[:, HALF:2 * HALF].astype(F32), dyn_ref[:CV_PAD, HALF:2 * HALF].astype(F32) * nm], axis=0)
        dn = dyb * (sn + n * sn * (1.0 - sn))
        dno = dn[:TB]
        dcvg_ref[...] += jnp.sum(dno * ch[:TB], axis=0, keepdims=True)
        dcvbb_ref[...] += jnp.sum(dno, axis=0, keepdims=True)
        dc = _ln_bwd(dn, ch, rc, cvg)
        sh2_ref[0] = dc
        _sublane_shifts(sh2_ref, RB)
        dcvb_ref[...] += jnp.sum(dc[:TB], axis=0, keepdims=True)
        dy0 = None
        for k in range(CV_KERNEL):
            wk = cvw_ref[k:k + 1, :]
            t = wk * _tap(sh2_ref, CV_KERNEL - 1 - k, TB)
            dy0 = t if dy0 is None else dy0 + t
            dcvw_ref[k:k + 1, :] += jnp.sum(dc[:TB] * _tap(sh_ref, CV_PAD - (CV_KERNEL - 1) + k, TB), axis=0, keepdims=True)
        ab = zc_ref[:, C_ZB:C_ZB + HALF].astype(F32)
        sg = _sig(zc_ref[:, C_ZB + HALF:C_ZB + 2 * HALF].astype(F32))
        dz_ref[:, C_ZB:C_ZB + HALF] = (dy0 * sg).astype(BF16)
        dz_ref[:, C_ZB + HALF:C_ZB + 2 * HALF] = (dy0 * ab * sg * (1.0 - sg)).astype(BF16)

        zd = colsE(C_ZD + HALF, C_ZD + 3 * HALF)
        scr_ref[...] = zd[:, :HALF] * zd[:, HALF:]
        dcv = dyC(3 * HALF, 4 * HALF) * colsC(C_ZD, C_ZD + HALF)
        scr2_ref[...] = dcv
        cv = None
        dud = None
        for k in range(SC_KERNEL):
            wk = scw_ref[k:k + 1, :]
            us = scr_ref[pl.ds(HALO - (SC_KERNEL - 1) + k, TB), :]
            t = wk * us
            cv = t if cv is None else cv + t
            t2 = wk * scr2_ref[pl.ds(SC_KERNEL - 1 - k, TB), :]
            dud = t2 if dud is None else dud + t2
            dscw_ref[k:k + 1, :] += jnp.sum(dcv[:TB] * us, axis=0, keepdims=True)
        dz_ref[:, C_ZD:C_ZD + HALF] = (dyc_ref[:, 3 * HALF:4 * HALF].astype(F32) * cv).astype(BF16)
        dz_ref[:, C_ZD + HALF:C_ZD + 2 * HALF] = (dud * zc_ref[:, C_ZD + 2 * HALF:C_ZD + 3 * HALF].astype(F32)).astype(BF16)
        dz_ref[:, C_ZD + 2 * HALF:C_ZD + 3 * HALF] = (dud * zc_ref[:, C_ZD + HALF:C_ZD + 2 * HALF].astype(F32)).astype(BF16)

        cosE = jnp.concatenate([cp_ref[...], cc_ref[...], cn_ref[...]], axis=0)
        sinE = jnp.concatenate([sp_ref[...], sc_ref[...], sn_ref[...]], axis=0)
        k_ref[...] = _rope(colsE(C_K, C_K + 128), cosE, sinE).astype(BF16)
        v_ref[...] = colsE(C_V, C_V + 128).astype(BF16)
        dk_ref[...] = jnp.zeros_like(dk_ref)
        dv_ref[...] = jnp.zeros_like(dv_ref)
        q = jnp.concatenate([_rope(colsC(C_Q + 128 * j, C_Q + 128 * (j + 1)), cosE[HALO:], sinE[HALO:])
                             for j in range(4)], axis=1).astype(BF16)
        dO = dyC(2 * HALF, 3 * HALF).astype(BF16)
        lane_s = lax.broadcasted_iota(jnp.int32, (1, 128), 1)
        for qb in range(r + 1):
            first_ok = (i * r + qb) > 0
            rows = slice(qb * WINDOW, (qb + 1) * WINDOW)
            band = slice(qb * WINDOW, qb * WINDOW + 2 * WINDOW)
            for h in range(N_KV_HEADS):
                hc = slice(h * HEAD_DIM, (h + 1) * HEAD_DIM)
                kh = k_ref[band, hc]
                vh_ = v_ref[band, hc]
                heads = [slice((h * Q_PER_KV + g) * HEAD_DIM, (h * Q_PER_KV + g + 1) * HEAD_DIM) for g in range(Q_PER_KV)]
                qs = jnp.concatenate([q[rows, hs] for hs in heads], axis=0)
                dos = jnp.concatenate([dO[rows, hs] for hs in heads], axis=0)
                probs, p_sink = _attn_probs(qs, kh, _sink_col(sinks_ref, h), first_ok)
                dP = _dot_nt(dos, vh_)
                rsum = jnp.sum(probs * dP, axis=-1, keepdims=True)
                dS = (probs * (dP - rsum) * (HEAD_DIM ** -0.5)).astype(BF16)
                dk_ref[band, hc] += _dot_tn(dS, qs)
                dv_ref[band, hc] += _dot_tn(probs.astype(BF16), dos)
                if qb < r:
                    dqs = _dot(dS, kh)
                    dsk = -p_sink * rsum
                    for g in range(Q_PER_KV):
                        dq_ref[rows, heads[g]] = dqs[g * WINDOW:(g + 1) * WINDOW]
                        dsink_ref[...] += jnp.where(lane_s == h * Q_PER_KV + g, jnp.sum(dsk[g * WINDOW:(g + 1) * WINDOW]), 0.0)
        cosC, sinC = cc_ref[...], sc_ref[...]
        for j in range(4):
            dz_ref[:, C_Q + 128 * j:C_Q + 128 * (j + 1)] = _rope_t(dq_ref[:, 128 * j:128 * (j + 1)], cosC, sinC).astype(BF16)
        dz_ref[:, C_K:C_K + 128] = _rope_t(dk_ref[HALO:HALO + TB, :], cosC, sinC).astype(BF16)
        dz_ref[:, C_V:C_V + 128] = dv_ref[HALO:HALO + TB, :].astype(BF16)

    small = [((1, HALF), F32), ((1, HALF), F32), ((SG_GROUPS, SG_CHUNK, SG_CHUNK), F32), ((SG_CHUNK, 128), F32),
             ((32, HALF), F32), ((1, HALF), F32), ((1, HALF), F32), ((1, HALF), F32), ((1, 128), F32), ((8, HALF), F32)]
    outs = pl.pallas_call(
        body, name=name, grid=(nb,),
        in_specs=[cur, prev, nxt, dcur, dnxt, tcur, tcur, tprev, tprev, tnxt, tnxt] + _mixer_param_specs()
                 + [pl.BlockSpec(memory_space=pl.ANY)],
        out_specs=[pl.BlockSpec((TB, MIX_W), lambda i: (i, 1))] + [_full(s) for s, _ in small],
        out_shape=[jax.ShapeDtypeStruct((T, PROJ_PAD), BF16)] + [jax.ShapeDtypeStruct(s, d) for s, d in small],
        scratch_shapes=[pltpu.VMEM((RE, HALF), F32), pltpu.VMEM((RC, HALF), F32), pltpu.VMEM((RE, 128), BF16), pltpu.VMEM((RE, 128), BF16),
                        pltpu.VMEM((RE, 128), F32), pltpu.VMEM((RE, 128), F32), pltpu.VMEM((TB, HALF), F32),
                        pltpu.VMEM((8, TB + 2 * CV_PAD, HALF), F32), pltpu.VMEM((8, TB + CV_PAD, HALF), F32)],
        input_output_aliases={21: 0},
        compiler_params=_params("arbitrary"),
    )(proj, proj, proj, dys, dys, cos_t, sin_t, cos_t, sin_t, cos_t, sin_t, *mp, dproj)
    return outs


def _rope_tables(T):
    pos = jnp.arange(T, dtype=F32)
    inv_freq = 1.0 / (ROPE_THETA ** (jnp.arange(0, HEAD_DIM, 2, dtype=F32) / HEAD_DIM))
    ang = pos[:, None] * inv_freq[None, :]
    cos, sin = jnp.cos(ang), jnp.sin(ang)
    cos_t = jnp.concatenate([cos, cos, cos, cos], axis=1)
    sin_t = jnp.concatenate([-sin, sin, -sin, sin], axis=1)
    return cos_t, sin_t


def _mixer_params(l, sg_ln_g, sg_ln_b, sg_w, sg_b, cv_w, cv_b, cv_ln_g, cv_ln_b, attn_sinks, sc_w):
    sgb_t = jnp.zeros((SG_CHUNK, 128), F32).at[:, :SG_GROUPS].set(sg_b[l].T)
    cvw = jnp.zeros((32, HALF), F32).at[:CV_KERNEL].set(cv_w[l])
    scw = jnp.zeros((8, HALF), F32).at[:SC_KERNEL].set(sc_w[l])
    sinks = jnp.zeros((1, 128), F32).at[0, :N_Q_HEADS].set(attn_sinks[l])
    return [sg_ln_g[l][None], sg_ln_b[l][None], sg_w[l], sgb_t, cvw, cv_b[l][None], cv_ln_g[l][None], cv_ln_b[l][None], sinks, scw]


def _w_in_layout(w_in_g):
    wmix = jnp.concatenate([w_in_g[0], w_in_g[1], w_in_g[2][:, :MIX_W - 2 * W_IN_SHARD]], axis=1)
    wg = jnp.concatenate([w_in_g[2][:, MIX_W - 2 * W_IN_SHARD:], w_in_g[3]], axis=1)
    return jnp.concatenate([wg, jnp.zeros((D_MODEL, MIX_W - GATE_W), w_in_g.dtype), wmix], axis=1)


def _w_in_unlayout(dw):
    cut = MIX_W - 2 * W_IN_SHARD
    return jnp.stack([dw[:, MIX_W:MIX_W + W_IN_SHARD], dw[:, MIX_W + W_IN_SHARD:MIX_W + 2 * W_IN_SHARD],
                      jnp.concatenate([dw[:, MIX_W + 2 * W_IN_SHARD:], dw[:, :W_IN_SHARD - cut]], axis=1),
                      dw[:, W_IN_SHARD - cut:GATE_W]], axis=0)


def _device_step(x, tgt, norm_mix, norm_ffn, norm_final, mixer_params, w_in_p, wb_g, wo_g, wgu_g, wd_g):
    T = x.shape[0]
    tables = _rope_tables(T)
    saved = []
    for l in range(DEPTH):
        lw = dict(w_in=w_in_p[l], w_branch=wb_g[l], w_out=wo_g[l], w_gate_up=wgu_g[l], w_down=wd_g[l],
                  norm_mix=norm_mix[l][None], norm_ffn=norm_ffn[l][None], mixer=mixer_params[l], after=jnp.zeros((8, 128), F32))
        x, sv = _fwd_layer(l, x, lw, tables)
        saved.append((lw, sv))
    dx, dnf, loss = _final_loss(x, norm_final[None], tgt, 256, "final_loss")
    grads = [None] * DEPTH
    for l in reversed(range(DEPTH)):
        lw, sv = saved[l]
        dxm, g_ffn = _bwd_layer_ffn(l, dx, lw, sv)
        dx, g_mix = _bwd_layer_mix(l, dxm, lw, sv, tables)
        grads[l] = {**g_ffn, **g_mix}
    return loss, dx, dnf[0], grads


MIX_BLOCK = 256


def _fwd_layer(l, x, lw, tables):
    return _fwd_layer_rest(l, x, _fwd_layer_mix(l, x, lw, tables), lw)


def _fwd_layer_mix(l, x, lw, tables):
    proj, xn = _rms_mm(x, lw["norm_mix"], lw["w_in"], min(x.shape[0], 1024), 2176, f"proj{l}")
    return proj, xn, _mixers_fwd(proj, *tables, lw["mixer"], MIX_BLOCK, f"mixers_fwd{l}")


def _fwd_layer_rest(l, x, mixed, lw):
    proj, xn, ys = mixed
    TM = min(x.shape[0], 1024)
    xm, merged = _merge_fwd(x, ys, proj, lw["w_branch"], lw["w_out"], 256, f"merge_fwd{l}")
    gu, hn = _rms_mm(xm, lw["norm_ffn"], lw["w_gate_up"], TM, GU_SHARD, f"ffn_up{l}")
    x_out = _ffn_down(xm, gu, lw["w_down"], 256, f"ffn_down{l}")
    return x_out, (x, proj, xn, ys, xm, merged, gu, hn)


def _bwd_layer_ffn(l, dx, lw, sv):
    x_in, proj, xn, ys, xm, merged, gu, hn = sv
    T = dx.shape[0]
    tkk = min(T, 1024)
    gk = T // tkk
    dgu, act = _swiglu_bwd(dx, gu, lw["w_down"], 256, f"swiglu_bwd{l}", lw["after"])
    d_wd = _mm_tn(act, dx, (2, 1, gk), (tkk, D_FF // 2), lambda i, j, k: (k, i), (tkk, D_MODEL), lambda i, j, k: (k, 0),
                  (D_FF, D_MODEL), (D_FF // 2, D_MODEL), lambda i, j, k: (i, 0), f"dw_down{l}")
    d_wgu = _mm_tn(hn, dgu, (1, N_CHIPS, gk), (tkk, D_MODEL), lambda i, j, k: (k, 0), (tkk, GU_SHARD), lambda i, j, k: (k, j),
                   (N_CHIPS, D_MODEL, GU_SHARD), (None, D_MODEL, GU_SHARD), lambda i, j, k: (j, 0, 0), f"dw_gate_up{l}")
    dxm, d_nffn = _mm_nt_rmsbwd(dgu, lw["w_gate_up"], xm, lw["norm_ffn"], dx, min(T, 512), GU_SHARD, f"ffn_up_bwd{l}")
    dys, dbr, dproj = _merge_bwd(dxm, ys, proj, lw["w_branch"], lw["w_out"], 256, f"merge_bwd{l}")
    d_wo = _mm_tn(merged, dxm, (2, 1, gk), (tkk, 512), lambda i, j, k: (k, i), (tkk, D_MODEL), lambda i, j, k: (k, 0),
                  (D_MODEL, D_MODEL), (512, D_MODEL), lambda i, j, k: (i, 0), f"dw_out{l}")
    d_wb = _mm_tn(ys, dbr, (N_BRANCH, N_CHIPS, gk), (tkk, HALF), lambda i, j, k: (k, i), (tkk, 256), lambda i, j, k: (k, i * N_CHIPS + j),
                  (N_CHIPS, N_BRANCH, HALF, 256), (None, None, HALF, 256), lambda i, j, k: (j, i, 0, 0), f"dw_branch{l}")
    return (dxm, dys, dproj), dict(w_branch=d_wb, w_out=d_wo, w_gate_up=d_wgu, w_down=d_wd, norm_ffn=d_nffn[0])


def _bwd_layer_mix(l, carry, lw, sv, tables):
    dxm, dys, dproj = carry
    x_in, proj, xn, ys, xm, merged, gu, hn = sv
    T = dxm.shape[0]
    tkk = min(T, 1024)
    gk = T // tkk
    mb = _mixers_bwd(proj, dys, dproj, *tables, lw["mixer"], MIX_BLOCK, f"mixers_bwd{l}")
    dproj = mb[0]
    d_win = _mm_tn(xn, dproj, (1, PROJ_PAD // 2176, gk), (tkk, D_MODEL), lambda i, j, k: (k, 0), (tkk, 2176), lambda i, j, k: (k, j),
                   (D_MODEL, PROJ_PAD), (D_MODEL, 2176), lambda i, j, k: (0, j), f"dw_in{l}")
    dx, d_nmix = _mm_nt_rmsbwd(dproj, lw["w_in"], x_in, lw["norm_mix"], dxm, min(T, 512), 2176, f"proj_bwd{l}")
    return dx, dict(w_in=d_win, norm_mix=d_nmix[0],
                    sg_ln_g=mb[1][0], sg_ln_b=mb[2][0], sg_w=mb[3], sg_b=mb[4][:, :SG_GROUPS].T,
                    cv_w=mb[5][:CV_KERNEL], cv_b=mb[6][0], cv_ln_g=mb[7][0], cv_ln_b=mb[8][0],
                    attn_sinks=mb[9][0, :N_Q_HEADS], sc_w=mb[10][:SC_KERNEL])


ANY = pl.BlockSpec(memory_space=pl.ANY)
BIG = ("w_in", "w_branch", "w_out", "w_gate_up", "w_down")
HALF_SHAPE = {"w_in": (2, 512, W_IN_SHARD), "w_branch": (2, 1024, 256), "w_out": (2, 128, D_MODEL),
              "w_gate_up": (2, 512, GU_SHARD), "w_down": (2, 352, D_MODEL)}
NB = len(BIG)


def _place():
    x, y, c = lax.axis_index("x"), lax.axis_index("y"), lax.axis_index("c")
    chips = [(1 - x, y), (x, 1 - y), (1 - x, 1 - y)]
    return x, y, c, 2 * x + y, chips, [2 * px + py for px, py in chips]


def _remote(src, dst, ssem, rsem, dev):
    return pltpu.make_async_remote_copy(src_ref=src, dst_ref=dst, send_sem=ssem, recv_sem=rsem, device_id=dev, device_id_type=MESH)


HBM_SPEC = pl.BlockSpec(memory_space=pltpu.HBM)
SEM_SPEC = pl.BlockSpec(memory_space=pltpu.SEMAPHORE)
DATAFLOW = pltpu.SideEffectType.DATAFLOW_SIDE_EFFECTING


def _ici_ends(kind, src, land, j, c, chip, chip_ids):
    if kind == "gather":
        return src.at[c], land.at[chip, c], land.at[chip_ids[j], c]
    return src.at[chip_ids[j]], land.at[chip], land.at[chip_ids[j]]


def _ici_start(kind, srcs, land_shapes, name):
    n = len(srcs)

    def body(*refs):
        src, land = refs[:n], refs[n:2 * n]
        ssem, rsem, token = refs[2 * n], refs[2 * n + 1], refs[-1]
        x, y, c, chip, chips, chip_ids = _place()
        for k in range(n):
            for j in range(3):
                s, d, _ = _ici_ends(kind, src[k], land[k], j, c, chip, chip_ids)
                _remote(s, d, ssem.at[3 * k + j], rsem.at[3 * k + j], (*chips[j], c)).start()
        token[...] = jnp.zeros_like(token)

    sem = pltpu.SemaphoreType.DMA((3 * n,))
    outs = pl.pallas_call(
        body, name=name,
        out_shape=(sem, sem, *[pltpu.HBM(s.shape, s.dtype) for s in srcs], *[pltpu.HBM(sh, BF16) for sh in land_shapes],
                   jax.ShapeDtypeStruct((8, 128), F32)),
        in_specs=[HBM_SPEC] * (2 * n),
        out_specs=(SEM_SPEC, SEM_SPEC, *[HBM_SPEC] * (2 * n), pl.BlockSpec(memory_space=pltpu.VMEM)),
        input_output_aliases={i: 2 + i for i in range(2 * n)},
        compiler_params=pltpu.CompilerParams(has_side_effects=DATAFLOW),
    )(*[pltpu.with_memory_space_constraint(s, pltpu.HBM) for s in srcs],
      *[pltpu.with_memory_space_constraint(lax.empty(sh, BF16), pltpu.HBM) for sh in land_shapes])
    return (kind, outs[0], outs[1], list(outs[2:2 + n]), list(outs[2 + n:2 + 2 * n])), outs[-1]


def _ici_wait(handle, after, name):
    kind, ssem_in, rsem_in, srcs, lands = handle
    n = len(srcs)

    def body(*refs):
        src, land = refs[:n], refs[n:2 * n]
        ssem, rsem = refs[2 * n], refs[2 * n + 1]
        x, y, c, chip, chips, chip_ids = _place()
        for k in range(n):
            for j in range(3):
                s, _, mine = _ici_ends(kind, src[k], land[k], j, c, chip, chip_ids)
                cp = _remote(s, mine, ssem.at[3 * k + j], rsem.at[3 * k + j], (*chips[j], c))
                cp.wait_send()
                cp.wait_recv()

    outs = pl.pallas_call(
        body, name=name, out_shape=[pltpu.HBM(t.shape, t.dtype) for t in srcs + lands],
        in_specs=[HBM_SPEC] * (2 * n) + [SEM_SPEC, SEM_SPEC, ANY], out_specs=[HBM_SPEC] * (2 * n),
        input_output_aliases={i: i for i in range(2 * n)},
        compiler_params=pltpu.CompilerParams(has_side_effects=DATAFLOW),
    )(*srcs, *lands, ssem_in, rsem_in, after)
    return list(outs[:n]), list(outs[n:])


def _ag_pair(shards, lands, name):
    n = len(shards)

    def body(*refs):
        ins, outs = refs[:n], refs[2 * n:3 * n]
        token = refs[3 * n]
        s_fwd, r_fwd, s_own, r_own = refs[3 * n + 1:]
        x, y, c, chip, chips, chip_ids = _place()
        sib = (x, y, 1 - c)
        cps = []
        for k in range(n):
            cp = _remote(ins[k], outs[k].at[chip], s_own.at[k], r_own.at[k], sib)
            cp.start()
            cps.append(cp)
            for j in range(3):
                got = outs[k].at[chip_ids[j], c]
                cp = _remote(got, got, s_fwd.at[k, j], r_fwd.at[k, j], sib)
                cp.start()
                cps.append(cp)
        for k in range(n):
            _remote(ins[k], outs[k].at[chip], s_own.at[k], r_own.at[k], sib).wait_recv()
            for j in range(3):
                got = outs[k].at[chip_ids[j], 1 - c]
                _remote(got, got, s_fwd.at[k, j], r_fwd.at[k, j], sib).wait_recv()
        for cp in cps:
            cp.wait_send()
        token[...] = jnp.zeros_like(token)

    sem, sem1 = pltpu.SemaphoreType.DMA((n, 3)), pltpu.SemaphoreType.DMA((n,))
    outs = pl.pallas_call(
        body, name=name, out_shape=[jax.ShapeDtypeStruct(t.shape, t.dtype) for t in lands] + [jax.ShapeDtypeStruct((8, 128), F32)],
        in_specs=[ANY] * (2 * n), out_specs=[ANY] * n + [pl.BlockSpec(memory_space=pltpu.VMEM)],
        input_output_aliases={n + k: k for k in range(n)},
        scratch_shapes=[sem, sem, sem1, sem1], compiler_params=pltpu.CompilerParams(has_side_effects=True),
    )(*shards, *lands)
    return list(outs[:n]), outs[n]


def _rs_pair(grads, name):
    n_arr = len(grads)

    def body(*refs):
        ins, got = refs[:n_arr], refs[n_arr:2 * n_arr]
        ssem, rsem = refs[2 * n_arr:]
        x, y, c, _, _, _ = _place()
        sib = (x, y, 1 - c)
        sends = []
        for k in reversed(range(n_arr)):
            for q in range(N_CHIPS):
                cp = _remote(ins[k].at[q, 1 - c], got[k].at[q], ssem.at[k, q], rsem.at[k, q], sib)
                cp.start()
                sends.append(cp)
        for k in range(n_arr):
            for q in range(N_CHIPS):
                _remote(got[k].at[q], got[k].at[q], ssem.at[k, q], rsem.at[k, q], sib).wait_recv()
        for cp in sends:
            cp.wait_send()

    shp = [jax.ShapeDtypeStruct((N_CHIPS,) + g.shape[2:], BF16) for g in grads]
    sem = pltpu.SemaphoreType.DMA((n_arr, N_CHIPS))
    outs = pl.pallas_call(
        body, name=name, out_shape=shp, in_specs=[ANY] * n_arr, out_specs=[ANY] * n_arr,
        scratch_shapes=[sem, sem], compiler_params=pltpu.CompilerParams(has_side_effects=True),
    )(*grads)
    return list(outs)


def _rs_share(bufs, name):
    n = len(bufs)

    def body(*refs):
        outs = refs[n:2 * n]
        ssem, rsem = refs[2 * n:]
        x, y, c, _, _, _ = _place()
        sib = (x, y, 1 - c)
        sends = []
        for k in range(n):
            for l in range(DEPTH):
                cp = _remote(outs[k].at[l, c], outs[k].at[l, c], ssem.at[k, l], rsem.at[k, l], sib)
                cp.start()
                sends.append(cp)
        for k in range(n):
            for l in range(DEPTH):
                dst = outs[k].at[l, 1 - c]
                _remote(dst, dst, ssem.at[k, l], rsem.at[k, l], sib).wait_recv()
        for cp in sends:
            cp.wait_send()

    sem = pltpu.SemaphoreType.DMA((n, DEPTH))
    outs = pl.pallas_call(
        body, name=name, out_shape=[jax.ShapeDtypeStruct(b.shape, b.dtype) for b in bufs], in_specs=[ANY] * n, out_specs=[ANY] * n,
        input_output_aliases={k: k for k in range(n)},
        scratch_shapes=[sem, sem], compiler_params=pltpu.CompilerParams(has_side_effects=True),
    )(*bufs)
    return list(outs)


def _all_reduce_small(buf, name):
    R = buf.shape[0]
    offs = [(dx, dy, dc) for dx in (0, 1) for dy in (0, 1) for dc in (0, 1)][1:]

    def body(in_ref, out_ref, token_ref, gather_ref, ssem, rsem):
        token_ref[...] = jnp.zeros_like(token_ref)
        x, y, c = lax.axis_index("x"), lax.axis_index("y"), lax.axis_index("c")
        me = 4 * x + 2 * y + c
        gather_ref[me] = in_ref[...]
        flip = lambda v, d: 1 - v if d else v
        peers = [(flip(x, dx), flip(y, dy), flip(c, dc)) for dx, dy, dc in offs]
        cps = [_remote(in_ref, gather_ref.at[me], ssem.at[k], rsem.at[k], peers[k]) for k in range(N_DEV - 1)]
        for cp in cps:
            cp.start()
        for k, (px, py, pc) in enumerate(peers):
            _remote(in_ref, gather_ref.at[4 * px + 2 * py + pc], ssem.at[k], rsem.at[k], peers[k]).wait_recv()
        acc = gather_ref[0]
        for s in range(1, N_DEV):
            acc = acc + gather_ref[s]
        out_ref[...] = acc
        for cp in cps:
            cp.wait_send()

    vm = pl.BlockSpec(memory_space=pltpu.VMEM)
    return pl.pallas_call(
        body, name=name, out_shape=[jax.ShapeDtypeStruct((R, 128), F32), jax.ShapeDtypeStruct((8, 128), F32)], in_specs=[vm], out_specs=[vm, vm],
        scratch_shapes=[pltpu.VMEM((N_DEV, R, 128), F32), pltpu.SemaphoreType.DMA((N_DEV - 1,)), pltpu.SemaphoreType.DMA((N_DEV - 1,))],
        compiler_params=pltpu.CompilerParams(vmem_limit_bytes=VMEM_LIMIT),
    )(buf)


def _row_tile(rows, cols, n_arrays):
    budget = 20 * 1024 * 1024 // (n_arrays * 2 * cols * 4)
    tr = rows
    while tr > budget or tr % 16:
        assert tr % 2 == 0, (rows, cols)
        tr //= 2
    return tr


def _add_pairs(g, got, place, name):
    _, _, rows, cols = g.shape
    tr = _row_tile(rows, cols, 3)

    def body(place_ref, a_ref, b_ref, o_ref):
        del place_ref
        o_ref[...] = (a_ref[...].astype(F32) + b_ref[...].astype(F32)).astype(BF16)

    spec = pl.BlockSpec((None, tr, cols), lambda q, i, p: (q, i, 0))
    grid_spec = pltpu.PrefetchScalarGridSpec(
        num_scalar_prefetch=1, grid=(N_CHIPS, rows // tr),
        in_specs=[pl.BlockSpec((None, None, tr, cols), lambda q, i, p: (q, p[1], i, 0)), spec], out_specs=spec)
    return pl.pallas_call(body, name=name, grid_spec=grid_spec, out_shape=jax.ShapeDtypeStruct((N_CHIPS, rows, cols), BF16),
                          compiler_params=_params("parallel", "parallel"))(place, g, got)


def _sum_chips(own, recv, place, l, buf, name, after):
    _, rows, cols = own.shape
    tr = _row_tile(rows, cols, 4)

    def body(place_ref, own_ref, recv_ref, *rest):
        chip = place_ref[0]
        acc = own_ref[...].astype(F32)
        for j in range(1, N_CHIPS):
            acc = acc + recv_ref[lax.rem(chip + j, N_CHIPS)].astype(F32)
        rest[-1][...] = acc

    in_specs = [pl.BlockSpec((None, tr, cols), lambda i, p: (p[0], i, 0)), pl.BlockSpec((N_CHIPS, tr, cols), lambda i, p: (0, i, 0)), ANY]
    args = [place, own, recv, after]
    aliases = {}
    if buf is not None:
        in_specs.append(ANY)
        args.append(buf)
        aliases = {4: 0}
    grid_spec = pltpu.PrefetchScalarGridSpec(
        num_scalar_prefetch=1, grid=(rows // tr,), in_specs=in_specs,
        out_specs=pl.BlockSpec((None, None, tr, cols), lambda i, p: (l, p[1], i, 0)))
    return pl.pallas_call(body, name=name, grid_spec=grid_spec, out_shape=jax.ShapeDtypeStruct((DEPTH, 2, rows, cols), F32),
                          input_output_aliases=aliases, compiler_params=_params("parallel"))(*args)


def _adamw(w, g, m, v, name):
    shape = w.shape
    lead, (rows, cols) = shape[:-2], shape[-2:]
    tr = _row_tile(rows, cols, 7)

    def body(w_ref, g_ref, m_ref, v_ref, d_ref, mo_ref, vo_ref):
        gv = g_ref[...]
        mn = ADAM_B1 * m_ref[...] + (1.0 - ADAM_B1) * gv
        vn = ADAM_B2 * v_ref[...] + (1.0 - ADAM_B2) * (gv * gv)
        m_hat = mn / (1.0 - ADAM_B1 ** ADAM_STEP)
        v_hat = vn / (1.0 - ADAM_B2 ** ADAM_STEP)
        d_ref[...] = -ADAM_LR * (m_hat / (jnp.sqrt(v_hat) + ADAM_EPS) + ADAM_WD * w_ref[...])
        mo_ref[...] = mn
        vo_ref[...] = vn

    spec = pl.BlockSpec((None,) * len(lead) + (tr, cols), lambda *idx: (*idx, 0))
    grid = lead + (rows // tr,)
    return list(pl.pallas_call(body, name=name, grid=grid, in_specs=[spec] * 4, out_specs=[spec] * 3,
                               out_shape=[jax.ShapeDtypeStruct(shape, F32)] * 3,
                               compiler_params=_params(*(["parallel"] * len(grid))))(w, g, m, v))


def _pack(arrays):
    parts = []
    for t in arrays:
        f = t.reshape(-1)
        parts.append(jnp.pad(f, (0, (-f.shape[0]) % 128)))
    f = jnp.concatenate(parts)
    f = jnp.pad(f, (0, (-f.shape[0]) % 2048))
    return f.reshape(-1, 128)


def _unpack(buf, shapes):
    f = buf.reshape(-1)
    out, off = [], 0
    for s in shapes:
        n = math.prod(s)
        out.append(f[off:off + n].reshape(s))
        off += n + (-n) % 128
    return out


SMALL = ("norm_mix", "sg_ln_g", "sg_ln_b", "sg_w", "sg_b", "cv_w", "cv_b", "cv_ln_g", "cv_ln_b", "attn_sinks", "sc_w", "norm_ffn", "norm_final")
ORDER = ("norm_mix", "w_in", "sg_ln_g", "sg_ln_b", "sg_w", "sg_b", "cv_w", "cv_b", "cv_ln_g", "cv_ln_b", "attn_sinks", "sc_w",
         "w_branch", "w_out", "norm_ffn", "w_gate_up", "w_down", "norm_final")


def kernel(x, norm_mix, w_in, sg_ln_g, sg_ln_b, sg_w, sg_b, cv_w, cv_b, cv_ln_g, cv_ln_b, attn_sinks, sc_w, w_branch, w_out, norm_ffn, w_gate_up, w_down, norm_final, loss_target, m_norm_mix, m_w_in, m_sg_ln_g, m_sg_ln_b, m_sg_w, m_sg_b, m_cv_w, m_cv_b, m_cv_ln_g, m_cv_ln_b, m_attn_sinks, m_sc_w, m_w_branch, m_w_out, m_norm_ffn, m_w_gate_up, m_w_down, m_norm_final, v_norm_mix, v_w_in, v_sg_ln_g, v_sg_ln_b, v_sg_w, v_sg_b, v_cv_w, v_cv_b, v_cv_ln_g, v_cv_ln_b, v_attn_sinks, v_sc_w, v_w_branch, v_w_out, v_norm_ffn, v_w_gate_up, v_w_down, v_norm_final):
    W = dict(norm_mix=norm_mix, w_in=w_in, sg_ln_g=sg_ln_g, sg_ln_b=sg_ln_b, sg_w=sg_w, sg_b=sg_b, cv_w=cv_w, cv_b=cv_b, cv_ln_g=cv_ln_g,
             cv_ln_b=cv_ln_b, attn_sinks=attn_sinks, sc_w=sc_w, w_branch=w_branch, w_out=w_out, norm_ffn=norm_ffn, w_gate_up=w_gate_up,
             w_down=w_down, norm_final=norm_final)
    M = dict(norm_mix=m_norm_mix, w_in=m_w_in, sg_ln_g=m_sg_ln_g, sg_ln_b=m_sg_ln_b, sg_w=m_sg_w, sg_b=m_sg_b, cv_w=m_cv_w, cv_b=m_cv_b,
             cv_ln_g=m_cv_ln_g, cv_ln_b=m_cv_ln_b, attn_sinks=m_attn_sinks, sc_w=m_sc_w, w_branch=m_w_branch, w_out=m_w_out,
             norm_ffn=m_norm_ffn, w_gate_up=m_w_gate_up, w_down=m_w_down, norm_final=m_norm_final)
    V = dict(norm_mix=v_norm_mix, w_in=v_w_in, sg_ln_g=v_sg_ln_g, sg_ln_b=v_sg_ln_b, sg_w=v_sg_w, sg_b=v_sg_b, cv_w=v_cv_w, cv_b=v_cv_b,
             cv_ln_g=v_cv_ln_g, cv_ln_b=v_cv_ln_b, attn_sinks=v_attn_sinks, sc_w=v_sc_w, w_branch=v_w_branch, w_out=v_w_out,
             norm_ffn=v_norm_ffn, w_gate_up=v_w_gate_up, w_down=v_w_down, norm_final=v_norm_final)
    mx, my, mc = lax.axis_index("x"), lax.axis_index("y"), lax.axis_index("c")
    chip = 2 * mx + my

    place = jnp.stack([chip, mc]).astype(jnp.int32)
    tables = _rope_tables(x.shape[1])
    land_shapes = [(N_CHIPS,) + HALF_SHAPE[n] for n in BIG]
    part_shapes = {n: (N_CHIPS,) + HALF_SHAPE[n][1:] for n in BIG}

    def shards_of(l, tok):
        return [(W[n][l] + tok[0, 0]).astype(BF16).reshape(HALF_SHAPE[n]) for n in BIG]

    def finish_gather(tag, handle, after):
        srcs, lands = _ici_wait(handle, after, f"ag_wait{tag}")
        return _ag_pair(srcs, lands, f"ag_pair{tag}")[0]

    def mix_weights(l, g_in):
        return dict(w_in=_w_in_layout(g_in[0].reshape(N_CHIPS, D_MODEL, W_IN_SHARD)), norm_mix=norm_mix[l][None], norm_ffn=norm_ffn[l][None],
                    mixer=_mixer_params(l, sg_ln_g, sg_ln_b, sg_w, sg_b, cvw_full, cv_b, cv_ln_g, cv_ln_b, attn_sinks, scw_full))

    def rest_weights(lw, g_rest):
        G = dict(zip(BIG[1:], g_rest))
        lw.update(w_branch=G["w_branch"].reshape(N_CHIPS, N_BRANCH, HALF, 256), w_out=G["w_out"].reshape(D_MODEL, D_MODEL),
                  w_gate_up=G["w_gate_up"].reshape(N_CHIPS, D_MODEL, GU_SHARD), w_down=G["w_down"].reshape(D_FF, D_MODEL))

    def shard_major(g):
        t = dict(g)
        if "w_in" in t:
            t["w_in"] = _w_in_unlayout(t["w_in"])
        return {n: t[n].reshape((N_CHIPS,) + HALF_SHAPE[n]) for n in BIG if n in t}

    def pair_sums(tag, g):
        names = list(g)
        got = _rs_pair([g[n] for n in names], f"rs_pair{tag}")
        return names, [_add_pairs(g[n], got[k], place, f"rs_add{tag}_{n}") for k, n in enumerate(names)]

    zero_tok = jnp.zeros((8, 128), F32)
    south = (mc == 0).astype(F32)
    cvw_z = lax.dynamic_update_slice(jnp.zeros((DEPTH, CV_KERNEL, HALF), F32), cv_w * south, (0, 0, chip * 128))
    scw_z = lax.dynamic_update_slice(jnp.zeros((DEPTH, SC_KERNEL, HALF), F32), sc_w * south, (0, 0, chip * 128))
    small_w, tok = _all_reduce_small(_pack([cvw_z, scw_z]), "ag_small")
    cvw_full, scw_full = _unpack(small_w, [cvw_z.shape, scw_z.shape])

    handles = []
    for l in range(DEPTH):
        for tag, sl in (("in", slice(0, 1)), ("rest", slice(1, NB))):
            h, tok = _ici_start("gather", shards_of(l, tok)[sl], land_shapes[sl], f"ag_start{l}{tag}")
            handles.append(h)
    x_l, saved = x[0], []
    for l in range(DEPTH):
        lw = mix_weights(l, finish_gather(f"{l}in", handles[2 * l], x_l if l else tok))
        mixed = _fwd_layer_mix(l, x_l, lw, tables)
        rest_weights(lw, finish_gather(f"{l}rest", handles[2 * l + 1], mixed[2]))
        x_l, sv = _fwd_layer_rest(l, x_l, mixed, lw)
        saved.append((lw, sv))
    (lw0, sv0), (lw1, sv1) = saved
    dx, d_nfinal, loss = _final_loss(x_l, norm_final[None], loss_target[0], 256, "final_loss")

    lw1["after"] = zero_tok
    carry, g_ffn1 = _bwd_layer_ffn(1, dx, lw1, sv1)
    dx, g_mix1 = _bwd_layer_mix(1, carry, lw1, sv1, tables)
    names1, part1 = pair_sums("1", shard_major({**g_ffn1, **g_mix1}))
    hr1, tok = _ici_start("scatter", part1, [part_shapes[n] for n in names1], "rs_start1")

    lw0["after"] = tok
    carry, g_ffn0 = _bwd_layer_ffn(0, dx, lw0, sv0)
    names_a, part_a = pair_sums("0a", shard_major(g_ffn0))
    _, recv1 = _ici_wait(hr1, part_a[0], "rs_wait1")
    hra, tok = _ici_start("scatter", part_a, [part_shapes[n] for n in names_a], "rs_start0a")

    lw0["mixer"] = [lw0["mixer"][0] + tok[0, 0]] + lw0["mixer"][1:]
    dx, g_mix0 = _bwd_layer_mix(0, carry, lw0, sv0, tables)
    _, recv_a = _ici_wait(hra, dx, "rs_wait0a")

    grads = [{**g_ffn0, **g_mix0}, {**g_ffn1, **g_mix1}]
    small_g = {n: jnp.stack([grads[l][n] for l in range(DEPTH)]) for n in SMALL if n != "norm_final"}
    small_g["norm_final"] = d_nfinal[0]
    small_sum, tok = _all_reduce_small(_pack([small_g[n] for n in SMALL] + [loss]), "ar_small")
    red = _unpack(small_sum, [small_g[n].shape for n in SMALL] + [loss.shape])
    small_red = dict(zip(SMALL, red[:-1]))
    loss_out = red[-1][0, 0]
    for n in ("cv_w", "sc_w"):
        small_red[n] = lax.dynamic_slice_in_dim(small_red[n], chip * 128, 128, axis=2)

    g_mix0["w_in"] = g_mix0["w_in"] + tok[0, 0].astype(BF16)
    names_b, part_b = pair_sums("0b", shard_major(g_mix0))
    hrb, tok = _ici_start("scatter", part_b, [part_shapes[n] for n in names_b], "rs_start0b")
    bufs = {n: _sum_chips(part1[k], recv1[k], place, 1, None, f"rs_sum1_{n}", tok) for k, n in enumerate(names1)}
    for k, n in enumerate(names_a):
        bufs[n] = _sum_chips(part_a[k], recv_a[k], place, 0, bufs[n], f"rs_sum0_{n}", tok)
    shared = dict(zip(names_a, _rs_share([bufs[n] for n in names_a], "rs_share_a")))
    upd = {}
    for n in names_a:
        red = shared[n].reshape(W[n].shape)
        upd[n] = [red] + _adamw(W[n], red, M[n], V[n], f"adamw_{n}")
    sw, sg, sm, sv = (_pack([t[n] for n in SMALL]) for t in (W, small_red, M, V))
    packed = _adamw(sw, sg, sm, sv, "adamw_small")
    shapes = [W[n].shape for n in SMALL]
    for n, d, mo, vo in zip(SMALL, *(_unpack(p, shapes) for p in packed)):
        upd[n] = [small_red[n], d, mo, vo]

    _, recv_b = _ici_wait(hrb, upd[names_a[-1]][1], "rs_wait0b")
    for k, n in enumerate(names_b):
        bufs[n] = _sum_chips(part_b[k], recv_b[k], place, 0, bufs[n], f"rs_sum0_{n}", tok)
    shared = dict(zip(names_b, _rs_share([bufs[n] for n in names_b], "rs_share_b")))
    for n in names_b:
        red = shared[n].reshape(W[n].shape)
        upd[n] = [red] + _adamw(W[n], red, M[n], V[n], f"adamw_{n}")

    out = [loss_out, dx[None]]
    for k in range(4):
        out += [upd[n][k] for n in ORDER]
    return tuple(out)
```

```python
import functools
import math

import jax
import jax.numpy as jnp
from jax import lax
from jax.experimental import pallas as pl
from jax.experimental.pallas import tpu as pltpu

F32 = jnp.float32
BF16 = jnp.bfloat16

D_MODEL = 1024
DEPTH = 2
HALF = 512
SG_CHUNK = 128
SG_GROUPS = 4
CV_KERNEL = 31
HEAD_DIM = 64
N_Q_HEADS = 8
N_KV_HEADS = 2
Q_PER_KV = N_Q_HEADS // N_KV_HEADS
WINDOW = 128
ROPE_THETA = 10000.0
SC_KERNEL = 3
N_BRANCH = 4
D_FF = 2816
EPS = 1e-6
N_CHIPS = 4
N_DEV = 8

MIX_W = 4352
GATE_W = N_BRANCH * D_MODEL
PROJ_PAD = 2 * MIX_W
W_IN_SHARD = 2112
GU_SHARD = 1408
HALO = 128
CV_PAD = 32

ADAM_LR = 0.001
ADAM_B1 = 0.9
ADAM_B2 = 0.999
ADAM_EPS = 1e-08
ADAM_WD = 0.01
ADAM_STEP = 10

VMEM_LIMIT = 56 * 1024 * 1024
INV_SQRT2 = 1.0 / math.sqrt(2.0)
INV_SQRT_2PI = 1.0 / math.sqrt(2.0 * math.pi)
NEG_BIG = -1e30
MESH = pl.DeviceIdType.MESH

C_ZA, C_ZB, C_Q, C_K, C_V, C_ZD = 0, 1024, 2048, 2560, 2688, 2816


def _params(*sem):
    return pltpu.CompilerParams(dimension_semantics=sem, vmem_limit_bytes=VMEM_LIMIT)


def _sig(v):
    return 1.0 / (1.0 + jnp.exp(-v))


def _dot(a, b):
    return jnp.dot(a, b, preferred_element_type=F32)


def _dot_nt(a, b):
    return lax.dot_general(a, b, (((1,), (1,)), ((), ())), preferred_element_type=F32)


def _dot_tn(a, b):
    return lax.dot_general(a, b, (((0,), (0,)), ((), ())), preferred_element_type=F32)


def _full(shape):
    nd = len(shape)
    return pl.BlockSpec(shape, lambda *_: (0,) * nd)


def _rms_mm(x, g, w, tm, tn, name):
    T = x.shape[0]
    if w.ndim == 3:
        tn = w.shape[2]
        N = w.shape[0] * tn
        wspec = pl.BlockSpec((None, D_MODEL, tn), lambda i, j: (j, 0, 0))
    else:
        N = w.shape[1]
        wspec = pl.BlockSpec((D_MODEL, tn), lambda i, j: (0, j))

    def body(x_ref, g_ref, w_ref, o_ref, xn_ref):
        @pl.when(pl.program_id(1) == 0)
        def _():
            xv = x_ref[...]
            r = lax.rsqrt(jnp.mean(xv * xv, axis=-1, keepdims=True) + EPS)
            xn_ref[...] = (xv * r * g_ref[...]).astype(BF16)

        o_ref[...] = _dot(xn_ref[...], w_ref[...]).astype(BF16)

    return pl.pallas_call(
        body, name=name, grid=(T // tm, N // tn),
        in_specs=[pl.BlockSpec((tm, D_MODEL), lambda i, j: (i, 0)), _full((1, D_MODEL)), wspec],
        out_specs=[pl.BlockSpec((tm, tn), lambda i, j: (i, j)), pl.BlockSpec((tm, D_MODEL), lambda i, j: (i, 0))],
        out_shape=[jax.ShapeDtypeStruct((T, N), BF16), jax.ShapeDtypeStruct((T, D_MODEL), BF16)],
        compiler_params=_params("parallel", "arbitrary"),
    )(x, g, w)


def _merge_fwd(x, ys, proj, wb, wo, tm, name):
    T = x.shape[0]

    def body(x_ref, ys_ref, zg_ref, wb_ref, wo_ref, xo_ref, mg_ref):
        merged = None
        for n in range(N_BRANCH):
            yn = ys_ref[:, n * HALF:(n + 1) * HALF]
            br = jnp.concatenate([_dot(yn, wb_ref[s, n]) for s in range(N_CHIPS)], axis=1)
            t = _sig(zg_ref[:, n * D_MODEL:(n + 1) * D_MODEL].astype(F32)) * br
            merged = t if merged is None else merged + t
        mb = merged.astype(BF16)
        mg_ref[...] = mb
        xo_ref[...] = x_ref[...] + _dot(mb, wo_ref[...])

    return pl.pallas_call(
        body, name=name, grid=(T // tm,),
        in_specs=[pl.BlockSpec((tm, D_MODEL), lambda i: (i, 0)), pl.BlockSpec((tm, N_BRANCH * HALF), lambda i: (i, 0)),
                  pl.BlockSpec((tm, GATE_W), lambda i: (i, 0)), _full(wb.shape), _full(wo.shape)],
        out_specs=[pl.BlockSpec((tm, D_MODEL), lambda i: (i, 0)), pl.BlockSpec((tm, D_MODEL), lambda i: (i, 0))],
        out_shape=[jax.ShapeDtypeStruct((T, D_MODEL), F32), jax.ShapeDtypeStruct((T, D_MODEL), BF16)],
        compiler_params=_params("parallel"),
    )(x, ys, proj, wb, wo)


def _ffn_down(xm, gu, wd, tm, name):
    T = xm.shape[0]

    def body(x_ref, gu_ref, wd_ref, o_ref):
        g = gu_ref[:, :D_FF].astype(F32)
        u = gu_ref[:, D_FF:].astype(F32)
        act = (g * _sig(g) * u).astype(BF16)
        o_ref[...] = x_ref[...] + _dot(act, wd_ref[...])

    return pl.pallas_call(
        body, name=name, grid=(T // tm,),
        in_specs=[pl.BlockSpec((tm, D_MODEL), lambda i: (i, 0)), pl.BlockSpec((tm, 2 * D_FF), lambda i: (i, 0)), _full(wd.shape)],
        out_specs=pl.BlockSpec((tm, D_MODEL), lambda i: (i, 0)),
        out_shape=jax.ShapeDtypeStruct((T, D_MODEL), F32),
        compiler_params=_params("parallel"),
    )(xm, gu, wd)


def _final_loss(x, g, tgt, tm, name):
    T = x.shape[0]

    def body(x_ref, g_ref, t_ref, dx_ref, dg_ref, ls_ref):
        @pl.when(pl.program_id(0) == 0)
        def _():
            dg_ref[...] = jnp.zeros_like(dg_ref)
            ls_ref[...] = jnp.zeros_like(ls_ref)

        xv = x_ref[...]
        gv = g_ref[...]
        r = lax.rsqrt(jnp.mean(xv * xv, axis=-1, keepdims=True) + EPS)
        xh = xv * r
        diff = xh * gv - t_ref[...]
        ls_ref[...] += jnp.full(ls_ref.shape, 0.5 / D_MODEL, F32) * jnp.sum(diff * diff)
        dy = diff * (1.0 / D_MODEL)
        dxh = dy * gv
        dx_ref[...] = r * (dxh - xh * jnp.mean(dxh * xh, axis=-1, keepdims=True))
        dg_ref[...] += jnp.sum(dy * xh, axis=0, keepdims=True)

    return pl.pallas_call(
        body, name=name, grid=(T // tm,),
        in_specs=[pl.BlockSpec((tm, D_MODEL), lambda i: (i, 0)), _full((1, D_MODEL)), pl.BlockSpec((tm, D_MODEL), lambda i: (i, 0))],
        out_specs=[pl.BlockSpec((tm, D_MODEL), lambda i: (i, 0)), _full((1, D_MODEL)), _full((1, 128))],
        out_shape=[jax.ShapeDtypeStruct((T, D_MODEL), F32), jax.ShapeDtypeStruct((1, D_MODEL), F32), jax.ShapeDtypeStruct((1, 128), F32)],
        compiler_params=_params("arbitrary"),
    )(x, g, tgt)


def _swiglu_bwd(dx, gu, wd, tm, name, after):
    T = dx.shape[0]

    def body(dx_ref, gu_ref, wd_ref, after_ref, dgu_ref, act_ref):
        del after_ref
        dact = _dot_nt(dx_ref[...].astype(BF16), wd_ref[...])
        g = gu_ref[:, :D_FF].astype(F32)
        u = gu_ref[:, D_FF:].astype(F32)
        s = _sig(g)
        silu = g * s
        act_ref[...] = (silu * u).astype(BF16)
        dgu_ref[:, :D_FF] = (dact * u * (s + silu * (1.0 - s))).astype(BF16)
        dgu_ref[:, D_FF:] = (dact * silu).astype(BF16)

    return pl.pallas_call(
        body, name=name, grid=(T // tm,),
        in_specs=[pl.BlockSpec((tm, D_MODEL), lambda i: (i, 0)), pl.BlockSpec((tm, 2 * D_FF), lambda i: (i, 0)), _full(wd.shape),
                  pl.BlockSpec(memory_space=pl.ANY)],
        out_specs=[pl.BlockSpec((tm, 2 * D_FF), lambda i: (i, 0)), pl.BlockSpec((tm, D_FF), lambda i: (i, 0))],
        out_shape=[jax.ShapeDtypeStruct((T, 2 * D_FF), BF16), jax.ShapeDtypeStruct((T, D_FF), BF16)],
        compiler_params=_params("parallel"),
    )(dx, gu, wd, after)


def _mm_tn(a, b, grid, a_block, a_map, b_block, b_map, o_shape, o_block, o_map, name):
    gk = grid[2]
    tm = [d for d in a_block if d is not None][-1]
    tn = [d for d in b_block if d is not None][-1]

    def body(a_ref, b_ref, o_ref, acc_ref):
        k = pl.program_id(2)
        p = _dot_tn(a_ref[...].astype(BF16), b_ref[...].astype(BF16))

        @pl.when(k == 0)
        def _():
            acc_ref[...] = p

        @pl.when(k > 0)
        def _():
            acc_ref[...] += p

        @pl.when(k == gk - 1)
        def _():
            o_ref[...] = acc_ref[...].astype(o_ref.dtype)

    return pl.pallas_call(
        body, name=name, grid=grid,
        in_specs=[pl.BlockSpec(a_block, a_map), pl.BlockSpec(b_block, b_map)],
        out_specs=pl.BlockSpec(o_block, o_map),
        out_shape=jax.ShapeDtypeStruct(o_shape, BF16),
        scratch_shapes=[pltpu.VMEM((tm, tn), F32)],
        compiler_params=_params("parallel", "parallel", "arbitrary"),
    )(a, b)


def _mm_nt_rmsbwd(a, w, x, g, dres, tm, tk, name):
    T = x.shape[0]
    if w.ndim == 3:
        tk = w.shape[2]
        gk = w.shape[0]
        wspec = pl.BlockSpec((None, D_MODEL, tk), lambda i, k: (k, 0, 0))
    else:
        gk = w.shape[1] // tk
        wspec = pl.BlockSpec((D_MODEL, tk), lambda i, k: (0, k))

    def body(a_ref, w_ref, x_ref, g_ref, r_ref, dx_ref, dg_ref, acc_ref):
        i, k = pl.program_id(0), pl.program_id(1)
        p = _dot_nt(a_ref[...], w_ref[...])

        @pl.when(k == 0)
        def _():
            acc_ref[...] = p

        @pl.when(k > 0)
        def _():
            acc_ref[...] += p

        @pl.when(jnp.logical_and(i == 0, k == 0))
        def _():
            dg_ref[...] = jnp.zeros_like(dg_ref)

        @pl.when(k == gk - 1)
        def _():
            dh = acc_ref[...]
            xv = x_ref[...]
            r = lax.rsqrt(jnp.mean(xv * xv, axis=-1, keepdims=True) + EPS)
            xh = xv * r
            dxh = dh * g_ref[...]
            dx_ref[...] = r_ref[...] + r * (dxh - xh * jnp.mean(dxh * xh, axis=-1, keepdims=True))
            dg_ref[...] += jnp.sum(dh * xh, axis=0, keepdims=True)

    return pl.pallas_call(
        body, name=name, grid=(T // tm, gk),
        in_specs=[pl.BlockSpec((tm, tk), lambda i, k: (i, k)), wspec, pl.BlockSpec((tm, D_MODEL), lambda i, k: (i, 0)),
                  _full((1, D_MODEL)), pl.BlockSpec((tm, D_MODEL), lambda i, k: (i, 0))],
        out_specs=[pl.BlockSpec((tm, D_MODEL), lambda i, k: (i, 0)), _full((1, D_MODEL))],
        out_shape=[jax.ShapeDtypeStruct((T, D_MODEL), F32), jax.ShapeDtypeStruct((1, D_MODEL), F32)],
        scratch_shapes=[pltpu.VMEM((tm, D_MODEL), F32)],
        compiler_params=_params("arbitrary", "arbitrary"),
    )(a, w, x, g, dres)


def _merge_bwd(dxm, ys, proj, wb, wo, tm, name):
    T = dxm.shape[0]

    def body(dx_ref, ys_ref, zg_ref, wb_ref, wo_ref, dys_ref, dbr_ref, dp_ref):
        dmerged = _dot_nt(dx_ref[...].astype(BF16), wo_ref[...])
        for n in range(N_BRANCH):
            yn = ys_ref[:, n * HALF:(n + 1) * HALF]
            br = jnp.concatenate([_dot(yn, wb_ref[s, n]) for s in range(N_CHIPS)], axis=1)
            gt = _sig(zg_ref[:, n * D_MODEL:(n + 1) * D_MODEL].astype(F32))
            dbr = (gt * dmerged).astype(BF16)
            dbr_ref[:, n * D_MODEL:(n + 1) * D_MODEL] = dbr
            dp_ref[:, n * D_MODEL:(n + 1) * D_MODEL] = (dmerged * br * gt * (1.0 - gt)).astype(BF16)
            dy = None
            for s in range(N_CHIPS):
                t = _dot_nt(dbr[:, s * 256:(s + 1) * 256], wb_ref[s, n])
                dy = t if dy is None else dy + t
            dys_ref[:, n * HALF:(n + 1) * HALF] = dy.astype(BF16)
        dp_ref[:, GATE_W:] = jnp.zeros((tm, MIX_W - GATE_W), BF16)

    return pl.pallas_call(
        body, name=name, grid=(T // tm,),
        in_specs=[pl.BlockSpec((tm, D_MODEL), lambda i: (i, 0)), pl.BlockSpec((tm, N_BRANCH * HALF), lambda i: (i, 0)),
                  pl.BlockSpec((tm, GATE_W), lambda i: (i, 0)), _full(wb.shape), _full(wo.shape)],
        out_specs=[pl.BlockSpec((tm, N_BRANCH * HALF), lambda i: (i, 0)), pl.BlockSpec((tm, GATE_W), lambda i: (i, 0)),
                   pl.BlockSpec((tm, MIX_W), lambda i: (i, 0))],
        out_shape=[jax.ShapeDtypeStruct((T, N_BRANCH * HALF), BF16), jax.ShapeDtypeStruct((T, GATE_W), BF16),
                   jax.ShapeDtypeStruct((T, PROJ_PAD), BF16)],
        compiler_params=_params("parallel"),
    )(dxm, ys, proj, wb, wo)


def _gelu(v):
    return 0.5 * v * (1.0 + lax.erf(v * INV_SQRT2))


def _gelu_grad(v):
    return 0.5 * (1.0 + lax.erf(v * INV_SQRT2)) + v * jnp.exp(-0.5 * v * v) * INV_SQRT_2PI


def _rot_half(t):
    w = t.shape[1]
    lane = lax.broadcasted_iota(jnp.int32, t.shape, 1)
    return jnp.where((lane % HEAD_DIM) < HEAD_DIM // 2, pltpu.roll(t, w - HEAD_DIM // 2, 1), pltpu.roll(t, HEAD_DIM // 2, 1))


def _rope(t, cos, sin_signed):
    return t * cos + _rot_half(t) * sin_signed


def _rope_t(d, cos, sin_signed):
    return d * cos + _rot_half(d * sin_signed)


def _ln_fwd(v, g, b):
    mu = jnp.mean(v, axis=-1, keepdims=True)
    vc = v - mu
    r = lax.rsqrt(jnp.mean(vc * vc, axis=-1, keepdims=True) + EPS)
    vh = vc * r
    return vh * g + b, vh, r


def _ln_bwd(dn, vh, r, g):
    dvh = dn * g
    return r * (dvh - jnp.mean(dvh, axis=-1, keepdims=True) - vh * jnp.mean(dvh * vh, axis=-1, keepdims=True))


def _sublane_shifts(sh_ref, rows):
    for b in range(1, 8):
        sh_ref[b, 0:rows - 8, :] = sh_ref[0, pl.ds(b, rows - 8), :]


def _tap(sh_ref, off, n):
    return sh_ref[off % 8, pl.ds(off - off % 8, n), :]


def _tril_mask():
    return lax.broadcasted_iota(jnp.int32, (SG_CHUNK, SG_CHUNK), 0) >= lax.broadcasted_iota(jnp.int32, (SG_CHUNK, SG_CHUNK), 1)


def _attn_probs(qs, kh, sink_col, first_ok):
    s = _dot_nt(qs, kh) * (HEAD_DIM ** -0.5)
    row = lax.broadcasted_iota(jnp.int32, s.shape, 0) % WINDOW
    col = lax.broadcasted_iota(jnp.int32, s.shape, 1)
    valid = (col > row) & (col <= row + WINDOW) & ((col >= WINDOW) | first_ok)
    s = jnp.where(valid, s, NEG_BIG)
    m = jnp.maximum(jnp.max(s, axis=-1, keepdims=True), sink_col)
    p = jnp.where(valid, jnp.exp(s - m), 0.0)
    es = jnp.exp(sink_col - m)
    inv = 1.0 / (jnp.sum(p, axis=-1, keepdims=True) + es)
    return p * inv, es * inv


def _sink_col(sinks_ref, h):
    return jnp.concatenate([jnp.broadcast_to(sinks_ref[:, h * Q_PER_KV + g:h * Q_PER_KV + g + 1], (WINDOW, 1))
                            for g in range(Q_PER_KV)], axis=0)


def _mixer_in_specs(TB, nb):
    r = TB // HALO
    last = nb * r - 1
    cur = pl.BlockSpec((TB, MIX_W), lambda i: (i, 1))
    prev = pl.BlockSpec((HALO, MIX_W), lambda i: (jnp.maximum(i * r - 1, 0), 1))
    nxt = pl.BlockSpec((HALO, MIX_W), lambda i: (jnp.minimum((i + 1) * r, last), 1))
    tcur = pl.BlockSpec((TB, 128), lambda i: (i, 0))
    tprev = pl.BlockSpec((HALO, 128), lambda i: (jnp.maximum(i * r - 1, 0), 0))
    tnxt = pl.BlockSpec((HALO, 128), lambda i: (jnp.minimum((i + 1) * r, last), 0))
    return cur, prev, nxt, tcur, tprev, tnxt


def _mixer_param_specs():
    return [_full((1, HALF)), _full((1, HALF)), _full((SG_GROUPS, SG_CHUNK, SG_CHUNK)), _full((SG_CHUNK, 128)),
            _full((32, HALF)), _full((1, HALF)), _full((1, HALF)), _full((1, HALF)), _full((1, 128)), _full((8, HALF))]


def _mixers_fwd(proj, cos_t, sin_t, mp, TB, name):
    T = proj.shape[0]
    nb = T // TB
    r = TB // HALO
    cur, prev, _, tcur, tprev, _ = _mixer_in_specs(TB, nb)

    def body(zc_ref, zp_ref, cc_ref, sc_ref, cp_ref, sp_ref,
             lg_ref, lb_ref, sgw_ref, sgb_ref, cvw_ref, cvb_ref, cvg_ref, cvbb_ref, sinks_ref, scw_ref,
             ys_ref, scr_ref, k_ref, v_ref, sh_ref):
        i = pl.program_id(0)
        pm = (i > 0).astype(F32)

        def colsE(c0, c1):
            return jnp.concatenate([zp_ref[:, c0:c1].astype(F32) * pm, zc_ref[:, c0:c1].astype(F32)], axis=0)

        a = _gelu(zc_ref[:, C_ZA:C_ZA + 2 * HALF].astype(F32))
        u = a[:, :HALF]
        vn, _, _ = _ln_fwd(a[:, HALF:], lg_ref[...], lb_ref[...])
        vnb = vn.astype(BF16)
        tril = _tril_mask()
        for g in range(SG_GROUPS):
            wt = jnp.where(tril, sgw_ref[g], 0.0).astype(BF16)
            for ci in range(r):
                rows = slice(ci * SG_CHUNK, (ci + 1) * SG_CHUNK)
                cols = slice(g * 128, (g + 1) * 128)
                mixed = _dot(wt, vnb[rows, cols]) + sgb_ref[:, g:g + 1]
                ys_ref[rows, g * 128:(g + 1) * 128] = (u[rows, cols] * mixed).astype(BF16)

        def colsB(c0, c1):
            return jnp.concatenate([zp_ref[HALO - CV_PAD:, c0:c1].astype(F32) * pm, zc_ref[:, c0:c1].astype(F32)], axis=0)

        sh_ref[0] = colsB(C_ZB, C_ZB + HALF) * _sig(colsB(C_ZB + HALF, C_ZB + 2 * HALF))
        _sublane_shifts(sh_ref, TB + CV_PAD)
        c = jnp.broadcast_to(cvb_ref[...], (TB, HALF))
        for k in range(CV_KERNEL):
            c = c + cvw_ref[k:k + 1, :] * _tap(sh_ref, CV_PAD - (CV_KERNEL - 1) + k, TB)
        n, _, _ = _ln_fwd(c, cvg_ref[...], cvbb_ref[...])
        ys_ref[:, HALF:2 * HALF] = (n * _sig(n)).astype(BF16)

        zd = colsE(C_ZD + HALF, C_ZD + 3 * HALF)
        scr_ref[...] = zd[:, :HALF] * zd[:, HALF:]
        cv = None
        for k in range(SC_KERNEL):
            t = scw_ref[k:k + 1, :] * scr_ref[pl.ds(HALO - (SC_KERNEL - 1) + k, TB), :]
            cv = t if cv is None else cv + t
        ys_ref[:, 3 * HALF:4 * HALF] = (zc_ref[:, C_ZD:C_ZD + HALF].astype(F32) * cv).astype(BF16)

        cosE = jnp.concatenate([cp_ref[...], cc_ref[...]], axis=0)
        sinE = jnp.concatenate([sp_ref[...], sc_ref[...]], axis=0)
        k_ref[...] = _rope(colsE(C_K, C_K + 128), cosE, sinE).astype(BF16)
        v_ref[...] = colsE(C_V, C_V + 128).astype(BF16)
        cosC, sinC = cc_ref[...], sc_ref[...]
        q = jnp.concatenate([_rope(zc_ref[:, C_Q + 128 * j:C_Q + 128 * (j + 1)].astype(F32), cosC, sinC)
                             for j in range(4)], axis=1).astype(BF16)
        for qb in range(r):
            first_ok = (i * r + qb) > 0
            for h in range(N_KV_HEADS):
                hc = slice(h * HEAD_DIM, (h + 1) * HEAD_DIM)
                kh = k_ref[qb * WINDOW:qb * WINDOW + 2 * WINDOW, hc]
                vh = v_ref[qb * WINDOW:qb * WINDOW + 2 * WINDOW, hc]
                qs = jnp.concatenate([q[qb * WINDOW:(qb + 1) * WINDOW, (h * Q_PER_KV + g) * HEAD_DIM:(h * Q_PER_KV + g + 1) * HEAD_DIM]
                                      for g in range(Q_PER_KV)], axis=0)
                probs, _ = _attn_probs(qs, kh, _sink_col(sinks_ref, h), first_ok)
                o = _dot(probs.astype(BF16), vh)
                for g in range(Q_PER_KV):
                    c0 = 2 * HALF + (h * Q_PER_KV + g) * HEAD_DIM
                    ys_ref[qb * WINDOW:(qb + 1) * WINDOW, c0:c0 + HEAD_DIM] = o[g * WINDOW:(g + 1) * WINDOW].astype(BF16)

    return pl.pallas_call(
        body, name=name, grid=(nb,),
        in_specs=[cur, prev, tcur, tcur, tprev, tprev] + _mixer_param_specs(),
        out_specs=pl.BlockSpec((TB, 4 * HALF), lambda i: (i, 0)),
        out_shape=jax.ShapeDtypeStruct((T, 4 * HALF), BF16),
        scratch_shapes=[pltpu.VMEM((TB + HALO, HALF), F32), pltpu.VMEM((TB + HALO, 128), BF16), pltpu.VMEM((TB + HALO, 128), BF16),
                        pltpu.VMEM((8, TB + CV_PAD, HALF), F32)],
        compiler_params=_params("parallel"),
    )(proj, proj, cos_t, sin_t, cos_t, sin_t, *mp)


def _mixers_bwd(proj, dys, dproj, cos_t, sin_t, mp, TB, name):
    T = proj.shape[0]
    nb = T // TB
    r = TB // HALO
    RE = TB + 2 * HALO
    RC = TB + HALO
    cur, prev, nxt, tcur, tprev, tnxt = _mixer_in_specs(TB, nb)
    dcur = pl.BlockSpec((TB, 4 * HALF), lambda i: (i, 0))
    dnxt = pl.BlockSpec((HALO, 4 * HALF), lambda i: (jnp.minimum((i + 1) * r, nb * r - 1), 0))

    def body(zc_ref, zp_ref, zn_ref, dyc_ref, dyn_ref, cc_ref, sc_ref, cp_ref, sp_ref, cn_ref, sn_ref,
             lg_ref, lb_ref, sgw_ref, sgb_ref, cvw_ref, cvb_ref, cvg_ref, cvbb_ref, sinks_ref, scw_ref, dp_in_ref,
             dz_ref, dlg_ref, dlb_ref, dsgw_ref, dsgb_ref, dcvw_ref, dcvb_ref, dcvg_ref, dcvbb_ref, dsink_ref, dscw_ref,
             scr_ref, scr2_ref, k_ref, v_ref, dk_ref, dv_ref, dq_ref, sh_ref, sh2_ref):
        del dp_in_ref
        i = pl.program_id(0)
        pm = (i > 0).astype(F32)
        nm = (i < nb - 1).astype(F32)

        @pl.when(i == 0)
        def _():
            for ref in (dlg_ref, dlb_ref, dsgw_ref, dsgb_ref, dcvw_ref, dcvb_ref, dcvg_ref, dcvbb_ref, dsink_ref, dscw_ref):
                ref[...] = jnp.zeros_like(ref)

        def colsE(c0, c1):
            return jnp.concatenate([zp_ref[:, c0:c1].astype(F32) * pm, zc_ref[:, c0:c1].astype(F32),
                                    zn_ref[:, c0:c1].astype(F32)], axis=0)

        def colsC(c0, c1):
            return jnp.concatenate([zc_ref[:, c0:c1].astype(F32), zn_ref[:, c0:c1].astype(F32)], axis=0)

        def dyC(c0, c1):
            return jnp.concatenate([dyc_ref[:, c0:c1].astype(F32), dyn_ref[:, c0:c1].astype(F32) * nm], axis=0)

        za = zc_ref[:, C_ZA:C_ZA + 2 * HALF].astype(F32)
        a = _gelu(za)
        u = a[:, :HALF]
        lg = lg_ref[...]
        vn, vh, rs = _ln_fwd(a[:, HALF:], lg, lb_ref[...])
        vnb = vn.astype(BF16)
        dya = dyc_ref[:, 0:HALF].astype(F32)
        tril = _tril_mask()
        lane128 = lax.broadcasted_iota(jnp.int32, (SG_CHUNK, 128), 1)
        du_parts, dvn_parts = [], []
        for ci in range(r):
            rows = slice(ci * SG_CHUNK, (ci + 1) * SG_CHUNK)
            du_g, dvn_g = [], []
            for g in range(SG_GROUPS):
                cols = slice(g * 128, (g + 1) * 128)
                wt = jnp.where(tril, sgw_ref[g], 0.0).astype(BF16)
                vb = vnb[rows, cols]
                mixed = _dot(wt, vb) + sgb_ref[:, g:g + 1]
                dy_blk = dya[rows, cols]
                du_g.append(dy_blk * mixed)
                dmix = dy_blk * u[rows, cols]
                dmb = dmix.astype(BF16)
                dvn_g.append(_dot_tn(wt, dmb))
                dsgw_ref[g] += jnp.where(tril, _dot_nt(dmb, vb), 0.0)
                dsgb_ref[...] += jnp.where(lane128 == g, jnp.sum(dmix, axis=1, keepdims=True), 0.0)
            du_parts.append(jnp.concatenate(du_g, axis=1))
            dvn_parts.append(jnp.concatenate(dvn_g, axis=1))
        du = jnp.concatenate(du_parts, axis=0) if r > 1 else du_parts[0]
        dvn = jnp.concatenate(dvn_parts, axis=0) if r > 1 else dvn_parts[0]
        dlg_ref[...] += jnp.sum(dvn * vh, axis=0, keepdims=True)
        dlb_ref[...] += jnp.sum(dvn, axis=0, keepdims=True)
        dvv = _ln_bwd(dvn, vh, rs, lg)
        gg = _gelu_grad(za)
        dz_ref[:, C_ZA:C_ZA + HALF] = (du * gg[:, :HALF]).astype(BF16)
        dz_ref[:, C_ZA + HALF:C_ZA + 2 * HALF] = (dvv * gg[:, HALF:]).astype(BF16)

        RB = TB + CV_PAD

        def colsB(c0, c1):
            return jnp.concatenate([zp_ref[HALO - CV_PAD:, c0:c1].astype(F32) * pm, zc_ref[:, c0:c1].astype(F32),
                                    zn_ref[:CV_PAD, c0:c1].astype(F32)], axis=0)

        sh_ref[0] = colsB(C_ZB, C_ZB + HALF) * _sig(colsB(C_ZB + HALF, C_ZB + 2 * HALF))
        _sublane_shifts(sh_ref, RB + CV_PAD)
        c = jnp.broadcast_to(cvb_ref[...], (RB, HALF))
        for k in range(CV_KERNEL):
            c = c + cvw_ref[k:k + 1, :] * _tap(sh_ref, CV_PAD - (CV_KERNEL - 1) + k, RB)
        cvg = cvg_ref[...]
        n, ch, rc = _ln_fwd(c, cvg, cvbb_ref[...])
        sn = _sig(n)
        dyb = jnp.concatenate([dyc_ref[:, HALF:2 * HALF].astype(F32), dyn_ref[:CV_PAD, HALF:2 * HALF].astype(F32) * nm], axis=0)
        dn = dyb * (sn + n * sn * (1.0 - sn))
        dno = dn[:TB]
        dcvg_ref[...] += jnp.sum(dno * ch[:TB], axis=0, keepdims=True)
        dcvbb_ref[...] += jnp.sum(dno, axis=0, keepdims=True)
        dc = _ln_bwd(dn, ch, rc, cvg)
        sh2_ref[0] = dc
        _sublane_shifts(sh2_ref, RB)
        dcvb_ref[...] += jnp.sum(dc[:TB], axis=0, keepdims=True)
        dy0 = None
        for k in range(CV_KERNEL):
            wk = cvw_ref[k:k + 1, :]
            t = wk * _tap(sh2_ref, CV_KERNEL - 1 - k, TB)
            dy0 = t if dy0 is None else dy0 + t
            dcvw_ref[k:k + 1, :] += jnp.sum(dc[:TB] * _tap(sh_ref, CV_PAD - (CV_KERNEL - 1) + k, TB), axis=0, keepdims=True)
        ab = zc_ref[:, C_ZB:C_ZB + HALF].astype(F32)
        sg = _sig(zc_ref[:, C_ZB + HALF:C_ZB + 2 * HALF].astype(F32))
        dz_ref[:, C_ZB:C_ZB + HALF] = (dy0 * sg).astype(BF16)
        dz_ref[:, C_ZB + HALF:C_ZB + 2 * HALF] = (dy0 * ab * sg * (1.0 - sg)).astype(BF16)

        zd = colsE(C_ZD + HALF, C_ZD + 3 * HALF)
        scr_ref[...] = zd[:, :HALF] * zd[:, HALF:]
        dcv = dyC(3 * HALF, 4 * HALF) * colsC(C_ZD, C_ZD + HALF)
        scr2_ref[...] = dcv
        cv = None
        dud = None
        for k in range(SC_KERNEL):
            wk = scw_ref[k:k + 1, :]
            us = scr_ref[pl.ds(HALO - (SC_KERNEL - 1) + k, TB), :]
            t = wk * us
            cv = t if cv is None else cv + t
            t2 = wk * scr2_ref[pl.ds(SC_KERNEL - 1 - k, TB), :]
            dud = t2 if dud is None else dud + t2
            dscw_ref[k:k + 1, :] += jnp.sum(dcv[:TB] * us, axis=0, keepdims=True)
        dz_ref[:, C_ZD:C_ZD + HALF] = (dyc_ref[:, 3 * HALF:4 * HALF].astype(F32) * cv).astype(BF16)
        dz_ref[:, C_ZD + HALF:C_ZD + 2 * HALF] = (dud * zc_ref[:, C_ZD + 2 * HALF:C_ZD + 3 * HALF].astype(F32)).astype(BF16)
        dz_ref[:, C_ZD + 2 * HALF:C_ZD + 3 * HALF] = (dud * zc_ref[:, C_ZD + HALF:C_ZD + 2 * HALF].astype(F32)).astype(BF16)

        cosE = jnp.concatenate([cp_ref[...], cc_ref[...], cn_ref[...]], axis=0)
        sinE = jnp.concatenate([sp_ref[...], sc_ref[...], sn_ref[...]], axis=0)
        k_ref[...] = _rope(colsE(C_K, C_K + 128), cosE, sinE).astype(BF16)
        v_ref[...] = colsE(C_V, C_V + 128).astype(BF16)
        dk_ref[...] = jnp.zeros_like(dk_ref)
        dv_ref[...] = jnp.zeros_like(dv_ref)
        q = jnp.concatenate([_rope(colsC(C_Q + 128 * j, C_Q + 128 * (j + 1)), cosE[HALO:], sinE[HALO:])
                             for j in range(4)], axis=1).astype(BF16)
        dO = dyC(2 * HALF, 3 * HALF).astype(BF16)
        lane_s = lax.broadcasted_iota(jnp.int32, (1, 128), 1)
        for qb in range(r + 1):
            first_ok = (i * r + qb) > 0
            rows = slice(qb * WINDOW, (qb + 1) * WINDOW)
            band = slice(qb * WINDOW, qb * WINDOW + 2 * WINDOW)
            for h in range(N_KV_HEADS):
                hc = slice(h * HEAD_DIM, (h + 1) * HEAD_DIM)
                kh = k_ref[band, hc]
                vh_ = v_ref[band, hc]
                heads = [slice((h * Q_PER_KV + g) * HEAD_DIM, (h * Q_PER_KV + g + 1) * HEAD_DIM) for g in range(Q_PER_KV)]
                qs = jnp.concatenate([q[rows, hs] for hs in heads], axis=0)
                dos = jnp.concatenate([dO[rows, hs] for hs in heads], axis=0)
                probs, p_sink = _attn_probs(qs, kh, _sink_col(sinks_ref, h), first_ok)
                dP = _dot_nt(dos, vh_)
                rsum = jnp.sum(probs * dP, axis=-1, keepdims=True)
                dS = (probs * (dP - rsum) * (HEAD_DIM ** -0.5)).astype(BF16)
                dk_ref[band, hc] += _dot_tn(dS, qs)
                dv_ref[band, hc] += _dot_tn(probs.astype(BF16), dos)
                if qb < r:
                    dqs = _dot(dS, kh)
                    dsk = -p_sink * rsum
                    for g in range(Q_PER_KV):
                        dq_ref[rows, heads[g]] = dqs[g * WINDOW:(g + 1) * WINDOW]
                        dsink_ref[...] += jnp.where(lane_s == h * Q_PER_KV + g, jnp.sum(dsk[g * WINDOW:(g + 1) * WINDOW]), 0.0)
        cosC, sinC = cc_ref[...], sc_ref[...]
        for j in range(4):
            dz_ref[:, C_Q + 128 * j:C_Q + 128 * (j + 1)] = _rope_t(dq_ref[:, 128 * j:128 * (j + 1)], cosC, sinC).astype(BF16)
        dz_ref[:, C_K:C_K + 128] = _rope_t(dk_ref[HALO:HALO + TB, :], cosC, sinC).astype(BF16)
        dz_ref[:, C_V:C_V + 128] = dv_ref[HALO:HALO + TB, :].astype(BF16)

    small = [((1, HALF), F32), ((1, HALF), F32), ((SG_GROUPS, SG_CHUNK, SG_CHUNK), F32), ((SG_CHUNK, 128), F32),
             ((32, HALF), F32), ((1, HALF), F32), ((1, HALF), F32), ((1, HALF), F32), ((1, 128), F32), ((8, HALF), F32)]
    outs = pl.pallas_call(
        body, name=name, grid=(nb,),
        in_specs=[cur, prev, nxt, dcur, dnxt, tcur, tcur, tprev, tprev, tnxt, tnxt] + _mixer_param_specs()
                 + [pl.BlockSpec(memory_space=pl.ANY)],
        out_specs=[pl.BlockSpec((TB, MIX_W), lambda i: (i, 1))] + [_full(s) for s, _ in small],
        out_shape=[jax.ShapeDtypeStruct((T, PROJ_PAD), BF16)] + [jax.ShapeDtypeStruct(s, d) for s, d in small],
        scratch_shapes=[pltpu.VMEM((RE, HALF), F32), pltpu.VMEM((RC, HALF), F32), pltpu.VMEM((RE, 128), BF16), pltpu.VMEM((RE, 128), BF16),
                        pltpu.VMEM((RE, 128), F32), pltpu.VMEM((RE, 128), F32), pltpu.VMEM((TB, HALF), F32),
                        pltpu.VMEM((8, TB + 2 * CV_PAD, HALF), F32), pltpu.VMEM((8, TB + CV_PAD, HALF), F32)],
        input_output_aliases={21: 0},
        compiler_params=_params("arbitrary"),
    )(proj, proj, proj, dys, dys, cos_t, sin_t, cos_t, sin_t, cos_t, sin_t, *mp, dproj)
    return outs


def _rope_tables(T):
    pos = jnp.arange(T, dtype=F32)
    inv_freq = 1.0 / (ROPE_THETA ** (jnp.arange(0, HEAD_DIM, 2, dtype=F32) / HEAD_DIM))
    ang = pos[:, None] * inv_freq[None, :]
    cos, sin = jnp.cos(ang), jnp.sin(ang)
    cos_t = jnp.concatenate([cos, cos, cos, cos], axis=1)
    sin_t = jnp.concatenate([-sin, sin, -sin, sin], axis=1)
    return cos_t, sin_t


def _mixer_params(l, sg_ln_g, sg_ln_b, sg_w, sg_b, cv_w, cv_b, cv_ln_g, cv_ln_b, attn_sinks, sc_w):
    sgb_t = jnp.zeros((SG_CHUNK, 128), F32).at[:, :SG_GROUPS].set(sg_b[l].T)
    cvw = jnp.zeros((32, HALF), F32).at[:CV_KERNEL].set(cv_w[l])
    scw = jnp.zeros((8, HALF), F32).at[:SC_KERNEL].set(sc_w[l])
    sinks = jnp.zeros((1, 128), F32).at[0, :N_Q_HEADS].set(attn_sinks[l])
    return [sg_ln_g[l][None], sg_ln_b[l][None], sg_w[l], sgb_t, cvw, cv_b[l][None], cv_ln_g[l][None], cv_ln_b[l][None], sinks, scw]


def _w_in_layout(w_in_g):
    wmix = jnp.concatenate([w_in_g[0], w_in_g[1], w_in_g[2][:, :MIX_W - 2 * W_IN_SHARD]], axis=1)
    wg = jnp.concatenate([w_in_g[2][:, MIX_W - 2 * W_IN_SHARD:], w_in_g[3]], axis=1)
    return jnp.concatenate([wg, jnp.zeros((D_MODEL, MIX_W - GATE_W), w_in_g.dtype), wmix], axis=1)


def _w_in_unlayout(dw):
    cut = MIX_W - 2 * W_IN_SHARD
    return jnp.stack([dw[:, MIX_W:MIX_W + W_IN_SHARD], dw[:, MIX_W + W_IN_SHARD:MIX_W + 2 * W_IN_SHARD],
                      jnp.concatenate([dw[:, MIX_W + 2 * W_IN_SHARD:], dw[:, :W_IN_SHARD - cut]], axis=1),
                      dw[:, W_IN_SHARD - cut:GATE_W]], axis=0)


def _device_step(x, tgt, norm_mix, norm_ffn, norm_final, mixer_params, w_in_p, wb_g, wo_g, wgu_g, wd_g):
    T = x.shape[0]
    tables = _rope_tables(T)
    saved = []
    for l in range(DEPTH):
        lw = dict(w_in=w_in_p[l], w_branch=wb_g[l], w_out=wo_g[l], w_gate_up=wgu_g[l], w_down=wd_g[l],
                  norm_mix=norm_mix[l][None], norm_ffn=norm_ffn[l][None], mixer=mixer_params[l], after=jnp.zeros((8, 128), F32))
        x, sv = _fwd_layer(l, x, lw, tables)
        saved.append((lw, sv))
    dx, dnf, loss = _final_loss(x, norm_final[None], tgt, 256, "final_loss")
    grads = [None] * DEPTH
    for l in reversed(range(DEPTH)):
        lw, sv = saved[l]
        dxm, g_ffn = _bwd_layer_ffn(l, dx, lw, sv)
        dx, g_mix = _bwd_layer_mix(l, dxm, lw, sv, tables)
        raw = {**g_ffn, **g_mix}
        grads[l] = {**raw, **_small_views(raw)}
    return loss, dx, dnf[0], grads


MIX_BLOCK = 256


def _fwd_layer(l, x, lw, tables):
    return _fwd_layer_rest(l, x, _fwd_layer_mix(l, x, lw, tables), lw)


def _fwd_layer_mix(l, x, lw, tables):
    proj, xn = _rms_mm(x, lw["norm_mix"], lw["w_in"], min(x.shape[0], 1024), 2176, f"proj{l}")
    return proj, xn, _mixers_fwd(proj, *tables, lw["mixer"], MIX_BLOCK, f"mixers_fwd{l}")


def _fwd_layer_rest(l, x, mixed, lw):
    proj, xn, ys = mixed
    TM = min(x.shape[0], 1024)
    xm, merged = _merge_fwd(x, ys, proj, lw["w_branch"], lw["w_out"], 256, f"merge_fwd{l}")
    gu, hn = _rms_mm(xm, lw["norm_ffn"], lw["w_gate_up"], TM, GU_SHARD, f"ffn_up{l}")
    x_out = _ffn_down(xm, gu, lw["w_down"], 256, f"ffn_down{l}")
    return x_out, (x, proj, xn, ys, xm, merged, gu, hn)


def _bwd_layer_ffn(l, dx, lw, sv):
    x_in, proj, xn, ys, xm, merged, gu, hn = sv
    T = dx.shape[0]
    tkk = min(T, 1024)
    gk = T // tkk
    dgu, act = _swiglu_bwd(dx, gu, lw["w_down"], 256, f"swiglu_bwd{l}", lw["after"])
    d_wd = _mm_tn(act, dx, (2, 1, gk), (tkk, D_FF // 2), lambda i, j, k: (k, i), (tkk, D_MODEL), lambda i, j, k: (k, 0),
                  (D_FF, D_MODEL), (D_FF // 2, D_MODEL), lambda i, j, k: (i, 0), f"dw_down{l}")
    d_wgu = _mm_tn(hn, dgu, (1, N_CHIPS, gk), (tkk, D_MODEL), lambda i, j, k: (k, 0), (tkk, GU_SHARD), lambda i, j, k: (k, j),
                   (N_CHIPS, D_MODEL, GU_SHARD), (None, D_MODEL, GU_SHARD), lambda i, j, k: (j, 0, 0), f"dw_gate_up{l}")
    dxm, d_nffn = _mm_nt_rmsbwd(dgu, lw["w_gate_up"], xm, lw["norm_ffn"], dx, min(T, 512), GU_SHARD, f"ffn_up_bwd{l}")
    dys, dbr, dproj = _merge_bwd(dxm, ys, proj, lw["w_branch"], lw["w_out"], 256, f"merge_bwd{l}")
    d_wo = _mm_tn(merged, dxm, (2, 1, gk), (tkk, 512), lambda i, j, k: (k, i), (tkk, D_MODEL), lambda i, j, k: (k, 0),
                  (D_MODEL, D_MODEL), (512, D_MODEL), lambda i, j, k: (i, 0), f"dw_out{l}")
    d_wb = _mm_tn(ys, dbr, (N_BRANCH, N_CHIPS, gk), (tkk, HALF), lambda i, j, k: (k, i), (tkk, 256), lambda i, j, k: (k, i * N_CHIPS + j),
                  (N_CHIPS, N_BRANCH, HALF, 256), (None, None, HALF, 256), lambda i, j, k: (j, i, 0, 0), f"dw_branch{l}")
    return (dxm, dys, dproj), dict(w_branch=d_wb, w_out=d_wo, w_gate_up=d_wgu, w_down=d_wd, norm_ffn=d_nffn)


def _bwd_layer_mix(l, carry, lw, sv, tables):
    dxm, dys, dproj = carry
    x_in, proj, xn, ys, xm, merged, gu, hn = sv
    T = dxm.shape[0]
    tkk = min(T, 1024)
    gk = T // tkk
    mb = _mixers_bwd(proj, dys, dproj, *tables, lw["mixer"], MIX_BLOCK, f"mixers_bwd{l}")
    dproj = mb[0]
    d_win = _mm_tn(xn, dproj, (1, PROJ_PAD // 2176, gk), (tkk, D_MODEL), lambda i, j, k: (k, 0), (tkk, 2176), lambda i, j, k: (k, j),
                   (D_MODEL, PROJ_PAD), (D_MODEL, 2176), lambda i, j, k: (0, j), f"dw_in{l}")
    dx, d_nmix = _mm_nt_rmsbwd(dproj, lw["w_in"], x_in, lw["norm_mix"], dxm, min(T, 512), 2176, f"proj_bwd{l}")
    return dx, dict(w_in=d_win, norm_mix=d_nmix, sg_ln_g=mb[1], sg_ln_b=mb[2], sg_w=mb[3], sg_b=mb[4], cv_w=mb[5], cv_b=mb[6],
                    cv_ln_g=mb[7], cv_ln_b=mb[8], attn_sinks=mb[9], sc_w=mb[10])


ANY = pl.BlockSpec(memory_space=pl.ANY)
BIG = ("w_in", "w_branch", "w_out", "w_gate_up", "w_down")
HALF_SHAPE = {"w_in": (2, 512, W_IN_SHARD), "w_branch": (2, 1024, 256), "w_out": (2, 128, D_MODEL),
              "w_gate_up": (2, 512, GU_SHARD), "w_down": (2, 352, D_MODEL)}
NB = len(BIG)


def _place():
    x, y, c = lax.axis_index("x"), lax.axis_index("y"), lax.axis_index("c")
    chips = [(1 - x, y), (x, 1 - y), (1 - x, 1 - y)]
    return x, y, c, 2 * x + y, chips, [2 * px + py for px, py in chips]


def _remote(src, dst, ssem, rsem, dev):
    return pltpu.make_async_remote_copy(src_ref=src, dst_ref=dst, send_sem=ssem, recv_sem=rsem, device_id=dev, device_id_type=MESH)


HBM_SPEC = pl.BlockSpec(memory_space=pltpu.HBM)
SEM_SPEC = pl.BlockSpec(memory_space=pltpu.SEMAPHORE)
DATAFLOW = pltpu.SideEffectType.DATAFLOW_SIDE_EFFECTING


def _ici_ends(kind, src, land, j, c, chip, chip_ids):
    if kind == "gather":
        return src.at[c], land.at[chip, c], land.at[chip_ids[j], c]
    return src.at[chip_ids[j]], land.at[chip], land.at[chip_ids[j]]


def _ici_start(kind, srcs, land_shapes, name):
    n = len(srcs)

    def body(*refs):
        src, land = refs[:n], refs[n:2 * n]
        ssem, rsem, token = refs[2 * n], refs[2 * n + 1], refs[-1]
        x, y, c, chip, chips, chip_ids = _place()
        for k in range(n):
            for j in range(3):
                s, d, _ = _ici_ends(kind, src[k], land[k], j, c, chip, chip_ids)
                _remote(s, d, ssem.at[3 * k + j], rsem.at[3 * k + j], (*chips[j], c)).start()
        token[...] = jnp.zeros_like(token)

    sem = pltpu.SemaphoreType.DMA((3 * n,))
    outs = pl.pallas_call(
        body, name=name,
        out_shape=(sem, sem, *[pltpu.HBM(s.shape, s.dtype) for s in srcs], *[pltpu.HBM(sh, BF16) for sh in land_shapes],
                   jax.ShapeDtypeStruct((8, 128), F32)),
        in_specs=[HBM_SPEC] * (2 * n),
        out_specs=(SEM_SPEC, SEM_SPEC, *[HBM_SPEC] * (2 * n), pl.BlockSpec(memory_space=pltpu.VMEM)),
        input_output_aliases={i: 2 + i for i in range(2 * n)},
        compiler_params=pltpu.CompilerParams(has_side_effects=DATAFLOW),
    )(*[pltpu.with_memory_space_constraint(s, pltpu.HBM) for s in srcs],
      *[pltpu.with_memory_space_constraint(lax.empty(sh, BF16), pltpu.HBM) for sh in land_shapes])
    return (kind, outs[0], outs[1], list(outs[2:2 + n]), list(outs[2 + n:2 + 2 * n])), outs[-1]


def _ici_wait(handle, after, name):
    kind, ssem_in, rsem_in, srcs, lands = handle
    n = len(srcs)

    def body(*refs):
        src, land = refs[:n], refs[n:2 * n]
        ssem, rsem = refs[2 * n], refs[2 * n + 1]
        x, y, c, chip, chips, chip_ids = _place()
        for k in range(n):
            for j in range(3):
                s, _, mine = _ici_ends(kind, src[k], land[k], j, c, chip, chip_ids)
                cp = _remote(s, mine, ssem.at[3 * k + j], rsem.at[3 * k + j], (*chips[j], c))
                cp.wait_send()
                cp.wait_recv()

    outs = pl.pallas_call(
        body, name=name, out_shape=[pltpu.HBM(t.shape, t.dtype) for t in srcs + lands],
        in_specs=[HBM_SPEC] * (2 * n) + [SEM_SPEC, SEM_SPEC, ANY], out_specs=[HBM_SPEC] * (2 * n),
        input_output_aliases={i: i for i in range(2 * n)},
        compiler_params=pltpu.CompilerParams(has_side_effects=DATAFLOW),
    )(*srcs, *lands, ssem_in, rsem_in, after)
    return list(outs[:n]), list(outs[n:])


def _ag_pair(shards, lands, name):
    n = len(shards)

    def body(*refs):
        ins, outs = refs[:n], refs[2 * n:3 * n]
        token = refs[3 * n]
        s_fwd, r_fwd, s_own, r_own = refs[3 * n + 1:]
        x, y, c, chip, chips, chip_ids = _place()
        sib = (x, y, 1 - c)
        cps = []
        for k in range(n):
            cp = _remote(ins[k], outs[k].at[chip], s_own.at[k], r_own.at[k], sib)
            cp.start()
            cps.append(cp)
            for j in range(3):
                got = outs[k].at[chip_ids[j], c]
                cp = _remote(got, got, s_fwd.at[k, j], r_fwd.at[k, j], sib)
                cp.start()
                cps.append(cp)
        for k in range(n):
            _remote(ins[k], outs[k].at[chip], s_own.at[k], r_own.at[k], sib).wait_recv()
            for j in range(3):
                got = outs[k].at[chip_ids[j], 1 - c]
                _remote(got, got, s_fwd.at[k, j], r_fwd.at[k, j], sib).wait_recv()
        for cp in cps:
            cp.wait_send()
        token[...] = jnp.zeros_like(token)

    sem, sem1 = pltpu.SemaphoreType.DMA((n, 3)), pltpu.SemaphoreType.DMA((n,))
    outs = pl.pallas_call(
        body, name=name, out_shape=[jax.ShapeDtypeStruct(t.shape, t.dtype) for t in lands] + [jax.ShapeDtypeStruct((8, 128), F32)],
        in_specs=[ANY] * (2 * n), out_specs=[ANY] * n + [pl.BlockSpec(memory_space=pltpu.VMEM)],
        input_output_aliases={n + k: k for k in range(n)},
        scratch_shapes=[sem, sem, sem1, sem1], compiler_params=pltpu.CompilerParams(has_side_effects=True),
    )(*shards, *lands)
    return list(outs[:n]), outs[n]


def _rs_pair(grads, name):
    n_arr = len(grads)

    def body(*refs):
        ins, got = refs[:n_arr], refs[n_arr:2 * n_arr]
        ssem, rsem = refs[2 * n_arr:]
        x, y, c, _, _, _ = _place()
        sib = (x, y, 1 - c)
        sends = []
        for k in reversed(range(n_arr)):
            for q in range(N_CHIPS):
                cp = _remote(ins[k].at[q, 1 - c], got[k].at[q], ssem.at[k, q], rsem.at[k, q], sib)
                cp.start()
                sends.append(cp)
        for k in range(n_arr):
            for q in range(N_CHIPS):
                _remote(got[k].at[q], got[k].at[q], ssem.at[k, q], rsem.at[k, q], sib).wait_recv()
        for cp in sends:
            cp.wait_send()

    shp = [jax.ShapeDtypeStruct((N_CHIPS,) + g.shape[2:], BF16) for g in grads]
    sem = pltpu.SemaphoreType.DMA((n_arr, N_CHIPS))
    outs = pl.pallas_call(
        body, name=name, out_shape=shp, in_specs=[ANY] * n_arr, out_specs=[ANY] * n_arr,
        scratch_shapes=[sem, sem], compiler_params=pltpu.CompilerParams(has_side_effects=True),
    )(*grads)
    return list(outs)


def _rs_share(bufs, name):
    n = len(bufs)

    def body(*refs):
        outs = refs[n:2 * n]
        ssem, rsem = refs[2 * n:]
        x, y, c, _, _, _ = _place()
        sib = (x, y, 1 - c)
        sends = []
        for k in range(n):
            for l in range(DEPTH):
                cp = _remote(outs[k].at[l, c], outs[k].at[l, c], ssem.at[k, l], rsem.at[k, l], sib)
                cp.start()
                sends.append(cp)
        for k in range(n):
            for l in range(DEPTH):
                dst = outs[k].at[l, 1 - c]
                _remote(dst, dst, ssem.at[k, l], rsem.at[k, l], sib).wait_recv()
        for cp in sends:
            cp.wait_send()

    sem = pltpu.SemaphoreType.DMA((n, DEPTH))
    outs = pl.pallas_call(
        body, name=name, out_shape=[jax.ShapeDtypeStruct(b.shape, b.dtype) for b in bufs], in_specs=[ANY] * n, out_specs=[ANY] * n,
        input_output_aliases={k: k for k in range(n)},
        scratch_shapes=[sem, sem], compiler_params=pltpu.CompilerParams(has_side_effects=True),
    )(*bufs)
    return list(outs)


def _piece(src, idx, rows, width=128, align=1, transposed=False):
    return dict(src=src, idx=idx, rows=rows, width=width, align=align, transposed=transposed)


def _all_reduce_pieces(inputs, pieces, out_shapes, writes, name):
    n_in, n_out = len(inputs), len(out_shapes)
    offs, R = [], 0
    for p in pieces:
        R = -(-R // p["align"]) * p["align"]
        offs.append(R)
        R += p["rows"]
    R = -(-R // 8) * 8
    dirs = [(dx, dy, dc) for dx in (0, 1) for dy in (0, 1) for dc in (0, 1)][1:]

    def body(*refs):
        ins, outs, token_ref = refs[:n_in], refs[n_in:n_in + n_out], refs[n_in + n_out]
        gather_ref, sum_ref, ssem, rsem = refs[n_in + n_out + 1:]
        token_ref[...] = jnp.zeros_like(token_ref)
        x, y, c = lax.axis_index("x"), lax.axis_index("y"), lax.axis_index("c")
        me = 4 * x + 2 * y + c
        gather_ref[me] = jnp.zeros((R, 128), F32)
        for p, off in zip(pieces, offs):
            v = ins[p["src"]][...].T[p["idx"]] if p["transposed"] else ins[p["src"]][p["idx"]]
            gather_ref[me, off:off + p["rows"], 0:p["width"]] = v
        flip = lambda v, d: 1 - v if d else v
        peers = [(flip(x, dx), flip(y, dy), flip(c, dc)) for dx, dy, dc in dirs]
        cps = [_remote(gather_ref.at[me], gather_ref.at[me], ssem.at[k], rsem.at[k], peers[k]) for k in range(N_DEV - 1)]
        for cp in cps:
            cp.start()
        for k, (px, py, pc) in enumerate(peers):
            slot = gather_ref.at[4 * px + 2 * py + pc]
            _remote(slot, slot, ssem.at[k], rsem.at[k], peers[k]).wait_recv()
        acc = gather_ref[0]
        for s in range(1, N_DEV):
            acc = acc + gather_ref[s]
        sum_ref[...] = acc
        for o, idx, p in writes:
            outs[o][idx] = sum_ref[offs[p]:offs[p] + pieces[p]["rows"], 0:pieces[p]["width"]]
        for cp in cps:
            cp.wait_send()

    vm = pl.BlockSpec(memory_space=pltpu.VMEM)
    outs = pl.pallas_call(
        body, name=name, out_shape=[jax.ShapeDtypeStruct(s, F32) for s in out_shapes] + [jax.ShapeDtypeStruct((8, 128), F32)],
        in_specs=[vm] * n_in, out_specs=[vm] * (n_out + 1),
        scratch_shapes=[pltpu.VMEM((N_DEV, R, 128), F32), pltpu.VMEM((R, 128), F32),
                        pltpu.SemaphoreType.DMA((N_DEV - 1,)), pltpu.SemaphoreType.DMA((N_DEV - 1,))],
        compiler_params=pltpu.CompilerParams(vmem_limit_bytes=VMEM_LIMIT),
    )(*inputs)
    return list(outs[:n_out]), outs[n_out]


def _lanes(width):
    return [slice(k, min(k + 128, width)) for k in range(0, width, 128)]


def _gather_small_weights(cvw_z, scw_z):
    pieces, writes = [], []
    for i, arr in enumerate((cvw_z, scw_z)):
        for l in range(DEPTH):
            for ln in _lanes(HALF):
                writes.append((i, (l, slice(None), ln), len(pieces)))
                pieces.append(_piece(i, (l, slice(None), ln), arr.shape[1], align=8))
    (cvw, scw), tok = _all_reduce_pieces([cvw_z, scw_z], pieces, [cvw_z.shape, scw_z.shape], writes, "ag_small")
    return cvw, scw, tok


SMALL_RAW = dict(norm_mix=(1, D_MODEL), norm_ffn=(1, D_MODEL), sg_ln_g=(1, HALF), sg_ln_b=(1, HALF), cv_b=(1, HALF), cv_ln_g=(1, HALF),
                 cv_ln_b=(1, HALF))


def _all_reduce_small_grads(raw, d_nfinal, loss):
    names = list(SMALL_RAW) + ["attn_sinks", "sg_b", "sc_w", "cv_w", "sg_w"]
    out_shape = dict(norm_mix=(DEPTH, D_MODEL), norm_ffn=(DEPTH, D_MODEL), sg_ln_g=(DEPTH, HALF), sg_ln_b=(DEPTH, HALF), cv_b=(DEPTH, HALF),
                     cv_ln_g=(DEPTH, HALF), cv_ln_b=(DEPTH, HALF), attn_sinks=(DEPTH, N_Q_HEADS), sg_b=(DEPTH, SG_GROUPS, SG_CHUNK),
                     sc_w=(DEPTH, SC_KERNEL, HALF), cv_w=(DEPTH, CV_KERNEL, HALF), sg_w=(DEPTH, SG_GROUPS, SG_CHUNK, SG_CHUNK))
    inputs, pieces, writes = [], [], []

    def add(src, idx, rows, out, out_idx, **kw):
        writes.append((names.index(out) if out in names else out, out_idx, len(pieces)))
        pieces.append(_piece(src, idx, rows, **kw))

    for l in range(DEPTH):
        row = slice(l, l + 1)
        for n, (_, width) in SMALL_RAW.items():
            inputs.append(raw[l][n])
            for ln in _lanes(width):
                add(len(inputs) - 1, (slice(0, 1), ln), 1, n, (row, ln))
        inputs.append(raw[l]["attn_sinks"])
        add(len(inputs) - 1, (slice(0, 1), slice(0, N_Q_HEADS)), 1, "attn_sinks", (row, slice(None)), width=N_Q_HEADS)
    for l in range(DEPTH):
        inputs.append(raw[l]["sg_b"])
        add(len(inputs) - 1, (slice(0, SG_GROUPS), slice(None)), SG_GROUPS, "sg_b", (l,), align=8, transposed=True)
        inputs.append(raw[l]["sc_w"])
        for ln in _lanes(HALF):
            add(len(inputs) - 1, (slice(0, SC_KERNEL), ln), SC_KERNEL, "sc_w", (l, slice(None), ln), align=8)
        inputs.append(raw[l]["cv_w"])
        for ln in _lanes(HALF):
            add(len(inputs) - 1, (slice(0, CV_KERNEL), ln), CV_KERNEL, "cv_w", (l, slice(None), ln), align=8)
        inputs.append(raw[l]["sg_w"])
        for g in range(SG_GROUPS):
            add(len(inputs) - 1, (g,), SG_CHUNK, "sg_w", (l, g), align=8)
    n_names = len(names)
    inputs.append(d_nfinal)
    for ln in _lanes(D_MODEL):
        add(len(inputs) - 1, (slice(0, 1), ln), 1, n_names, (slice(0, 1), ln))
    inputs.append(loss)
    add(len(inputs) - 1, (slice(0, 1), slice(None)), 1, n_names + 1, (slice(0, 1), slice(None)))
    outs, tok = _all_reduce_pieces(inputs, pieces, [out_shape[n] for n in names] + [(1, D_MODEL), (1, 128)], writes, "ar_small")
    return dict(zip(names, outs[:n_names])), outs[n_names], outs[n_names + 1], tok


def _small_views(raw):
    v = {n: raw[n][0] for n in SMALL_RAW}
    v.update(sg_w=raw["sg_w"], sg_b=raw["sg_b"][:, :SG_GROUPS].T, cv_w=raw["cv_w"][:CV_KERNEL],
             attn_sinks=raw["attn_sinks"][0, :N_Q_HEADS], sc_w=raw["sc_w"][:SC_KERNEL])
    return v


def _row_tile(rows, cols, n_arrays):
    budget = 20 * 1024 * 1024 // (n_arrays * 2 * cols * 4)
    tr = rows
    while tr > budget or tr % 16:
        assert tr % 2 == 0, (rows, cols)
        tr //= 2
    return tr


def _add_pairs(g, got, place, name):
    _, _, rows, cols = g.shape
    tr = _row_tile(rows, cols, 3)

    def body(place_ref, a_ref, b_ref, o_ref):
        del place_ref
        o_ref[...] = (a_ref[...].astype(F32) + b_ref[...].astype(F32)).astype(BF16)

    spec = pl.BlockSpec((None, tr, cols), lambda q, i, p: (q, i, 0))
    grid_spec = pltpu.PrefetchScalarGridSpec(
        num_scalar_prefetch=1, grid=(N_CHIPS, rows // tr),
        in_specs=[pl.BlockSpec((None, None, tr, cols), lambda q, i, p: (q, p[1], i, 0)), spec], out_specs=spec)
    return pl.pallas_call(body, name=name, grid_spec=grid_spec, out_shape=jax.ShapeDtypeStruct((N_CHIPS, rows, cols), BF16),
                          compiler_params=_params("parallel", "parallel"))(place, g, got)


def _sum_chips(own, recv, place, l, buf, name, after):
    _, rows, cols = own.shape
    tr = _row_tile(rows, cols, 4)

    def body(place_ref, own_ref, recv_ref, *rest):
        chip = place_ref[0]
        acc = own_ref[...].astype(F32)
        for j in range(1, N_CHIPS):
            acc = acc + recv_ref[lax.rem(chip + j, N_CHIPS)].astype(F32)
        rest[-1][...] = acc

    in_specs = [pl.BlockSpec((None, tr, cols), lambda i, p: (p[0], i, 0)), pl.BlockSpec((N_CHIPS, tr, cols), lambda i, p: (0, i, 0)), ANY]
    args = [place, own, recv, after]
    aliases = {}
    if buf is not None:
        in_specs.append(ANY)
        args.append(buf)
        aliases = {4: 0}
    grid_spec = pltpu.PrefetchScalarGridSpec(
        num_scalar_prefetch=1, grid=(rows // tr,), in_specs=in_specs,
        out_specs=pl.BlockSpec((None, None, tr, cols), lambda i, p: (l, p[1], i, 0)))
    return pl.pallas_call(body, name=name, grid_spec=grid_spec, out_shape=jax.ShapeDtypeStruct((DEPTH, 2, rows, cols), F32),
                          input_output_aliases=aliases, compiler_params=_params("parallel"))(*args)


def _adamw(w, g, m, v, name):
    shape = w.shape
    lead, (rows, cols) = shape[:-2], shape[-2:]
    tr = _row_tile(rows, cols, 7)

    def body(w_ref, g_ref, m_ref, v_ref, d_ref, mo_ref, vo_ref):
        gv = g_ref[...]
        mn = ADAM_B1 * m_ref[...] + (1.0 - ADAM_B1) * gv
        vn = ADAM_B2 * v_ref[...] + (1.0 - ADAM_B2) * (gv * gv)
        m_hat = mn / (1.0 - ADAM_B1 ** ADAM_STEP)
        v_hat = vn / (1.0 - ADAM_B2 ** ADAM_STEP)
        d_ref[...] = -ADAM_LR * (m_hat / (jnp.sqrt(v_hat) + ADAM_EPS) + ADAM_WD * w_ref[...])
        mo_ref[...] = mn
        vo_ref[...] = vn

    spec = pl.BlockSpec((None,) * len(lead) + (tr, cols), lambda *idx: (*idx, 0))
    grid = lead + (rows // tr,)
    return list(pl.pallas_call(body, name=name, grid=grid, in_specs=[spec] * 4, out_specs=[spec] * 3,
                               out_shape=[jax.ShapeDtypeStruct(shape, F32)] * 3,
                               compiler_params=_params(*(["parallel"] * len(grid))))(w, g, m, v))


def _adamw_small(ws, gs, ms, vs, name):
    n = len(ws)

    def body(*refs):
        for i in range(n):
            gv = refs[n + i][...]
            mn = ADAM_B1 * refs[2 * n + i][...] + (1.0 - ADAM_B1) * gv
            vn = ADAM_B2 * refs[3 * n + i][...] + (1.0 - ADAM_B2) * (gv * gv)
            m_hat = mn / (1.0 - ADAM_B1 ** ADAM_STEP)
            v_hat = vn / (1.0 - ADAM_B2 ** ADAM_STEP)
            refs[4 * n + i][...] = -ADAM_LR * (m_hat / (jnp.sqrt(v_hat) + ADAM_EPS) + ADAM_WD * refs[i][...])
            refs[5 * n + i][...] = mn
            refs[6 * n + i][...] = vn

    vm = pl.BlockSpec(memory_space=pltpu.VMEM)
    outs = pl.pallas_call(body, name=name, out_shape=[jax.ShapeDtypeStruct(t.shape, F32) for t in ws] * 3,
                          in_specs=[vm] * (4 * n), out_specs=[vm] * (3 * n),
                          compiler_params=pltpu.CompilerParams(vmem_limit_bytes=VMEM_LIMIT))(*ws, *gs, *ms, *vs)
    return outs[:n], outs[n:2 * n], outs[2 * n:]


SMALL = ("norm_mix", "sg_ln_g", "sg_ln_b", "sg_w", "sg_b", "cv_w", "cv_b", "cv_ln_g", "cv_ln_b", "attn_sinks", "sc_w", "norm_ffn", "norm_final")
ORDER = ("norm_mix", "w_in", "sg_ln_g", "sg_ln_b", "sg_w", "sg_b", "cv_w", "cv_b", "cv_ln_g", "cv_ln_b", "attn_sinks", "sc_w",
         "w_branch", "w_out", "norm_ffn", "w_gate_up", "w_down", "norm_final")


def kernel(x, norm_mix, w_in, sg_ln_g, sg_ln_b, sg_w, sg_b, cv_w, cv_b, cv_ln_g, cv_ln_b, attn_sinks, sc_w, w_branch, w_out, norm_ffn, w_gate_up, w_down, norm_final, loss_target, m_norm_mix, m_w_in, m_sg_ln_g, m_sg_ln_b, m_sg_w, m_sg_b, m_cv_w, m_cv_b, m_cv_ln_g, m_cv_ln_b, m_attn_sinks, m_sc_w, m_w_branch, m_w_out, m_norm_ffn, m_w_gate_up, m_w_down, m_norm_final, v_norm_mix, v_w_in, v_sg_ln_g, v_sg_ln_b, v_sg_w, v_sg_b, v_cv_w, v_cv_b, v_cv_ln_g, v_cv_ln_b, v_attn_sinks, v_sc_w, v_w_branch, v_w_out, v_norm_ffn, v_w_gate_up, v_w_down, v_norm_final):
    W = dict(norm_mix=norm_mix, w_in=w_in, sg_ln_g=sg_ln_g, sg_ln_b=sg_ln_b, sg_w=sg_w, sg_b=sg_b, cv_w=cv_w, cv_b=cv_b, cv_ln_g=cv_ln_g,
             cv_ln_b=cv_ln_b, attn_sinks=attn_sinks, sc_w=sc_w, w_branch=w_branch, w_out=w_out, norm_ffn=norm_ffn, w_gate_up=w_gate_up,
             w_down=w_down, norm_final=norm_final)
    M = dict(norm_mix=m_norm_mix, w_in=m_w_in, sg_ln_g=m_sg_ln_g, sg_ln_b=m_sg_ln_b, sg_w=m_sg_w, sg_b=m_sg_b, cv_w=m_cv_w, cv_b=m_cv_b,
             cv_ln_g=m_cv_ln_g, cv_ln_b=m_cv_ln_b, attn_sinks=m_attn_sinks, sc_w=m_sc_w, w_branch=m_w_branch, w_out=m_w_out,
             norm_ffn=m_norm_ffn, w_gate_up=m_w_gate_up, w_down=m_w_down, norm_final=m_norm_final)
    V = dict(norm_mix=v_norm_mix, w_in=v_w_in, sg_ln_g=v_sg_ln_g, sg_ln_b=v_sg_ln_b, sg_w=v_sg_w, sg_b=v_sg_b, cv_w=v_cv_w, cv_b=v_cv_b,
             cv_ln_g=v_cv_ln_g, cv_ln_b=v_cv_ln_b, attn_sinks=v_attn_sinks, sc_w=v_sc_w, w_branch=v_w_branch, w_out=v_w_out,
             norm_ffn=v_norm_ffn, w_gate_up=v_w_gate_up, w_down=v_w_down, norm_final=v_norm_final)
    mx, my, mc = lax.axis_index("x"), lax.axis_index("y"), lax.axis_index("c")
    chip = 2 * mx + my

    place = jnp.stack([chip, mc]).astype(jnp.int32)
    tables = _rope_tables(x.shape[1])
    land_shapes = [(N_CHIPS,) + HALF_SHAPE[n] for n in BIG]
    part_shapes = {n: (N_CHIPS,) + HALF_SHAPE[n][1:] for n in BIG}

    def shards_of(l, tok):
        return [(W[n][l] + tok[0, 0]).astype(BF16).reshape(HALF_SHAPE[n]) for n in BIG]

    def finish_gather(tag, handle, after):
        srcs, lands = _ici_wait(handle, after, f"ag_wait{tag}")
        return _ag_pair(srcs, lands, f"ag_pair{tag}")[0]

    def mix_weights(l, g_in):
        return dict(w_in=_w_in_layout(g_in[0].reshape(N_CHIPS, D_MODEL, W_IN_SHARD)), norm_mix=norm_mix[l][None], norm_ffn=norm_ffn[l][None],
                    mixer=_mixer_params(l, sg_ln_g, sg_ln_b, sg_w, sg_b, cvw_full, cv_b, cv_ln_g, cv_ln_b, attn_sinks, scw_full))

    def rest_weights(lw, g_rest):
        G = dict(zip(BIG[1:], g_rest))
        lw.update(w_branch=G["w_branch"].reshape(N_CHIPS, N_BRANCH, HALF, 256), w_out=G["w_out"].reshape(D_MODEL, D_MODEL),
                  w_gate_up=G["w_gate_up"].reshape(N_CHIPS, D_MODEL, GU_SHARD), w_down=G["w_down"].reshape(D_FF, D_MODEL))

    def shard_major(g):
        t = dict(g)
        if "w_in" in t:
            t["w_in"] = _w_in_unlayout(t["w_in"])
        return {n: t[n].reshape((N_CHIPS,) + HALF_SHAPE[n]) for n in BIG if n in t}

    def pair_sums(tag, g):
        names = list(g)
        got = _rs_pair([g[n] for n in names], f"rs_pair{tag}")
        return names, [_add_pairs(g[n], got[k], place, f"rs_add{tag}_{n}") for k, n in enumerate(names)]

    zero_tok = jnp.zeros((8, 128), F32)
    south = (mc == 0).astype(F32)
    cvw_z = lax.dynamic_update_slice(jnp.zeros((DEPTH, CV_KERNEL, HALF), F32), cv_w * south, (0, 0, chip * 128))
    scw_z = lax.dynamic_update_slice(jnp.zeros((DEPTH, SC_KERNEL, HALF), F32), sc_w * south, (0, 0, chip * 128))
    cvw_full, scw_full, tok = _gather_small_weights(cvw_z, scw_z)

    handles = []
    for l in range(DEPTH):
        for tag, sl in (("in", slice(0, 1)), ("rest", slice(1, NB))):
            h, tok = _ici_start("gather", shards_of(l, tok)[sl], land_shapes[sl], f"ag_start{l}{tag}")
            handles.append(h)
    x_l, saved = x[0], []
    for l in range(DEPTH):
        lw = mix_weights(l, finish_gather(f"{l}in", handles[2 * l], x_l if l else tok))
        mixed = _fwd_layer_mix(l, x_l, lw, tables)
        rest_weights(lw, finish_gather(f"{l}rest", handles[2 * l + 1], mixed[2]))
        x_l, sv = _fwd_layer_rest(l, x_l, mixed, lw)
        saved.append((lw, sv))
    (lw0, sv0), (lw1, sv1) = saved
    dx, d_nfinal, loss = _final_loss(x_l, norm_final[None], loss_target[0], 256, "final_loss")

    lw1["after"] = zero_tok
    carry, g_ffn1 = _bwd_layer_ffn(1, dx, lw1, sv1)
    dx, g_mix1 = _bwd_layer_mix(1, carry, lw1, sv1, tables)
    names1, part1 = pair_sums("1", shard_major({**g_ffn1, **g_mix1}))
    hr1, tok = _ici_start("scatter", part1, [part_shapes[n] for n in names1], "rs_start1")

    lw0["after"] = tok
    carry, g_ffn0 = _bwd_layer_ffn(0, dx, lw0, sv0)
    names_a, part_a = pair_sums("0a", shard_major(g_ffn0))
    _, recv1 = _ici_wait(hr1, part_a[0], "rs_wait1")
    hra, tok = _ici_start("scatter", part_a, [part_shapes[n] for n in names_a], "rs_start0a")

    lw0["mixer"] = [lw0["mixer"][0] + tok[0, 0]] + lw0["mixer"][1:]
    dx, g_mix0 = _bwd_layer_mix(0, carry, lw0, sv0, tables)
    _, recv_a = _ici_wait(hra, dx, "rs_wait0a")

    small_red, nf_red, loss_red, tok = _all_reduce_small_grads([{**g_ffn0, **g_mix0}, {**g_ffn1, **g_mix1}], d_nfinal, loss)
    small_red["norm_final"] = nf_red
    loss_out = loss_red[0, 0]
    for n in ("cv_w", "sc_w"):
        small_red[n] = lax.dynamic_slice_in_dim(small_red[n], chip * 128, 128, axis=2)

    g_mix0["w_in"] = g_mix0["w_in"] + tok[0, 0].astype(BF16)
    names_b, part_b = pair_sums("0b", shard_major(g_mix0))
    hrb, tok = _ici_start("scatter", part_b, [part_shapes[n] for n in names_b], "rs_start0b")
    bufs = {n: _sum_chips(part1[k], recv1[k], place, 1, None, f"rs_sum1_{n}", tok) for k, n in enumerate(names1)}
    for k, n in enumerate(names_a):
        bufs[n] = _sum_chips(part_a[k], recv_a[k], place, 0, bufs[n], f"rs_sum0_{n}", tok)
    shared = dict(zip(names_a, _rs_share([bufs[n] for n in names_a], "rs_share_a")))
    upd = {}
    for n in names_a:
        red = shared[n].reshape(W[n].shape)
        upd[n] = [red] + _adamw(W[n], red, M[n], V[n], f"adamw_{n}")
    two_d = lambda t: t[None] if t.ndim == 1 else t
    small_upd = _adamw_small(*([two_d(t[n]) for n in SMALL] for t in (W, small_red, M, V)), "adamw_small")
    for n, d, mo, vo in zip(SMALL, *small_upd):
        upd[n] = [t.reshape(W[n].shape) for t in (small_red[n], d, mo, vo)]

    _, recv_b = _ici_wait(hrb, upd[names_a[-1]][1], "rs_wait0b")
    for k, n in enumerate(names_b):
        bufs[n] = _sum_chips(part_b[k], recv_b[k], place, 0, bufs[n], f"rs_sum0_{n}", tok)
    shared = dict(zip(names_b, _rs_share([bufs[n] for n in names_b], "rs_share_b")))
    for n in names_b:
        red = shared[n].reshape(W[n].shape)
        upd[n] = [red] + _adamw(W[n], red, M[n], V[n], f"adamw_{n}")

    out = [loss_out, dx[None]]
    for k in range(4):
        out += [upd[n][k] for n in ORDER]
    return tuple(out)
```

```python
import functools
import math

import jax
import jax.numpy as jnp
from jax import lax
from jax.experimental import pallas as pl
from jax.experimental.pallas import tpu as pltpu

F32 = jnp.float32
BF16 = jnp.bfloat16

D_MODEL = 1024
DEPTH = 2
HALF = 512
SG_CHUNK = 128
SG_GROUPS = 4
CV_KERNEL = 31
HEAD_DIM = 64
N_Q_HEADS = 8
N_KV_HEADS = 2
Q_PER_KV = N_Q_HEADS // N_KV_HEADS
WINDOW = 128
ROPE_THETA = 10000.0
SC_KERNEL = 3
N_BRANCH = 4
D_FF = 2816
EPS = 1e-6
N_CHIPS = 4
N_DEV = 8

MIX_W = 4352
GATE_W = N_BRANCH * D_MODEL
PROJ_PAD = 2 * MIX_W
W_IN_SHARD = 2112
GU_SHARD = 1408
HALO = 128
CV_PAD = 32

ADAM_LR = 0.001
ADAM_B1 = 0.9
ADAM_B2 = 0.999
ADAM_EPS = 1e-08
ADAM_WD = 0.01
ADAM_STEP = 10

VMEM_LIMIT = 56 * 1024 * 1024
INV_SQRT2 = 1.0 / math.sqrt(2.0)
INV_SQRT_2PI = 1.0 / math.sqrt(2.0 * math.pi)
NEG_BIG = -1e30
MESH = pl.DeviceIdType.MESH

C_ZA, C_ZB, C_Q, C_K, C_V, C_ZD = 0, 1024, 2048, 2560, 2688, 2816


def _params(*sem):
    return pltpu.CompilerParams(dimension_semantics=sem, vmem_limit_bytes=VMEM_LIMIT)


def _sig(v):
    return 1.0 / (1.0 + jnp.exp(-v))


def _dot(a, b):
    return jnp.dot(a, b, preferred_element_type=F32)


def _dot_nt(a, b):
    return lax.dot_general(a, b, (((1,), (1,)), ((), ())), preferred_element_type=F32)


def _dot_tn(a, b):
    return lax.dot_general(a, b, (((0,), (0,)), ((), ())), preferred_element_type=F32)


def _full(shape):
    nd = len(shape)
    return pl.BlockSpec(shape, lambda *_: (0,) * nd)


def _rms_mm(x, g, w, tm, tn, name):
    T = x.shape[0]
    transposed = w.ndim == 2
    if transposed:
        N = w.shape[0]
        wspec = pl.BlockSpec((tn, D_MODEL), lambda i, j: (j, 0))
    else:
        tn = w.shape[2]
        N = w.shape[0] * tn
        wspec = pl.BlockSpec((None, D_MODEL, tn), lambda i, j: (j, 0, 0))

    def body(x_ref, g_ref, w_ref, o_ref, xn_ref):
        @pl.when(pl.program_id(1) == 0)
        def _():
            xv = x_ref[...]
            r = lax.rsqrt(jnp.mean(xv * xv, axis=-1, keepdims=True) + EPS)
            xn_ref[...] = (xv * r * g_ref[...]).astype(BF16)

        o_ref[...] = (_dot_nt if transposed else _dot)(xn_ref[...], w_ref[...]).astype(BF16)

    return pl.pallas_call(
        body, name=name, grid=(T // tm, N // tn),
        in_specs=[pl.BlockSpec((tm, D_MODEL), lambda i, j: (i, 0)), _full((1, D_MODEL)), wspec],
        out_specs=[pl.BlockSpec((tm, tn), lambda i, j: (i, j)), pl.BlockSpec((tm, D_MODEL), lambda i, j: (i, 0))],
        out_shape=[jax.ShapeDtypeStruct((T, N), BF16), jax.ShapeDtypeStruct((T, D_MODEL), BF16)],
        compiler_params=_params("parallel", "arbitrary"),
    )(x, g, w)


def _merge_fwd(x, ys, proj, wb, wo, tm, name):
    T = x.shape[0]

    def body(x_ref, ys_ref, zg_ref, wb_ref, wo_ref, xo_ref, mg_ref):
        merged = None
        for n in range(N_BRANCH):
            yn = ys_ref[:, n * HALF:(n + 1) * HALF]
            br = jnp.concatenate([_dot(yn, wb_ref[s, n]) for s in range(N_CHIPS)], axis=1)
            t = _sig(zg_ref[:, n * D_MODEL:(n + 1) * D_MODEL].astype(F32)) * br
            merged = t if merged is None else merged + t
        mb = merged.astype(BF16)
        mg_ref[...] = mb
        xo_ref[...] = x_ref[...] + _dot(mb, wo_ref[...])

    return pl.pallas_call(
        body, name=name, grid=(T // tm,),
        in_specs=[pl.BlockSpec((tm, D_MODEL), lambda i: (i, 0)), pl.BlockSpec((tm, N_BRANCH * HALF), lambda i: (i, 0)),
                  pl.BlockSpec((tm, GATE_W), lambda i: (i, 0)), _full(wb.shape), _full(wo.shape)],
        out_specs=[pl.BlockSpec((tm, D_MODEL), lambda i: (i, 0)), pl.BlockSpec((tm, D_MODEL), lambda i: (i, 0))],
        out_shape=[jax.ShapeDtypeStruct((T, D_MODEL), F32), jax.ShapeDtypeStruct((T, D_MODEL), BF16)],
        compiler_params=_params("parallel"),
    )(x, ys, proj, wb, wo)


def _ffn_down(xm, gu, wd, tm, name):
    T = xm.shape[0]

    def body(x_ref, gu_ref, wd_ref, o_ref):
        g = gu_ref[:, :D_FF].astype(F32)
        u = gu_ref[:, D_FF:].astype(F32)
        act = (g * _sig(g) * u).astype(BF16)
        o_ref[...] = x_ref[...] + _dot(act, wd_ref[...])

    return pl.pallas_call(
        body, name=name, grid=(T // tm,),
        in_specs=[pl.BlockSpec((tm, D_MODEL), lambda i: (i, 0)), pl.BlockSpec((tm, 2 * D_FF), lambda i: (i, 0)), _full(wd.shape)],
        out_specs=pl.BlockSpec((tm, D_MODEL), lambda i: (i, 0)),
        out_shape=jax.ShapeDtypeStruct((T, D_MODEL), F32),
        compiler_params=_params("parallel"),
    )(xm, gu, wd)


def _final_loss(x, g, tgt, tm, name):
    T = x.shape[0]

    def body(x_ref, g_ref, t_ref, dx_ref, dg_ref, ls_ref):
        @pl.when(pl.program_id(0) == 0)
        def _():
            dg_ref[...] = jnp.zeros_like(dg_ref)
            ls_ref[...] = jnp.zeros_like(ls_ref)

        xv = x_ref[...]
        gv = g_ref[...]
        r = lax.rsqrt(jnp.mean(xv * xv, axis=-1, keepdims=True) + EPS)
        xh = xv * r
        diff = xh * gv - t_ref[...]
        ls_ref[...] += jnp.full(ls_ref.shape, 0.5 / D_MODEL, F32) * jnp.sum(diff * diff)
        dy = diff * (1.0 / D_MODEL)
        dxh = dy * gv
        dx_ref[...] = r * (dxh - xh * jnp.mean(dxh * xh, axis=-1, keepdims=True))
        dg_ref[...] += jnp.sum(dy * xh, axis=0, keepdims=True)

    return pl.pallas_call(
        body, name=name, grid=(T // tm,),
        in_specs=[pl.BlockSpec((tm, D_MODEL), lambda i: (i, 0)), _full((1, D_MODEL)), pl.BlockSpec((tm, D_MODEL), lambda i: (i, 0))],
        out_specs=[pl.BlockSpec((tm, D_MODEL), lambda i: (i, 0)), _full((1, D_MODEL)), _full((1, 128))],
        out_shape=[jax.ShapeDtypeStruct((T, D_MODEL), F32), jax.ShapeDtypeStruct((1, D_MODEL), F32), jax.ShapeDtypeStruct((1, 128), F32)],
        compiler_params=_params("arbitrary"),
    )(x, g, tgt)


def _swiglu_bwd(dx, gu, wd, tm, name, after):
    T = dx.shape[0]

    def body(dx_ref, gu_ref, wd_ref, after_ref, dgu_ref, act_ref):
        del after_ref
        dact = _dot_nt(dx_ref[...].astype(BF16), wd_ref[...])
        g = gu_ref[:, :D_FF].astype(F32)
        u = gu_ref[:, D_FF:].astype(F32)
        s = _sig(g)
        silu = g * s
        act_ref[...] = (silu * u).astype(BF16)
        dgu_ref[:, :D_FF] = (dact * u * (s + silu * (1.0 - s))).astype(BF16)
        dgu_ref[:, D_FF:] = (dact * silu).astype(BF16)

    return pl.pallas_call(
        body, name=name, grid=(T // tm,),
        in_specs=[pl.BlockSpec((tm, D_MODEL), lambda i: (i, 0)), pl.BlockSpec((tm, 2 * D_FF), lambda i: (i, 0)), _full(wd.shape),
                  pl.BlockSpec(memory_space=pl.ANY)],
        out_specs=[pl.BlockSpec((tm, 2 * D_FF), lambda i: (i, 0)), pl.BlockSpec((tm, D_FF), lambda i: (i, 0))],
        out_shape=[jax.ShapeDtypeStruct((T, 2 * D_FF), BF16), jax.ShapeDtypeStruct((T, D_FF), BF16)],
        compiler_params=_params("parallel"),
    )(dx, gu, wd, after)


def _mm_tn(a, b, grid, a_block, a_map, b_block, b_map, o_shape, o_block, o_map, name):
    gk = grid[2]
    tm = [d for d in a_block if d is not None][-1]
    tn = [d for d in b_block if d is not None][-1]

    def body(a_ref, b_ref, o_ref, acc_ref):
        k = pl.program_id(2)
        p = _dot_tn(a_ref[...].astype(BF16), b_ref[...].astype(BF16))

        @pl.when(k == 0)
        def _():
            acc_ref[...] = p

        @pl.when(k > 0)
        def _():
            acc_ref[...] += p

        @pl.when(k == gk - 1)
        def _():
            o_ref[...] = acc_ref[...].astype(o_ref.dtype)

    return pl.pallas_call(
        body, name=name, grid=grid,
        in_specs=[pl.BlockSpec(a_block, a_map), pl.BlockSpec(b_block, b_map)],
        out_specs=pl.BlockSpec(o_block, o_map),
        out_shape=jax.ShapeDtypeStruct(o_shape, BF16),
        scratch_shapes=[pltpu.VMEM((tm, tn), F32)],
        compiler_params=_params("parallel", "parallel", "arbitrary"),
    )(a, b)


def _mm_nt_rmsbwd(a, w, x, g, dres, tm, tk, name):
    T = x.shape[0]
    transposed = w.ndim == 2
    if transposed:
        gk = w.shape[0] // tk
        wspec = pl.BlockSpec((tk, D_MODEL), lambda i, k: (k, 0))
    else:
        tk = w.shape[2]
        gk = w.shape[0]
        wspec = pl.BlockSpec((None, D_MODEL, tk), lambda i, k: (k, 0, 0))

    def body(a_ref, w_ref, x_ref, g_ref, r_ref, dx_ref, dg_ref, acc_ref):
        i, k = pl.program_id(0), pl.program_id(1)
        p = (_dot if transposed else _dot_nt)(a_ref[...], w_ref[...])

        @pl.when(k == 0)
        def _():
            acc_ref[...] = p

        @pl.when(k > 0)
        def _():
            acc_ref[...] += p

        @pl.when(jnp.logical_and(i == 0, k == 0))
        def _():
            dg_ref[...] = jnp.zeros_like(dg_ref)

        @pl.when(k == gk - 1)
        def _():
            dh = acc_ref[...]
            xv = x_ref[...]
            r = lax.rsqrt(jnp.mean(xv * xv, axis=-1, keepdims=True) + EPS)
            xh = xv * r
            dxh = dh * g_ref[...]
            dx_ref[...] = r_ref[...] + r * (dxh - xh * jnp.mean(dxh * xh, axis=-1, keepdims=True))
            dg_ref[...] += jnp.sum(dh * xh, axis=0, keepdims=True)

    return pl.pallas_call(
        body, name=name, grid=(T // tm, gk),
        in_specs=[pl.BlockSpec((tm, tk), lambda i, k: (i, k)), wspec, pl.BlockSpec((tm, D_MODEL), lambda i, k: (i, 0)),
                  _full((1, D_MODEL)), pl.BlockSpec((tm, D_MODEL), lambda i, k: (i, 0))],
        out_specs=[pl.BlockSpec((tm, D_MODEL), lambda i, k: (i, 0)), _full((1, D_MODEL))],
        out_shape=[jax.ShapeDtypeStruct((T, D_MODEL), F32), jax.ShapeDtypeStruct((1, D_MODEL), F32)],
        scratch_shapes=[pltpu.VMEM((tm, D_MODEL), F32)],
        compiler_params=_params("arbitrary", "arbitrary"),
    )(a, w, x, g, dres)


def _merge_bwd(dxm, ys, proj, wb, wo, tm, name):
    T = dxm.shape[0]

    def body(dx_ref, ys_ref, zg_ref, wb_ref, wo_ref, dys_ref, dbr_ref, dp_ref):
        dmerged = _dot_nt(dx_ref[...].astype(BF16), wo_ref[...])
        for n in range(N_BRANCH):
            yn = ys_ref[:, n * HALF:(n + 1) * HALF]
            br = jnp.concatenate([_dot(yn, wb_ref[s, n]) for s in range(N_CHIPS)], axis=1)
            gt = _sig(zg_ref[:, n * D_MODEL:(n + 1) * D_MODEL].astype(F32))
            dbr = (gt * dmerged).astype(BF16)
            dbr_ref[:, n * D_MODEL:(n + 1) * D_MODEL] = dbr
            dp_ref[:, n * D_MODEL:(n + 1) * D_MODEL] = (dmerged * br * gt * (1.0 - gt)).astype(BF16)
            dy = None
            for s in range(N_CHIPS):
                t = _dot_nt(dbr[:, s * 256:(s + 1) * 256], wb_ref[s, n])
                dy = t if dy is None else dy + t
            dys_ref[:, n * HALF:(n + 1) * HALF] = dy.astype(BF16)
        dp_ref[:, GATE_W:] = jnp.zeros((tm, MIX_W - GATE_W), BF16)

    return pl.pallas_call(
        body, name=name, grid=(T // tm,),
        in_specs=[pl.BlockSpec((tm, D_MODEL), lambda i: (i, 0)), pl.BlockSpec((tm, N_BRANCH * HALF), lambda i: (i, 0)),
                  pl.BlockSpec((tm, GATE_W), lambda i: (i, 0)), _full(wb.shape), _full(wo.shape)],
        out_specs=[pl.BlockSpec((tm, N_BRANCH * HALF), lambda i: (i, 0)), pl.BlockSpec((tm, GATE_W), lambda i: (i, 0)),
                   pl.BlockSpec((tm, MIX_W), lambda i: (i, 0))],
        out_shape=[jax.ShapeDtypeStruct((T, N_BRANCH * HALF), BF16), jax.ShapeDtypeStruct((T, GATE_W), BF16),
                   jax.ShapeDtypeStruct((T, PROJ_PAD), BF16)],
        compiler_params=_params("parallel"),
    )(dxm, ys, proj, wb, wo)


def _gelu(v):
    return 0.5 * v * (1.0 + lax.erf(v * INV_SQRT2))


def _gelu_grad(v):
    return 0.5 * (1.0 + lax.erf(v * INV_SQRT2)) + v * jnp.exp(-0.5 * v * v) * INV_SQRT_2PI


def _rot_half(t):
    w = t.shape[1]
    lane = lax.broadcasted_iota(jnp.int32, t.shape, 1)
    return jnp.where((lane % HEAD_DIM) < HEAD_DIM // 2, pltpu.roll(t, w - HEAD_DIM // 2, 1), pltpu.roll(t, HEAD_DIM // 2, 1))


def _rope(t, cos, sin_signed):
    return t * cos + _rot_half(t) * sin_signed


def _rope_t(d, cos, sin_signed):
    return d * cos + _rot_half(d * sin_signed)


def _ln_fwd(v, g, b):
    mu = jnp.mean(v, axis=-1, keepdims=True)
    vc = v - mu
    r = lax.rsqrt(jnp.mean(vc * vc, axis=-1, keepdims=True) + EPS)
    vh = vc * r
    return vh * g + b, vh, r


def _ln_bwd(dn, vh, r, g):
    dvh = dn * g
    return r * (dvh - jnp.mean(dvh, axis=-1, keepdims=True) - vh * jnp.mean(dvh * vh, axis=-1, keepdims=True))


def _sublane_shifts(sh_ref, rows):
    for b in range(1, 8):
        sh_ref[b, 0:rows - 8, :] = sh_ref[0, pl.ds(b, rows - 8), :]


def _tap(sh_ref, off, n):
    return sh_ref[off % 8, pl.ds(off - off % 8, n), :]


def _tril_mask():
    return lax.broadcasted_iota(jnp.int32, (SG_CHUNK, SG_CHUNK), 0) >= lax.broadcasted_iota(jnp.int32, (SG_CHUNK, SG_CHUNK), 1)


def _attn_probs(qs, kh, sink_col, first_ok):
    s = _dot_nt(qs, kh) * (HEAD_DIM ** -0.5)
    row = lax.broadcasted_iota(jnp.int32, s.shape, 0) % WINDOW
    col = lax.broadcasted_iota(jnp.int32, s.shape, 1)
    valid = (col > row) & (col <= row + WINDOW) & ((col >= WINDOW) | first_ok)
    s = jnp.where(valid, s, NEG_BIG)
    m = jnp.maximum(jnp.max(s, axis=-1, keepdims=True), sink_col)
    p = jnp.where(valid, jnp.exp(s - m), 0.0)
    es = jnp.exp(sink_col - m)
    inv = 1.0 / (jnp.sum(p, axis=-1, keepdims=True) + es)
    return p * inv, es * inv


def _sink_col(sinks_ref, h):
    return jnp.concatenate([jnp.broadcast_to(sinks_ref[:, h * Q_PER_KV + g:h * Q_PER_KV + g + 1], (WINDOW, 1))
                            for g in range(Q_PER_KV)], axis=0)


def _mixer_in_specs(TB, nb):
    r = TB // HALO
    last = nb * r - 1
    cur = pl.BlockSpec((TB, MIX_W), lambda i: (i, 1))
    prev = pl.BlockSpec((HALO, MIX_W), lambda i: (jnp.maximum(i * r - 1, 0), 1))
    nxt = pl.BlockSpec((HALO, MIX_W), lambda i: (jnp.minimum((i + 1) * r, last), 1))
    tcur = pl.BlockSpec((TB, 128), lambda i: (i, 0))
    tprev = pl.BlockSpec((HALO, 128), lambda i: (jnp.maximum(i * r - 1, 0), 0))
    tnxt = pl.BlockSpec((HALO, 128), lambda i: (jnp.minimum((i + 1) * r, last), 0))
    return cur, prev, nxt, tcur, tprev, tnxt


def _mixer_param_specs():
    return [_full((1, HALF)), _full((1, HALF)), _full((SG_GROUPS, SG_CHUNK, SG_CHUNK)), _full((SG_CHUNK, 128)),
            _full((32, HALF)), _full((1, HALF)), _full((1, HALF)), _full((1, HALF)), _full((1, 128)), _full((8, HALF))]


def _mixers_fwd(proj, cos_t, sin_t, mp, TB, name):
    T = proj.shape[0]
    nb = T // TB
    r = TB // HALO
    cur, prev, _, tcur, tprev, _ = _mixer_in_specs(TB, nb)

    def body(zc_ref, zp_ref, cc_ref, sc_ref, cp_ref, sp_ref,
             lg_ref, lb_ref, sgw_ref, sgb_ref, cvw_ref, cvb_ref, cvg_ref, cvbb_ref, sinks_ref, scw_ref,
             ys_ref, scr_ref, k_ref, v_ref, sh_ref):
        i = pl.program_id(0)
        pm = (i > 0).astype(F32)

        def colsE(c0, c1):
            return jnp.concatenate([zp_ref[:, c0:c1].astype(F32) * pm, zc_ref[:, c0:c1].astype(F32)], axis=0)

        a = _gelu(zc_ref[:, C_ZA:C_ZA + 2 * HALF].astype(F32))
        u = a[:, :HALF]
        vn, _, _ = _ln_fwd(a[:, HALF:], lg_ref[...], lb_ref[...])
        vnb = vn.astype(BF16)
        tril = _tril_mask()
        for g in range(SG_GROUPS):
            wt = jnp.where(tril, sgw_ref[g], 0.0).astype(BF16)
            for ci in range(r):
                rows = slice(ci * SG_CHUNK, (ci + 1) * SG_CHUNK)
                cols = slice(g * 128, (g + 1) * 128)
                mixed = _dot(wt, vnb[rows, cols]) + sgb_ref[:, g:g + 1]
                ys_ref[rows, g * 128:(g + 1) * 128] = (u[rows, cols] * mixed).astype(BF16)

        def colsB(c0, c1):
            return jnp.concatenate([zp_ref[HALO - CV_PAD:, c0:c1].astype(F32) * pm, zc_ref[:, c0:c1].astype(F32)], axis=0)

        sh_ref[0] = colsB(C_ZB, C_ZB + HALF) * _sig(colsB(C_ZB + HALF, C_ZB + 2 * HALF))
        _sublane_shifts(sh_ref, TB + CV_PAD)
        c = jnp.broadcast_to(cvb_ref[...], (TB, HALF))
        for k in range(CV_KERNEL):
            c = c + cvw_ref[k:k + 1, :] * _tap(sh_ref, CV_PAD - (CV_KERNEL - 1) + k, TB)
        n, _, _ = _ln_fwd(c, cvg_ref[...], cvbb_ref[...])
        ys_ref[:, HALF:2 * HALF] = (n * _sig(n)).astype(BF16)

        zd = colsE(C_ZD + HALF, C_ZD + 3 * HALF)
        scr_ref[...] = zd[:, :HALF] * zd[:, HALF:]
        cv = None
        for k in range(SC_KERNEL):
            t = scw_ref[k:k + 1, :] * scr_ref[pl.ds(HALO - (SC_KERNEL - 1) + k, TB), :]
            cv = t if cv is None else cv + t
        ys_ref[:, 3 * HALF:4 * HALF] = (zc_ref[:, C_ZD:C_ZD + HALF].astype(F32) * cv).astype(BF16)

        cosE = jnp.concatenate([cp_ref[...], cc_ref[...]], axis=0)
        sinE = jnp.concatenate([sp_ref[...], sc_ref[...]], axis=0)
        k_ref[...] = _rope(colsE(C_K, C_K + 128), cosE, sinE).astype(BF16)
        v_ref[...] = colsE(C_V, C_V + 128).astype(BF16)
        cosC, sinC = cc_ref[...], sc_ref[...]
        q = jnp.concatenate([_rope(zc_ref[:, C_Q + 128 * j:C_Q + 128 * (j + 1)].astype(F32), cosC, sinC)
                             for j in range(4)], axis=1).astype(BF16)
        for qb in range(r):
            first_ok = (i * r + qb) > 0
            for h in range(N_KV_HEADS):
                hc = slice(h * HEAD_DIM, (h + 1) * HEAD_DIM)
                kh = k_ref[qb * WINDOW:qb * WINDOW + 2 * WINDOW, hc]
                vh = v_ref[qb * WINDOW:qb * WINDOW + 2 * WINDOW, hc]
                qs = jnp.concatenate([q[qb * WINDOW:(qb + 1) * WINDOW, (h * Q_PER_KV + g) * HEAD_DIM:(h * Q_PER_KV + g + 1) * HEAD_DIM]
                                      for g in range(Q_PER_KV)], axis=0)
                probs, _ = _attn_probs(qs, kh, _sink_col(sinks_ref, h), first_ok)
                o = _dot(probs.astype(BF16), vh)
                for g in range(Q_PER_KV):
                    c0 = 2 * HALF + (h * Q_PER_KV + g) * HEAD_DIM
                    ys_ref[qb * WINDOW:(qb + 1) * WINDOW, c0:c0 + HEAD_DIM] = o[g * WINDOW:(g + 1) * WINDOW].astype(BF16)

    return pl.pallas_call(
        body, name=name, grid=(nb,),
        in_specs=[cur, prev, tcur, tcur, tprev, tprev] + _mixer_param_specs(),
        out_specs=pl.BlockSpec((TB, 4 * HALF), lambda i: (i, 0)),
        out_shape=jax.ShapeDtypeStruct((T, 4 * HALF), BF16),
        scratch_shapes=[pltpu.VMEM((TB + HALO, HALF), F32), pltpu.VMEM((TB + HALO, 128), BF16), pltpu.VMEM((TB + HALO, 128), BF16),
                        pltpu.VMEM((8, TB + CV_PAD, HALF), F32)],
        compiler_params=_params("parallel"),
    )(proj, proj, cos_t, sin_t, cos_t, sin_t, *mp)


def _mixers_bwd(proj, dys, dproj, cos_t, sin_t, mp, TB, name):
    T = proj.shape[0]
    nb = T // TB
    r = TB // HALO
    RE = TB + 2 * HALO
    RC = TB + HALO
    cur, prev, nxt, tcur, tprev, tnxt = _mixer_in_specs(TB, nb)
    dcur = pl.BlockSpec((TB, 4 * HALF), lambda i: (i, 0))
    dnxt = pl.BlockSpec((HALO, 4 * HALF), lambda i: (jnp.minimum((i + 1) * r, nb * r - 1), 0))

    def body(zc_ref, zp_ref, zn_ref, dyc_ref, dyn_ref, cc_ref, sc_ref, cp_ref, sp_ref, cn_ref, sn_ref,
             lg_ref, lb_ref, sgw_ref, sgb_ref, cvw_ref, cvb_ref, cvg_ref, cvbb_ref, sinks_ref, scw_ref, dp_in_ref,
             dz_ref, dlg_ref, dlb_ref, dsgw_ref, dsgb_ref, dcvw_ref, dcvb_ref, dcvg_ref, dcvbb_ref, dsink_ref, dscw_ref,
             scr_ref, scr2_ref, k_ref, v_ref, dk_ref, dv_ref, dq_ref, sh_ref, sh2_ref):
        del dp_in_ref
        i = pl.program_id(0)
        pm = (i > 0).astype(F32)
        nm = (i < nb - 1).astype(F32)

        @pl.when(i == 0)
        def _():
            for ref in (dlg_ref, dlb_ref, dsgw_ref, dsgb_ref, dcvw_ref, dcvb_ref, dcvg_ref, dcvbb_ref, dsink_ref, dscw_ref):
                ref[...] = jnp.zeros_like(ref)

        def colsE(c0, c1):
            return jnp.concatenate([zp_ref[:, c0:c1].astype(F32) * pm, zc_ref[:, c0:c1].astype(F32),
                                    zn_ref[:, c0:c1].astype(F32)], axis=0)

        def colsC(c0, c1):
            return jnp.concatenate([zc_ref[:, c0:c1].astype(F32), zn_ref[:, c0:c1].astype(F32)], axis=0)

        def dyC(c0, c1):
            return jnp.concatenate([dyc_ref[:, c0:c1].astype(F32), dyn_ref[:, c0:c1].astype(F32) * nm], axis=0)

        za = zc_ref[:, C_ZA:C_ZA + 2 * HALF].astype(F32)
        a = _gelu(za)
        u = a[:, :HALF]
        lg = lg_ref[...]
        vn, vh, rs = _ln_fwd(a[:, HALF:], lg, lb_ref[...])
        vnb = vn.astype(BF16)
        dya = dyc_ref[:, 0:HALF].astype(F32)
        tril = _tril_mask()
        lane128 = lax.broadcasted_iota(jnp.int32, (SG_CHUNK, 128), 1)
        du_parts, dvn_parts = [], []
        for ci in range(r):
            rows = slice(ci * SG_CHUNK, (ci + 1) * SG_CHUNK)
            du_g, dvn_g = [], []
            for g in range(SG_GROUPS):
                cols = slice(g * 128, (g + 1) * 128)
                wt = jnp.where(tril, sgw_ref[g], 0.0).astype(BF16)
                vb = vnb[rows, cols]
                mixed = _dot(wt, vb) + sgb_ref[:, g:g + 1]
                dy_blk = dya[rows, cols]
                du_g.append(dy_blk * mixed)
                dmix = dy_blk * u[rows, cols]
                dmb = dmix.astype(BF16)
                dvn_g.append(_dot_tn(wt, dmb))
                dsgw_ref[g] += jnp.where(tril, _dot_nt(dmb, vb), 0.0)
                dsgb_ref[...] += jnp.where(lane128 == g, jnp.sum(dmix, axis=1, keepdims=True), 0.0)
            du_parts.append(jnp.concatenate(du_g, axis=1))
            dvn_parts.append(jnp.concatenate(dvn_g, axis=1))
        du = jnp.concatenate(du_parts, axis=0) if r > 1 else du_parts[0]
        dvn = jnp.concatenate(dvn_parts, axis=0) if r > 1 else dvn_parts[0]
        dlg_ref[...] += jnp.sum(dvn * vh, axis=0, keepdims=True)
        dlb_ref[...] += jnp.sum(dvn, axis=0, keepdims=True)
        dvv = _ln_bwd(dvn, vh, rs, lg)
        gg = _gelu_grad(za)
        dz_ref[:, C_ZA:C_ZA + HALF] = (du * gg[:, :HALF]).astype(BF16)
        dz_ref[:, C_ZA + HALF:C_ZA + 2 * HALF] = (dvv * gg[:, HALF:]).astype(BF16)

        RB = TB + CV_PAD

        def colsB(c0, c1):
            return jnp.concatenate([zp_ref[HALO - CV_PAD:, c0:c1].astype(F32) * pm, zc_ref[:, c0:c1].astype(F32),
                                    zn_ref[:CV_PAD, c0:c1].astype(F32)], axis=0)

        sh_ref[0] = colsB(C_ZB, C_ZB + HALF) * _sig(colsB(C_ZB + HALF, C_ZB + 2 * HALF))
        _sublane_shifts(sh_ref, RB + CV_PAD)
        c = jnp.broadcast_to(cvb_ref[...], (RB, HALF))
        for k in range(CV_KERNEL):
            c = c + cvw_ref[k:k + 1, :] * _tap(sh_ref, CV_PAD - (CV_KERNEL - 1) + k, RB)
        cvg = cvg_ref[...]
        n, ch, rc = _ln_fwd(c, cvg, cvbb_ref[...])
        sn = _sig(n)
        dyb = jnp.concatenate([dyc_ref[:, HALF:2 * HALF].astype(F32), dyn_ref[:CV_PAD, HALF:2 * HALF].astype(F32) * nm], axis=0)
        dn = dyb * (sn + n * sn * (1.0 - sn))
        dno = dn[:TB]
        dcvg_ref[...] += jnp.sum(dno * ch[:TB], axis=0, keepdims=True)
        dcvbb_ref[...] += jnp.sum(dno, axis=0, keepdims=True)
        dc = _ln_bwd(dn, ch, rc, cvg)
        sh2_ref[0] = dc
        _sublane_shifts(sh2_ref, RB)
        dcvb_ref[...] += jnp.sum(dc[:TB], axis=0, keepdims=True)
        dy0 = None
        for k in range(CV_KERNEL):
            wk = cvw_ref[k:k + 1, :]
            t = wk * _tap(sh2_ref, CV_KERNEL - 1 - k, TB)
            dy0 = t if dy0 is None else dy0 + t
            dcvw_ref[k:k + 1, :] += jnp.sum(dc[:TB] * _tap(sh_ref, CV_PAD - (CV_KERNEL - 1) + k, TB), axis=0, keepdims=True)
        ab = zc_ref[:, C_ZB:C_ZB + HALF].astype(F32)
        sg = _sig(zc_ref[:, C_ZB + HALF:C_ZB + 2 * HALF].astype(F32))
        dz_ref[:, C_ZB:C_ZB + HALF] = (dy0 * sg).astype(BF16)
        dz_ref[:, C_ZB + HALF:C_ZB + 2 * HALF] = (dy0 * ab * sg * (1.0 - sg)).astype(BF16)

        zd = colsE(C_ZD + HALF, C_ZD + 3 * HALF)
        scr_ref[...] = zd[:, :HALF] * zd[:, HALF:]
        dcv = dyC(3 * HALF, 4 * HALF) * colsC(C_ZD, C_ZD + HALF)
        scr2_ref[...] = dcv
        cv = None
        dud = None
        for k in range(SC_KERNEL):
            wk = scw_ref[k:k + 1, :]
            us = scr_ref[pl.ds(HALO - (SC_KERNEL - 1) + k, TB), :]
            t = wk * us
            cv = t if cv is None else cv + t
            t2 = wk * scr2_ref[pl.ds(SC_KERNEL - 1 - k, TB), :]
            dud = t2 if dud is None else dud + t2
            dscw_ref[k:k + 1, :] += jnp.sum(dcv[:TB] * us, axis=0, keepdims=True)
        dz_ref[:, C_ZD:C_ZD + HALF] = (dyc_ref[:, 3 * HALF:4 * HALF].astype(F32) * cv).astype(BF16)
        dz_ref[:, C_ZD + HALF:C_ZD + 2 * HALF] = (dud * zc_ref[:, C_ZD + 2 * HALF:C_ZD + 3 * HALF].astype(F32)).astype(BF16)
        dz_ref[:, C_ZD + 2 * HALF:C_ZD + 3 * HALF] = (dud * zc_ref[:, C_ZD + HALF:C_ZD + 2 * HALF].astype(F32)).astype(BF16)

        cosE = jnp.concatenate([cp_ref[...], cc_ref[...], cn_ref[...]], axis=0)
        sinE = jnp.concatenate([sp_ref[...], sc_ref[...], sn_ref[...]], axis=0)
        k_ref[...] = _rope(colsE(C_K, C_K + 128), cosE, sinE).astype(BF16)
        v_ref[...] = colsE(C_V, C_V + 128).astype(BF16)
        dk_ref[...] = jnp.zeros_like(dk_ref)
        dv_ref[...] = jnp.zeros_like(dv_ref)
        q = jnp.concatenate([_rope(colsC(C_Q + 128 * j, C_Q + 128 * (j + 1)), cosE[HALO:], sinE[HALO:])
                             for j in range(4)], axis=1).astype(BF16)
        dO = dyC(2 * HALF, 3 * HALF).astype(BF16)
        lane_s = lax.broadcasted_iota(jnp.int32, (1, 128), 1)
        for qb in range(r + 1):
            first_ok = (i * r + qb) > 0
            rows = slice(qb * WINDOW, (qb + 1) * WINDOW)
            band = slice(qb * WINDOW, qb * WINDOW + 2 * WINDOW)
            for h in range(N_KV_HEADS):
                hc = slice(h * HEAD_DIM, (h + 1) * HEAD_DIM)
                kh = k_ref[band, hc]
                vh_ = v_ref[band, hc]
                heads = [slice((h * Q_PER_KV + g) * HEAD_DIM, (h * Q_PER_KV + g + 1) * HEAD_DIM) for g in range(Q_PER_KV)]
                qs = jnp.concatenate([q[rows, hs] for hs in heads], axis=0)
                dos = jnp.concatenate([dO[rows, hs] for hs in heads], axis=0)
                probs, p_sink = _attn_probs(qs, kh, _sink_col(sinks_ref, h), first_ok)
                dP = _dot_nt(dos, vh_)
                rsum = jnp.sum(probs * dP, axis=-1, keepdims=True)
                dS = (probs * (dP - rsum) * (HEAD_DIM ** -0.5)).astype(BF16)
                dk_ref[band, hc] += _dot_tn(dS, qs)
                dv_ref[band, hc] += _dot_tn(probs.astype(BF16), dos)
                if qb < r:
                    dqs = _dot(dS, kh)
                    dsk = -p_sink * rsum
                    for g in range(Q_PER_KV):
                        dq_ref[rows, heads[g]] = dqs[g * WINDOW:(g + 1) * WINDOW]
                        dsink_ref[...] += jnp.where(lane_s == h * Q_PER_KV + g, jnp.sum(dsk[g * WINDOW:(g + 1) * WINDOW]), 0.0)
        cosC, sinC = cc_ref[...], sc_ref[...]
        for j in range(4):
            dz_ref[:, C_Q + 128 * j:C_Q + 128 * (j + 1)] = _rope_t(dq_ref[:, 128 * j:128 * (j + 1)], cosC, sinC).astype(BF16)
        dz_ref[:, C_K:C_K + 128] = _rope_t(dk_ref[HALO:HALO + TB, :], cosC, sinC).astype(BF16)
        dz_ref[:, C_V:C_V + 128] = dv_ref[HALO:HALO + TB, :].astype(BF16)

    small = [((1, HALF), F32), ((1, HALF), F32), ((SG_GROUPS, SG_CHUNK, SG_CHUNK), F32), ((SG_CHUNK, 128), F32),
             ((32, HALF), F32), ((1, HALF), F32), ((1, HALF), F32), ((1, HALF), F32), ((1, 128), F32), ((8, HALF), F32)]
    outs = pl.pallas_call(
        body, name=name, grid=(nb,),
        in_specs=[cur, prev, nxt, dcur, dnxt, tcur, tcur, tprev, tprev, tnxt, tnxt] + _mixer_param_specs()
                 + [pl.BlockSpec(memory_space=pl.ANY)],
        out_specs=[pl.BlockSpec((TB, MIX_W), lambda i: (i, 1))] + [_full(s) for s, _ in small],
        out_shape=[jax.ShapeDtypeStruct((T, PROJ_PAD), BF16)] + [jax.ShapeDtypeStruct(s, d) for s, d in small],
        scratch_shapes=[pltpu.VMEM((RE, HALF), F32), pltpu.VMEM((RC, HALF), F32), pltpu.VMEM((RE, 128), BF16), pltpu.VMEM((RE, 128), BF16),
                        pltpu.VMEM((RE, 128), F32), pltpu.VMEM((RE, 128), F32), pltpu.VMEM((TB, HALF), F32),
                        pltpu.VMEM((8, TB + 2 * CV_PAD, HALF), F32), pltpu.VMEM((8, TB + CV_PAD, HALF), F32)],
        input_output_aliases={21: 0},
        compiler_params=_params("arbitrary"),
    )(proj, proj, proj, dys, dys, cos_t, sin_t, cos_t, sin_t, cos_t, sin_t, *mp, dproj)
    return outs


def _rope_tables(T):
    pos = jnp.arange(T, dtype=F32)
    inv_freq = 1.0 / (ROPE_THETA ** (jnp.arange(0, HEAD_DIM, 2, dtype=F32) / HEAD_DIM))
    ang = pos[:, None] * inv_freq[None, :]
    cos, sin = jnp.cos(ang), jnp.sin(ang)
    cos_t = jnp.concatenate([cos, cos, cos, cos], axis=1)
    sin_t = jnp.concatenate([-sin, sin, -sin, sin], axis=1)
    return cos_t, sin_t


def _mixer_params(l, sg_ln_g, sg_ln_b, sg_w, sg_b, cv_w, cv_b, cv_ln_g, cv_ln_b, attn_sinks, sc_w):
    sgb_t = jnp.zeros((SG_CHUNK, 128), F32).at[:, :SG_GROUPS].set(sg_b[l].T)
    cvw = jnp.zeros((32, HALF), F32).at[:CV_KERNEL].set(cv_w[l])
    scw = jnp.zeros((8, HALF), F32).at[:SC_KERNEL].set(sc_w[l])
    sinks = jnp.zeros((1, 128), F32).at[0, :N_Q_HEADS].set(attn_sinks[l])
    return [sg_ln_g[l][None], sg_ln_b[l][None], sg_w[l], sgb_t, cvw, cv_b[l][None], cv_ln_g[l][None], cv_ln_b[l][None], sinks, scw]


def _w_in_layout(w_in_g):
    cut = MIX_W - 2 * W_IN_SHARD
    return jnp.concatenate([w_in_g[2][cut:], w_in_g[3], jnp.zeros((MIX_W - GATE_W, D_MODEL), w_in_g.dtype),
                            w_in_g[0], w_in_g[1], w_in_g[2][:cut]], axis=0)


def _w_in_unlayout(dw):
    cut = MIX_W - 2 * W_IN_SHARD
    return jnp.stack([dw[MIX_W:MIX_W + W_IN_SHARD], dw[MIX_W + W_IN_SHARD:MIX_W + 2 * W_IN_SHARD],
                      jnp.concatenate([dw[MIX_W + 2 * W_IN_SHARD:], dw[:W_IN_SHARD - cut]], axis=0),
                      dw[W_IN_SHARD - cut:GATE_W]], axis=0)


def _device_step(x, tgt, norm_mix, norm_ffn, norm_final, mixer_params, w_in_p, wb_g, wo_g, wgu_g, wd_g):
    T = x.shape[0]
    tables = _rope_tables(T)
    saved = []
    for l in range(DEPTH):
        lw = dict(w_in=w_in_p[l], w_branch=wb_g[l], w_out=wo_g[l], w_gate_up=wgu_g[l], w_down=wd_g[l],
                  norm_mix=norm_mix[l][None], norm_ffn=norm_ffn[l][None], mixer=mixer_params[l], after=jnp.zeros((8, 128), F32))
        x, sv = _fwd_layer(l, x, lw, tables)
        saved.append((lw, sv))
    dx, dnf, loss = _final_loss(x, norm_final[None], tgt, 256, "final_loss")
    grads = [None] * DEPTH
    for l in reversed(range(DEPTH)):
        lw, sv = saved[l]
        dxm, g_ffn = _bwd_layer_ffn(l, dx, lw, sv)
        dx, g_mix = _bwd_layer_mix(l, dxm, lw, sv, tables)
        raw = {**g_ffn, **g_mix}
        grads[l] = {**raw, **_small_views(raw)}
    return loss, dx, dnf[0], grads


MIX_BLOCK = 256


def _fwd_layer(l, x, lw, tables):
    return _fwd_layer_rest(l, x, _fwd_layer_mix(l, x, lw, tables), lw)


def _fwd_layer_mix(l, x, lw, tables):
    proj, xn = _rms_mm(x, lw["norm_mix"], lw["w_in"], min(x.shape[0], 1024), 2176, f"proj{l}")
    return proj, xn, _mixers_fwd(proj, *tables, lw["mixer"], MIX_BLOCK, f"mixers_fwd{l}")


def _fwd_layer_rest(l, x, mixed, lw):
    proj, xn, ys = mixed
    TM = min(x.shape[0], 1024)
    xm, merged = _merge_fwd(x, ys, proj, lw["w_branch"], lw["w_out"], 256, f"merge_fwd{l}")
    gu, hn = _rms_mm(xm, lw["norm_ffn"], lw["w_gate_up"], TM, GU_SHARD, f"ffn_up{l}")
    x_out = _ffn_down(xm, gu, lw["w_down"], 256, f"ffn_down{l}")
    return x_out, (x, proj, xn, ys, xm, merged, gu, hn)


def _bwd_layer_ffn(l, dx, lw, sv):
    x_in, proj, xn, ys, xm, merged, gu, hn = sv
    T = dx.shape[0]
    tkk = min(T, 1024)
    gk = T // tkk
    dgu, act = _swiglu_bwd(dx, gu, lw["w_down"], 256, f"swiglu_bwd{l}", lw["after"])
    d_wd = _mm_tn(act, dx, (2, 1, gk), (tkk, D_FF // 2), lambda i, j, k: (k, i), (tkk, D_MODEL), lambda i, j, k: (k, 0),
                  (D_FF, D_MODEL), (D_FF // 2, D_MODEL), lambda i, j, k: (i, 0), f"dw_down{l}")
    d_wgu = _mm_tn(hn, dgu, (1, N_CHIPS, gk), (tkk, D_MODEL), lambda i, j, k: (k, 0), (tkk, GU_SHARD), lambda i, j, k: (k, j),
                   (N_CHIPS, D_MODEL, GU_SHARD), (None, D_MODEL, GU_SHARD), lambda i, j, k: (j, 0, 0), f"dw_gate_up{l}")
    dxm, d_nffn = _mm_nt_rmsbwd(dgu, lw["w_gate_up"], xm, lw["norm_ffn"], dx, min(T, 512), GU_SHARD, f"ffn_up_bwd{l}")
    dys, dbr, dproj = _merge_bwd(dxm, ys, proj, lw["w_branch"], lw["w_out"], 256, f"merge_bwd{l}")
    d_wo = _mm_tn(merged, dxm, (2, 1, gk), (tkk, 512), lambda i, j, k: (k, i), (tkk, D_MODEL), lambda i, j, k: (k, 0),
                  (D_MODEL, D_MODEL), (512, D_MODEL), lambda i, j, k: (i, 0), f"dw_out{l}")
    d_wb = _mm_tn(ys, dbr, (N_BRANCH, N_CHIPS, gk), (tkk, HALF), lambda i, j, k: (k, i), (tkk, 256), lambda i, j, k: (k, i * N_CHIPS + j),
                  (N_CHIPS, N_BRANCH, HALF, 256), (None, None, HALF, 256), lambda i, j, k: (j, i, 0, 0), f"dw_branch{l}")
    return (dxm, dys, dproj), dict(w_branch=d_wb, w_out=d_wo, w_gate_up=d_wgu, w_down=d_wd, norm_ffn=d_nffn)


def _bwd_layer_mix(l, carry, lw, sv, tables):
    dxm, dys, dproj = carry
    x_in, proj, xn, ys, xm, merged, gu, hn = sv
    T = dxm.shape[0]
    tkk = min(T, 1024)
    gk = T // tkk
    mb = _mixers_bwd(proj, dys, dproj, *tables, lw["mixer"], MIX_BLOCK, f"mixers_bwd{l}")
    dproj = mb[0]
    d_win = _mm_tn(dproj, xn, (PROJ_PAD // 2176, 1, gk), (tkk, 2176), lambda i, j, k: (k, i), (tkk, D_MODEL), lambda i, j, k: (k, 0),
                   (PROJ_PAD, D_MODEL), (2176, D_MODEL), lambda i, j, k: (i, 0), f"dw_in{l}")
    dx, d_nmix = _mm_nt_rmsbwd(dproj, lw["w_in"], x_in, lw["norm_mix"], dxm, min(T, 512), 2176, f"proj_bwd{l}")
    return dx, dict(w_in=d_win, norm_mix=d_nmix, sg_ln_g=mb[1], sg_ln_b=mb[2], sg_w=mb[3], sg_b=mb[4], cv_w=mb[5], cv_b=mb[6],
                    cv_ln_g=mb[7], cv_ln_b=mb[8], attn_sinks=mb[9], sc_w=mb[10])


ANY = pl.BlockSpec(memory_space=pl.ANY)
BIG = ("w_in", "w_branch", "w_out", "w_gate_up", "w_down")
HALF_SHAPE = {"w_in": (2, W_IN_SHARD // 2, D_MODEL), "w_branch": (2, 1024, 256), "w_out": (2, 128, D_MODEL),
              "w_gate_up": (2, 512, GU_SHARD), "w_down": (2, 352, D_MODEL)}
NB = len(BIG)


def _place():
    x, y, c = lax.axis_index("x"), lax.axis_index("y"), lax.axis_index("c")
    chips = [(1 - x, y), (x, 1 - y), (1 - x, 1 - y)]
    return x, y, c, 2 * x + y, chips, [2 * px + py for px, py in chips]


def _remote(src, dst, ssem, rsem, dev):
    return pltpu.make_async_remote_copy(src_ref=src, dst_ref=dst, send_sem=ssem, recv_sem=rsem, device_id=dev, device_id_type=MESH)


HBM_SPEC = pl.BlockSpec(memory_space=pltpu.HBM)
SEM_SPEC = pl.BlockSpec(memory_space=pltpu.SEMAPHORE)
DATAFLOW = pltpu.SideEffectType.DATAFLOW_SIDE_EFFECTING


def _ici_ends(kind, src, land, j, c, chip, chip_ids):
    if kind == "gather":
        return src.at[c], land.at[chip, c], land.at[chip_ids[j], c]
    return src.at[chip_ids[j]], land.at[chip], land.at[chip_ids[j]]


def _ici_start(kind, srcs, land_shapes, name):
    n = len(srcs)

    def body(*refs):
        src, land = refs[:n], refs[n:2 * n]
        ssem, rsem, token = refs[2 * n], refs[2 * n + 1], refs[-1]
        x, y, c, chip, chips, chip_ids = _place()
        for k in range(n):
            for j in range(3):
                s, d, _ = _ici_ends(kind, src[k], land[k], j, c, chip, chip_ids)
                _remote(s, d, ssem.at[3 * k + j], rsem.at[3 * k + j], (*chips[j], c)).start()
        token[...] = jnp.zeros_like(token)

    sem = pltpu.SemaphoreType.DMA((3 * n,))
    outs = pl.pallas_call(
        body, name=name,
        out_shape=(sem, sem, *[pltpu.HBM(s.shape, s.dtype) for s in srcs], *[pltpu.HBM(sh, BF16) for sh in land_shapes],
                   jax.ShapeDtypeStruct((8, 128), F32)),
        in_specs=[HBM_SPEC] * (2 * n),
        out_specs=(SEM_SPEC, SEM_SPEC, *[HBM_SPEC] * (2 * n), pl.BlockSpec(memory_space=pltpu.VMEM)),
        input_output_aliases={i: 2 + i for i in range(2 * n)},
        compiler_params=pltpu.CompilerParams(has_side_effects=DATAFLOW),
    )(*[pltpu.with_memory_space_constraint(s, pltpu.HBM) for s in srcs],
      *[pltpu.with_memory_space_constraint(lax.empty(sh, BF16), pltpu.HBM) for sh in land_shapes])
    return (kind, outs[0], outs[1], list(outs[2:2 + n]), list(outs[2 + n:2 + 2 * n])), outs[-1]


def _ici_wait(handle, after, name):
    kind, ssem_in, rsem_in, srcs, lands = handle
    n = len(srcs)

    def body(*refs):
        src, land = refs[:n], refs[n:2 * n]
        ssem, rsem = refs[2 * n], refs[2 * n + 1]
        x, y, c, chip, chips, chip_ids = _place()
        for k in range(n):
            for j in range(3):
                s, _, mine = _ici_ends(kind, src[k], land[k], j, c, chip, chip_ids)
                cp = _remote(s, mine, ssem.at[3 * k + j], rsem.at[3 * k + j], (*chips[j], c))
                cp.wait_send()
                cp.wait_recv()

    outs = pl.pallas_call(
        body, name=name, out_shape=[pltpu.HBM(t.shape, t.dtype) for t in srcs + lands],
        in_specs=[HBM_SPEC] * (2 * n) + [SEM_SPEC, SEM_SPEC, ANY], out_specs=[HBM_SPEC] * (2 * n),
        input_output_aliases={i: i for i in range(2 * n)},
        compiler_params=pltpu.CompilerParams(has_side_effects=DATAFLOW),
    )(*srcs, *lands, ssem_in, rsem_in, after)
    return list(outs[:n]), list(outs[n:])


def _ag_pair(shards, lands, name):
    n = len(shards)

    def body(*refs):
        ins, outs = refs[:n], refs[2 * n:3 * n]
        token = refs[3 * n]
        s_fwd, r_fwd, s_own, r_own = refs[3 * n + 1:]
        x, y, c, chip, chips, chip_ids = _place()
        sib = (x, y, 1 - c)
        cps = []
        for k in range(n):
            cp = _remote(ins[k], outs[k].at[chip], s_own.at[k], r_own.at[k], sib)
            cp.start()
            cps.append(cp)
            for j in range(3):
                got = outs[k].at[chip_ids[j], c]
                cp = _remote(got, got, s_fwd.at[k, j], r_fwd.at[k, j], sib)
                cp.start()
                cps.append(cp)
        for k in range(n):
            _remote(ins[k], outs[k].at[chip], s_own.at[k], r_own.at[k], sib).wait_recv()
            for j in range(3):
                got = outs[k].at[chip_ids[j], 1 - c]
                _remote(got, got, s_fwd.at[k, j], r_fwd.at[k, j], sib).wait_recv()
        for cp in cps:
            cp.wait_send()
        token[...] = jnp.zeros_like(token)

    sem, sem1 = pltpu.SemaphoreType.DMA((n, 3)), pltpu.SemaphoreType.DMA((n,))
    outs = pl.pallas_call(
        body, name=name, out_shape=[jax.ShapeDtypeStruct(t.shape, t.dtype) for t in lands] + [jax.ShapeDtypeStruct((8, 128), F32)],
        in_specs=[ANY] * (2 * n), out_specs=[ANY] * n + [pl.BlockSpec(memory_space=pltpu.VMEM)],
        input_output_aliases={n + k: k for k in range(n)},
        scratch_shapes=[sem, sem, sem1, sem1], compiler_params=pltpu.CompilerParams(has_side_effects=True),
    )(*shards, *lands)
    return list(outs[:n]), outs[n]


def _rs_pair(grads, name):
    n_arr = len(grads)

    def body(*refs):
        ins, got = refs[:n_arr], refs[n_arr:2 * n_arr]
        ssem, rsem = refs[2 * n_arr:]
        x, y, c, _, _, _ = _place()
        sib = (x, y, 1 - c)
        sends = []
        for k in reversed(range(n_arr)):
            for q in range(N_CHIPS):
                cp = _remote(ins[k].at[q, 1 - c], got[k].at[q], ssem.at[k, q], rsem.at[k, q], sib)
                cp.start()
                sends.append(cp)
        for k in range(n_arr):
            for q in range(N_CHIPS):
                _remote(got[k].at[q], got[k].at[q], ssem.at[k, q], rsem.at[k, q], sib).wait_recv()
        for cp in sends:
            cp.wait_send()

    shp = [jax.ShapeDtypeStruct((N_CHIPS,) + g.shape[2:], BF16) for g in grads]
    sem = pltpu.SemaphoreType.DMA((n_arr, N_CHIPS))
    outs = pl.pallas_call(
        body, name=name, out_shape=shp, in_specs=[ANY] * n_arr, out_specs=[ANY] * n_arr,
        scratch_shapes=[sem, sem], compiler_params=pltpu.CompilerParams(has_side_effects=True),
    )(*grads)
    return list(outs)


def _rs_share(bufs, name):
    n = len(bufs)

    def body(*refs):
        outs = refs[n:2 * n]
        ssem, rsem = refs[2 * n:]
        x, y, c, _, _, _ = _place()
        sib = (x, y, 1 - c)
        sends = []
        for k in range(n):
            for l in range(DEPTH):
                cp = _remote(outs[k].at[l, c], outs[k].at[l, c], ssem.at[k, l], rsem.at[k, l], sib)
                cp.start()
                sends.append(cp)
        for k in range(n):
            for l in range(DEPTH):
                dst = outs[k].at[l, 1 - c]
                _remote(dst, dst, ssem.at[k, l], rsem.at[k, l], sib).wait_recv()
        for cp in sends:
            cp.wait_send()

    sem = pltpu.SemaphoreType.DMA((n, DEPTH))
    outs = pl.pallas_call(
        body, name=name, out_shape=[jax.ShapeDtypeStruct(b.shape, b.dtype) for b in bufs], in_specs=[ANY] * n, out_specs=[ANY] * n,
        input_output_aliases={k: k for k in range(n)},
        scratch_shapes=[sem, sem], compiler_params=pltpu.CompilerParams(has_side_effects=True),
    )(*bufs)
    return list(outs)


def _piece(src, idx, rows, width=128, align=1, transposed=False):
    return dict(src=src, idx=idx, rows=rows, width=width, align=align, transposed=transposed)


def _all_reduce_pieces(inputs, pieces, out_shapes, writes, name):
    n_in, n_out = len(inputs), len(out_shapes)
    offs, R = [], 0
    for p in pieces:
        R = -(-R // p["align"]) * p["align"]
        offs.append(R)
        R += p["rows"]
    R = -(-R // 8) * 8
    dirs = [(dx, dy, dc) for dx in (0, 1) for dy in (0, 1) for dc in (0, 1)][1:]

    def body(*refs):
        ins, outs, token_ref = refs[:n_in], refs[n_in:n_in + n_out], refs[n_in + n_out]
        gather_ref, sum_ref, ssem, rsem = refs[n_in + n_out + 1:]
        token_ref[...] = jnp.zeros_like(token_ref)
        x, y, c = lax.axis_index("x"), lax.axis_index("y"), lax.axis_index("c")
        me = 4 * x + 2 * y + c
        gather_ref[me] = jnp.zeros((R, 128), F32)
        for p, off in zip(pieces, offs):
            v = ins[p["src"]][...].T[p["idx"]] if p["transposed"] else ins[p["src"]][p["idx"]]
            gather_ref[me, off:off + p["rows"], 0:p["width"]] = v
        flip = lambda v, d: 1 - v if d else v
        peers = [(flip(x, dx), flip(y, dy), flip(c, dc)) for dx, dy, dc in dirs]
        cps = [_remote(gather_ref.at[me], gather_ref.at[me], ssem.at[k], rsem.at[k], peers[k]) for k in range(N_DEV - 1)]
        for cp in cps:
            cp.start()
        for k, (px, py, pc) in enumerate(peers):
            slot = gather_ref.at[4 * px + 2 * py + pc]
            _remote(slot, slot, ssem.at[k], rsem.at[k], peers[k]).wait_recv()
        acc = gather_ref[0]
        for s in range(1, N_DEV):
            acc = acc + gather_ref[s]
        sum_ref[...] = acc
        for o, idx, p in writes:
            outs[o][idx] = sum_ref[offs[p]:offs[p] + pieces[p]["rows"], 0:pieces[p]["width"]]
        for cp in cps:
            cp.wait_send()

    vm = pl.BlockSpec(memory_space=pltpu.VMEM)
    outs = pl.pallas_call(
        body, name=name, out_shape=[jax.ShapeDtypeStruct(s, F32) for s in out_shapes] + [jax.ShapeDtypeStruct((8, 128), F32)],
        in_specs=[vm] * n_in, out_specs=[vm] * (n_out + 1),
        scratch_shapes=[pltpu.VMEM((N_DEV, R, 128), F32), pltpu.VMEM((R, 128), F32),
                        pltpu.SemaphoreType.DMA((N_DEV - 1,)), pltpu.SemaphoreType.DMA((N_DEV - 1,))],
        compiler_params=pltpu.CompilerParams(vmem_limit_bytes=VMEM_LIMIT),
    )(*inputs)
    return list(outs[:n_out]), outs[n_out]


def _lanes(width):
    return [slice(k, min(k + 128, width)) for k in range(0, width, 128)]


def _gather_small_weights(cvw_z, scw_z):
    pieces, writes = [], []
    for i, arr in enumerate((cvw_z, scw_z)):
        for l in range(DEPTH):
            for ln in _lanes(HALF):
                writes.append((i, (l, slice(None), ln), len(pieces)))
                pieces.append(_piece(i, (l, slice(None), ln), arr.shape[1], align=8))
    (cvw, scw), tok = _all_reduce_pieces([cvw_z, scw_z], pieces, [cvw_z.shape, scw_z.shape], writes, "ag_small")
    return cvw, scw, tok


SMALL_RAW = dict(norm_mix=(1, D_MODEL), norm_ffn=(1, D_MODEL), sg_ln_g=(1, HALF), sg_ln_b=(1, HALF), cv_b=(1, HALF), cv_ln_g=(1, HALF),
                 cv_ln_b=(1, HALF))


def _all_reduce_small_grads(raw, d_nfinal, loss):
    names = list(SMALL_RAW) + ["attn_sinks", "sg_b", "sc_w", "cv_w", "sg_w"]
    out_shape = dict(norm_mix=(DEPTH, D_MODEL), norm_ffn=(DEPTH, D_MODEL), sg_ln_g=(DEPTH, HALF), sg_ln_b=(DEPTH, HALF), cv_b=(DEPTH, HALF),
                     cv_ln_g=(DEPTH, HALF), cv_ln_b=(DEPTH, HALF), attn_sinks=(DEPTH, N_Q_HEADS), sg_b=(DEPTH, SG_GROUPS, SG_CHUNK),
                     sc_w=(DEPTH, SC_KERNEL, HALF), cv_w=(DEPTH, CV_KERNEL, HALF), sg_w=(DEPTH, SG_GROUPS, SG_CHUNK, SG_CHUNK))
    inputs, pieces, writes = [], [], []

    def add(src, idx, rows, out, out_idx, **kw):
        writes.append((names.index(out) if out in names else out, out_idx, len(pieces)))
        pieces.append(_piece(src, idx, rows, **kw))

    for l in range(DEPTH):
        row = slice(l, l + 1)
        for n, (_, width) in SMALL_RAW.items():
            inputs.append(raw[l][n])
            for ln in _lanes(width):
                add(len(inputs) - 1, (slice(0, 1), ln), 1, n, (row, ln))
        inputs.append(raw[l]["attn_sinks"])
        add(len(inputs) - 1, (slice(0, 1), slice(0, N_Q_HEADS)), 1, "attn_sinks", (row, slice(None)), width=N_Q_HEADS)
    for l in range(DEPTH):
        inputs.append(raw[l]["sg_b"])
        add(len(inputs) - 1, (slice(0, SG_GROUPS), slice(None)), SG_GROUPS, "sg_b", (l,), align=8, transposed=True)
        inputs.append(raw[l]["sc_w"])
        for ln in _lanes(HALF):
            add(len(inputs) - 1, (slice(0, SC_KERNEL), ln), SC_KERNEL, "sc_w", (l, slice(None), ln), align=8)
        inputs.append(raw[l]["cv_w"])
        for ln in _lanes(HALF):
            add(len(inputs) - 1, (slice(0, CV_KERNEL), ln), CV_KERNEL, "cv_w", (l, slice(None), ln), align=8)
        inputs.append(raw[l]["sg_w"])
        for g in range(SG_GROUPS):
            add(len(inputs) - 1, (g,), SG_CHUNK, "sg_w", (l, g), align=8)
    n_names = len(names)
    inputs.append(d_nfinal)
    for ln in _lanes(D_MODEL):
        add(len(inputs) - 1, (slice(0, 1), ln), 1, n_names, (slice(0, 1), ln))
    inputs.append(loss)
    add(len(inputs) - 1, (slice(0, 1), slice(None)), 1, n_names + 1, (slice(0, 1), slice(None)))
    outs, tok = _all_reduce_pieces(inputs, pieces, [out_shape[n] for n in names] + [(1, D_MODEL), (1, 128)], writes, "ar_small")
    return dict(zip(names, outs[:n_names])), outs[n_names], outs[n_names + 1], tok


def _small_views(raw):
    v = {n: raw[n][0] for n in SMALL_RAW}
    v.update(sg_w=raw["sg_w"], sg_b=raw["sg_b"][:, :SG_GROUPS].T, cv_w=raw["cv_w"][:CV_KERNEL],
             attn_sinks=raw["attn_sinks"][0, :N_Q_HEADS], sc_w=raw["sc_w"][:SC_KERNEL])
    return v


def _row_tile(rows, cols, n_arrays):
    budget = 20 * 1024 * 1024 // (n_arrays * 2 * cols * 4)
    tiles = [t for t in range(16, min(rows, budget) + 1, 16) if rows % t == 0]
    assert tiles, (rows, cols)
    return tiles[-1]


def _add_pairs(g, got, place, name):
    _, _, rows, cols = g.shape
    tr = _row_tile(rows, cols, 3)

    def body(place_ref, a_ref, b_ref, o_ref):
        del place_ref
        o_ref[...] = (a_ref[...].astype(F32) + b_ref[...].astype(F32)).astype(BF16)

    spec = pl.BlockSpec((None, tr, cols), lambda q, i, p: (q, i, 0))
    grid_spec = pltpu.PrefetchScalarGridSpec(
        num_scalar_prefetch=1, grid=(N_CHIPS, rows // tr),
        in_specs=[pl.BlockSpec((None, None, tr, cols), lambda q, i, p: (q, p[1], i, 0)), spec], out_specs=spec)
    return pl.pallas_call(body, name=name, grid_spec=grid_spec, out_shape=jax.ShapeDtypeStruct((N_CHIPS, rows, cols), BF16),
                          compiler_params=_params("parallel", "parallel"))(place, g, got)


def _sum_chips(own, recv, place, l, buf, name, after):
    _, rows, cols = own.shape
    tr = _row_tile(rows, cols, 4)

    def body(place_ref, own_ref, recv_ref, *rest):
        chip = place_ref[0]
        acc = own_ref[...].astype(F32)
        for j in range(1, N_CHIPS):
            acc = acc + recv_ref[lax.rem(chip + j, N_CHIPS)].astype(F32)
        rest[-1][...] = acc

    in_specs = [pl.BlockSpec((None, tr, cols), lambda i, p: (p[0], i, 0)), pl.BlockSpec((N_CHIPS, tr, cols), lambda i, p: (0, i, 0)), ANY]
    args = [place, own, recv, after]
    aliases = {}
    if buf is not None:
        in_specs.append(ANY)
        args.append(buf)
        aliases = {4: 0}
    grid_spec = pltpu.PrefetchScalarGridSpec(
        num_scalar_prefetch=1, grid=(rows // tr,), in_specs=in_specs,
        out_specs=pl.BlockSpec((None, None, tr, cols), lambda i, p: (l, p[1], i, 0)))
    return pl.pallas_call(body, name=name, grid_spec=grid_spec, out_shape=jax.ShapeDtypeStruct((DEPTH, 2, rows, cols), F32),
                          input_output_aliases=aliases, compiler_params=_params("parallel"))(*args)


def _adamw(w, g, m, v, name):
    shape = w.shape
    lead, (rows, cols) = shape[:-2], shape[-2:]
    tr = _row_tile(rows, cols, 7)

    def body(w_ref, g_ref, m_ref, v_ref, d_ref, mo_ref, vo_ref):
        gv = g_ref[...]
        mn = ADAM_B1 * m_ref[...] + (1.0 - ADAM_B1) * gv
        vn = ADAM_B2 * v_ref[...] + (1.0 - ADAM_B2) * (gv * gv)
        m_hat = mn / (1.0 - ADAM_B1 ** ADAM_STEP)
        v_hat = vn / (1.0 - ADAM_B2 ** ADAM_STEP)
        d_ref[...] = -ADAM_LR * (m_hat / (jnp.sqrt(v_hat) + ADAM_EPS) + ADAM_WD * w_ref[...])
        mo_ref[...] = mn
        vo_ref[...] = vn

    spec = pl.BlockSpec((None,) * len(lead) + (tr, cols), lambda *idx: (*idx, 0))
    grid = lead + (rows // tr,)
    return list(pl.pallas_call(body, name=name, grid=grid, in_specs=[spec] * 4, out_specs=[spec] * 3,
                               out_shape=[jax.ShapeDtypeStruct(shape, F32)] * 3,
                               compiler_params=_params(*(["parallel"] * len(grid))))(w, g, m, v))


def _adamw_small(ws, gs, ms, vs, name):
    n = len(ws)

    def body(*refs):
        for i in range(n):
            gv = refs[n + i][...]
            mn = ADAM_B1 * refs[2 * n + i][...] + (1.0 - ADAM_B1) * gv
            vn = ADAM_B2 * refs[3 * n + i][...] + (1.0 - ADAM_B2) * (gv * gv)
            m_hat = mn / (1.0 - ADAM_B1 ** ADAM_STEP)
            v_hat = vn / (1.0 - ADAM_B2 ** ADAM_STEP)
            refs[4 * n + i][...] = -ADAM_LR * (m_hat / (jnp.sqrt(v_hat) + ADAM_EPS) + ADAM_WD * refs[i][...])
            refs[5 * n + i][...] = mn
            refs[6 * n + i][...] = vn

    vm = pl.BlockSpec(memory_space=pltpu.VMEM)
    outs = pl.pallas_call(body, name=name, out_shape=[jax.ShapeDtypeStruct(t.shape, F32) for t in ws] * 3,
                          in_specs=[vm] * (4 * n), out_specs=[vm] * (3 * n),
                          compiler_params=pltpu.CompilerParams(vmem_limit_bytes=VMEM_LIMIT))(*ws, *gs, *ms, *vs)
    return outs[:n], outs[n:2 * n], outs[2 * n:]


SMALL = ("norm_mix", "sg_ln_g", "sg_ln_b", "sg_w", "sg_b", "cv_w", "cv_b", "cv_ln_g", "cv_ln_b", "attn_sinks", "sc_w", "norm_ffn", "norm_final")
ORDER = ("norm_mix", "w_in", "sg_ln_g", "sg_ln_b", "sg_w", "sg_b", "cv_w", "cv_b", "cv_ln_g", "cv_ln_b", "attn_sinks", "sc_w",
         "w_branch", "w_out", "norm_ffn", "w_gate_up", "w_down", "norm_final")


def kernel(x, norm_mix, w_in, sg_ln_g, sg_ln_b, sg_w, sg_b, cv_w, cv_b, cv_ln_g, cv_ln_b, attn_sinks, sc_w, w_branch, w_out, norm_ffn, w_gate_up, w_down, norm_final, loss_target, m_norm_mix, m_w_in, m_sg_ln_g, m_sg_ln_b, m_sg_w, m_sg_b, m_cv_w, m_cv_b, m_cv_ln_g, m_cv_ln_b, m_attn_sinks, m_sc_w, m_w_branch, m_w_out, m_norm_ffn, m_w_gate_up, m_w_down, m_norm_final, v_norm_mix, v_w_in, v_sg_ln_g, v_sg_ln_b, v_sg_w, v_sg_b, v_cv_w, v_cv_b, v_cv_ln_g, v_cv_ln_b, v_attn_sinks, v_sc_w, v_w_branch, v_w_out, v_norm_ffn, v_w_gate_up, v_w_down, v_norm_final):
    W = dict(norm_mix=norm_mix, w_in=w_in, sg_ln_g=sg_ln_g, sg_ln_b=sg_ln_b, sg_w=sg_w, sg_b=sg_b, cv_w=cv_w, cv_b=cv_b, cv_ln_g=cv_ln_g,
             cv_ln_b=cv_ln_b, attn_sinks=attn_sinks, sc_w=sc_w, w_branch=w_branch, w_out=w_out, norm_ffn=norm_ffn, w_gate_up=w_gate_up,
             w_down=w_down, norm_final=norm_final)
    M = dict(norm_mix=m_norm_mix, w_in=m_w_in, sg_ln_g=m_sg_ln_g, sg_ln_b=m_sg_ln_b, sg_w=m_sg_w, sg_b=m_sg_b, cv_w=m_cv_w, cv_b=m_cv_b,
             cv_ln_g=m_cv_ln_g, cv_ln_b=m_cv_ln_b, attn_sinks=m_attn_sinks, sc_w=m_sc_w, w_branch=m_w_branch, w_out=m_w_out,
             norm_ffn=m_norm_ffn, w_gate_up=m_w_gate_up, w_down=m_w_down, norm_final=m_norm_final)
    V = dict(norm_mix=v_norm_mix, w_in=v_w_in, sg_ln_g=v_sg_ln_g, sg_ln_b=v_sg_ln_b, sg_w=v_sg_w, sg_b=v_sg_b, cv_w=v_cv_w, cv_b=v_cv_b,
             cv_ln_g=v_cv_ln_g, cv_ln_b=v_cv_ln_b, attn_sinks=v_attn_sinks, sc_w=v_sc_w, w_branch=v_w_branch, w_out=v_w_out,
             norm_ffn=v_norm_ffn, w_gate_up=v_w_gate_up, w_down=v_w_down, norm_final=v_norm_final)
    mx, my, mc = lax.axis_index("x"), lax.axis_index("y"), lax.axis_index("c")
    chip = 2 * mx + my

    place = jnp.stack([chip, mc]).astype(jnp.int32)
    tables = _rope_tables(x.shape[1])
    land_shapes = [(N_CHIPS,) + HALF_SHAPE[n] for n in BIG]
    part_shapes = {n: (N_CHIPS,) + HALF_SHAPE[n][1:] for n in BIG}

    T_ = lambda t: jnp.swapaxes(t, 1, 2)
    Wt, Mt, Vt = ({**t, "w_in": T_(t["w_in"])} for t in (W, M, V))

    def shards_of(l, tok):
        return [(Wt[n][l] + tok[0, 0]).astype(BF16).reshape(HALF_SHAPE[n]) for n in BIG]

    def finish_gather(tag, handle, after):
        srcs, lands = _ici_wait(handle, after, f"ag_wait{tag}")
        return _ag_pair(srcs, lands, f"ag_pair{tag}")[0]

    def mix_weights(l, g_in):
        return dict(w_in=_w_in_layout(g_in[0].reshape(N_CHIPS, W_IN_SHARD, D_MODEL)), norm_mix=norm_mix[l][None], norm_ffn=norm_ffn[l][None],
                    mixer=_mixer_params(l, sg_ln_g, sg_ln_b, sg_w, sg_b, cvw_full, cv_b, cv_ln_g, cv_ln_b, attn_sinks, scw_full))

    def rest_weights(lw, g_rest):
        G = dict(zip(BIG[1:], g_rest))
        lw.update(w_branch=G["w_branch"].reshape(N_CHIPS, N_BRANCH, HALF, 256), w_out=G["w_out"].reshape(D_MODEL, D_MODEL),
                  w_gate_up=G["w_gate_up"].reshape(N_CHIPS, D_MODEL, GU_SHARD), w_down=G["w_down"].reshape(D_FF, D_MODEL))

    def shard_major(g):
        t = dict(g)
        if "w_in" in t:
            t["w_in"] = _w_in_unlayout(t["w_in"])
        return {n: t[n].reshape((N_CHIPS,) + HALF_SHAPE[n]) for n in BIG if n in t}

    def pair_sums(tag, g):
        names = list(g)
        got = _rs_pair([g[n] for n in names], f"rs_pair{tag}")
        return names, [_add_pairs(g[n], got[k], place, f"rs_add{tag}_{n}") for k, n in enumerate(names)]

    zero_tok = jnp.zeros((8, 128), F32)
    south = (mc == 0).astype(F32)
    cvw_z = lax.dynamic_update_slice(jnp.zeros((DEPTH, CV_KERNEL, HALF), F32), cv_w * south, (0, 0, chip * 128))
    scw_z = lax.dynamic_update_slice(jnp.zeros((DEPTH, SC_KERNEL, HALF), F32), sc_w * south, (0, 0, chip * 128))
    cvw_full, scw_full, tok = _gather_small_weights(cvw_z, scw_z)

    handles = []
    for l in range(DEPTH):
        for tag, sl in (("in", slice(0, 1)), ("rest", slice(1, NB))):
            h, tok = _ici_start("gather", shards_of(l, tok)[sl], land_shapes[sl], f"ag_start{l}{tag}")
            handles.append(h)
    x_l, saved = x[0], []
    for l in range(DEPTH):
        lw = mix_weights(l, finish_gather(f"{l}in", handles[2 * l], x_l if l else tok))
        mixed = _fwd_layer_mix(l, x_l, lw, tables)
        rest_weights(lw, finish_gather(f"{l}rest", handles[2 * l + 1], mixed[2]))
        x_l, sv = _fwd_layer_rest(l, x_l, mixed, lw)
        saved.append((lw, sv))
    (lw0, sv0), (lw1, sv1) = saved
    dx, d_nfinal, loss = _final_loss(x_l, norm_final[None], loss_target[0], 256, "final_loss")

    lw1["after"] = zero_tok
    carry, g_ffn1 = _bwd_layer_ffn(1, dx, lw1, sv1)
    dx, g_mix1 = _bwd_layer_mix(1, carry, lw1, sv1, tables)
    names1, part1 = pair_sums("1", shard_major({**g_ffn1, **g_mix1}))
    hr1, tok = _ici_start("scatter", part1, [part_shapes[n] for n in names1], "rs_start1")

    lw0["after"] = tok
    carry, g_ffn0 = _bwd_layer_ffn(0, dx, lw0, sv0)
    names_a, part_a = pair_sums("0a", shard_major(g_ffn0))
    _, recv1 = _ici_wait(hr1, part_a[0], "rs_wait1")
    hra, tok = _ici_start("scatter", part_a, [part_shapes[n] for n in names_a], "rs_start0a")

    lw0["mixer"] = [lw0["mixer"][0] + tok[0, 0]] + lw0["mixer"][1:]
    dx, g_mix0 = _bwd_layer_mix(0, carry, lw0, sv0, tables)
    _, recv_a = _ici_wait(hra, dx, "rs_wait0a")

    small_red, nf_red, loss_red, tok = _all_reduce_small_grads([{**g_ffn0, **g_mix0}, {**g_ffn1, **g_mix1}], d_nfinal, loss)
    small_red["norm_final"] = nf_red
    loss_out = loss_red[0, 0]
    for n in ("cv_w", "sc_w"):
        small_red[n] = lax.dynamic_slice_in_dim(small_red[n], chip * 128, 128, axis=2)

    g_mix0["w_in"] = g_mix0["w_in"] + tok[0, 0].astype(BF16)
    names_b, part_b = pair_sums("0b", shard_major(g_mix0))
    hrb, tok = _ici_start("scatter", part_b, [part_shapes[n] for n in names_b], "rs_start0b")
    bufs = {n: _sum_chips(part1[k], recv1[k], place, 1, None, f"rs_sum1_{n}", tok) for k, n in enumerate(names1)}
    for k, n in enumerate(names_a):
        bufs[n] = _sum_chips(part_a[k], recv_a[k], place, 0, bufs[n], f"rs_sum0_{n}", tok)
    shared = dict(zip(names_a, _rs_share([bufs[n] for n in names_a], "rs_share_a")))
    upd = {}
    for n in names_a:
        red = shared[n].reshape(W[n].shape)
        upd[n] = [red] + _adamw(W[n], red, M[n], V[n], f"adamw_{n}")
    two_d = lambda t: t[None] if t.ndim == 1 else t
    small_upd = _adamw_small(*([two_d(t[n]) for n in SMALL] for t in (W, small_red, M, V)), "adamw_small")
    for n, d, mo, vo in zip(SMALL, *small_upd):
        upd[n] = [t.reshape(W[n].shape) for t in (small_red[n], d, mo, vo)]

    _, recv_b = _ici_wait(hrb, upd[names_a[-1]][1], "rs_wait0b")
    for k, n in enumerate(names_b):
        bufs[n] = _sum_chips(part_b[k], recv_b[k], place, 0, bufs[n], f"rs_sum0_{n}", tok)
    shared = dict(zip(names_b, _rs_share([bufs[n] for n in names_b], "rs_share_b")))
    for n in names_b:
        red = shared[n].reshape(Wt[n].shape)
        upd[n] = [T_(t) for t in [red] + _adamw(Wt[n], red, Mt[n], Vt[n], f"adamw_{n}")]

    out = [loss_out, dx[None]]
    for k in range(4):
        out += [upd[n][k] for n in ORDER]
    return tuple(out)
```

```python
import functools
import math

import jax
import jax.numpy as jnp
from jax import lax
from jax.experimental import pallas as pl
from jax.experimental.pallas import tpu as pltpu

F32 = jnp.float32
BF16 = jnp.bfloat16

D_MODEL = 1024
DEPTH = 2
HALF = 512
SG_CHUNK = 128
SG_GROUPS = 4
CV_KERNEL = 31
HEAD_DIM = 64
N_Q_HEADS = 8
N_KV_HEADS = 2
Q_PER_KV = N_Q_HEADS // N_KV_HEADS
WINDOW = 128
ROPE_THETA = 10000.0
SC_KERNEL = 3
N_BRANCH = 4
D_FF = 2816
EPS = 1e-6
N_CHIPS = 4
N_DEV = 8

MIX_W = 4352
GATE_W = N_BRANCH * D_MODEL
PROJ_PAD = 2 * MIX_W
W_IN_SHARD = 2112
GU_SHARD = 1408
HALO = 128
CV_PAD = 32

ADAM_LR = 0.001
ADAM_B1 = 0.9
ADAM_B2 = 0.999
ADAM_EPS = 1e-08
ADAM_WD = 0.01
ADAM_STEP = 10

VMEM_LIMIT = 56 * 1024 * 1024
INV_SQRT2 = 1.0 / math.sqrt(2.0)
INV_SQRT_2PI = 1.0 / math.sqrt(2.0 * math.pi)
NEG_BIG = -1e30
MESH = pl.DeviceIdType.MESH

C_ZA, C_ZB, C_Q, C_K, C_V, C_ZD = 0, 1024, 2048, 2560, 2688, 2816


def _params(*sem):
    return pltpu.CompilerParams(dimension_semantics=sem, vmem_limit_bytes=VMEM_LIMIT)


def _sig(v):
    return 1.0 / (1.0 + jnp.exp(-v))


def _dot(a, b):
    return jnp.dot(a, b, preferred_element_type=F32)


def _dot_nt(a, b):
    return lax.dot_general(a, b, (((1,), (1,)), ((), ())), preferred_element_type=F32)


def _dot_tn(a, b):
    return lax.dot_general(a, b, (((0,), (0,)), ((), ())), preferred_element_type=F32)


def _full(shape):
    nd = len(shape)
    return pl.BlockSpec(shape, lambda *_: (0,) * nd)


def _rms_mm(x, g, w, tm, tn, name):
    T = x.shape[0]
    transposed = w.ndim == 2
    if transposed:
        N = w.shape[0]
        wspec = pl.BlockSpec((tn, D_MODEL), lambda i, j: (j, 0))
    else:
        tn = w.shape[2]
        N = w.shape[0] * tn
        wspec = pl.BlockSpec((None, D_MODEL, tn), lambda i, j: (j, 0, 0))

    def body(x_ref, g_ref, w_ref, o_ref, xn_ref):
        @pl.when(pl.program_id(1) == 0)
        def _():
            xv = x_ref[...]
            r = lax.rsqrt(jnp.mean(xv * xv, axis=-1, keepdims=True) + EPS)
            xn_ref[...] = (xv * r * g_ref[...]).astype(BF16)

        o_ref[...] = (_dot_nt if transposed else _dot)(xn_ref[...], w_ref[...]).astype(BF16)

    return pl.pallas_call(
        body, name=name, grid=(T // tm, N // tn),
        in_specs=[pl.BlockSpec((tm, D_MODEL), lambda i, j: (i, 0)), _full((1, D_MODEL)), wspec],
        out_specs=[pl.BlockSpec((tm, tn), lambda i, j: (i, j)), pl.BlockSpec((tm, D_MODEL), lambda i, j: (i, 0))],
        out_shape=[jax.ShapeDtypeStruct((T, N), BF16), jax.ShapeDtypeStruct((T, D_MODEL), BF16)],
        compiler_params=_params("parallel", "arbitrary"),
    )(x, g, w)


def _merge_fwd(x, ys, proj, wb, wo, tm, name):
    T = x.shape[0]

    def body(x_ref, ys_ref, zg_ref, wb_ref, wo_ref, xo_ref, mg_ref):
        merged = None
        for n in range(N_BRANCH):
            yn = ys_ref[:, n * HALF:(n + 1) * HALF]
            br = jnp.concatenate([_dot(yn, wb_ref[s, n]) for s in range(N_CHIPS)], axis=1)
            t = _sig(zg_ref[:, n * D_MODEL:(n + 1) * D_MODEL].astype(F32)) * br
            merged = t if merged is None else merged + t
        mb = merged.astype(BF16)
        mg_ref[...] = mb
        xo_ref[...] = x_ref[...] + _dot(mb, wo_ref[...])

    return pl.pallas_call(
        body, name=name, grid=(T // tm,),
        in_specs=[pl.BlockSpec((tm, D_MODEL), lambda i: (i, 0)), pl.BlockSpec((tm, N_BRANCH * HALF), lambda i: (i, 0)),
                  pl.BlockSpec((tm, GATE_W), lambda i: (i, 0)), _full(wb.shape), _full(wo.shape)],
        out_specs=[pl.BlockSpec((tm, D_MODEL), lambda i: (i, 0)), pl.BlockSpec((tm, D_MODEL), lambda i: (i, 0))],
        out_shape=[jax.ShapeDtypeStruct((T, D_MODEL), F32), jax.ShapeDtypeStruct((T, D_MODEL), BF16)],
        compiler_params=_params("parallel"),
    )(x, ys, proj, wb, wo)


def _ffn_down(xm, gu, wd, tm, name):
    T = xm.shape[0]

    def body(x_ref, gu_ref, wd_ref, o_ref):
        g = gu_ref[:, :D_FF].astype(F32)
        u = gu_ref[:, D_FF:].astype(F32)
        act = (g * _sig(g) * u).astype(BF16)
        o_ref[...] = x_ref[...] + _dot(act, wd_ref[...])

    return pl.pallas_call(
        body, name=name, grid=(T // tm,),
        in_specs=[pl.BlockSpec((tm, D_MODEL), lambda i: (i, 0)), pl.BlockSpec((tm, 2 * D_FF), lambda i: (i, 0)), _full(wd.shape)],
        out_specs=pl.BlockSpec((tm, D_MODEL), lambda i: (i, 0)),
        out_shape=jax.ShapeDtypeStruct((T, D_MODEL), F32),
        compiler_params=_params("parallel"),
    )(xm, gu, wd)


def _final_loss(x, g, tgt, tm, name):
    T = x.shape[0]

    def body(x_ref, g_ref, t_ref, dx_ref, dg_ref, ls_ref):
        @pl.when(pl.program_id(0) == 0)
        def _():
            dg_ref[...] = jnp.zeros_like(dg_ref)
            ls_ref[...] = jnp.zeros_like(ls_ref)

        xv = x_ref[...]
        gv = g_ref[...]
        r = lax.rsqrt(jnp.mean(xv * xv, axis=-1, keepdims=True) + EPS)
        xh = xv * r
        diff = xh * gv - t_ref[...]
        ls_ref[...] += jnp.full(ls_ref.shape, 0.5 / D_MODEL, F32) * jnp.sum(diff * diff)
        dy = diff * (1.0 / D_MODEL)
        dxh = dy * gv
        dx_ref[...] = r * (dxh - xh * jnp.mean(dxh * xh, axis=-1, keepdims=True))
        dg_ref[...] += jnp.sum(dy * xh, axis=0, keepdims=True)

    return pl.pallas_call(
        body, name=name, grid=(T // tm,),
        in_specs=[pl.BlockSpec((tm, D_MODEL), lambda i: (i, 0)), _full((1, D_MODEL)), pl.BlockSpec((tm, D_MODEL), lambda i: (i, 0))],
        out_specs=[pl.BlockSpec((tm, D_MODEL), lambda i: (i, 0)), _full((1, D_MODEL)), _full((1, 128))],
        out_shape=[jax.ShapeDtypeStruct((T, D_MODEL), F32), jax.ShapeDtypeStruct((1, D_MODEL), F32), jax.ShapeDtypeStruct((1, 128), F32)],
        compiler_params=_params("arbitrary"),
    )(x, g, tgt)


def _swiglu_bwd(dx, gu, wd, tm, name, after):
    T = dx.shape[0]

    def body(dx_ref, gu_ref, wd_ref, after_ref, dgu_ref, act_ref):
        del after_ref
        dact = _dot_nt(dx_ref[...].astype(BF16), wd_ref[...])
        g = gu_ref[:, :D_FF].astype(F32)
        u = gu_ref[:, D_FF:].astype(F32)
        s = _sig(g)
        silu = g * s
        act_ref[...] = (silu * u).astype(BF16)
        dgu_ref[:, :D_FF] = (dact * u * (s + silu * (1.0 - s))).astype(BF16)
        dgu_ref[:, D_FF:] = (dact * silu).astype(BF16)

    return pl.pallas_call(
        body, name=name, grid=(T // tm,),
        in_specs=[pl.BlockSpec((tm, D_MODEL), lambda i: (i, 0)), pl.BlockSpec((tm, 2 * D_FF), lambda i: (i, 0)), _full(wd.shape),
                  pl.BlockSpec(memory_space=pl.ANY)],
        out_specs=[pl.BlockSpec((tm, 2 * D_FF), lambda i: (i, 0)), pl.BlockSpec((tm, D_FF), lambda i: (i, 0))],
        out_shape=[jax.ShapeDtypeStruct((T, 2 * D_FF), BF16), jax.ShapeDtypeStruct((T, D_FF), BF16)],
        compiler_params=_params("parallel"),
    )(dx, gu, wd, after)


def _mm_tn(a, b, grid, a_block, a_map, b_block, b_map, o_shape, o_block, o_map, name, col_split=1):
    gk = grid[2]
    tm = [d for d in a_block if d is not None][-1]
    tn = [d for d in b_block if d is not None][-1]

    def body(a_ref, b_ref, o_ref, acc_ref):
        k = pl.program_id(2)
        p = _dot_tn(a_ref[...].astype(BF16), b_ref[...].astype(BF16))

        @pl.when(k == 0)
        def _():
            acc_ref[...] = p

        @pl.when(k > 0)
        def _():
            acc_ref[...] += p

        @pl.when(k == gk - 1)
        def _():
            if col_split == 1:
                o_ref[...] = acc_ref[...].astype(o_ref.dtype)
            else:
                w = tn // col_split
                for s in range(col_split):
                    o_ref[s] = acc_ref[:, s * w:(s + 1) * w].astype(o_ref.dtype)

    return pl.pallas_call(
        body, name=name, grid=grid,
        in_specs=[pl.BlockSpec(a_block, a_map), pl.BlockSpec(b_block, b_map)],
        out_specs=pl.BlockSpec(o_block, o_map),
        out_shape=jax.ShapeDtypeStruct(o_shape, BF16),
        scratch_shapes=[pltpu.VMEM((tm, tn), F32)],
        compiler_params=_params("parallel", "parallel", "arbitrary"),
    )(a, b)


def _mm_nt_rmsbwd(a, w, x, g, dres, tm, tk, name):
    T = x.shape[0]
    transposed = w.ndim == 2
    if transposed:
        gk = w.shape[0] // tk
        wspec = pl.BlockSpec((tk, D_MODEL), lambda i, k: (k, 0))
    else:
        tk = w.shape[2]
        gk = w.shape[0]
        wspec = pl.BlockSpec((None, D_MODEL, tk), lambda i, k: (k, 0, 0))

    def body(a_ref, w_ref, x_ref, g_ref, r_ref, dx_ref, dg_ref, acc_ref):
        i, k = pl.program_id(0), pl.program_id(1)
        p = (_dot if transposed else _dot_nt)(a_ref[...], w_ref[...])

        @pl.when(k == 0)
        def _():
            acc_ref[...] = p

        @pl.when(k > 0)
        def _():
            acc_ref[...] += p

        @pl.when(jnp.logical_and(i == 0, k == 0))
        def _():
            dg_ref[...] = jnp.zeros_like(dg_ref)

        @pl.when(k == gk - 1)
        def _():
            dh = acc_ref[...]
            xv = x_ref[...]
            r = lax.rsqrt(jnp.mean(xv * xv, axis=-1, keepdims=True) + EPS)
            xh = xv * r
            dxh = dh * g_ref[...]
            dx_ref[...] = r_ref[...] + r * (dxh - xh * jnp.mean(dxh * xh, axis=-1, keepdims=True))
            dg_ref[...] += jnp.sum(dh * xh, axis=0, keepdims=True)

    return pl.pallas_call(
        body, name=name, grid=(T // tm, gk),
        in_specs=[pl.BlockSpec((tm, tk), lambda i, k: (i, k)), wspec, pl.BlockSpec((tm, D_MODEL), lambda i, k: (i, 0)),
                  _full((1, D_MODEL)), pl.BlockSpec((tm, D_MODEL), lambda i, k: (i, 0))],
        out_specs=[pl.BlockSpec((tm, D_MODEL), lambda i, k: (i, 0)), _full((1, D_MODEL))],
        out_shape=[jax.ShapeDtypeStruct((T, D_MODEL), F32), jax.ShapeDtypeStruct((1, D_MODEL), F32)],
        scratch_shapes=[pltpu.VMEM((tm, D_MODEL), F32)],
        compiler_params=_params("arbitrary", "arbitrary"),
    )(a, w, x, g, dres)


def _merge_bwd(dxm, ys, proj, wb, wo, tm, name):
    T = dxm.shape[0]

    def body(dx_ref, ys_ref, zg_ref, wb_ref, wo_ref, dys_ref, dbr_ref, dp_ref):
        dmerged = _dot_nt(dx_ref[...].astype(BF16), wo_ref[...])
        for n in range(N_BRANCH):
            yn = ys_ref[:, n * HALF:(n + 1) * HALF]
            br = jnp.concatenate([_dot(yn, wb_ref[s, n]) for s in range(N_CHIPS)], axis=1)
            gt = _sig(zg_ref[:, n * D_MODEL:(n + 1) * D_MODEL].astype(F32))
            dbr = (gt * dmerged).astype(BF16)
            dbr_ref[:, n * D_MODEL:(n + 1) * D_MODEL] = dbr
            dp_ref[:, n * D_MODEL:(n + 1) * D_MODEL] = (dmerged * br * gt * (1.0 - gt)).astype(BF16)
            dy = None
            for s in range(N_CHIPS):
                t = _dot_nt(dbr[:, s * 256:(s + 1) * 256], wb_ref[s, n])
                dy = t if dy is None else dy + t
            dys_ref[:, n * HALF:(n + 1) * HALF] = dy.astype(BF16)
        dp_ref[:, GATE_W:] = jnp.zeros((tm, MIX_W - GATE_W), BF16)

    return pl.pallas_call(
        body, name=name, grid=(T // tm,),
        in_specs=[pl.BlockSpec((tm, D_MODEL), lambda i: (i, 0)), pl.BlockSpec((tm, N_BRANCH * HALF), lambda i: (i, 0)),
                  pl.BlockSpec((tm, GATE_W), lambda i: (i, 0)), _full(wb.shape), _full(wo.shape)],
        out_specs=[pl.BlockSpec((tm, N_BRANCH * HALF), lambda i: (i, 0)), pl.BlockSpec((tm, GATE_W), lambda i: (i, 0)),
                   pl.BlockSpec((tm, MIX_W), lambda i: (i, 0))],
        out_shape=[jax.ShapeDtypeStruct((T, N_BRANCH * HALF), BF16), jax.ShapeDtypeStruct((T, GATE_W), BF16),
                   jax.ShapeDtypeStruct((T, PROJ_PAD), BF16)],
        compiler_params=_params("parallel"),
    )(dxm, ys, proj, wb, wo)


def _gelu(v):
    return 0.5 * v * (1.0 + lax.erf(v * INV_SQRT2))


def _gelu_grad(v):
    return 0.5 * (1.0 + lax.erf(v * INV_SQRT2)) + v * jnp.exp(-0.5 * v * v) * INV_SQRT_2PI


def _rot_half(t):
    w = t.shape[1]
    lane = lax.broadcasted_iota(jnp.int32, t.shape, 1)
    return jnp.where((lane % HEAD_DIM) < HEAD_DIM // 2, pltpu.roll(t, w - HEAD_DIM // 2, 1), pltpu.roll(t, HEAD_DIM // 2, 1))


def _rope(t, cos, sin_signed):
    return t * cos + _rot_half(t) * sin_signed


def _rope_t(d, cos, sin_signed):
    return d * cos + _rot_half(d * sin_signed)


def _ln_fwd(v, g, b):
    mu = jnp.mean(v, axis=-1, keepdims=True)
    vc = v - mu
    r = lax.rsqrt(jnp.mean(vc * vc, axis=-1, keepdims=True) + EPS)
    vh = vc * r
    return vh * g + b, vh, r


def _ln_bwd(dn, vh, r, g):
    dvh = dn * g
    return r * (dvh - jnp.mean(dvh, axis=-1, keepdims=True) - vh * jnp.mean(dvh * vh, axis=-1, keepdims=True))


def _sublane_shifts(sh_ref, rows):
    for b in range(1, 8):
        sh_ref[b, 0:rows - 8, :] = sh_ref[0, pl.ds(b, rows - 8), :]


def _tap(sh_ref, off, n):
    return sh_ref[off % 8, pl.ds(off - off % 8, n), :]


def _tril_mask():
    return lax.broadcasted_iota(jnp.int32, (SG_CHUNK, SG_CHUNK), 0) >= lax.broadcasted_iota(jnp.int32, (SG_CHUNK, SG_CHUNK), 1)


def _attn_probs(qs, kh, sink_col, first_ok):
    s = _dot_nt(qs, kh) * (HEAD_DIM ** -0.5)
    row = lax.broadcasted_iota(jnp.int32, s.shape, 0) % WINDOW
    col = lax.broadcasted_iota(jnp.int32, s.shape, 1)
    valid = (col > row) & (col <= row + WINDOW) & ((col >= WINDOW) | first_ok)
    s = jnp.where(valid, s, NEG_BIG)
    m = jnp.maximum(jnp.max(s, axis=-1, keepdims=True), sink_col)
    p = jnp.where(valid, jnp.exp(s - m), 0.0)
    es = jnp.exp(sink_col - m)
    inv = 1.0 / (jnp.sum(p, axis=-1, keepdims=True) + es)
    return p * inv, es * inv


def _sink_col(sinks_ref, h):
    return jnp.concatenate([jnp.broadcast_to(sinks_ref[:, h * Q_PER_KV + g:h * Q_PER_KV + g + 1], (WINDOW, 1))
                            for g in range(Q_PER_KV)], axis=0)


def _mixer_in_specs(TB, nb):
    r = TB // HALO
    last = nb * r - 1
    cur = pl.BlockSpec((TB, MIX_W), lambda i: (i, 1))
    prev = pl.BlockSpec((HALO, MIX_W), lambda i: (jnp.maximum(i * r - 1, 0), 1))
    nxt = pl.BlockSpec((HALO, MIX_W), lambda i: (jnp.minimum((i + 1) * r, last), 1))
    tcur = pl.BlockSpec((TB, 128), lambda i: (i, 0))
    tprev = pl.BlockSpec((HALO, 128), lambda i: (jnp.maximum(i * r - 1, 0), 0))
    tnxt = pl.BlockSpec((HALO, 128), lambda i: (jnp.minimum((i + 1) * r, last), 0))
    return cur, prev, nxt, tcur, tprev, tnxt


def _mixer_param_specs():
    return [_full((1, HALF)), _full((1, HALF)), _full((SG_GROUPS, SG_CHUNK, SG_CHUNK)), _full((SG_CHUNK, 128)),
            _full((32, HALF)), _full((1, HALF)), _full((1, HALF)), _full((1, HALF)), _full((1, 128)), _full((8, HALF))]


def _mixers_fwd(proj, cos_t, sin_t, mp, TB, name):
    T = proj.shape[0]
    nb = T // TB
    r = TB // HALO
    cur, prev, _, tcur, tprev, _ = _mixer_in_specs(TB, nb)

    def body(zc_ref, zp_ref, cc_ref, sc_ref, cp_ref, sp_ref,
             lg_ref, lb_ref, sgw_ref, sgb_ref, cvw_ref, cvb_ref, cvg_ref, cvbb_ref, sinks_ref, scw_ref,
             ys_ref, scr_ref, k_ref, v_ref, sh_ref):
        i = pl.program_id(0)
        pm = (i > 0).astype(F32)

        def colsE(c0, c1):
            return jnp.concatenate([zp_ref[:, c0:c1].astype(F32) * pm, zc_ref[:, c0:c1].astype(F32)], axis=0)

        a = _gelu(zc_ref[:, C_ZA:C_ZA + 2 * HALF].astype(F32))
        u = a[:, :HALF]
        vn, _, _ = _ln_fwd(a[:, HALF:], lg_ref[...], lb_ref[...])
        vnb = vn.astype(BF16)
        tril = _tril_mask()
        for g in range(SG_GROUPS):
            wt = jnp.where(tril, sgw_ref[g], 0.0).astype(BF16)
            for ci in range(r):
                rows = slice(ci * SG_CHUNK, (ci + 1) * SG_CHUNK)
                cols = slice(g * 128, (g + 1) * 128)
                mixed = _dot(wt, vnb[rows, cols]) + sgb_ref[:, g:g + 1]
                ys_ref[rows, g * 128:(g + 1) * 128] = (u[rows, cols] * mixed).astype(BF16)

        def colsB(c0, c1):
            return jnp.concatenate([zp_ref[HALO - CV_PAD:, c0:c1].astype(F32) * pm, zc_ref[:, c0:c1].astype(F32)], axis=0)

        sh_ref[0] = colsB(C_ZB, C_ZB + HALF) * _sig(colsB(C_ZB + HALF, C_ZB + 2 * HALF))
        _sublane_shifts(sh_ref, TB + CV_PAD)
        c = jnp.broadcast_to(cvb_ref[...], (TB, HALF))
        for k in range(CV_KERNEL):
            c = c + cvw_ref[k:k + 1, :] * _tap(sh_ref, CV_PAD - (CV_KERNEL - 1) + k, TB)
        n, _, _ = _ln_fwd(c, cvg_ref[...], cvbb_ref[...])
        ys_ref[:, HALF:2 * HALF] = (n * _sig(n)).astype(BF16)

        zd = colsE(C_ZD + HALF, C_ZD + 3 * HALF)
        scr_ref[...] = zd[:, :HALF] * zd[:, HALF:]
        cv = None
        for k in range(SC_KERNEL):
            t = scw_ref[k:k + 1, :] * scr_ref[pl.ds(HALO - (SC_KERNEL - 1) + k, TB), :]
            cv = t if cv is None else cv + t
        ys_ref[:, 3 * HALF:4 * HALF] = (zc_ref[:, C_ZD:C_ZD + HALF].astype(F32) * cv).astype(BF16)

        cosE = jnp.concatenate([cp_ref[...], cc_ref[...]], axis=0)
        sinE = jnp.concatenate([sp_ref[...], sc_ref[...]], axis=0)
        k_ref[...] = _rope(colsE(C_K, C_K + 128), cosE, sinE).astype(BF16)
        v_ref[...] = colsE(C_V, C_V + 128).astype(BF16)
        cosC, sinC = cc_ref[...], sc_ref[...]
        q = jnp.concatenate([_rope(zc_ref[:, C_Q + 128 * j:C_Q + 128 * (j + 1)].astype(F32), cosC, sinC)
                             for j in range(4)], axis=1).astype(BF16)
        for qb in range(r):
            first_ok = (i * r + qb) > 0
            for h in range(N_KV_HEADS):
                hc = slice(h * HEAD_DIM, (h + 1) * HEAD_DIM)
                kh = k_ref[qb * WINDOW:qb * WINDOW + 2 * WINDOW, hc]
                vh = v_ref[qb * WINDOW:qb * WINDOW + 2 * WINDOW, hc]
                qs = jnp.concatenate([q[qb * WINDOW:(qb + 1) * WINDOW, (h * Q_PER_KV + g) * HEAD_DIM:(h * Q_PER_KV + g + 1) * HEAD_DIM]
                                      for g in range(Q_PER_KV)], axis=0)
                probs, _ = _attn_probs(qs, kh, _sink_col(sinks_ref, h), first_ok)
                o = _dot(probs.astype(BF16), vh)
                for g in range(Q_PER_KV):
                    c0 = 2 * HALF + (h * Q_PER_KV + g) * HEAD_DIM
                    ys_ref[qb * WINDOW:(qb + 1) * WINDOW, c0:c0 + HEAD_DIM] = o[g * WINDOW:(g + 1) * WINDOW].astype(BF16)

    return pl.pallas_call(
        body, name=name, grid=(nb,),
        in_specs=[cur, prev, tcur, tcur, tprev, tprev] + _mixer_param_specs(),
        out_specs=pl.BlockSpec((TB, 4 * HALF), lambda i: (i, 0)),
        out_shape=jax.ShapeDtypeStruct((T, 4 * HALF), BF16),
        scratch_shapes=[pltpu.VMEM((TB + HALO, HALF), F32), pltpu.VMEM((TB + HALO, 128), BF16), pltpu.VMEM((TB + HALO, 128), BF16),
                        pltpu.VMEM((8, TB + CV_PAD, HALF), F32)],
        compiler_params=_params("parallel"),
    )(proj, proj, cos_t, sin_t, cos_t, sin_t, *mp)


def _mixers_bwd(proj, dys, dproj, cos_t, sin_t, mp, TB, name):
    T = proj.shape[0]
    nb = T // TB
    r = TB // HALO
    RE = TB + 2 * HALO
    RC = TB + HALO
    cur, prev, nxt, tcur, tprev, tnxt = _mixer_in_specs(TB, nb)
    dcur = pl.BlockSpec((TB, 4 * HALF), lambda i: (i, 0))
    dnxt = pl.BlockSpec((HALO, 4 * HALF), lambda i: (jnp.minimum((i + 1) * r, nb * r - 1), 0))

    def body(zc_ref, zp_ref, zn_ref, dyc_ref, dyn_ref, cc_ref, sc_ref, cp_ref, sp_ref, cn_ref, sn_ref,
             lg_ref, lb_ref, sgw_ref, sgb_ref, cvw_ref, cvb_ref, cvg_ref, cvbb_ref, sinks_ref, scw_ref, dp_in_ref,
             dz_ref, dlg_ref, dlb_ref, dsgw_ref, dsgb_ref, dcvw_ref, dcvb_ref, dcvg_ref, dcvbb_ref, dsink_ref, dscw_ref,
             scr_ref, scr2_ref, k_ref, v_ref, dk_ref, dv_ref, dq_ref, sh_ref, sh2_ref):
        del dp_in_ref
        i = pl.program_id(0)
        pm = (i > 0).astype(F32)
        nm = (i < nb - 1).astype(F32)

        @pl.when(i == 0)
        def _():
            for ref in (dlg_ref, dlb_ref, dsgw_ref, dsgb_ref, dcvw_ref, dcvb_ref, dcvg_ref, dcvbb_ref, dsink_ref, dscw_ref):
                ref[...] = jnp.zeros_like(ref)

        def colsE(c0, c1):
            return jnp.concatenate([zp_ref[:, c0:c1].astype(F32) * pm, zc_ref[:, c0:c1].astype(F32),
                                    zn_ref[:, c0:c1].astype(F32)], axis=0)

        def colsC(c0, c1):
            return jnp.concatenate([zc_ref[:, c0:c1].astype(F32), zn_ref[:, c0:c1].astype(F32)], axis=0)

        def dyC(c0, c1):
            return jnp.concatenate([dyc_ref[:, c0:c1].astype(F32), dyn_ref[:, c0:c1].astype(F32) * nm], axis=0)

        za = zc_ref[:, C_ZA:C_ZA + 2 * HALF].astype(F32)
        a = _gelu(za)
        u = a[:, :HALF]
        lg = lg_ref[...]
        vn, vh, rs = _ln_fwd(a[:, HALF:], lg, lb_ref[...])
        vnb = vn.astype(BF16)
        dya = dyc_ref[:, 0:HALF].astype(F32)
        tril = _tril_mask()
        lane128 = lax.broadcasted_iota(jnp.int32, (SG_CHUNK, 128), 1)
        du_parts, dvn_parts = [], []
        for ci in range(r):
            rows = slice(ci * SG_CHUNK, (ci + 1) * SG_CHUNK)
            du_g, dvn_g = [], []
            for g in range(SG_GROUPS):
                cols = slice(g * 128, (g + 1) * 128)
                wt = jnp.where(tril, sgw_ref[g], 0.0).astype(BF16)
                vb = vnb[rows, cols]
                mixed = _dot(wt, vb) + sgb_ref[:, g:g + 1]
                dy_blk = dya[rows, cols]
                du_g.append(dy_blk * mixed)
                dmix = dy_blk * u[rows, cols]
                dmb = dmix.astype(BF16)
                dvn_g.append(_dot_tn(wt, dmb))
                dsgw_ref[g] += jnp.where(tril, _dot_nt(dmb, vb), 0.0)
                dsgb_ref[...] += jnp.where(lane128 == g, jnp.sum(dmix, axis=1, keepdims=True), 0.0)
            du_parts.append(jnp.concatenate(du_g, axis=1))
            dvn_parts.append(jnp.concatenate(dvn_g, axis=1))
        du = jnp.concatenate(du_parts, axis=0) if r > 1 else du_parts[0]
        dvn = jnp.concatenate(dvn_parts, axis=0) if r > 1 else dvn_parts[0]
        dlg_ref[...] += jnp.sum(dvn * vh, axis=0, keepdims=True)
        dlb_ref[...] += jnp.sum(dvn, axis=0, keepdims=True)
        dvv = _ln_bwd(dvn, vh, rs, lg)
        gg = _gelu_grad(za)
        dz_ref[:, C_ZA:C_ZA + HALF] = (du * gg[:, :HALF]).astype(BF16)
        dz_ref[:, C_ZA + HALF:C_ZA + 2 * HALF] = (dvv * gg[:, HALF:]).astype(BF16)

        RB = TB + CV_PAD

        def colsB(c0, c1):
            return jnp.concatenate([zp_ref[HALO - CV_PAD:, c0:c1].astype(F32) * pm, zc_ref[:, c0:c1].astype(F32),
                                    zn_ref[:CV_PAD, c0:c1].astype(F32)], axis=0)

        sh_ref[0] = colsB(C_ZB, C_ZB + HALF) * _sig(colsB(C_ZB + HALF, C_ZB + 2 * HALF))
        _sublane_shifts(sh_ref, RB + CV_PAD)
        c = jnp.broadcast_to(cvb_ref[...], (RB, HALF))
        for k in range(CV_KERNEL):
            c = c + cvw_ref[k:k + 1, :] * _tap(sh_ref, CV_PAD - (CV_KERNEL - 1) + k, RB)
        cvg = cvg_ref[...]
        n, ch, rc = _ln_fwd(c, cvg, cvbb_ref[...])
        sn = _sig(n)
        dyb = jnp.concatenate([dyc_ref[:, HALF:2 * HALF].astype(F32), dyn_ref[:CV_PAD, HALF:2 * HALF].astype(F32) * nm], axis=0)
        dn = dyb * (sn + n * sn * (1.0 - sn))
        dno = dn[:TB]
        dcvg_ref[...] += jnp.sum(dno * ch[:TB], axis=0, keepdims=True)
        dcvbb_ref[...] += jnp.sum(dno, axis=0, keepdims=True)
        dc = _ln_bwd(dn, ch, rc, cvg)
        sh2_ref[0] = dc
        _sublane_shifts(sh2_ref, RB)
        dcvb_ref[...] += jnp.sum(dc[:TB], axis=0, keepdims=True)
        dy0 = None
        for k in range(CV_KERNEL):
            wk = cvw_ref[k:k + 1, :]
            t = wk * _tap(sh2_ref, CV_KERNEL - 1 - k, TB)
            dy0 = t if dy0 is None else dy0 + t
            dcvw_ref[k:k + 1, :] += jnp.sum(dc[:TB] * _tap(sh_ref, CV_PAD - (CV_KERNEL - 1) + k, TB), axis=0, keepdims=True)
        ab = zc_ref[:, C_ZB:C_ZB + HALF].astype(F32)
        sg = _sig(zc_ref[:, C_ZB + HALF:C_ZB + 2 * HALF].astype(F32))
        dz_ref[:, C_ZB:C_ZB + HALF] = (dy0 * sg).astype(BF16)
        dz_ref[:, C_ZB + HALF:C_ZB + 2 * HALF] = (dy0 * ab * sg * (1.0 - sg)).astype(BF16)

        zd = colsE(C_ZD + HALF, C_ZD + 3 * HALF)
        scr_ref[...] = zd[:, :HALF] * zd[:, HALF:]
        dcv = dyC(3 * HALF, 4 * HALF) * colsC(C_ZD, C_ZD + HALF)
        scr2_ref[...] = dcv
        cv = None
        dud = None
        for k in range(SC_KERNEL):
            wk = scw_ref[k:k + 1, :]
            us = scr_ref[pl.ds(HALO - (SC_KERNEL - 1) + k, TB), :]
            t = wk * us
            cv = t if cv is None else cv + t
            t2 = wk * scr2_ref[pl.ds(SC_KERNEL - 1 - k, TB), :]
            dud = t2 if dud is None else dud + t2
            dscw_ref[k:k + 1, :] += jnp.sum(dcv[:TB] * us, axis=0, keepdims=True)
        dz_ref[:, C_ZD:C_ZD + HALF] = (dyc_ref[:, 3 * HALF:4 * HALF].astype(F32) * cv).astype(BF16)
        dz_ref[:, C_ZD + HALF:C_ZD + 2 * HALF] = (dud * zc_ref[:, C_ZD + 2 * HALF:C_ZD + 3 * HALF].astype(F32)).astype(BF16)
        dz_ref[:, C_ZD + 2 * HALF:C_ZD + 3 * HALF] = (dud * zc_ref[:, C_ZD + HALF:C_ZD + 2 * HALF].astype(F32)).astype(BF16)

        cosE = jnp.concatenate([cp_ref[...], cc_ref[...], cn_ref[...]], axis=0)
        sinE = jnp.concatenate([sp_ref[...], sc_ref[...], sn_ref[...]], axis=0)
        k_ref[...] = _rope(colsE(C_K, C_K + 128), cosE, sinE).astype(BF16)
        v_ref[...] = colsE(C_V, C_V + 128).astype(BF16)
        dk_ref[...] = jnp.zeros_like(dk_ref)
        dv_ref[...] = jnp.zeros_like(dv_ref)
        q = jnp.concatenate([_rope(colsC(C_Q + 128 * j, C_Q + 128 * (j + 1)), cosE[HALO:], sinE[HALO:])
                             for j in range(4)], axis=1).astype(BF16)
        dO = dyC(2 * HALF, 3 * HALF).astype(BF16)
        lane_s = lax.broadcasted_iota(jnp.int32, (1, 128), 1)
        for qb in range(r + 1):
            first_ok = (i * r + qb) > 0
            rows = slice(qb * WINDOW, (qb + 1) * WINDOW)
            band = slice(qb * WINDOW, qb * WINDOW + 2 * WINDOW)
            for h in range(N_KV_HEADS):
                hc = slice(h * HEAD_DIM, (h + 1) * HEAD_DIM)
                kh = k_ref[band, hc]
                vh_ = v_ref[band, hc]
                heads = [slice((h * Q_PER_KV + g) * HEAD_DIM, (h * Q_PER_KV + g + 1) * HEAD_DIM) for g in range(Q_PER_KV)]
                qs = jnp.concatenate([q[rows, hs] for hs in heads], axis=0)
                dos = jnp.concatenate([dO[rows, hs] for hs in heads], axis=0)
                probs, p_sink = _attn_probs(qs, kh, _sink_col(sinks_ref, h), first_ok)
                dP = _dot_nt(dos, vh_)
                rsum = jnp.sum(probs * dP, axis=-1, keepdims=True)
                dS = (probs * (dP - rsum) * (HEAD_DIM ** -0.5)).astype(BF16)
                dk_ref[band, hc] += _dot_tn(dS, qs)
                dv_ref[band, hc] += _dot_tn(probs.astype(BF16), dos)
                if qb < r:
                    dqs = _dot(dS, kh)
                    dsk = -p_sink * rsum
                    for g in range(Q_PER_KV):
                        dq_ref[rows, heads[g]] = dqs[g * WINDOW:(g + 1) * WINDOW]
                        dsink_ref[...] += jnp.where(lane_s == h * Q_PER_KV + g, jnp.sum(dsk[g * WINDOW:(g + 1) * WINDOW]), 0.0)
        cosC, sinC = cc_ref[...], sc_ref[...]
        for j in range(4):
            dz_ref[:, C_Q + 128 * j:C_Q + 128 * (j + 1)] = _rope_t(dq_ref[:, 128 * j:128 * (j + 1)], cosC, sinC).astype(BF16)
        dz_ref[:, C_K:C_K + 128] = _rope_t(dk_ref[HALO:HALO + TB, :], cosC, sinC).astype(BF16)
        dz_ref[:, C_V:C_V + 128] = dv_ref[HALO:HALO + TB, :].astype(BF16)

    small = [((1, HALF), F32), ((1, HALF), F32), ((SG_GROUPS, SG_CHUNK, SG_CHUNK), F32), ((SG_CHUNK, 128), F32),
             ((32, HALF), F32), ((1, HALF), F32), ((1, HALF), F32), ((1, HALF), F32), ((1, 128), F32), ((8, HALF), F32)]
    outs = pl.pallas_call(
        body, name=name, grid=(nb,),
        in_specs=[cur, prev, nxt, dcur, dnxt, tcur, tcur, tprev, tprev, tnxt, tnxt] + _mixer_param_specs()
                 + [pl.BlockSpec(memory_space=pl.ANY)],
        out_specs=[pl.BlockSpec((TB, MIX_W), lambda i: (i, 1))] + [_full(s) for s, _ in small],
        out_shape=[jax.ShapeDtypeStruct((T, PROJ_PAD), BF16)] + [jax.ShapeDtypeStruct(s, d) for s, d in small],
        scratch_shapes=[pltpu.VMEM((RE, HALF), F32), pltpu.VMEM((RC, HALF), F32), pltpu.VMEM((RE, 128), BF16), pltpu.VMEM((RE, 128), BF16),
                        pltpu.VMEM((RE, 128), F32), pltpu.VMEM((RE, 128), F32), pltpu.VMEM((TB, HALF), F32),
                        pltpu.VMEM((8, TB + 2 * CV_PAD, HALF), F32), pltpu.VMEM((8, TB + CV_PAD, HALF), F32)],
        input_output_aliases={21: 0},
        compiler_params=_params("arbitrary"),
    )(proj, proj, proj, dys, dys, cos_t, sin_t, cos_t, sin_t, cos_t, sin_t, *mp, dproj)
    return outs


def _rope_tables(T):
    pos = jnp.arange(T, dtype=F32)
    inv_freq = 1.0 / (ROPE_THETA ** (jnp.arange(0, HEAD_DIM, 2, dtype=F32) / HEAD_DIM))
    ang = pos[:, None] * inv_freq[None, :]
    cos, sin = jnp.cos(ang), jnp.sin(ang)
    cos_t = jnp.concatenate([cos, cos, cos, cos], axis=1)
    sin_t = jnp.concatenate([-sin, sin, -sin, sin], axis=1)
    return cos_t, sin_t


def _mixer_params(l, sg_ln_g, sg_ln_b, sg_w, sg_b, cv_w, cv_b, cv_ln_g, cv_ln_b, attn_sinks, sc_w):
    sgb_t = jnp.zeros((SG_CHUNK, 128), F32).at[:, :SG_GROUPS].set(sg_b[l].T)
    cvw = jnp.zeros((32, HALF), F32).at[:CV_KERNEL].set(cv_w[l])
    scw = jnp.zeros((8, HALF), F32).at[:SC_KERNEL].set(sc_w[l])
    sinks = jnp.zeros((1, 128), F32).at[0, :N_Q_HEADS].set(attn_sinks[l])
    return [sg_ln_g[l][None], sg_ln_b[l][None], sg_w[l], sgb_t, cvw, cv_b[l][None], cv_ln_g[l][None], cv_ln_b[l][None], sinks, scw]


def _w_in_layout(w_in_g):
    cut = MIX_W - 2 * W_IN_SHARD
    return jnp.concatenate([w_in_g[2][cut:], w_in_g[3], jnp.zeros((MIX_W - GATE_W, D_MODEL), w_in_g.dtype),
                            w_in_g[0], w_in_g[1], w_in_g[2][:cut]], axis=0)


def _w_in_unlayout(dw):
    cut = MIX_W - 2 * W_IN_SHARD
    return jnp.stack([dw[MIX_W:MIX_W + W_IN_SHARD], dw[MIX_W + W_IN_SHARD:MIX_W + 2 * W_IN_SHARD],
                      jnp.concatenate([dw[MIX_W + 2 * W_IN_SHARD:], dw[:W_IN_SHARD - cut]], axis=0),
                      dw[W_IN_SHARD - cut:GATE_W]], axis=0)


def _device_step(x, tgt, norm_mix, norm_ffn, norm_final, mixer_params, w_in_p, wb_g, wo_g, wgu_g, wd_g):
    T = x.shape[0]
    tables = _rope_tables(T)
    saved = []
    for l in range(DEPTH):
        lw = dict(w_in=w_in_p[l], w_branch=wb_g[l], w_out=wo_g[l], w_gate_up=wgu_g[l], w_down=wd_g[l],
                  norm_mix=norm_mix[l][None], norm_ffn=norm_ffn[l][None], mixer=mixer_params[l], after=jnp.zeros((8, 128), F32))
        x, sv = _fwd_layer(l, x, lw, tables)
        saved.append((lw, sv))
    dx, dnf, loss = _final_loss(x, norm_final[None], tgt, 256, "final_loss")
    grads = [None] * DEPTH
    for l in reversed(range(DEPTH)):
        lw, sv = saved[l]
        dxm, g_ffn = _bwd_layer_ffn(l, dx, lw, sv)
        dx, g_mix = _bwd_layer_mix(l, dxm, lw, sv, tables)
        raw = {**g_ffn, **g_mix}
        grads[l] = {**raw, **_small_views(raw)}
    return loss, dx, dnf[0], grads


MIX_BLOCK = 256


def _fwd_layer(l, x, lw, tables):
    return _fwd_layer_rest(l, x, _fwd_layer_mix(l, x, lw, tables), lw)


def _fwd_layer_mix(l, x, lw, tables):
    proj, xn = _rms_mm(x, lw["norm_mix"], lw["w_in"], min(x.shape[0], 1024), 2176, f"proj{l}")
    return proj, xn, _mixers_fwd(proj, *tables, lw["mixer"], MIX_BLOCK, f"mixers_fwd{l}")


def _fwd_layer_rest(l, x, mixed, lw):
    proj, xn, ys = mixed
    TM = min(x.shape[0], 1024)
    xm, merged = _merge_fwd(x, ys, proj, lw["w_branch"], lw["w_out"], 256, f"merge_fwd{l}")
    gu, hn = _rms_mm(xm, lw["norm_ffn"], lw["w_gate_up"], TM, GU_SHARD, f"ffn_up{l}")
    x_out = _ffn_down(xm, gu, lw["w_down"], 256, f"ffn_down{l}")
    return x_out, (x, proj, xn, ys, xm, merged, gu, hn)


def _bwd_layer_ffn(l, dx, lw, sv):
    x_in, proj, xn, ys, xm, merged, gu, hn = sv
    T = dx.shape[0]
    tkk = min(T, 1024)
    gk = T // tkk
    dgu, act = _swiglu_bwd(dx, gu, lw["w_down"], 256, f"swiglu_bwd{l}", lw["after"])
    d_wd = _mm_tn(act, dx, (2, 1, gk), (tkk, D_FF // 2), lambda i, j, k: (k, i), (tkk, D_MODEL), lambda i, j, k: (k, 0),
                  (D_FF, D_MODEL), (D_FF // 2, D_MODEL), lambda i, j, k: (i, 0), f"dw_down{l}")
    d_wgu = _mm_tn(hn, dgu, (1, N_CHIPS, gk), (tkk, D_MODEL), lambda i, j, k: (k, 0), (tkk, GU_SHARD), lambda i, j, k: (k, j),
                   (N_CHIPS, D_MODEL, GU_SHARD), (None, D_MODEL, GU_SHARD), lambda i, j, k: (j, 0, 0), f"dw_gate_up{l}")
    dxm, d_nffn = _mm_nt_rmsbwd(dgu, lw["w_gate_up"], xm, lw["norm_ffn"], dx, min(T, 512), GU_SHARD, f"ffn_up_bwd{l}")
    dys, dbr, dproj = _merge_bwd(dxm, ys, proj, lw["w_branch"], lw["w_out"], 256, f"merge_bwd{l}")
    d_wo = _mm_tn(merged, dxm, (2, 1, gk), (tkk, 512), lambda i, j, k: (k, i), (tkk, D_MODEL), lambda i, j, k: (k, 0),
                  (D_MODEL, D_MODEL), (512, D_MODEL), lambda i, j, k: (i, 0), f"dw_out{l}")
    d_wb = _mm_tn(ys, dbr, (N_BRANCH, 1, gk), (tkk, HALF), lambda i, j, k: (k, i), (tkk, D_MODEL), lambda i, j, k: (k, i),
                  (N_CHIPS, N_BRANCH, HALF, 256), (N_CHIPS, None, HALF, 256), lambda i, j, k: (0, i, 0, 0), f"dw_branch{l}", col_split=N_CHIPS)
    return (dxm, dys, dproj), dict(w_branch=d_wb, w_out=d_wo, w_gate_up=d_wgu, w_down=d_wd, norm_ffn=d_nffn)


def _bwd_layer_mix(l, carry, lw, sv, tables):
    dxm, dys, dproj = carry
    x_in, proj, xn, ys, xm, merged, gu, hn = sv
    T = dxm.shape[0]
    tkk = min(T, 1024)
    gk = T // tkk
    mb = _mixers_bwd(proj, dys, dproj, *tables, lw["mixer"], MIX_BLOCK, f"mixers_bwd{l}")
    dproj = mb[0]
    d_win = _mm_tn(dproj, xn, (PROJ_PAD // 2176, 1, gk), (tkk, 2176), lambda i, j, k: (k, i), (tkk, D_MODEL), lambda i, j, k: (k, 0),
                   (PROJ_PAD, D_MODEL), (2176, D_MODEL), lambda i, j, k: (i, 0), f"dw_in{l}")
    dx, d_nmix = _mm_nt_rmsbwd(dproj, lw["w_in"], x_in, lw["norm_mix"], dxm, min(T, 512), 2176, f"proj_bwd{l}")
    return dx, dict(w_in=d_win, norm_mix=d_nmix, sg_ln_g=mb[1], sg_ln_b=mb[2], sg_w=mb[3], sg_b=mb[4], cv_w=mb[5], cv_b=mb[6],
                    cv_ln_g=mb[7], cv_ln_b=mb[8], attn_sinks=mb[9], sc_w=mb[10])


ANY = pl.BlockSpec(memory_space=pl.ANY)
BIG = ("w_in", "w_branch", "w_out", "w_gate_up", "w_down")
HALF_SHAPE = {"w_in": (2, W_IN_SHARD // 2, D_MODEL), "w_branch": (2, 1024, 256), "w_out": (2, 128, D_MODEL),
              "w_gate_up": (2, 512, GU_SHARD), "w_down": (2, 352, D_MODEL)}
NB = len(BIG)


def _place():
    x, y, c = lax.axis_index("x"), lax.axis_index("y"), lax.axis_index("c")
    chips = [(1 - x, y), (x, 1 - y), (1 - x, 1 - y)]
    return x, y, c, 2 * x + y, chips, [2 * px + py for px, py in chips]


def _remote(src, dst, ssem, rsem, dev):
    return pltpu.make_async_remote_copy(src_ref=src, dst_ref=dst, send_sem=ssem, recv_sem=rsem, device_id=dev, device_id_type=MESH)


HBM_SPEC = pl.BlockSpec(memory_space=pltpu.HBM)
SEM_SPEC = pl.BlockSpec(memory_space=pltpu.SEMAPHORE)
DATAFLOW = pltpu.SideEffectType.DATAFLOW_SIDE_EFFECTING


def _ici_ends(kind, src, land, j, c, chip, chip_ids):
    if kind == "gather":
        return src.at[c], land.at[chip, c], land.at[chip_ids[j], c]
    return src.at[chip_ids[j]], land.at[chip], land.at[chip_ids[j]]


def _ici_start(kind, srcs, land_shapes, name):
    n = len(srcs)

    def body(*refs):
        src, land = refs[:n], refs[n:2 * n]
        ssem, rsem, token = refs[2 * n], refs[2 * n + 1], refs[-1]
        x, y, c, chip, chips, chip_ids = _place()
        for k in range(n):
            for j in range(3):
                s, d, _ = _ici_ends(kind, src[k], land[k], j, c, chip, chip_ids)
                _remote(s, d, ssem.at[3 * k + j], rsem.at[3 * k + j], (*chips[j], c)).start()
        token[...] = jnp.zeros_like(token)

    sem = pltpu.SemaphoreType.DMA((3 * n,))
    outs = pl.pallas_call(
        body, name=name,
        out_shape=(sem, sem, *[pltpu.HBM(s.shape, s.dtype) for s in srcs], *[pltpu.HBM(sh, BF16) for sh in land_shapes],
                   jax.ShapeDtypeStruct((8, 128), F32)),
        in_specs=[HBM_SPEC] * (2 * n),
        out_specs=(SEM_SPEC, SEM_SPEC, *[HBM_SPEC] * (2 * n), pl.BlockSpec(memory_space=pltpu.VMEM)),
        input_output_aliases={i: 2 + i for i in range(2 * n)},
        compiler_params=pltpu.CompilerParams(has_side_effects=DATAFLOW),
    )(*[pltpu.with_memory_space_constraint(s, pltpu.HBM) for s in srcs],
      *[pltpu.with_memory_space_constraint(lax.empty(sh, BF16), pltpu.HBM) for sh in land_shapes])
    return (kind, outs[0], outs[1], list(outs[2:2 + n]), list(outs[2 + n:2 + 2 * n])), outs[-1]


def _ici_wait(handle, after, name):
    kind, ssem_in, rsem_in, srcs, lands = handle
    n = len(srcs)

    def body(*refs):
        src, land = refs[:n], refs[n:2 * n]
        ssem, rsem = refs[2 * n], refs[2 * n + 1]
        x, y, c, chip, chips, chip_ids = _place()
        for k in range(n):
            for j in range(3):
                s, _, mine = _ici_ends(kind, src[k], land[k], j, c, chip, chip_ids)
                cp = _remote(s, mine, ssem.at[3 * k + j], rsem.at[3 * k + j], (*chips[j], c))
                cp.wait_send()
                cp.wait_recv()

    outs = pl.pallas_call(
        body, name=name, out_shape=[pltpu.HBM(t.shape, t.dtype) for t in srcs + lands],
        in_specs=[HBM_SPEC] * (2 * n) + [SEM_SPEC, SEM_SPEC, ANY], out_specs=[HBM_SPEC] * (2 * n),
        input_output_aliases={i: i for i in range(2 * n)},
        compiler_params=pltpu.CompilerParams(has_side_effects=DATAFLOW),
    )(*srcs, *lands, ssem_in, rsem_in, after)
    return list(outs[:n]), list(outs[n:])


def _ag_pair(shards, lands, name):
    n = len(shards)

    def body(*refs):
        ins, outs = refs[:n], refs[2 * n:3 * n]
        token = refs[3 * n]
        s_fwd, r_fwd, s_own, r_own = refs[3 * n + 1:]
        x, y, c, chip, chips, chip_ids = _place()
        sib = (x, y, 1 - c)
        cps = []
        for k in range(n):
            cp = _remote(ins[k], outs[k].at[chip], s_own.at[k], r_own.at[k], sib)
            cp.start()
            cps.append(cp)
            for j in range(3):
                got = outs[k].at[chip_ids[j], c]
                cp = _remote(got, got, s_fwd.at[k, j], r_fwd.at[k, j], sib)
                cp.start()
                cps.append(cp)
        for k in range(n):
            _remote(ins[k], outs[k].at[chip], s_own.at[k], r_own.at[k], sib).wait_recv()
            for j in range(3):
                got = outs[k].at[chip_ids[j], 1 - c]
                _remote(got, got, s_fwd.at[k, j], r_fwd.at[k, j], sib).wait_recv()
        for cp in cps:
            cp.wait_send()
        token[...] = jnp.zeros_like(token)

    sem, sem1 = pltpu.SemaphoreType.DMA((n, 3)), pltpu.SemaphoreType.DMA((n,))
    outs = pl.pallas_call(
        body, name=name, out_shape=[jax.ShapeDtypeStruct(t.shape, t.dtype) for t in lands] + [jax.ShapeDtypeStruct((8, 128), F32)],
        in_specs=[ANY] * (2 * n), out_specs=[ANY] * n + [pl.BlockSpec(memory_space=pltpu.VMEM)],
        input_output_aliases={n + k: k for k in range(n)},
        scratch_shapes=[sem, sem, sem1, sem1], compiler_params=pltpu.CompilerParams(has_side_effects=True),
    )(*shards, *lands)
    return list(outs[:n]), outs[n]


def _rs_pair(grads, name):
    n_arr = len(grads)

    def body(*refs):
        ins, got = refs[:n_arr], refs[n_arr:2 * n_arr]
        ssem, rsem = refs[2 * n_arr:]
        x, y, c, _, _, _ = _place()
        sib = (x, y, 1 - c)
        sends = []
        for k in reversed(range(n_arr)):
            for q in range(N_CHIPS):
                cp = _remote(ins[k].at[q, 1 - c], got[k].at[q], ssem.at[k, q], rsem.at[k, q], sib)
                cp.start()
                sends.append(cp)
        for k in range(n_arr):
            for q in range(N_CHIPS):
                _remote(got[k].at[q], got[k].at[q], ssem.at[k, q], rsem.at[k, q], sib).wait_recv()
        for cp in sends:
            cp.wait_send()

    shp = [jax.ShapeDtypeStruct((N_CHIPS,) + g.shape[2:], BF16) for g in grads]
    sem = pltpu.SemaphoreType.DMA((n_arr, N_CHIPS))
    outs = pl.pallas_call(
        body, name=name, out_shape=shp, in_specs=[ANY] * n_arr, out_specs=[ANY] * n_arr,
        scratch_shapes=[sem, sem], compiler_params=pltpu.CompilerParams(has_side_effects=True),
    )(*grads)
    return list(outs)


def _rs_share(bufs, name):
    n = len(bufs)

    def body(*refs):
        outs = refs[n:2 * n]
        ssem, rsem = refs[2 * n:]
        x, y, c, _, _, _ = _place()
        sib = (x, y, 1 - c)
        sends = []
        for k in range(n):
            for l in range(DEPTH):
                cp = _remote(outs[k].at[l, c], outs[k].at[l, c], ssem.at[k, l], rsem.at[k, l], sib)
                cp.start()
                sends.append(cp)
        for k in range(n):
            for l in range(DEPTH):
                dst = outs[k].at[l, 1 - c]
                _remote(dst, dst, ssem.at[k, l], rsem.at[k, l], sib).wait_recv()
        for cp in sends:
            cp.wait_send()

    sem = pltpu.SemaphoreType.DMA((n, DEPTH))
    outs = pl.pallas_call(
        body, name=name, out_shape=[jax.ShapeDtypeStruct(b.shape, b.dtype) for b in bufs], in_specs=[ANY] * n, out_specs=[ANY] * n,
        input_output_aliases={k: k for k in range(n)},
        scratch_shapes=[sem, sem], compiler_params=pltpu.CompilerParams(has_side_effects=True),
    )(*bufs)
    return list(outs)


def _piece(src, idx, rows, width=128, align=1, transposed=False):
    return dict(src=src, idx=idx, rows=rows, width=width, align=align, transposed=transposed)


def _all_reduce_pieces(inputs, pieces, out_shapes, writes, name):
    n_in, n_out = len(inputs), len(out_shapes)
    offs, R = [], 0
    for p in pieces:
        R = -(-R // p["align"]) * p["align"]
        offs.append(R)
        R += p["rows"]
    R = -(-R // 8) * 8

    def body(*refs):
        ins, outs, token_ref = refs[:n_in], refs[n_in:n_in + n_out], refs[n_in + n_out]
        pair_ref, chip_ref, sum_ref, ssem, rsem = refs[n_in + n_out + 1:]
        token_ref[...] = jnp.zeros_like(token_ref)
        x, y, c, chip, chips, chip_ids = _place()
        pair_ref[c] = jnp.zeros((R, 128), F32)
        for p, off in zip(pieces, offs):
            v = ins[p["src"]][...].T[p["idx"]] if p["transposed"] else ins[p["src"]][p["idx"]]
            pair_ref[c, off:off + p["rows"], 0:p["width"]] = v
        mine = _remote(pair_ref.at[c], pair_ref.at[c], ssem.at[3], rsem.at[3], (x, y, 1 - c))
        mine.start()
        _remote(pair_ref.at[1 - c], pair_ref.at[1 - c], ssem.at[3], rsem.at[3], (x, y, 1 - c)).wait_recv()
        chip_ref[chip] = pair_ref[0] + pair_ref[1]
        cps = [_remote(chip_ref.at[chip], chip_ref.at[chip], ssem.at[j], rsem.at[j], (*chips[j], c)) for j in range(3)]
        for cp in cps:
            cp.start()
        for j in range(3):
            slot = chip_ref.at[chip_ids[j]]
            _remote(slot, slot, ssem.at[j], rsem.at[j], (*chips[j], c)).wait_recv()
        acc = chip_ref[0]
        for s in range(1, N_CHIPS):
            acc = acc + chip_ref[s]
        sum_ref[...] = acc
        for o, idx, p in writes:
            outs[o][idx] = sum_ref[offs[p]:offs[p] + pieces[p]["rows"], 0:pieces[p]["width"]]
        for cp in cps + [mine]:
            cp.wait_send()

    vm = pl.BlockSpec(memory_space=pltpu.VMEM)
    outs = pl.pallas_call(
        body, name=name, out_shape=[jax.ShapeDtypeStruct(s, F32) for s in out_shapes] + [jax.ShapeDtypeStruct((8, 128), F32)],
        in_specs=[vm] * n_in, out_specs=[vm] * (n_out + 1),
        scratch_shapes=[pltpu.VMEM((2, R, 128), F32), pltpu.VMEM((N_CHIPS, R, 128), F32), pltpu.VMEM((R, 128), F32),
                        pltpu.SemaphoreType.DMA((4,)), pltpu.SemaphoreType.DMA((4,))],
        compiler_params=pltpu.CompilerParams(vmem_limit_bytes=VMEM_LIMIT),
    )(*inputs)
    return list(outs[:n_out]), outs[n_out]


def _lanes(width):
    return [slice(k, min(k + 128, width)) for k in range(0, width, 128)]


def _gather_small_weights(cvw_z, scw_z):
    pieces, writes = [], []
    for i, arr in enumerate((cvw_z, scw_z)):
        for l in range(DEPTH):
            for ln in _lanes(HALF):
                writes.append((i, (l, slice(None), ln), len(pieces)))
                pieces.append(_piece(i, (l, slice(None), ln), arr.shape[1], align=8))
    (cvw, scw), tok = _all_reduce_pieces([cvw_z, scw_z], pieces, [cvw_z.shape, scw_z.shape], writes, "ag_small")
    return cvw, scw, tok


SMALL_RAW = dict(norm_mix=(1, D_MODEL), norm_ffn=(1, D_MODEL), sg_ln_g=(1, HALF), sg_ln_b=(1, HALF), cv_b=(1, HALF), cv_ln_g=(1, HALF),
                 cv_ln_b=(1, HALF))


def _all_reduce_small_grads(raw, d_nfinal, loss):
    names = list(SMALL_RAW) + ["attn_sinks", "sg_b", "sc_w", "cv_w", "sg_w"]
    out_shape = dict(norm_mix=(DEPTH, D_MODEL), norm_ffn=(DEPTH, D_MODEL), sg_ln_g=(DEPTH, HALF), sg_ln_b=(DEPTH, HALF), cv_b=(DEPTH, HALF),
                     cv_ln_g=(DEPTH, HALF), cv_ln_b=(DEPTH, HALF), attn_sinks=(DEPTH, N_Q_HEADS), sg_b=(DEPTH, SG_GROUPS, SG_CHUNK),
                     sc_w=(DEPTH, SC_KERNEL, HALF), cv_w=(DEPTH, CV_KERNEL, HALF), sg_w=(DEPTH, SG_GROUPS, SG_CHUNK, SG_CHUNK))
    inputs, pieces, writes = [], [], []

    def add(src, idx, rows, out, out_idx, **kw):
        writes.append((names.index(out) if out in names else out, out_idx, len(pieces)))
        pieces.append(_piece(src, idx, rows, **kw))

    for l in range(DEPTH):
        row = slice(l, l + 1)
        for n, (_, width) in SMALL_RAW.items():
            inputs.append(raw[l][n])
            for ln in _lanes(width):
                add(len(inputs) - 1, (slice(0, 1), ln), 1, n, (row, ln))
        inputs.append(raw[l]["attn_sinks"])
        add(len(inputs) - 1, (slice(0, 1), slice(0, N_Q_HEADS)), 1, "attn_sinks", (row, slice(None)), width=N_Q_HEADS)
    for l in range(DEPTH):
        inputs.append(raw[l]["sg_b"])
        add(len(inputs) - 1, (slice(0, SG_GROUPS), slice(None)), SG_GROUPS, "sg_b", (l,), align=8, transposed=True)
        inputs.append(raw[l]["sc_w"])
        for ln in _lanes(HALF):
            add(len(inputs) - 1, (slice(0, SC_KERNEL), ln), SC_KERNEL, "sc_w", (l, slice(None), ln), align=8)
        inputs.append(raw[l]["cv_w"])
        for ln in _lanes(HALF):
            add(len(inputs) - 1, (slice(0, CV_KERNEL), ln), CV_KERNEL, "cv_w", (l, slice(None), ln), align=8)
        inputs.append(raw[l]["sg_w"])
        for g in range(SG_GROUPS):
            add(len(inputs) - 1, (g,), SG_CHUNK, "sg_w", (l, g), align=8)
    n_names = len(names)
    inputs.append(d_nfinal)
    for ln in _lanes(D_MODEL):
        add(len(inputs) - 1, (slice(0, 1), ln), 1, n_names, (slice(0, 1), ln))
    inputs.append(loss)
    add(len(inputs) - 1, (slice(0, 1), slice(None)), 1, n_names + 1, (slice(0, 1), slice(None)))
    outs, tok = _all_reduce_pieces(inputs, pieces, [out_shape[n] for n in names] + [(1, D_MODEL), (1, 128)], writes, "ar_small")
    return dict(zip(names, outs[:n_names])), outs[n_names], outs[n_names + 1], tok


def _small_views(raw):
    v = {n: raw[n][0] for n in SMALL_RAW}
    v.update(sg_w=raw["sg_w"], sg_b=raw["sg_b"][:, :SG_GROUPS].T, cv_w=raw["cv_w"][:CV_KERNEL],
             attn_sinks=raw["attn_sinks"][0, :N_Q_HEADS], sc_w=raw["sc_w"][:SC_KERNEL])
    return v


def _row_tile(rows, cols, n_arrays):
    budget = 20 * 1024 * 1024 // (n_arrays * 2 * cols * 4)
    tiles = [t for t in range(16, min(rows, budget) + 1, 16) if rows % t == 0]
    assert tiles, (rows, cols)
    return tiles[-1]


def _add_pairs(g, got, place, name):
    _, _, rows, cols = g.shape
    tr = _row_tile(rows, cols, 3)

    def body(place_ref, a_ref, b_ref, o_ref):
        del place_ref
        o_ref[...] = (a_ref[...].astype(F32) + b_ref[...].astype(F32)).astype(BF16)

    spec = pl.BlockSpec((None, tr, cols), lambda q, i, p: (q, i, 0))
    grid_spec = pltpu.PrefetchScalarGridSpec(
        num_scalar_prefetch=1, grid=(N_CHIPS, rows // tr),
        in_specs=[pl.BlockSpec((None, None, tr, cols), lambda q, i, p: (q, p[1], i, 0)), spec], out_specs=spec)
    return pl.pallas_call(body, name=name, grid_spec=grid_spec, out_shape=jax.ShapeDtypeStruct((N_CHIPS, rows, cols), BF16),
                          compiler_params=_params("parallel", "parallel"))(place, g, got)


def _sum_chips(own, recv, place, l, buf, name, after):
    _, rows, cols = own.shape
    tr = _row_tile(rows, cols, 4)

    def body(place_ref, own_ref, recv_ref, *rest):
        chip = place_ref[0]
        acc = own_ref[...].astype(F32)
        for j in range(1, N_CHIPS):
            acc = acc + recv_ref[lax.rem(chip + j, N_CHIPS)].astype(F32)
        rest[-1][...] = acc

    in_specs = [pl.BlockSpec((None, tr, cols), lambda i, p: (p[0], i, 0)), pl.BlockSpec((N_CHIPS, tr, cols), lambda i, p: (0, i, 0)), ANY]
    args = [place, own, recv, after]
    aliases = {}
    if buf is not None:
        in_specs.append(ANY)
        args.append(buf)
        aliases = {4: 0}
    grid_spec = pltpu.PrefetchScalarGridSpec(
        num_scalar_prefetch=1, grid=(rows // tr,), in_specs=in_specs,
        out_specs=pl.BlockSpec((None, None, tr, cols), lambda i, p: (l, p[1], i, 0)))
    return pl.pallas_call(body, name=name, grid_spec=grid_spec, out_shape=jax.ShapeDtypeStruct((DEPTH, 2, rows, cols), F32),
                          input_output_aliases=aliases, compiler_params=_params("parallel"))(*args)


def _adamw(w, g, m, v, name):
    shape = w.shape
    lead, (rows, cols) = shape[:-2], shape[-2:]
    tr = _row_tile(rows, cols, 7)

    def body(w_ref, g_ref, m_ref, v_ref, d_ref, mo_ref, vo_ref):
        gv = g_ref[...]
        mn = ADAM_B1 * m_ref[...] + (1.0 - ADAM_B1) * gv
        vn = ADAM_B2 * v_ref[...] + (1.0 - ADAM_B2) * (gv * gv)
        m_hat = mn / (1.0 - ADAM_B1 ** ADAM_STEP)
        v_hat = vn / (1.0 - ADAM_B2 ** ADAM_STEP)
        d_ref[...] = -ADAM_LR * (m_hat / (jnp.sqrt(v_hat) + ADAM_EPS) + ADAM_WD * w_ref[...])
        mo_ref[...] = mn
        vo_ref[...] = vn

    spec = pl.BlockSpec((None,) * len(lead) + (tr, cols), lambda *idx: (*idx, 0))
    grid = lead + (rows // tr,)
    return list(pl.pallas_call(body, name=name, grid=grid, in_specs=[spec] * 4, out_specs=[spec] * 3,
                               out_shape=[jax.ShapeDtypeStruct(shape, F32)] * 3,
                               compiler_params=_params(*(["parallel"] * len(grid))))(w, g, m, v))


def _adamw_small(ws, gs, ms, vs, name):
    n = len(ws)

    def body(*refs):
        for i in range(n):
            gv = refs[n + i][...]
            mn = ADAM_B1 * refs[2 * n + i][...] + (1.0 - ADAM_B1) * gv
            vn = ADAM_B2 * refs[3 * n + i][...] + (1.0 - ADAM_B2) * (gv * gv)
            m_hat = mn / (1.0 - ADAM_B1 ** ADAM_STEP)
            v_hat = vn / (1.0 - ADAM_B2 ** ADAM_STEP)
            refs[4 * n + i][...] = -ADAM_LR * (m_hat / (jnp.sqrt(v_hat) + ADAM_EPS) + ADAM_WD * refs[i][...])
            refs[5 * n + i][...] = mn
            refs[6 * n + i][...] = vn

    vm = pl.BlockSpec(memory_space=pltpu.VMEM)
    outs = pl.pallas_call(body, name=name, out_shape=[jax.ShapeDtypeStruct(t.shape, F32) for t in ws] * 3,
                          in_specs=[vm] * (4 * n), out_specs=[vm] * (3 * n),
                          compiler_params=pltpu.CompilerParams(vmem_limit_bytes=VMEM_LIMIT))(*ws, *gs, *ms, *vs)
    return outs[:n], outs[n:2 * n], outs[2 * n:]


SMALL = ("norm_mix", "sg_ln_g", "sg_ln_b", "sg_w", "sg_b", "cv_w", "cv_b", "cv_ln_g", "cv_ln_b", "attn_sinks", "sc_w", "norm_ffn", "norm_final")
ORDER = ("norm_mix", "w_in", "sg_ln_g", "sg_ln_b", "sg_w", "sg_b", "cv_w", "cv_b", "cv_ln_g", "cv_ln_b", "attn_sinks", "sc_w",
         "w_branch", "w_out", "norm_ffn", "w_gate_up", "w_down", "norm_final")


def kernel(x, norm_mix, w_in, sg_ln_g, sg_ln_b, sg_w, sg_b, cv_w, cv_b, cv_ln_g, cv_ln_b, attn_sinks, sc_w, w_branch, w_out, norm_ffn, w_gate_up, w_down, norm_final, loss_target, m_norm_mix, m_w_in, m_sg_ln_g, m_sg_ln_b, m_sg_w, m_sg_b, m_cv_w, m_cv_b, m_cv_ln_g, m_cv_ln_b, m_attn_sinks, m_sc_w, m_w_branch, m_w_out, m_norm_ffn, m_w_gate_up, m_w_down, m_norm_final, v_norm_mix, v_w_in, v_sg_ln_g, v_sg_ln_b, v_sg_w, v_sg_b, v_cv_w, v_cv_b, v_cv_ln_g, v_cv_ln_b, v_attn_sinks, v_sc_w, v_w_branch, v_w_out, v_norm_ffn, v_w_gate_up, v_w_down, v_norm_final):
    W = dict(norm_mix=norm_mix, w_in=w_in, sg_ln_g=sg_ln_g, sg_ln_b=sg_ln_b, sg_w=sg_w, sg_b=sg_b, cv_w=cv_w, cv_b=cv_b, cv_ln_g=cv_ln_g,
             cv_ln_b=cv_ln_b, attn_sinks=attn_sinks, sc_w=sc_w, w_branch=w_branch, w_out=w_out, norm_ffn=norm_ffn, w_gate_up=w_gate_up,
             w_down=w_down, norm_final=norm_final)
    M = dict(norm_mix=m_norm_mix, w_in=m_w_in, sg_ln_g=m_sg_ln_g, sg_ln_b=m_sg_ln_b, sg_w=m_sg_w, sg_b=m_sg_b, cv_w=m_cv_w, cv_b=m_cv_b,
             cv_ln_g=m_cv_ln_g, cv_ln_b=m_cv_ln_b, attn_sinks=m_attn_sinks, sc_w=m_sc_w, w_branch=m_w_branch, w_out=m_w_out,
             norm_ffn=m_norm_ffn, w_gate_up=m_w_gate_up, w_down=m_w_down, norm_final=m_norm_final)
    V = dict(norm_mix=v_norm_mix, w_in=v_w_in, sg_ln_g=v_sg_ln_g, sg_ln_b=v_sg_ln_b, sg_w=v_sg_w, sg_b=v_sg_b, cv_w=v_cv_w, cv_b=v_cv_b,
             cv_ln_g=v_cv_ln_g, cv_ln_b=v_cv_ln_b, attn_sinks=v_attn_sinks, sc_w=v_sc_w, w_branch=v_w_branch, w_out=v_w_out,
             norm_ffn=v_norm_ffn, w_gate_up=v_w_gate_up, w_down=v_w_down, norm_final=v_norm_final)
    mx, my, mc = lax.axis_index("x"), lax.axis_index("y"), lax.axis_index("c")
    chip = 2 * mx + my

    place = jnp.stack([chip, mc]).astype(jnp.int32)
    tables = _rope_tables(x.shape[1])
    land_shapes = [(N_CHIPS,) + HALF_SHAPE[n] for n in BIG]
    part_shapes = {n: (N_CHIPS,) + HALF_SHAPE[n][1:] for n in BIG}

    T_ = lambda t: jnp.swapaxes(t, 1, 2)
    Wt, Mt, Vt = ({**t, "w_in": T_(t["w_in"])} for t in (W, M, V))

    def shards_of(l, tok):
        return [(Wt[n][l] + tok[0, 0]).astype(BF16).reshape(HALF_SHAPE[n]) for n in BIG]

    def finish_gather(tag, handle, after):
        srcs, lands = _ici_wait(handle, after, f"ag_wait{tag}")
        return _ag_pair(srcs, lands, f"ag_pair{tag}")[0]

    def mix_weights(l, g_in):
        return dict(w_in=_w_in_layout(g_in[0].reshape(N_CHIPS, W_IN_SHARD, D_MODEL)), norm_mix=norm_mix[l][None], norm_ffn=norm_ffn[l][None],
                    mixer=_mixer_params(l, sg_ln_g, sg_ln_b, sg_w, sg_b, cvw_full, cv_b, cv_ln_g, cv_ln_b, attn_sinks, scw_full))

    def rest_weights(lw, g_rest):
        G = dict(zip(BIG[1:], g_rest))
        lw.update(w_branch=G["w_branch"].reshape(N_CHIPS, N_BRANCH, HALF, 256), w_out=G["w_out"].reshape(D_MODEL, D_MODEL),
                  w_gate_up=G["w_gate_up"].reshape(N_CHIPS, D_MODEL, GU_SHARD), w_down=G["w_down"].reshape(D_FF, D_MODEL))

    def shard_major(g):
        t = dict(g)
        if "w_in" in t:
            t["w_in"] = _w_in_unlayout(t["w_in"])
        return {n: t[n].reshape((N_CHIPS,) + HALF_SHAPE[n]) for n in BIG if n in t}

    def pair_sums(tag, g):
        names = list(g)
        got = _rs_pair([g[n] for n in names], f"rs_pair{tag}")
        return names, [_add_pairs(g[n], got[k], place, f"rs_add{tag}_{n}") for k, n in enumerate(names)]

    zero_tok = jnp.zeros((8, 128), F32)
    south = (mc == 0).astype(F32)
    cvw_z = lax.dynamic_update_slice(jnp.zeros((DEPTH, CV_KERNEL, HALF), F32), cv_w * south, (0, 0, chip * 128))
    scw_z = lax.dynamic_update_slice(jnp.zeros((DEPTH, SC_KERNEL, HALF), F32), sc_w * south, (0, 0, chip * 128))
    cvw_full, scw_full, tok = _gather_small_weights(cvw_z, scw_z)

    handles = []
    for l in range(DEPTH):
        for tag, sl in (("in", slice(0, 1)), ("rest", slice(1, NB))):
            h, tok = _ici_start("gather", shards_of(l, tok)[sl], land_shapes[sl], f"ag_start{l}{tag}")
            handles.append(h)
    x_l, saved = x[0], []
    for l in range(DEPTH):
        lw = mix_weights(l, finish_gather(f"{l}in", handles[2 * l], x_l if l else tok))
        mixed = _fwd_layer_mix(l, x_l, lw, tables)
        rest_weights(lw, finish_gather(f"{l}rest", handles[2 * l + 1], mixed[2]))
        x_l, sv = _fwd_layer_rest(l, x_l, mixed, lw)
        saved.append((lw, sv))
    (lw0, sv0), (lw1, sv1) = saved
    dx, d_nfinal, loss = _final_loss(x_l, norm_final[None], loss_target[0], 256, "final_loss")

    lw1["after"] = zero_tok
    carry, g_ffn1 = _bwd_layer_ffn(1, dx, lw1, sv1)
    dx, g_mix1 = _bwd_layer_mix(1, carry, lw1, sv1, tables)
    names1, part1 = pair_sums("1", shard_major({**g_ffn1, **g_mix1}))
    hr1, tok = _ici_start("scatter", part1, [part_shapes[n] for n in names1], "rs_start1")

    lw0["after"] = tok
    carry, g_ffn0 = _bwd_layer_ffn(0, dx, lw0, sv0)
    names_a, part_a = pair_sums("0a", shard_major(g_ffn0))
    _, recv1 = _ici_wait(hr1, part_a[0], "rs_wait1")
    hra, tok = _ici_start("scatter", part_a, [part_shapes[n] for n in names_a], "rs_start0a")

    lw0["mixer"] = [lw0["mixer"][0] + tok[0, 0]] + lw0["mixer"][1:]
    dx, g_mix0 = _bwd_layer_mix(0, carry, lw0, sv0, tables)
    _, recv_a = _ici_wait(hra, dx, "rs_wait0a")

    small_red, nf_red, loss_red, tok = _all_reduce_small_grads([{**g_ffn0, **g_mix0}, {**g_ffn1, **g_mix1}], d_nfinal, loss)
    small_red["norm_final"] = nf_red
    loss_out = loss_red[0, 0]
    for n in ("cv_w", "sc_w"):
        small_red[n] = lax.dynamic_slice_in_dim(small_red[n], chip * 128, 128, axis=2)

    g_mix0["w_in"] = g_mix0["w_in"] + tok[0, 0].astype(BF16)
    names_b, part_b = pair_sums("0b", shard_major(g_mix0))
    hrb, tok = _ici_start("scatter", part_b, [part_shapes[n] for n in names_b], "rs_start0b")
    bufs = {n: _sum_chips(part1[k], recv1[k], place, 1, None, f"rs_sum1_{n}", tok) for k, n in enumerate(names1)}
    for k, n in enumerate(names_a):
        bufs[n] = _sum_chips(part_a[k], recv_a[k], place, 0, bufs[n], f"rs_sum0_{n}", tok)
    shared = dict(zip(names_a, _rs_share([bufs[n] for n in names_a], "rs_share_a")))
    upd = {}
    for n in names_a:
        red = shared[n].reshape(W[n].shape)
        upd[n] = [red] + _adamw(W[n], red, M[n], V[n], f"adamw_{n}")
    two_d = lambda t: t[None] if t.ndim == 1 else t
    small_upd = _adamw_small(*([two_d(t[n]) for n in SMALL] for t in (W, small_red, M, V)), "adamw_small")
    for n, d, mo, vo in zip(SMALL, *small_upd):
        upd[n] = [t.reshape(W[n].shape) for t in (small_red[n], d, mo, vo)]

    _, recv_b = _ici_wait(hrb, upd[names_a[-1]][1], "rs_wait0b")
    for k, n in enumerate(names_b):
        bufs[n] = _sum_chips(part_b[k], recv_b[k], place, 0, bufs[n], f"rs_sum0_{n}", tok)
    shared = dict(zip(names_b, _rs_share([bufs[n] for n in names_b], "rs_share_b")))
    for n in names_b:
        red = shared[n].reshape(Wt[n].shape)
        upd[n] = [T_(t) for t in [red] + _adamw(Wt[n], red, Mt[n], Vt[n], f"adamw_{n}")]

    out = [loss_out, dx[None]]
    for k in range(4):
        out += [upd[n][k] for n in ORDER]
    return tuple(out)
```

```python
import functools
import math

import jax
import jax.numpy as jnp
from jax import lax
from jax.experimental import pallas as pl
from jax.experimental.pallas import tpu as pltpu

F32 = jnp.float32
BF16 = jnp.bfloat16

D_MODEL = 1024
DEPTH = 2
HALF = 512
SG_CHUNK = 128
SG_GROUPS = 4
CV_KERNEL = 31
HEAD_DIM = 64
N_Q_HEADS = 8
N_KV_HEADS = 2
Q_PER_KV = N_Q_HEADS // N_KV_HEADS
WINDOW = 128
ROPE_THETA = 10000.0
SC_KERNEL = 3
N_BRANCH = 4
D_FF = 2816
EPS = 1e-6
N_CHIPS = 4
N_DEV = 8

MIX_W = 4352
GATE_W = N_BRANCH * D_MODEL
PROJ_PAD = 2 * MIX_W
W_IN_SHARD = 2112
GU_SHARD = 1408
HALO = 128
CV_PAD = 32

ADAM_LR = 0.001
ADAM_B1 = 0.9
ADAM_B2 = 0.999
ADAM_EPS = 1e-08
ADAM_WD = 0.01
ADAM_STEP = 10

VMEM_LIMIT = 56 * 1024 * 1024
INV_SQRT2 = 1.0 / math.sqrt(2.0)
INV_SQRT_2PI = 1.0 / math.sqrt(2.0 * math.pi)
NEG_BIG = -1e30
MESH = pl.DeviceIdType.MESH

C_ZA, C_ZB, C_Q, C_K, C_V, C_ZD = 0, 1024, 2048, 2560, 2688, 2816


def _params(*sem):
    return pltpu.CompilerParams(dimension_semantics=sem, vmem_limit_bytes=VMEM_LIMIT)


def _sig(v):
    return 1.0 / (1.0 + jnp.exp(-v))


def _dot(a, b):
    return jnp.dot(a, b, preferred_element_type=F32)


def _dot_nt(a, b):
    return lax.dot_general(a, b, (((1,), (1,)), ((), ())), preferred_element_type=F32)


def _dot_tn(a, b):
    return lax.dot_general(a, b, (((0,), (0,)), ((), ())), preferred_element_type=F32)


def _full(shape):
    nd = len(shape)
    return pl.BlockSpec(shape, lambda *_: (0,) * nd)


def _rms_mm(x, g, w, tm, tn, name):
    T = x.shape[0]
    transposed = w.ndim == 2
    if transposed:
        N = w.shape[0]
        wspec = pl.BlockSpec((tn, D_MODEL), lambda i, j: (j, 0))
    else:
        tn = w.shape[2]
        N = w.shape[0] * tn
        wspec = pl.BlockSpec((None, D_MODEL, tn), lambda i, j: (j, 0, 0))

    def body(x_ref, g_ref, w_ref, o_ref, xn_ref):
        @pl.when(pl.program_id(1) == 0)
        def _():
            xv = x_ref[...]
            r = lax.rsqrt(jnp.mean(xv * xv, axis=-1, keepdims=True) + EPS)
            xn_ref[...] = (xv * r * g_ref[...]).astype(BF16)

        o_ref[...] = (_dot_nt if transposed else _dot)(xn_ref[...], w_ref[...]).astype(BF16)

    return pl.pallas_call(
        body, name=name, grid=(T // tm, N // tn),
        in_specs=[pl.BlockSpec((tm, D_MODEL), lambda i, j: (i, 0)), _full((1, D_MODEL)), wspec],
        out_specs=[pl.BlockSpec((tm, tn), lambda i, j: (i, j)), pl.BlockSpec((tm, D_MODEL), lambda i, j: (i, 0))],
        out_shape=[jax.ShapeDtypeStruct((T, N), BF16), jax.ShapeDtypeStruct((T, D_MODEL), BF16)],
        compiler_params=_params("parallel", "arbitrary"),
    )(x, g, w)


def _merge_fwd(x, ys, proj, wb, wo, tm, name):
    T = x.shape[0]

    def body(x_ref, ys_ref, zg_ref, wb_ref, wo_ref, xo_ref, mg_ref):
        merged = None
        for n in range(N_BRANCH):
            yn = ys_ref[:, n * HALF:(n + 1) * HALF]
            br = jnp.concatenate([_dot(yn, wb_ref[s, n]) for s in range(N_CHIPS)], axis=1)
            t = _sig(zg_ref[:, n * D_MODEL:(n + 1) * D_MODEL].astype(F32)) * br
            merged = t if merged is None else merged + t
        mb = merged.astype(BF16)
        mg_ref[...] = mb
        xo_ref[...] = x_ref[...] + _dot(mb, wo_ref[...])

    return pl.pallas_call(
        body, name=name, grid=(T // tm,),
        in_specs=[pl.BlockSpec((tm, D_MODEL), lambda i: (i, 0)), pl.BlockSpec((tm, N_BRANCH * HALF), lambda i: (i, 0)),
                  pl.BlockSpec((tm, GATE_W), lambda i: (i, 0)), _full(wb.shape), _full(wo.shape)],
        out_specs=[pl.BlockSpec((tm, D_MODEL), lambda i: (i, 0)), pl.BlockSpec((tm, D_MODEL), lambda i: (i, 0))],
        out_shape=[jax.ShapeDtypeStruct((T, D_MODEL), F32), jax.ShapeDtypeStruct((T, D_MODEL), BF16)],
        compiler_params=_params("parallel"),
    )(x, ys, proj, wb, wo)


def _ffn_down(xm, gu, wd, tm, name):
    T = xm.shape[0]

    def body(x_ref, gu_ref, wd_ref, o_ref):
        g = gu_ref[:, :D_FF].astype(F32)
        u = gu_ref[:, D_FF:].astype(F32)
        act = (g * _sig(g) * u).astype(BF16)
        o_ref[...] = x_ref[...] + _dot(act, wd_ref[...])

    return pl.pallas_call(
        body, name=name, grid=(T // tm,),
        in_specs=[pl.BlockSpec((tm, D_MODEL), lambda i: (i, 0)), pl.BlockSpec((tm, 2 * D_FF), lambda i: (i, 0)), _full(wd.shape)],
        out_specs=pl.BlockSpec((tm, D_MODEL), lambda i: (i, 0)),
        out_shape=jax.ShapeDtypeStruct((T, D_MODEL), F32),
        compiler_params=_params("parallel"),
    )(xm, gu, wd)


def _final_loss(x, g, tgt, tm, name):
    T = x.shape[0]

    def body(x_ref, g_ref, t_ref, dx_ref, dg_ref, ls_ref):
        @pl.when(pl.program_id(0) == 0)
        def _():
            dg_ref[...] = jnp.zeros_like(dg_ref)
            ls_ref[...] = jnp.zeros_like(ls_ref)

        xv = x_ref[...]
        gv = g_ref[...]
        r = lax.rsqrt(jnp.mean(xv * xv, axis=-1, keepdims=True) + EPS)
        xh = xv * r
        diff = xh * gv - t_ref[...]
        ls_ref[...] += jnp.full(ls_ref.shape, 0.5 / D_MODEL, F32) * jnp.sum(diff * diff)
        dy = diff * (1.0 / D_MODEL)
        dxh = dy * gv
        dx_ref[...] = r * (dxh - xh * jnp.mean(dxh * xh, axis=-1, keepdims=True))
        dg_ref[...] += jnp.sum(dy * xh, axis=0, keepdims=True)

    return pl.pallas_call(
        body, name=name, grid=(T // tm,),
        in_specs=[pl.BlockSpec((tm, D_MODEL), lambda i: (i, 0)), _full((1, D_MODEL)), pl.BlockSpec((tm, D_MODEL), lambda i: (i, 0))],
        out_specs=[pl.BlockSpec((tm, D_MODEL), lambda i: (i, 0)), _full((1, D_MODEL)), _full((1, 128))],
        out_shape=[jax.ShapeDtypeStruct((T, D_MODEL), F32), jax.ShapeDtypeStruct((1, D_MODEL), F32), jax.ShapeDtypeStruct((1, 128), F32)],
        compiler_params=_params("arbitrary"),
    )(x, g, tgt)


def _swiglu_bwd(dx, gu, wd, tm, name, after):
    T = dx.shape[0]

    def body(dx_ref, gu_ref, wd_ref, after_ref, dgu_ref, act_ref):
        del after_ref
        dact = _dot_nt(dx_ref[...].astype(BF16), wd_ref[...])
        g = gu_ref[:, :D_FF].astype(F32)
        u = gu_ref[:, D_FF:].astype(F32)
        s = _sig(g)
        silu = g * s
        act_ref[...] = (silu * u).astype(BF16)
        dgu_ref[:, :D_FF] = (dact * u * (s + silu * (1.0 - s))).astype(BF16)
        dgu_ref[:, D_FF:] = (dact * silu).astype(BF16)

    return pl.pallas_call(
        body, name=name, grid=(T // tm,),
        in_specs=[pl.BlockSpec((tm, D_MODEL), lambda i: (i, 0)), pl.BlockSpec((tm, 2 * D_FF), lambda i: (i, 0)), _full(wd.shape),
                  pl.BlockSpec(memory_space=pl.ANY)],
        out_specs=[pl.BlockSpec((tm, 2 * D_FF), lambda i: (i, 0)), pl.BlockSpec((tm, D_FF), lambda i: (i, 0))],
        out_shape=[jax.ShapeDtypeStruct((T, 2 * D_FF), BF16), jax.ShapeDtypeStruct((T, D_FF), BF16)],
        compiler_params=_params("parallel"),
    )(dx, gu, wd, after)


def _mm_tn(a, b, grid, a_block, a_map, b_block, b_map, o_shape, o_block, o_map, name, col_split=1):
    gk = grid[2]
    tm = [d for d in a_block if d is not None][-1]
    tn = [d for d in b_block if d is not None][-1]

    def body(a_ref, b_ref, o_ref, acc_ref):
        k = pl.program_id(2)
        p = _dot_tn(a_ref[...].astype(BF16), b_ref[...].astype(BF16))

        @pl.when(k == 0)
        def _():
            acc_ref[...] = p

        @pl.when(k > 0)
        def _():
            acc_ref[...] += p

        @pl.when(k == gk - 1)
        def _():
            if col_split == 1:
                o_ref[...] = acc_ref[...].astype(o_ref.dtype)
            else:
                w = tn // col_split
                for s in range(col_split):
                    o_ref[s] = acc_ref[:, s * w:(s + 1) * w].astype(o_ref.dtype)

    return pl.pallas_call(
        body, name=name, grid=grid,
        in_specs=[pl.BlockSpec(a_block, a_map), pl.BlockSpec(b_block, b_map)],
        out_specs=pl.BlockSpec(o_block, o_map),
        out_shape=jax.ShapeDtypeStruct(o_shape, BF16),
        scratch_shapes=[pltpu.VMEM((tm, tn), F32)],
        compiler_params=_params("parallel", "parallel", "arbitrary"),
    )(a, b)


def _mm_nt_rmsbwd(a, w, x, g, dres, tm, tk, name):
    T = x.shape[0]
    transposed = w.ndim == 2
    if transposed:
        gk = w.shape[0] // tk
        wspec = pl.BlockSpec((tk, D_MODEL), lambda i, k: (k, 0))
    else:
        tk = w.shape[2]
        gk = w.shape[0]
        wspec = pl.BlockSpec((None, D_MODEL, tk), lambda i, k: (k, 0, 0))

    def body(a_ref, w_ref, x_ref, g_ref, r_ref, dx_ref, dg_ref, acc_ref):
        i, k = pl.program_id(0), pl.program_id(1)
        p = (_dot if transposed else _dot_nt)(a_ref[...], w_ref[...])

        @pl.when(k == 0)
        def _():
            acc_ref[...] = p

        @pl.when(k > 0)
        def _():
            acc_ref[...] += p

        @pl.when(jnp.logical_and(i == 0, k == 0))
        def _():
            dg_ref[...] = jnp.zeros_like(dg_ref)

        @pl.when(k == gk - 1)
        def _():
            dh = acc_ref[...]
            xv = x_ref[...]
            r = lax.rsqrt(jnp.mean(xv * xv, axis=-1, keepdims=True) + EPS)
            xh = xv * r
            dxh = dh * g_ref[...]
            dx_ref[...] = r_ref[...] + r * (dxh - xh * jnp.mean(dxh * xh, axis=-1, keepdims=True))
            dg_ref[...] += jnp.sum(dh * xh, axis=0, keepdims=True)

    return pl.pallas_call(
        body, name=name, grid=(T // tm, gk),
        in_specs=[pl.BlockSpec((tm, tk), lambda i, k: (i, k)), wspec, pl.BlockSpec((tm, D_MODEL), lambda i, k: (i, 0)),
                  _full((1, D_MODEL)), pl.BlockSpec((tm, D_MODEL), lambda i, k: (i, 0))],
        out_specs=[pl.BlockSpec((tm, D_MODEL), lambda i, k: (i, 0)), _full((1, D_MODEL))],
        out_shape=[jax.ShapeDtypeStruct((T, D_MODEL), F32), jax.ShapeDtypeStruct((1, D_MODEL), F32)],
        scratch_shapes=[pltpu.VMEM((tm, D_MODEL), F32)],
        compiler_params=_params("arbitrary", "arbitrary"),
    )(a, w, x, g, dres)


def _merge_bwd(dxm, ys, proj, wb, wo, tm, name):
    T = dxm.shape[0]

    def body(dx_ref, ys_ref, zg_ref, wb_ref, wo_ref, dys_ref, dbr_ref, dp_ref):
        dmerged = _dot_nt(dx_ref[...].astype(BF16), wo_ref[...])
        for n in range(N_BRANCH):
            yn = ys_ref[:, n * HALF:(n + 1) * HALF]
            br = jnp.concatenate([_dot(yn, wb_ref[s, n]) for s in range(N_CHIPS)], axis=1)
            gt = _sig(zg_ref[:, n * D_MODEL:(n + 1) * D_MODEL].astype(F32))
            dbr = (gt * dmerged).astype(BF16)
            dbr_ref[:, n * D_MODEL:(n + 1) * D_MODEL] = dbr
            dp_ref[:, n * D_MODEL:(n + 1) * D_MODEL] = (dmerged * br * gt * (1.0 - gt)).astype(BF16)
            dy = None
            for s in range(N_CHIPS):
                t = _dot_nt(dbr[:, s * 256:(s + 1) * 256], wb_ref[s, n])
                dy = t if dy is None else dy + t
            dys_ref[:, n * HALF:(n + 1) * HALF] = dy.astype(BF16)
        dp_ref[:, GATE_W:] = jnp.zeros((tm, MIX_W - GATE_W), BF16)

    return pl.pallas_call(
        body, name=name, grid=(T // tm,),
        in_specs=[pl.BlockSpec((tm, D_MODEL), lambda i: (i, 0)), pl.BlockSpec((tm, N_BRANCH * HALF), lambda i: (i, 0)),
                  pl.BlockSpec((tm, GATE_W), lambda i: (i, 0)), _full(wb.shape), _full(wo.shape)],
        out_specs=[pl.BlockSpec((tm, N_BRANCH * HALF), lambda i: (i, 0)), pl.BlockSpec((tm, GATE_W), lambda i: (i, 0)),
                   pl.BlockSpec((tm, MIX_W), lambda i: (i, 0))],
        out_shape=[jax.ShapeDtypeStruct((T, N_BRANCH * HALF), BF16), jax.ShapeDtypeStruct((T, GATE_W), BF16),
                   jax.ShapeDtypeStruct((T, PROJ_PAD), BF16)],
        compiler_params=_params("parallel"),
    )(dxm, ys, proj, wb, wo)


def _gelu(v):
    return 0.5 * v * (1.0 + lax.erf(v * INV_SQRT2))


def _gelu_grad(v):
    return 0.5 * (1.0 + lax.erf(v * INV_SQRT2)) + v * jnp.exp(-0.5 * v * v) * INV_SQRT_2PI


def _rot_half(t):
    w = t.shape[1]
    lane = lax.broadcasted_iota(jnp.int32, t.shape, 1)
    return jnp.where((lane % HEAD_DIM) < HEAD_DIM // 2, pltpu.roll(t, w - HEAD_DIM // 2, 1), pltpu.roll(t, HEAD_DIM // 2, 1))


def _rope(t, cos, sin_signed):
    return t * cos + _rot_half(t) * sin_signed


def _rope_t(d, cos, sin_signed):
    return d * cos + _rot_half(d * sin_signed)


def _ln_fwd(v, g, b):
    mu = jnp.mean(v, axis=-1, keepdims=True)
    vc = v - mu
    r = lax.rsqrt(jnp.mean(vc * vc, axis=-1, keepdims=True) + EPS)
    vh = vc * r
    return vh * g + b, vh, r


def _ln_bwd(dn, vh, r, g):
    dvh = dn * g
    return r * (dvh - jnp.mean(dvh, axis=-1, keepdims=True) - vh * jnp.mean(dvh * vh, axis=-1, keepdims=True))


def _sublane_shifts(sh_ref, rows):
    for b in range(1, 8):
        sh_ref[b, 0:rows - 8, :] = sh_ref[0, pl.ds(b, rows - 8), :]


def _tap(sh_ref, off, n):
    return sh_ref[off % 8, pl.ds(off - off % 8, n), :]


def _tril_mask():
    return lax.broadcasted_iota(jnp.int32, (SG_CHUNK, SG_CHUNK), 0) >= lax.broadcasted_iota(jnp.int32, (SG_CHUNK, SG_CHUNK), 1)


def _band_masks():
    shape = (Q_PER_KV * WINDOW, 2 * WINDOW)
    row = lax.broadcasted_iota(jnp.int32, shape, 0) % WINDOW
    col = lax.broadcasted_iota(jnp.int32, shape, 1)
    band = (col > row) & (col <= row + WINDOW)
    return band, band & (col >= WINDOW)


def _attn_probs(qs, kh, sink_col, valid):
    s = jnp.where(valid, _dot_nt(qs, kh) * (HEAD_DIM ** -0.5), NEG_BIG)
    m = jnp.maximum(jnp.max(s, axis=-1, keepdims=True), sink_col)
    p = jnp.exp(s - m)
    es = jnp.exp(sink_col - m)
    inv = 1.0 / (jnp.sum(p, axis=-1, keepdims=True) + es)
    return p * inv, es * inv


def _sink_col(sinks_ref, h):
    return jnp.concatenate([jnp.broadcast_to(sinks_ref[:, h * Q_PER_KV + g:h * Q_PER_KV + g + 1], (WINDOW, 1))
                            for g in range(Q_PER_KV)], axis=0)


def _mixer_in_specs(TB, nb):
    r = TB // HALO
    last = nb * r - 1
    cur = pl.BlockSpec((TB, MIX_W), lambda i: (i, 1))
    prev = pl.BlockSpec((HALO, MIX_W), lambda i: (jnp.maximum(i * r - 1, 0), 1))
    nxt = pl.BlockSpec((HALO, MIX_W), lambda i: (jnp.minimum((i + 1) * r, last), 1))
    tcur = pl.BlockSpec((TB, 128), lambda i: (i, 0))
    tprev = pl.BlockSpec((HALO, 128), lambda i: (jnp.maximum(i * r - 1, 0), 0))
    tnxt = pl.BlockSpec((HALO, 128), lambda i: (jnp.minimum((i + 1) * r, last), 0))
    return cur, prev, nxt, tcur, tprev, tnxt


def _mixer_param_specs():
    return [_full((1, HALF)), _full((1, HALF)), _full((SG_GROUPS, SG_CHUNK, SG_CHUNK)), _full((SG_CHUNK, 128)),
            _full((32, HALF)), _full((1, HALF)), _full((1, HALF)), _full((1, HALF)), _full((1, 128)), _full((8, HALF))]


def _mixers_fwd(proj, cos_t, sin_t, mp, TB, name):
    T = proj.shape[0]
    nb = T // TB
    r = TB // HALO
    cur, prev, _, tcur, tprev, _ = _mixer_in_specs(TB, nb)

    def body(zc_ref, zp_ref, cc_ref, sc_ref, cp_ref, sp_ref,
             lg_ref, lb_ref, sgw_ref, sgb_ref, cvw_ref, cvb_ref, cvg_ref, cvbb_ref, sinks_ref, scw_ref,
             ys_ref, scr_ref, k_ref, v_ref, sh_ref):
        i = pl.program_id(0)
        pm = (i > 0).astype(F32)

        def colsE(c0, c1):
            return jnp.concatenate([zp_ref[:, c0:c1].astype(F32) * pm, zc_ref[:, c0:c1].astype(F32)], axis=0)

        a = _gelu(zc_ref[:, C_ZA:C_ZA + 2 * HALF].astype(F32))
        u = a[:, :HALF]
        vn, _, _ = _ln_fwd(a[:, HALF:], lg_ref[...], lb_ref[...])
        vnb = vn.astype(BF16)
        tril = _tril_mask()
        chunks = [slice(ci * SG_CHUNK, (ci + 1) * SG_CHUNK) for ci in range(r)]
        for g in range(SG_GROUPS):
            cols = slice(g * 128, (g + 1) * 128)
            wt = jnp.where(tril, sgw_ref[g], 0.0).astype(BF16)
            mixed = _dot(wt, jnp.concatenate([vnb[rows, cols] for rows in chunks], axis=1)) + sgb_ref[:, g:g + 1]
            for ci, rows in enumerate(chunks):
                ys_ref[rows, cols] = (u[rows, cols] * mixed[:, ci * 128:(ci + 1) * 128]).astype(BF16)

        def colsB(c0, c1):
            return jnp.concatenate([zp_ref[HALO - CV_PAD:, c0:c1].astype(F32) * pm, zc_ref[:, c0:c1].astype(F32)], axis=0)

        sh_ref[0] = colsB(C_ZB, C_ZB + HALF) * _sig(colsB(C_ZB + HALF, C_ZB + 2 * HALF))
        _sublane_shifts(sh_ref, TB + CV_PAD)
        c = jnp.broadcast_to(cvb_ref[...], (TB, HALF))
        for k in range(CV_KERNEL):
            c = c + cvw_ref[k:k + 1, :] * _tap(sh_ref, CV_PAD - (CV_KERNEL - 1) + k, TB)
        n, _, _ = _ln_fwd(c, cvg_ref[...], cvbb_ref[...])
        ys_ref[:, HALF:2 * HALF] = (n * _sig(n)).astype(BF16)

        zd = colsE(C_ZD + HALF, C_ZD + 3 * HALF)
        scr_ref[...] = zd[:, :HALF] * zd[:, HALF:]
        cv = None
        for k in range(SC_KERNEL):
            t = scw_ref[k:k + 1, :] * scr_ref[pl.ds(HALO - (SC_KERNEL - 1) + k, TB), :]
            cv = t if cv is None else cv + t
        ys_ref[:, 3 * HALF:4 * HALF] = (zc_ref[:, C_ZD:C_ZD + HALF].astype(F32) * cv).astype(BF16)

        cosE = jnp.concatenate([cp_ref[...], cc_ref[...]], axis=0)
        sinE = jnp.concatenate([sp_ref[...], sc_ref[...]], axis=0)
        k_ref[...] = _rope(colsE(C_K, C_K + 128), cosE, sinE).astype(BF16)
        v_ref[...] = colsE(C_V, C_V + 128).astype(BF16)
        cosC, sinC = cc_ref[...], sc_ref[...]
        q = jnp.concatenate([_rope(zc_ref[:, C_Q + 128 * j:C_Q + 128 * (j + 1)].astype(F32), cosC, sinC)
                             for j in range(4)], axis=1).astype(BF16)
        in_band, in_band_cur = _band_masks()
        sink_cols = [_sink_col(sinks_ref, h) for h in range(N_KV_HEADS)]
        for qb in range(r):
            valid = in_band if qb else in_band_cur | (in_band & (i > 0))
            for h in range(N_KV_HEADS):
                hc = slice(h * HEAD_DIM, (h + 1) * HEAD_DIM)
                kh = k_ref[qb * WINDOW:qb * WINDOW + 2 * WINDOW, hc]
                vh = v_ref[qb * WINDOW:qb * WINDOW + 2 * WINDOW, hc]
                qs = jnp.concatenate([q[qb * WINDOW:(qb + 1) * WINDOW, (h * Q_PER_KV + g) * HEAD_DIM:(h * Q_PER_KV + g + 1) * HEAD_DIM]
                                      for g in range(Q_PER_KV)], axis=0)
                probs, _ = _attn_probs(qs, kh, sink_cols[h], valid)
                o = _dot(probs.astype(BF16), vh)
                for g in range(Q_PER_KV):
                    c0 = 2 * HALF + (h * Q_PER_KV + g) * HEAD_DIM
                    ys_ref[qb * WINDOW:(qb + 1) * WINDOW, c0:c0 + HEAD_DIM] = o[g * WINDOW:(g + 1) * WINDOW].astype(BF16)

    return pl.pallas_call(
        body, name=name, grid=(nb,),
        in_specs=[cur, prev, tcur, tcur, tprev, tprev] + _mixer_param_specs(),
        out_specs=pl.BlockSpec((TB, 4 * HALF), lambda i: (i, 0)),
        out_shape=jax.ShapeDtypeStruct((T, 4 * HALF), BF16),
        scratch_shapes=[pltpu.VMEM((TB + HALO, HALF), F32), pltpu.VMEM((TB + HALO, 128), BF16), pltpu.VMEM((TB + HALO, 128), BF16),
                        pltpu.VMEM((8, TB + CV_PAD, HALF), F32)],
        compiler_params=_params("parallel"),
    )(proj, proj, cos_t, sin_t, cos_t, sin_t, *mp)


def _mixers_bwd(proj, dys, dproj, cos_t, sin_t, mp, TB, name):
    T = proj.shape[0]
    nb = T // TB
    r = TB // HALO
    RE = TB + 2 * HALO
    RC = TB + HALO
    cur, prev, nxt, tcur, tprev, tnxt = _mixer_in_specs(TB, nb)
    dcur = pl.BlockSpec((TB, 4 * HALF), lambda i: (i, 0))
    dnxt = pl.BlockSpec((HALO, 4 * HALF), lambda i: (jnp.minimum((i + 1) * r, nb * r - 1), 0))

    def body(zc_ref, zp_ref, zn_ref, dyc_ref, dyn_ref, cc_ref, sc_ref, cp_ref, sp_ref, cn_ref, sn_ref,
             lg_ref, lb_ref, sgw_ref, sgb_ref, cvw_ref, cvb_ref, cvg_ref, cvbb_ref, sinks_ref, scw_ref, dp_in_ref,
             dz_ref, dlg_ref, dlb_ref, dsgw_ref, dsgb_ref, dcvw_ref, dcvb_ref, dcvg_ref, dcvbb_ref, dsink_ref, dscw_ref,
             scr_ref, scr2_ref, k_ref, v_ref, dk_ref, dv_ref, dq_ref, sh_ref, sh2_ref):
        del dp_in_ref
        i = pl.program_id(0)
        pm = (i > 0).astype(F32)
        nm = (i < nb - 1).astype(F32)

        @pl.when(i == 0)
        def _():
            for ref in (dlg_ref, dlb_ref, dsgw_ref, dsgb_ref, dcvw_ref, dcvb_ref, dcvg_ref, dcvbb_ref, dsink_ref, dscw_ref):
                ref[...] = jnp.zeros_like(ref)

        def colsE(c0, c1):
            return jnp.concatenate([zp_ref[:, c0:c1].astype(F32) * pm, zc_ref[:, c0:c1].astype(F32),
                                    zn_ref[:, c0:c1].astype(F32)], axis=0)

        def colsC(c0, c1):
            return jnp.concatenate([zc_ref[:, c0:c1].astype(F32), zn_ref[:, c0:c1].astype(F32)], axis=0)

        def dyC(c0, c1):
            return jnp.concatenate([dyc_ref[:, c0:c1].astype(F32), dyn_ref[:, c0:c1].astype(F32) * nm], axis=0)

        za = zc_ref[:, C_ZA:C_ZA + 2 * HALF].astype(F32)
        a = _gelu(za)
        u = a[:, :HALF]
        lg = lg_ref[...]
        vn, vh, rs = _ln_fwd(a[:, HALF:], lg, lb_ref[...])
        vnb = vn.astype(BF16)
        dya = dyc_ref[:, 0:HALF].astype(F32)
        tril = _tril_mask()
        lane128 = lax.broadcasted_iota(jnp.int32, (SG_CHUNK, 128), 1)
        chunks = [slice(ci * SG_CHUNK, (ci + 1) * SG_CHUNK) for ci in range(r)]
        side = lambda t, cols: jnp.concatenate([t[rows, cols] for rows in chunks], axis=1)
        for g in range(SG_GROUPS):
            cols = slice(g * 128, (g + 1) * 128)
            wt = jnp.where(tril, sgw_ref[g], 0.0).astype(BF16)
            vb = side(vnb, cols)
            dy_blk = side(dya, cols)
            du_g = dy_blk * (_dot(wt, vb) + sgb_ref[:, g:g + 1])
            dmix = dy_blk * side(u, cols)
            dmb = dmix.astype(BF16)
            dvn_g = _dot_tn(wt, dmb)
            dsgw_ref[g] += jnp.where(tril, _dot_nt(dmb, vb), 0.0)
            dsgb_ref[...] += jnp.where(lane128 == g, jnp.sum(dmix, axis=1, keepdims=True), 0.0)
            for ci, rows in enumerate(chunks):
                scr_ref[rows, cols] = du_g[:, ci * 128:(ci + 1) * 128]
                scr2_ref[rows, cols] = dvn_g[:, ci * 128:(ci + 1) * 128]
        du, dvn = scr_ref[0:TB, :], scr2_ref[0:TB, :]
        dlg_ref[...] += jnp.sum(dvn * vh, axis=0, keepdims=True)
        dlb_ref[...] += jnp.sum(dvn, axis=0, keepdims=True)
        dvv = _ln_bwd(dvn, vh, rs, lg)
        gg = _gelu_grad(za)
        dz_ref[:, C_ZA:C_ZA + HALF] = (du * gg[:, :HALF]).astype(BF16)
        dz_ref[:, C_ZA + HALF:C_ZA + 2 * HALF] = (dvv * gg[:, HALF:]).astype(BF16)

        RB = TB + CV_PAD

        def colsB(c0, c1):
            return jnp.concatenate([zp_ref[HALO - CV_PAD:, c0:c1].astype(F32) * pm, zc_ref[:, c0:c1].astype(F32),
                                    zn_ref[:CV_PAD, c0:c1].astype(F32)], axis=0)

        sh_ref[0] = colsB(C_ZB, C_ZB + HALF) * _sig(colsB(C_ZB + HALF, C_ZB + 2 * HALF))
        _sublane_shifts(sh_ref, RB + CV_PAD)
        c = jnp.broadcast_to(cvb_ref[...], (RB, HALF))
        for k in range(CV_KERNEL):
            c = c + cvw_ref[k:k + 1, :] * _tap(sh_ref, CV_PAD - (CV_KERNEL - 1) + k, RB)
        cvg = cvg_ref[...]
        n, ch, rc = _ln_fwd(c, cvg, cvbb_ref[...])
        sn = _sig(n)
        dyb = jnp.concatenate([dyc_ref[:, HALF:2 * HALF].astype(F32), dyn_ref[:CV_PAD, HALF:2 * HALF].astype(F32) * nm], axis=0)
        dn = dyb * (sn + n * sn * (1.0 - sn))
        dno = dn[:TB]
        dcvg_ref[...] += jnp.sum(dno * ch[:TB], axis=0, keepdims=True)
        dcvbb_ref[...] += jnp.sum(dno, axis=0, keepdims=True)
        dc = _ln_bwd(dn, ch, rc, cvg)
        sh2_ref[0] = dc
        _sublane_shifts(sh2_ref, RB)
        dcvb_ref[...] += jnp.sum(dc[:TB], axis=0, keepdims=True)
        dy0 = None
        for k in range(CV_KERNEL):
            wk = cvw_ref[k:k + 1, :]
            t = wk * _tap(sh2_ref, CV_KERNEL - 1 - k, TB)
            dy0 = t if dy0 is None else dy0 + t
            dcvw_ref[k:k + 1, :] += jnp.sum(dc[:TB] * _tap(sh_ref, CV_PAD - (CV_KERNEL - 1) + k, TB), axis=0, keepdims=True)
        ab = zc_ref[:, C_ZB:C_ZB + HALF].astype(F32)
        sg = _sig(zc_ref[:, C_ZB + HALF:C_ZB + 2 * HALF].astype(F32))
        dz_ref[:, C_ZB:C_ZB + HALF] = (dy0 * sg).astype(BF16)
        dz_ref[:, C_ZB + HALF:C_ZB + 2 * HALF] = (dy0 * ab * sg * (1.0 - sg)).astype(BF16)

        zd = colsE(C_ZD + HALF, C_ZD + 3 * HALF)
        scr_ref[...] = zd[:, :HALF] * zd[:, HALF:]
        dcv = dyC(3 * HALF, 4 * HALF) * colsC(C_ZD, C_ZD + HALF)
        scr2_ref[...] = dcv
        cv = None
        dud = None
        for k in range(SC_KERNEL):
            wk = scw_ref[k:k + 1, :]
            us = scr_ref[pl.ds(HALO - (SC_KERNEL - 1) + k, TB), :]
            t = wk * us
            cv = t if cv is None else cv + t
            t2 = wk * scr2_ref[pl.ds(SC_KERNEL - 1 - k, TB), :]
            dud = t2 if dud is None else dud + t2
            dscw_ref[k:k + 1, :] += jnp.sum(dcv[:TB] * us, axis=0, keepdims=True)
        dz_ref[:, C_ZD:C_ZD + HALF] = (dyc_ref[:, 3 * HALF:4 * HALF].astype(F32) * cv).astype(BF16)
        dz_ref[:, C_ZD + HALF:C_ZD + 2 * HALF] = (dud * zc_ref[:, C_ZD + 2 * HALF:C_ZD + 3 * HALF].astype(F32)).astype(BF16)
        dz_ref[:, C_ZD + 2 * HALF:C_ZD + 3 * HALF] = (dud * zc_ref[:, C_ZD + HALF:C_ZD + 2 * HALF].astype(F32)).astype(BF16)

        cosE = jnp.concatenate([cp_ref[...], cc_ref[...], cn_ref[...]], axis=0)
        sinE = jnp.concatenate([sp_ref[...], sc_ref[...], sn_ref[...]], axis=0)
        k_ref[...] = _rope(colsE(C_K, C_K + 128), cosE, sinE).astype(BF16)
        v_ref[...] = colsE(C_V, C_V + 128).astype(BF16)
        dk_ref[...] = jnp.zeros_like(dk_ref)
        dv_ref[...] = jnp.zeros_like(dv_ref)
        q = jnp.concatenate([_rope(colsC(C_Q + 128 * j, C_Q + 128 * (j + 1)), cosE[HALO:], sinE[HALO:])
                             for j in range(4)], axis=1).astype(BF16)
        dO = dyC(2 * HALF, 3 * HALF).astype(BF16)
        lane_s = lax.broadcasted_iota(jnp.int32, (1, 128), 1)
        in_band, in_band_cur = _band_masks()
        sink_cols = [_sink_col(sinks_ref, h) for h in range(N_KV_HEADS)]
        for qb in range(r + 1):
            valid = in_band if qb else in_band_cur | (in_band & (i > 0))
            rows = slice(qb * WINDOW, (qb + 1) * WINDOW)
            band = slice(qb * WINDOW, qb * WINDOW + 2 * WINDOW)
            for h in range(N_KV_HEADS):
                hc = slice(h * HEAD_DIM, (h + 1) * HEAD_DIM)
                kh = k_ref[band, hc]
                vh_ = v_ref[band, hc]
                heads = [slice((h * Q_PER_KV + g) * HEAD_DIM, (h * Q_PER_KV + g + 1) * HEAD_DIM) for g in range(Q_PER_KV)]
                qs = jnp.concatenate([q[rows, hs] for hs in heads], axis=0)
                dos = jnp.concatenate([dO[rows, hs] for hs in heads], axis=0)
                probs, p_sink = _attn_probs(qs, kh, sink_cols[h], valid)
                dP = _dot_nt(dos, vh_)
                rsum = jnp.sum(probs * dP, axis=-1, keepdims=True)
                dS = (probs * (dP - rsum) * (HEAD_DIM ** -0.5)).astype(BF16)
                dk_ref[band, hc] += _dot_tn(dS, qs)
                dv_ref[band, hc] += _dot_tn(probs.astype(BF16), dos)
                if qb < r:
                    dqs = _dot(dS, kh)
                    dsk = -p_sink * rsum
                    for g in range(Q_PER_KV):
                        dq_ref[rows, heads[g]] = dqs[g * WINDOW:(g + 1) * WINDOW]
                        dsink_ref[...] += jnp.where(lane_s == h * Q_PER_KV + g, jnp.sum(dsk[g * WINDOW:(g + 1) * WINDOW]), 0.0)
        cosC, sinC = cc_ref[...], sc_ref[...]
        for j in range(4):
            dz_ref[:, C_Q + 128 * j:C_Q + 128 * (j + 1)] = _rope_t(dq_ref[:, 128 * j:128 * (j + 1)], cosC, sinC).astype(BF16)
        dz_ref[:, C_K:C_K + 128] = _rope_t(dk_ref[HALO:HALO + TB, :], cosC, sinC).astype(BF16)
        dz_ref[:, C_V:C_V + 128] = dv_ref[HALO:HALO + TB, :].astype(BF16)

    small = [((1, HALF), F32), ((1, HALF), F32), ((SG_GROUPS, SG_CHUNK, SG_CHUNK), F32), ((SG_CHUNK, 128), F32),
             ((32, HALF), F32), ((1, HALF), F32), ((1, HALF), F32), ((1, HALF), F32), ((1, 128), F32), ((8, HALF), F32)]
    outs = pl.pallas_call(
        body, name=name, grid=(nb,),
        in_specs=[cur, prev, nxt, dcur, dnxt, tcur, tcur, tprev, tprev, tnxt, tnxt] + _mixer_param_specs()
                 + [pl.BlockSpec(memory_space=pl.ANY)],
        out_specs=[pl.BlockSpec((TB, MIX_W), lambda i: (i, 1))] + [_full(s) for s, _ in small],
        out_shape=[jax.ShapeDtypeStruct((T, PROJ_PAD), BF16)] + [jax.ShapeDtypeStruct(s, d) for s, d in small],
        scratch_shapes=[pltpu.VMEM((RE, HALF), F32), pltpu.VMEM((RC, HALF), F32), pltpu.VMEM((RE, 128), BF16), pltpu.VMEM((RE, 128), BF16),
                        pltpu.VMEM((RE, 128), F32), pltpu.VMEM((RE, 128), F32), pltpu.VMEM((TB, HALF), F32),
                        pltpu.VMEM((8, TB + 2 * CV_PAD, HALF), F32), pltpu.VMEM((8, TB + CV_PAD, HALF), F32)],
        input_output_aliases={21: 0},
        compiler_params=_params("arbitrary"),
    )(proj, proj, proj, dys, dys, cos_t, sin_t, cos_t, sin_t, cos_t, sin_t, *mp, dproj)
    return outs


def _rope_tables(T):
    pos = jnp.arange(T, dtype=F32)
    inv_freq = 1.0 / (ROPE_THETA ** (jnp.arange(0, HEAD_DIM, 2, dtype=F32) / HEAD_DIM))
    ang = pos[:, None] * inv_freq[None, :]
    cos, sin = jnp.cos(ang), jnp.sin(ang)
    cos_t = jnp.concatenate([cos, cos, cos, cos], axis=1)
    sin_t = jnp.concatenate([-sin, sin, -sin, sin], axis=1)
    return cos_t, sin_t


def _mixer_params(l, sg_ln_g, sg_ln_b, sg_w, sg_b, cv_w, cv_b, cv_ln_g, cv_ln_b, attn_sinks, sc_w):
    sgb_t = jnp.zeros((SG_CHUNK, 128), F32).at[:, :SG_GROUPS].set(sg_b[l].T)
    cvw = jnp.zeros((32, HALF), F32).at[:CV_KERNEL].set(cv_w[l])
    scw = jnp.zeros((8, HALF), F32).at[:SC_KERNEL].set(sc_w[l])
    sinks = jnp.zeros((1, 128), F32).at[0, :N_Q_HEADS].set(attn_sinks[l])
    return [sg_ln_g[l][None], sg_ln_b[l][None], sg_w[l], sgb_t, cvw, cv_b[l][None], cv_ln_g[l][None], cv_ln_b[l][None], sinks, scw]


def _w_in_layout(w_in_g):
    cut = MIX_W - 2 * W_IN_SHARD
    return jnp.concatenate([w_in_g[2][cut:], w_in_g[3], jnp.zeros((MIX_W - GATE_W, D_MODEL), w_in_g.dtype),
                            w_in_g[0], w_in_g[1], w_in_g[2][:cut]], axis=0)


def _w_in_unlayout(dw):
    cut = MIX_W - 2 * W_IN_SHARD
    return jnp.stack([dw[MIX_W:MIX_W + W_IN_SHARD], dw[MIX_W + W_IN_SHARD:MIX_W + 2 * W_IN_SHARD],
                      jnp.concatenate([dw[MIX_W + 2 * W_IN_SHARD:], dw[:W_IN_SHARD - cut]], axis=0),
                      dw[W_IN_SHARD - cut:GATE_W]], axis=0)


def _device_step(x, tgt, norm_mix, norm_ffn, norm_final, mixer_params, w_in_p, wb_g, wo_g, wgu_g, wd_g):
    T = x.shape[0]
    tables = _rope_tables(T)
    saved = []
    for l in range(DEPTH):
        lw = dict(w_in=w_in_p[l], w_branch=wb_g[l], w_out=wo_g[l], w_gate_up=wgu_g[l], w_down=wd_g[l],
                  norm_mix=norm_mix[l][None], norm_ffn=norm_ffn[l][None], mixer=mixer_params[l], after=jnp.zeros((8, 128), F32))
        x, sv = _fwd_layer(l, x, lw, tables)
        saved.append((lw, sv))
    dx, dnf, loss = _final_loss(x, norm_final[None], tgt, 256, "final_loss")
    grads = [None] * DEPTH
    for l in reversed(range(DEPTH)):
        lw, sv = saved[l]
        dxm, g_ffn = _bwd_layer_ffn(l, dx, lw, sv)
        dx, g_mix = _bwd_layer_mix(l, dxm, lw, sv, tables)
        raw = {**g_ffn, **g_mix}
        grads[l] = {**raw, **_small_views(raw)}
    return loss, dx, dnf[0], grads


MIX_BLOCK = 256


def _fwd_layer(l, x, lw, tables):
    return _fwd_layer_rest(l, x, _fwd_layer_mix(l, x, lw, tables), lw)


def _fwd_layer_mix(l, x, lw, tables):
    proj, xn = _rms_mm(x, lw["norm_mix"], lw["w_in"], min(x.shape[0], 1024), 2176, f"proj{l}")
    return proj, xn, _mixers_fwd(proj, *tables, lw["mixer"], MIX_BLOCK, f"mixers_fwd{l}")


def _fwd_layer_rest(l, x, mixed, lw):
    proj, xn, ys = mixed
    TM = min(x.shape[0], 1024)
    xm, merged = _merge_fwd(x, ys, proj, lw["w_branch"], lw["w_out"], 256, f"merge_fwd{l}")
    gu, hn = _rms_mm(xm, lw["norm_ffn"], lw["w_gate_up"], TM, GU_SHARD, f"ffn_up{l}")
    x_out = _ffn_down(xm, gu, lw["w_down"], 256, f"ffn_down{l}")
    return x_out, (x, proj, xn, ys, xm, merged, gu, hn)


def _bwd_layer_ffn(l, dx, lw, sv):
    x_in, proj, xn, ys, xm, merged, gu, hn = sv
    T = dx.shape[0]
    tkk = min(T, 1024)
    gk = T // tkk
    dgu, act = _swiglu_bwd(dx, gu, lw["w_down"], 256, f"swiglu_bwd{l}", lw["after"])
    d_wd = _mm_tn(act, dx, (2, 1, gk), (tkk, D_FF // 2), lambda i, j, k: (k, i), (tkk, D_MODEL), lambda i, j, k: (k, 0),
                  (D_FF, D_MODEL), (D_FF // 2, D_MODEL), lambda i, j, k: (i, 0), f"dw_down{l}")
    d_wgu = _mm_tn(hn, dgu, (1, N_CHIPS, gk), (tkk, D_MODEL), lambda i, j, k: (k, 0), (tkk, GU_SHARD), lambda i, j, k: (k, j),
                   (N_CHIPS, D_MODEL, GU_SHARD), (None, D_MODEL, GU_SHARD), lambda i, j, k: (j, 0, 0), f"dw_gate_up{l}")
    dxm, d_nffn = _mm_nt_rmsbwd(dgu, lw["w_gate_up"], xm, lw["norm_ffn"], dx, min(T, 1024), GU_SHARD, f"ffn_up_bwd{l}")
    dys, dbr, dproj = _merge_bwd(dxm, ys, proj, lw["w_branch"], lw["w_out"], 256, f"merge_bwd{l}")
    d_wo = _mm_tn(merged, dxm, (2, 1, gk), (tkk, 512), lambda i, j, k: (k, i), (tkk, D_MODEL), lambda i, j, k: (k, 0),
                  (D_MODEL, D_MODEL), (512, D_MODEL), lambda i, j, k: (i, 0), f"dw_out{l}")
    d_wb = _mm_tn(ys, dbr, (N_BRANCH, 1, gk), (tkk, HALF), lambda i, j, k: (k, i), (tkk, D_MODEL), lambda i, j, k: (k, i),
                  (N_CHIPS, N_BRANCH, HALF, 256), (N_CHIPS, None, HALF, 256), lambda i, j, k: (0, i, 0, 0), f"dw_branch{l}", col_split=N_CHIPS)
    return (dxm, dys, dproj), dict(w_branch=d_wb, w_out=d_wo, w_gate_up=d_wgu, w_down=d_wd, norm_ffn=d_nffn)


def _bwd_layer_mix(l, carry, lw, sv, tables):
    dxm, dys, dproj = carry
    x_in, proj, xn, ys, xm, merged, gu, hn = sv
    T = dxm.shape[0]
    tkk = min(T, 1024)
    gk = T // tkk
    mb = _mixers_bwd(proj, dys, dproj, *tables, lw["mixer"], MIX_BLOCK, f"mixers_bwd{l}")
    dproj = mb[0]
    d_win = _mm_tn(dproj, xn, (PROJ_PAD // 2176, 1, gk), (tkk, 2176), lambda i, j, k: (k, i), (tkk, D_MODEL), lambda i, j, k: (k, 0),
                   (PROJ_PAD, D_MODEL), (2176, D_MODEL), lambda i, j, k: (i, 0), f"dw_in{l}")
    dx, d_nmix = _mm_nt_rmsbwd(dproj, lw["w_in"], x_in, lw["norm_mix"], dxm, min(T, 1024), 512, f"proj_bwd{l}")
    return dx, dict(w_in=d_win, norm_mix=d_nmix, sg_ln_g=mb[1], sg_ln_b=mb[2], sg_w=mb[3], sg_b=mb[4], cv_w=mb[5], cv_b=mb[6],
                    cv_ln_g=mb[7], cv_ln_b=mb[8], attn_sinks=mb[9], sc_w=mb[10])


ANY = pl.BlockSpec(memory_space=pl.ANY)
BIG = ("w_in", "w_branch", "w_out", "w_gate_up", "w_down")
HALF_SHAPE = {"w_in": (2, W_IN_SHARD // 2, D_MODEL), "w_branch": (2, 1024, 256), "w_out": (2, 128, D_MODEL),
              "w_gate_up": (2, 512, GU_SHARD), "w_down": (2, 352, D_MODEL)}
NB = len(BIG)


def _place():
    x, y, c = lax.axis_index("x"), lax.axis_index("y"), lax.axis_index("c")
    chips = [(1 - x, y), (x, 1 - y), (1 - x, 1 - y)]
    return x, y, c, 2 * x + y, chips, [2 * px + py for px, py in chips]


def _remote(src, dst, ssem, rsem, dev):
    return pltpu.make_async_remote_copy(src_ref=src, dst_ref=dst, send_sem=ssem, recv_sem=rsem, device_id=dev, device_id_type=MESH)


HBM_SPEC = pl.BlockSpec(memory_space=pltpu.HBM)
SEM_SPEC = pl.BlockSpec(memory_space=pltpu.SEMAPHORE)
DATAFLOW = pltpu.SideEffectType.DATAFLOW_SIDE_EFFECTING


def _ici_ends(kind, src, land, j, c, chip, chip_ids):
    if kind == "gather":
        return src.at[c], land.at[chip, c], land.at[chip_ids[j], c]
    return src.at[chip_ids[j]], land.at[chip], land.at[chip_ids[j]]


def _ici_start(kind, srcs, land_shapes, name):
    n = len(srcs)

    def body(*refs):
        src, land = refs[:n], refs[n:2 * n]
        ssem, rsem, token = refs[2 * n], refs[2 * n + 1], refs[-1]
        x, y, c, chip, chips, chip_ids = _place()
        for k in range(n):
            for j in range(3):
                s, d, _ = _ici_ends(kind, src[k], land[k], j, c, chip, chip_ids)
                _remote(s, d, ssem.at[3 * k + j], rsem.at[3 * k + j], (*chips[j], c)).start()
        token[...] = jnp.zeros_like(token)

    sem = pltpu.SemaphoreType.DMA((3 * n,))
    outs = pl.pallas_call(
        body, name=name,
        out_shape=(sem, sem, *[pltpu.HBM(s.shape, s.dtype) for s in srcs], *[pltpu.HBM(sh, BF16) for sh in land_shapes],
                   jax.ShapeDtypeStruct((8, 128), F32)),
        in_specs=[HBM_SPEC] * (2 * n),
        out_specs=(SEM_SPEC, SEM_SPEC, *[HBM_SPEC] * (2 * n), pl.BlockSpec(memory_space=pltpu.VMEM)),
        input_output_aliases={i: 2 + i for i in range(2 * n)},
        compiler_params=pltpu.CompilerParams(has_side_effects=DATAFLOW),
    )(*[pltpu.with_memory_space_constraint(s, pltpu.HBM) for s in srcs],
      *[pltpu.with_memory_space_constraint(lax.empty(sh, BF16), pltpu.HBM) for sh in land_shapes])
    return (kind, outs[0], outs[1], list(outs[2:2 + n]), list(outs[2 + n:2 + 2 * n])), outs[-1]


def _ici_wait(handle, after, name):
    kind, ssem_in, rsem_in, srcs, lands = handle
    n = len(srcs)

    def body(*refs):
        src, land = refs[:n], refs[n:2 * n]
        ssem, rsem = refs[2 * n], refs[2 * n + 1]
        x, y, c, chip, chips, chip_ids = _place()
        for k in range(n):
            for j in range(3):
                s, _, mine = _ici_ends(kind, src[k], land[k], j, c, chip, chip_ids)
                cp = _remote(s, mine, ssem.at[3 * k + j], rsem.at[3 * k + j], (*chips[j], c))
                cp.wait_send()
                cp.wait_recv()

    outs = pl.pallas_call(
        body, name=name, out_shape=[pltpu.HBM(t.shape, t.dtype) for t in srcs + lands],
        in_specs=[HBM_SPEC] * (2 * n) + [SEM_SPEC, SEM_SPEC, ANY], out_specs=[HBM_SPEC] * (2 * n),
        input_output_aliases={i: i for i in range(2 * n)},
        compiler_params=pltpu.CompilerParams(has_side_effects=DATAFLOW),
    )(*srcs, *lands, ssem_in, rsem_in, after)
    return list(outs[:n]), list(outs[n:])


def _ag_pair(shards, lands, name):
    n = len(shards)

    def body(*refs):
        ins, outs = refs[:n], refs[2 * n:3 * n]
        token = refs[3 * n]
        s_fwd, r_fwd, s_own, r_own = refs[3 * n + 1:]
        x, y, c, chip, chips, chip_ids = _place()
        sib = (x, y, 1 - c)
        cps = []
        for k in range(n):
            cp = _remote(ins[k], outs[k].at[chip], s_own.at[k], r_own.at[k], sib)
            cp.start()
            cps.append(cp)
            for j in range(3):
                got = outs[k].at[chip_ids[j], c]
                cp = _remote(got, got, s_fwd.at[k, j], r_fwd.at[k, j], sib)
                cp.start()
                cps.append(cp)
        for k in range(n):
            _remote(ins[k], outs[k].at[chip], s_own.at[k], r_own.at[k], sib).wait_recv()
            for j in range(3):
                got = outs[k].at[chip_ids[j], 1 - c]
                _remote(got, got, s_fwd.at[k, j], r_fwd.at[k, j], sib).wait_recv()
        for cp in cps:
            cp.wait_send()
        token[...] = jnp.zeros_like(token)

    sem, sem1 = pltpu.SemaphoreType.DMA((n, 3)), pltpu.SemaphoreType.DMA((n,))
    outs = pl.pallas_call(
        body, name=name, out_shape=[jax.ShapeDtypeStruct(t.shape, t.dtype) for t in lands] + [jax.ShapeDtypeStruct((8, 128), F32)],
        in_specs=[ANY] * (2 * n), out_specs=[ANY] * n + [pl.BlockSpec(memory_space=pltpu.VMEM)],
        input_output_aliases={n + k: k for k in range(n)},
        scratch_shapes=[sem, sem, sem1, sem1], compiler_params=pltpu.CompilerParams(has_side_effects=True),
    )(*shards, *lands)
    return list(outs[:n]), outs[n]


def _rs_pair(grads, name):
    n_arr = len(grads)

    def body(*refs):
        ins, got = refs[:n_arr], refs[n_arr:2 * n_arr]
        ssem, rsem = refs[2 * n_arr:]
        x, y, c, _, _, _ = _place()
        sib = (x, y, 1 - c)
        sends = []
        for k in reversed(range(n_arr)):
            for q in range(N_CHIPS):
                cp = _remote(ins[k].at[q, 1 - c], got[k].at[q], ssem.at[k, q], rsem.at[k, q], sib)
                cp.start()
                sends.append(cp)
        for k in range(n_arr):
            for q in range(N_CHIPS):
                _remote(got[k].at[q], got[k].at[q], ssem.at[k, q], rsem.at[k, q], sib).wait_recv()
        for cp in sends:
            cp.wait_send()

    shp = [jax.ShapeDtypeStruct((N_CHIPS,) + g.shape[2:], BF16) for g in grads]
    sem = pltpu.SemaphoreType.DMA((n_arr, N_CHIPS))
    outs = pl.pallas_call(
        body, name=name, out_shape=shp, in_specs=[ANY] * n_arr, out_specs=[ANY] * n_arr,
        scratch_shapes=[sem, sem], compiler_params=pltpu.CompilerParams(has_side_effects=True),
    )(*grads)
    return list(outs)


def _rs_share(bufs, name):
    n = len(bufs)

    def body(*refs):
        outs = refs[n:2 * n]
        ssem, rsem = refs[2 * n:]
        x, y, c, _, _, _ = _place()
        sib = (x, y, 1 - c)
        sends = []
        for k in range(n):
            for l in range(DEPTH):
                cp = _remote(outs[k].at[l, c], outs[k].at[l, c], ssem.at[k, l], rsem.at[k, l], sib)
                cp.start()
                sends.append(cp)
        for k in range(n):
            for l in range(DEPTH):
                dst = outs[k].at[l, 1 - c]
                _remote(dst, dst, ssem.at[k, l], rsem.at[k, l], sib).wait_recv()
        for cp in sends:
            cp.wait_send()

    sem = pltpu.SemaphoreType.DMA((n, DEPTH))
    outs = pl.pallas_call(
        body, name=name, out_shape=[jax.ShapeDtypeStruct(b.shape, b.dtype) for b in bufs], in_specs=[ANY] * n, out_specs=[ANY] * n,
        input_output_aliases={k: k for k in range(n)},
        scratch_shapes=[sem, sem], compiler_params=pltpu.CompilerParams(has_side_effects=True),
    )(*bufs)
    return list(outs)


def _piece(src, idx, rows, width=128, align=1, transposed=False):
    return dict(src=src, idx=idx, rows=rows, width=width, align=align, transposed=transposed)


def _all_reduce_pieces(inputs, pieces, out_shapes, writes, name):
    n_in, n_out = len(inputs), len(out_shapes)
    offs, R = [], 0
    for p in pieces:
        R = -(-R // p["align"]) * p["align"]
        offs.append(R)
        R += p["rows"]
    R = -(-R // 8) * 8

    def body(*refs):
        ins, outs, token_ref = refs[:n_in], refs[n_in:n_in + n_out], refs[n_in + n_out]
        pair_ref, chip_ref, sum_ref, ssem, rsem = refs[n_in + n_out + 1:]
        token_ref[...] = jnp.zeros_like(token_ref)
        x, y, c, chip, chips, chip_ids = _place()
        pair_ref[c] = jnp.zeros((R, 128), F32)
        for p, off in zip(pieces, offs):
            v = ins[p["src"]][...].T[p["idx"]] if p["transposed"] else ins[p["src"]][p["idx"]]
            pair_ref[c, off:off + p["rows"], 0:p["width"]] = v
        mine = _remote(pair_ref.at[c], pair_ref.at[c], ssem.at[3], rsem.at[3], (x, y, 1 - c))
        mine.start()
        _remote(pair_ref.at[1 - c], pair_ref.at[1 - c], ssem.at[3], rsem.at[3], (x, y, 1 - c)).wait_recv()
        chip_ref[chip] = pair_ref[0] + pair_ref[1]
        cps = [_remote(chip_ref.at[chip], chip_ref.at[chip], ssem.at[j], rsem.at[j], (*chips[j], c)) for j in range(3)]
        for cp in cps:
            cp.start()
        for j in range(3):
            slot = chip_ref.at[chip_ids[j]]
            _remote(slot, slot, ssem.at[j], rsem.at[j], (*chips[j], c)).wait_recv()
        acc = chip_ref[0]
        for s in range(1, N_CHIPS):
            acc = acc + chip_ref[s]
        sum_ref[...] = acc
        for o, idx, p in writes:
            outs[o][idx] = sum_ref[offs[p]:offs[p] + pieces[p]["rows"], 0:pieces[p]["width"]]
        for cp in cps + [mine]:
            cp.wait_send()

    vm = pl.BlockSpec(memory_space=pltpu.VMEM)
    outs = pl.pallas_call(
        body, name=name, out_shape=[jax.ShapeDtypeStruct(s, F32) for s in out_shapes] + [jax.ShapeDtypeStruct((8, 128), F32)],
        in_specs=[vm] * n_in, out_specs=[vm] * (n_out + 1),
        scratch_shapes=[pltpu.VMEM((2, R, 128), F32), pltpu.VMEM((N_CHIPS, R, 128), F32), pltpu.VMEM((R, 128), F32),
                        pltpu.SemaphoreType.DMA((4,)), pltpu.SemaphoreType.DMA((4,))],
        compiler_params=pltpu.CompilerParams(vmem_limit_bytes=VMEM_LIMIT),
    )(*inputs)
    return list(outs[:n_out]), outs[n_out]


def _lanes(width):
    return [slice(k, min(k + 128, width)) for k in range(0, width, 128)]


def _gather_small_weights(cvw_z, scw_z):
    pieces, writes = [], []
    for i, arr in enumerate((cvw_z, scw_z)):
        for l in range(DEPTH):
            for ln in _lanes(HALF):
                writes.append((i, (l, slice(None), ln), len(pieces)))
                pieces.append(_piece(i, (l, slice(None), ln), arr.shape[1], align=8))
    (cvw, scw), tok = _all_reduce_pieces([cvw_z, scw_z], pieces, [cvw_z.shape, scw_z.shape], writes, "ag_small")
    return cvw, scw, tok


SMALL_RAW = dict(norm_mix=(1, D_MODEL), norm_ffn=(1, D_MODEL), sg_ln_g=(1, HALF), sg_ln_b=(1, HALF), cv_b=(1, HALF), cv_ln_g=(1, HALF),
                 cv_ln_b=(1, HALF))


def _all_reduce_small_grads(raw, d_nfinal, loss):
    names = list(SMALL_RAW) + ["attn_sinks", "sg_b", "sc_w", "cv_w", "sg_w"]
    out_shape = dict(norm_mix=(DEPTH, D_MODEL), norm_ffn=(DEPTH, D_MODEL), sg_ln_g=(DEPTH, HALF), sg_ln_b=(DEPTH, HALF), cv_b=(DEPTH, HALF),
                     cv_ln_g=(DEPTH, HALF), cv_ln_b=(DEPTH, HALF), attn_sinks=(DEPTH, N_Q_HEADS), sg_b=(DEPTH, SG_GROUPS, SG_CHUNK),
                     sc_w=(DEPTH, SC_KERNEL, HALF), cv_w=(DEPTH, CV_KERNEL, HALF), sg_w=(DEPTH, SG_GROUPS, SG_CHUNK, SG_CHUNK))
    inputs, pieces, writes = [], [], []

    def add(src, idx, rows, out, out_idx, **kw):
        writes.append((names.index(out) if out in names else out, out_idx, len(pieces)))
        pieces.append(_piece(src, idx, rows, **kw))

    for l in range(DEPTH):
        row = slice(l, l + 1)
        for n, (_, width) in SMALL_RAW.items():
            inputs.append(raw[l][n])
            for ln in _lanes(width):
                add(len(inputs) - 1, (slice(0, 1), ln), 1, n, (row, ln))
        inputs.append(raw[l]["attn_sinks"])
        add(len(inputs) - 1, (slice(0, 1), slice(0, N_Q_HEADS)), 1, "attn_sinks", (row, slice(None)), width=N_Q_HEADS)
    for l in range(DEPTH):
        inputs.append(raw[l]["sg_b"])
        add(len(inputs) - 1, (slice(0, SG_GROUPS), slice(None)), SG_GROUPS, "sg_b", (l,), align=8, transposed=True)
        inputs.append(raw[l]["sc_w"])
        for ln in _lanes(HALF):
            add(len(inputs) - 1, (slice(0, SC_KERNEL), ln), SC_KERNEL, "sc_w", (l, slice(None), ln), align=8)
        inputs.append(raw[l]["cv_w"])
        for ln in _lanes(HALF):
            add(len(inputs) - 1, (slice(0, CV_KERNEL), ln), CV_KERNEL, "cv_w", (l, slice(None), ln), align=8)
        inputs.append(raw[l]["sg_w"])
        for g in range(SG_GROUPS):
            add(len(inputs) - 1, (g,), SG_CHUNK, "sg_w", (l, g), align=8)
    n_names = len(names)
    inputs.append(d_nfinal)
    for ln in _lanes(D_MODEL):
        add(len(inputs) - 1, (slice(0, 1), ln), 1, n_names, (slice(0, 1), ln))
    inputs.append(loss)
    add(len(inputs) - 1, (slice(0, 1), slice(None)), 1, n_names + 1, (slice(0, 1), slice(None)))
    outs, tok = _all_reduce_pieces(inputs, pieces, [out_shape[n] for n in names] + [(1, D_MODEL), (1, 128)], writes, "ar_small")
    return dict(zip(names, outs[:n_names])), outs[n_names], outs[n_names + 1], tok


def _small_views(raw):
    v = {n: raw[n][0] for n in SMALL_RAW}
    v.update(sg_w=raw["sg_w"], sg_b=raw["sg_b"][:, :SG_GROUPS].T, cv_w=raw["cv_w"][:CV_KERNEL],
             attn_sinks=raw["attn_sinks"][0, :N_Q_HEADS], sc_w=raw["sc_w"][:SC_KERNEL])
    return v


def _row_tile(rows, cols, n_arrays):
    budget = 20 * 1024 * 1024 // (n_arrays * 2 * cols * 4)
    tiles = [t for t in range(16, min(rows, budget) + 1, 16) if rows % t == 0]
    assert tiles, (rows, cols)
    return tiles[-1]


def _add_pairs(g, got, place, name):
    _, _, rows, cols = g.shape
    tr = _row_tile(rows, cols, 3)

    def body(place_ref, a_ref, b_ref, o_ref):
        del place_ref
        o_ref[...] = (a_ref[...].astype(F32) + b_ref[...].astype(F32)).astype(BF16)

    spec = pl.BlockSpec((None, tr, cols), lambda q, i, p: (q, i, 0))
    grid_spec = pltpu.PrefetchScalarGridSpec(
        num_scalar_prefetch=1, grid=(N_CHIPS, rows // tr),
        in_specs=[pl.BlockSpec((None, None, tr, cols), lambda q, i, p: (q, p[1], i, 0)), spec], out_specs=spec)
    return pl.pallas_call(body, name=name, grid_spec=grid_spec, out_shape=jax.ShapeDtypeStruct((N_CHIPS, rows, cols), BF16),
                          compiler_params=_params("parallel", "parallel"))(place, g, got)


def _sum_chips(own, recv, place, l, buf, name, after):
    _, rows, cols = own.shape
    tr = _row_tile(rows, cols, 4)

    def body(place_ref, own_ref, recv_ref, *rest):
        chip = place_ref[0]
        acc = own_ref[...].astype(F32)
        for j in range(1, N_CHIPS):
            acc = acc + recv_ref[lax.rem(chip + j, N_CHIPS)].astype(F32)
        rest[-1][...] = acc

    in_specs = [pl.BlockSpec((None, tr, cols), lambda i, p: (p[0], i, 0)), pl.BlockSpec((N_CHIPS, tr, cols), lambda i, p: (0, i, 0)), ANY]
    args = [place, own, recv, after]
    aliases = {}
    if buf is not None:
        in_specs.append(ANY)
        args.append(buf)
        aliases = {4: 0}
    grid_spec = pltpu.PrefetchScalarGridSpec(
        num_scalar_prefetch=1, grid=(rows // tr,), in_specs=in_specs,
        out_specs=pl.BlockSpec((None, None, tr, cols), lambda i, p: (l, p[1], i, 0)))
    return pl.pallas_call(body, name=name, grid_spec=grid_spec, out_shape=jax.ShapeDtypeStruct((DEPTH, 2, rows, cols), F32),
                          input_output_aliases=aliases, compiler_params=_params("parallel"))(*args)


def _adamw(w, g, m, v, name):
    shape = w.shape
    lead, (rows, cols) = shape[:-2], shape[-2:]
    tr = _row_tile(rows, cols, 7)

    def body(w_ref, g_ref, m_ref, v_ref, d_ref, mo_ref, vo_ref):
        gv = g_ref[...]
        mn = ADAM_B1 * m_ref[...] + (1.0 - ADAM_B1) * gv
        vn = ADAM_B2 * v_ref[...] + (1.0 - ADAM_B2) * (gv * gv)
        m_hat = mn / (1.0 - ADAM_B1 ** ADAM_STEP)
        v_hat = vn / (1.0 - ADAM_B2 ** ADAM_STEP)
        d_ref[...] = -ADAM_LR * (m_hat / (jnp.sqrt(v_hat) + ADAM_EPS) + ADAM_WD * w_ref[...])
        mo_ref[...] = mn
        vo_ref[...] = vn

    spec = pl.BlockSpec((None,) * len(lead) + (tr, cols), lambda *idx: (*idx, 0))
    grid = lead + (rows // tr,)
    return list(pl.pallas_call(body, name=name, grid=grid, in_specs=[spec] * 4, out_specs=[spec] * 3,
                               out_shape=[jax.ShapeDtypeStruct(shape, F32)] * 3,
                               compiler_params=_params(*(["parallel"] * len(grid))))(w, g, m, v))


def _adamw_small(ws, gs, ms, vs, name):
    n = len(ws)

    def body(*refs):
        for i in range(n):
            gv = refs[n + i][...]
            mn = ADAM_B1 * refs[2 * n + i][...] + (1.0 - ADAM_B1) * gv
            vn = ADAM_B2 * refs[3 * n + i][...] + (1.0 - ADAM_B2) * (gv * gv)
            m_hat = mn / (1.0 - ADAM_B1 ** ADAM_STEP)
            v_hat = vn / (1.0 - ADAM_B2 ** ADAM_STEP)
            refs[4 * n + i][...] = -ADAM_LR * (m_hat / (jnp.sqrt(v_hat) + ADAM_EPS) + ADAM_WD * refs[i][...])
            refs[5 * n + i][...] = mn
            refs[6 * n + i][...] = vn

    vm = pl.BlockSpec(memory_space=pltpu.VMEM)
    outs = pl.pallas_call(body, name=name, out_shape=[jax.ShapeDtypeStruct(t.shape, F32) for t in ws] * 3,
                          in_specs=[vm] * (4 * n), out_specs=[vm] * (3 * n),
                          compiler_params=pltpu.CompilerParams(vmem_limit_bytes=VMEM_LIMIT))(*ws, *gs, *ms, *vs)
    return outs[:n], outs[n:2 * n], outs[2 * n:]


SMALL = ("norm_mix", "sg_ln_g", "sg_ln_b", "sg_w", "sg_b", "cv_w", "cv_b", "cv_ln_g", "cv_ln_b", "attn_sinks", "sc_w", "norm_ffn", "norm_final")
ORDER = ("norm_mix", "w_in", "sg_ln_g", "sg_ln_b", "sg_w", "sg_b", "cv_w", "cv_b", "cv_ln_g", "cv_ln_b", "attn_sinks", "sc_w",
         "w_branch", "w_out", "norm_ffn", "w_gate_up", "w_down", "norm_final")


def kernel(x, norm_mix, w_in, sg_ln_g, sg_ln_b, sg_w, sg_b, cv_w, cv_b, cv_ln_g, cv_ln_b, attn_sinks, sc_w, w_branch, w_out, norm_ffn, w_gate_up, w_down, norm_final, loss_target, m_norm_mix, m_w_in, m_sg_ln_g, m_sg_ln_b, m_sg_w, m_sg_b, m_cv_w, m_cv_b, m_cv_ln_g, m_cv_ln_b, m_attn_sinks, m_sc_w, m_w_branch, m_w_out, m_norm_ffn, m_w_gate_up, m_w_down, m_norm_final, v_norm_mix, v_w_in, v_sg_ln_g, v_sg_ln_b, v_sg_w, v_sg_b, v_cv_w, v_cv_b, v_cv_ln_g, v_cv_ln_b, v_attn_sinks, v_sc_w, v_w_branch, v_w_out, v_norm_ffn, v_w_gate_up, v_w_down, v_norm_final):
    W = dict(norm_mix=norm_mix, w_in=w_in, sg_ln_g=sg_ln_g, sg_ln_b=sg_ln_b, sg_w=sg_w, sg_b=sg_b, cv_w=cv_w, cv_b=cv_b, cv_ln_g=cv_ln_g,
             cv_ln_b=cv_ln_b, attn_sinks=attn_sinks, sc_w=sc_w, w_branch=w_branch, w_out=w_out, norm_ffn=norm_ffn, w_gate_up=w_gate_up,
             w_down=w_down, norm_final=norm_final)
    M = dict(norm_mix=m_norm_mix, w_in=m_w_in, sg_ln_g=m_sg_ln_g, sg_ln_b=m_sg_ln_b, sg_w=m_sg_w, sg_b=m_sg_b, cv_w=m_cv_w, cv_b=m_cv_b,
             cv_ln_g=m_cv_ln_g, cv_ln_b=m_cv_ln_b, attn_sinks=m_attn_sinks, sc_w=m_sc_w, w_branch=m_w_branch, w_out=m_w_out,
             norm_ffn=m_norm_ffn, w_gate_up=m_w_gate_up, w_down=m_w_down, norm_final=m_norm_final)
    V = dict(norm_mix=v_norm_mix, w_in=v_w_in, sg_ln_g=v_sg_ln_g, sg_ln_b=v_sg_ln_b, sg_w=v_sg_w, sg_b=v_sg_b, cv_w=v_cv_w, cv_b=v_cv_b,
             cv_ln_g=v_cv_ln_g, cv_ln_b=v_cv_ln_b, attn_sinks=v_attn_sinks, sc_w=v_sc_w, w_branch=v_w_branch, w_out=v_w_out,
             norm_ffn=v_norm_ffn, w_gate_up=v_w_gate_up, w_down=v_w_down, norm_final=v_norm_final)
    mx, my, mc = lax.axis_index("x"), lax.axis_index("y"), lax.axis_index("c")
    chip = 2 * mx + my

    place = jnp.stack([chip, mc]).astype(jnp.int32)
    tables = _rope_tables(x.shape[1])
    land_shapes = [(N_CHIPS,) + HALF_SHAPE[n] for n in BIG]
    part_shapes = {n: (N_CHIPS,) + HALF_SHAPE[n][1:] for n in BIG}

    T_ = lambda t: jnp.swapaxes(t, 1, 2)
    Wt, Mt, Vt = ({**t, "w_in": T_(t["w_in"])} for t in (W, M, V))

    def shards_of(l, tok):
        return [(Wt[n][l] + tok[0, 0]).astype(BF16).reshape(HALF_SHAPE[n]) for n in BIG]

    def finish_gather(tag, handle, after):
        srcs, lands = _ici_wait(handle, after, f"ag_wait{tag}")
        return _ag_pair(srcs, lands, f"ag_pair{tag}")[0]

    def mix_weights(l, g_in):
        return dict(w_in=_w_in_layout(g_in[0].reshape(N_CHIPS, W_IN_SHARD, D_MODEL)), norm_mix=norm_mix[l][None], norm_ffn=norm_ffn[l][None],
                    mixer=_mixer_params(l, sg_ln_g, sg_ln_b, sg_w, sg_b, cvw_full, cv_b, cv_ln_g, cv_ln_b, attn_sinks, scw_full))

    def rest_weights(lw, g_rest):
        G = dict(zip(BIG[1:], g_rest))
        lw.update(w_branch=G["w_branch"].reshape(N_CHIPS, N_BRANCH, HALF, 256), w_out=G["w_out"].reshape(D_MODEL, D_MODEL),
                  w_gate_up=G["w_gate_up"].reshape(N_CHIPS, D_MODEL, GU_SHARD), w_down=G["w_down"].reshape(D_FF, D_MODEL))

    def shard_major(g):
        t = dict(g)
        if "w_in" in t:
            t["w_in"] = _w_in_unlayout(t["w_in"])
        return {n: t[n].reshape((N_CHIPS,) + HALF_SHAPE[n]) for n in BIG if n in t}

    def pair_sums(tag, g):
        names = list(g)
        got = _rs_pair([g[n] for n in names], f"rs_pair{tag}")
        return names, [_add_pairs(g[n], got[k], place, f"rs_add{tag}_{n}") for k, n in enumerate(names)]

    zero_tok = jnp.zeros((8, 128), F32)
    south = (mc == 0).astype(F32)
    cvw_z = lax.dynamic_update_slice(jnp.zeros((DEPTH, CV_KERNEL, HALF), F32), cv_w * south, (0, 0, chip * 128))
    scw_z = lax.dynamic_update_slice(jnp.zeros((DEPTH, SC_KERNEL, HALF), F32), sc_w * south, (0, 0, chip * 128))
    cvw_full, scw_full, tok = _gather_small_weights(cvw_z, scw_z)

    handles = []
    for l in range(DEPTH):
        for tag, sl in (("in", slice(0, 1)), ("rest", slice(1, NB))):
            h, tok = _ici_start("gather", shards_of(l, tok)[sl], land_shapes[sl], f"ag_start{l}{tag}")
            handles.append(h)
    x_l, saved = x[0], []
    for l in range(DEPTH):
        lw = mix_weights(l, finish_gather(f"{l}in", handles[2 * l], x_l if l else tok))
        mixed = _fwd_layer_mix(l, x_l, lw, tables)
        rest_weights(lw, finish_gather(f"{l}rest", handles[2 * l + 1], mixed[2]))
        x_l, sv = _fwd_layer_rest(l, x_l, mixed, lw)
        saved.append((lw, sv))
    (lw0, sv0), (lw1, sv1) = saved
    dx, d_nfinal, loss = _final_loss(x_l, norm_final[None], loss_target[0], 256, "final_loss")

    lw1["after"] = zero_tok
    carry, g_ffn1 = _bwd_layer_ffn(1, dx, lw1, sv1)
    dx, g_mix1 = _bwd_layer_mix(1, carry, lw1, sv1, tables)
    names1, part1 = pair_sums("1", shard_major({**g_ffn1, **g_mix1}))
    hr1, tok = _ici_start("scatter", part1, [part_shapes[n] for n in names1], "rs_start1")

    lw0["after"] = tok
    carry, g_ffn0 = _bwd_layer_ffn(0, dx, lw0, sv0)
    names_a, part_a = pair_sums("0a", shard_major(g_ffn0))
    _, recv1 = _ici_wait(hr1, part_a[0], "rs_wait1")
    hra, tok = _ici_start("scatter", part_a, [part_shapes[n] for n in names_a], "rs_start0a")

    lw0["mixer"] = [lw0["mixer"][0] + tok[0, 0]] + lw0["mixer"][1:]
    dx, g_mix0 = _bwd_layer_mix(0, carry, lw0, sv0, tables)
    _, recv_a = _ici_wait(hra, dx, "rs_wait0a")

    small_red, nf_red, loss_red, tok = _all_reduce_small_grads([{**g_ffn0, **g_mix0}, {**g_ffn1, **g_mix1}], d_nfinal, loss)
    small_red["norm_final"] = nf_red
    loss_out = loss_red[0, 0]
    for n in ("cv_w", "sc_w"):
        small_red[n] = lax.dynamic_slice_in_dim(small_red[n], chip * 128, 128, axis=2)

    g_mix0["w_in"] = g_mix0["w_in"] + tok[0, 0].astype(BF16)
    names_b, part_b = pair_sums("0b", shard_major(g_mix0))
    hrb, tok = _ici_start("scatter", part_b, [part_shapes[n] for n in names_b], "rs_start0b")
    bufs = {n: _sum_chips(part1[k], recv1[k], place, 1, None, f"rs_sum1_{n}", tok) for k, n in enumerate(names1)}
    for k, n in enumerate(names_a):
        bufs[n] = _sum_chips(part_a[k], recv_a[k], place, 0, bufs[n], f"rs_sum0_{n}", tok)
    shared = dict(zip(names_a, _rs_share([bufs[n] for n in names_a], "rs_share_a")))
    upd = {}
    for n in names_a:
        red = shared[n].reshape(W[n].shape)
        upd[n] = [red] + _adamw(W[n], red, M[n], V[n], f"adamw_{n}")
    two_d = lambda t: t[None] if t.ndim == 1 else t
    small_upd = _adamw_small(*([two_d(t[n]) for n in SMALL] for t in (W, small_red, M, V)), "adamw_small")
    for n, d, mo, vo in zip(SMALL, *small_upd):
        upd[n] = [t.reshape(W[n].shape) for t in (small_red[n], d, mo, vo)]

    _, recv_b = _ici_wait(hrb, upd[names_a[-1]][1], "rs_wait0b")
    for k, n in enumerate(names_b):
        bufs[n] = _sum_chips(part_b[k], recv_b[k], place, 0, bufs[n], f"rs_sum0_{n}", tok)
    shared = dict(zip(names_b, _rs_share([bufs[n] for n in names_b], "rs_share_b")))
    for n in names_b:
        red = shared[n].reshape(Wt[n].shape)
        upd[n] = [T_(t) for t in [red] + _adamw(Wt[n], red, Mt[n], Vt[n], f"adamw_{n}")]

    out = [loss_out, dx[None]]
    for k in range(4):
        out += [upd[n][k] for n in ORDER]
    return tuple(out)
```

```python
import functools
import math

import jax
import jax.numpy as jnp
from jax import lax
from jax.experimental import pallas as pl
from jax.experimental.pallas import tpu as pltpu

F32 = jnp.float32
BF16 = jnp.bfloat16

D_MODEL = 1024
DEPTH = 2
HALF = 512
SG_CHUNK = 128
SG_GROUPS = 4
CV_KERNEL = 31
HEAD_DIM = 64
N_Q_HEADS = 8
N_KV_HEADS = 2
Q_PER_KV = N_Q_HEADS // N_KV_HEADS
WINDOW = 128
ROPE_THETA = 10000.0
SC_KERNEL = 3
N_BRANCH = 4
D_FF = 2816
EPS = 1e-6
N_CHIPS = 4
N_DEV = 8

MIX_W = 4352
GATE_W = N_BRANCH * D_MODEL
PROJ_PAD = 2 * MIX_W
W_IN_SHARD = 2112
GU_SHARD = 1408
HALO = 128
CV_PAD = 32

ADAM_LR = 0.001
ADAM_B1 = 0.9
ADAM_B2 = 0.999
ADAM_EPS = 1e-08
ADAM_WD = 0.01
ADAM_STEP = 10

VMEM_LIMIT = 56 * 1024 * 1024
INV_SQRT2 = 1.0 / math.sqrt(2.0)
INV_SQRT_2PI = 1.0 / math.sqrt(2.0 * math.pi)
NEG_BIG = -1e30
MESH = pl.DeviceIdType.MESH

C_ZA, C_ZB, C_Q, C_K, C_V, C_ZD = 0, 1024, 2048, 2560, 2688, 2816


def _params(*sem):
    return pltpu.CompilerParams(dimension_semantics=sem, vmem_limit_bytes=VMEM_LIMIT)


def _sig(v):
    return 1.0 / (1.0 + jnp.exp(-v))


def _dot(a, b):
    return jnp.dot(a, b, preferred_element_type=F32)


def _dot_nt(a, b):
    return lax.dot_general(a, b, (((1,), (1,)), ((), ())), preferred_element_type=F32)


def _dot_tn(a, b):
    return lax.dot_general(a, b, (((0,), (0,)), ((), ())), preferred_element_type=F32)


def _full(shape):
    nd = len(shape)
    return pl.BlockSpec(shape, lambda *_: (0,) * nd)


def _rms_mm(x, g, w, tm, tn, name):
    T = x.shape[0]
    transposed = w.ndim == 2
    if transposed:
        N = w.shape[0]
        wspec = pl.BlockSpec((tn, D_MODEL), lambda i, j: (j, 0))
    else:
        tn = w.shape[2]
        N = w.shape[0] * tn
        wspec = pl.BlockSpec((None, D_MODEL, tn), lambda i, j: (j, 0, 0))

    def body(x_ref, g_ref, w_ref, o_ref, xn_ref):
        @pl.when(pl.program_id(1) == 0)
        def _():
            xv = x_ref[...]
            r = lax.rsqrt(jnp.mean(xv * xv, axis=-1, keepdims=True) + EPS)
            xn_ref[...] = (xv * r * g_ref[...]).astype(BF16)

        o_ref[...] = (_dot_nt if transposed else _dot)(xn_ref[...], w_ref[...]).astype(BF16)

    return pl.pallas_call(
        body, name=name, grid=(T // tm, N // tn),
        in_specs=[pl.BlockSpec((tm, D_MODEL), lambda i, j: (i, 0)), _full((1, D_MODEL)), wspec],
        out_specs=[pl.BlockSpec((tm, tn), lambda i, j: (i, j)), pl.BlockSpec((tm, D_MODEL), lambda i, j: (i, 0))],
        out_shape=[jax.ShapeDtypeStruct((T, N), BF16), jax.ShapeDtypeStruct((T, D_MODEL), BF16)],
        compiler_params=_params("parallel", "arbitrary"),
    )(x, g, w)


def _merge_fwd(x, ys, proj, wb, wo, tm, name):
    T = x.shape[0]

    def body(x_ref, ys_ref, zg_ref, wb_ref, wo_ref, xo_ref, mg_ref):
        merged = None
        for n in range(N_BRANCH):
            yn = ys_ref[:, n * HALF:(n + 1) * HALF]
            br = jnp.concatenate([_dot(yn, wb_ref[s, n]) for s in range(N_CHIPS)], axis=1)
            t = _sig(zg_ref[:, n * D_MODEL:(n + 1) * D_MODEL].astype(F32)) * br
            merged = t if merged is None else merged + t
        mb = merged.astype(BF16)
        mg_ref[...] = mb
        xo_ref[...] = x_ref[...] + _dot(mb, wo_ref[...])

    return pl.pallas_call(
        body, name=name, grid=(T // tm,),
        in_specs=[pl.BlockSpec((tm, D_MODEL), lambda i: (i, 0)), pl.BlockSpec((tm, N_BRANCH * HALF), lambda i: (i, 0)),
                  pl.BlockSpec((tm, GATE_W), lambda i: (i, 0)), _full(wb.shape), _full(wo.shape)],
        out_specs=[pl.BlockSpec((tm, D_MODEL), lambda i: (i, 0)), pl.BlockSpec((tm, D_MODEL), lambda i: (i, 0))],
        out_shape=[jax.ShapeDtypeStruct((T, D_MODEL), F32), jax.ShapeDtypeStruct((T, D_MODEL), BF16)],
        compiler_params=_params("parallel"),
    )(x, ys, proj, wb, wo)


def _ffn_down(xm, gu, wd, tm, name):
    T = xm.shape[0]

    def body(x_ref, gu_ref, wd_ref, o_ref):
        g = gu_ref[:, :D_FF].astype(F32)
        u = gu_ref[:, D_FF:].astype(F32)
        act = (g * _sig(g) * u).astype(BF16)
        o_ref[...] = x_ref[...] + _dot(act, wd_ref[...])

    return pl.pallas_call(
        body, name=name, grid=(T // tm,),
        in_specs=[pl.BlockSpec((tm, D_MODEL), lambda i: (i, 0)), pl.BlockSpec((tm, 2 * D_FF), lambda i: (i, 0)), _full(wd.shape)],
        out_specs=pl.BlockSpec((tm, D_MODEL), lambda i: (i, 0)),
        out_shape=jax.ShapeDtypeStruct((T, D_MODEL), F32),
        compiler_params=_params("parallel"),
    )(xm, gu, wd)


def _final_loss(x, g, tgt, tm, name):
    T = x.shape[0]

    def body(x_ref, g_ref, t_ref, dx_ref, dg_ref, ls_ref):
        @pl.when(pl.program_id(0) == 0)
        def _():
            dg_ref[...] = jnp.zeros_like(dg_ref)
            ls_ref[...] = jnp.zeros_like(ls_ref)

        xv = x_ref[...]
        gv = g_ref[...]
        r = lax.rsqrt(jnp.mean(xv * xv, axis=-1, keepdims=True) + EPS)
        xh = xv * r
        diff = xh * gv - t_ref[...]
        ls_ref[...] += jnp.full(ls_ref.shape, 0.5 / D_MODEL, F32) * jnp.sum(diff * diff)
        dy = diff * (1.0 / D_MODEL)
        dxh = dy * gv
        dx_ref[...] = r * (dxh - xh * jnp.mean(dxh * xh, axis=-1, keepdims=True))
        dg_ref[...] += jnp.sum(dy * xh, axis=0, keepdims=True)

    return pl.pallas_call(
        body, name=name, grid=(T // tm,),
        in_specs=[pl.BlockSpec((tm, D_MODEL), lambda i: (i, 0)), _full((1, D_MODEL)), pl.BlockSpec((tm, D_MODEL), lambda i: (i, 0))],
        out_specs=[pl.BlockSpec((tm, D_MODEL), lambda i: (i, 0)), _full((1, D_MODEL)), _full((1, 128))],
        out_shape=[jax.ShapeDtypeStruct((T, D_MODEL), F32), jax.ShapeDtypeStruct((1, D_MODEL), F32), jax.ShapeDtypeStruct((1, 128), F32)],
        compiler_params=_params("arbitrary"),
    )(x, g, tgt)


def _swiglu_bwd(dx, gu, wd, tm, name, after):
    T = dx.shape[0]

    def body(dx_ref, gu_ref, wd_ref, after_ref, dgu_ref, act_ref):
        del after_ref
        dact = _dot_nt(dx_ref[...].astype(BF16), wd_ref[...])
        g = gu_ref[:, :D_FF].astype(F32)
        u = gu_ref[:, D_FF:].astype(F32)
        s = _sig(g)
        silu = g * s
        act_ref[...] = (silu * u).astype(BF16)
        dgu_ref[:, :D_FF] = (dact * u * (s + silu * (1.0 - s))).astype(BF16)
        dgu_ref[:, D_FF:] = (dact * silu).astype(BF16)

    return pl.pallas_call(
        body, name=name, grid=(T // tm,),
        in_specs=[pl.BlockSpec((tm, D_MODEL), lambda i: (i, 0)), pl.BlockSpec((tm, 2 * D_FF), lambda i: (i, 0)), _full(wd.shape),
                  pl.BlockSpec(memory_space=pl.ANY)],
        out_specs=[pl.BlockSpec((tm, 2 * D_FF), lambda i: (i, 0)), pl.BlockSpec((tm, D_FF), lambda i: (i, 0))],
        out_shape=[jax.ShapeDtypeStruct((T, 2 * D_FF), BF16), jax.ShapeDtypeStruct((T, D_FF), BF16)],
        compiler_params=_params("parallel"),
    )(dx, gu, wd, after)


def _mm_tn(a, b, grid, a_block, a_map, b_block, b_map, o_shape, o_block, o_map, name, col_split=1):
    gk = grid[2]
    tm = [d for d in a_block if d is not None][-1]
    tn = [d for d in b_block if d is not None][-1]

    def body(a_ref, b_ref, o_ref, acc_ref):
        k = pl.program_id(2)
        p = _dot_tn(a_ref[...].astype(BF16), b_ref[...].astype(BF16))

        @pl.when(k == 0)
        def _():
            acc_ref[...] = p

        @pl.when(k > 0)
        def _():
            acc_ref[...] += p

        @pl.when(k == gk - 1)
        def _():
            if col_split == 1:
                o_ref[...] = acc_ref[...].astype(o_ref.dtype)
            else:
                w = tn // col_split
                for s in range(col_split):
                    o_ref[s] = acc_ref[:, s * w:(s + 1) * w].astype(o_ref.dtype)

    return pl.pallas_call(
        body, name=name, grid=grid,
        in_specs=[pl.BlockSpec(a_block, a_map), pl.BlockSpec(b_block, b_map)],
        out_specs=pl.BlockSpec(o_block, o_map),
        out_shape=jax.ShapeDtypeStruct(o_shape, BF16),
        scratch_shapes=[pltpu.VMEM((tm, tn), F32)],
        compiler_params=_params("parallel", "parallel", "arbitrary"),
    )(a, b)


def _mm_nt_rmsbwd(a, w, x, g, dres, tm, tk, name):
    T = x.shape[0]
    transposed = w.ndim == 2
    if transposed:
        gk = w.shape[0] // tk
        wspec = pl.BlockSpec((tk, D_MODEL), lambda i, k: (k, 0))
    else:
        tk = w.shape[2]
        gk = w.shape[0]
        wspec = pl.BlockSpec((None, D_MODEL, tk), lambda i, k: (k, 0, 0))

    def body(a_ref, w_ref, x_ref, g_ref, r_ref, dx_ref, dg_ref, acc_ref):
        i, k = pl.program_id(0), pl.program_id(1)
        p = (_dot if transposed else _dot_nt)(a_ref[...], w_ref[...])

        @pl.when(k == 0)
        def _():
            acc_ref[...] = p

        @pl.when(k > 0)
        def _():
            acc_ref[...] += p

        @pl.when(jnp.logical_and(i == 0, k == 0))
        def _():
            dg_ref[...] = jnp.zeros_like(dg_ref)

        @pl.when(k == gk - 1)
        def _():
            dh = acc_ref[...]
            xv = x_ref[...]
            r = lax.rsqrt(jnp.mean(xv * xv, axis=-1, keepdims=True) + EPS)
            xh = xv * r
            dxh = dh * g_ref[...]
            dx_ref[...] = r_ref[...] + r * (dxh - xh * jnp.mean(dxh * xh, axis=-1, keepdims=True))
            dg_ref[...] += jnp.sum(dh * xh, axis=0, keepdims=True)

    return pl.pallas_call(
        body, name=name, grid=(T // tm, gk),
        in_specs=[pl.BlockSpec((tm, tk), lambda i, k: (i, k)), wspec, pl.BlockSpec((tm, D_MODEL), lambda i, k: (i, 0)),
                  _full((1, D_MODEL)), pl.BlockSpec((tm, D_MODEL), lambda i, k: (i, 0))],
        out_specs=[pl.BlockSpec((tm, D_MODEL), lambda i, k: (i, 0)), _full((1, D_MODEL))],
        out_shape=[jax.ShapeDtypeStruct((T, D_MODEL), F32), jax.ShapeDtypeStruct((1, D_MODEL), F32)],
        scratch_shapes=[pltpu.VMEM((tm, D_MODEL), F32)],
        compiler_params=_params("arbitrary", "arbitrary"),
    )(a, w, x, g, dres)


def _merge_bwd(dxm, ys, proj, wb, wo, tm, name):
    T = dxm.shape[0]

    def body(dx_ref, ys_ref, zg_ref, wb_ref, wo_ref, dys_ref, dbr_ref, dp_ref):
        dmerged = _dot_nt(dx_ref[...].astype(BF16), wo_ref[...])
        for n in range(N_BRANCH):
            yn = ys_ref[:, n * HALF:(n + 1) * HALF]
            br = jnp.concatenate([_dot(yn, wb_ref[s, n]) for s in range(N_CHIPS)], axis=1)
            gt = _sig(zg_ref[:, n * D_MODEL:(n + 1) * D_MODEL].astype(F32))
            dbr = (gt * dmerged).astype(BF16)
            dbr_ref[:, n * D_MODEL:(n + 1) * D_MODEL] = dbr
            dp_ref[:, n * D_MODEL:(n + 1) * D_MODEL] = (dmerged * br * gt * (1.0 - gt)).astype(BF16)
            dy = None
            for s in range(N_CHIPS):
                t = _dot_nt(dbr[:, s * 256:(s + 1) * 256], wb_ref[s, n])
                dy = t if dy is None else dy + t
            dys_ref[:, n * HALF:(n + 1) * HALF] = dy.astype(BF16)
        dp_ref[:, GATE_W:] = jnp.zeros((tm, MIX_W - GATE_W), BF16)

    return pl.pallas_call(
        body, name=name, grid=(T // tm,),
        in_specs=[pl.BlockSpec((tm, D_MODEL), lambda i: (i, 0)), pl.BlockSpec((tm, N_BRANCH * HALF), lambda i: (i, 0)),
                  pl.BlockSpec((tm, GATE_W), lambda i: (i, 0)), _full(wb.shape), _full(wo.shape)],
        out_specs=[pl.BlockSpec((tm, N_BRANCH * HALF), lambda i: (i, 0)), pl.BlockSpec((tm, GATE_W), lambda i: (i, 0)),
                   pl.BlockSpec((tm, MIX_W), lambda i: (i, 0))],
        out_shape=[jax.ShapeDtypeStruct((T, N_BRANCH * HALF), BF16), jax.ShapeDtypeStruct((T, GATE_W), BF16),
                   jax.ShapeDtypeStruct((T, PROJ_PAD), BF16)],
        compiler_params=_params("parallel"),
    )(dxm, ys, proj, wb, wo)


def _gelu(v):
    return 0.5 * v * (1.0 + lax.erf(v * INV_SQRT2))


def _gelu_grad(v):
    return 0.5 * (1.0 + lax.erf(v * INV_SQRT2)) + v * jnp.exp(-0.5 * v * v) * INV_SQRT_2PI


def _rot_half(t):
    w = t.shape[1]
    lane = lax.broadcasted_iota(jnp.int32, t.shape, 1)
    return jnp.where((lane % HEAD_DIM) < HEAD_DIM // 2, pltpu.roll(t, w - HEAD_DIM // 2, 1), pltpu.roll(t, HEAD_DIM // 2, 1))


def _rope(t, cos, sin_signed):
    return t * cos + _rot_half(t) * sin_signed


def _rope_t(d, cos, sin_signed):
    return d * cos + _rot_half(d * sin_signed)


def _ln_fwd(v, g, b):
    mu = jnp.mean(v, axis=-1, keepdims=True)
    vc = v - mu
    r = lax.rsqrt(jnp.mean(vc * vc, axis=-1, keepdims=True) + EPS)
    vh = vc * r
    return vh * g + b, vh, r


def _ln_bwd(dn, vh, r, g):
    dvh = dn * g
    return r * (dvh - jnp.mean(dvh, axis=-1, keepdims=True) - vh * jnp.mean(dvh * vh, axis=-1, keepdims=True))


def _sublane_shifts(sh_ref, rows):
    for b in range(1, 8):
        sh_ref[b, 0:rows - 8, :] = sh_ref[0, pl.ds(b, rows - 8), :]


def _tap(sh_ref, off, n):
    return sh_ref[off % 8, pl.ds(off - off % 8, n), :]


def _tril_mask():
    return lax.broadcasted_iota(jnp.int32, (SG_CHUNK, SG_CHUNK), 0) >= lax.broadcasted_iota(jnp.int32, (SG_CHUNK, SG_CHUNK), 1)


def _band_masks():
    shape = (Q_PER_KV * WINDOW, 2 * WINDOW)
    row = lax.broadcasted_iota(jnp.int32, shape, 0) % WINDOW
    col = lax.broadcasted_iota(jnp.int32, shape, 1)
    band = (col > row) & (col <= row + WINDOW)
    return band, band & (col >= WINDOW)


def _attn_probs(qs, kh, sink_col, valid):
    s = jnp.where(valid, _dot_nt(qs, kh) * (HEAD_DIM ** -0.5), NEG_BIG)
    m = jnp.maximum(jnp.max(s, axis=-1, keepdims=True), sink_col)
    p = jnp.exp(s - m)
    es = jnp.exp(sink_col - m)
    inv = 1.0 / (jnp.sum(p, axis=-1, keepdims=True) + es)
    return p * inv, es * inv


def _sink_col(sinks_ref, h):
    return jnp.concatenate([jnp.broadcast_to(sinks_ref[:, h * Q_PER_KV + g:h * Q_PER_KV + g + 1], (WINDOW, 1))
                            for g in range(Q_PER_KV)], axis=0)


def _mixer_in_specs(TB, nb):
    r = TB // HALO
    last = nb * r - 1
    cur = pl.BlockSpec((TB, MIX_W), lambda i: (i, 1))
    prev = pl.BlockSpec((HALO, MIX_W), lambda i: (jnp.maximum(i * r - 1, 0), 1))
    nxt = pl.BlockSpec((HALO, MIX_W), lambda i: (jnp.minimum((i + 1) * r, last), 1))
    tcur = pl.BlockSpec((TB, 128), lambda i: (i, 0))
    tprev = pl.BlockSpec((HALO, 128), lambda i: (jnp.maximum(i * r - 1, 0), 0))
    tnxt = pl.BlockSpec((HALO, 128), lambda i: (jnp.minimum((i + 1) * r, last), 0))
    return cur, prev, nxt, tcur, tprev, tnxt


def _mixer_param_specs():
    return [_full((1, HALF)), _full((1, HALF)), _full((SG_GROUPS, SG_CHUNK, SG_CHUNK)), _full((SG_CHUNK, 128)),
            _full((32, HALF)), _full((1, HALF)), _full((1, HALF)), _full((1, HALF)), _full((1, 128)), _full((8, HALF))]


def _mixers_fwd(proj, cos_t, sin_t, mp, TB, name):
    T = proj.shape[0]
    nb = T // TB
    r = TB // HALO
    cur, prev, _, tcur, tprev, _ = _mixer_in_specs(TB, nb)

    def body(zc_ref, zp_ref, cc_ref, sc_ref, cp_ref, sp_ref,
             lg_ref, lb_ref, sgw_ref, sgb_ref, cvw_ref, cvb_ref, cvg_ref, cvbb_ref, sinks_ref, scw_ref,
             ys_ref, scr_ref, k_ref, v_ref, sh_ref):
        i = pl.program_id(0)
        pm = (i > 0).astype(F32)

        def colsE(c0, c1):
            return jnp.concatenate([zp_ref[:, c0:c1].astype(F32) * pm, zc_ref[:, c0:c1].astype(F32)], axis=0)

        a = _gelu(zc_ref[:, C_ZA:C_ZA + 2 * HALF].astype(F32))
        u = a[:, :HALF]
        vn, _, _ = _ln_fwd(a[:, HALF:], lg_ref[...], lb_ref[...])
        vnb = vn.astype(BF16)
        tril = _tril_mask()
        chunks = [slice(ci * SG_CHUNK, (ci + 1) * SG_CHUNK) for ci in range(r)]
        for g in range(SG_GROUPS):
            cols = slice(g * 128, (g + 1) * 128)
            wt = jnp.where(tril, sgw_ref[g], 0.0).astype(BF16)
            mixed = _dot(wt, jnp.concatenate([vnb[rows, cols] for rows in chunks], axis=1)) + sgb_ref[:, g:g + 1]
            for ci, rows in enumerate(chunks):
                ys_ref[rows, cols] = (u[rows, cols] * mixed[:, ci * 128:(ci + 1) * 128]).astype(BF16)

        def colsB(c0, c1):
            return jnp.concatenate([zp_ref[HALO - CV_PAD:, c0:c1].astype(F32) * pm, zc_ref[:, c0:c1].astype(F32)], axis=0)

        sh_ref[0] = colsB(C_ZB, C_ZB + HALF) * _sig(colsB(C_ZB + HALF, C_ZB + 2 * HALF))
        _sublane_shifts(sh_ref, TB + CV_PAD)
        c = jnp.broadcast_to(cvb_ref[...], (TB, HALF))
        for k in range(CV_KERNEL):
            c = c + cvw_ref[k:k + 1, :] * _tap(sh_ref, CV_PAD - (CV_KERNEL - 1) + k, TB)
        n, _, _ = _ln_fwd(c, cvg_ref[...], cvbb_ref[...])
        ys_ref[:, HALF:2 * HALF] = (n * _sig(n)).astype(BF16)

        zd = colsE(C_ZD + HALF, C_ZD + 3 * HALF)
        scr_ref[...] = zd[:, :HALF] * zd[:, HALF:]
        cv = None
        for k in range(SC_KERNEL):
            t = scw_ref[k:k + 1, :] * scr_ref[pl.ds(HALO - (SC_KERNEL - 1) + k, TB), :]
            cv = t if cv is None else cv + t
        ys_ref[:, 3 * HALF:4 * HALF] = (zc_ref[:, C_ZD:C_ZD + HALF].astype(F32) * cv).astype(BF16)

        cosE = jnp.concatenate([cp_ref[...], cc_ref[...]], axis=0)
        sinE = jnp.concatenate([sp_ref[...], sc_ref[...]], axis=0)
        k_ref[...] = _rope(colsE(C_K, C_K + 128), cosE, sinE).astype(BF16)
        v_ref[...] = colsE(C_V, C_V + 128).astype(BF16)
        cosC, sinC = cc_ref[...], sc_ref[...]
        q = jnp.concatenate([_rope(zc_ref[:, C_Q + 128 * j:C_Q + 128 * (j + 1)].astype(F32), cosC, sinC)
                             for j in range(4)], axis=1).astype(BF16)
        in_band, in_band_cur = _band_masks()
        sink_cols = [_sink_col(sinks_ref, h) for h in range(N_KV_HEADS)]
        for qb in range(r):
            valid = in_band if qb else in_band_cur | (in_band & (i > 0))
            for h in range(N_KV_HEADS):
                hc = slice(h * HEAD_DIM, (h + 1) * HEAD_DIM)
                kh = k_ref[qb * WINDOW:qb * WINDOW + 2 * WINDOW, hc]
                vh = v_ref[qb * WINDOW:qb * WINDOW + 2 * WINDOW, hc]
                qs = jnp.concatenate([q[qb * WINDOW:(qb + 1) * WINDOW, (h * Q_PER_KV + g) * HEAD_DIM:(h * Q_PER_KV + g + 1) * HEAD_DIM]
                                      for g in range(Q_PER_KV)], axis=0)
                probs, _ = _attn_probs(qs, kh, sink_cols[h], valid)
                o = _dot(probs.astype(BF16), vh)
                for g in range(Q_PER_KV):
                    c0 = 2 * HALF + (h * Q_PER_KV + g) * HEAD_DIM
                    ys_ref[qb * WINDOW:(qb + 1) * WINDOW, c0:c0 + HEAD_DIM] = o[g * WINDOW:(g + 1) * WINDOW].astype(BF16)

    return pl.pallas_call(
        body, name=name, grid=(nb,),
        in_specs=[cur, prev, tcur, tcur, tprev, tprev] + _mixer_param_specs(),
        out_specs=pl.BlockSpec((TB, 4 * HALF), lambda i: (i, 0)),
        out_shape=jax.ShapeDtypeStruct((T, 4 * HALF), BF16),
        scratch_shapes=[pltpu.VMEM((TB + HALO, HALF), F32), pltpu.VMEM((TB + HALO, 128), BF16), pltpu.VMEM((TB + HALO, 128), BF16),
                        pltpu.VMEM((8, TB + CV_PAD, HALF), F32)],
        compiler_params=_params("parallel"),
    )(proj, proj, cos_t, sin_t, cos_t, sin_t, *mp)


def _mixers_bwd(proj, dys, dproj, cos_t, sin_t, mp, TB, name):
    T = proj.shape[0]
    nb = T // TB
    r = TB // HALO
    RE = TB + 2 * HALO
    RC = TB + HALO
    cur, prev, nxt, tcur, tprev, tnxt = _mixer_in_specs(TB, nb)
    dcur = pl.BlockSpec((TB, 4 * HALF), lambda i: (i, 0))
    dnxt = pl.BlockSpec((HALO, 4 * HALF), lambda i: (jnp.minimum((i + 1) * r, nb * r - 1), 0))

    def body(zc_ref, zp_ref, zn_ref, dyc_ref, dyn_ref, cc_ref, sc_ref, cp_ref, sp_ref, cn_ref, sn_ref,
             lg_ref, lb_ref, sgw_ref, sgb_ref, cvw_ref, cvb_ref, cvg_ref, cvbb_ref, sinks_ref, scw_ref, dp_in_ref,
             dz_ref, dlg_ref, dlb_ref, dsgw_ref, dsgb_ref, dcvw_ref, dcvb_ref, dcvg_ref, dcvbb_ref, dsink_ref, dscw_ref,
             scr_ref, scr2_ref, k_ref, v_ref, dk_ref, dv_ref, dq_ref, sh_ref, sh2_ref):
        del dp_in_ref
        i = pl.program_id(0)
        pm = (i > 0).astype(F32)
        nm = (i < nb - 1).astype(F32)

        @pl.when(i == 0)
        def _():
            for ref in (dlg_ref, dlb_ref, dsgw_ref, dsgb_ref, dcvw_ref, dcvb_ref, dcvg_ref, dcvbb_ref, dsink_ref, dscw_ref):
                ref[...] = jnp.zeros_like(ref)

        def colsE(c0, c1):
            return jnp.concatenate([zp_ref[:, c0:c1].astype(F32) * pm, zc_ref[:, c0:c1].astype(F32),
                                    zn_ref[:, c0:c1].astype(F32)], axis=0)

        def colsC(c0, c1):
            return jnp.concatenate([zc_ref[:, c0:c1].astype(F32), zn_ref[:, c0:c1].astype(F32)], axis=0)

        def dyC(c0, c1):
            return jnp.concatenate([dyc_ref[:, c0:c1].astype(F32), dyn_ref[:, c0:c1].astype(F32) * nm], axis=0)

        za = zc_ref[:, C_ZA:C_ZA + 2 * HALF].astype(F32)
        a = _gelu(za)
        u = a[:, :HALF]
        lg = lg_ref[...]
        vn, vh, rs = _ln_fwd(a[:, HALF:], lg, lb_ref[...])
        vnb = vn.astype(BF16)
        dya = dyc_ref[:, 0:HALF].astype(F32)
        tril = _tril_mask()
        lane128 = lax.broadcasted_iota(jnp.int32, (SG_CHUNK, 128), 1)
        chunks = [slice(ci * SG_CHUNK, (ci + 1) * SG_CHUNK) for ci in range(r)]
        side = lambda t, cols: jnp.concatenate([t[rows, cols] for rows in chunks], axis=1)
        for g in range(SG_GROUPS):
            cols = slice(g * 128, (g + 1) * 128)
            wt = jnp.where(tril, sgw_ref[g], 0.0).astype(BF16)
            vb = side(vnb, cols)
            dy_blk = side(dya, cols)
            du_g = dy_blk * (_dot(wt, vb) + sgb_ref[:, g:g + 1])
            dmix = dy_blk * side(u, cols)
            dmb = dmix.astype(BF16)
            dvn_g = _dot_tn(wt, dmb)
            dsgw_ref[g] += jnp.where(tril, _dot_nt(dmb, vb), 0.0)
            dsgb_ref[...] += jnp.where(lane128 == g, jnp.sum(dmix, axis=1, keepdims=True), 0.0)
            for ci, rows in enumerate(chunks):
                scr_ref[rows, cols] = du_g[:, ci * 128:(ci + 1) * 128]
                scr2_ref[rows, cols] = dvn_g[:, ci * 128:(ci + 1) * 128]
        du, dvn = scr_ref[0:TB, :], scr2_ref[0:TB, :]
        dlg_ref[...] += jnp.sum(dvn * vh, axis=0, keepdims=True)
        dlb_ref[...] += jnp.sum(dvn, axis=0, keepdims=True)
        dvv = _ln_bwd(dvn, vh, rs, lg)
        gg = _gelu_grad(za)
        dz_ref[:, C_ZA:C_ZA + HALF] = (du * gg[:, :HALF]).astype(BF16)
        dz_ref[:, C_ZA + HALF:C_ZA + 2 * HALF] = (dvv * gg[:, HALF:]).astype(BF16)

        RB = TB + CV_PAD

        def colsB(c0, c1):
            return jnp.concatenate([zp_ref[HALO - CV_PAD:, c0:c1].astype(F32) * pm, zc_ref[:, c0:c1].astype(F32),
                                    zn_ref[:CV_PAD, c0:c1].astype(F32)], axis=0)

        sh_ref[0] = colsB(C_ZB, C_ZB + HALF) * _sig(colsB(C_ZB + HALF, C_ZB + 2 * HALF))
        _sublane_shifts(sh_ref, RB + CV_PAD)
        c = jnp.broadcast_to(cvb_ref[...], (RB, HALF))
        for k in range(CV_KERNEL):
            c = c + cvw_ref[k:k + 1, :] * _tap(sh_ref, CV_PAD - (CV_KERNEL - 1) + k, RB)
        cvg = cvg_ref[...]
        n, ch, rc = _ln_fwd(c, cvg, cvbb_ref[...])
        sn = _sig(n)
        dyb = jnp.concatenate([dyc_ref[:, HALF:2 * HALF].astype(F32), dyn_ref[:CV_PAD, HALF:2 * HALF].astype(F32) * nm], axis=0)
        dn = dyb * (sn + n * sn * (1.0 - sn))
        dno = dn[:TB]
        dcvg_ref[...] += jnp.sum(dno * ch[:TB], axis=0, keepdims=True)
        dcvbb_ref[...] += jnp.sum(dno, axis=0, keepdims=True)
        dc = _ln_bwd(dn, ch, rc, cvg)
        sh2_ref[0] = dc
        _sublane_shifts(sh2_ref, RB)
        dcvb_ref[...] += jnp.sum(dc[:TB], axis=0, keepdims=True)
        dy0 = None
        for k in range(CV_KERNEL):
            wk = cvw_ref[k:k + 1, :]
            t = wk * _tap(sh2_ref, CV_KERNEL - 1 - k, TB)
            dy0 = t if dy0 is None else dy0 + t
            dcvw_ref[k:k + 1, :] += jnp.sum(dc[:TB] * _tap(sh_ref, CV_PAD - (CV_KERNEL - 1) + k, TB), axis=0, keepdims=True)
        ab = zc_ref[:, C_ZB:C_ZB + HALF].astype(F32)
        sg = _sig(zc_ref[:, C_ZB + HALF:C_ZB + 2 * HALF].astype(F32))
        dz_ref[:, C_ZB:C_ZB + HALF] = (dy0 * sg).astype(BF16)
        dz_ref[:, C_ZB + HALF:C_ZB + 2 * HALF] = (dy0 * ab * sg * (1.0 - sg)).astype(BF16)

        zd = colsE(C_ZD + HALF, C_ZD + 3 * HALF)
        scr_ref[...] = zd[:, :HALF] * zd[:, HALF:]
        dcv = dyC(3 * HALF, 4 * HALF) * colsC(C_ZD, C_ZD + HALF)
        scr2_ref[...] = dcv
        cv = None
        dud = None
        for k in range(SC_KERNEL):
            wk = scw_ref[k:k + 1, :]
            us = scr_ref[pl.ds(HALO - (SC_KERNEL - 1) + k, TB), :]
            t = wk * us
            cv = t if cv is None else cv + t
            t2 = wk * scr2_ref[pl.ds(SC_KERNEL - 1 - k, TB), :]
            dud = t2 if dud is None else dud + t2
            dscw_ref[k:k + 1, :] += jnp.sum(dcv[:TB] * us, axis=0, keepdims=True)
        dz_ref[:, C_ZD:C_ZD + HALF] = (dyc_ref[:, 3 * HALF:4 * HALF].astype(F32) * cv).astype(BF16)
        dz_ref[:, C_ZD + HALF:C_ZD + 2 * HALF] = (dud * zc_ref[:, C_ZD + 2 * HALF:C_ZD + 3 * HALF].astype(F32)).astype(BF16)
        dz_ref[:, C_ZD + 2 * HALF:C_ZD + 3 * HALF] = (dud * zc_ref[:, C_ZD + HALF:C_ZD + 2 * HALF].astype(F32)).astype(BF16)

        cosE = jnp.concatenate([cp_ref[...], cc_ref[...], cn_ref[...]], axis=0)
        sinE = jnp.concatenate([sp_ref[...], sc_ref[...], sn_ref[...]], axis=0)
        k_ref[...] = _rope(colsE(C_K, C_K + 128), cosE, sinE).astype(BF16)
        v_ref[...] = colsE(C_V, C_V + 128).astype(BF16)
        dk_ref[...] = jnp.zeros_like(dk_ref)
        dv_ref[...] = jnp.zeros_like(dv_ref)
        q = jnp.concatenate([_rope(colsC(C_Q + 128 * j, C_Q + 128 * (j + 1)), cosE[HALO:], sinE[HALO:])
                             for j in range(4)], axis=1).astype(BF16)
        dO = dyC(2 * HALF, 3 * HALF).astype(BF16)
        lane_s = lax.broadcasted_iota(jnp.int32, (1, 128), 1)
        in_band, in_band_cur = _band_masks()
        sink_cols = [_sink_col(sinks_ref, h) for h in range(N_KV_HEADS)]
        for qb in range(r + 1):
            valid = in_band if qb else in_band_cur | (in_band & (i > 0))
            rows = slice(qb * WINDOW, (qb + 1) * WINDOW)
            band = slice(qb * WINDOW, qb * WINDOW + 2 * WINDOW)
            for h in range(N_KV_HEADS):
                hc = slice(h * HEAD_DIM, (h + 1) * HEAD_DIM)
                kh = k_ref[band, hc]
                vh_ = v_ref[band, hc]
                heads = [slice((h * Q_PER_KV + g) * HEAD_DIM, (h * Q_PER_KV + g + 1) * HEAD_DIM) for g in range(Q_PER_KV)]
                qs = jnp.concatenate([q[rows, hs] for hs in heads], axis=0)
                dos = jnp.concatenate([dO[rows, hs] for hs in heads], axis=0)
                probs, p_sink = _attn_probs(qs, kh, sink_cols[h], valid)
                dP = _dot_nt(dos, vh_)
                rsum = jnp.sum(probs * dP, axis=-1, keepdims=True)
                dS = (probs * (dP - rsum) * (HEAD_DIM ** -0.5)).astype(BF16)
                dk_ref[band, hc] += _dot_tn(dS, qs)
                dv_ref[band, hc] += _dot_tn(probs.astype(BF16), dos)
                if qb < r:
                    dqs = _dot(dS, kh)
                    dsk = -p_sink * rsum
                    for g in range(Q_PER_KV):
                        dq_ref[rows, heads[g]] = dqs[g * WINDOW:(g + 1) * WINDOW]
                        dsink_ref[...] += jnp.where(lane_s == h * Q_PER_KV + g, jnp.sum(dsk[g * WINDOW:(g + 1) * WINDOW]), 0.0)
        cosC, sinC = cc_ref[...], sc_ref[...]
        for j in range(4):
            dz_ref[:, C_Q + 128 * j:C_Q + 128 * (j + 1)] = _rope_t(dq_ref[:, 128 * j:128 * (j + 1)], cosC, sinC).astype(BF16)
        dz_ref[:, C_K:C_K + 128] = _rope_t(dk_ref[HALO:HALO + TB, :], cosC, sinC).astype(BF16)
        dz_ref[:, C_V:C_V + 128] = dv_ref[HALO:HALO + TB, :].astype(BF16)

    small = [((1, HALF), F32), ((1, HALF), F32), ((SG_GROUPS, SG_CHUNK, SG_CHUNK), F32), ((SG_CHUNK, 128), F32),
             ((32, HALF), F32), ((1, HALF), F32), ((1, HALF), F32), ((1, HALF), F32), ((1, 128), F32), ((8, HALF), F32)]
    outs = pl.pallas_call(
        body, name=name, grid=(nb,),
        in_specs=[cur, prev, nxt, dcur, dnxt, tcur, tcur, tprev, tprev, tnxt, tnxt] + _mixer_param_specs()
                 + [pl.BlockSpec(memory_space=pl.ANY)],
        out_specs=[pl.BlockSpec((TB, MIX_W), lambda i: (i, 1))] + [_full(s) for s, _ in small],
        out_shape=[jax.ShapeDtypeStruct((T, PROJ_PAD), BF16)] + [jax.ShapeDtypeStruct(s, d) for s, d in small],
        scratch_shapes=[pltpu.VMEM((RE, HALF), F32), pltpu.VMEM((RC, HALF), F32), pltpu.VMEM((RE, 128), BF16), pltpu.VMEM((RE, 128), BF16),
                        pltpu.VMEM((RE, 128), F32), pltpu.VMEM((RE, 128), F32), pltpu.VMEM((TB, HALF), F32),
                        pltpu.VMEM((8, TB + 2 * CV_PAD, HALF), F32), pltpu.VMEM((8, TB + CV_PAD, HALF), F32)],
        input_output_aliases={21: 0},
        compiler_params=_params("arbitrary"),
    )(proj, proj, proj, dys, dys, cos_t, sin_t, cos_t, sin_t, cos_t, sin_t, *mp, dproj)
    return outs


def _rope_tables(T):
    pos = jnp.arange(T, dtype=F32)
    inv_freq = 1.0 / (ROPE_THETA ** (jnp.arange(0, HEAD_DIM, 2, dtype=F32) / HEAD_DIM))
    ang = pos[:, None] * inv_freq[None, :]
    cos, sin = jnp.cos(ang), jnp.sin(ang)
    cos_t = jnp.concatenate([cos, cos, cos, cos], axis=1)
    sin_t = jnp.concatenate([-sin, sin, -sin, sin], axis=1)
    return cos_t, sin_t


def _mixer_params(l, sg_ln_g, sg_ln_b, sg_w, sg_b, cv_w, cv_b, cv_ln_g, cv_ln_b, attn_sinks, sc_w):
    sgb_t = jnp.zeros((SG_CHUNK, 128), F32).at[:, :SG_GROUPS].set(sg_b[l].T)
    cvw = jnp.zeros((32, HALF), F32).at[:CV_KERNEL].set(cv_w[l])
    scw = jnp.zeros((8, HALF), F32).at[:SC_KERNEL].set(sc_w[l])
    sinks = jnp.zeros((1, 128), F32).at[0, :N_Q_HEADS].set(attn_sinks[l])
    return [sg_ln_g[l][None], sg_ln_b[l][None], sg_w[l], sgb_t, cvw, cv_b[l][None], cv_ln_g[l][None], cv_ln_b[l][None], sinks, scw]


def _w_in_layout(w_in_g):
    cut = MIX_W - 2 * W_IN_SHARD
    return jnp.concatenate([w_in_g[2][cut:], w_in_g[3], jnp.zeros((MIX_W - GATE_W, D_MODEL), w_in_g.dtype),
                            w_in_g[0], w_in_g[1], w_in_g[2][:cut]], axis=0)


def _w_in_unlayout(dw):
    cut = MIX_W - 2 * W_IN_SHARD
    return jnp.stack([dw[MIX_W:MIX_W + W_IN_SHARD], dw[MIX_W + W_IN_SHARD:MIX_W + 2 * W_IN_SHARD],
                      jnp.concatenate([dw[MIX_W + 2 * W_IN_SHARD:], dw[:W_IN_SHARD - cut]], axis=0),
                      dw[W_IN_SHARD - cut:GATE_W]], axis=0)


def _device_step(x, tgt, norm_mix, norm_ffn, norm_final, mixer_params, w_in_p, wb_g, wo_g, wgu_g, wd_g):
    T = x.shape[0]
    tables = _rope_tables(T)
    saved = []
    for l in range(DEPTH):
        lw = dict(w_in=w_in_p[l], w_branch=wb_g[l], w_out=wo_g[l], w_gate_up=wgu_g[l], w_down=wd_g[l],
                  norm_mix=norm_mix[l][None], norm_ffn=norm_ffn[l][None], mixer=mixer_params[l], after=jnp.zeros((8, 128), F32))
        x, sv = _fwd_layer(l, x, lw, tables)
        saved.append((lw, sv))
    dx, dnf, loss = _final_loss(x, norm_final[None], tgt, 256, "final_loss")
    grads = [None] * DEPTH
    for l in reversed(range(DEPTH)):
        lw, sv = saved[l]
        dxm, g_ffn = _bwd_layer_ffn(l, dx, lw, sv)
        dx, g_mix = _bwd_layer_mix(l, dxm, lw, sv, tables)
        raw = {**g_ffn, **g_mix}
        grads[l] = {**raw, **_small_views(raw)}
    return loss, dx, dnf[0], grads


MIX_BLOCK = 256


def _fwd_layer(l, x, lw, tables):
    return _fwd_layer_rest(l, x, _fwd_layer_mix(l, x, lw, tables), lw)


def _fwd_layer_mix(l, x, lw, tables, between=None):
    proj, xn = _rms_mm(x, lw["norm_mix"], lw["w_in"], min(x.shape[0], 1024), 2176, f"proj{l}")
    if between is not None:
        between(proj, lw)
    return proj, xn, _mixers_fwd(proj, *tables, lw["mixer"], MIX_BLOCK, f"mixers_fwd{l}")


def _fwd_layer_rest(l, x, mixed, lw, between=None):
    proj, xn, ys = mixed
    TM = min(x.shape[0], 1024)
    xm, merged = _merge_fwd(x, ys, proj, lw["w_branch"], lw["w_out"], min(x.shape[0], 512), f"merge_fwd{l}")
    if between is not None:
        between(xm, lw)
    gu, hn = _rms_mm(xm, lw["norm_ffn"], lw["w_gate_up"], TM, GU_SHARD, f"ffn_up{l}")
    x_out = _ffn_down(xm, gu, lw["w_down"], 256, f"ffn_down{l}")
    return x_out, (x, proj, xn, ys, xm, merged, gu, hn)


def _bwd_layer_ffn(l, dx, lw, sv):
    x_in, proj, xn, ys, xm, merged, gu, hn = sv
    T = dx.shape[0]
    tkk = min(T, 1024)
    gk = T // tkk
    dgu, act = _swiglu_bwd(dx, gu, lw["w_down"], 256, f"swiglu_bwd{l}", lw["after"])
    d_wd = _mm_tn(act, dx, (2, 1, gk), (tkk, D_FF // 2), lambda i, j, k: (k, i), (tkk, D_MODEL), lambda i, j, k: (k, 0),
                  (D_FF, D_MODEL), (D_FF // 2, D_MODEL), lambda i, j, k: (i, 0), f"dw_down{l}")
    d_wgu = _mm_tn(hn, dgu, (1, N_CHIPS, gk), (tkk, D_MODEL), lambda i, j, k: (k, 0), (tkk, GU_SHARD), lambda i, j, k: (k, j),
                   (N_CHIPS, D_MODEL, GU_SHARD), (None, D_MODEL, GU_SHARD), lambda i, j, k: (j, 0, 0), f"dw_gate_up{l}")
    dxm, d_nffn = _mm_nt_rmsbwd(dgu, lw["w_gate_up"], xm, lw["norm_ffn"], dx, min(T, 1024), GU_SHARD, f"ffn_up_bwd{l}")
    dys, dbr, dproj = _merge_bwd(dxm, ys, proj, lw["w_branch"], lw["w_out"], 256, f"merge_bwd{l}")
    d_wo = _mm_tn(merged, dxm, (2, 1, gk), (tkk, 512), lambda i, j, k: (k, i), (tkk, D_MODEL), lambda i, j, k: (k, 0),
                  (D_MODEL, D_MODEL), (512, D_MODEL), lambda i, j, k: (i, 0), f"dw_out{l}")
    d_wb = _mm_tn(ys, dbr, (N_BRANCH, 1, gk), (tkk, HALF), lambda i, j, k: (k, i), (tkk, D_MODEL), lambda i, j, k: (k, i),
                  (N_CHIPS, N_BRANCH, HALF, 256), (N_CHIPS, None, HALF, 256), lambda i, j, k: (0, i, 0, 0), f"dw_branch{l}", col_split=N_CHIPS)
    return (dxm, dys, dproj), dict(w_branch=d_wb, w_out=d_wo, w_gate_up=d_wgu, w_down=d_wd, norm_ffn=d_nffn)


def _bwd_layer_mix(l, carry, lw, sv, tables):
    dxm, dys, dproj = carry
    x_in, proj, xn, ys, xm, merged, gu, hn = sv
    T = dxm.shape[0]
    tkk = min(T, 1024)
    gk = T // tkk
    mb = _mixers_bwd(proj, dys, dproj, *tables, lw["mixer"], MIX_BLOCK, f"mixers_bwd{l}")
    dproj = mb[0]
    d_win = _mm_tn(dproj, xn, (PROJ_PAD // 2176, 1, gk), (tkk, 2176), lambda i, j, k: (k, i), (tkk, D_MODEL), lambda i, j, k: (k, 0),
                   (PROJ_PAD, D_MODEL), (2176, D_MODEL), lambda i, j, k: (i, 0), f"dw_in{l}")
    dx, d_nmix = _mm_nt_rmsbwd(dproj, lw["w_in"], x_in, lw["norm_mix"], dxm, min(T, 1024), 2176, f"proj_bwd{l}")
    return dx, dict(w_in=d_win, norm_mix=d_nmix, sg_ln_g=mb[1], sg_ln_b=mb[2], sg_w=mb[3], sg_b=mb[4], cv_w=mb[5], cv_b=mb[6],
                    cv_ln_g=mb[7], cv_ln_b=mb[8], attn_sinks=mb[9], sc_w=mb[10])


ANY = pl.BlockSpec(memory_space=pl.ANY)
BIG = ("w_in", "w_branch", "w_out", "w_gate_up", "w_down")
HALF_SHAPE = {"w_in": (2, W_IN_SHARD // 2, D_MODEL), "w_branch": (2, 1024, 256), "w_out": (2, 128, D_MODEL),
              "w_gate_up": (2, 512, GU_SHARD), "w_down": (2, 352, D_MODEL)}
NB = len(BIG)


def _place():
    x, y, c = lax.axis_index("x"), lax.axis_index("y"), lax.axis_index("c")
    chips = [(1 - x, y), (x, 1 - y), (1 - x, 1 - y)]
    return x, y, c, 2 * x + y, chips, [2 * px + py for px, py in chips]


def _remote(src, dst, ssem, rsem, dev):
    return pltpu.make_async_remote_copy(src_ref=src, dst_ref=dst, send_sem=ssem, recv_sem=rsem, device_id=dev, device_id_type=MESH)


HBM_SPEC = pl.BlockSpec(memory_space=pltpu.HBM)
SEM_SPEC = pl.BlockSpec(memory_space=pltpu.SEMAPHORE)
DATAFLOW = pltpu.SideEffectType.DATAFLOW_SIDE_EFFECTING


def _ici_ends(kind, src, land, j, c, chip, chip_ids):
    if kind == "gather":
        return src.at[c], land.at[chip, c], land.at[chip_ids[j], c]
    return src.at[chip_ids[j]], land.at[chip], land.at[chip_ids[j]]


def _ici_start(kind, srcs, land_shapes, name):
    n = len(srcs)

    def body(*refs):
        src, land = refs[:n], refs[n:2 * n]
        ssem, rsem, token = refs[2 * n], refs[2 * n + 1], refs[-1]
        x, y, c, chip, chips, chip_ids = _place()
        for k in range(n):
            for j in range(3):
                s, d, _ = _ici_ends(kind, src[k], land[k], j, c, chip, chip_ids)
                _remote(s, d, ssem.at[3 * k + j], rsem.at[3 * k + j], (*chips[j], c)).start()
        token[...] = jnp.zeros_like(token)

    sem = pltpu.SemaphoreType.DMA((3 * n,))
    outs = pl.pallas_call(
        body, name=name,
        out_shape=(sem, sem, *[pltpu.HBM(s.shape, s.dtype) for s in srcs], *[pltpu.HBM(sh, BF16) for sh in land_shapes],
                   jax.ShapeDtypeStruct((8, 128), F32)),
        in_specs=[HBM_SPEC] * (2 * n),
        out_specs=(SEM_SPEC, SEM_SPEC, *[HBM_SPEC] * (2 * n), pl.BlockSpec(memory_space=pltpu.VMEM)),
        input_output_aliases={i: 2 + i for i in range(2 * n)},
        compiler_params=pltpu.CompilerParams(has_side_effects=DATAFLOW),
    )(*[pltpu.with_memory_space_constraint(s, pltpu.HBM) for s in srcs],
      *[pltpu.with_memory_space_constraint(lax.empty(sh, BF16), pltpu.HBM) for sh in land_shapes])
    return (kind, outs[0], outs[1], list(outs[2:2 + n]), list(outs[2 + n:2 + 2 * n])), outs[-1]


def _ici_wait(handle, after, name):
    kind, ssem_in, rsem_in, srcs, lands = handle
    n = len(srcs)

    def body(*refs):
        src, land = refs[:n], refs[n:2 * n]
        ssem, rsem = refs[2 * n], refs[2 * n + 1]
        x, y, c, chip, chips, chip_ids = _place()
        for k in range(n):
            for j in range(3):
                s, _, mine = _ici_ends(kind, src[k], land[k], j, c, chip, chip_ids)
                cp = _remote(s, mine, ssem.at[3 * k + j], rsem.at[3 * k + j], (*chips[j], c))
                cp.wait_send()
                cp.wait_recv()

    outs = pl.pallas_call(
        body, name=name, out_shape=[pltpu.HBM(t.shape, t.dtype) for t in srcs + lands],
        in_specs=[HBM_SPEC] * (2 * n) + [SEM_SPEC, SEM_SPEC, ANY], out_specs=[HBM_SPEC] * (2 * n),
        input_output_aliases={i: i for i in range(2 * n)},
        compiler_params=pltpu.CompilerParams(has_side_effects=DATAFLOW),
    )(*srcs, *lands, ssem_in, rsem_in, after)
    return list(outs[:n]), list(outs[n:])


def _ag_pair(shards, lands, name):
    n = len(shards)

    def body(*refs):
        ins, outs = refs[:n], refs[2 * n:3 * n]
        token = refs[3 * n]
        s_fwd, r_fwd, s_own, r_own = refs[3 * n + 1:]
        x, y, c, chip, chips, chip_ids = _place()
        sib = (x, y, 1 - c)
        cps = []
        for k in range(n):
            cp = _remote(ins[k], outs[k].at[chip], s_own.at[k], r_own.at[k], sib)
            cp.start()
            cps.append(cp)
            for j in range(3):
                got = outs[k].at[chip_ids[j], c]
                cp = _remote(got, got, s_fwd.at[k, j], r_fwd.at[k, j], sib)
                cp.start()
                cps.append(cp)
        for k in range(n):
            _remote(ins[k], outs[k].at[chip], s_own.at[k], r_own.at[k], sib).wait_recv()
            for j in range(3):
                got = outs[k].at[chip_ids[j], 1 - c]
                _remote(got, got, s_fwd.at[k, j], r_fwd.at[k, j], sib).wait_recv()
        for cp in cps:
            cp.wait_send()
        token[...] = jnp.zeros_like(token)

    sem, sem1 = pltpu.SemaphoreType.DMA((n, 3)), pltpu.SemaphoreType.DMA((n,))
    outs = pl.pallas_call(
        body, name=name, out_shape=[jax.ShapeDtypeStruct(t.shape, t.dtype) for t in lands] + [jax.ShapeDtypeStruct((8, 128), F32)],
        in_specs=[ANY] * (2 * n), out_specs=[ANY] * n + [pl.BlockSpec(memory_space=pltpu.VMEM)],
        input_output_aliases={n + k: k for k in range(n)},
        scratch_shapes=[sem, sem, sem1, sem1], compiler_params=pltpu.CompilerParams(has_side_effects=True),
    )(*shards, *lands)
    return list(outs[:n]), outs[n]


def _forward_plan(n):
    def plan(refs, c, chip, chip_ids):
        out = []
        for k in range(n):
            shard, land = refs[k], refs[n + k]
            out.append((shard, land.at[chip], land.at[chip]))
            out += [(land.at[q, c], land.at[q, c], land.at[q, 1 - c]) for q in chip_ids]
        return out
    return plan, 4 * n


def _swap_plan(n):
    def plan(refs, c, chip, chip_ids):
        return [(refs[k].at[q, 1 - c], refs[n + k].at[q], refs[n + k].at[q]) for k in range(n) for q in range(N_CHIPS)]
    return plan, N_CHIPS * n


def _d2d_start(arrays, new_shapes, plan_n, name):
    plan, n_copies = plan_n
    n = len(arrays) + len(new_shapes)

    def body(*refs):
        ssem, rsem, token = refs[n], refs[n + 1], refs[-1]
        x, y, c, chip, _, chip_ids = _place()
        for i, (s, d, _) in enumerate(plan(refs[:n], c, chip, chip_ids)):
            _remote(s, d, ssem.at[i], rsem.at[i], (x, y, 1 - c)).start()
        token[...] = jnp.zeros_like(token)

    sem = pltpu.SemaphoreType.DMA((n_copies,))
    args = [pltpu.with_memory_space_constraint(t, pltpu.HBM) for t in arrays] + \
           [pltpu.with_memory_space_constraint(lax.empty(sh, BF16), pltpu.HBM) for sh in new_shapes]
    outs = pl.pallas_call(
        body, name=name,
        out_shape=(sem, sem, *[pltpu.HBM(t.shape, t.dtype) for t in args], jax.ShapeDtypeStruct((8, 128), F32)),
        in_specs=[HBM_SPEC] * n, out_specs=(SEM_SPEC, SEM_SPEC, *[HBM_SPEC] * n, pl.BlockSpec(memory_space=pltpu.VMEM)),
        input_output_aliases={i: 2 + i for i in range(n)},
        compiler_params=pltpu.CompilerParams(has_side_effects=DATAFLOW),
    )(*args)
    return (plan, outs[0], outs[1], list(outs[2:2 + n])), outs[-1]


def _d2d_wait(handle, after, name):
    plan, ssem_in, rsem_in, arrays = handle
    n = len(arrays)

    def body(*refs):
        ssem, rsem = refs[n], refs[n + 1]
        x, y, c, chip, _, chip_ids = _place()
        for i, (s, _, mine) in enumerate(plan(refs[:n], c, chip, chip_ids)):
            cp = _remote(s, mine, ssem.at[i], rsem.at[i], (x, y, 1 - c))
            cp.wait_send()
            cp.wait_recv()

    outs = pl.pallas_call(
        body, name=name, out_shape=[pltpu.HBM(t.shape, t.dtype) for t in arrays],
        in_specs=[HBM_SPEC] * n + [SEM_SPEC, SEM_SPEC, ANY], out_specs=[HBM_SPEC] * n,
        input_output_aliases={i: i for i in range(n)},
        compiler_params=pltpu.CompilerParams(has_side_effects=DATAFLOW),
    )(*arrays, ssem_in, rsem_in, after)
    return list(outs)


def _rs_pair(grads, name):
    n_arr = len(grads)

    def body(*refs):
        ins, got = refs[:n_arr], refs[n_arr:2 * n_arr]
        ssem, rsem = refs[2 * n_arr:]
        x, y, c, _, _, _ = _place()
        sib = (x, y, 1 - c)
        sends = []
        for k in reversed(range(n_arr)):
            for q in range(N_CHIPS):
                cp = _remote(ins[k].at[q, 1 - c], got[k].at[q], ssem.at[k, q], rsem.at[k, q], sib)
                cp.start()
                sends.append(cp)
        for k in range(n_arr):
            for q in range(N_CHIPS):
                _remote(got[k].at[q], got[k].at[q], ssem.at[k, q], rsem.at[k, q], sib).wait_recv()
        for cp in sends:
            cp.wait_send()

    shp = [jax.ShapeDtypeStruct((N_CHIPS,) + g.shape[2:], BF16) for g in grads]
    sem = pltpu.SemaphoreType.DMA((n_arr, N_CHIPS))
    outs = pl.pallas_call(
        body, name=name, out_shape=shp, in_specs=[ANY] * n_arr, out_specs=[ANY] * n_arr,
        scratch_shapes=[sem, sem], compiler_params=pltpu.CompilerParams(has_side_effects=True),
    )(*grads)
    return list(outs)


def _rs_share(bufs, name):
    n = len(bufs)

    def body(*refs):
        outs = refs[n:2 * n]
        ssem, rsem = refs[2 * n:]
        x, y, c, _, _, _ = _place()
        sib = (x, y, 1 - c)
        sends = []
        for k in range(n):
            for l in range(DEPTH):
                cp = _remote(outs[k].at[l, c], outs[k].at[l, c], ssem.at[k, l], rsem.at[k, l], sib)
                cp.start()
                sends.append(cp)
        for k in range(n):
            for l in range(DEPTH):
                dst = outs[k].at[l, 1 - c]
                _remote(dst, dst, ssem.at[k, l], rsem.at[k, l], sib).wait_recv()
        for cp in sends:
            cp.wait_send()

    sem = pltpu.SemaphoreType.DMA((n, DEPTH))
    outs = pl.pallas_call(
        body, name=name, out_shape=[jax.ShapeDtypeStruct(b.shape, b.dtype) for b in bufs], in_specs=[ANY] * n, out_specs=[ANY] * n,
        input_output_aliases={k: k for k in range(n)},
        scratch_shapes=[sem, sem], compiler_params=pltpu.CompilerParams(has_side_effects=True),
    )(*bufs)
    return list(outs)


def _piece(src, idx, rows, width=128, align=1, transposed=False):
    return dict(src=src, idx=idx, rows=rows, width=width, align=align, transposed=transposed)


def _all_reduce_pieces(inputs, pieces, out_shapes, writes, name):
    n_in, n_out = len(inputs), len(out_shapes)
    offs, R = [], 0
    for p in pieces:
        R = -(-R // p["align"]) * p["align"]
        offs.append(R)
        R += p["rows"]
    R = -(-R // 8) * 8

    def body(*refs):
        ins, outs, token_ref = refs[:n_in], refs[n_in:n_in + n_out], refs[n_in + n_out]
        pair_ref, chip_ref, sum_ref, ssem, rsem = refs[n_in + n_out + 1:]
        token_ref[...] = jnp.zeros_like(token_ref)
        x, y, c, chip, chips, chip_ids = _place()
        pair_ref[c] = jnp.zeros((R, 128), F32)
        for p, off in zip(pieces, offs):
            v = ins[p["src"]][...].T[p["idx"]] if p["transposed"] else ins[p["src"]][p["idx"]]
            pair_ref[c, off:off + p["rows"], 0:p["width"]] = v
        mine = _remote(pair_ref.at[c], pair_ref.at[c], ssem.at[3], rsem.at[3], (x, y, 1 - c))
        mine.start()
        _remote(pair_ref.at[1 - c], pair_ref.at[1 - c], ssem.at[3], rsem.at[3], (x, y, 1 - c)).wait_recv()
        chip_ref[chip] = pair_ref[0] + pair_ref[1]
        cps = [_remote(chip_ref.at[chip], chip_ref.at[chip], ssem.at[j], rsem.at[j], (*chips[j], c)) for j in range(3)]
        for cp in cps:
            cp.start()
        for j in range(3):
            slot = chip_ref.at[chip_ids[j]]
            _remote(slot, slot, ssem.at[j], rsem.at[j], (*chips[j], c)).wait_recv()
        acc = chip_ref[0]
        for s in range(1, N_CHIPS):
            acc = acc + chip_ref[s]
        sum_ref[...] = acc
        for o, idx, p in writes:
            outs[o][idx] = sum_ref[offs[p]:offs[p] + pieces[p]["rows"], 0:pieces[p]["width"]]
        for cp in cps + [mine]:
            cp.wait_send()

    vm = pl.BlockSpec(memory_space=pltpu.VMEM)
    outs = pl.pallas_call(
        body, name=name, out_shape=[jax.ShapeDtypeStruct(s, F32) for s in out_shapes] + [jax.ShapeDtypeStruct((8, 128), F32)],
        in_specs=[vm] * n_in, out_specs=[vm] * (n_out + 1),
        scratch_shapes=[pltpu.VMEM((2, R, 128), F32), pltpu.VMEM((N_CHIPS, R, 128), F32), pltpu.VMEM((R, 128), F32),
                        pltpu.SemaphoreType.DMA((4,)), pltpu.SemaphoreType.DMA((4,))],
        compiler_params=pltpu.CompilerParams(vmem_limit_bytes=VMEM_LIMIT),
    )(*inputs)
    return list(outs[:n_out]), outs[n_out]


def _lanes(width):
    return [slice(k, min(k + 128, width)) for k in range(0, width, 128)]


def _gather_small_weights(cvw_z, scw_z):
    pieces, writes = [], []
    for i, arr in enumerate((cvw_z, scw_z)):
        for l in range(DEPTH):
            for ln in _lanes(HALF):
                writes.append((i, (l, slice(None), ln), len(pieces)))
                pieces.append(_piece(i, (l, slice(None), ln), arr.shape[1], align=8))
    (cvw, scw), tok = _all_reduce_pieces([cvw_z, scw_z], pieces, [cvw_z.shape, scw_z.shape], writes, "ag_small")
    return cvw, scw, tok


SMALL_RAW = dict(norm_mix=(1, D_MODEL), norm_ffn=(1, D_MODEL), sg_ln_g=(1, HALF), sg_ln_b=(1, HALF), cv_b=(1, HALF), cv_ln_g=(1, HALF),
                 cv_ln_b=(1, HALF))


def _all_reduce_small_grads(raw, d_nfinal, loss):
    names = list(SMALL_RAW) + ["attn_sinks", "sg_b", "sc_w", "cv_w", "sg_w"]
    out_shape = dict(norm_mix=(DEPTH, D_MODEL), norm_ffn=(DEPTH, D_MODEL), sg_ln_g=(DEPTH, HALF), sg_ln_b=(DEPTH, HALF), cv_b=(DEPTH, HALF),
                     cv_ln_g=(DEPTH, HALF), cv_ln_b=(DEPTH, HALF), attn_sinks=(DEPTH, N_Q_HEADS), sg_b=(DEPTH, SG_GROUPS, SG_CHUNK),
                     sc_w=(DEPTH, SC_KERNEL, HALF), cv_w=(DEPTH, CV_KERNEL, HALF), sg_w=(DEPTH, SG_GROUPS, SG_CHUNK, SG_CHUNK))
    inputs, pieces, writes = [], [], []

    def add(src, idx, rows, out, out_idx, **kw):
        writes.append((names.index(out) if out in names else out, out_idx, len(pieces)))
        pieces.append(_piece(src, idx, rows, **kw))

    for l in range(DEPTH):
        row = slice(l, l + 1)
        for n, (_, width) in SMALL_RAW.items():
            inputs.append(raw[l][n])
            for ln in _lanes(width):
                add(len(inputs) - 1, (slice(0, 1), ln), 1, n, (row, ln))
        inputs.append(raw[l]["attn_sinks"])
        add(len(inputs) - 1, (slice(0, 1), slice(0, N_Q_HEADS)), 1, "attn_sinks", (row, slice(None)), width=N_Q_HEADS)
    for l in range(DEPTH):
        inputs.append(raw[l]["sg_b"])
        add(len(inputs) - 1, (slice(0, SG_GROUPS), slice(None)), SG_GROUPS, "sg_b", (l,), align=8, transposed=True)
        inputs.append(raw[l]["sc_w"])
        for ln in _lanes(HALF):
            add(len(inputs) - 1, (slice(0, SC_KERNEL), ln), SC_KERNEL, "sc_w", (l, slice(None), ln), align=8)
        inputs.append(raw[l]["cv_w"])
        for ln in _lanes(HALF):
            add(len(inputs) - 1, (slice(0, CV_KERNEL), ln), CV_KERNEL, "cv_w", (l, slice(None), ln), align=8)
        inputs.append(raw[l]["sg_w"])
        for g in range(SG_GROUPS):
            add(len(inputs) - 1, (g,), SG_CHUNK, "sg_w", (l, g), align=8)
    n_names = len(names)
    inputs.append(d_nfinal)
    for ln in _lanes(D_MODEL):
        add(len(inputs) - 1, (slice(0, 1), ln), 1, n_names, (slice(0, 1), ln))
    inputs.append(loss)
    add(len(inputs) - 1, (slice(0, 1), slice(None)), 1, n_names + 1, (slice(0, 1), slice(None)))
    outs, tok = _all_reduce_pieces(inputs, pieces, [out_shape[n] for n in names] + [(1, D_MODEL), (1, 128)], writes, "ar_small")
    return dict(zip(names, outs[:n_names])), outs[n_names], outs[n_names + 1], tok


def _small_views(raw):
    v = {n: raw[n][0] for n in SMALL_RAW}
    v.update(sg_w=raw["sg_w"], sg_b=raw["sg_b"][:, :SG_GROUPS].T, cv_w=raw["cv_w"][:CV_KERNEL],
             attn_sinks=raw["attn_sinks"][0, :N_Q_HEADS], sc_w=raw["sc_w"][:SC_KERNEL])
    return v


def _row_tile(rows, cols, n_arrays):
    budget = 20 * 1024 * 1024 // (n_arrays * 2 * cols * 4)
    tiles = [t for t in range(16, min(rows, budget) + 1, 16) if rows % t == 0]
    assert tiles, (rows, cols)
    return tiles[-1]


def _add_pairs(g, got, place, name):
    _, _, rows, cols = g.shape
    tr = _row_tile(rows, cols, 3)

    def body(place_ref, a_ref, b_ref, o_ref):
        del place_ref
        o_ref[...] = (a_ref[...].astype(F32) + b_ref[...].astype(F32)).astype(BF16)

    spec = pl.BlockSpec((None, tr, cols), lambda q, i, p: (q, i, 0))
    grid_spec = pltpu.PrefetchScalarGridSpec(
        num_scalar_prefetch=1, grid=(N_CHIPS, rows // tr),
        in_specs=[pl.BlockSpec((None, None, tr, cols), lambda q, i, p: (q, p[1], i, 0)), spec], out_specs=spec)
    return pl.pallas_call(body, name=name, grid_spec=grid_spec, out_shape=jax.ShapeDtypeStruct((N_CHIPS, rows, cols), BF16),
                          compiler_params=_params("parallel", "parallel"))(place, g, got)


def _sum_chips(own, recv, place, l, buf, name, after):
    _, rows, cols = own.shape
    tr = _row_tile(rows, cols, 4)

    def body(place_ref, own_ref, recv_ref, *rest):
        chip = place_ref[0]
        acc = own_ref[...].astype(F32)
        for j in range(1, N_CHIPS):
            acc = acc + recv_ref[lax.rem(chip + j, N_CHIPS)].astype(F32)
        rest[-1][...] = acc

    in_specs = [pl.BlockSpec((None, tr, cols), lambda i, p: (p[0], i, 0)), pl.BlockSpec((N_CHIPS, tr, cols), lambda i, p: (0, i, 0)), ANY]
    args = [place, own, recv, after]
    aliases = {}
    if buf is not None:
        in_specs.append(ANY)
        args.append(buf)
        aliases = {4: 0}
    grid_spec = pltpu.PrefetchScalarGridSpec(
        num_scalar_prefetch=1, grid=(rows // tr,), in_specs=in_specs,
        out_specs=pl.BlockSpec((None, None, tr, cols), lambda i, p: (l, p[1], i, 0)))
    return pl.pallas_call(body, name=name, grid_spec=grid_spec, out_shape=jax.ShapeDtypeStruct((DEPTH, 2, rows, cols), F32),
                          input_output_aliases=aliases, compiler_params=_params("parallel"))(*args)


def _adamw(w, g, m, v, name):
    shape = w.shape
    lead, (rows, cols) = shape[:-2], shape[-2:]
    tr = _row_tile(rows, cols, 7)

    def body(w_ref, g_ref, m_ref, v_ref, d_ref, mo_ref, vo_ref):
        gv = g_ref[...]
        mn = ADAM_B1 * m_ref[...] + (1.0 - ADAM_B1) * gv
        vn = ADAM_B2 * v_ref[...] + (1.0 - ADAM_B2) * (gv * gv)
        m_hat = mn / (1.0 - ADAM_B1 ** ADAM_STEP)
        v_hat = vn / (1.0 - ADAM_B2 ** ADAM_STEP)
        d_ref[...] = -ADAM_LR * (m_hat / (jnp.sqrt(v_hat) + ADAM_EPS) + ADAM_WD * w_ref[...])
        mo_ref[...] = mn
        vo_ref[...] = vn

    spec = pl.BlockSpec((None,) * len(lead) + (tr, cols), lambda *idx: (*idx, 0))
    grid = lead + (rows // tr,)
    return list(pl.pallas_call(body, name=name, grid=grid, in_specs=[spec] * 4, out_specs=[spec] * 3,
                               out_shape=[jax.ShapeDtypeStruct(shape, F32)] * 3,
                               compiler_params=_params(*(["parallel"] * len(grid))))(w, g, m, v))


def _adamw_small(ws, gs, ms, vs, name):
    n = len(ws)

    def body(*refs):
        for i in range(n):
            gv = refs[n + i][...]
            mn = ADAM_B1 * refs[2 * n + i][...] + (1.0 - ADAM_B1) * gv
            vn = ADAM_B2 * refs[3 * n + i][...] + (1.0 - ADAM_B2) * (gv * gv)
            m_hat = mn / (1.0 - ADAM_B1 ** ADAM_STEP)
            v_hat = vn / (1.0 - ADAM_B2 ** ADAM_STEP)
            refs[4 * n + i][...] = -ADAM_LR * (m_hat / (jnp.sqrt(v_hat) + ADAM_EPS) + ADAM_WD * refs[i][...])
            refs[5 * n + i][...] = mn
            refs[6 * n + i][...] = vn

    vm = pl.BlockSpec(memory_space=pltpu.VMEM)
    outs = pl.pallas_call(body, name=name, out_shape=[jax.ShapeDtypeStruct(t.shape, F32) for t in ws] * 3,
                          in_specs=[vm] * (4 * n), out_specs=[vm] * (3 * n),
                          compiler_params=pltpu.CompilerParams(vmem_limit_bytes=VMEM_LIMIT))(*ws, *gs, *ms, *vs)
    return outs[:n], outs[n:2 * n], outs[2 * n:]


SMALL = ("norm_mix", "sg_ln_g", "sg_ln_b", "sg_w", "sg_b", "cv_w", "cv_b", "cv_ln_g", "cv_ln_b", "attn_sinks", "sc_w", "norm_ffn", "norm_final")
ORDER = ("norm_mix", "w_in", "sg_ln_g", "sg_ln_b", "sg_w", "sg_b", "cv_w", "cv_b", "cv_ln_g", "cv_ln_b", "attn_sinks", "sc_w",
         "w_branch", "w_out", "norm_ffn", "w_gate_up", "w_down", "norm_final")


def kernel(x, norm_mix, w_in, sg_ln_g, sg_ln_b, sg_w, sg_b, cv_w, cv_b, cv_ln_g, cv_ln_b, attn_sinks, sc_w, w_branch, w_out, norm_ffn, w_gate_up, w_down, norm_final, loss_target, m_norm_mix, m_w_in, m_sg_ln_g, m_sg_ln_b, m_sg_w, m_sg_b, m_cv_w, m_cv_b, m_cv_ln_g, m_cv_ln_b, m_attn_sinks, m_sc_w, m_w_branch, m_w_out, m_norm_ffn, m_w_gate_up, m_w_down, m_norm_final, v_norm_mix, v_w_in, v_sg_ln_g, v_sg_ln_b, v_sg_w, v_sg_b, v_cv_w, v_cv_b, v_cv_ln_g, v_cv_ln_b, v_attn_sinks, v_sc_w, v_w_branch, v_w_out, v_norm_ffn, v_w_gate_up, v_w_down, v_norm_final):
    W = dict(norm_mix=norm_mix, w_in=w_in, sg_ln_g=sg_ln_g, sg_ln_b=sg_ln_b, sg_w=sg_w, sg_b=sg_b, cv_w=cv_w, cv_b=cv_b, cv_ln_g=cv_ln_g,
             cv_ln_b=cv_ln_b, attn_sinks=attn_sinks, sc_w=sc_w, w_branch=w_branch, w_out=w_out, norm_ffn=norm_ffn, w_gate_up=w_gate_up,
             w_down=w_down, norm_final=norm_final)
    M = dict(norm_mix=m_norm_mix, w_in=m_w_in, sg_ln_g=m_sg_ln_g, sg_ln_b=m_sg_ln_b, sg_w=m_sg_w, sg_b=m_sg_b, cv_w=m_cv_w, cv_b=m_cv_b,
             cv_ln_g=m_cv_ln_g, cv_ln_b=m_cv_ln_b, attn_sinks=m_attn_sinks, sc_w=m_sc_w, w_branch=m_w_branch, w_out=m_w_out,
             norm_ffn=m_norm_ffn, w_gate_up=m_w_gate_up, w_down=m_w_down, norm_final=m_norm_final)
    V = dict(norm_mix=v_norm_mix, w_in=v_w_in, sg_ln_g=v_sg_ln_g, sg_ln_b=v_sg_ln_b, sg_w=v_sg_w, sg_b=v_sg_b, cv_w=v_cv_w, cv_b=v_cv_b,
             cv_ln_g=v_cv_ln_g, cv_ln_b=v_cv_ln_b, attn_sinks=v_attn_sinks, sc_w=v_sc_w, w_branch=v_w_branch, w_out=v_w_out,
             norm_ffn=v_norm_ffn, w_gate_up=v_w_gate_up, w_down=v_w_down, norm_final=v_norm_final)
    mx, my, mc = lax.axis_index("x"), lax.axis_index("y"), lax.axis_index("c")
    chip = 2 * mx + my

    place = jnp.stack([chip, mc]).astype(jnp.int32)
    tables = _rope_tables(x.shape[1])
    land_shapes = [(N_CHIPS,) + HALF_SHAPE[n] for n in BIG]
    part_shapes = {n: (N_CHIPS,) + HALF_SHAPE[n][1:] for n in BIG}

    T_ = lambda t: jnp.swapaxes(t, 1, 2)
    Wt, Mt, Vt = ({**t, "w_in": T_(t["w_in"])} for t in (W, M, V))

    def shards_of(l, tok):
        return [(Wt[n][l] + tok[0, 0]).astype(BF16).reshape(HALF_SHAPE[n]) for n in BIG]

    def finish_gather(tag, handle, after):
        srcs, lands = _ici_wait(handle, after, f"ag_wait{tag}")
        return _ag_pair(srcs, lands, f"ag_pair{tag}")[0]

    def mix_weights(l, g_in):
        return dict(w_in=_w_in_layout(g_in[0].reshape(N_CHIPS, W_IN_SHARD, D_MODEL)), norm_mix=norm_mix[l][None], norm_ffn=norm_ffn[l][None],
                    mixer=_mixer_params(l, sg_ln_g, sg_ln_b, sg_w, sg_b, cvw_full, cv_b, cv_ln_g, cv_ln_b, attn_sinks, scw_full))

    def rest_weights(lw, g_rest):
        G = dict(zip(BIG[1:], g_rest))
        lw.update(w_branch=G["w_branch"].reshape(N_CHIPS, N_BRANCH, HALF, 256), w_out=G["w_out"].reshape(D_MODEL, D_MODEL),
                  w_gate_up=G["w_gate_up"].reshape(N_CHIPS, D_MODEL, GU_SHARD), w_down=G["w_down"].reshape(D_FF, D_MODEL))

    def shard_major(g):
        t = dict(g)
        if "w_in" in t:
            t["w_in"] = _w_in_unlayout(t["w_in"])
        return {n: t[n].reshape((N_CHIPS,) + HALF_SHAPE[n]) for n in BIG if n in t}

    def pair_sums(tag, g):
        names = list(g)
        got = _rs_pair([g[n] for n in names], f"rs_pair{tag}")
        return names, [_add_pairs(g[n], got[k], place, f"rs_add{tag}_{n}") for k, n in enumerate(names)]

    zero_tok = jnp.zeros((8, 128), F32)
    south = (mc == 0).astype(F32)
    cvw_z = lax.dynamic_update_slice(jnp.zeros((DEPTH, CV_KERNEL, HALF), F32), cv_w * south, (0, 0, chip * 128))
    scw_z = lax.dynamic_update_slice(jnp.zeros((DEPTH, SC_KERNEL, HALF), F32), sc_w * south, (0, 0, chip * 128))
    cvw_full, scw_full, tok = _gather_small_weights(cvw_z, scw_z)

    handles = []
    for l in range(DEPTH):
        for tag, sl in (("in", slice(0, 1)), ("rest", slice(1, NB))):
            h, tok = _ici_start("gather", shards_of(l, tok)[sl], land_shapes[sl], f"ag_start{l}{tag}")
            handles.append(h)
    pending = {}

    def behind(l, key):
        def order(lw, token):
            if key == "mixer":
                lw["mixer"] = [lw["mixer"][0] + token[0, 0]] + lw["mixer"][1:]
            else:
                lw[key] = lw[key] + token[0, 0]
        return order

    def early_pair(tag, handle, order):
        def between(after, lw):
            srcs, lands = _ici_wait(handle, after, f"ag_wait{tag}")
            pending[tag], token = _d2d_start(srcs + lands, [], _forward_plan(len(srcs)), f"ag_pair_start{tag}")
            order(lw, token)
        return between

    def finish_pair(tag, after):
        arrays = _d2d_wait(pending.pop(tag), after, f"ag_pair_wait{tag}")
        return arrays[len(arrays) // 2:]

    lw0 = mix_weights(0, finish_gather("0in", handles[0], tok))
    mixed = _fwd_layer_mix(0, x[0], lw0, tables)
    rest_weights(lw0, finish_gather("0rest", handles[1], mixed[2]))
    x1, sv0 = _fwd_layer_rest(0, x[0], mixed, lw0, early_pair("1in", handles[2], behind(0, "norm_ffn")))
    lw1 = mix_weights(1, finish_pair("1in", x1))
    mixed = _fwd_layer_mix(1, x1, lw1, tables, early_pair("1rest", handles[3], behind(1, "mixer")))
    rest_weights(lw1, finish_pair("1rest", mixed[2]))
    x2, sv1 = _fwd_layer_rest(1, x1, mixed, lw1)
    dx, d_nfinal, loss = _final_loss(x2, norm_final[None], loss_target[0], 256, "final_loss")

    lw1["after"] = zero_tok
    carry, g_ffn1 = _bwd_layer_ffn(1, dx, lw1, sv1)
    g1 = shard_major(g_ffn1)
    names_f = list(g1)
    h_swap, tok = _d2d_start([g1[n] for n in names_f], [part_shapes[n] for n in names_f], _swap_plan(len(names_f)), "rs_pair_start1")
    behind(1, "mixer")(lw1, tok)
    dx, g_mix1 = _bwd_layer_mix(1, carry, lw1, sv1, tables)
    swapped = _d2d_wait(h_swap, dx, "rs_pair_wait1")
    g1_in = shard_major(g_mix1)
    names1 = list(g1_in) + names_f
    own1 = [g1_in[n] for n in g1_in] + swapped[:len(names_f)]
    got1 = _rs_pair([g1_in[n] for n in g1_in], "rs_pair1") + swapped[len(names_f):]
    part1 = [_add_pairs(own1[k], got1[k], place, f"rs_add1_{n}") for k, n in enumerate(names1)]
    hr1, tok = _ici_start("scatter", part1, [part_shapes[n] for n in names1], "rs_start1")

    lw0["after"] = tok
    carry, g_ffn0 = _bwd_layer_ffn(0, dx, lw0, sv0)
    names_a, part_a = pair_sums("0a", shard_major(g_ffn0))
    _, recv1 = _ici_wait(hr1, part_a[0], "rs_wait1")
    hra, tok = _ici_start("scatter", part_a, [part_shapes[n] for n in names_a], "rs_start0a")

    lw0["mixer"] = [lw0["mixer"][0] + tok[0, 0]] + lw0["mixer"][1:]
    dx, g_mix0 = _bwd_layer_mix(0, carry, lw0, sv0, tables)
    _, recv_a = _ici_wait(hra, dx, "rs_wait0a")

    small_red, nf_red, loss_red, tok = _all_reduce_small_grads([{**g_ffn0, **g_mix0}, {**g_ffn1, **g_mix1}], d_nfinal, loss)
    small_red["norm_final"] = nf_red
    loss_out = loss_red[0, 0]
    for n in ("cv_w", "sc_w"):
        small_red[n] = lax.dynamic_slice_in_dim(small_red[n], chip * 128, 128, axis=2)

    g_mix0["w_in"] = g_mix0["w_in"] + tok[0, 0].astype(BF16)
    names_b, part_b = pair_sums("0b", shard_major(g_mix0))
    hrb, tok = _ici_start("scatter", part_b, [part_shapes[n] for n in names_b], "rs_start0b")
    bufs = {n: _sum_chips(part1[k], recv1[k], place, 1, None, f"rs_sum1_{n}", tok) for k, n in enumerate(names1)}
    for k, n in enumerate(names_a):
        bufs[n] = _sum_chips(part_a[k], recv_a[k], place, 0, bufs[n], f"rs_sum0_{n}", tok)
    shared = dict(zip(names_a, _rs_share([bufs[n] for n in names_a], "rs_share_a")))
    upd = {}
    for n in names_a:
        red = shared[n].reshape(W[n].shape)
        upd[n] = [red] + _adamw(W[n], red, M[n], V[n], f"adamw_{n}")
    two_d = lambda t: t[None] if t.ndim == 1 else t
    small_upd = _adamw_small(*([two_d(t[n]) for n in SMALL] for t in (W, small_red, M, V)), "adamw_small")
    for n, d, mo, vo in zip(SMALL, *small_upd):
        upd[n] = [t.reshape(W[n].shape) for t in (small_red[n], d, mo, vo)]

    _, recv_b = _ici_wait(hrb, upd[names_a[-1]][1], "rs_wait0b")
    for k, n in enumerate(names_b):
        bufs[n] = _sum_chips(part_b[k], recv_b[k], place, 0, bufs[n], f"rs_sum0_{n}", tok)
    shared = dict(zip(names_b, _rs_share([bufs[n] for n in names_b], "rs_share_b")))
    for n in names_b:
        red = shared[n].reshape(Wt[n].shape)
        upd[n] = [T_(t) for t in [red] + _adamw(Wt[n], red, Mt[n], Vt[n], f"adamw_{n}")]

    out = [loss_out, dx[None]]
    for k in range(4):
        out += [upd[n][k] for n in ORDER]
    return tuple(out)
```

```python
import functools
import math

import jax
import jax.numpy as jnp
from jax import lax
from jax.experimental import pallas as pl
from jax.experimental.pallas import tpu as pltpu

F32 = jnp.float32
BF16 = jnp.bfloat16

D_MODEL = 1024
DEPTH = 2
HALF = 512
SG_CHUNK = 128
SG_GROUPS = 4
CV_KERNEL = 31
HEAD_DIM = 64
N_Q_HEADS = 8
N_KV_HEADS = 2
Q_PER_KV = N_Q_HEADS // N_KV_HEADS
WINDOW = 128
ROPE_THETA = 10000.0
SC_KERNEL = 3
N_BRANCH = 4
D_FF = 2816
EPS = 1e-6
N_CHIPS = 4
N_DEV = 8

MIX_W = 4352
GATE_W = N_BRANCH * D_MODEL
PROJ_PAD = 2 * MIX_W
W_IN_SHARD = 2112
GU_SHARD = 1408
HALO = 128
CV_PAD = 32

ADAM_LR = 0.001
ADAM_B1 = 0.9
ADAM_B2 = 0.999
ADAM_EPS = 1e-08
ADAM_WD = 0.01
ADAM_STEP = 10

VMEM_LIMIT = 56 * 1024 * 1024
INV_SQRT2 = 1.0 / math.sqrt(2.0)
INV_SQRT_2PI = 1.0 / math.sqrt(2.0 * math.pi)
NEG_BIG = -1e30
MESH = pl.DeviceIdType.MESH

C_ZA, C_ZB, C_Q, C_K, C_V, C_ZD = 0, 1024, 2048, 2560, 2688, 2816


def _params(*sem):
    return pltpu.CompilerParams(dimension_semantics=sem, vmem_limit_bytes=VMEM_LIMIT)


def _sig(v):
    return 1.0 / (1.0 + jnp.exp(-v))


def _dot(a, b):
    return jnp.dot(a, b, preferred_element_type=F32)


def _dot_nt(a, b):
    return lax.dot_general(a, b, (((1,), (1,)), ((), ())), preferred_element_type=F32)


def _dot_tn(a, b):
    return lax.dot_general(a, b, (((0,), (0,)), ((), ())), preferred_element_type=F32)


def _full(shape):
    nd = len(shape)
    return pl.BlockSpec(shape, lambda *_: (0,) * nd)


def _rms_mm(x, g, w, tm, tn, name):
    T = x.shape[0]
    transposed = w.ndim == 2
    if transposed:
        N = w.shape[0]
        wspec = pl.BlockSpec((tn, D_MODEL), lambda i, j: (j, 0))
    else:
        tn = w.shape[2]
        N = w.shape[0] * tn
        wspec = pl.BlockSpec((None, D_MODEL, tn), lambda i, j: (j, 0, 0))

    def body(x_ref, g_ref, w_ref, o_ref, xn_ref):
        @pl.when(pl.program_id(1) == 0)
        def _():
            xv = x_ref[...]
            r = lax.rsqrt(jnp.mean(xv * xv, axis=-1, keepdims=True) + EPS)
            xn_ref[...] = (xv * r * g_ref[...]).astype(BF16)

        o_ref[...] = (_dot_nt if transposed else _dot)(xn_ref[...], w_ref[...]).astype(BF16)

    return pl.pallas_call(
        body, name=name, grid=(T // tm, N // tn),
        in_specs=[pl.BlockSpec((tm, D_MODEL), lambda i, j: (i, 0)), _full((1, D_MODEL)), wspec],
        out_specs=[pl.BlockSpec((tm, tn), lambda i, j: (i, j)), pl.BlockSpec((tm, D_MODEL), lambda i, j: (i, 0))],
        out_shape=[jax.ShapeDtypeStruct((T, N), BF16), jax.ShapeDtypeStruct((T, D_MODEL), BF16)],
        compiler_params=_params("parallel", "arbitrary"),
    )(x, g, w)


def _merge_fwd(x, ys, proj, wb, wo, tm, name):
    T = x.shape[0]

    def body(x_ref, ys_ref, zg_ref, wb_ref, wo_ref, xo_ref, mg_ref):
        merged = None
        for n in range(N_BRANCH):
            yn = ys_ref[:, n * HALF:(n + 1) * HALF]
            br = jnp.concatenate([_dot(yn, wb_ref[s, n]) for s in range(N_CHIPS)], axis=1)
            t = _sig(zg_ref[:, n * D_MODEL:(n + 1) * D_MODEL].astype(F32)) * br
            merged = t if merged is None else merged + t
        mb = merged.astype(BF16)
        mg_ref[...] = mb
        xo_ref[...] = x_ref[...] + _dot(mb, wo_ref[...])

    return pl.pallas_call(
        body, name=name, grid=(T // tm,),
        in_specs=[pl.BlockSpec((tm, D_MODEL), lambda i: (i, 0)), pl.BlockSpec((tm, N_BRANCH * HALF), lambda i: (i, 0)),
                  pl.BlockSpec((tm, GATE_W), lambda i: (i, 0)), _full(wb.shape), _full(wo.shape)],
        out_specs=[pl.BlockSpec((tm, D_MODEL), lambda i: (i, 0)), pl.BlockSpec((tm, D_MODEL), lambda i: (i, 0))],
        out_shape=[jax.ShapeDtypeStruct((T, D_MODEL), F32), jax.ShapeDtypeStruct((T, D_MODEL), BF16)],
        compiler_params=_params("parallel"),
    )(x, ys, proj, wb, wo)


def _ffn_down(xm, gu, wd, tm, name):
    T = xm.shape[0]

    def body(x_ref, gu_ref, wd_ref, o_ref):
        g = gu_ref[:, :D_FF].astype(F32)
        u = gu_ref[:, D_FF:].astype(F32)
        act = (g * _sig(g) * u).astype(BF16)
        o_ref[...] = x_ref[...] + _dot(act, wd_ref[...])

    return pl.pallas_call(
        body, name=name, grid=(T // tm,),
        in_specs=[pl.BlockSpec((tm, D_MODEL), lambda i: (i, 0)), pl.BlockSpec((tm, 2 * D_FF), lambda i: (i, 0)), _full(wd.shape)],
        out_specs=pl.BlockSpec((tm, D_MODEL), lambda i: (i, 0)),
        out_shape=jax.ShapeDtypeStruct((T, D_MODEL), F32),
        compiler_params=_params("parallel"),
    )(xm, gu, wd)


def _final_loss(x, g, tgt, tm, name):
    T = x.shape[0]

    def body(x_ref, g_ref, t_ref, dx_ref, dg_ref, ls_ref):
        @pl.when(pl.program_id(0) == 0)
        def _():
            dg_ref[...] = jnp.zeros_like(dg_ref)
            ls_ref[...] = jnp.zeros_like(ls_ref)

        xv = x_ref[...]
        gv = g_ref[...]
        r = lax.rsqrt(jnp.mean(xv * xv, axis=-1, keepdims=True) + EPS)
        xh = xv * r
        diff = xh * gv - t_ref[...]
        ls_ref[...] += jnp.full(ls_ref.shape, 0.5 / D_MODEL, F32) * jnp.sum(diff * diff)
        dy = diff * (1.0 / D_MODEL)
        dxh = dy * gv
        dx_ref[...] = r * (dxh - xh * jnp.mean(dxh * xh, axis=-1, keepdims=True))
        dg_ref[...] += jnp.sum(dy * xh, axis=0, keepdims=True)

    return pl.pallas_call(
        body, name=name, grid=(T // tm,),
        in_specs=[pl.BlockSpec((tm, D_MODEL), lambda i: (i, 0)), _full((1, D_MODEL)), pl.BlockSpec((tm, D_MODEL), lambda i: (i, 0))],
        out_specs=[pl.BlockSpec((tm, D_MODEL), lambda i: (i, 0)), _full((1, D_MODEL)), _full((1, 128))],
        out_shape=[jax.ShapeDtypeStruct((T, D_MODEL), F32), jax.ShapeDtypeStruct((1, D_MODEL), F32), jax.ShapeDtypeStruct((1, 128), F32)],
        compiler_params=_params("arbitrary"),
    )(x, g, tgt)


def _swiglu_bwd(dx, gu, wd, tm, name, after):
    T = dx.shape[0]

    def body(dx_ref, gu_ref, wd_ref, after_ref, dgu_ref, act_ref):
        del after_ref
        dact = _dot_nt(dx_ref[...].astype(BF16), wd_ref[...])
        g = gu_ref[:, :D_FF].astype(F32)
        u = gu_ref[:, D_FF:].astype(F32)
        s = _sig(g)
        silu = g * s
        act_ref[...] = (silu * u).astype(BF16)
        dgu_ref[:, :D_FF] = (dact * u * (s + silu * (1.0 - s))).astype(BF16)
        dgu_ref[:, D_FF:] = (dact * silu).astype(BF16)

    return pl.pallas_call(
        body, name=name, grid=(T // tm,),
        in_specs=[pl.BlockSpec((tm, D_MODEL), lambda i: (i, 0)), pl.BlockSpec((tm, 2 * D_FF), lambda i: (i, 0)), _full(wd.shape),
                  pl.BlockSpec(memory_space=pl.ANY)],
        out_specs=[pl.BlockSpec((tm, 2 * D_FF), lambda i: (i, 0)), pl.BlockSpec((tm, D_FF), lambda i: (i, 0))],
        out_shape=[jax.ShapeDtypeStruct((T, 2 * D_FF), BF16), jax.ShapeDtypeStruct((T, D_FF), BF16)],
        compiler_params=_params("parallel"),
    )(dx, gu, wd, after)


def _mm_tn(a, b, grid, a_block, a_map, b_block, b_map, o_shape, o_block, o_map, name, col_split=1):
    gk = grid[2]
    tm = [d for d in a_block if d is not None][-1]
    tn = [d for d in b_block if d is not None][-1]

    def body(a_ref, b_ref, o_ref, acc_ref):
        k = pl.program_id(2)
        p = _dot_tn(a_ref[...].astype(BF16), b_ref[...].astype(BF16))

        @pl.when(k == 0)
        def _():
            acc_ref[...] = p

        @pl.when(k > 0)
        def _():
            acc_ref[...] += p

        @pl.when(k == gk - 1)
        def _():
            if col_split == 1:
                o_ref[...] = acc_ref[...].astype(o_ref.dtype)
            else:
                w = tn // col_split
                for s in range(col_split):
                    o_ref[s] = acc_ref[:, s * w:(s + 1) * w].astype(o_ref.dtype)

    return pl.pallas_call(
        body, name=name, grid=grid,
        in_specs=[pl.BlockSpec(a_block, a_map), pl.BlockSpec(b_block, b_map)],
        out_specs=pl.BlockSpec(o_block, o_map),
        out_shape=jax.ShapeDtypeStruct(o_shape, BF16),
        scratch_shapes=[pltpu.VMEM((tm, tn), F32)],
        compiler_params=_params("parallel", "parallel", "arbitrary"),
    )(a, b)


def _mm_nt_rmsbwd(a, w, x, g, dres, tm, tk, name):
    T = x.shape[0]
    transposed = w.ndim == 2
    if transposed:
        gk = w.shape[0] // tk
        wspec = pl.BlockSpec((tk, D_MODEL), lambda i, k: (k, 0))
    else:
        tk = w.shape[2]
        gk = w.shape[0]
        wspec = pl.BlockSpec((None, D_MODEL, tk), lambda i, k: (k, 0, 0))

    def body(a_ref, w_ref, x_ref, g_ref, r_ref, dx_ref, dg_ref, acc_ref):
        i, k = pl.program_id(0), pl.program_id(1)
        p = (_dot if transposed else _dot_nt)(a_ref[...], w_ref[...])

        @pl.when(k == 0)
        def _():
            acc_ref[...] = p

        @pl.when(k > 0)
        def _():
            acc_ref[...] += p

        @pl.when(jnp.logical_and(i == 0, k == 0))
        def _():
            dg_ref[...] = jnp.zeros_like(dg_ref)

        @pl.when(k == gk - 1)
        def _():
            dh = acc_ref[...]
            xv = x_ref[...]
            r = lax.rsqrt(jnp.mean(xv * xv, axis=-1, keepdims=True) + EPS)
            xh = xv * r
            dxh = dh * g_ref[...]
            dx_ref[...] = r_ref[...] + r * (dxh - xh * jnp.mean(dxh * xh, axis=-1, keepdims=True))
            dg_ref[...] += jnp.sum(dh * xh, axis=0, keepdims=True)

    return pl.pallas_call(
        body, name=name, grid=(T // tm, gk),
        in_specs=[pl.BlockSpec((tm, tk), lambda i, k: (i, k)), wspec, pl.BlockSpec((tm, D_MODEL), lambda i, k: (i, 0)),
                  _full((1, D_MODEL)), pl.BlockSpec((tm, D_MODEL), lambda i, k: (i, 0))],
        out_specs=[pl.BlockSpec((tm, D_MODEL), lambda i, k: (i, 0)), _full((1, D_MODEL))],
        out_shape=[jax.ShapeDtypeStruct((T, D_MODEL), F32), jax.ShapeDtypeStruct((1, D_MODEL), F32)],
        scratch_shapes=[pltpu.VMEM((tm, D_MODEL), F32)],
        compiler_params=_params("arbitrary", "arbitrary"),
    )(a, w, x, g, dres)


def _merge_bwd(dxm, ys, proj, wb, wo, tm, name):
    T = dxm.shape[0]

    def body(dx_ref, ys_ref, zg_ref, wb_ref, wo_ref, dys_ref, dbr_ref, dp_ref):
        dmerged = _dot_nt(dx_ref[...].astype(BF16), wo_ref[...])
        for n in range(N_BRANCH):
            yn = ys_ref[:, n * HALF:(n + 1) * HALF]
            br = jnp.concatenate([_dot(yn, wb_ref[s, n]) for s in range(N_CHIPS)], axis=1)
            gt = _sig(zg_ref[:, n * D_MODEL:(n + 1) * D_MODEL].astype(F32))
            dbr = (gt * dmerged).astype(BF16)
            dbr_ref[:, n * D_MODEL:(n + 1) * D_MODEL] = dbr
            dp_ref[:, n * D_MODEL:(n + 1) * D_MODEL] = (dmerged * br * gt * (1.0 - gt)).astype(BF16)
            dy = None
            for s in range(N_CHIPS):
                t = _dot_nt(dbr[:, s * 256:(s + 1) * 256], wb_ref[s, n])
                dy = t if dy is None else dy + t
            dys_ref[:, n * HALF:(n + 1) * HALF] = dy.astype(BF16)
        dp_ref[:, GATE_W:] = jnp.zeros((tm, MIX_W - GATE_W), BF16)

    return pl.pallas_call(
        body, name=name, grid=(T // tm,),
        in_specs=[pl.BlockSpec((tm, D_MODEL), lambda i: (i, 0)), pl.BlockSpec((tm, N_BRANCH * HALF), lambda i: (i, 0)),
                  pl.BlockSpec((tm, GATE_W), lambda i: (i, 0)), _full(wb.shape), _full(wo.shape)],
        out_specs=[pl.BlockSpec((tm, N_BRANCH * HALF), lambda i: (i, 0)), pl.BlockSpec((tm, GATE_W), lambda i: (i, 0)),
                   pl.BlockSpec((tm, MIX_W), lambda i: (i, 0))],
        out_shape=[jax.ShapeDtypeStruct((T, N_BRANCH * HALF), BF16), jax.ShapeDtypeStruct((T, GATE_W), BF16),
                   jax.ShapeDtypeStruct((T, PROJ_PAD), BF16)],
        compiler_params=_params("parallel"),
    )(dxm, ys, proj, wb, wo)


def _gelu(v):
    return 0.5 * v * (1.0 + lax.erf(v * INV_SQRT2))


def _gelu_grad(v):
    return 0.5 * (1.0 + lax.erf(v * INV_SQRT2)) + v * jnp.exp(-0.5 * v * v) * INV_SQRT_2PI


def _rot_half(t):
    w = t.shape[1]
    lane = lax.broadcasted_iota(jnp.int32, t.shape, 1)
    return jnp.where((lane % HEAD_DIM) < HEAD_DIM // 2, pltpu.roll(t, w - HEAD_DIM // 2, 1), pltpu.roll(t, HEAD_DIM // 2, 1))


def _rope(t, cos, sin_signed):
    return t * cos + _rot_half(t) * sin_signed


def _rope_t(d, cos, sin_signed):
    return d * cos + _rot_half(d * sin_signed)


def _ln_fwd(v, g, b):
    mu = jnp.mean(v, axis=-1, keepdims=True)
    vc = v - mu
    r = lax.rsqrt(jnp.mean(vc * vc, axis=-1, keepdims=True) + EPS)
    vh = vc * r
    return vh * g + b, vh, r


def _ln_bwd(dn, vh, r, g):
    dvh = dn * g
    return r * (dvh - jnp.mean(dvh, axis=-1, keepdims=True) - vh * jnp.mean(dvh * vh, axis=-1, keepdims=True))


def _sublane_shifts(sh_ref, rows):
    for b in range(1, 8):
        sh_ref[b, 0:rows - 8, :] = sh_ref[0, pl.ds(b, rows - 8), :]


def _tap(sh_ref, off, n):
    return sh_ref[off % 8, pl.ds(off - off % 8, n), :]


def _tril_mask():
    return lax.broadcasted_iota(jnp.int32, (SG_CHUNK, SG_CHUNK), 0) >= lax.broadcasted_iota(jnp.int32, (SG_CHUNK, SG_CHUNK), 1)


def _band_masks():
    shape = (Q_PER_KV * WINDOW, 2 * WINDOW)
    row = lax.broadcasted_iota(jnp.int32, shape, 0) % WINDOW
    col = lax.broadcasted_iota(jnp.int32, shape, 1)
    band = (col > row) & (col <= row + WINDOW)
    return band, band & (col >= WINDOW)


def _attn_probs(qs, kh, sink_col, valid):
    s = jnp.where(valid, _dot_nt(qs, kh) * (HEAD_DIM ** -0.5), NEG_BIG)
    m = jnp.maximum(jnp.max(s, axis=-1, keepdims=True), sink_col)
    p = jnp.exp(s - m)
    es = jnp.exp(sink_col - m)
    inv = 1.0 / (jnp.sum(p, axis=-1, keepdims=True) + es)
    return p * inv, es * inv


def _sink_col(sinks_ref, h):
    return jnp.concatenate([jnp.broadcast_to(sinks_ref[:, h * Q_PER_KV + g:h * Q_PER_KV + g + 1], (WINDOW, 1))
                            for g in range(Q_PER_KV)], axis=0)


def _mixer_in_specs(TB, nb):
    r = TB // HALO
    last = nb * r - 1
    cur = pl.BlockSpec((TB, MIX_W), lambda i: (i, 1))
    prev = pl.BlockSpec((HALO, MIX_W), lambda i: (jnp.maximum(i * r - 1, 0), 1))
    nxt = pl.BlockSpec((HALO, MIX_W), lambda i: (jnp.minimum((i + 1) * r, last), 1))
    tcur = pl.BlockSpec((TB, 128), lambda i: (i, 0))
    tprev = pl.BlockSpec((HALO, 128), lambda i: (jnp.maximum(i * r - 1, 0), 0))
    tnxt = pl.BlockSpec((HALO, 128), lambda i: (jnp.minimum((i + 1) * r, last), 0))
    return cur, prev, nxt, tcur, tprev, tnxt


def _mixer_param_specs():
    return [_full((1, HALF)), _full((1, HALF)), _full((SG_GROUPS, SG_CHUNK, SG_CHUNK)), _full((SG_CHUNK, 128)),
            _full((32, HALF)), _full((1, HALF)), _full((1, HALF)), _full((1, HALF)), _full((1, 128)), _full((8, HALF))]


def _mixers_fwd(proj, cos_t, sin_t, mp, TB, name):
    T = proj.shape[0]
    nb = T // TB
    r = TB // HALO
    cur, prev, _, tcur, tprev, _ = _mixer_in_specs(TB, nb)

    def body(zc_ref, zp_ref, cc_ref, sc_ref, cp_ref, sp_ref,
             lg_ref, lb_ref, sgw_ref, sgb_ref, cvw_ref, cvb_ref, cvg_ref, cvbb_ref, sinks_ref, scw_ref,
             ys_ref, scr_ref, k_ref, v_ref, sh_ref):
        i = pl.program_id(0)
        pm = (i > 0).astype(F32)

        def colsE(c0, c1):
            return jnp.concatenate([zp_ref[:, c0:c1].astype(F32) * pm, zc_ref[:, c0:c1].astype(F32)], axis=0)

        a = _gelu(zc_ref[:, C_ZA:C_ZA + 2 * HALF].astype(F32))
        u = a[:, :HALF]
        vn, _, _ = _ln_fwd(a[:, HALF:], lg_ref[...], lb_ref[...])
        vnb = vn.astype(BF16)
        tril = _tril_mask()
        chunks = [slice(ci * SG_CHUNK, (ci + 1) * SG_CHUNK) for ci in range(r)]
        for g in range(SG_GROUPS):
            cols = slice(g * 128, (g + 1) * 128)
            wt = jnp.where(tril, sgw_ref[g], 0.0).astype(BF16)
            mixed = _dot(wt, jnp.concatenate([vnb[rows, cols] for rows in chunks], axis=1)) + sgb_ref[:, g:g + 1]
            for ci, rows in enumerate(chunks):
                ys_ref[rows, cols] = (u[rows, cols] * mixed[:, ci * 128:(ci + 1) * 128]).astype(BF16)

        def colsB(c0, c1):
            return jnp.concatenate([zp_ref[HALO - CV_PAD:, c0:c1].astype(F32) * pm, zc_ref[:, c0:c1].astype(F32)], axis=0)

        sh_ref[0] = colsB(C_ZB, C_ZB + HALF) * _sig(colsB(C_ZB + HALF, C_ZB + 2 * HALF))
        _sublane_shifts(sh_ref, TB + CV_PAD)
        c = jnp.broadcast_to(cvb_ref[...], (TB, HALF))
        for k in range(CV_KERNEL):
            c = c + cvw_ref[k:k + 1, :] * _tap(sh_ref, CV_PAD - (CV_KERNEL - 1) + k, TB)
        n, _, _ = _ln_fwd(c, cvg_ref[...], cvbb_ref[...])
        ys_ref[:, HALF:2 * HALF] = (n * _sig(n)).astype(BF16)

        zd = colsE(C_ZD + HALF, C_ZD + 3 * HALF)
        scr_ref[...] = zd[:, :HALF] * zd[:, HALF:]
        cv = None
        for k in range(SC_KERNEL):
            t = scw_ref[k:k + 1, :] * scr_ref[pl.ds(HALO - (SC_KERNEL - 1) + k, TB), :]
            cv = t if cv is None else cv + t
        ys_ref[:, 3 * HALF:4 * HALF] = (zc_ref[:, C_ZD:C_ZD + HALF].astype(F32) * cv).astype(BF16)

        cosE = jnp.concatenate([cp_ref[...], cc_ref[...]], axis=0)
        sinE = jnp.concatenate([sp_ref[...], sc_ref[...]], axis=0)
        k_ref[...] = _rope(colsE(C_K, C_K + 128), cosE, sinE).astype(BF16)
        v_ref[...] = colsE(C_V, C_V + 128).astype(BF16)
        cosC, sinC = cc_ref[...], sc_ref[...]
        q = jnp.concatenate([_rope(zc_ref[:, C_Q + 128 * j:C_Q + 128 * (j + 1)].astype(F32), cosC, sinC)
                             for j in range(4)], axis=1).astype(BF16)
        in_band, in_band_cur = _band_masks()
        sink_cols = [_sink_col(sinks_ref, h) for h in range(N_KV_HEADS)]
        for qb in range(r):
            valid = in_band if qb else in_band_cur | (in_band & (i > 0))
            for h in range(N_KV_HEADS):
                hc = slice(h * HEAD_DIM, (h + 1) * HEAD_DIM)
                kh = k_ref[qb * WINDOW:qb * WINDOW + 2 * WINDOW, hc]
                vh = v_ref[qb * WINDOW:qb * WINDOW + 2 * WINDOW, hc]
                qs = jnp.concatenate([q[qb * WINDOW:(qb + 1) * WINDOW, (h * Q_PER_KV + g) * HEAD_DIM:(h * Q_PER_KV + g + 1) * HEAD_DIM]
                                      for g in range(Q_PER_KV)], axis=0)
                probs, _ = _attn_probs(qs, kh, sink_cols[h], valid)
                o = _dot(probs.astype(BF16), vh)
                for g in range(Q_PER_KV):
                    c0 = 2 * HALF + (h * Q_PER_KV + g) * HEAD_DIM
                    ys_ref[qb * WINDOW:(qb + 1) * WINDOW, c0:c0 + HEAD_DIM] = o[g * WINDOW:(g + 1) * WINDOW].astype(BF16)

    return pl.pallas_call(
        body, name=name, grid=(nb,),
        in_specs=[cur, prev, tcur, tcur, tprev, tprev] + _mixer_param_specs(),
        out_specs=pl.BlockSpec((TB, 4 * HALF), lambda i: (i, 0)),
        out_shape=jax.ShapeDtypeStruct((T, 4 * HALF), BF16),
        scratch_shapes=[pltpu.VMEM((TB + HALO, HALF), F32), pltpu.VMEM((TB + HALO, 128), BF16), pltpu.VMEM((TB + HALO, 128), BF16),
                        pltpu.VMEM((8, TB + CV_PAD, HALF), F32)],
        compiler_params=_params("parallel"),
    )(proj, proj, cos_t, sin_t, cos_t, sin_t, *mp)


def _mixers_bwd(proj, dys, dproj, cos_t, sin_t, mp, TB, name):
    T = proj.shape[0]
    nb = T // TB
    r = TB // HALO
    RE = TB + 2 * HALO
    RC = TB + HALO
    cur, prev, nxt, tcur, tprev, tnxt = _mixer_in_specs(TB, nb)
    dcur = pl.BlockSpec((TB, 4 * HALF), lambda i: (i, 0))
    dnxt = pl.BlockSpec((HALO, 4 * HALF), lambda i: (jnp.minimum((i + 1) * r, nb * r - 1), 0))

    def body(zc_ref, zp_ref, zn_ref, dyc_ref, dyn_ref, cc_ref, sc_ref, cp_ref, sp_ref, cn_ref, sn_ref,
             lg_ref, lb_ref, sgw_ref, sgb_ref, cvw_ref, cvb_ref, cvg_ref, cvbb_ref, sinks_ref, scw_ref, dp_in_ref,
             dz_ref, dlg_ref, dlb_ref, dsgw_ref, dsgb_ref, dcvw_ref, dcvb_ref, dcvg_ref, dcvbb_ref, dsink_ref, dscw_ref,
             scr_ref, scr2_ref, k_ref, v_ref, dk_ref, dv_ref, dq_ref, sh_ref, sh2_ref):
        del dp_in_ref
        i = pl.program_id(0)
        pm = (i > 0).astype(F32)
        nm = (i < nb - 1).astype(F32)

        @pl.when(i == 0)
        def _():
            for ref in (dlg_ref, dlb_ref, dsgw_ref, dsgb_ref, dcvw_ref, dcvb_ref, dcvg_ref, dcvbb_ref, dsink_ref, dscw_ref):
                ref[...] = jnp.zeros_like(ref)

        def colsE(c0, c1):
            return jnp.concatenate([zp_ref[:, c0:c1].astype(F32) * pm, zc_ref[:, c0:c1].astype(F32),
                                    zn_ref[:, c0:c1].astype(F32)], axis=0)

        def colsC(c0, c1):
            return jnp.concatenate([zc_ref[:, c0:c1].astype(F32), zn_ref[:, c0:c1].astype(F32)], axis=0)

        def dyC(c0, c1):
            return jnp.concatenate([dyc_ref[:, c0:c1].astype(F32), dyn_ref[:, c0:c1].astype(F32) * nm], axis=0)

        za = zc_ref[:, C_ZA:C_ZA + 2 * HALF].astype(F32)
        a = _gelu(za)
        u = a[:, :HALF]
        lg = lg_ref[...]
        vn, vh, rs = _ln_fwd(a[:, HALF:], lg, lb_ref[...])
        vnb = vn.astype(BF16)
        dya = dyc_ref[:, 0:HALF].astype(F32)
        tril = _tril_mask()
        lane128 = lax.broadcasted_iota(jnp.int32, (SG_CHUNK, 128), 1)
        chunks = [slice(ci * SG_CHUNK, (ci + 1) * SG_CHUNK) for ci in range(r)]
        side = lambda t, cols: jnp.concatenate([t[rows, cols] for rows in chunks], axis=1)
        for g in range(SG_GROUPS):
            cols = slice(g * 128, (g + 1) * 128)
            wt = jnp.where(tril, sgw_ref[g], 0.0).astype(BF16)
            vb = side(vnb, cols)
            dy_blk = side(dya, cols)
            du_g = dy_blk * (_dot(wt, vb) + sgb_ref[:, g:g + 1])
            dmix = dy_blk * side(u, cols)
            dmb = dmix.astype(BF16)
            dvn_g = _dot_tn(wt, dmb)
            dsgw_ref[g] += jnp.where(tril, _dot_nt(dmb, vb), 0.0)
            dsgb_ref[...] += jnp.where(lane128 == g, jnp.sum(dmix, axis=1, keepdims=True), 0.0)
            for ci, rows in enumerate(chunks):
                scr_ref[rows, cols] = du_g[:, ci * 128:(ci + 1) * 128]
                scr2_ref[rows, cols] = dvn_g[:, ci * 128:(ci + 1) * 128]
        du, dvn = scr_ref[0:TB, :], scr2_ref[0:TB, :]
        dlg_ref[...] += jnp.sum(dvn * vh, axis=0, keepdims=True)
        dlb_ref[...] += jnp.sum(dvn, axis=0, keepdims=True)
        dvv = _ln_bwd(dvn, vh, rs, lg)
        gg = _gelu_grad(za)
        dz_ref[:, C_ZA:C_ZA + HALF] = (du * gg[:, :HALF]).astype(BF16)
        dz_ref[:, C_ZA + HALF:C_ZA + 2 * HALF] = (dvv * gg[:, HALF:]).astype(BF16)

        RB = TB + CV_PAD

        def colsB(c0, c1):
            return jnp.concatenate([zp_ref[HALO - CV_PAD:, c0:c1].astype(F32) * pm, zc_ref[:, c0:c1].astype(F32),
                                    zn_ref[:CV_PAD, c0:c1].astype(F32)], axis=0)

        sh_ref[0] = colsB(C_ZB, C_ZB + HALF) * _sig(colsB(C_ZB + HALF, C_ZB + 2 * HALF))
        _sublane_shifts(sh_ref, RB + CV_PAD)
        c = jnp.broadcast_to(cvb_ref[...], (RB, HALF))
        for k in range(CV_KERNEL):
            c = c + cvw_ref[k:k + 1, :] * _tap(sh_ref, CV_PAD - (CV_KERNEL - 1) + k, RB)
        cvg = cvg_ref[...]
        n, ch, rc = _ln_fwd(c, cvg, cvbb_ref[...])
        sn = _sig(n)
        dyb = jnp.concatenate([dyc_ref[:, HALF:2 * HALF].astype(F32), dyn_ref[:CV_PAD, HALF:2 * HALF].astype(F32) * nm], axis=0)
        dn = dyb * (sn + n * sn * (1.0 - sn))
        dno = dn[:TB]
        dcvg_ref[...] += jnp.sum(dno * ch[:TB], axis=0, keepdims=True)
        dcvbb_ref[...] += jnp.sum(dno, axis=0, keepdims=True)
        dc = _ln_bwd(dn, ch, rc, cvg)
        sh2_ref[0] = dc
        _sublane_shifts(sh2_ref, RB)
        dcvb_ref[...] += jnp.sum(dc[:TB], axis=0, keepdims=True)
        dy0 = None
        for k in range(CV_KERNEL):
            wk = cvw_ref[k:k + 1, :]
            t = wk * _tap(sh2_ref, CV_KERNEL - 1 - k, TB)
            dy0 = t if dy0 is None else dy0 + t
            dcvw_ref[k:k + 1, :] += jnp.sum(dc[:TB] * _tap(sh_ref, CV_PAD - (CV_KERNEL - 1) + k, TB), axis=0, keepdims=True)
        ab = zc_ref[:, C_ZB:C_ZB + HALF].astype(F32)
        sg = _sig(zc_ref[:, C_ZB + HALF:C_ZB + 2 * HALF].astype(F32))
        dz_ref[:, C_ZB:C_ZB + HALF] = (dy0 * sg).astype(BF16)
        dz_ref[:, C_ZB + HALF:C_ZB + 2 * HALF] = (dy0 * ab * sg * (1.0 - sg)).astype(BF16)

        zd = colsE(C_ZD + HALF, C_ZD + 3 * HALF)
        scr_ref[...] = zd[:, :HALF] * zd[:, HALF:]
        dcv = dyC(3 * HALF, 4 * HALF) * colsC(C_ZD, C_ZD + HALF)
        scr2_ref[...] = dcv
        cv = None
        dud = None
        for k in range(SC_KERNEL):
            wk = scw_ref[k:k + 1, :]
            us = scr_ref[pl.ds(HALO - (SC_KERNEL - 1) + k, TB), :]
            t = wk * us
            cv = t if cv is None else cv + t
            t2 = wk * scr2_ref[pl.ds(SC_KERNEL - 1 - k, TB), :]
            dud = t2 if dud is None else dud + t2
            dscw_ref[k:k + 1, :] += jnp.sum(dcv[:TB] * us, axis=0, keepdims=True)
        dz_ref[:, C_ZD:C_ZD + HALF] = (dyc_ref[:, 3 * HALF:4 * HALF].astype(F32) * cv).astype(BF16)
        dz_ref[:, C_ZD + HALF:C_ZD + 2 * HALF] = (dud * zc_ref[:, C_ZD + 2 * HALF:C_ZD + 3 * HALF].astype(F32)).astype(BF16)
        dz_ref[:, C_ZD + 2 * HALF:C_ZD + 3 * HALF] = (dud * zc_ref[:, C_ZD + HALF:C_ZD + 2 * HALF].astype(F32)).astype(BF16)

        cosE = jnp.concatenate([cp_ref[...], cc_ref[...], cn_ref[...]], axis=0)
        sinE = jnp.concatenate([sp_ref[...], sc_ref[...], sn_ref[...]], axis=0)
        k_ref[...] = _rope(colsE(C_K, C_K + 128), cosE, sinE).astype(BF16)
        v_ref[...] = colsE(C_V, C_V + 128).astype(BF16)
        dk_ref[...] = jnp.zeros_like(dk_ref)
        dv_ref[...] = jnp.zeros_like(dv_ref)
        q = jnp.concatenate([_rope(colsC(C_Q + 128 * j, C_Q + 128 * (j + 1)), cosE[HALO:], sinE[HALO:])
                             for j in range(4)], axis=1).astype(BF16)
        dO = dyC(2 * HALF, 3 * HALF).astype(BF16)
        lane_s = lax.broadcasted_iota(jnp.int32, (1, 128), 1)
        in_band, in_band_cur = _band_masks()
        sink_cols = [_sink_col(sinks_ref, h) for h in range(N_KV_HEADS)]
        for qb in range(r + 1):
            valid = in_band if qb else in_band_cur | (in_band & (i > 0))
            rows = slice(qb * WINDOW, (qb + 1) * WINDOW)
            band = slice(qb * WINDOW, qb * WINDOW + 2 * WINDOW)
            for h in range(N_KV_HEADS):
                hc = slice(h * HEAD_DIM, (h + 1) * HEAD_DIM)
                kh = k_ref[band, hc]
                vh_ = v_ref[band, hc]
                heads = [slice((h * Q_PER_KV + g) * HEAD_DIM, (h * Q_PER_KV + g + 1) * HEAD_DIM) for g in range(Q_PER_KV)]
                qs = jnp.concatenate([q[rows, hs] for hs in heads], axis=0)
                dos = jnp.concatenate([dO[rows, hs] for hs in heads], axis=0)
                probs, p_sink = _attn_probs(qs, kh, sink_cols[h], valid)
                dP = _dot_nt(dos, vh_)
                rsum = jnp.sum(probs * dP, axis=-1, keepdims=True)
                dS = (probs * (dP - rsum) * (HEAD_DIM ** -0.5)).astype(BF16)
                dk_ref[band, hc] += _dot_tn(dS, qs)
                dv_ref[band, hc] += _dot_tn(probs.astype(BF16), dos)
                if qb < r:
                    dqs = _dot(dS, kh)
                    dsk = -p_sink * rsum
                    for g in range(Q_PER_KV):
                        dq_ref[rows, heads[g]] = dqs[g * WINDOW:(g + 1) * WINDOW]
                        dsink_ref[...] += jnp.where(lane_s == h * Q_PER_KV + g, jnp.sum(dsk[g * WINDOW:(g + 1) * WINDOW]), 0.0)
        cosC, sinC = cc_ref[...], sc_ref[...]
        for j in range(4):
            dz_ref[:, C_Q + 128 * j:C_Q + 128 * (j + 1)] = _rope_t(dq_ref[:, 128 * j:128 * (j + 1)], cosC, sinC).astype(BF16)
        dz_ref[:, C_K:C_K + 128] = _rope_t(dk_ref[HALO:HALO + TB, :], cosC, sinC).astype(BF16)
        dz_ref[:, C_V:C_V + 128] = dv_ref[HALO:HALO + TB, :].astype(BF16)

    small = [((1, HALF), F32), ((1, HALF), F32), ((SG_GROUPS, SG_CHUNK, SG_CHUNK), F32), ((SG_CHUNK, 128), F32),
             ((32, HALF), F32), ((1, HALF), F32), ((1, HALF), F32), ((1, HALF), F32), ((1, 128), F32), ((8, HALF), F32)]
    outs = pl.pallas_call(
        body, name=name, grid=(nb,),
        in_specs=[cur, prev, nxt, dcur, dnxt, tcur, tcur, tprev, tprev, tnxt, tnxt] + _mixer_param_specs()
                 + [pl.BlockSpec(memory_space=pl.ANY)],
        out_specs=[pl.BlockSpec((TB, MIX_W), lambda i: (i, 1))] + [_full(s) for s, _ in small],
        out_shape=[jax.ShapeDtypeStruct((T, PROJ_PAD), BF16)] + [jax.ShapeDtypeStruct(s, d) for s, d in small],
        scratch_shapes=[pltpu.VMEM((RE, HALF), F32), pltpu.VMEM((RC, HALF), F32), pltpu.VMEM((RE, 128), BF16), pltpu.VMEM((RE, 128), BF16),
                        pltpu.VMEM((RE, 128), F32), pltpu.VMEM((RE, 128), F32), pltpu.VMEM((TB, HALF), F32),
                        pltpu.VMEM((8, TB + 2 * CV_PAD, HALF), F32), pltpu.VMEM((8, TB + CV_PAD, HALF), F32)],
        input_output_aliases={21: 0},
        compiler_params=_params("arbitrary"),
    )(proj, proj, proj, dys, dys, cos_t, sin_t, cos_t, sin_t, cos_t, sin_t, *mp, dproj)
    return outs


def _rope_tables(T):
    pos = jnp.arange(T, dtype=F32)
    inv_freq = 1.0 / (ROPE_THETA ** (jnp.arange(0, HEAD_DIM, 2, dtype=F32) / HEAD_DIM))
    ang = pos[:, None] * inv_freq[None, :]
    cos, sin = jnp.cos(ang), jnp.sin(ang)
    cos_t = jnp.concatenate([cos, cos, cos, cos], axis=1)
    sin_t = jnp.concatenate([-sin, sin, -sin, sin], axis=1)
    return cos_t, sin_t


def _mixer_params(l, sg_ln_g, sg_ln_b, sg_w, sg_b, cv_w, cv_b, cv_ln_g, cv_ln_b, attn_sinks, sc_w):
    sgb_t = jnp.zeros((SG_CHUNK, 128), F32).at[:, :SG_GROUPS].set(sg_b[l].T)
    cvw = jnp.zeros((32, HALF), F32).at[:CV_KERNEL].set(cv_w[l])
    scw = jnp.zeros((8, HALF), F32).at[:SC_KERNEL].set(sc_w[l])
    sinks = jnp.zeros((1, 128), F32).at[0, :N_Q_HEADS].set(attn_sinks[l])
    return [sg_ln_g[l][None], sg_ln_b[l][None], sg_w[l], sgb_t, cvw, cv_b[l][None], cv_ln_g[l][None], cv_ln_b[l][None], sinks, scw]


def _w_in_layout(w_in_g):
    cut = MIX_W - 2 * W_IN_SHARD
    return jnp.concatenate([w_in_g[2][cut:], w_in_g[3], jnp.zeros((MIX_W - GATE_W, D_MODEL), w_in_g.dtype),
                            w_in_g[0], w_in_g[1], w_in_g[2][:cut]], axis=0)


def _w_in_unlayout(dw):
    cut = MIX_W - 2 * W_IN_SHARD
    return jnp.stack([dw[MIX_W:MIX_W + W_IN_SHARD], dw[MIX_W + W_IN_SHARD:MIX_W + 2 * W_IN_SHARD],
                      jnp.concatenate([dw[MIX_W + 2 * W_IN_SHARD:], dw[:W_IN_SHARD - cut]], axis=0),
                      dw[W_IN_SHARD - cut:GATE_W]], axis=0)


def _device_step(x, tgt, norm_mix, norm_ffn, norm_final, mixer_params, w_in_p, wb_g, wo_g, wgu_g, wd_g):
    T = x.shape[0]
    tables = _rope_tables(T)
    saved = []
    for l in range(DEPTH):
        lw = dict(w_in=w_in_p[l], w_branch=wb_g[l], w_out=wo_g[l], w_gate_up=wgu_g[l], w_down=wd_g[l],
                  norm_mix=norm_mix[l][None], norm_ffn=norm_ffn[l][None], mixer=mixer_params[l], after=jnp.zeros((8, 128), F32))
        x, sv = _fwd_layer(l, x, lw, tables)
        saved.append((lw, sv))
    dx, dnf, loss = _final_loss(x, norm_final[None], tgt, 256, "final_loss")
    grads = [None] * DEPTH
    for l in reversed(range(DEPTH)):
        lw, sv = saved[l]
        dxm, g_ffn = _bwd_layer_ffn(l, dx, lw, sv)
        dx, g_mix = _bwd_layer_mix(l, dxm, lw, sv, tables)
        raw = {**g_ffn, **g_mix}
        grads[l] = {**raw, **_small_views(raw)}
    return loss, dx, dnf[0], grads


MIX_BLOCK = 256


def _fwd_layer(l, x, lw, tables):
    return _fwd_layer_rest(l, x, _fwd_layer_mix(l, x, lw, tables), lw)


def _fwd_layer_mix(l, x, lw, tables, between=None):
    proj, xn = _rms_mm(x, lw["norm_mix"], lw["w_in"], min(x.shape[0], 1024), 2176, f"proj{l}")
    if between is not None:
        between(proj, lw)
    return proj, xn, _mixers_fwd(proj, *tables, lw["mixer"], MIX_BLOCK, f"mixers_fwd{l}")


def _fwd_layer_rest(l, x, mixed, lw, between=None):
    proj, xn, ys = mixed
    TM = min(x.shape[0], 1024)
    xm, merged = _merge_fwd(x, ys, proj, lw["w_branch"], lw["w_out"], min(x.shape[0], 512), f"merge_fwd{l}")
    if between is not None:
        between(xm, lw)
    gu, hn = _rms_mm(xm, lw["norm_ffn"], lw["w_gate_up"], TM, GU_SHARD, f"ffn_up{l}")
    x_out = _ffn_down(xm, gu, lw["w_down"], 256, f"ffn_down{l}")
    return x_out, (x, proj, xn, ys, xm, merged, gu, hn)


def _bwd_layer_ffn(l, dx, lw, sv):
    x_in, proj, xn, ys, xm, merged, gu, hn = sv
    T = dx.shape[0]
    tkk = min(T, 1024)
    gk = T // tkk
    dgu, act = _swiglu_bwd(dx, gu, lw["w_down"], 256, f"swiglu_bwd{l}", lw["after"])
    d_wd = _mm_tn(act, dx, (2, 1, gk), (tkk, D_FF // 2), lambda i, j, k: (k, i), (tkk, D_MODEL), lambda i, j, k: (k, 0),
                  (D_FF, D_MODEL), (D_FF // 2, D_MODEL), lambda i, j, k: (i, 0), f"dw_down{l}")
    d_wgu = _mm_tn(hn, dgu, (1, N_CHIPS, gk), (tkk, D_MODEL), lambda i, j, k: (k, 0), (tkk, GU_SHARD), lambda i, j, k: (k, j),
                   (N_CHIPS, D_MODEL, GU_SHARD), (None, D_MODEL, GU_SHARD), lambda i, j, k: (j, 0, 0), f"dw_gate_up{l}")
    dxm, d_nffn = _mm_nt_rmsbwd(dgu, lw["w_gate_up"], xm, lw["norm_ffn"], dx, min(T, 1024), GU_SHARD, f"ffn_up_bwd{l}")
    dys, dbr, dproj = _merge_bwd(dxm, ys, proj, lw["w_branch"], lw["w_out"], 256, f"merge_bwd{l}")
    d_wo = _mm_tn(merged, dxm, (2, 1, gk), (tkk, 512), lambda i, j, k: (k, i), (tkk, D_MODEL), lambda i, j, k: (k, 0),
                  (D_MODEL, D_MODEL), (512, D_MODEL), lambda i, j, k: (i, 0), f"dw_out{l}")
    d_wb = _mm_tn(ys, dbr, (N_BRANCH, 1, gk), (tkk, HALF), lambda i, j, k: (k, i), (tkk, D_MODEL), lambda i, j, k: (k, i),
                  (N_CHIPS, N_BRANCH, HALF, 256), (N_CHIPS, None, HALF, 256), lambda i, j, k: (0, i, 0, 0), f"dw_branch{l}", col_split=N_CHIPS)
    return (dxm, dys, dproj), dict(w_branch=d_wb, w_out=d_wo, w_gate_up=d_wgu, w_down=d_wd, norm_ffn=d_nffn)


def _bwd_layer_mix(l, carry, lw, sv, tables, between=None):
    dxm, dys, dproj = carry
    x_in, proj, xn, ys, xm, merged, gu, hn = sv
    T = dxm.shape[0]
    tkk = min(T, 1024)
    gk = T // tkk
    mb = _mixers_bwd(proj, dys, dproj, *tables, lw["mixer"], MIX_BLOCK, f"mixers_bwd{l}")
    dproj = mb[0]
    d_win = _mm_tn(dproj, xn, (PROJ_PAD // 2176, 1, gk), (tkk, 2176), lambda i, j, k: (k, i), (tkk, D_MODEL), lambda i, j, k: (k, 0),
                   (PROJ_PAD, D_MODEL), (2176, D_MODEL), lambda i, j, k: (i, 0), f"dw_in{l}")
    if between is not None:
        between(d_win, lw)
    dx, d_nmix = _mm_nt_rmsbwd(dproj, lw["w_in"], x_in, lw["norm_mix"], dxm, min(T, 1024), 2176, f"proj_bwd{l}")
    return dx, dict(w_in=d_win, norm_mix=d_nmix, sg_ln_g=mb[1], sg_ln_b=mb[2], sg_w=mb[3], sg_b=mb[4], cv_w=mb[5], cv_b=mb[6],
                    cv_ln_g=mb[7], cv_ln_b=mb[8], attn_sinks=mb[9], sc_w=mb[10])


ANY = pl.BlockSpec(memory_space=pl.ANY)
BIG = ("w_in", "w_branch", "w_out", "w_gate_up", "w_down")
HALF_SHAPE = {"w_in": (2, W_IN_SHARD // 2, D_MODEL), "w_branch": (2, 1024, 256), "w_out": (2, 128, D_MODEL),
              "w_gate_up": (2, 512, GU_SHARD), "w_down": (2, 352, D_MODEL)}
NB = len(BIG)


def _place():
    x, y, c = lax.axis_index("x"), lax.axis_index("y"), lax.axis_index("c")
    chips = [(1 - x, y), (x, 1 - y), (1 - x, 1 - y)]
    return x, y, c, 2 * x + y, chips, [2 * px + py for px, py in chips]


def _remote(src, dst, ssem, rsem, dev):
    return pltpu.make_async_remote_copy(src_ref=src, dst_ref=dst, send_sem=ssem, recv_sem=rsem, device_id=dev, device_id_type=MESH)


HBM_SPEC = pl.BlockSpec(memory_space=pltpu.HBM)
SEM_SPEC = pl.BlockSpec(memory_space=pltpu.SEMAPHORE)
DATAFLOW = pltpu.SideEffectType.DATAFLOW_SIDE_EFFECTING


def _ici_ends(kind, src, land, j, c, chip, chip_ids):
    if kind == "gather":
        return src.at[c], land.at[chip, c], land.at[chip_ids[j], c]
    return src.at[chip_ids[j]], land.at[chip], land.at[chip_ids[j]]


def _ici_start(kind, srcs, land_shapes, name):
    n = len(srcs)

    def body(*refs):
        src, land = refs[:n], refs[n:2 * n]
        ssem, rsem, token = refs[2 * n], refs[2 * n + 1], refs[-1]
        x, y, c, chip, chips, chip_ids = _place()
        for k in range(n):
            for j in range(3):
                s, d, _ = _ici_ends(kind, src[k], land[k], j, c, chip, chip_ids)
                _remote(s, d, ssem.at[3 * k + j], rsem.at[3 * k + j], (*chips[j], c)).start()
        token[...] = jnp.zeros_like(token)

    sem = pltpu.SemaphoreType.DMA((3 * n,))
    outs = pl.pallas_call(
        body, name=name,
        out_shape=(sem, sem, *[pltpu.HBM(s.shape, s.dtype) for s in srcs], *[pltpu.HBM(sh, BF16) for sh in land_shapes],
                   jax.ShapeDtypeStruct((8, 128), F32)),
        in_specs=[HBM_SPEC] * (2 * n),
        out_specs=(SEM_SPEC, SEM_SPEC, *[HBM_SPEC] * (2 * n), pl.BlockSpec(memory_space=pltpu.VMEM)),
        input_output_aliases={i: 2 + i for i in range(2 * n)},
        compiler_params=pltpu.CompilerParams(has_side_effects=DATAFLOW),
    )(*[pltpu.with_memory_space_constraint(s, pltpu.HBM) for s in srcs],
      *[pltpu.with_memory_space_constraint(lax.empty(sh, BF16), pltpu.HBM) for sh in land_shapes])
    return (kind, outs[0], outs[1], list(outs[2:2 + n]), list(outs[2 + n:2 + 2 * n])), outs[-1]


def _ici_wait(handle, after, name):
    kind, ssem_in, rsem_in, srcs, lands = handle
    n = len(srcs)

    def body(*refs):
        src, land = refs[:n], refs[n:2 * n]
        ssem, rsem = refs[2 * n], refs[2 * n + 1]
        x, y, c, chip, chips, chip_ids = _place()
        for k in range(n):
            for j in range(3):
                s, _, mine = _ici_ends(kind, src[k], land[k], j, c, chip, chip_ids)
                cp = _remote(s, mine, ssem.at[3 * k + j], rsem.at[3 * k + j], (*chips[j], c))
                cp.wait_send()
                cp.wait_recv()

    outs = pl.pallas_call(
        body, name=name, out_shape=[pltpu.HBM(t.shape, t.dtype) for t in srcs + lands],
        in_specs=[HBM_SPEC] * (2 * n) + [SEM_SPEC, SEM_SPEC, ANY], out_specs=[HBM_SPEC] * (2 * n),
        input_output_aliases={i: i for i in range(2 * n)},
        compiler_params=pltpu.CompilerParams(has_side_effects=DATAFLOW),
    )(*srcs, *lands, ssem_in, rsem_in, after)
    return list(outs[:n]), list(outs[n:])


def _ag_pair(shards, lands, name):
    n = len(shards)

    def body(*refs):
        ins, outs = refs[:n], refs[2 * n:3 * n]
        token = refs[3 * n]
        s_fwd, r_fwd, s_own, r_own = refs[3 * n + 1:]
        x, y, c, chip, chips, chip_ids = _place()
        sib = (x, y, 1 - c)
        cps = []
        for k in range(n):
            cp = _remote(ins[k], outs[k].at[chip], s_own.at[k], r_own.at[k], sib)
            cp.start()
            cps.append(cp)
            for j in range(3):
                got = outs[k].at[chip_ids[j], c]
                cp = _remote(got, got, s_fwd.at[k, j], r_fwd.at[k, j], sib)
                cp.start()
                cps.append(cp)
        for k in range(n):
            _remote(ins[k], outs[k].at[chip], s_own.at[k], r_own.at[k], sib).wait_recv()
            for j in range(3):
                got = outs[k].at[chip_ids[j], 1 - c]
                _remote(got, got, s_fwd.at[k, j], r_fwd.at[k, j], sib).wait_recv()
        for cp in cps:
            cp.wait_send()
        token[...] = jnp.zeros_like(token)

    sem, sem1 = pltpu.SemaphoreType.DMA((n, 3)), pltpu.SemaphoreType.DMA((n,))
    outs = pl.pallas_call(
        body, name=name, out_shape=[jax.ShapeDtypeStruct(t.shape, t.dtype) for t in lands] + [jax.ShapeDtypeStruct((8, 128), F32)],
        in_specs=[ANY] * (2 * n), out_specs=[ANY] * n + [pl.BlockSpec(memory_space=pltpu.VMEM)],
        input_output_aliases={n + k: k for k in range(n)},
        scratch_shapes=[sem, sem, sem1, sem1], compiler_params=pltpu.CompilerParams(has_side_effects=True),
    )(*shards, *lands)
    return list(outs[:n]), outs[n]


def _forward_plan(n):
    def plan(refs, c, chip, chip_ids):
        out = []
        for k in range(n):
            shard, land = refs[k], refs[n + k]
            out.append((shard, land.at[chip], land.at[chip]))
            out += [(land.at[q, c], land.at[q, c], land.at[q, 1 - c]) for q in chip_ids]
        return out
    return plan, 4 * n


def _swap_plan(n):
    def plan(refs, c, chip, chip_ids):
        return [(refs[k].at[q, 1 - c], refs[n + k].at[q], refs[n + k].at[q]) for k in range(n) for q in range(N_CHIPS)]
    return plan, N_CHIPS * n


def _d2d_start(arrays, new_shapes, plan_n, name):
    plan, n_copies = plan_n
    n = len(arrays) + len(new_shapes)

    def body(*refs):
        ssem, rsem, token = refs[n], refs[n + 1], refs[-1]
        x, y, c, chip, _, chip_ids = _place()
        for i, (s, d, _) in enumerate(plan(refs[:n], c, chip, chip_ids)):
            _remote(s, d, ssem.at[i], rsem.at[i], (x, y, 1 - c)).start()
        token[...] = jnp.zeros_like(token)

    sem = pltpu.SemaphoreType.DMA((n_copies,))
    args = [pltpu.with_memory_space_constraint(t, pltpu.HBM) for t in arrays] + \
           [pltpu.with_memory_space_constraint(lax.empty(sh, BF16), pltpu.HBM) for sh in new_shapes]
    outs = pl.pallas_call(
        body, name=name,
        out_shape=(sem, sem, *[pltpu.HBM(t.shape, t.dtype) for t in args], jax.ShapeDtypeStruct((8, 128), F32)),
        in_specs=[HBM_SPEC] * n, out_specs=(SEM_SPEC, SEM_SPEC, *[HBM_SPEC] * n, pl.BlockSpec(memory_space=pltpu.VMEM)),
        input_output_aliases={i: 2 + i for i in range(n)},
        compiler_params=pltpu.CompilerParams(has_side_effects=DATAFLOW),
    )(*args)
    return (plan, outs[0], outs[1], list(outs[2:2 + n])), outs[-1]


def _d2d_wait(handle, after, name):
    plan, ssem_in, rsem_in, arrays = handle
    n = len(arrays)

    def body(*refs):
        ssem, rsem = refs[n], refs[n + 1]
        x, y, c, chip, _, chip_ids = _place()
        for i, (s, _, mine) in enumerate(plan(refs[:n], c, chip, chip_ids)):
            cp = _remote(s, mine, ssem.at[i], rsem.at[i], (x, y, 1 - c))
            cp.wait_send()
            cp.wait_recv()

    outs = pl.pallas_call(
        body, name=name, out_shape=[pltpu.HBM(t.shape, t.dtype) for t in arrays],
        in_specs=[HBM_SPEC] * n + [SEM_SPEC, SEM_SPEC, ANY], out_specs=[HBM_SPEC] * n,
        input_output_aliases={i: i for i in range(n)},
        compiler_params=pltpu.CompilerParams(has_side_effects=DATAFLOW),
    )(*arrays, ssem_in, rsem_in, after)
    return list(outs)


def _rs_pair(grads, name):
    n_arr = len(grads)

    def body(*refs):
        ins, got = refs[:n_arr], refs[n_arr:2 * n_arr]
        ssem, rsem = refs[2 * n_arr:]
        x, y, c, _, _, _ = _place()
        sib = (x, y, 1 - c)
        sends = []
        for k in reversed(range(n_arr)):
            for q in range(N_CHIPS):
                cp = _remote(ins[k].at[q, 1 - c], got[k].at[q], ssem.at[k, q], rsem.at[k, q], sib)
                cp.start()
                sends.append(cp)
        for k in range(n_arr):
            for q in range(N_CHIPS):
                _remote(got[k].at[q], got[k].at[q], ssem.at[k, q], rsem.at[k, q], sib).wait_recv()
        for cp in sends:
            cp.wait_send()

    shp = [jax.ShapeDtypeStruct((N_CHIPS,) + g.shape[2:], BF16) for g in grads]
    sem = pltpu.SemaphoreType.DMA((n_arr, N_CHIPS))
    outs = pl.pallas_call(
        body, name=name, out_shape=shp, in_specs=[ANY] * n_arr, out_specs=[ANY] * n_arr,
        scratch_shapes=[sem, sem], compiler_params=pltpu.CompilerParams(has_side_effects=True),
    )(*grads)
    return list(outs)


def _rs_share(bufs, name):
    n = len(bufs)

    def body(*refs):
        outs = refs[n:2 * n]
        ssem, rsem = refs[2 * n:]
        x, y, c, _, _, _ = _place()
        sib = (x, y, 1 - c)
        sends = []
        for k in range(n):
            for l in range(DEPTH):
                cp = _remote(outs[k].at[l, c], outs[k].at[l, c], ssem.at[k, l], rsem.at[k, l], sib)
                cp.start()
                sends.append(cp)
        for k in range(n):
            for l in range(DEPTH):
                dst = outs[k].at[l, 1 - c]
                _remote(dst, dst, ssem.at[k, l], rsem.at[k, l], sib).wait_recv()
        for cp in sends:
            cp.wait_send()

    sem = pltpu.SemaphoreType.DMA((n, DEPTH))
    outs = pl.pallas_call(
        body, name=name, out_shape=[jax.ShapeDtypeStruct(b.shape, b.dtype) for b in bufs], in_specs=[ANY] * n, out_specs=[ANY] * n,
        input_output_aliases={k: k for k in range(n)},
        scratch_shapes=[sem, sem], compiler_params=pltpu.CompilerParams(has_side_effects=True),
    )(*bufs)
    return list(outs)


def _piece(src, idx, rows, width=128, align=1, transposed=False):
    return dict(src=src, idx=idx, rows=rows, width=width, align=align, transposed=transposed)


def _all_reduce_pieces(inputs, pieces, out_shapes, writes, name):
    n_in, n_out = len(inputs), len(out_shapes)
    offs, R = [], 0
    for p in pieces:
        R = -(-R // p["align"]) * p["align"]
        offs.append(R)
        R += p["rows"]
    R = -(-R // 8) * 8

    def body(*refs):
        ins, outs, token_ref = refs[:n_in], refs[n_in:n_in + n_out], refs[n_in + n_out]
        pair_ref, chip_ref, sum_ref, ssem, rsem = refs[n_in + n_out + 1:]
        token_ref[...] = jnp.zeros_like(token_ref)
        x, y, c, chip, chips, chip_ids = _place()
        pair_ref[c] = jnp.zeros((R, 128), F32)
        for p, off in zip(pieces, offs):
            v = ins[p["src"]][...].T[p["idx"]] if p["transposed"] else ins[p["src"]][p["idx"]]
            pair_ref[c, off:off + p["rows"], 0:p["width"]] = v
        mine = _remote(pair_ref.at[c], pair_ref.at[c], ssem.at[3], rsem.at[3], (x, y, 1 - c))
        mine.start()
        _remote(pair_ref.at[1 - c], pair_ref.at[1 - c], ssem.at[3], rsem.at[3], (x, y, 1 - c)).wait_recv()
        chip_ref[chip] = pair_ref[0] + pair_ref[1]
        cps = [_remote(chip_ref.at[chip], chip_ref.at[chip], ssem.at[j], rsem.at[j], (*chips[j], c)) for j in range(3)]
        for cp in cps:
            cp.start()
        for j in range(3):
            slot = chip_ref.at[chip_ids[j]]
            _remote(slot, slot, ssem.at[j], rsem.at[j], (*chips[j], c)).wait_recv()
        acc = chip_ref[0]
        for s in range(1, N_CHIPS):
            acc = acc + chip_ref[s]
        sum_ref[...] = acc
        for o, idx, p in writes:
            outs[o][idx] = sum_ref[offs[p]:offs[p] + pieces[p]["rows"], 0:pieces[p]["width"]]
        for cp in cps + [mine]:
            cp.wait_send()

    vm = pl.BlockSpec(memory_space=pltpu.VMEM)
    outs = pl.pallas_call(
        body, name=name, out_shape=[jax.ShapeDtypeStruct(s, F32) for s in out_shapes] + [jax.ShapeDtypeStruct((8, 128), F32)],
        in_specs=[vm] * n_in, out_specs=[vm] * (n_out + 1),
        scratch_shapes=[pltpu.VMEM((2, R, 128), F32), pltpu.VMEM((N_CHIPS, R, 128), F32), pltpu.VMEM((R, 128), F32),
                        pltpu.SemaphoreType.DMA((4,)), pltpu.SemaphoreType.DMA((4,))],
        compiler_params=pltpu.CompilerParams(vmem_limit_bytes=VMEM_LIMIT),
    )(*inputs)
    return list(outs[:n_out]), outs[n_out]


def _lanes(width):
    return [slice(k, min(k + 128, width)) for k in range(0, width, 128)]


def _gather_small_weights(cvw_z, scw_z):
    pieces, writes = [], []
    for i, arr in enumerate((cvw_z, scw_z)):
        for l in range(DEPTH):
            for ln in _lanes(HALF):
                writes.append((i, (l, slice(None), ln), len(pieces)))
                pieces.append(_piece(i, (l, slice(None), ln), arr.shape[1], align=8))
    (cvw, scw), tok = _all_reduce_pieces([cvw_z, scw_z], pieces, [cvw_z.shape, scw_z.shape], writes, "ag_small")
    return cvw, scw, tok


SMALL_RAW = dict(norm_mix=(1, D_MODEL), norm_ffn=(1, D_MODEL), sg_ln_g=(1, HALF), sg_ln_b=(1, HALF), cv_b=(1, HALF), cv_ln_g=(1, HALF),
                 cv_ln_b=(1, HALF))


def _all_reduce_small_grads(raw, d_nfinal, loss):
    names = list(SMALL_RAW) + ["attn_sinks", "sg_b", "sc_w", "cv_w", "sg_w"]
    out_shape = dict(norm_mix=(DEPTH, D_MODEL), norm_ffn=(DEPTH, D_MODEL), sg_ln_g=(DEPTH, HALF), sg_ln_b=(DEPTH, HALF), cv_b=(DEPTH, HALF),
                     cv_ln_g=(DEPTH, HALF), cv_ln_b=(DEPTH, HALF), attn_sinks=(DEPTH, N_Q_HEADS), sg_b=(DEPTH, SG_GROUPS, SG_CHUNK),
                     sc_w=(DEPTH, SC_KERNEL, HALF), cv_w=(DEPTH, CV_KERNEL, HALF), sg_w=(DEPTH, SG_GROUPS, SG_CHUNK, SG_CHUNK))
    inputs, pieces, writes = [], [], []

    def add(src, idx, rows, out, out_idx, **kw):
        writes.append((names.index(out) if out in names else out, out_idx, len(pieces)))
        pieces.append(_piece(src, idx, rows, **kw))

    for l in range(DEPTH):
        row = slice(l, l + 1)
        for n, (_, width) in SMALL_RAW.items():
            inputs.append(raw[l][n])
            for ln in _lanes(width):
                add(len(inputs) - 1, (slice(0, 1), ln), 1, n, (row, ln))
        inputs.append(raw[l]["attn_sinks"])
        add(len(inputs) - 1, (slice(0, 1), slice(0, N_Q_HEADS)), 1, "attn_sinks", (row, slice(None)), width=N_Q_HEADS)
    for l in range(DEPTH):
        inputs.append(raw[l]["sg_b"])
        add(len(inputs) - 1, (slice(0, SG_GROUPS), slice(None)), SG_GROUPS, "sg_b", (l,), align=8, transposed=True)
        inputs.append(raw[l]["sc_w"])
        for ln in _lanes(HALF):
            add(len(inputs) - 1, (slice(0, SC_KERNEL), ln), SC_KERNEL, "sc_w", (l, slice(None), ln), align=8)
        inputs.append(raw[l]["cv_w"])
        for ln in _lanes(HALF):
            add(len(inputs) - 1, (slice(0, CV_KERNEL), ln), CV_KERNEL, "cv_w", (l, slice(None), ln), align=8)
        inputs.append(raw[l]["sg_w"])
        for g in range(SG_GROUPS):
            add(len(inputs) - 1, (g,), SG_CHUNK, "sg_w", (l, g), align=8)
    n_names = len(names)
    inputs.append(d_nfinal)
    for ln in _lanes(D_MODEL):
        add(len(inputs) - 1, (slice(0, 1), ln), 1, n_names, (slice(0, 1), ln))
    inputs.append(loss)
    add(len(inputs) - 1, (slice(0, 1), slice(None)), 1, n_names + 1, (slice(0, 1), slice(None)))
    outs, tok = _all_reduce_pieces(inputs, pieces, [out_shape[n] for n in names] + [(1, D_MODEL), (1, 128)], writes, "ar_small")
    return dict(zip(names, outs[:n_names])), outs[n_names], outs[n_names + 1], tok


def _small_views(raw):
    v = {n: raw[n][0] for n in SMALL_RAW}
    v.update(sg_w=raw["sg_w"], sg_b=raw["sg_b"][:, :SG_GROUPS].T, cv_w=raw["cv_w"][:CV_KERNEL],
             attn_sinks=raw["attn_sinks"][0, :N_Q_HEADS], sc_w=raw["sc_w"][:SC_KERNEL])
    return v


def _row_tile(rows, cols, n_arrays):
    budget = 20 * 1024 * 1024 // (n_arrays * 2 * cols * 4)
    tiles = [t for t in range(16, min(rows, budget) + 1, 16) if rows % t == 0]
    assert tiles, (rows, cols)
    return tiles[-1]


def _add_pairs(g, got, place, name):
    _, _, rows, cols = g.shape
    tr = _row_tile(rows, cols, 3)

    def body(place_ref, a_ref, b_ref, o_ref):
        del place_ref
        o_ref[...] = (a_ref[...].astype(F32) + b_ref[...].astype(F32)).astype(BF16)

    spec = pl.BlockSpec((None, tr, cols), lambda q, i, p: (q, i, 0))
    grid_spec = pltpu.PrefetchScalarGridSpec(
        num_scalar_prefetch=1, grid=(N_CHIPS, rows // tr),
        in_specs=[pl.BlockSpec((None, None, tr, cols), lambda q, i, p: (q, p[1], i, 0)), spec], out_specs=spec)
    return pl.pallas_call(body, name=name, grid_spec=grid_spec, out_shape=jax.ShapeDtypeStruct((N_CHIPS, rows, cols), BF16),
                          compiler_params=_params("parallel", "parallel"))(place, g, got)


def _sum_chips(own, recv, place, l, buf, name, after):
    _, rows, cols = own.shape
    tr = _row_tile(rows, cols, 4)

    def body(place_ref, own_ref, recv_ref, *rest):
        chip = place_ref[0]
        acc = own_ref[...].astype(F32)
        for j in range(1, N_CHIPS):
            acc = acc + recv_ref[lax.rem(chip + j, N_CHIPS)].astype(F32)
        rest[-1][...] = acc

    in_specs = [pl.BlockSpec((None, tr, cols), lambda i, p: (p[0], i, 0)), pl.BlockSpec((N_CHIPS, tr, cols), lambda i, p: (0, i, 0)), ANY]
    args = [place, own, recv, after]
    aliases = {}
    if buf is not None:
        in_specs.append(ANY)
        args.append(buf)
        aliases = {4: 0}
    grid_spec = pltpu.PrefetchScalarGridSpec(
        num_scalar_prefetch=1, grid=(rows // tr,), in_specs=in_specs,
        out_specs=pl.BlockSpec((None, None, tr, cols), lambda i, p: (l, p[1], i, 0)))
    return pl.pallas_call(body, name=name, grid_spec=grid_spec, out_shape=jax.ShapeDtypeStruct((DEPTH, 2, rows, cols), F32),
                          input_output_aliases=aliases, compiler_params=_params("parallel"))(*args)


def _adamw(w, g, m, v, name):
    shape = w.shape
    lead, (rows, cols) = shape[:-2], shape[-2:]
    tr = _row_tile(rows, cols, 7)

    def body(w_ref, g_ref, m_ref, v_ref, d_ref, mo_ref, vo_ref):
        gv = g_ref[...]
        mn = ADAM_B1 * m_ref[...] + (1.0 - ADAM_B1) * gv
        vn = ADAM_B2 * v_ref[...] + (1.0 - ADAM_B2) * (gv * gv)
        m_hat = mn / (1.0 - ADAM_B1 ** ADAM_STEP)
        v_hat = vn / (1.0 - ADAM_B2 ** ADAM_STEP)
        d_ref[...] = -ADAM_LR * (m_hat / (jnp.sqrt(v_hat) + ADAM_EPS) + ADAM_WD * w_ref[...])
        mo_ref[...] = mn
        vo_ref[...] = vn

    spec = pl.BlockSpec((None,) * len(lead) + (tr, cols), lambda *idx: (*idx, 0))
    grid = lead + (rows // tr,)
    return list(pl.pallas_call(body, name=name, grid=grid, in_specs=[spec] * 4, out_specs=[spec] * 3,
                               out_shape=[jax.ShapeDtypeStruct(shape, F32)] * 3,
                               compiler_params=_params(*(["parallel"] * len(grid))))(w, g, m, v))


def _adamw_small(ws, gs, ms, vs, name):
    n = len(ws)

    def body(*refs):
        for i in range(n):
            gv = refs[n + i][...]
            mn = ADAM_B1 * refs[2 * n + i][...] + (1.0 - ADAM_B1) * gv
            vn = ADAM_B2 * refs[3 * n + i][...] + (1.0 - ADAM_B2) * (gv * gv)
            m_hat = mn / (1.0 - ADAM_B1 ** ADAM_STEP)
            v_hat = vn / (1.0 - ADAM_B2 ** ADAM_STEP)
            refs[4 * n + i][...] = -ADAM_LR * (m_hat / (jnp.sqrt(v_hat) + ADAM_EPS) + ADAM_WD * refs[i][...])
            refs[5 * n + i][...] = mn
            refs[6 * n + i][...] = vn

    vm = pl.BlockSpec(memory_space=pltpu.VMEM)
    outs = pl.pallas_call(body, name=name, out_shape=[jax.ShapeDtypeStruct(t.shape, F32) for t in ws] * 3,
                          in_specs=[vm] * (4 * n), out_specs=[vm] * (3 * n),
                          compiler_params=pltpu.CompilerParams(vmem_limit_bytes=VMEM_LIMIT))(*ws, *gs, *ms, *vs)
    return outs[:n], outs[n:2 * n], outs[2 * n:]


SMALL = ("norm_mix", "sg_ln_g", "sg_ln_b", "sg_w", "sg_b", "cv_w", "cv_b", "cv_ln_g", "cv_ln_b", "attn_sinks", "sc_w", "norm_ffn", "norm_final")
ORDER = ("norm_mix", "w_in", "sg_ln_g", "sg_ln_b", "sg_w", "sg_b", "cv_w", "cv_b", "cv_ln_g", "cv_ln_b", "attn_sinks", "sc_w",
         "w_branch", "w_out", "norm_ffn", "w_gate_up", "w_down", "norm_final")


def kernel(x, norm_mix, w_in, sg_ln_g, sg_ln_b, sg_w, sg_b, cv_w, cv_b, cv_ln_g, cv_ln_b, attn_sinks, sc_w, w_branch, w_out, norm_ffn, w_gate_up, w_down, norm_final, loss_target, m_norm_mix, m_w_in, m_sg_ln_g, m_sg_ln_b, m_sg_w, m_sg_b, m_cv_w, m_cv_b, m_cv_ln_g, m_cv_ln_b, m_attn_sinks, m_sc_w, m_w_branch, m_w_out, m_norm_ffn, m_w_gate_up, m_w_down, m_norm_final, v_norm_mix, v_w_in, v_sg_ln_g, v_sg_ln_b, v_sg_w, v_sg_b, v_cv_w, v_cv_b, v_cv_ln_g, v_cv_ln_b, v_attn_sinks, v_sc_w, v_w_branch, v_w_out, v_norm_ffn, v_w_gate_up, v_w_down, v_norm_final):
    W = dict(norm_mix=norm_mix, w_in=w_in, sg_ln_g=sg_ln_g, sg_ln_b=sg_ln_b, sg_w=sg_w, sg_b=sg_b, cv_w=cv_w, cv_b=cv_b, cv_ln_g=cv_ln_g,
             cv_ln_b=cv_ln_b, attn_sinks=attn_sinks, sc_w=sc_w, w_branch=w_branch, w_out=w_out, norm_ffn=norm_ffn, w_gate_up=w_gate_up,
             w_down=w_down, norm_final=norm_final)
    M = dict(norm_mix=m_norm_mix, w_in=m_w_in, sg_ln_g=m_sg_ln_g, sg_ln_b=m_sg_ln_b, sg_w=m_sg_w, sg_b=m_sg_b, cv_w=m_cv_w, cv_b=m_cv_b,
             cv_ln_g=m_cv_ln_g, cv_ln_b=m_cv_ln_b, attn_sinks=m_attn_sinks, sc_w=m_sc_w, w_branch=m_w_branch, w_out=m_w_out,
             norm_ffn=m_norm_ffn, w_gate_up=m_w_gate_up, w_down=m_w_down, norm_final=m_norm_final)
    V = dict(norm_mix=v_norm_mix, w_in=v_w_in, sg_ln_g=v_sg_ln_g, sg_ln_b=v_sg_ln_b, sg_w=v_sg_w, sg_b=v_sg_b, cv_w=v_cv_w, cv_b=v_cv_b,
             cv_ln_g=v_cv_ln_g, cv_ln_b=v_cv_ln_b, attn_sinks=v_attn_sinks, sc_w=v_sc_w, w_branch=v_w_branch, w_out=v_w_out,
             norm_ffn=v_norm_ffn, w_gate_up=v_w_gate_up, w_down=v_w_down, norm_final=v_norm_final)
    mx, my, mc = lax.axis_index("x"), lax.axis_index("y"), lax.axis_index("c")
    chip = 2 * mx + my

    place = jnp.stack([chip, mc]).astype(jnp.int32)
    tables = _rope_tables(x.shape[1])
    land_shapes = [(N_CHIPS,) + HALF_SHAPE[n] for n in BIG]
    part_shapes = {n: (N_CHIPS,) + HALF_SHAPE[n][1:] for n in BIG}

    T_ = lambda t: jnp.swapaxes(t, 1, 2)
    Wt, Mt, Vt = ({**t, "w_in": T_(t["w_in"])} for t in (W, M, V))

    def shards_of(l, tok):
        return [(Wt[n][l] + tok[0, 0]).astype(BF16).reshape(HALF_SHAPE[n]) for n in BIG]

    def finish_gather(tag, handle, after):
        srcs, lands = _ici_wait(handle, after, f"ag_wait{tag}")
        return _ag_pair(srcs, lands, f"ag_pair{tag}")[0]

    def mix_weights(l, g_in):
        return dict(w_in=_w_in_layout(g_in[0].reshape(N_CHIPS, W_IN_SHARD, D_MODEL)), norm_mix=norm_mix[l][None], norm_ffn=norm_ffn[l][None],
                    mixer=_mixer_params(l, sg_ln_g, sg_ln_b, sg_w, sg_b, cvw_full, cv_b, cv_ln_g, cv_ln_b, attn_sinks, scw_full))

    def rest_weights(lw, g_rest):
        G = dict(zip(BIG[1:], g_rest))
        lw.update(w_branch=G["w_branch"].reshape(N_CHIPS, N_BRANCH, HALF, 256), w_out=G["w_out"].reshape(D_MODEL, D_MODEL),
                  w_gate_up=G["w_gate_up"].reshape(N_CHIPS, D_MODEL, GU_SHARD), w_down=G["w_down"].reshape(D_FF, D_MODEL))

    def shard_major(g):
        t = dict(g)
        if "w_in" in t:
            t["w_in"] = _w_in_unlayout(t["w_in"])
        return {n: t[n].reshape((N_CHIPS,) + HALF_SHAPE[n]) for n in BIG if n in t}

    def pair_sums(tag, g):
        names = list(g)
        got = _rs_pair([g[n] for n in names], f"rs_pair{tag}")
        return names, [_add_pairs(g[n], got[k], place, f"rs_add{tag}_{n}") for k, n in enumerate(names)]

    zero_tok = jnp.zeros((8, 128), F32)
    south = (mc == 0).astype(F32)
    cvw_z = lax.dynamic_update_slice(jnp.zeros((DEPTH, CV_KERNEL, HALF), F32), cv_w * south, (0, 0, chip * 128))
    scw_z = lax.dynamic_update_slice(jnp.zeros((DEPTH, SC_KERNEL, HALF), F32), sc_w * south, (0, 0, chip * 128))
    cvw_full, scw_full, tok = _gather_small_weights(cvw_z, scw_z)

    handles = []
    for l in range(DEPTH):
        for tag, sl in (("in", slice(0, 1)), ("rest", slice(1, NB))):
            h, tok = _ici_start("gather", shards_of(l, tok)[sl], land_shapes[sl], f"ag_start{l}{tag}")
            handles.append(h)
    pending = {}

    def behind(l, key):
        def order(lw, token):
            if key == "mixer":
                lw["mixer"] = [lw["mixer"][0] + token[0, 0]] + lw["mixer"][1:]
            else:
                lw[key] = lw[key] + token[0, 0]
        return order

    def early_pair(tag, handle, order):
        def between(after, lw):
            srcs, lands = _ici_wait(handle, after, f"ag_wait{tag}")
            pending[tag], token = _d2d_start(srcs + lands, [], _forward_plan(len(srcs)), f"ag_pair_start{tag}")
            order(lw, token)
        return between

    def finish_pair(tag, after):
        arrays = _d2d_wait(pending.pop(tag), after, f"ag_pair_wait{tag}")
        return arrays[len(arrays) // 2:]

    lw0 = mix_weights(0, finish_gather("0in", handles[0], tok))
    mixed = _fwd_layer_mix(0, x[0], lw0, tables)
    rest_weights(lw0, finish_gather("0rest", handles[1], mixed[2]))
    x1, sv0 = _fwd_layer_rest(0, x[0], mixed, lw0, early_pair("1in", handles[2], behind(0, "norm_ffn")))
    lw1 = mix_weights(1, finish_pair("1in", x1))
    mixed = _fwd_layer_mix(1, x1, lw1, tables, early_pair("1rest", handles[3], behind(1, "mixer")))
    rest_weights(lw1, finish_pair("1rest", mixed[2]))
    x2, sv1 = _fwd_layer_rest(1, x1, mixed, lw1)
    dx, d_nfinal, loss = _final_loss(x2, norm_final[None], loss_target[0], 256, "final_loss")

    lw1["after"] = zero_tok
    carry, g_ffn1 = _bwd_layer_ffn(1, dx, lw1, sv1)
    g1 = shard_major(g_ffn1)
    names_f = list(g1)
    h_swap, tok = _d2d_start([g1[n] for n in names_f], [part_shapes[n] for n in names_f], _swap_plan(len(names_f)), "rs_pair_start1")
    behind(1, "mixer")(lw1, tok)
    def early_swap(tag):
        def between(d_win, lw):
            g = shard_major({"w_in": d_win})["w_in"]
            pending[tag], token = _d2d_start([g], [part_shapes["w_in"]], _swap_plan(1), f"rs_pair_start{tag}")
            behind(None, "norm_mix")(lw, token)
        return between

    dx, g_mix1 = _bwd_layer_mix(1, carry, lw1, sv1, tables, early_swap("1in"))
    swapped = _d2d_wait(h_swap, dx, "rs_pair_wait1")
    own_in, got_in = _d2d_wait(pending.pop("1in"), dx, "rs_pair_wait1in")
    names1 = ["w_in"] + names_f
    own1 = [own_in] + swapped[:len(names_f)]
    got1 = [got_in] + swapped[len(names_f):]
    part1 = [_add_pairs(own1[k], got1[k], place, f"rs_add1_{n}") for k, n in enumerate(names1)]
    hr1, tok = _ici_start("scatter", part1, [part_shapes[n] for n in names1], "rs_start1")

    lw0["after"] = tok
    carry, g_ffn0 = _bwd_layer_ffn(0, dx, lw0, sv0)
    names_a, part_a = pair_sums("0a", shard_major(g_ffn0))
    _, recv1 = _ici_wait(hr1, part_a[0], "rs_wait1")
    hra, tok = _ici_start("scatter", part_a, [part_shapes[n] for n in names_a], "rs_start0a")

    lw0["mixer"] = [lw0["mixer"][0] + tok[0, 0]] + lw0["mixer"][1:]
    dx, g_mix0 = _bwd_layer_mix(0, carry, lw0, sv0, tables, early_swap("0in"))
    _, recv_a = _ici_wait(hra, dx, "rs_wait0a")

    small_red, nf_red, loss_red, tok = _all_reduce_small_grads([{**g_ffn0, **g_mix0}, {**g_ffn1, **g_mix1}], d_nfinal, loss)
    small_red["norm_final"] = nf_red
    loss_out = loss_red[0, 0]
    for n in ("cv_w", "sc_w"):
        small_red[n] = lax.dynamic_slice_in_dim(small_red[n], chip * 128, 128, axis=2)

    own_in, got_in = _d2d_wait(pending.pop("0in"), tok, "rs_pair_wait0in")
    names_b, part_b = ["w_in"], [_add_pairs(own_in, got_in, place, "rs_add0b_w_in")]
    hrb, tok = _ici_start("scatter", part_b, [part_shapes[n] for n in names_b], "rs_start0b")
    bufs = {n: _sum_chips(part1[k], recv1[k], place, 1, None, f"rs_sum1_{n}", tok) for k, n in enumerate(names1)}
    for k, n in enumerate(names_a):
        bufs[n] = _sum_chips(part_a[k], recv_a[k], place, 0, bufs[n], f"rs_sum0_{n}", tok)
    shared = dict(zip(names_a, _rs_share([bufs[n] for n in names_a], "rs_share_a")))
    upd = {}
    for n in names_a:
        red = shared[n].reshape(W[n].shape)
        upd[n] = [red] + _adamw(W[n], red, M[n], V[n], f"adamw_{n}")
    two_d = lambda t: t[None] if t.ndim == 1 else t
    small_upd = _adamw_small(*([two_d(t[n]) for n in SMALL] for t in (W, small_red, M, V)), "adamw_small")
    for n, d, mo, vo in zip(SMALL, *small_upd):
        upd[n] = [t.reshape(W[n].shape) for t in (small_red[n], d, mo, vo)]

    _, recv_b = _ici_wait(hrb, upd[names_a[-1]][1], "rs_wait0b")
    for k, n in enumerate(names_b):
        bufs[n] = _sum_chips(part_b[k], recv_b[k], place, 0, bufs[n], f"rs_sum0_{n}", tok)
    shared = dict(zip(names_b, _rs_share([bufs[n] for n in names_b], "rs_share_b")))
    for n in names_b:
        red = shared[n].reshape(Wt[n].shape)
        upd[n] = [T_(t) for t in [red] + _adamw(Wt[n], red, Mt[n], Vt[n], f"adamw_{n}")]

    out = [loss_out, dx[None]]
    for k in range(4):
        out += [upd[n][k] for n in ORDER]
    return tuple(out)
```

```python
import functools
import math

import jax
import jax.numpy as jnp
from jax import lax
from jax.experimental import pallas as pl
from jax.experimental.pallas import tpu as pltpu

F32 = jnp.float32
BF16 = jnp.bfloat16

D_MODEL = 1024
DEPTH = 2
HALF = 512
SG_CHUNK = 128
SG_GROUPS = 4
CV_KERNEL = 31
HEAD_DIM = 64
N_Q_HEADS = 8
N_KV_HEADS = 2
Q_PER_KV = N_Q_HEADS // N_KV_HEADS
WINDOW = 128
ROPE_THETA = 10000.0
SC_KERNEL = 3
N_BRANCH = 4
D_FF = 2816
EPS = 1e-6
N_CHIPS = 4
N_DEV = 8

MIX_W = 4352
GATE_W = N_BRANCH * D_MODEL
PROJ_PAD = 2 * MIX_W
W_IN_SHARD = 2112
GU_SHARD = 1408
HALO = 128
CV_PAD = 32

ADAM_LR = 0.001
ADAM_B1 = 0.9
ADAM_B2 = 0.999
ADAM_EPS = 1e-08
ADAM_WD = 0.01
ADAM_STEP = 10

VMEM_LIMIT = 56 * 1024 * 1024
INV_SQRT2 = 1.0 / math.sqrt(2.0)
INV_SQRT_2PI = 1.0 / math.sqrt(2.0 * math.pi)
NEG_BIG = -1e30
MESH = pl.DeviceIdType.MESH

C_ZA, C_ZB, C_Q, C_K, C_V, C_ZD = 0, 1024, 2048, 2560, 2688, 2816


def _params(*sem):
    return pltpu.CompilerParams(dimension_semantics=sem, vmem_limit_bytes=VMEM_LIMIT)


def _sig(v):
    return 1.0 / (1.0 + jnp.exp(-v))


def _dot(a, b):
    return jnp.dot(a, b, preferred_element_type=F32)


def _dot_nt(a, b):
    return lax.dot_general(a, b, (((1,), (1,)), ((), ())), preferred_element_type=F32)


def _dot_tn(a, b):
    return lax.dot_general(a, b, (((0,), (0,)), ((), ())), preferred_element_type=F32)


def _full(shape):
    nd = len(shape)
    return pl.BlockSpec(shape, lambda *_: (0,) * nd)


def _rms_mm(x, g, w, tm, tn, name):
    T = x.shape[0]
    transposed = w.ndim == 2
    if transposed:
        N = w.shape[0]
        wspec = pl.BlockSpec((tn, D_MODEL), lambda i, j: (j, 0))
    else:
        tn = w.shape[2]
        N = w.shape[0] * tn
        wspec = pl.BlockSpec((None, D_MODEL, tn), lambda i, j: (j, 0, 0))

    def body(x_ref, g_ref, w_ref, o_ref, xn_ref):
        @pl.when(pl.program_id(1) == 0)
        def _():
            xv = x_ref[...]
            r = lax.rsqrt(jnp.mean(xv * xv, axis=-1, keepdims=True) + EPS)
            xn_ref[...] = (xv * r * g_ref[...]).astype(BF16)

        o_ref[...] = (_dot_nt if transposed else _dot)(xn_ref[...], w_ref[...]).astype(BF16)

    return pl.pallas_call(
        body, name=name, grid=(T // tm, N // tn),
        in_specs=[pl.BlockSpec((tm, D_MODEL), lambda i, j: (i, 0)), _full((1, D_MODEL)), wspec],
        out_specs=[pl.BlockSpec((tm, tn), lambda i, j: (i, j)), pl.BlockSpec((tm, D_MODEL), lambda i, j: (i, 0))],
        out_shape=[jax.ShapeDtypeStruct((T, N), BF16), jax.ShapeDtypeStruct((T, D_MODEL), BF16)],
        compiler_params=_params("parallel", "arbitrary"),
    )(x, g, w)


def _merge_fwd(x, ys, proj, wb, wo, tm, name):
    T = x.shape[0]

    def body(x_ref, ys_ref, zg_ref, wb_ref, wo_ref, xo_ref, mg_ref):
        merged = None
        for n in range(N_BRANCH):
            yn = ys_ref[:, n * HALF:(n + 1) * HALF]
            br = jnp.concatenate([_dot(yn, wb_ref[s, n]) for s in range(N_CHIPS)], axis=1)
            t = _sig(zg_ref[:, n * D_MODEL:(n + 1) * D_MODEL].astype(F32)) * br
            merged = t if merged is None else merged + t
        mb = merged.astype(BF16)
        mg_ref[...] = mb
        xo_ref[...] = x_ref[...] + _dot(mb, wo_ref[...])

    return pl.pallas_call(
        body, name=name, grid=(T // tm,),
        in_specs=[pl.BlockSpec((tm, D_MODEL), lambda i: (i, 0)), pl.BlockSpec((tm, N_BRANCH * HALF), lambda i: (i, 0)),
                  pl.BlockSpec((tm, GATE_W), lambda i: (i, 0)), _full(wb.shape), _full(wo.shape)],
        out_specs=[pl.BlockSpec((tm, D_MODEL), lambda i: (i, 0)), pl.BlockSpec((tm, D_MODEL), lambda i: (i, 0))],
        out_shape=[jax.ShapeDtypeStruct((T, D_MODEL), F32), jax.ShapeDtypeStruct((T, D_MODEL), BF16)],
        compiler_params=_params("parallel"),
    )(x, ys, proj, wb, wo)


def _ffn_down(xm, gu, wd, tm, name):
    T = xm.shape[0]

    def body(x_ref, gu_ref, wd_ref, o_ref):
        g = gu_ref[:, :D_FF].astype(F32)
        u = gu_ref[:, D_FF:].astype(F32)
        act = (g * _sig(g) * u).astype(BF16)
        o_ref[...] = x_ref[...] + _dot(act, wd_ref[...])

    return pl.pallas_call(
        body, name=name, grid=(T // tm,),
        in_specs=[pl.BlockSpec((tm, D_MODEL), lambda i: (i, 0)), pl.BlockSpec((tm, 2 * D_FF), lambda i: (i, 0)), _full(wd.shape)],
        out_specs=pl.BlockSpec((tm, D_MODEL), lambda i: (i, 0)),
        out_shape=jax.ShapeDtypeStruct((T, D_MODEL), F32),
        compiler_params=_params("parallel"),
    )(xm, gu, wd)


def _final_loss(x, g, tgt, tm, name):
    T = x.shape[0]

    def body(x_ref, g_ref, t_ref, dx_ref, dg_ref, ls_ref):
        @pl.when(pl.program_id(0) == 0)
        def _():
            dg_ref[...] = jnp.zeros_like(dg_ref)
            ls_ref[...] = jnp.zeros_like(ls_ref)

        xv = x_ref[...]
        gv = g_ref[...]
        r = lax.rsqrt(jnp.mean(xv * xv, axis=-1, keepdims=True) + EPS)
        xh = xv * r
        diff = xh * gv - t_ref[...]
        ls_ref[...] += jnp.full(ls_ref.shape, 0.5 / D_MODEL, F32) * jnp.sum(diff * diff)
        dy = diff * (1.0 / D_MODEL)
        dxh = dy * gv
        dx_ref[...] = r * (dxh - xh * jnp.mean(dxh * xh, axis=-1, keepdims=True))
        dg_ref[...] += jnp.sum(dy * xh, axis=0, keepdims=True)

    return pl.pallas_call(
        body, name=name, grid=(T // tm,),
        in_specs=[pl.BlockSpec((tm, D_MODEL), lambda i: (i, 0)), _full((1, D_MODEL)), pl.BlockSpec((tm, D_MODEL), lambda i: (i, 0))],
        out_specs=[pl.BlockSpec((tm, D_MODEL), lambda i: (i, 0)), _full((1, D_MODEL)), _full((1, 128))],
        out_shape=[jax.ShapeDtypeStruct((T, D_MODEL), F32), jax.ShapeDtypeStruct((1, D_MODEL), F32), jax.ShapeDtypeStruct((1, 128), F32)],
        compiler_params=_params("arbitrary"),
    )(x, g, tgt)


def _swiglu_bwd(dx, gu, wd, tm, name, after):
    T = dx.shape[0]

    def body(dx_ref, gu_ref, wd_ref, after_ref, dgu_ref, act_ref):
        del after_ref
        dact = _dot_nt(dx_ref[...].astype(BF16), wd_ref[...])
        g = gu_ref[:, :D_FF].astype(F32)
        u = gu_ref[:, D_FF:].astype(F32)
        s = _sig(g)
        silu = g * s
        act_ref[...] = (silu * u).astype(BF16)
        dgu_ref[:, :D_FF] = (dact * u * (s + silu * (1.0 - s))).astype(BF16)
        dgu_ref[:, D_FF:] = (dact * silu).astype(BF16)

    return pl.pallas_call(
        body, name=name, grid=(T // tm,),
        in_specs=[pl.BlockSpec((tm, D_MODEL), lambda i: (i, 0)), pl.BlockSpec((tm, 2 * D_FF), lambda i: (i, 0)), _full(wd.shape),
                  pl.BlockSpec(memory_space=pl.ANY)],
        out_specs=[pl.BlockSpec((tm, 2 * D_FF), lambda i: (i, 0)), pl.BlockSpec((tm, D_FF), lambda i: (i, 0))],
        out_shape=[jax.ShapeDtypeStruct((T, 2 * D_FF), BF16), jax.ShapeDtypeStruct((T, D_FF), BF16)],
        compiler_params=_params("parallel"),
    )(dx, gu, wd, after)


def _mm_tn(a, b, grid, a_block, a_map, b_block, b_map, o_shape, o_block, o_map, name, col_split=1):
    gk = grid[2]
    tm = [d for d in a_block if d is not None][-1]
    tn = [d for d in b_block if d is not None][-1]

    def body(a_ref, b_ref, o_ref, acc_ref):
        k = pl.program_id(2)
        p = _dot_tn(a_ref[...].astype(BF16), b_ref[...].astype(BF16))

        @pl.when(k == 0)
        def _():
            acc_ref[...] = p

        @pl.when(k > 0)
        def _():
            acc_ref[...] += p

        @pl.when(k == gk - 1)
        def _():
            if col_split == 1:
                o_ref[...] = acc_ref[...].astype(o_ref.dtype)
            else:
                w = tn // col_split
                for s in range(col_split):
                    o_ref[s] = acc_ref[:, s * w:(s + 1) * w].astype(o_ref.dtype)

    return pl.pallas_call(
        body, name=name, grid=grid,
        in_specs=[pl.BlockSpec(a_block, a_map), pl.BlockSpec(b_block, b_map)],
        out_specs=pl.BlockSpec(o_block, o_map),
        out_shape=jax.ShapeDtypeStruct(o_shape, BF16),
        scratch_shapes=[pltpu.VMEM((tm, tn), F32)],
        compiler_params=_params("parallel", "parallel", "arbitrary"),
    )(a, b)


def _mm_nt_rmsbwd(a, w, x, g, dres, tm, tk, name):
    T = x.shape[0]
    transposed = w.ndim == 2
    if transposed:
        gk = w.shape[0] // tk
        wspec = pl.BlockSpec((tk, D_MODEL), lambda i, k: (k, 0))
    else:
        tk = w.shape[2]
        gk = w.shape[0]
        wspec = pl.BlockSpec((None, D_MODEL, tk), lambda i, k: (k, 0, 0))

    def body(a_ref, w_ref, x_ref, g_ref, r_ref, dx_ref, dg_ref, acc_ref):
        i, k = pl.program_id(0), pl.program_id(1)
        p = (_dot if transposed else _dot_nt)(a_ref[...], w_ref[...])

        @pl.when(k == 0)
        def _():
            acc_ref[...] = p

        @pl.when(k > 0)
        def _():
            acc_ref[...] += p

        @pl.when(jnp.logical_and(i == 0, k == 0))
        def _():
            dg_ref[...] = jnp.zeros_like(dg_ref)

        @pl.when(k == gk - 1)
        def _():
            dh = acc_ref[...]
            xv = x_ref[...]
            r = lax.rsqrt(jnp.mean(xv * xv, axis=-1, keepdims=True) + EPS)
            xh = xv * r
            dxh = dh * g_ref[...]
            dx_ref[...] = r_ref[...] + r * (dxh - xh * jnp.mean(dxh * xh, axis=-1, keepdims=True))
            dg_ref[...] += jnp.sum(dh * xh, axis=0, keepdims=True)

    return pl.pallas_call(
        body, name=name, grid=(T // tm, gk),
        in_specs=[pl.BlockSpec((tm, tk), lambda i, k: (i, k)), wspec, pl.BlockSpec((tm, D_MODEL), lambda i, k: (i, 0)),
                  _full((1, D_MODEL)), pl.BlockSpec((tm, D_MODEL), lambda i, k: (i, 0))],
        out_specs=[pl.BlockSpec((tm, D_MODEL), lambda i, k: (i, 0)), _full((1, D_MODEL))],
        out_shape=[jax.ShapeDtypeStruct((T, D_MODEL), F32), jax.ShapeDtypeStruct((1, D_MODEL), F32)],
        scratch_shapes=[pltpu.VMEM((tm, D_MODEL), F32)],
        compiler_params=_params("arbitrary", "arbitrary"),
    )(a, w, x, g, dres)


def _merge_bwd(dxm, ys, proj, wb, wo, tm, name):
    T = dxm.shape[0]

    def body(dx_ref, ys_ref, zg_ref, wb_ref, wo_ref, dys_ref, dbr_ref, dp_ref):
        dmerged = _dot_nt(dx_ref[...].astype(BF16), wo_ref[...])
        for n in range(N_BRANCH):
            yn = ys_ref[:, n * HALF:(n + 1) * HALF]
            br = jnp.concatenate([_dot(yn, wb_ref[s, n]) for s in range(N_CHIPS)], axis=1)
            gt = _sig(zg_ref[:, n * D_MODEL:(n + 1) * D_MODEL].astype(F32))
            dbr = (gt * dmerged).astype(BF16)
            dbr_ref[:, n * D_MODEL:(n + 1) * D_MODEL] = dbr
            dp_ref[:, n * D_MODEL:(n + 1) * D_MODEL] = (dmerged * br * gt * (1.0 - gt)).astype(BF16)
            dy = None
            for s in range(N_CHIPS):
                t = _dot_nt(dbr[:, s * 256:(s + 1) * 256], wb_ref[s, n])
                dy = t if dy is None else dy + t
            dys_ref[:, n * HALF:(n + 1) * HALF] = dy.astype(BF16)
        dp_ref[:, GATE_W:] = jnp.zeros((tm, MIX_W - GATE_W), BF16)

    return pl.pallas_call(
        body, name=name, grid=(T // tm,),
        in_specs=[pl.BlockSpec((tm, D_MODEL), lambda i: (i, 0)), pl.BlockSpec((tm, N_BRANCH * HALF), lambda i: (i, 0)),
                  pl.BlockSpec((tm, GATE_W), lambda i: (i, 0)), _full(wb.shape), _full(wo.shape)],
        out_specs=[pl.BlockSpec((tm, N_BRANCH * HALF), lambda i: (i, 0)), pl.BlockSpec((tm, GATE_W), lambda i: (i, 0)),
                   pl.BlockSpec((tm, MIX_W), lambda i: (i, 0))],
        out_shape=[jax.ShapeDtypeStruct((T, N_BRANCH * HALF), BF16), jax.ShapeDtypeStruct((T, GATE_W), BF16),
                   jax.ShapeDtypeStruct((T, PROJ_PAD), BF16)],
        compiler_params=_params("parallel"),
    )(dxm, ys, proj, wb, wo)


def _gelu(v):
    return 0.5 * v * (1.0 + lax.erf(v * INV_SQRT2))


def _gelu_grad(v):
    return 0.5 * (1.0 + lax.erf(v * INV_SQRT2)) + v * jnp.exp(-0.5 * v * v) * INV_SQRT_2PI


def _rot_half(t):
    w = t.shape[1]
    lane = lax.broadcasted_iota(jnp.int32, t.shape, 1)
    return jnp.where((lane % HEAD_DIM) < HEAD_DIM // 2, pltpu.roll(t, w - HEAD_DIM // 2, 1), pltpu.roll(t, HEAD_DIM // 2, 1))


def _rope(t, cos, sin_signed):
    return t * cos + _rot_half(t) * sin_signed


def _rope_t(d, cos, sin_signed):
    return d * cos + _rot_half(d * sin_signed)


def _ln_fwd(v, g, b):
    mu = jnp.mean(v, axis=-1, keepdims=True)
    vc = v - mu
    r = lax.rsqrt(jnp.mean(vc * vc, axis=-1, keepdims=True) + EPS)
    vh = vc * r
    return vh * g + b, vh, r


def _ln_bwd(dn, vh, r, g):
    dvh = dn * g
    return r * (dvh - jnp.mean(dvh, axis=-1, keepdims=True) - vh * jnp.mean(dvh * vh, axis=-1, keepdims=True))


def _sublane_shifts(sh_ref, rows):
    for b in range(1, 8):
        sh_ref[b, 0:rows - 8, :] = sh_ref[0, pl.ds(b, rows - 8), :]


def _tap(sh_ref, off, n):
    return sh_ref[off % 8, pl.ds(off - off % 8, n), :]


def _tril_mask():
    return lax.broadcasted_iota(jnp.int32, (SG_CHUNK, SG_CHUNK), 0) >= lax.broadcasted_iota(jnp.int32, (SG_CHUNK, SG_CHUNK), 1)


def _band_masks():
    shape = (Q_PER_KV * WINDOW, 2 * WINDOW)
    row = lax.broadcasted_iota(jnp.int32, shape, 0) % WINDOW
    col = lax.broadcasted_iota(jnp.int32, shape, 1)
    band = (col > row) & (col <= row + WINDOW)
    return band, band & (col >= WINDOW)


def _attn_probs(qs, kh, sink_col, valid):
    s = jnp.where(valid, _dot_nt(qs, kh) * (HEAD_DIM ** -0.5), NEG_BIG)
    m = jnp.maximum(jnp.max(s, axis=-1, keepdims=True), sink_col)
    p = jnp.exp(s - m)
    es = jnp.exp(sink_col - m)
    inv = 1.0 / (jnp.sum(p, axis=-1, keepdims=True) + es)
    return p * inv, es * inv


def _sink_col(sinks_ref, h):
    return jnp.concatenate([jnp.broadcast_to(sinks_ref[:, h * Q_PER_KV + g:h * Q_PER_KV + g + 1], (WINDOW, 1))
                            for g in range(Q_PER_KV)], axis=0)


def _mixer_in_specs(TB, nb):
    r = TB // HALO
    last = nb * r - 1
    cur = pl.BlockSpec((TB, MIX_W), lambda i: (i, 1))
    prev = pl.BlockSpec((HALO, MIX_W), lambda i: (jnp.maximum(i * r - 1, 0), 1))
    nxt = pl.BlockSpec((HALO, MIX_W), lambda i: (jnp.minimum((i + 1) * r, last), 1))
    tcur = pl.BlockSpec((TB, 128), lambda i: (i, 0))
    tprev = pl.BlockSpec((HALO, 128), lambda i: (jnp.maximum(i * r - 1, 0), 0))
    tnxt = pl.BlockSpec((HALO, 128), lambda i: (jnp.minimum((i + 1) * r, last), 0))
    return cur, prev, nxt, tcur, tprev, tnxt


def _mixer_param_specs():
    return [_full((1, HALF)), _full((1, HALF)), _full((SG_GROUPS, SG_CHUNK, SG_CHUNK)), _full((SG_CHUNK, 128)),
            _full((32, HALF)), _full((1, HALF)), _full((1, HALF)), _full((1, HALF)), _full((1, 128)), _full((8, HALF))]


def _mixers_fwd(proj, cos_t, sin_t, mp, TB, name):
    T = proj.shape[0]
    nb = T // TB
    r = TB // HALO
    cur, prev, _, tcur, tprev, _ = _mixer_in_specs(TB, nb)

    def body(zc_ref, zp_ref, cc_ref, sc_ref, cp_ref, sp_ref,
             lg_ref, lb_ref, sgw_ref, sgb_ref, cvw_ref, cvb_ref, cvg_ref, cvbb_ref, sinks_ref, scw_ref,
             ys_ref, scr_ref, k_ref, v_ref, sh_ref):
        i = pl.program_id(0)
        pm = (i > 0).astype(F32)

        def colsE(c0, c1):
            return jnp.concatenate([zp_ref[:, c0:c1].astype(F32) * pm, zc_ref[:, c0:c1].astype(F32)], axis=0)

        a = _gelu(zc_ref[:, C_ZA:C_ZA + 2 * HALF].astype(F32))
        u = a[:, :HALF]
        vn, _, _ = _ln_fwd(a[:, HALF:], lg_ref[...], lb_ref[...])
        vnb = vn.astype(BF16)
        tril = _tril_mask()
        chunks = [slice(ci * SG_CHUNK, (ci + 1) * SG_CHUNK) for ci in range(r)]
        for g in range(SG_GROUPS):
            cols = slice(g * 128, (g + 1) * 128)
            wt = jnp.where(tril, sgw_ref[g], 0.0).astype(BF16)
            mixed = _dot(wt, jnp.concatenate([vnb[rows, cols] for rows in chunks], axis=1)) + sgb_ref[:, g:g + 1]
            for ci, rows in enumerate(chunks):
                ys_ref[rows, cols] = (u[rows, cols] * mixed[:, ci * 128:(ci + 1) * 128]).astype(BF16)

        def colsB(c0, c1):
            return jnp.concatenate([zp_ref[HALO - CV_PAD:, c0:c1].astype(F32) * pm, zc_ref[:, c0:c1].astype(F32)], axis=0)

        sh_ref[0] = colsB(C_ZB, C_ZB + HALF) * _sig(colsB(C_ZB + HALF, C_ZB + 2 * HALF))
        _sublane_shifts(sh_ref, TB + CV_PAD)
        c = jnp.broadcast_to(cvb_ref[...], (TB, HALF))
        for k in range(CV_KERNEL):
            c = c + cvw_ref[k:k + 1, :] * _tap(sh_ref, CV_PAD - (CV_KERNEL - 1) + k, TB)
        n, _, _ = _ln_fwd(c, cvg_ref[...], cvbb_ref[...])
        ys_ref[:, HALF:2 * HALF] = (n * _sig(n)).astype(BF16)

        zd = colsE(C_ZD + HALF, C_ZD + 3 * HALF)
        scr_ref[...] = zd[:, :HALF] * zd[:, HALF:]
        cv = None
        for k in range(SC_KERNEL):
            t = scw_ref[k:k + 1, :] * scr_ref[pl.ds(HALO - (SC_KERNEL - 1) + k, TB), :]
            cv = t if cv is None else cv + t
        ys_ref[:, 3 * HALF:4 * HALF] = (zc_ref[:, C_ZD:C_ZD + HALF].astype(F32) * cv).astype(BF16)

        cosE = jnp.concatenate([cp_ref[...], cc_ref[...]], axis=0)
        sinE = jnp.concatenate([sp_ref[...], sc_ref[...]], axis=0)
        k_ref[...] = _rope(colsE(C_K, C_K + 128), cosE, sinE).astype(BF16)
        v_ref[...] = colsE(C_V, C_V + 128).astype(BF16)
        cosC, sinC = cc_ref[...], sc_ref[...]
        q = jnp.concatenate([_rope(zc_ref[:, C_Q + 128 * j:C_Q + 128 * (j + 1)].astype(F32), cosC, sinC)
                             for j in range(4)], axis=1).astype(BF16)
        in_band, in_band_cur = _band_masks()
        sink_cols = [_sink_col(sinks_ref, h) for h in range(N_KV_HEADS)]
        for qb in range(r):
            valid = in_band if qb else in_band_cur | (in_band & (i > 0))
            for h in range(N_KV_HEADS):
                hc = slice(h * HEAD_DIM, (h + 1) * HEAD_DIM)
                kh = k_ref[qb * WINDOW:qb * WINDOW + 2 * WINDOW, hc]
                vh = v_ref[qb * WINDOW:qb * WINDOW + 2 * WINDOW, hc]
                qs = jnp.concatenate([q[qb * WINDOW:(qb + 1) * WINDOW, (h * Q_PER_KV + g) * HEAD_DIM:(h * Q_PER_KV + g + 1) * HEAD_DIM]
                                      for g in range(Q_PER_KV)], axis=0)
                probs, _ = _attn_probs(qs, kh, sink_cols[h], valid)
                o = _dot(probs.astype(BF16), vh)
                for g in range(Q_PER_KV):
                    c0 = 2 * HALF + (h * Q_PER_KV + g) * HEAD_DIM
                    ys_ref[qb * WINDOW:(qb + 1) * WINDOW, c0:c0 + HEAD_DIM] = o[g * WINDOW:(g + 1) * WINDOW].astype(BF16)

    return pl.pallas_call(
        body, name=name, grid=(nb,),
        in_specs=[cur, prev, tcur, tcur, tprev, tprev] + _mixer_param_specs(),
        out_specs=pl.BlockSpec((TB, 4 * HALF), lambda i: (i, 0)),
        out_shape=jax.ShapeDtypeStruct((T, 4 * HALF), BF16),
        scratch_shapes=[pltpu.VMEM((TB + HALO, HALF), F32), pltpu.VMEM((TB + HALO, 128), BF16), pltpu.VMEM((TB + HALO, 128), BF16),
                        pltpu.VMEM((8, TB + CV_PAD, HALF), F32)],
        compiler_params=_params("parallel"),
    )(proj, proj, cos_t, sin_t, cos_t, sin_t, *mp)


def _mixers_bwd(proj, dys, dproj, cos_t, sin_t, mp, TB, name):
    T = proj.shape[0]
    nb = T // TB
    r = TB // HALO
    RE = TB + 2 * HALO
    RC = TB + HALO
    cur, prev, nxt, tcur, tprev, tnxt = _mixer_in_specs(TB, nb)
    dcur = pl.BlockSpec((TB, 4 * HALF), lambda i: (i, 0))
    dnxt = pl.BlockSpec((HALO, 4 * HALF), lambda i: (jnp.minimum((i + 1) * r, nb * r - 1), 0))

    def body(zc_ref, zp_ref, zn_ref, dyc_ref, dyn_ref, cc_ref, sc_ref, cp_ref, sp_ref, cn_ref, sn_ref,
             lg_ref, lb_ref, sgw_ref, sgb_ref, cvw_ref, cvb_ref, cvg_ref, cvbb_ref, sinks_ref, scw_ref, dp_in_ref,
             dz_ref, dlg_ref, dlb_ref, dsgw_ref, dsgb_ref, dcvw_ref, dcvb_ref, dcvg_ref, dcvbb_ref, dsink_ref, dscw_ref,
             scr_ref, scr2_ref, k_ref, v_ref, dk_ref, dv_ref, dq_ref, sh_ref, sh2_ref):
        del dp_in_ref
        i = pl.program_id(0)
        pm = (i > 0).astype(F32)
        nm = (i < nb - 1).astype(F32)

        @pl.when(i == 0)
        def _():
            for ref in (dlg_ref, dlb_ref, dsgw_ref, dsgb_ref, dcvw_ref, dcvb_ref, dcvg_ref, dcvbb_ref, dsink_ref, dscw_ref):
                ref[...] = jnp.zeros_like(ref)

        def colsE(c0, c1):
            return jnp.concatenate([zp_ref[:, c0:c1].astype(F32) * pm, zc_ref[:, c0:c1].astype(F32),
                                    zn_ref[:, c0:c1].astype(F32)], axis=0)

        def colsC(c0, c1):
            return jnp.concatenate([zc_ref[:, c0:c1].astype(F32), zn_ref[:, c0:c1].astype(F32)], axis=0)

        def dyC(c0, c1):
            return jnp.concatenate([dyc_ref[:, c0:c1].astype(F32), dyn_ref[:, c0:c1].astype(F32) * nm], axis=0)

        za = zc_ref[:, C_ZA:C_ZA + 2 * HALF].astype(F32)
        a = _gelu(za)
        u = a[:, :HALF]
        lg = lg_ref[...]
        vn, vh, rs = _ln_fwd(a[:, HALF:], lg, lb_ref[...])
        vnb = vn.astype(BF16)
        dya = dyc_ref[:, 0:HALF].astype(F32)
        tril = _tril_mask()
        lane128 = lax.broadcasted_iota(jnp.int32, (SG_CHUNK, 128), 1)
        chunks = [slice(ci * SG_CHUNK, (ci + 1) * SG_CHUNK) for ci in range(r)]
        side = lambda t, cols: jnp.concatenate([t[rows, cols] for rows in chunks], axis=1)
        for g in range(SG_GROUPS):
            cols = slice(g * 128, (g + 1) * 128)
            wt = jnp.where(tril, sgw_ref[g], 0.0).astype(BF16)
            vb = side(vnb, cols)
            dy_blk = side(dya, cols)
            du_g = dy_blk * (_dot(wt, vb) + sgb_ref[:, g:g + 1])
            dmix = dy_blk * side(u, cols)
            dmb = dmix.astype(BF16)
            dvn_g = _dot_tn(wt, dmb)
            dsgw_ref[g] += jnp.where(tril, _dot_nt(dmb, vb), 0.0)
            dsgb_ref[...] += jnp.where(lane128 == g, jnp.sum(dmix, axis=1, keepdims=True), 0.0)
            for ci, rows in enumerate(chunks):
                scr_ref[rows, cols] = du_g[:, ci * 128:(ci + 1) * 128]
                scr2_ref[rows, cols] = dvn_g[:, ci * 128:(ci + 1) * 128]
        du, dvn = scr_ref[0:TB, :], scr2_ref[0:TB, :]
        dlg_ref[...] += jnp.sum(dvn * vh, axis=0, keepdims=True)
        dlb_ref[...] += jnp.sum(dvn, axis=0, keepdims=True)
        dvv = _ln_bwd(dvn, vh, rs, lg)
        gg = _gelu_grad(za)
        dz_ref[:, C_ZA:C_ZA + HALF] = (du * gg[:, :HALF]).astype(BF16)
        dz_ref[:, C_ZA + HALF:C_ZA + 2 * HALF] = (dvv * gg[:, HALF:]).astype(BF16)

        RB = TB + CV_PAD

        def colsB(c0, c1):
            return jnp.concatenate([zp_ref[HALO - CV_PAD:, c0:c1].astype(F32) * pm, zc_ref[:, c0:c1].astype(F32),
                                    zn_ref[:CV_PAD, c0:c1].astype(F32)], axis=0)

        sh_ref[0] = colsB(C_ZB, C_ZB + HALF) * _sig(colsB(C_ZB + HALF, C_ZB + 2 * HALF))
        _sublane_shifts(sh_ref, RB + CV_PAD)
        c = jnp.broadcast_to(cvb_ref[...], (RB, HALF))
        for k in range(CV_KERNEL):
            c = c + cvw_ref[k:k + 1, :] * _tap(sh_ref, CV_PAD - (CV_KERNEL - 1) + k, RB)
        cvg = cvg_ref[...]
        n, ch, rc = _ln_fwd(c, cvg, cvbb_ref[...])
        sn = _sig(n)
        dyb = jnp.concatenate([dyc_ref[:, HALF:2 * HALF].astype(F32), dyn_ref[:CV_PAD, HALF:2 * HALF].astype(F32) * nm], axis=0)
        dn = dyb * (sn + n * sn * (1.0 - sn))
        dno = dn[:TB]
        dcvg_ref[...] += jnp.sum(dno * ch[:TB], axis=0, keepdims=True)
        dcvbb_ref[...] += jnp.sum(dno, axis=0, keepdims=True)
        dc = _ln_bwd(dn, ch, rc, cvg)
        sh2_ref[0] = dc
        _sublane_shifts(sh2_ref, RB)
        dcvb_ref[...] += jnp.sum(dc[:TB], axis=0, keepdims=True)
        dy0 = None
        for k in range(CV_KERNEL):
            wk = cvw_ref[k:k + 1, :]
            t = wk * _tap(sh2_ref, CV_KERNEL - 1 - k, TB)
            dy0 = t if dy0 is None else dy0 + t
            dcvw_ref[k:k + 1, :] += jnp.sum(dc[:TB] * _tap(sh_ref, CV_PAD - (CV_KERNEL - 1) + k, TB), axis=0, keepdims=True)
        ab = zc_ref[:, C_ZB:C_ZB + HALF].astype(F32)
        sg = _sig(zc_ref[:, C_ZB + HALF:C_ZB + 2 * HALF].astype(F32))
        dz_ref[:, C_ZB:C_ZB + HALF] = (dy0 * sg).astype(BF16)
        dz_ref[:, C_ZB + HALF:C_ZB + 2 * HALF] = (dy0 * ab * sg * (1.0 - sg)).astype(BF16)

        zd = colsE(C_ZD + HALF, C_ZD + 3 * HALF)
        scr_ref[...] = zd[:, :HALF] * zd[:, HALF:]
        dcv = dyC(3 * HALF, 4 * HALF) * colsC(C_ZD, C_ZD + HALF)
        scr2_ref[...] = dcv
        cv = None
        dud = None
        for k in range(SC_KERNEL):
            wk = scw_ref[k:k + 1, :]
            us = scr_ref[pl.ds(HALO - (SC_KERNEL - 1) + k, TB), :]
            t = wk * us
            cv = t if cv is None else cv + t
            t2 = wk * scr2_ref[pl.ds(SC_KERNEL - 1 - k, TB), :]
            dud = t2 if dud is None else dud + t2
            dscw_ref[k:k + 1, :] += jnp.sum(dcv[:TB] * us, axis=0, keepdims=True)
        dz_ref[:, C_ZD:C_ZD + HALF] = (dyc_ref[:, 3 * HALF:4 * HALF].astype(F32) * cv).astype(BF16)
        dz_ref[:, C_ZD + HALF:C_ZD + 2 * HALF] = (dud * zc_ref[:, C_ZD + 2 * HALF:C_ZD + 3 * HALF].astype(F32)).astype(BF16)
        dz_ref[:, C_ZD + 2 * HALF:C_ZD + 3 * HALF] = (dud * zc_ref[:, C_ZD + HALF:C_ZD + 2 * HALF].astype(F32)).astype(BF16)

        cosE = jnp.concatenate([cp_ref[...], cc_ref[...], cn_ref[...]], axis=0)
        sinE = jnp.concatenate([sp_ref[...], sc_ref[...], sn_ref[...]], axis=0)
        k_ref[...] = _rope(colsE(C_K, C_K + 128), cosE, sinE).astype(BF16)
        v_ref[...] = colsE(C_V, C_V + 128).astype(BF16)
        dk_ref[...] = jnp.zeros_like(dk_ref)
        dv_ref[...] = jnp.zeros_like(dv_ref)
        q = jnp.concatenate([_rope(colsC(C_Q + 128 * j, C_Q + 128 * (j + 1)), cosE[HALO:], sinE[HALO:])
                             for j in range(4)], axis=1).astype(BF16)
        dO = dyC(2 * HALF, 3 * HALF).astype(BF16)
        lane_s = lax.broadcasted_iota(jnp.int32, (1, 128), 1)
        in_band, in_band_cur = _band_masks()
        sink_cols = [_sink_col(sinks_ref, h) for h in range(N_KV_HEADS)]
        for qb in range(r + 1):
            valid = in_band if qb else in_band_cur | (in_band & (i > 0))
            rows = slice(qb * WINDOW, (qb + 1) * WINDOW)
            band = slice(qb * WINDOW, qb * WINDOW + 2 * WINDOW)
            for h in range(N_KV_HEADS):
                hc = slice(h * HEAD_DIM, (h + 1) * HEAD_DIM)
                kh = k_ref[band, hc]
                vh_ = v_ref[band, hc]
                heads = [slice((h * Q_PER_KV + g) * HEAD_DIM, (h * Q_PER_KV + g + 1) * HEAD_DIM) for g in range(Q_PER_KV)]
                qs = jnp.concatenate([q[rows, hs] for hs in heads], axis=0)
                dos = jnp.concatenate([dO[rows, hs] for hs in heads], axis=0)
                probs, p_sink = _attn_probs(qs, kh, sink_cols[h], valid)
                dP = _dot_nt(dos, vh_)
                rsum = jnp.sum(probs * dP, axis=-1, keepdims=True)
                dS = (probs * (dP - rsum) * (HEAD_DIM ** -0.5)).astype(BF16)
                dk_ref[band, hc] += _dot_tn(dS, qs)
                dv_ref[band, hc] += _dot_tn(probs.astype(BF16), dos)
                if qb < r:
                    dqs = _dot(dS, kh)
                    dsk = -p_sink * rsum
                    for g in range(Q_PER_KV):
                        dq_ref[rows, heads[g]] = dqs[g * WINDOW:(g + 1) * WINDOW]
                        dsink_ref[...] += jnp.where(lane_s == h * Q_PER_KV + g, jnp.sum(dsk[g * WINDOW:(g + 1) * WINDOW]), 0.0)
        cosC, sinC = cc_ref[...], sc_ref[...]
        for j in range(4):
            dz_ref[:, C_Q + 128 * j:C_Q + 128 * (j + 1)] = _rope_t(dq_ref[:, 128 * j:128 * (j + 1)], cosC, sinC).astype(BF16)
        dz_ref[:, C_K:C_K + 128] = _rope_t(dk_ref[HALO:HALO + TB, :], cosC, sinC).astype(BF16)
        dz_ref[:, C_V:C_V + 128] = dv_ref[HALO:HALO + TB, :].astype(BF16)

    small = [((1, HALF), F32), ((1, HALF), F32), ((SG_GROUPS, SG_CHUNK, SG_CHUNK), F32), ((SG_CHUNK, 128), F32),
             ((32, HALF), F32), ((1, HALF), F32), ((1, HALF), F32), ((1, HALF), F32), ((1, 128), F32), ((8, HALF), F32)]
    outs = pl.pallas_call(
        body, name=name, grid=(nb,),
        in_specs=[cur, prev, nxt, dcur, dnxt, tcur, tcur, tprev, tprev, tnxt, tnxt] + _mixer_param_specs()
                 + [pl.BlockSpec(memory_space=pl.ANY)],
        out_specs=[pl.BlockSpec((TB, MIX_W), lambda i: (i, 1))] + [_full(s) for s, _ in small],
        out_shape=[jax.ShapeDtypeStruct((T, PROJ_PAD), BF16)] + [jax.ShapeDtypeStruct(s, d) for s, d in small],
        scratch_shapes=[pltpu.VMEM((RE, HALF), F32), pltpu.VMEM((RC, HALF), F32), pltpu.VMEM((RE, 128), BF16), pltpu.VMEM((RE, 128), BF16),
                        pltpu.VMEM((RE, 128), F32), pltpu.VMEM((RE, 128), F32), pltpu.VMEM((TB, HALF), F32),
                        pltpu.VMEM((8, TB + 2 * CV_PAD, HALF), F32), pltpu.VMEM((8, TB + CV_PAD, HALF), F32)],
        input_output_aliases={21: 0},
        compiler_params=_params("arbitrary"),
    )(proj, proj, proj, dys, dys, cos_t, sin_t, cos_t, sin_t, cos_t, sin_t, *mp, dproj)
    return outs


def _rope_tables(T):
    pos = jnp.arange(T, dtype=F32)
    inv_freq = 1.0 / (ROPE_THETA ** (jnp.arange(0, HEAD_DIM, 2, dtype=F32) / HEAD_DIM))
    ang = pos[:, None] * inv_freq[None, :]
    cos, sin = jnp.cos(ang), jnp.sin(ang)
    cos_t = jnp.concatenate([cos, cos, cos, cos], axis=1)
    sin_t = jnp.concatenate([-sin, sin, -sin, sin], axis=1)
    return cos_t, sin_t


def _mixer_params(l, sg_ln_g, sg_ln_b, sg_w, sg_b, cv_w, cv_b, cv_ln_g, cv_ln_b, attn_sinks, sc_w):
    sgb_t = jnp.zeros((SG_CHUNK, 128), F32).at[:, :SG_GROUPS].set(sg_b[l].T)
    cvw = jnp.zeros((32, HALF), F32).at[:CV_KERNEL].set(cv_w[l])
    scw = jnp.zeros((8, HALF), F32).at[:SC_KERNEL].set(sc_w[l])
    sinks = jnp.zeros((1, 128), F32).at[0, :N_Q_HEADS].set(attn_sinks[l])
    return [sg_ln_g[l][None], sg_ln_b[l][None], sg_w[l], sgb_t, cvw, cv_b[l][None], cv_ln_g[l][None], cv_ln_b[l][None], sinks, scw]


def _w_in_layout(w_in_g):
    cut = MIX_W - 2 * W_IN_SHARD
    return jnp.concatenate([w_in_g[2][cut:], w_in_g[3], jnp.zeros((MIX_W - GATE_W, D_MODEL), w_in_g.dtype),
                            w_in_g[0], w_in_g[1], w_in_g[2][:cut]], axis=0)


def _w_in_unlayout(dw):
    cut = MIX_W - 2 * W_IN_SHARD
    return jnp.stack([dw[MIX_W:MIX_W + W_IN_SHARD], dw[MIX_W + W_IN_SHARD:MIX_W + 2 * W_IN_SHARD],
                      jnp.concatenate([dw[MIX_W + 2 * W_IN_SHARD:], dw[:W_IN_SHARD - cut]], axis=0),
                      dw[W_IN_SHARD - cut:GATE_W]], axis=0)


def _device_step(x, tgt, norm_mix, norm_ffn, norm_final, mixer_params, w_in_p, wb_g, wo_g, wgu_g, wd_g):
    T = x.shape[0]
    tables = _rope_tables(T)
    saved = []
    for l in range(DEPTH):
        lw = dict(w_in=w_in_p[l], w_branch=wb_g[l], w_out=wo_g[l], w_gate_up=wgu_g[l], w_down=wd_g[l],
                  norm_mix=norm_mix[l][None], norm_ffn=norm_ffn[l][None], mixer=mixer_params[l], after=jnp.zeros((8, 128), F32))
        x, sv = _fwd_layer(l, x, lw, tables)
        saved.append((lw, sv))
    dx, dnf, loss = _final_loss(x, norm_final[None], tgt, 256, "final_loss")
    grads = [None] * DEPTH
    for l in reversed(range(DEPTH)):
        lw, sv = saved[l]
        dxm, g_ffn = _bwd_layer_ffn(l, dx, lw, sv)
        dx, g_mix = _bwd_layer_mix(l, dxm, lw, sv, tables)
        raw = {**g_ffn, **g_mix}
        grads[l] = {**raw, **_small_views(raw)}
    return loss, dx, dnf[0], grads


MIX_BLOCK = 256


def _fwd_layer(l, x, lw, tables):
    return _fwd_layer_rest(l, x, _fwd_layer_mix(l, x, lw, tables), lw)


def _fwd_layer_mix(l, x, lw, tables, between=None):
    proj, xn = _rms_mm(x, lw["norm_mix"], lw["w_in"], min(x.shape[0], 1024), 2176, f"proj{l}")
    if between is not None:
        between(proj, lw)
    return proj, xn, _mixers_fwd(proj, *tables, lw["mixer"], MIX_BLOCK, f"mixers_fwd{l}")


def _fwd_layer_rest(l, x, mixed, lw, between=None):
    proj, xn, ys = mixed
    TM = min(x.shape[0], 1024)
    xm, merged = _merge_fwd(x, ys, proj, lw["w_branch"], lw["w_out"], min(x.shape[0], 512), f"merge_fwd{l}")
    if between is not None:
        between(xm, lw)
    gu, hn = _rms_mm(xm, lw["norm_ffn"], lw["w_gate_up"], TM, GU_SHARD, f"ffn_up{l}")
    x_out = _ffn_down(xm, gu, lw["w_down"], 256, f"ffn_down{l}")
    return x_out, (x, proj, xn, ys, xm, merged, gu, hn)


def _bwd_layer_ffn(l, dx, lw, sv, between=None):
    x_in, proj, xn, ys, xm, merged, gu, hn = sv
    T = dx.shape[0]
    tkk = min(T, 1024)
    gk = T // tkk
    dgu, act = _swiglu_bwd(dx, gu, lw["w_down"], 256, f"swiglu_bwd{l}", lw["after"])
    d_wd = _mm_tn(act, dx, (2, 1, gk), (tkk, D_FF // 2), lambda i, j, k: (k, i), (tkk, D_MODEL), lambda i, j, k: (k, 0),
                  (D_FF, D_MODEL), (D_FF // 2, D_MODEL), lambda i, j, k: (i, 0), f"dw_down{l}")
    d_wgu = _mm_tn(hn, dgu, (1, N_CHIPS, gk), (tkk, D_MODEL), lambda i, j, k: (k, 0), (tkk, GU_SHARD), lambda i, j, k: (k, j),
                   (N_CHIPS, D_MODEL, GU_SHARD), (None, D_MODEL, GU_SHARD), lambda i, j, k: (j, 0, 0), f"dw_gate_up{l}")
    if between is not None:
        between(dict(w_gate_up=d_wgu, w_down=d_wd), lw)
    dxm, d_nffn = _mm_nt_rmsbwd(dgu, lw["w_gate_up"], xm, lw["norm_ffn"], dx, min(T, 1024), GU_SHARD, f"ffn_up_bwd{l}")
    dys, dbr, dproj = _merge_bwd(dxm, ys, proj, lw["w_branch"], lw["w_out"], 256, f"merge_bwd{l}")
    d_wo = _mm_tn(merged, dxm, (2, 1, gk), (tkk, 512), lambda i, j, k: (k, i), (tkk, D_MODEL), lambda i, j, k: (k, 0),
                  (D_MODEL, D_MODEL), (512, D_MODEL), lambda i, j, k: (i, 0), f"dw_out{l}")
    d_wb = _mm_tn(ys, dbr, (N_BRANCH, 1, gk), (tkk, HALF), lambda i, j, k: (k, i), (tkk, D_MODEL), lambda i, j, k: (k, i),
                  (N_CHIPS, N_BRANCH, HALF, 256), (N_CHIPS, None, HALF, 256), lambda i, j, k: (0, i, 0, 0), f"dw_branch{l}", col_split=N_CHIPS)
    return (dxm, dys, dproj), dict(w_branch=d_wb, w_out=d_wo, w_gate_up=d_wgu, w_down=d_wd, norm_ffn=d_nffn)


def _bwd_layer_mix(l, carry, lw, sv, tables, between=None):
    dxm, dys, dproj = carry
    x_in, proj, xn, ys, xm, merged, gu, hn = sv
    T = dxm.shape[0]
    tkk = min(T, 1024)
    gk = T // tkk
    mb = _mixers_bwd(proj, dys, dproj, *tables, lw["mixer"], MIX_BLOCK, f"mixers_bwd{l}")
    dproj = mb[0]
    d_win = _mm_tn(dproj, xn, (PROJ_PAD // 2176, 1, gk), (tkk, 2176), lambda i, j, k: (k, i), (tkk, D_MODEL), lambda i, j, k: (k, 0),
                   (PROJ_PAD, D_MODEL), (2176, D_MODEL), lambda i, j, k: (i, 0), f"dw_in{l}")
    if between is not None:
        between(d_win, lw)
    dx, d_nmix = _mm_nt_rmsbwd(dproj, lw["w_in"], x_in, lw["norm_mix"], dxm, min(T, 1024), 2176, f"proj_bwd{l}")
    return dx, dict(w_in=d_win, norm_mix=d_nmix, sg_ln_g=mb[1], sg_ln_b=mb[2], sg_w=mb[3], sg_b=mb[4], cv_w=mb[5], cv_b=mb[6],
                    cv_ln_g=mb[7], cv_ln_b=mb[8], attn_sinks=mb[9], sc_w=mb[10])


ANY = pl.BlockSpec(memory_space=pl.ANY)
BIG = ("w_in", "w_branch", "w_out", "w_gate_up", "w_down")
HALF_SHAPE = {"w_in": (2, W_IN_SHARD // 2, D_MODEL), "w_branch": (2, 1024, 256), "w_out": (2, 128, D_MODEL),
              "w_gate_up": (2, 512, GU_SHARD), "w_down": (2, 352, D_MODEL)}
NB = len(BIG)


def _place():
    x, y, c = lax.axis_index("x"), lax.axis_index("y"), lax.axis_index("c")
    chips = [(1 - x, y), (x, 1 - y), (1 - x, 1 - y)]
    return x, y, c, 2 * x + y, chips, [2 * px + py for px, py in chips]


def _remote(src, dst, ssem, rsem, dev):
    return pltpu.make_async_remote_copy(src_ref=src, dst_ref=dst, send_sem=ssem, recv_sem=rsem, device_id=dev, device_id_type=MESH)


HBM_SPEC = pl.BlockSpec(memory_space=pltpu.HBM)
SEM_SPEC = pl.BlockSpec(memory_space=pltpu.SEMAPHORE)
DATAFLOW = pltpu.SideEffectType.DATAFLOW_SIDE_EFFECTING


def _ici_ends(kind, src, land, j, c, chip, chip_ids):
    if kind == "gather":
        return src.at[c], land.at[chip, c], land.at[chip_ids[j], c]
    return src.at[chip_ids[j]], land.at[chip], land.at[chip_ids[j]]


def _ici_start(kind, srcs, land_shapes, name):
    n = len(srcs)

    def body(*refs):
        src, land = refs[:n], refs[n:2 * n]
        ssem, rsem, token = refs[2 * n], refs[2 * n + 1], refs[-1]
        x, y, c, chip, chips, chip_ids = _place()
        for k in range(n):
            for j in range(3):
                s, d, _ = _ici_ends(kind, src[k], land[k], j, c, chip, chip_ids)
                _remote(s, d, ssem.at[3 * k + j], rsem.at[3 * k + j], (*chips[j], c)).start()
        token[...] = jnp.zeros_like(token)

    sem = pltpu.SemaphoreType.DMA((3 * n,))
    outs = pl.pallas_call(
        body, name=name,
        out_shape=(sem, sem, *[pltpu.HBM(s.shape, s.dtype) for s in srcs], *[pltpu.HBM(sh, BF16) for sh in land_shapes],
                   jax.ShapeDtypeStruct((8, 128), F32)),
        in_specs=[HBM_SPEC] * (2 * n),
        out_specs=(SEM_SPEC, SEM_SPEC, *[HBM_SPEC] * (2 * n), pl.BlockSpec(memory_space=pltpu.VMEM)),
        input_output_aliases={i: 2 + i for i in range(2 * n)},
        compiler_params=pltpu.CompilerParams(has_side_effects=DATAFLOW),
    )(*[pltpu.with_memory_space_constraint(s, pltpu.HBM) for s in srcs],
      *[pltpu.with_memory_space_constraint(lax.empty(sh, BF16), pltpu.HBM) for sh in land_shapes])
    return (kind, outs[0], outs[1], list(outs[2:2 + n]), list(outs[2 + n:2 + 2 * n])), outs[-1]


def _ici_wait(handle, after, name):
    kind, ssem_in, rsem_in, srcs, lands = handle
    n = len(srcs)

    def body(*refs):
        src, land = refs[:n], refs[n:2 * n]
        ssem, rsem = refs[2 * n], refs[2 * n + 1]
        x, y, c, chip, chips, chip_ids = _place()
        for k in range(n):
            for j in range(3):
                s, _, mine = _ici_ends(kind, src[k], land[k], j, c, chip, chip_ids)
                cp = _remote(s, mine, ssem.at[3 * k + j], rsem.at[3 * k + j], (*chips[j], c))
                cp.wait_send()
                cp.wait_recv()

    outs = pl.pallas_call(
        body, name=name, out_shape=[pltpu.HBM(t.shape, t.dtype) for t in srcs + lands],
        in_specs=[HBM_SPEC] * (2 * n) + [SEM_SPEC, SEM_SPEC, ANY], out_specs=[HBM_SPEC] * (2 * n),
        input_output_aliases={i: i for i in range(2 * n)},
        compiler_params=pltpu.CompilerParams(has_side_effects=DATAFLOW),
    )(*srcs, *lands, ssem_in, rsem_in, after)
    return list(outs[:n]), list(outs[n:])


def _ag_pair(shards, lands, name):
    n = len(shards)

    def body(*refs):
        ins, outs = refs[:n], refs[2 * n:3 * n]
        token = refs[3 * n]
        s_fwd, r_fwd, s_own, r_own = refs[3 * n + 1:]
        x, y, c, chip, chips, chip_ids = _place()
        sib = (x, y, 1 - c)
        cps = []
        for k in range(n):
            cp = _remote(ins[k], outs[k].at[chip], s_own.at[k], r_own.at[k], sib)
            cp.start()
            cps.append(cp)
            for j in range(3):
                got = outs[k].at[chip_ids[j], c]
                cp = _remote(got, got, s_fwd.at[k, j], r_fwd.at[k, j], sib)
                cp.start()
                cps.append(cp)
        for k in range(n):
            _remote(ins[k], outs[k].at[chip], s_own.at[k], r_own.at[k], sib).wait_recv()
            for j in range(3):
                got = outs[k].at[chip_ids[j], 1 - c]
                _remote(got, got, s_fwd.at[k, j], r_fwd.at[k, j], sib).wait_recv()
        for cp in cps:
            cp.wait_send()
        token[...] = jnp.zeros_like(token)

    sem, sem1 = pltpu.SemaphoreType.DMA((n, 3)), pltpu.SemaphoreType.DMA((n,))
    outs = pl.pallas_call(
        body, name=name, out_shape=[jax.ShapeDtypeStruct(t.shape, t.dtype) for t in lands] + [jax.ShapeDtypeStruct((8, 128), F32)],
        in_specs=[ANY] * (2 * n), out_specs=[ANY] * n + [pl.BlockSpec(memory_space=pltpu.VMEM)],
        input_output_aliases={n + k: k for k in range(n)},
        scratch_shapes=[sem, sem, sem1, sem1], compiler_params=pltpu.CompilerParams(has_side_effects=True),
    )(*shards, *lands)
    return list(outs[:n]), outs[n]


def _forward_plan(n):
    def plan(refs, c, chip, chip_ids):
        out = []
        for k in range(n):
            shard, land = refs[k], refs[n + k]
            out.append((shard, land.at[chip], land.at[chip]))
            out += [(land.at[q, c], land.at[q, c], land.at[q, 1 - c]) for q in chip_ids]
        return out
    return plan, 4 * n


def _swap_plan(n):
    def plan(refs, c, chip, chip_ids):
        return [(refs[k].at[q, 1 - c], refs[n + k].at[q], refs[n + k].at[q]) for k in range(n) for q in range(N_CHIPS)]
    return plan, N_CHIPS * n


def _d2d_start(arrays, new_shapes, plan_n, name):
    plan, n_copies = plan_n
    n = len(arrays) + len(new_shapes)

    def body(*refs):
        ssem, rsem, token = refs[n], refs[n + 1], refs[-1]
        x, y, c, chip, _, chip_ids = _place()
        for i, (s, d, _) in enumerate(plan(refs[:n], c, chip, chip_ids)):
            _remote(s, d, ssem.at[i], rsem.at[i], (x, y, 1 - c)).start()
        token[...] = jnp.zeros_like(token)

    sem = pltpu.SemaphoreType.DMA((n_copies,))
    args = [pltpu.with_memory_space_constraint(t, pltpu.HBM) for t in arrays] + \
           [pltpu.with_memory_space_constraint(lax.empty(sh, BF16), pltpu.HBM) for sh in new_shapes]
    outs = pl.pallas_call(
        body, name=name,
        out_shape=(sem, sem, *[pltpu.HBM(t.shape, t.dtype) for t in args], jax.ShapeDtypeStruct((8, 128), F32)),
        in_specs=[HBM_SPEC] * n, out_specs=(SEM_SPEC, SEM_SPEC, *[HBM_SPEC] * n, pl.BlockSpec(memory_space=pltpu.VMEM)),
        input_output_aliases={i: 2 + i for i in range(n)},
        compiler_params=pltpu.CompilerParams(has_side_effects=DATAFLOW),
    )(*args)
    return (plan, outs[0], outs[1], list(outs[2:2 + n])), outs[-1]


def _d2d_wait(handle, after, name):
    plan, ssem_in, rsem_in, arrays = handle
    n = len(arrays)

    def body(*refs):
        ssem, rsem = refs[n], refs[n + 1]
        x, y, c, chip, _, chip_ids = _place()
        for i, (s, _, mine) in enumerate(plan(refs[:n], c, chip, chip_ids)):
            cp = _remote(s, mine, ssem.at[i], rsem.at[i], (x, y, 1 - c))
            cp.wait_send()
            cp.wait_recv()

    outs = pl.pallas_call(
        body, name=name, out_shape=[pltpu.HBM(t.shape, t.dtype) for t in arrays],
        in_specs=[HBM_SPEC] * n + [SEM_SPEC, SEM_SPEC, ANY], out_specs=[HBM_SPEC] * n,
        input_output_aliases={i: i for i in range(n)},
        compiler_params=pltpu.CompilerParams(has_side_effects=DATAFLOW),
    )(*arrays, ssem_in, rsem_in, after)
    return list(outs)


def _rs_pair(grads, name):
    n_arr = len(grads)

    def body(*refs):
        ins, got = refs[:n_arr], refs[n_arr:2 * n_arr]
        ssem, rsem = refs[2 * n_arr:]
        x, y, c, _, _, _ = _place()
        sib = (x, y, 1 - c)
        sends = []
        for k in reversed(range(n_arr)):
            for q in range(N_CHIPS):
                cp = _remote(ins[k].at[q, 1 - c], got[k].at[q], ssem.at[k, q], rsem.at[k, q], sib)
                cp.start()
                sends.append(cp)
        for k in range(n_arr):
            for q in range(N_CHIPS):
                _remote(got[k].at[q], got[k].at[q], ssem.at[k, q], rsem.at[k, q], sib).wait_recv()
        for cp in sends:
            cp.wait_send()

    shp = [jax.ShapeDtypeStruct((N_CHIPS,) + g.shape[2:], BF16) for g in grads]
    sem = pltpu.SemaphoreType.DMA((n_arr, N_CHIPS))
    outs = pl.pallas_call(
        body, name=name, out_shape=shp, in_specs=[ANY] * n_arr, out_specs=[ANY] * n_arr,
        scratch_shapes=[sem, sem], compiler_params=pltpu.CompilerParams(has_side_effects=True),
    )(*grads)
    return list(outs)


def _rs_share(bufs, name):
    n = len(bufs)

    def body(*refs):
        outs = refs[n:2 * n]
        ssem, rsem = refs[2 * n:]
        x, y, c, _, _, _ = _place()
        sib = (x, y, 1 - c)
        sends = []
        for k in range(n):
            for l in range(DEPTH):
                cp = _remote(outs[k].at[l, c], outs[k].at[l, c], ssem.at[k, l], rsem.at[k, l], sib)
                cp.start()
                sends.append(cp)
        for k in range(n):
            for l in range(DEPTH):
                dst = outs[k].at[l, 1 - c]
                _remote(dst, dst, ssem.at[k, l], rsem.at[k, l], sib).wait_recv()
        for cp in sends:
            cp.wait_send()

    sem = pltpu.SemaphoreType.DMA((n, DEPTH))
    outs = pl.pallas_call(
        body, name=name, out_shape=[jax.ShapeDtypeStruct(b.shape, b.dtype) for b in bufs], in_specs=[ANY] * n, out_specs=[ANY] * n,
        input_output_aliases={k: k for k in range(n)},
        scratch_shapes=[sem, sem], compiler_params=pltpu.CompilerParams(has_side_effects=True),
    )(*bufs)
    return list(outs)


def _piece(src, idx, rows, width=128, align=1, transposed=False):
    return dict(src=src, idx=idx, rows=rows, width=width, align=align, transposed=transposed)


def _all_reduce_pieces(inputs, pieces, out_shapes, writes, name):
    n_in, n_out = len(inputs), len(out_shapes)
    offs, R = [], 0
    for p in pieces:
        R = -(-R // p["align"]) * p["align"]
        offs.append(R)
        R += p["rows"]
    R = -(-R // 8) * 8

    def body(*refs):
        ins, outs, token_ref = refs[:n_in], refs[n_in:n_in + n_out], refs[n_in + n_out]
        pair_ref, chip_ref, sum_ref, ssem, rsem = refs[n_in + n_out + 1:]
        token_ref[...] = jnp.zeros_like(token_ref)
        x, y, c, chip, chips, chip_ids = _place()
        pair_ref[c] = jnp.zeros((R, 128), F32)
        for p, off in zip(pieces, offs):
            v = ins[p["src"]][...].T[p["idx"]] if p["transposed"] else ins[p["src"]][p["idx"]]
            pair_ref[c, off:off + p["rows"], 0:p["width"]] = v
        mine = _remote(pair_ref.at[c], pair_ref.at[c], ssem.at[3], rsem.at[3], (x, y, 1 - c))
        mine.start()
        _remote(pair_ref.at[1 - c], pair_ref.at[1 - c], ssem.at[3], rsem.at[3], (x, y, 1 - c)).wait_recv()
        chip_ref[chip] = pair_ref[0] + pair_ref[1]
        cps = [_remote(chip_ref.at[chip], chip_ref.at[chip], ssem.at[j], rsem.at[j], (*chips[j], c)) for j in range(3)]
        for cp in cps:
            cp.start()
        for j in range(3):
            slot = chip_ref.at[chip_ids[j]]
            _remote(slot, slot, ssem.at[j], rsem.at[j], (*chips[j], c)).wait_recv()
        acc = chip_ref[0]
        for s in range(1, N_CHIPS):
            acc = acc + chip_ref[s]
        sum_ref[...] = acc
        for o, idx, p in writes:
            outs[o][idx] = sum_ref[offs[p]:offs[p] + pieces[p]["rows"], 0:pieces[p]["width"]]
        for cp in cps + [mine]:
            cp.wait_send()

    vm = pl.BlockSpec(memory_space=pltpu.VMEM)
    outs = pl.pallas_call(
        body, name=name, out_shape=[jax.ShapeDtypeStruct(s, F32) for s in out_shapes] + [jax.ShapeDtypeStruct((8, 128), F32)],
        in_specs=[vm] * n_in, out_specs=[vm] * (n_out + 1),
        scratch_shapes=[pltpu.VMEM((2, R, 128), F32), pltpu.VMEM((N_CHIPS, R, 128), F32), pltpu.VMEM((R, 128), F32),
                        pltpu.SemaphoreType.DMA((4,)), pltpu.SemaphoreType.DMA((4,))],
        compiler_params=pltpu.CompilerParams(vmem_limit_bytes=VMEM_LIMIT),
    )(*inputs)
    return list(outs[:n_out]), outs[n_out]


def _lanes(width):
    return [slice(k, min(k + 128, width)) for k in range(0, width, 128)]


def _gather_small_weights(cvw_z, scw_z):
    pieces, writes = [], []
    for i, arr in enumerate((cvw_z, scw_z)):
        for l in range(DEPTH):
            for ln in _lanes(HALF):
                writes.append((i, (l, slice(None), ln), len(pieces)))
                pieces.append(_piece(i, (l, slice(None), ln), arr.shape[1], align=8))
    (cvw, scw), tok = _all_reduce_pieces([cvw_z, scw_z], pieces, [cvw_z.shape, scw_z.shape], writes, "ag_small")
    return cvw, scw, tok


SMALL_RAW = dict(norm_mix=(1, D_MODEL), norm_ffn=(1, D_MODEL), sg_ln_g=(1, HALF), sg_ln_b=(1, HALF), cv_b=(1, HALF), cv_ln_g=(1, HALF),
                 cv_ln_b=(1, HALF))


def _all_reduce_small_grads(raw, d_nfinal, loss):
    names = list(SMALL_RAW) + ["attn_sinks", "sg_b", "sc_w", "cv_w", "sg_w"]
    out_shape = dict(norm_mix=(DEPTH, D_MODEL), norm_ffn=(DEPTH, D_MODEL), sg_ln_g=(DEPTH, HALF), sg_ln_b=(DEPTH, HALF), cv_b=(DEPTH, HALF),
                     cv_ln_g=(DEPTH, HALF), cv_ln_b=(DEPTH, HALF), attn_sinks=(DEPTH, N_Q_HEADS), sg_b=(DEPTH, SG_GROUPS, SG_CHUNK),
                     sc_w=(DEPTH, SC_KERNEL, HALF), cv_w=(DEPTH, CV_KERNEL, HALF), sg_w=(DEPTH, SG_GROUPS, SG_CHUNK, SG_CHUNK))
    inputs, pieces, writes = [], [], []

    def add(src, idx, rows, out, out_idx, **kw):
        writes.append((names.index(out) if out in names else out, out_idx, len(pieces)))
        pieces.append(_piece(src, idx, rows, **kw))

    for l in range(DEPTH):
        row = slice(l, l + 1)
        for n, (_, width) in SMALL_RAW.items():
            inputs.append(raw[l][n])
            for ln in _lanes(width):
                add(len(inputs) - 1, (slice(0, 1), ln), 1, n, (row, ln))
        inputs.append(raw[l]["attn_sinks"])
        add(len(inputs) - 1, (slice(0, 1), slice(0, N_Q_HEADS)), 1, "attn_sinks", (row, slice(None)), width=N_Q_HEADS)
    for l in range(DEPTH):
        inputs.append(raw[l]["sg_b"])
        add(len(inputs) - 1, (slice(0, SG_GROUPS), slice(None)), SG_GROUPS, "sg_b", (l,), align=8, transposed=True)
        inputs.append(raw[l]["sc_w"])
        for ln in _lanes(HALF):
            add(len(inputs) - 1, (slice(0, SC_KERNEL), ln), SC_KERNEL, "sc_w", (l, slice(None), ln), align=8)
        inputs.append(raw[l]["cv_w"])
        for ln in _lanes(HALF):
            add(len(inputs) - 1, (slice(0, CV_KERNEL), ln), CV_KERNEL, "cv_w", (l, slice(None), ln), align=8)
        inputs.append(raw[l]["sg_w"])
        for g in range(SG_GROUPS):
            add(len(inputs) - 1, (g,), SG_CHUNK, "sg_w", (l, g), align=8)
    n_names = len(names)
    inputs.append(d_nfinal)
    for ln in _lanes(D_MODEL):
        add(len(inputs) - 1, (slice(0, 1), ln), 1, n_names, (slice(0, 1), ln))
    inputs.append(loss)
    add(len(inputs) - 1, (slice(0, 1), slice(None)), 1, n_names + 1, (slice(0, 1), slice(None)))
    outs, tok = _all_reduce_pieces(inputs, pieces, [out_shape[n] for n in names] + [(1, D_MODEL), (1, 128)], writes, "ar_small")
    return dict(zip(names, outs[:n_names])), outs[n_names], outs[n_names + 1], tok


def _small_views(raw):
    v = {n: raw[n][0] for n in SMALL_RAW}
    v.update(sg_w=raw["sg_w"], sg_b=raw["sg_b"][:, :SG_GROUPS].T, cv_w=raw["cv_w"][:CV_KERNEL],
             attn_sinks=raw["attn_sinks"][0, :N_Q_HEADS], sc_w=raw["sc_w"][:SC_KERNEL])
    return v


def _row_tile(rows, cols, n_arrays):
    budget = 20 * 1024 * 1024 // (n_arrays * 2 * cols * 4)
    tiles = [t for t in range(16, min(rows, budget) + 1, 16) if rows % t == 0]
    assert tiles, (rows, cols)
    return tiles[-1]


def _add_pairs(g, got, place, name):
    _, _, rows, cols = g.shape
    tr = _row_tile(rows, cols, 3)

    def body(place_ref, a_ref, b_ref, o_ref):
        del place_ref
        o_ref[...] = (a_ref[...].astype(F32) + b_ref[...].astype(F32)).astype(BF16)

    spec = pl.BlockSpec((None, tr, cols), lambda q, i, p: (q, i, 0))
    grid_spec = pltpu.PrefetchScalarGridSpec(
        num_scalar_prefetch=1, grid=(N_CHIPS, rows // tr),
        in_specs=[pl.BlockSpec((None, None, tr, cols), lambda q, i, p: (q, p[1], i, 0)), spec], out_specs=spec)
    return pl.pallas_call(body, name=name, grid_spec=grid_spec, out_shape=jax.ShapeDtypeStruct((N_CHIPS, rows, cols), BF16),
                          compiler_params=_params("parallel", "parallel"))(place, g, got)


def _sum_chips(own, recv, place, l, buf, name, after):
    _, rows, cols = own.shape
    tr = _row_tile(rows, cols, 4)

    def body(place_ref, own_ref, recv_ref, *rest):
        chip = place_ref[0]
        acc = own_ref[...].astype(F32)
        for j in range(1, N_CHIPS):
            acc = acc + recv_ref[lax.rem(chip + j, N_CHIPS)].astype(F32)
        rest[-1][...] = acc

    in_specs = [pl.BlockSpec((None, tr, cols), lambda i, p: (p[0], i, 0)), pl.BlockSpec((N_CHIPS, tr, cols), lambda i, p: (0, i, 0)), ANY]
    args = [place, own, recv, after]
    aliases = {}
    if buf is not None:
        in_specs.append(ANY)
        args.append(buf)
        aliases = {4: 0}
    grid_spec = pltpu.PrefetchScalarGridSpec(
        num_scalar_prefetch=1, grid=(rows // tr,), in_specs=in_specs,
        out_specs=pl.BlockSpec((None, None, tr, cols), lambda i, p: (l, p[1], i, 0)))
    return pl.pallas_call(body, name=name, grid_spec=grid_spec, out_shape=jax.ShapeDtypeStruct((DEPTH, 2, rows, cols), F32),
                          input_output_aliases=aliases, compiler_params=_params("parallel"))(*args)


def _adamw(w, g, m, v, name):
    shape = w.shape
    lead, (rows, cols) = shape[:-2], shape[-2:]
    tr = _row_tile(rows, cols, 7)

    def body(w_ref, g_ref, m_ref, v_ref, d_ref, mo_ref, vo_ref):
        gv = g_ref[...]
        mn = ADAM_B1 * m_ref[...] + (1.0 - ADAM_B1) * gv
        vn = ADAM_B2 * v_ref[...] + (1.0 - ADAM_B2) * (gv * gv)
        m_hat = mn / (1.0 - ADAM_B1 ** ADAM_STEP)
        v_hat = vn / (1.0 - ADAM_B2 ** ADAM_STEP)
        d_ref[...] = -ADAM_LR * (m_hat / (jnp.sqrt(v_hat) + ADAM_EPS) + ADAM_WD * w_ref[...])
        mo_ref[...] = mn
        vo_ref[...] = vn

    spec = pl.BlockSpec((None,) * len(lead) + (tr, cols), lambda *idx: (*idx, 0))
    grid = lead + (rows // tr,)
    return list(pl.pallas_call(body, name=name, grid=grid, in_specs=[spec] * 4, out_specs=[spec] * 3,
                               out_shape=[jax.ShapeDtypeStruct(shape, F32)] * 3,
                               compiler_params=_params(*(["parallel"] * len(grid))))(w, g, m, v))


def _adamw_small(ws, gs, ms, vs, name):
    n = len(ws)

    def body(*refs):
        for i in range(n):
            gv = refs[n + i][...]
            mn = ADAM_B1 * refs[2 * n + i][...] + (1.0 - ADAM_B1) * gv
            vn = ADAM_B2 * refs[3 * n + i][...] + (1.0 - ADAM_B2) * (gv * gv)
            m_hat = mn / (1.0 - ADAM_B1 ** ADAM_STEP)
            v_hat = vn / (1.0 - ADAM_B2 ** ADAM_STEP)
            refs[4 * n + i][...] = -ADAM_LR * (m_hat / (jnp.sqrt(v_hat) + ADAM_EPS) + ADAM_WD * refs[i][...])
            refs[5 * n + i][...] = mn
            refs[6 * n + i][...] = vn

    vm = pl.BlockSpec(memory_space=pltpu.VMEM)
    outs = pl.pallas_call(body, name=name, out_shape=[jax.ShapeDtypeStruct(t.shape, F32) for t in ws] * 3,
                          in_specs=[vm] * (4 * n), out_specs=[vm] * (3 * n),
                          compiler_params=pltpu.CompilerParams(vmem_limit_bytes=VMEM_LIMIT))(*ws, *gs, *ms, *vs)
    return outs[:n], outs[n:2 * n], outs[2 * n:]


SMALL = ("norm_mix", "sg_ln_g", "sg_ln_b", "sg_w", "sg_b", "cv_w", "cv_b", "cv_ln_g", "cv_ln_b", "attn_sinks", "sc_w", "norm_ffn", "norm_final")
ORDER = ("norm_mix", "w_in", "sg_ln_g", "sg_ln_b", "sg_w", "sg_b", "cv_w", "cv_b", "cv_ln_g", "cv_ln_b", "attn_sinks", "sc_w",
         "w_branch", "w_out", "norm_ffn", "w_gate_up", "w_down", "norm_final")


def kernel(x, norm_mix, w_in, sg_ln_g, sg_ln_b, sg_w, sg_b, cv_w, cv_b, cv_ln_g, cv_ln_b, attn_sinks, sc_w, w_branch, w_out, norm_ffn, w_gate_up, w_down, norm_final, loss_target, m_norm_mix, m_w_in, m_sg_ln_g, m_sg_ln_b, m_sg_w, m_sg_b, m_cv_w, m_cv_b, m_cv_ln_g, m_cv_ln_b, m_attn_sinks, m_sc_w, m_w_branch, m_w_out, m_norm_ffn, m_w_gate_up, m_w_down, m_norm_final, v_norm_mix, v_w_in, v_sg_ln_g, v_sg_ln_b, v_sg_w, v_sg_b, v_cv_w, v_cv_b, v_cv_ln_g, v_cv_ln_b, v_attn_sinks, v_sc_w, v_w_branch, v_w_out, v_norm_ffn, v_w_gate_up, v_w_down, v_norm_final):
    W = dict(norm_mix=norm_mix, w_in=w_in, sg_ln_g=sg_ln_g, sg_ln_b=sg_ln_b, sg_w=sg_w, sg_b=sg_b, cv_w=cv_w, cv_b=cv_b, cv_ln_g=cv_ln_g,
             cv_ln_b=cv_ln_b, attn_sinks=attn_sinks, sc_w=sc_w, w_branch=w_branch, w_out=w_out, norm_ffn=norm_ffn, w_gate_up=w_gate_up,
             w_down=w_down, norm_final=norm_final)
    M = dict(norm_mix=m_norm_mix, w_in=m_w_in, sg_ln_g=m_sg_ln_g, sg_ln_b=m_sg_ln_b, sg_w=m_sg_w, sg_b=m_sg_b, cv_w=m_cv_w, cv_b=m_cv_b,
             cv_ln_g=m_cv_ln_g, cv_ln_b=m_cv_ln_b, attn_sinks=m_attn_sinks, sc_w=m_sc_w, w_branch=m_w_branch, w_out=m_w_out,
             norm_ffn=m_norm_ffn, w_gate_up=m_w_gate_up, w_down=m_w_down, norm_final=m_norm_final)
    V = dict(norm_mix=v_norm_mix, w_in=v_w_in, sg_ln_g=v_sg_ln_g, sg_ln_b=v_sg_ln_b, sg_w=v_sg_w, sg_b=v_sg_b, cv_w=v_cv_w, cv_b=v_cv_b,
             cv_ln_g=v_cv_ln_g, cv_ln_b=v_cv_ln_b, attn_sinks=v_attn_sinks, sc_w=v_sc_w, w_branch=v_w_branch, w_out=v_w_out,
             norm_ffn=v_norm_ffn, w_gate_up=v_w_gate_up, w_down=v_w_down, norm_final=v_norm_final)
    mx, my, mc = lax.axis_index("x"), lax.axis_index("y"), lax.axis_index("c")
    chip = 2 * mx + my

    place = jnp.stack([chip, mc]).astype(jnp.int32)
    tables = _rope_tables(x.shape[1])
    land_shapes = [(N_CHIPS,) + HALF_SHAPE[n] for n in BIG]
    part_shapes = {n: (N_CHIPS,) + HALF_SHAPE[n][1:] for n in BIG}

    T_ = lambda t: jnp.swapaxes(t, 1, 2)
    Wt, Mt, Vt = ({**t, "w_in": T_(t["w_in"])} for t in (W, M, V))

    def shards_of(l, tok):
        return [(Wt[n][l] + tok[0, 0]).astype(BF16).reshape(HALF_SHAPE[n]) for n in BIG]

    def finish_gather(tag, handle, after):
        srcs, lands = _ici_wait(handle, after, f"ag_wait{tag}")
        return _ag_pair(srcs, lands, f"ag_pair{tag}")[0]

    def mix_weights(l, g_in):
        return dict(w_in=_w_in_layout(g_in[0].reshape(N_CHIPS, W_IN_SHARD, D_MODEL)), norm_mix=norm_mix[l][None], norm_ffn=norm_ffn[l][None],
                    mixer=_mixer_params(l, sg_ln_g, sg_ln_b, sg_w, sg_b, cvw_full, cv_b, cv_ln_g, cv_ln_b, attn_sinks, scw_full))

    def rest_weights(lw, g_rest):
        G = dict(zip(BIG[1:], g_rest))
        lw.update(w_branch=G["w_branch"].reshape(N_CHIPS, N_BRANCH, HALF, 256), w_out=G["w_out"].reshape(D_MODEL, D_MODEL),
                  w_gate_up=G["w_gate_up"].reshape(N_CHIPS, D_MODEL, GU_SHARD), w_down=G["w_down"].reshape(D_FF, D_MODEL))

    def shard_major(g):
        t = dict(g)
        if "w_in" in t:
            t["w_in"] = _w_in_unlayout(t["w_in"])
        return {n: t[n].reshape((N_CHIPS,) + HALF_SHAPE[n]) for n in BIG if n in t}

    zero_tok = jnp.zeros((8, 128), F32)
    handles, tok = [], zero_tok
    for l in range(DEPTH):
        for tag, sl in (("in", slice(0, 1)), ("rest", slice(1, NB))):
            h, tok = _ici_start("gather", shards_of(l, tok)[sl], land_shapes[sl], f"ag_start{l}{tag}")
            handles.append(h)
            if not handles[1:]:
                south = (mc == 0).astype(F32) + tok[0, 0]
                cvw_z = lax.dynamic_update_slice(jnp.zeros((DEPTH, CV_KERNEL, HALF), F32), cv_w * south, (0, 0, chip * 128))
                scw_z = lax.dynamic_update_slice(jnp.zeros((DEPTH, SC_KERNEL, HALF), F32), sc_w * south, (0, 0, chip * 128))
                cvw_full, scw_full, tok = _gather_small_weights(cvw_z, scw_z)
    pending = {}

    def behind(l, key):
        def order(lw, token):
            if key == "mixer":
                lw["mixer"] = [lw["mixer"][0] + token[0, 0]] + lw["mixer"][1:]
            else:
                lw[key] = lw[key] + token[0, 0]
        return order

    def early_pair(tag, handle, order):
        def between(after, lw):
            srcs, lands = _ici_wait(handle, after, f"ag_wait{tag}")
            pending[tag], token = _d2d_start(srcs + lands, [], _forward_plan(len(srcs)), f"ag_pair_start{tag}")
            order(lw, token)
        return between

    def finish_pair(tag, after):
        arrays = _d2d_wait(pending.pop(tag), after, f"ag_pair_wait{tag}")
        return arrays[len(arrays) // 2:]

    lw0 = mix_weights(0, finish_gather("0in", handles[0], tok))
    mixed = _fwd_layer_mix(0, x[0], lw0, tables)
    rest_weights(lw0, finish_gather("0rest", handles[1], mixed[2]))
    x1, sv0 = _fwd_layer_rest(0, x[0], mixed, lw0, early_pair("1in", handles[2], behind(0, "norm_ffn")))
    lw1 = mix_weights(1, finish_pair("1in", x1))
    mixed = _fwd_layer_mix(1, x1, lw1, tables, early_pair("1rest", handles[3], behind(1, "mixer")))
    rest_weights(lw1, finish_pair("1rest", mixed[2]))
    x2, sv1 = _fwd_layer_rest(1, x1, mixed, lw1)
    dx, d_nfinal, loss = _final_loss(x2, norm_final[None], loss_target[0], 256, "final_loss")

    lw1["after"] = zero_tok
    carry, g_ffn1 = _bwd_layer_ffn(1, dx, lw1, sv1)
    g1 = shard_major(g_ffn1)
    names_f = list(g1)
    h_swap, tok = _d2d_start([g1[n] for n in names_f], [part_shapes[n] for n in names_f], _swap_plan(len(names_f)), "rs_pair_start1")
    behind(1, "mixer")(lw1, tok)
    def early_swap(tag):
        def between(d_win, lw):
            g = shard_major({"w_in": d_win})["w_in"]
            pending[tag], token = _d2d_start([g], [part_shapes["w_in"]], _swap_plan(1), f"rs_pair_start{tag}")
            behind(None, "norm_mix")(lw, token)
        return between

    dx, g_mix1 = _bwd_layer_mix(1, carry, lw1, sv1, tables, early_swap("1in"))
    swapped = _d2d_wait(h_swap, dx, "rs_pair_wait1")
    own_in, got_in = _d2d_wait(pending.pop("1in"), dx, "rs_pair_wait1in")
    names1 = ["w_in"] + names_f
    own1 = [own_in] + swapped[:len(names_f)]
    got1 = [got_in] + swapped[len(names_f):]
    part1 = [_add_pairs(own1[k], got1[k], place, f"rs_add1_{n}") for k, n in enumerate(names1)]
    hr1, tok = _ici_start("scatter", part1, [part_shapes[n] for n in names1], "rs_start1")

    def early_ffn_swap(grads, lw):
        g = shard_major(grads)
        pending["0ffn"], token = _d2d_start([g[n] for n in g], [part_shapes[n] for n in g], _swap_plan(len(g)), "rs_pair_start0ffn")
        behind(None, "norm_ffn")(lw, token)

    lw0["after"] = tok
    carry, g_ffn0 = _bwd_layer_ffn(0, dx, lw0, sv0, early_ffn_swap)
    g0 = shard_major({n: g_ffn0[n] for n in ("w_branch", "w_out")})
    names_a = list(g0) + ["w_gate_up", "w_down"]
    swapped = _d2d_wait(pending.pop("0ffn"), g_ffn0["w_branch"], "rs_pair_wait0ffn")
    own_a = [g0[n] for n in g0] + swapped[:2]
    got_a = _rs_pair([g0[n] for n in g0], "rs_pair0a") + swapped[2:]
    part_a = [_add_pairs(own_a[k], got_a[k], place, f"rs_add0a_{n}") for k, n in enumerate(names_a)]
    _, recv1 = _ici_wait(hr1, part_a[0], "rs_wait1")
    hra, tok = _ici_start("scatter", part_a, [part_shapes[n] for n in names_a], "rs_start0a")

    lw0["mixer"] = [lw0["mixer"][0] + tok[0, 0]] + lw0["mixer"][1:]
    dx, g_mix0 = _bwd_layer_mix(0, carry, lw0, sv0, tables, early_swap("0in"))
    _, recv_a = _ici_wait(hra, dx, "rs_wait0a")

    small_red, nf_red, loss_red, tok = _all_reduce_small_grads([{**g_ffn0, **g_mix0}, {**g_ffn1, **g_mix1}], d_nfinal, loss)
    small_red["norm_final"] = nf_red
    loss_out = loss_red[0, 0]
    for n in ("cv_w", "sc_w"):
        small_red[n] = lax.dynamic_slice_in_dim(small_red[n], chip * 128, 128, axis=2)

    own_in, got_in = _d2d_wait(pending.pop("0in"), tok, "rs_pair_wait0in")
    names_b, part_b = ["w_in"], [_add_pairs(own_in, got_in, place, "rs_add0b_w_in")]
    hrb, tok = _ici_start("scatter", part_b, [part_shapes[n] for n in names_b], "rs_start0b")
    bufs = {n: _sum_chips(part1[k], recv1[k], place, 1, None, f"rs_sum1_{n}", tok) for k, n in enumerate(names1)}
    for k, n in enumerate(names_a):
        bufs[n] = _sum_chips(part_a[k], recv_a[k], place, 0, bufs[n], f"rs_sum0_{n}", tok)
    shared = dict(zip(names_a, _rs_share([bufs[n] for n in names_a], "rs_share_a")))
    upd = {}
    for n in names_a:
        red = shared[n].reshape(W[n].shape)
        upd[n] = [red] + _adamw(W[n], red, M[n], V[n], f"adamw_{n}")
    two_d = lambda t: t[None] if t.ndim == 1 else t
    small_upd = _adamw_small(*([two_d(t[n]) for n in SMALL] for t in (W, small_red, M, V)), "adamw_small")
    for n, d, mo, vo in zip(SMALL, *small_upd):
        upd[n] = [t.reshape(W[n].shape) for t in (small_red[n], d, mo, vo)]

    _, recv_b = _ici_wait(hrb, upd[names_a[-1]][1], "rs_wait0b")
    for k, n in enumerate(names_b):
        bufs[n] = _sum_chips(part_b[k], recv_b[k], place, 0, bufs[n], f"rs_sum0_{n}", tok)
    shared = dict(zip(names_b, _rs_share([bufs[n] for n in names_b], "rs_share_b")))
    for n in names_b:
        red = shared[n].reshape(Wt[n].shape)
        upd[n] = [T_(t) for t in [red] + _adamw(Wt[n], red, Mt[n], Vt[n], f"adamw_{n}")]

    out = [loss_out, dx[None]]
    for k in range(4):
        out += [upd[n][k] for n in ORDER]
    return tuple(out)
```

```python
import functools
import math

import jax
import jax.numpy as jnp
from jax import lax
from jax.experimental import pallas as pl
from jax.experimental.pallas import tpu as pltpu

F32 = jnp.float32
BF16 = jnp.bfloat16

D_MODEL = 1024
DEPTH = 2
HALF = 512
SG_CHUNK = 128
SG_GROUPS = 4
CV_KERNEL = 31
HEAD_DIM = 64
N_Q_HEADS = 8
N_KV_HEADS = 2
Q_PER_KV = N_Q_HEADS // N_KV_HEADS
WINDOW = 128
ROPE_THETA = 10000.0
SC_KERNEL = 3
N_BRANCH = 4
D_FF = 2816
EPS = 1e-6
N_CHIPS = 4
N_DEV = 8

MIX_W = 4352
GATE_W = N_BRANCH * D_MODEL
PROJ_PAD = 2 * MIX_W
W_IN_SHARD = 2112
GU_SHARD = 1408
HALO = 128
CV_PAD = 32

ADAM_LR = 0.001
ADAM_B1 = 0.9
ADAM_B2 = 0.999
ADAM_EPS = 1e-08
ADAM_WD = 0.01
ADAM_STEP = 10

VMEM_LIMIT = 56 * 1024 * 1024
INV_SQRT2 = 1.0 / math.sqrt(2.0)
INV_SQRT_2PI = 1.0 / math.sqrt(2.0 * math.pi)
NEG_BIG = -1e30
MESH = pl.DeviceIdType.MESH

C_ZA, C_ZB, C_Q, C_K, C_V, C_ZD = 0, 1024, 2048, 2560, 2688, 2816


def _params(*sem):
    return pltpu.CompilerParams(dimension_semantics=sem, vmem_limit_bytes=VMEM_LIMIT)


def _sig(v):
    return 1.0 / (1.0 + jnp.exp(-v))


def _dot(a, b):
    return jnp.dot(a, b, preferred_element_type=F32)


def _dot_nt(a, b):
    return lax.dot_general(a, b, (((1,), (1,)), ((), ())), preferred_element_type=F32)


def _dot_tn(a, b):
    return lax.dot_general(a, b, (((0,), (0,)), ((), ())), preferred_element_type=F32)


def _full(shape):
    nd = len(shape)
    return pl.BlockSpec(shape, lambda *_: (0,) * nd)


def _rms_mm(x, g, w, tm, tn, name):
    T = x.shape[0]
    transposed = w.ndim == 2
    if transposed:
        N = w.shape[0]
        wspec = pl.BlockSpec((tn, D_MODEL), lambda i, j: (j, 0))
    else:
        tn = w.shape[2]
        N = w.shape[0] * tn
        wspec = pl.BlockSpec((None, D_MODEL, tn), lambda i, j: (j, 0, 0))

    def body(x_ref, g_ref, w_ref, o_ref, xn_ref):
        @pl.when(pl.program_id(1) == 0)
        def _():
            xv = x_ref[...]
            r = lax.rsqrt(jnp.mean(xv * xv, axis=-1, keepdims=True) + EPS)
            xn_ref[...] = (xv * r * g_ref[...]).astype(BF16)

        o_ref[...] = (_dot_nt if transposed else _dot)(xn_ref[...], w_ref[...]).astype(BF16)

    return pl.pallas_call(
        body, name=name, grid=(T // tm, N // tn),
        in_specs=[pl.BlockSpec((tm, D_MODEL), lambda i, j: (i, 0)), _full((1, D_MODEL)), wspec],
        out_specs=[pl.BlockSpec((tm, tn), lambda i, j: (i, j)), pl.BlockSpec((tm, D_MODEL), lambda i, j: (i, 0))],
        out_shape=[jax.ShapeDtypeStruct((T, N), BF16), jax.ShapeDtypeStruct((T, D_MODEL), BF16)],
        compiler_params=_params("parallel", "arbitrary"),
    )(x, g, w)


def _merge_fwd(x, ys, proj, wb, wo, tm, name):
    T = x.shape[0]

    def body(x_ref, ys_ref, zg_ref, wb_ref, wo_ref, xo_ref, mg_ref):
        merged = None
        for n in range(N_BRANCH):
            yn = ys_ref[:, n * HALF:(n + 1) * HALF]
            br = jnp.concatenate([_dot(yn, wb_ref[s, n]) for s in range(N_CHIPS)], axis=1)
            t = _sig(zg_ref[:, n * D_MODEL:(n + 1) * D_MODEL].astype(F32)) * br
            merged = t if merged is None else merged + t
        mb = merged.astype(BF16)
        mg_ref[...] = mb
        xo_ref[...] = x_ref[...] + _dot(mb, wo_ref[...])

    return pl.pallas_call(
        body, name=name, grid=(T // tm,),
        in_specs=[pl.BlockSpec((tm, D_MODEL), lambda i: (i, 0)), pl.BlockSpec((tm, N_BRANCH * HALF), lambda i: (i, 0)),
                  pl.BlockSpec((tm, GATE_W), lambda i: (i, 0)), _full(wb.shape), _full(wo.shape)],
        out_specs=[pl.BlockSpec((tm, D_MODEL), lambda i: (i, 0)), pl.BlockSpec((tm, D_MODEL), lambda i: (i, 0))],
        out_shape=[jax.ShapeDtypeStruct((T, D_MODEL), F32), jax.ShapeDtypeStruct((T, D_MODEL), BF16)],
        compiler_params=_params("parallel"),
    )(x, ys, proj, wb, wo)


def _ffn_down(xm, gu, wd, tm, name):
    T = xm.shape[0]

    def body(x_ref, gu_ref, wd_ref, o_ref):
        g = gu_ref[:, :D_FF].astype(F32)
        u = gu_ref[:, D_FF:].astype(F32)
        act = (g * _sig(g) * u).astype(BF16)
        o_ref[...] = x_ref[...] + _dot(act, wd_ref[...])

    return pl.pallas_call(
        body, name=name, grid=(T // tm,),
        in_specs=[pl.BlockSpec((tm, D_MODEL), lambda i: (i, 0)), pl.BlockSpec((tm, 2 * D_FF), lambda i: (i, 0)), _full(wd.shape)],
        out_specs=pl.BlockSpec((tm, D_MODEL), lambda i: (i, 0)),
        out_shape=jax.ShapeDtypeStruct((T, D_MODEL), F32),
        compiler_params=_params("parallel"),
    )(xm, gu, wd)


def _final_loss(x, g, tgt, tm, name):
    T = x.shape[0]

    def body(x_ref, g_ref, t_ref, dx_ref, dg_ref, ls_ref):
        @pl.when(pl.program_id(0) == 0)
        def _():
            dg_ref[...] = jnp.zeros_like(dg_ref)
            ls_ref[...] = jnp.zeros_like(ls_ref)

        xv = x_ref[...]
        gv = g_ref[...]
        r = lax.rsqrt(jnp.mean(xv * xv, axis=-1, keepdims=True) + EPS)
        xh = xv * r
        diff = xh * gv - t_ref[...]
        ls_ref[...] += jnp.full(ls_ref.shape, 0.5 / D_MODEL, F32) * jnp.sum(diff * diff)
        dy = diff * (1.0 / D_MODEL)
        dxh = dy * gv
        dx_ref[...] = r * (dxh - xh * jnp.mean(dxh * xh, axis=-1, keepdims=True))
        dg_ref[...] += jnp.sum(dy * xh, axis=0, keepdims=True)

    return pl.pallas_call(
        body, name=name, grid=(T // tm,),
        in_specs=[pl.BlockSpec((tm, D_MODEL), lambda i: (i, 0)), _full((1, D_MODEL)), pl.BlockSpec((tm, D_MODEL), lambda i: (i, 0))],
        out_specs=[pl.BlockSpec((tm, D_MODEL), lambda i: (i, 0)), _full((1, D_MODEL)), _full((1, 128))],
        out_shape=[jax.ShapeDtypeStruct((T, D_MODEL), F32), jax.ShapeDtypeStruct((1, D_MODEL), F32), jax.ShapeDtypeStruct((1, 128), F32)],
        compiler_params=_params("arbitrary"),
    )(x, g, tgt)


def _swiglu_bwd(dx, gu, wd, tm, name, after):
    T = dx.shape[0]

    def body(dx_ref, gu_ref, wd_ref, after_ref, dgu_ref, act_ref):
        del after_ref
        dact = _dot_nt(dx_ref[...].astype(BF16), wd_ref[...])
        g = gu_ref[:, :D_FF].astype(F32)
        u = gu_ref[:, D_FF:].astype(F32)
        s = _sig(g)
        silu = g * s
        act_ref[...] = (silu * u).astype(BF16)
        dgu_ref[:, :D_FF] = (dact * u * (s + silu * (1.0 - s))).astype(BF16)
        dgu_ref[:, D_FF:] = (dact * silu).astype(BF16)

    return pl.pallas_call(
        body, name=name, grid=(T // tm,),
        in_specs=[pl.BlockSpec((tm, D_MODEL), lambda i: (i, 0)), pl.BlockSpec((tm, 2 * D_FF), lambda i: (i, 0)), _full(wd.shape),
                  pl.BlockSpec(memory_space=pl.ANY)],
        out_specs=[pl.BlockSpec((tm, 2 * D_FF), lambda i: (i, 0)), pl.BlockSpec((tm, D_FF), lambda i: (i, 0))],
        out_shape=[jax.ShapeDtypeStruct((T, 2 * D_FF), BF16), jax.ShapeDtypeStruct((T, D_FF), BF16)],
        compiler_params=_params("parallel"),
    )(dx, gu, wd, after)


def _mm_tn(a, b, grid, a_block, a_map, b_block, b_map, o_shape, o_block, o_map, name, col_split=1):
    gk = grid[2]
    tm = [d for d in a_block if d is not None][-1]
    tn = [d for d in b_block if d is not None][-1]

    def body(a_ref, b_ref, o_ref, acc_ref):
        k = pl.program_id(2)
        p = _dot_tn(a_ref[...].astype(BF16), b_ref[...].astype(BF16))

        @pl.when(k == 0)
        def _():
            acc_ref[...] = p

        @pl.when(k > 0)
        def _():
            acc_ref[...] += p

        @pl.when(k == gk - 1)
        def _():
            if col_split == 1:
                o_ref[...] = acc_ref[...].astype(o_ref.dtype)
            else:
                w = tn // col_split
                for s in range(col_split):
                    o_ref[s] = acc_ref[:, s * w:(s + 1) * w].astype(o_ref.dtype)

    return pl.pallas_call(
        body, name=name, grid=grid,
        in_specs=[pl.BlockSpec(a_block, a_map), pl.BlockSpec(b_block, b_map)],
        out_specs=pl.BlockSpec(o_block, o_map),
        out_shape=jax.ShapeDtypeStruct(o_shape, BF16),
        scratch_shapes=[pltpu.VMEM((tm, tn), F32)],
        compiler_params=_params("parallel", "parallel", "arbitrary"),
    )(a, b)


def _mm_nt_rmsbwd(a, w, x, g, dres, tm, tk, name):
    T = x.shape[0]
    transposed = w.ndim == 2
    if transposed:
        gk = w.shape[0] // tk
        wspec = pl.BlockSpec((tk, D_MODEL), lambda i, k: (k, 0))
    else:
        tk = w.shape[2]
        gk = w.shape[0]
        wspec = pl.BlockSpec((None, D_MODEL, tk), lambda i, k: (k, 0, 0))

    def body(a_ref, w_ref, x_ref, g_ref, r_ref, dx_ref, dg_ref, acc_ref):
        i, k = pl.program_id(0), pl.program_id(1)
        p = (_dot if transposed else _dot_nt)(a_ref[...], w_ref[...])

        @pl.when(k == 0)
        def _():
            acc_ref[...] = p

        @pl.when(k > 0)
        def _():
            acc_ref[...] += p

        @pl.when(jnp.logical_and(i == 0, k == 0))
        def _():
            dg_ref[...] = jnp.zeros_like(dg_ref)

        @pl.when(k == gk - 1)
        def _():
            dh = acc_ref[...]
            xv = x_ref[...]
            r = lax.rsqrt(jnp.mean(xv * xv, axis=-1, keepdims=True) + EPS)
            xh = xv * r
            dxh = dh * g_ref[...]
            dx_ref[...] = r_ref[...] + r * (dxh - xh * jnp.mean(dxh * xh, axis=-1, keepdims=True))
            dg_ref[...] += jnp.sum(dh * xh, axis=0, keepdims=True)

    return pl.pallas_call(
        body, name=name, grid=(T // tm, gk),
        in_specs=[pl.BlockSpec((tm, tk), lambda i, k: (i, k)), wspec, pl.BlockSpec((tm, D_MODEL), lambda i, k: (i, 0)),
                  _full((1, D_MODEL)), pl.BlockSpec((tm, D_MODEL), lambda i, k: (i, 0))],
        out_specs=[pl.BlockSpec((tm, D_MODEL), lambda i, k: (i, 0)), _full((1, D_MODEL))],
        out_shape=[jax.ShapeDtypeStruct((T, D_MODEL), F32), jax.ShapeDtypeStruct((1, D_MODEL), F32)],
        scratch_shapes=[pltpu.VMEM((tm, D_MODEL), F32)],
        compiler_params=_params("arbitrary", "arbitrary"),
    )(a, w, x, g, dres)


def _merge_bwd(dxm, ys, proj, wb, wo, tm, name):
    T = dxm.shape[0]

    def body(dx_ref, ys_ref, zg_ref, wb_ref, wo_ref, dys_ref, dbr_ref, dp_ref):
        dmerged = _dot_nt(dx_ref[...].astype(BF16), wo_ref[...])
        for n in range(N_BRANCH):
            yn = ys_ref[:, n * HALF:(n + 1) * HALF]
            br = jnp.concatenate([_dot(yn, wb_ref[s, n]) for s in range(N_CHIPS)], axis=1)
            gt = _sig(zg_ref[:, n * D_MODEL:(n + 1) * D_MODEL].astype(F32))
            dbr = (gt * dmerged).astype(BF16)
            dbr_ref[:, n * D_MODEL:(n + 1) * D_MODEL] = dbr
            dp_ref[:, n * D_MODEL:(n + 1) * D_MODEL] = (dmerged * br * gt * (1.0 - gt)).astype(BF16)
            dy = None
            for s in range(N_CHIPS):
                t = _dot_nt(dbr[:, s * 256:(s + 1) * 256], wb_ref[s, n])
                dy = t if dy is None else dy + t
            dys_ref[:, n * HALF:(n + 1) * HALF] = dy.astype(BF16)
        dp_ref[:, GATE_W:] = jnp.zeros((tm, MIX_W - GATE_W), BF16)

    return pl.pallas_call(
        body, name=name, grid=(T // tm,),
        in_specs=[pl.BlockSpec((tm, D_MODEL), lambda i: (i, 0)), pl.BlockSpec((tm, N_BRANCH * HALF), lambda i: (i, 0)),
                  pl.BlockSpec((tm, GATE_W), lambda i: (i, 0)), _full(wb.shape), _full(wo.shape)],
        out_specs=[pl.BlockSpec((tm, N_BRANCH * HALF), lambda i: (i, 0)), pl.BlockSpec((tm, GATE_W), lambda i: (i, 0)),
                   pl.BlockSpec((tm, MIX_W), lambda i: (i, 0))],
        out_shape=[jax.ShapeDtypeStruct((T, N_BRANCH * HALF), BF16), jax.ShapeDtypeStruct((T, GATE_W), BF16),
                   jax.ShapeDtypeStruct((T, PROJ_PAD), BF16)],
        compiler_params=_params("parallel"),
    )(dxm, ys, proj, wb, wo)


def _gelu(v):
    return 0.5 * v * (1.0 + lax.erf(v * INV_SQRT2))


def _gelu_grad(v):
    return 0.5 * (1.0 + lax.erf(v * INV_SQRT2)) + v * jnp.exp(-0.5 * v * v) * INV_SQRT_2PI


def _rot_half(t):
    w = t.shape[1]
    lane = lax.broadcasted_iota(jnp.int32, t.shape, 1)
    return jnp.where((lane % HEAD_DIM) < HEAD_DIM // 2, pltpu.roll(t, w - HEAD_DIM // 2, 1), pltpu.roll(t, HEAD_DIM // 2, 1))


def _rope(t, cos, sin_signed):
    return t * cos + _rot_half(t) * sin_signed


def _rope_t(d, cos, sin_signed):
    return d * cos + _rot_half(d * sin_signed)


def _ln_fwd(v, g, b):
    mu = jnp.mean(v, axis=-1, keepdims=True)
    vc = v - mu
    r = lax.rsqrt(jnp.mean(vc * vc, axis=-1, keepdims=True) + EPS)
    vh = vc * r
    return vh * g + b, vh, r


def _ln_bwd(dn, vh, r, g):
    dvh = dn * g
    return r * (dvh - jnp.mean(dvh, axis=-1, keepdims=True) - vh * jnp.mean(dvh * vh, axis=-1, keepdims=True))


def _sublane_shifts(sh_ref, rows):
    for b in range(1, 8):
        sh_ref[b, 0:rows - 8, :] = sh_ref[0, pl.ds(b, rows - 8), :]


def _tap(sh_ref, off, n):
    return sh_ref[off % 8, pl.ds(off - off % 8, n), :]


def _tril_mask():
    return lax.broadcasted_iota(jnp.int32, (SG_CHUNK, SG_CHUNK), 0) >= lax.broadcasted_iota(jnp.int32, (SG_CHUNK, SG_CHUNK), 1)


def _band_masks():
    shape = (Q_PER_KV * WINDOW, 2 * WINDOW)
    row = lax.broadcasted_iota(jnp.int32, shape, 0) % WINDOW
    col = lax.broadcasted_iota(jnp.int32, shape, 1)
    band = (col > row) & (col <= row + WINDOW)
    return band, band & (col >= WINDOW)


def _attn_probs(qs, kh, sink_col, valid):
    s = jnp.where(valid, _dot_nt(qs, kh) * (HEAD_DIM ** -0.5), NEG_BIG)
    m = jnp.maximum(jnp.max(s, axis=-1, keepdims=True), sink_col)
    p = jnp.exp(s - m)
    es = jnp.exp(sink_col - m)
    inv = 1.0 / (jnp.sum(p, axis=-1, keepdims=True) + es)
    return p * inv, es * inv


def _sink_col(sinks_ref, h):
    return jnp.concatenate([jnp.broadcast_to(sinks_ref[:, h * Q_PER_KV + g:h * Q_PER_KV + g + 1], (WINDOW, 1))
                            for g in range(Q_PER_KV)], axis=0)


def _mixer_in_specs(TB, nb):
    r = TB // HALO
    last = nb * r - 1
    cur = pl.BlockSpec((TB, MIX_W), lambda i: (i, 1))
    prev = pl.BlockSpec((HALO, MIX_W), lambda i: (jnp.maximum(i * r - 1, 0), 1))
    nxt = pl.BlockSpec((HALO, MIX_W), lambda i: (jnp.minimum((i + 1) * r, last), 1))
    tcur = pl.BlockSpec((TB, 128), lambda i: (i, 0))
    tprev = pl.BlockSpec((HALO, 128), lambda i: (jnp.maximum(i * r - 1, 0), 0))
    tnxt = pl.BlockSpec((HALO, 128), lambda i: (jnp.minimum((i + 1) * r, last), 0))
    return cur, prev, nxt, tcur, tprev, tnxt


def _mixer_param_specs():
    return [_full((1, HALF)), _full((1, HALF)), _full((SG_GROUPS, SG_CHUNK, SG_CHUNK)), _full((SG_CHUNK, 128)),
            _full((32, HALF)), _full((1, HALF)), _full((1, HALF)), _full((1, HALF)), _full((1, 128)), _full((8, HALF))]


def _mixers_fwd(proj, cos_t, sin_t, mp, TB, name):
    T = proj.shape[0]
    nb = T // TB
    r = TB // HALO
    cur, prev, _, tcur, tprev, _ = _mixer_in_specs(TB, nb)

    def body(zc_ref, zp_ref, cc_ref, sc_ref, cp_ref, sp_ref,
             lg_ref, lb_ref, sgw_ref, sgb_ref, cvw_ref, cvb_ref, cvg_ref, cvbb_ref, sinks_ref, scw_ref,
             ys_ref, scr_ref, k_ref, v_ref, sh_ref):
        i = pl.program_id(0)
        pm = (i > 0).astype(F32)

        def colsE(c0, c1):
            return jnp.concatenate([zp_ref[:, c0:c1].astype(F32) * pm, zc_ref[:, c0:c1].astype(F32)], axis=0)

        a = _gelu(zc_ref[:, C_ZA:C_ZA + 2 * HALF].astype(F32))
        u = a[:, :HALF]
        vn, _, _ = _ln_fwd(a[:, HALF:], lg_ref[...], lb_ref[...])
        vnb = vn.astype(BF16)
        tril = _tril_mask()
        chunks = [slice(ci * SG_CHUNK, (ci + 1) * SG_CHUNK) for ci in range(r)]
        for g in range(SG_GROUPS):
            cols = slice(g * 128, (g + 1) * 128)
            wt = jnp.where(tril, sgw_ref[g], 0.0).astype(BF16)
            mixed = _dot(wt, jnp.concatenate([vnb[rows, cols] for rows in chunks], axis=1)) + sgb_ref[:, g:g + 1]
            for ci, rows in enumerate(chunks):
                ys_ref[rows, cols] = (u[rows, cols] * mixed[:, ci * 128:(ci + 1) * 128]).astype(BF16)

        def colsB(c0, c1):
            return jnp.concatenate([zp_ref[HALO - CV_PAD:, c0:c1].astype(F32) * pm, zc_ref[:, c0:c1].astype(F32)], axis=0)

        sh_ref[0] = colsB(C_ZB, C_ZB + HALF) * _sig(colsB(C_ZB + HALF, C_ZB + 2 * HALF))
        _sublane_shifts(sh_ref, TB + CV_PAD)
        c = jnp.broadcast_to(cvb_ref[...], (TB, HALF))
        for k in range(CV_KERNEL):
            c = c + cvw_ref[k:k + 1, :] * _tap(sh_ref, CV_PAD - (CV_KERNEL - 1) + k, TB)
        n, _, _ = _ln_fwd(c, cvg_ref[...], cvbb_ref[...])
        ys_ref[:, HALF:2 * HALF] = (n * _sig(n)).astype(BF16)

        zd = colsE(C_ZD + HALF, C_ZD + 3 * HALF)
        scr_ref[...] = zd[:, :HALF] * zd[:, HALF:]
        cv = None
        for k in range(SC_KERNEL):
            t = scw_ref[k:k + 1, :] * scr_ref[pl.ds(HALO - (SC_KERNEL - 1) + k, TB), :]
            cv = t if cv is None else cv + t
        ys_ref[:, 3 * HALF:4 * HALF] = (zc_ref[:, C_ZD:C_ZD + HALF].astype(F32) * cv).astype(BF16)

        cosE = jnp.concatenate([cp_ref[...], cc_ref[...]], axis=0)
        sinE = jnp.concatenate([sp_ref[...], sc_ref[...]], axis=0)
        k_ref[...] = _rope(colsE(C_K, C_K + 128), cosE, sinE).astype(BF16)
        v_ref[...] = colsE(C_V, C_V + 128).astype(BF16)
        cosC, sinC = cc_ref[...], sc_ref[...]
        q = jnp.concatenate([_rope(zc_ref[:, C_Q + 128 * j:C_Q + 128 * (j + 1)].astype(F32), cosC, sinC)
                             for j in range(4)], axis=1).astype(BF16)
        in_band, in_band_cur = _band_masks()
        sink_cols = [_sink_col(sinks_ref, h) for h in range(N_KV_HEADS)]
        for qb in range(r):
            valid = in_band if qb else in_band_cur | (in_band & (i > 0))
            for h in range(N_KV_HEADS):
                hc = slice(h * HEAD_DIM, (h + 1) * HEAD_DIM)
                kh = k_ref[qb * WINDOW:qb * WINDOW + 2 * WINDOW, hc]
                vh = v_ref[qb * WINDOW:qb * WINDOW + 2 * WINDOW, hc]
                qs = jnp.concatenate([q[qb * WINDOW:(qb + 1) * WINDOW, (h * Q_PER_KV + g) * HEAD_DIM:(h * Q_PER_KV + g + 1) * HEAD_DIM]
                                      for g in range(Q_PER_KV)], axis=0)
                probs, _ = _attn_probs(qs, kh, sink_cols[h], valid)
                o = _dot(probs.astype(BF16), vh)
                for g in range(Q_PER_KV):
                    c0 = 2 * HALF + (h * Q_PER_KV + g) * HEAD_DIM
                    ys_ref[qb * WINDOW:(qb + 1) * WINDOW, c0:c0 + HEAD_DIM] = o[g * WINDOW:(g + 1) * WINDOW].astype(BF16)

    return pl.pallas_call(
        body, name=name, grid=(nb,),
        in_specs=[cur, prev, tcur, tcur, tprev, tprev] + _mixer_param_specs(),
        out_specs=pl.BlockSpec((TB, 4 * HALF), lambda i: (i, 0)),
        out_shape=jax.ShapeDtypeStruct((T, 4 * HALF), BF16),
        scratch_shapes=[pltpu.VMEM((TB + HALO, HALF), F32), pltpu.VMEM((TB + HALO, 128), BF16), pltpu.VMEM((TB + HALO, 128), BF16),
                        pltpu.VMEM((8, TB + CV_PAD, HALF), F32)],
        compiler_params=_params("parallel"),
    )(proj, proj, cos_t, sin_t, cos_t, sin_t, *mp)


def _mixers_bwd(proj, dys, dproj, cos_t, sin_t, mp, TB, name):
    T = proj.shape[0]
    nb = T // TB
    r = TB // HALO
    RE = TB + 2 * HALO
    RC = TB + HALO
    cur, prev, nxt, tcur, tprev, tnxt = _mixer_in_specs(TB, nb)
    dcur = pl.BlockSpec((TB, 4 * HALF), lambda i: (i, 0))
    dnxt = pl.BlockSpec((HALO, 4 * HALF), lambda i: (jnp.minimum((i + 1) * r, nb * r - 1), 0))

    def body(zc_ref, zp_ref, zn_ref, dyc_ref, dyn_ref, cc_ref, sc_ref, cp_ref, sp_ref, cn_ref, sn_ref,
             lg_ref, lb_ref, sgw_ref, sgb_ref, cvw_ref, cvb_ref, cvg_ref, cvbb_ref, sinks_ref, scw_ref, dp_in_ref,
             dz_ref, dlg_ref, dlb_ref, dsgw_ref, dsgb_ref, dcvw_ref, dcvb_ref, dcvg_ref, dcvbb_ref, dsink_ref, dscw_ref,
             scr_ref, scr2_ref, k_ref, v_ref, dk_ref, dv_ref, dq_ref, sh_ref, sh2_ref):
        del dp_in_ref
        i = pl.program_id(0)
        pm = (i > 0).astype(F32)
        nm = (i < nb - 1).astype(F32)

        @pl.when(i == 0)
        def _():
            for ref in (dlg_ref, dlb_ref, dsgw_ref, dsgb_ref, dcvw_ref, dcvb_ref, dcvg_ref, dcvbb_ref, dsink_ref, dscw_ref):
                ref[...] = jnp.zeros_like(ref)

        def colsE(c0, c1):
            return jnp.concatenate([zp_ref[:, c0:c1].astype(F32) * pm, zc_ref[:, c0:c1].astype(F32),
                                    zn_ref[:, c0:c1].astype(F32)], axis=0)

        def colsC(c0, c1):
            return jnp.concatenate([zc_ref[:, c0:c1].astype(F32), zn_ref[:, c0:c1].astype(F32)], axis=0)

        def dyC(c0, c1):
            return jnp.concatenate([dyc_ref[:, c0:c1].astype(F32), dyn_ref[:, c0:c1].astype(F32) * nm], axis=0)

        za = zc_ref[:, C_ZA:C_ZA + 2 * HALF].astype(F32)
        a = _gelu(za)
        u = a[:, :HALF]
        lg = lg_ref[...]
        vn, vh, rs = _ln_fwd(a[:, HALF:], lg, lb_ref[...])
        vnb = vn.astype(BF16)
        dya = dyc_ref[:, 0:HALF].astype(F32)
        tril = _tril_mask()
        lane128 = lax.broadcasted_iota(jnp.int32, (SG_CHUNK, 128), 1)
        chunks = [slice(ci * SG_CHUNK, (ci + 1) * SG_CHUNK) for ci in range(r)]
        side = lambda t, cols: jnp.concatenate([t[rows, cols] for rows in chunks], axis=1)
        for g in range(SG_GROUPS):
            cols = slice(g * 128, (g + 1) * 128)
            wt = jnp.where(tril, sgw_ref[g], 0.0).astype(BF16)
            vb = side(vnb, cols)
            dy_blk = side(dya, cols)
            du_g = dy_blk * (_dot(wt, vb) + sgb_ref[:, g:g + 1])
            dmix = dy_blk * side(u, cols)
            dmb = dmix.astype(BF16)
            dvn_g = _dot_tn(wt, dmb)
            dsgw_ref[g] += jnp.where(tril, _dot_nt(dmb, vb), 0.0)
            dsgb_ref[...] += jnp.where(lane128 == g, jnp.sum(dmix, axis=1, keepdims=True), 0.0)
            for ci, rows in enumerate(chunks):
                scr_ref[rows, cols] = du_g[:, ci * 128:(ci + 1) * 128]
                scr2_ref[rows, cols] = dvn_g[:, ci * 128:(ci + 1) * 128]
        du, dvn = scr_ref[0:TB, :], scr2_ref[0:TB, :]
        dlg_ref[...] += jnp.sum(dvn * vh, axis=0, keepdims=True)
        dlb_ref[...] += jnp.sum(dvn, axis=0, keepdims=True)
        dvv = _ln_bwd(dvn, vh, rs, lg)
        gg = _gelu_grad(za)
        dz_ref[:, C_ZA:C_ZA + HALF] = (du * gg[:, :HALF]).astype(BF16)
        dz_ref[:, C_ZA + HALF:C_ZA + 2 * HALF] = (dvv * gg[:, HALF:]).astype(BF16)

        RB = TB + CV_PAD

        def colsB(c0, c1):
            return jnp.concatenate([zp_ref[HALO - CV_PAD:, c0:c1].astype(F32) * pm, zc_ref[:, c0:c1].astype(F32),
                                    zn_ref[:CV_PAD, c0:c1].astype(F32)], axis=0)

        sh_ref[0] = colsB(C_ZB, C_ZB + HALF) * _sig(colsB(C_ZB + HALF, C_ZB + 2 * HALF))
        _sublane_shifts(sh_ref, RB + CV_PAD)
        c = jnp.broadcast_to(cvb_ref[...], (RB, HALF))
        for k in range(CV_KERNEL):
            c = c + cvw_ref[k:k + 1, :] * _tap(sh_ref, CV_PAD - (CV_KERNEL - 1) + k, RB)
        cvg = cvg_ref[...]
        n, ch, rc = _ln_fwd(c, cvg, cvbb_ref[...])
        sn = _sig(n)
        dyb = jnp.concatenate([dyc_ref[:, HALF:2 * HALF].astype(F32), dyn_ref[:CV_PAD, HALF:2 * HALF].astype(F32) * nm], axis=0)
        dn = dyb * (sn + n * sn * (1.0 - sn))
        dno = dn[:TB]
        dcvg_ref[...] += jnp.sum(dno * ch[:TB], axis=0, keepdims=True)
        dcvbb_ref[...] += jnp.sum(dno, axis=0, keepdims=True)
        dc = _ln_bwd(dn, ch, rc, cvg)
        sh2_ref[0] = dc
        _sublane_shifts(sh2_ref, RB)
        dcvb_ref[...] += jnp.sum(dc[:TB], axis=0, keepdims=True)
        dy0 = None
        for k in range(CV_KERNEL):
            wk = cvw_ref[k:k + 1, :]
            t = wk * _tap(sh2_ref, CV_KERNEL - 1 - k, TB)
            dy0 = t if dy0 is None else dy0 + t
            dcvw_ref[k:k + 1, :] += jnp.sum(dc[:TB] * _tap(sh_ref, CV_PAD - (CV_KERNEL - 1) + k, TB), axis=0, keepdims=True)
        ab = zc_ref[:, C_ZB:C_ZB + HALF].astype(F32)
        sg = _sig(zc_ref[:, C_ZB + HALF:C_ZB + 2 * HALF].astype(F32))
        dz_ref[:, C_ZB:C_ZB + HALF] = (dy0 * sg).astype(BF16)
        dz_ref[:, C_ZB + HALF:C_ZB + 2 * HALF] = (dy0 * ab * sg * (1.0 - sg)).astype(BF16)

        zd = colsE(C_ZD + HALF, C_ZD + 3 * HALF)
        scr_ref[...] = zd[:, :HALF] * zd[:, HALF:]
        dcv = dyC(3 * HALF, 4 * HALF) * colsC(C_ZD, C_ZD + HALF)
        scr2_ref[...] = dcv
        cv = None
        dud = None
        for k in range(SC_KERNEL):
            wk = scw_ref[k:k + 1, :]
            us = scr_ref[pl.ds(HALO - (SC_KERNEL - 1) + k, TB), :]
            t = wk * us
            cv = t if cv is None else cv + t
            t2 = wk * scr2_ref[pl.ds(SC_KERNEL - 1 - k, TB), :]
            dud = t2 if dud is None else dud + t2
            dscw_ref[k:k + 1, :] += jnp.sum(dcv[:TB] * us, axis=0, keepdims=True)
        dz_ref[:, C_ZD:C_ZD + HALF] = (dyc_ref[:, 3 * HALF:4 * HALF].astype(F32) * cv).astype(BF16)
        dz_ref[:, C_ZD + HALF:C_ZD + 2 * HALF] = (dud * zc_ref[:, C_ZD + 2 * HALF:C_ZD + 3 * HALF].astype(F32)).astype(BF16)
        dz_ref[:, C_ZD + 2 * HALF:C_ZD + 3 * HALF] = (dud * zc_ref[:, C_ZD + HALF:C_ZD + 2 * HALF].astype(F32)).astype(BF16)

        cosE = jnp.concatenate([cp_ref[...], cc_ref[...], cn_ref[...]], axis=0)
        sinE = jnp.concatenate([sp_ref[...], sc_ref[...], sn_ref[...]], axis=0)
        k_ref[...] = _rope(colsE(C_K, C_K + 128), cosE, sinE).astype(BF16)
        v_ref[...] = colsE(C_V, C_V + 128).astype(BF16)
        dk_ref[...] = jnp.zeros_like(dk_ref)
        dv_ref[...] = jnp.zeros_like(dv_ref)
        q = jnp.concatenate([_rope(colsC(C_Q + 128 * j, C_Q + 128 * (j + 1)), cosE[HALO:], sinE[HALO:])
                             for j in range(4)], axis=1).astype(BF16)
        dO = dyC(2 * HALF, 3 * HALF).astype(BF16)
        lane_s = lax.broadcasted_iota(jnp.int32, (1, 128), 1)
        in_band, in_band_cur = _band_masks()
        sink_cols = [_sink_col(sinks_ref, h) for h in range(N_KV_HEADS)]
        for qb in range(r + 1):
            valid = in_band if qb else in_band_cur | (in_band & (i > 0))
            rows = slice(qb * WINDOW, (qb + 1) * WINDOW)
            band = slice(qb * WINDOW, qb * WINDOW + 2 * WINDOW)
            for h in range(N_KV_HEADS):
                hc = slice(h * HEAD_DIM, (h + 1) * HEAD_DIM)
                kh = k_ref[band, hc]
                vh_ = v_ref[band, hc]
                heads = [slice((h * Q_PER_KV + g) * HEAD_DIM, (h * Q_PER_KV + g + 1) * HEAD_DIM) for g in range(Q_PER_KV)]
                qs = jnp.concatenate([q[rows, hs] for hs in heads], axis=0)
                dos = jnp.concatenate([dO[rows, hs] for hs in heads], axis=0)
                probs, p_sink = _attn_probs(qs, kh, sink_cols[h], valid)
                dP = _dot_nt(dos, vh_)
                rsum = jnp.sum(probs * dP, axis=-1, keepdims=True)
                dS = (probs * (dP - rsum) * (HEAD_DIM ** -0.5)).astype(BF16)
                dk_ref[band, hc] += _dot_tn(dS, qs)
                dv_ref[band, hc] += _dot_tn(probs.astype(BF16), dos)
                if qb < r:
                    dqs = _dot(dS, kh)
                    dsk = -p_sink * rsum
                    for g in range(Q_PER_KV):
                        dq_ref[rows, heads[g]] = dqs[g * WINDOW:(g + 1) * WINDOW]
                        dsink_ref[...] += jnp.where(lane_s == h * Q_PER_KV + g, jnp.sum(dsk[g * WINDOW:(g + 1) * WINDOW]), 0.0)
        cosC, sinC = cc_ref[...], sc_ref[...]
        for j in range(4):
            dz_ref[:, C_Q + 128 * j:C_Q + 128 * (j + 1)] = _rope_t(dq_ref[:, 128 * j:128 * (j + 1)], cosC, sinC).astype(BF16)
        dz_ref[:, C_K:C_K + 128] = _rope_t(dk_ref[HALO:HALO + TB, :], cosC, sinC).astype(BF16)
        dz_ref[:, C_V:C_V + 128] = dv_ref[HALO:HALO + TB, :].astype(BF16)

    small = [((1, HALF), F32), ((1, HALF), F32), ((SG_GROUPS, SG_CHUNK, SG_CHUNK), F32), ((SG_CHUNK, 128), F32),
             ((32, HALF), F32), ((1, HALF), F32), ((1, HALF), F32), ((1, HALF), F32), ((1, 128), F32), ((8, HALF), F32)]
    outs = pl.pallas_call(
        body, name=name, grid=(nb,),
        in_specs=[cur, prev, nxt, dcur, dnxt, tcur, tcur, tprev, tprev, tnxt, tnxt] + _mixer_param_specs()
                 + [pl.BlockSpec(memory_space=pl.ANY)],
        out_specs=[pl.BlockSpec((TB, MIX_W), lambda i: (i, 1))] + [_full(s) for s, _ in small],
        out_shape=[jax.ShapeDtypeStruct((T, PROJ_PAD), BF16)] + [jax.ShapeDtypeStruct(s, d) for s, d in small],
        scratch_shapes=[pltpu.VMEM((RE, HALF), F32), pltpu.VMEM((RC, HALF), F32), pltpu.VMEM((RE, 128), BF16), pltpu.VMEM((RE, 128), BF16),
                        pltpu.VMEM((RE, 128), F32), pltpu.VMEM((RE, 128), F32), pltpu.VMEM((TB, HALF), F32),
                        pltpu.VMEM((8, TB + 2 * CV_PAD, HALF), F32), pltpu.VMEM((8, TB + CV_PAD, HALF), F32)],
        input_output_aliases={21: 0},
        compiler_params=_params("arbitrary"),
    )(proj, proj, proj, dys, dys, cos_t, sin_t, cos_t, sin_t, cos_t, sin_t, *mp, dproj)
    return outs


def _rope_tables(T):
    pos = jnp.arange(T, dtype=F32)
    inv_freq = 1.0 / (ROPE_THETA ** (jnp.arange(0, HEAD_DIM, 2, dtype=F32) / HEAD_DIM))
    ang = pos[:, None] * inv_freq[None, :]
    cos, sin = jnp.cos(ang), jnp.sin(ang)
    cos_t = jnp.concatenate([cos, cos, cos, cos], axis=1)
    sin_t = jnp.concatenate([-sin, sin, -sin, sin], axis=1)
    return cos_t, sin_t


def _mixer_params(l, sg_ln_g, sg_ln_b, sg_w, sg_b, cv_w, cv_b, cv_ln_g, cv_ln_b, attn_sinks, sc_w):
    sgb_t = jnp.zeros((SG_CHUNK, 128), F32).at[:, :SG_GROUPS].set(sg_b[l].T)
    cvw = jnp.zeros((32, HALF), F32).at[:CV_KERNEL].set(cv_w[l])
    scw = jnp.zeros((8, HALF), F32).at[:SC_KERNEL].set(sc_w[l])
    sinks = jnp.zeros((1, 128), F32).at[0, :N_Q_HEADS].set(attn_sinks[l])
    return [sg_ln_g[l][None], sg_ln_b[l][None], sg_w[l], sgb_t, cvw, cv_b[l][None], cv_ln_g[l][None], cv_ln_b[l][None], sinks, scw]


def _w_in_layout(w_in_g):
    cut = MIX_W - 2 * W_IN_SHARD
    return jnp.concatenate([w_in_g[2][cut:], w_in_g[3], jnp.zeros((MIX_W - GATE_W, D_MODEL), w_in_g.dtype),
                            w_in_g[0], w_in_g[1], w_in_g[2][:cut]], axis=0)


def _w_in_unlayout(dw):
    cut = MIX_W - 2 * W_IN_SHARD
    return jnp.stack([dw[MIX_W:MIX_W + W_IN_SHARD], dw[MIX_W + W_IN_SHARD:MIX_W + 2 * W_IN_SHARD],
                      jnp.concatenate([dw[MIX_W + 2 * W_IN_SHARD:], dw[:W_IN_SHARD - cut]], axis=0),
                      dw[W_IN_SHARD - cut:GATE_W]], axis=0)


def _device_step(x, tgt, norm_mix, norm_ffn, norm_final, mixer_params, w_in_p, wb_g, wo_g, wgu_g, wd_g):
    T = x.shape[0]
    tables = _rope_tables(T)
    saved = []
    for l in range(DEPTH):
        lw = dict(w_in=w_in_p[l], w_branch=wb_g[l], w_out=wo_g[l], w_gate_up=wgu_g[l], w_down=wd_g[l],
                  norm_mix=norm_mix[l][None], norm_ffn=norm_ffn[l][None], mixer=mixer_params[l], after=jnp.zeros((8, 128), F32))
        x, sv = _fwd_layer(l, x, lw, tables)
        saved.append((lw, sv))
    dx, dnf, loss = _final_loss(x, norm_final[None], tgt, 256, "final_loss")
    grads = [None] * DEPTH
    for l in reversed(range(DEPTH)):
        lw, sv = saved[l]
        dxm, g_ffn = _bwd_layer_ffn(l, dx, lw, sv)
        dx, g_mix = _bwd_layer_mix(l, dxm, lw, sv, tables)
        raw = {**g_ffn, **g_mix}
        grads[l] = {**raw, **_small_views(raw)}
    return loss, dx, dnf[0], grads


MIX_BLOCK = 256


def _fwd_layer(l, x, lw, tables):
    return _fwd_layer_rest(l, x, _fwd_layer_mix(l, x, lw, tables), lw)


def _fwd_layer_mix(l, x, lw, tables, between=None):
    proj, xn = _rms_mm(x, lw["norm_mix"], lw["w_in"], min(x.shape[0], 1024), 2176, f"proj{l}")
    if between is not None:
        between(proj, lw)
    return proj, xn, _mixers_fwd(proj, *tables, lw["mixer"], MIX_BLOCK, f"mixers_fwd{l}")


def _fwd_layer_rest(l, x, mixed, lw, between=None):
    proj, xn, ys = mixed
    TM = min(x.shape[0], 1024)
    xm, merged = _merge_fwd(x, ys, proj, lw["w_branch"], lw["w_out"], min(x.shape[0], 512), f"merge_fwd{l}")
    if between is not None:
        between(xm, lw)
    gu, hn = _rms_mm(xm, lw["norm_ffn"], lw["w_gate_up"], TM, GU_SHARD, f"ffn_up{l}")
    x_out = _ffn_down(xm, gu, lw["w_down"], 256, f"ffn_down{l}")
    return x_out, (x, proj, xn, ys, xm, merged, gu, hn)


def _bwd_layer_ffn(l, dx, lw, sv, between=None):
    x_in, proj, xn, ys, xm, merged, gu, hn = sv
    T = dx.shape[0]
    tkk = min(T, 1024)
    gk = T // tkk
    dgu, act = _swiglu_bwd(dx, gu, lw["w_down"], 256, f"swiglu_bwd{l}", lw["after"])
    d_wd = _mm_tn(act, dx, (2, 1, gk), (tkk, D_FF // 2), lambda i, j, k: (k, i), (tkk, D_MODEL), lambda i, j, k: (k, 0),
                  (D_FF, D_MODEL), (D_FF // 2, D_MODEL), lambda i, j, k: (i, 0), f"dw_down{l}")
    d_wgu = _mm_tn(hn, dgu, (1, N_CHIPS, gk), (tkk, D_MODEL), lambda i, j, k: (k, 0), (tkk, GU_SHARD), lambda i, j, k: (k, j),
                   (N_CHIPS, D_MODEL, GU_SHARD), (None, D_MODEL, GU_SHARD), lambda i, j, k: (j, 0, 0), f"dw_gate_up{l}")
    if between is not None:
        between(dict(w_gate_up=d_wgu, w_down=d_wd), lw)
    dxm, d_nffn = _mm_nt_rmsbwd(dgu, lw["w_gate_up"], xm, lw["norm_ffn"], dx, min(T, 1024), GU_SHARD, f"ffn_up_bwd{l}")
    dys, dbr, dproj = _merge_bwd(dxm, ys, proj, lw["w_branch"], lw["w_out"], 256, f"merge_bwd{l}")
    d_wo = _mm_tn(merged, dxm, (2, 1, gk), (tkk, 512), lambda i, j, k: (k, i), (tkk, D_MODEL), lambda i, j, k: (k, 0),
                  (D_MODEL, D_MODEL), (512, D_MODEL), lambda i, j, k: (i, 0), f"dw_out{l}")
    d_wb = _mm_tn(ys, dbr, (N_BRANCH, 1, gk), (tkk, HALF), lambda i, j, k: (k, i), (tkk, D_MODEL), lambda i, j, k: (k, i),
                  (N_CHIPS, N_BRANCH, HALF, 256), (N_CHIPS, None, HALF, 256), lambda i, j, k: (0, i, 0, 0), f"dw_branch{l}", col_split=N_CHIPS)
    return (dxm, dys, dproj), dict(w_branch=d_wb, w_out=d_wo, w_gate_up=d_wgu, w_down=d_wd, norm_ffn=d_nffn)


def _bwd_layer_mix(l, carry, lw, sv, tables, between=None):
    dxm, dys, dproj = carry
    x_in, proj, xn, ys, xm, merged, gu, hn = sv
    T = dxm.shape[0]
    tkk = min(T, 1024)
    gk = T // tkk
    mb = _mixers_bwd(proj, dys, dproj, *tables, lw["mixer"], MIX_BLOCK, f"mixers_bwd{l}")
    dproj = mb[0]
    d_win = _mm_tn(dproj, xn, (PROJ_PAD // 2176, 1, gk), (tkk, 2176), lambda i, j, k: (k, i), (tkk, D_MODEL), lambda i, j, k: (k, 0),
                   (PROJ_PAD, D_MODEL), (2176, D_MODEL), lambda i, j, k: (i, 0), f"dw_in{l}")
    if between is not None:
        between(d_win, lw)
    dx, d_nmix = _mm_nt_rmsbwd(dproj, lw["w_in"], x_in, lw["norm_mix"], dxm, min(T, 1024), 2176, f"proj_bwd{l}")
    return dx, dict(w_in=d_win, norm_mix=d_nmix, sg_ln_g=mb[1], sg_ln_b=mb[2], sg_w=mb[3], sg_b=mb[4], cv_w=mb[5], cv_b=mb[6],
                    cv_ln_g=mb[7], cv_ln_b=mb[8], attn_sinks=mb[9], sc_w=mb[10])


ANY = pl.BlockSpec(memory_space=pl.ANY)
BIG = ("w_in", "w_branch", "w_out", "w_gate_up", "w_down")
HALF_SHAPE = {"w_in": (2, W_IN_SHARD // 2, D_MODEL), "w_branch": (2, 1024, 256), "w_out": (2, 128, D_MODEL),
              "w_gate_up": (2, 512, GU_SHARD), "w_down": (2, 352, D_MODEL)}
NB = len(BIG)


def _place():
    x, y, c = lax.axis_index("x"), lax.axis_index("y"), lax.axis_index("c")
    chips = [(1 - x, y), (x, 1 - y), (1 - x, 1 - y)]
    return x, y, c, 2 * x + y, chips, [2 * px + py for px, py in chips]


def _remote(src, dst, ssem, rsem, dev):
    return pltpu.make_async_remote_copy(src_ref=src, dst_ref=dst, send_sem=ssem, recv_sem=rsem, device_id=dev, device_id_type=MESH)


HBM_SPEC = pl.BlockSpec(memory_space=pltpu.HBM)
SEM_SPEC = pl.BlockSpec(memory_space=pltpu.SEMAPHORE)
DATAFLOW = pltpu.SideEffectType.DATAFLOW_SIDE_EFFECTING


def _ici_ends(kind, src, land, j, c, chip, chip_ids):
    if kind == "gather":
        return src.at[c], land.at[chip, c], land.at[chip_ids[j], c]
    return src.at[chip_ids[j]], land.at[chip], land.at[chip_ids[j]]


def _ici_start(kind, srcs, land_shapes, name):
    n = len(srcs)

    def body(*refs):
        src, land = refs[:n], refs[n:2 * n]
        ssem, rsem, token = refs[2 * n], refs[2 * n + 1], refs[-1]
        x, y, c, chip, chips, chip_ids = _place()
        for k in range(n):
            for j in range(3):
                s, d, _ = _ici_ends(kind, src[k], land[k], j, c, chip, chip_ids)
                _remote(s, d, ssem.at[3 * k + j], rsem.at[3 * k + j], (*chips[j], c)).start()
        token[...] = jnp.zeros_like(token)

    sem = pltpu.SemaphoreType.DMA((3 * n,))
    outs = pl.pallas_call(
        body, name=name,
        out_shape=(sem, sem, *[pltpu.HBM(s.shape, s.dtype) for s in srcs], *[pltpu.HBM(sh, BF16) for sh in land_shapes],
                   jax.ShapeDtypeStruct((8, 128), F32)),
        in_specs=[HBM_SPEC] * (2 * n),
        out_specs=(SEM_SPEC, SEM_SPEC, *[HBM_SPEC] * (2 * n), pl.BlockSpec(memory_space=pltpu.VMEM)),
        input_output_aliases={i: 2 + i for i in range(2 * n)},
        compiler_params=pltpu.CompilerParams(has_side_effects=DATAFLOW),
    )(*[pltpu.with_memory_space_constraint(s, pltpu.HBM) for s in srcs],
      *[pltpu.with_memory_space_constraint(lax.empty(sh, BF16), pltpu.HBM) for sh in land_shapes])
    return (kind, outs[0], outs[1], list(outs[2:2 + n]), list(outs[2 + n:2 + 2 * n])), outs[-1]


def _ici_wait(handle, after, name):
    kind, ssem_in, rsem_in, srcs, lands = handle
    n = len(srcs)

    def body(*refs):
        src, land = refs[:n], refs[n:2 * n]
        ssem, rsem = refs[2 * n], refs[2 * n + 1]
        x, y, c, chip, chips, chip_ids = _place()
        for k in range(n):
            for j in range(3):
                s, _, mine = _ici_ends(kind, src[k], land[k], j, c, chip, chip_ids)
                cp = _remote(s, mine, ssem.at[3 * k + j], rsem.at[3 * k + j], (*chips[j], c))
                cp.wait_send()
                cp.wait_recv()

    outs = pl.pallas_call(
        body, name=name, out_shape=[pltpu.HBM(t.shape, t.dtype) for t in srcs + lands],
        in_specs=[HBM_SPEC] * (2 * n) + [SEM_SPEC, SEM_SPEC, ANY], out_specs=[HBM_SPEC] * (2 * n),
        input_output_aliases={i: i for i in range(2 * n)},
        compiler_params=pltpu.CompilerParams(has_side_effects=DATAFLOW),
    )(*srcs, *lands, ssem_in, rsem_in, after)
    return list(outs[:n]), list(outs[n:])


def _ag_pair(shards, lands, name):
    n = len(shards)

    def body(*refs):
        ins, outs = refs[:n], refs[2 * n:3 * n]
        token = refs[3 * n]
        s_fwd, r_fwd, s_own, r_own = refs[3 * n + 1:]
        x, y, c, chip, chips, chip_ids = _place()
        sib = (x, y, 1 - c)
        cps = []
        for k in range(n):
            cp = _remote(ins[k], outs[k].at[chip], s_own.at[k], r_own.at[k], sib)
            cp.start()
            cps.append(cp)
            for j in range(3):
                got = outs[k].at[chip_ids[j], c]
                cp = _remote(got, got, s_fwd.at[k, j], r_fwd.at[k, j], sib)
                cp.start()
                cps.append(cp)
        for k in range(n):
            _remote(ins[k], outs[k].at[chip], s_own.at[k], r_own.at[k], sib).wait_recv()
            for j in range(3):
                got = outs[k].at[chip_ids[j], 1 - c]
                _remote(got, got, s_fwd.at[k, j], r_fwd.at[k, j], sib).wait_recv()
        for cp in cps:
            cp.wait_send()
        token[...] = jnp.zeros_like(token)

    sem, sem1 = pltpu.SemaphoreType.DMA((n, 3)), pltpu.SemaphoreType.DMA((n,))
    outs = pl.pallas_call(
        body, name=name, out_shape=[jax.ShapeDtypeStruct(t.shape, t.dtype) for t in lands] + [jax.ShapeDtypeStruct((8, 128), F32)],
        in_specs=[ANY] * (2 * n), out_specs=[ANY] * n + [pl.BlockSpec(memory_space=pltpu.VMEM)],
        input_output_aliases={n + k: k for k in range(n)},
        scratch_shapes=[sem, sem, sem1, sem1], compiler_params=pltpu.CompilerParams(has_side_effects=True),
    )(*shards, *lands)
    return list(outs[:n]), outs[n]


def _forward_plan(n):
    def plan(refs, c, chip, chip_ids):
        out = []
        for k in range(n):
            shard, land = refs[k], refs[n + k]
            out.append((shard, land.at[chip], land.at[chip]))
            out += [(land.at[q, c], land.at[q, c], land.at[q, 1 - c]) for q in chip_ids]
        return out
    return plan, 4 * n


def _swap_plan(n):
    def plan(refs, c, chip, chip_ids):
        return [(refs[k].at[q, 1 - c], refs[n + k].at[q], refs[n + k].at[q]) for k in range(n) for q in range(N_CHIPS)]
    return plan, N_CHIPS * n


def _d2d_start(arrays, new_shapes, plan_n, name):
    plan, n_copies = plan_n
    n = len(arrays) + len(new_shapes)

    def body(*refs):
        ssem, rsem, token = refs[n], refs[n + 1], refs[-1]
        x, y, c, chip, _, chip_ids = _place()
        for i, (s, d, _) in enumerate(plan(refs[:n], c, chip, chip_ids)):
            _remote(s, d, ssem.at[i], rsem.at[i], (x, y, 1 - c)).start()
        token[...] = jnp.zeros_like(token)

    sem = pltpu.SemaphoreType.DMA((n_copies,))
    args = [pltpu.with_memory_space_constraint(t, pltpu.HBM) for t in arrays] + \
           [pltpu.with_memory_space_constraint(lax.empty(sh, BF16), pltpu.HBM) for sh in new_shapes]
    outs = pl.pallas_call(
        body, name=name,
        out_shape=(sem, sem, *[pltpu.HBM(t.shape, t.dtype) for t in args], jax.ShapeDtypeStruct((8, 128), F32)),
        in_specs=[HBM_SPEC] * n, out_specs=(SEM_SPEC, SEM_SPEC, *[HBM_SPEC] * n, pl.BlockSpec(memory_space=pltpu.VMEM)),
        input_output_aliases={i: 2 + i for i in range(n)},
        compiler_params=pltpu.CompilerParams(has_side_effects=DATAFLOW),
    )(*args)
    return (plan, outs[0], outs[1], list(outs[2:2 + n])), outs[-1]


def _d2d_wait(handle, after, name):
    plan, ssem_in, rsem_in, arrays = handle
    n = len(arrays)

    def body(*refs):
        ssem, rsem = refs[n], refs[n + 1]
        x, y, c, chip, _, chip_ids = _place()
        for i, (s, _, mine) in enumerate(plan(refs[:n], c, chip, chip_ids)):
            cp = _remote(s, mine, ssem.at[i], rsem.at[i], (x, y, 1 - c))
            cp.wait_send()
            cp.wait_recv()

    outs = pl.pallas_call(
        body, name=name, out_shape=[pltpu.HBM(t.shape, t.dtype) for t in arrays],
        in_specs=[HBM_SPEC] * n + [SEM_SPEC, SEM_SPEC, ANY], out_specs=[HBM_SPEC] * n,
        input_output_aliases={i: i for i in range(n)},
        compiler_params=pltpu.CompilerParams(has_side_effects=DATAFLOW),
    )(*arrays, ssem_in, rsem_in, after)
    return list(outs)


def _rs_pair(grads, name):
    n_arr = len(grads)

    def body(*refs):
        ins, got = refs[:n_arr], refs[n_arr:2 * n_arr]
        ssem, rsem = refs[2 * n_arr:]
        x, y, c, _, _, _ = _place()
        sib = (x, y, 1 - c)
        sends = []
        for k in reversed(range(n_arr)):
            for q in range(N_CHIPS):
                cp = _remote(ins[k].at[q, 1 - c], got[k].at[q], ssem.at[k, q], rsem.at[k, q], sib)
                cp.start()
                sends.append(cp)
        for k in range(n_arr):
            for q in range(N_CHIPS):
                _remote(got[k].at[q], got[k].at[q], ssem.at[k, q], rsem.at[k, q], sib).wait_recv()
        for cp in sends:
            cp.wait_send()

    shp = [jax.ShapeDtypeStruct((N_CHIPS,) + g.shape[2:], BF16) for g in grads]
    sem = pltpu.SemaphoreType.DMA((n_arr, N_CHIPS))
    outs = pl.pallas_call(
        body, name=name, out_shape=shp, in_specs=[ANY] * n_arr, out_specs=[ANY] * n_arr,
        scratch_shapes=[sem, sem], compiler_params=pltpu.CompilerParams(has_side_effects=True),
    )(*grads)
    return list(outs)


def _rs_share(bufs, name):
    n = len(bufs)

    def body(*refs):
        outs = refs[n:2 * n]
        ssem, rsem = refs[2 * n:]
        x, y, c, _, _, _ = _place()
        sib = (x, y, 1 - c)
        sends = []
        for k in range(n):
            for l in range(DEPTH):
                cp = _remote(outs[k].at[l, c], outs[k].at[l, c], ssem.at[k, l], rsem.at[k, l], sib)
                cp.start()
                sends.append(cp)
        for k in range(n):
            for l in range(DEPTH):
                dst = outs[k].at[l, 1 - c]
                _remote(dst, dst, ssem.at[k, l], rsem.at[k, l], sib).wait_recv()
        for cp in sends:
            cp.wait_send()

    sem = pltpu.SemaphoreType.DMA((n, DEPTH))
    outs = pl.pallas_call(
        body, name=name, out_shape=[jax.ShapeDtypeStruct(b.shape, b.dtype) for b in bufs], in_specs=[ANY] * n, out_specs=[ANY] * n,
        input_output_aliases={k: k for k in range(n)},
        scratch_shapes=[sem, sem], compiler_params=pltpu.CompilerParams(has_side_effects=True),
    )(*bufs)
    return list(outs)


def _piece(src, idx, rows, width=128, align=1, transposed=False):
    return dict(src=src, idx=idx, rows=rows, width=width, align=align, transposed=transposed)


def _all_reduce_pieces(inputs, pieces, out_shapes, writes, name):
    n_in, n_out = len(inputs), len(out_shapes)
    offs, R = [], 0
    for p in pieces:
        R = -(-R // p["align"]) * p["align"]
        offs.append(R)
        R += p["rows"]
    R = -(-R // 8) * 8

    def body(*refs):
        ins, outs, token_ref = refs[:n_in], refs[n_in:n_in + n_out], refs[n_in + n_out]
        pair_ref, chip_ref, sum_ref, ssem, rsem = refs[n_in + n_out + 1:]
        token_ref[...] = jnp.zeros_like(token_ref)
        x, y, c, chip, chips, chip_ids = _place()
        pair_ref[c] = jnp.zeros((R, 128), F32)
        for p, off in zip(pieces, offs):
            v = ins[p["src"]][...].T[p["idx"]] if p["transposed"] else ins[p["src"]][p["idx"]]
            pair_ref[c, off:off + p["rows"], 0:p["width"]] = v
        mine = _remote(pair_ref.at[c], pair_ref.at[c], ssem.at[3], rsem.at[3], (x, y, 1 - c))
        mine.start()
        _remote(pair_ref.at[1 - c], pair_ref.at[1 - c], ssem.at[3], rsem.at[3], (x, y, 1 - c)).wait_recv()
        chip_ref[chip] = pair_ref[0] + pair_ref[1]
        cps = [_remote(chip_ref.at[chip], chip_ref.at[chip], ssem.at[j], rsem.at[j], (*chips[j], c)) for j in range(3)]
        for cp in cps:
            cp.start()
        for j in range(3):
            slot = chip_ref.at[chip_ids[j]]
            _remote(slot, slot, ssem.at[j], rsem.at[j], (*chips[j], c)).wait_recv()
        acc = chip_ref[0]
        for s in range(1, N_CHIPS):
            acc = acc + chip_ref[s]
        sum_ref[...] = acc
        for o, idx, p in writes:
            outs[o][idx] = sum_ref[offs[p]:offs[p] + pieces[p]["rows"], 0:pieces[p]["width"]]
        for cp in cps + [mine]:
            cp.wait_send()

    vm = pl.BlockSpec(memory_space=pltpu.VMEM)
    outs = pl.pallas_call(
        body, name=name, out_shape=[jax.ShapeDtypeStruct(s, F32) for s in out_shapes] + [jax.ShapeDtypeStruct((8, 128), F32)],
        in_specs=[vm] * n_in, out_specs=[vm] * (n_out + 1),
        scratch_shapes=[pltpu.VMEM((2, R, 128), F32), pltpu.VMEM((N_CHIPS, R, 128), F32), pltpu.VMEM((R, 128), F32),
                        pltpu.SemaphoreType.DMA((4,)), pltpu.SemaphoreType.DMA((4,))],
        compiler_params=pltpu.CompilerParams(vmem_limit_bytes=VMEM_LIMIT),
    )(*inputs)
    return list(outs[:n_out]), outs[n_out]


def _lanes(width):
    return [slice(k, min(k + 128, width)) for k in range(0, width, 128)]


def _gather_small_weights(cvw_z, scw_z):
    pieces, writes = [], []
    for i, arr in enumerate((cvw_z, scw_z)):
        for l in range(DEPTH):
            for ln in _lanes(HALF):
                writes.append((i, (l, slice(None), ln), len(pieces)))
                pieces.append(_piece(i, (l, slice(None), ln), arr.shape[1], align=8))
    (cvw, scw), tok = _all_reduce_pieces([cvw_z, scw_z], pieces, [cvw_z.shape, scw_z.shape], writes, "ag_small")
    return cvw, scw, tok


SMALL_RAW = dict(norm_mix=(1, D_MODEL), norm_ffn=(1, D_MODEL), sg_ln_g=(1, HALF), sg_ln_b=(1, HALF), cv_b=(1, HALF), cv_ln_g=(1, HALF),
                 cv_ln_b=(1, HALF))


def _all_reduce_small_grads(raw, d_nfinal, loss):
    names = list(SMALL_RAW) + ["attn_sinks", "sg_b", "sc_w", "cv_w", "sg_w"]
    out_shape = dict(norm_mix=(DEPTH, D_MODEL), norm_ffn=(DEPTH, D_MODEL), sg_ln_g=(DEPTH, HALF), sg_ln_b=(DEPTH, HALF), cv_b=(DEPTH, HALF),
                     cv_ln_g=(DEPTH, HALF), cv_ln_b=(DEPTH, HALF), attn_sinks=(DEPTH, N_Q_HEADS), sg_b=(DEPTH, SG_GROUPS, SG_CHUNK),
                     sc_w=(DEPTH, SC_KERNEL, HALF), cv_w=(DEPTH, CV_KERNEL, HALF), sg_w=(DEPTH, SG_GROUPS, SG_CHUNK, SG_CHUNK))
    inputs, pieces, writes = [], [], []

    def add(src, idx, rows, out, out_idx, **kw):
        writes.append((names.index(out) if out in names else out, out_idx, len(pieces)))
        pieces.append(_piece(src, idx, rows, **kw))

    for l in range(DEPTH):
        row = slice(l, l + 1)
        for n, (_, width) in SMALL_RAW.items():
            inputs.append(raw[l][n])
            for ln in _lanes(width):
                add(len(inputs) - 1, (slice(0, 1), ln), 1, n, (row, ln))
        inputs.append(raw[l]["attn_sinks"])
        add(len(inputs) - 1, (slice(0, 1), slice(0, N_Q_HEADS)), 1, "attn_sinks", (row, slice(None)), width=N_Q_HEADS)
    for l in range(DEPTH):
        inputs.append(raw[l]["sg_b"])
        add(len(inputs) - 1, (slice(0, SG_GROUPS), slice(None)), SG_GROUPS, "sg_b", (l,), align=8, transposed=True)
        inputs.append(raw[l]["sc_w"])
        for ln in _lanes(HALF):
            add(len(inputs) - 1, (slice(0, SC_KERNEL), ln), SC_KERNEL, "sc_w", (l, slice(None), ln), align=8)
        inputs.append(raw[l]["cv_w"])
        for ln in _lanes(HALF):
            add(len(inputs) - 1, (slice(0, CV_KERNEL), ln), CV_KERNEL, "cv_w", (l, slice(None), ln), align=8)
        inputs.append(raw[l]["sg_w"])
        for g in range(SG_GROUPS):
            add(len(inputs) - 1, (g,), SG_CHUNK, "sg_w", (l, g), align=8)
    n_names = len(names)
    inputs.append(d_nfinal)
    for ln in _lanes(D_MODEL):
        add(len(inputs) - 1, (slice(0, 1), ln), 1, n_names, (slice(0, 1), ln))
    inputs.append(loss)
    add(len(inputs) - 1, (slice(0, 1), slice(None)), 1, n_names + 1, (slice(0, 1), slice(None)))
    outs, tok = _all_reduce_pieces(inputs, pieces, [out_shape[n] for n in names] + [(1, D_MODEL), (1, 128)], writes, "ar_small")
    return dict(zip(names, outs[:n_names])), outs[n_names], outs[n_names + 1], tok


def _small_views(raw):
    v = {n: raw[n][0] for n in SMALL_RAW}
    v.update(sg_w=raw["sg_w"], sg_b=raw["sg_b"][:, :SG_GROUPS].T, cv_w=raw["cv_w"][:CV_KERNEL],
             attn_sinks=raw["attn_sinks"][0, :N_Q_HEADS], sc_w=raw["sc_w"][:SC_KERNEL])
    return v


def _row_tile(rows, cols, n_arrays):
    budget = 20 * 1024 * 1024 // (n_arrays * 2 * cols * 4)
    tiles = [t for t in range(16, min(rows, budget) + 1, 16) if rows % t == 0]
    assert tiles, (rows, cols)
    return tiles[-1]


def _add_pairs(g, got, place, name):
    _, _, rows, cols = g.shape
    tr = _row_tile(rows, cols, 3)

    def body(place_ref, a_ref, b_ref, o_ref):
        del place_ref
        o_ref[...] = (a_ref[...].astype(F32) + b_ref[...].astype(F32)).astype(BF16)

    spec = pl.BlockSpec((None, tr, cols), lambda q, i, p: (q, i, 0))
    grid_spec = pltpu.PrefetchScalarGridSpec(
        num_scalar_prefetch=1, grid=(N_CHIPS, rows // tr),
        in_specs=[pl.BlockSpec((None, None, tr, cols), lambda q, i, p: (q, p[1], i, 0)), spec], out_specs=spec)
    return pl.pallas_call(body, name=name, grid_spec=grid_spec, out_shape=jax.ShapeDtypeStruct((N_CHIPS, rows, cols), BF16),
                          compiler_params=_params("parallel", "parallel"))(place, g, got)


def _sum_chips(own, recv, place, l, buf, name, after):
    _, rows, cols = own.shape
    tr = _row_tile(rows, cols, 4)

    def body(place_ref, own_ref, recv_ref, *rest):
        chip = place_ref[0]
        acc = own_ref[...].astype(F32)
        for j in range(1, N_CHIPS):
            acc = acc + recv_ref[lax.rem(chip + j, N_CHIPS)].astype(F32)
        rest[-1][...] = acc

    in_specs = [pl.BlockSpec((None, tr, cols), lambda i, p: (p[0], i, 0)), pl.BlockSpec((N_CHIPS, tr, cols), lambda i, p: (0, i, 0)), ANY]
    args = [place, own, recv, after]
    aliases = {}
    if buf is not None:
        in_specs.append(ANY)
        args.append(buf)
        aliases = {4: 0}
    grid_spec = pltpu.PrefetchScalarGridSpec(
        num_scalar_prefetch=1, grid=(rows // tr,), in_specs=in_specs,
        out_specs=pl.BlockSpec((None, None, tr, cols), lambda i, p: (l, p[1], i, 0)))
    return pl.pallas_call(body, name=name, grid_spec=grid_spec, out_shape=jax.ShapeDtypeStruct((DEPTH, 2, rows, cols), F32),
                          input_output_aliases=aliases, compiler_params=_params("parallel"))(*args)


def _adamw(w, g, m, v, name):
    shape = w.shape
    lead, (rows, cols) = shape[:-2], shape[-2:]
    tr = _row_tile(rows, cols, 7)

    def body(w_ref, g_ref, m_ref, v_ref, d_ref, mo_ref, vo_ref):
        gv = g_ref[...]
        mn = ADAM_B1 * m_ref[...] + (1.0 - ADAM_B1) * gv
        vn = ADAM_B2 * v_ref[...] + (1.0 - ADAM_B2) * (gv * gv)
        m_hat = mn / (1.0 - ADAM_B1 ** ADAM_STEP)
        v_hat = vn / (1.0 - ADAM_B2 ** ADAM_STEP)
        d_ref[...] = -ADAM_LR * (m_hat / (jnp.sqrt(v_hat) + ADAM_EPS) + ADAM_WD * w_ref[...])
        mo_ref[...] = mn
        vo_ref[...] = vn

    spec = pl.BlockSpec((None,) * len(lead) + (tr, cols), lambda *idx: (*idx, 0))
    grid = lead + (rows // tr,)
    return list(pl.pallas_call(body, name=name, grid=grid, in_specs=[spec] * 4, out_specs=[spec] * 3,
                               out_shape=[jax.ShapeDtypeStruct(shape, F32)] * 3,
                               compiler_params=_params(*(["parallel"] * len(grid))))(w, g, m, v))


def _adamw_small(ws, gs, ms, vs, name):
    n = len(ws)

    def body(*refs):
        for i in range(n):
            gv = refs[n + i][...]
            mn = ADAM_B1 * refs[2 * n + i][...] + (1.0 - ADAM_B1) * gv
            vn = ADAM_B2 * refs[3 * n + i][...] + (1.0 - ADAM_B2) * (gv * gv)
            m_hat = mn / (1.0 - ADAM_B1 ** ADAM_STEP)
            v_hat = vn / (1.0 - ADAM_B2 ** ADAM_STEP)
            refs[4 * n + i][...] = -ADAM_LR * (m_hat / (jnp.sqrt(v_hat) + ADAM_EPS) + ADAM_WD * refs[i][...])
            refs[5 * n + i][...] = mn
            refs[6 * n + i][...] = vn

    vm = pl.BlockSpec(memory_space=pltpu.VMEM)
    outs = pl.pallas_call(body, name=name, out_shape=[jax.ShapeDtypeStruct(t.shape, F32) for t in ws] * 3,
                          in_specs=[vm] * (4 * n), out_specs=[vm] * (3 * n),
                          compiler_params=pltpu.CompilerParams(vmem_limit_bytes=VMEM_LIMIT))(*ws, *gs, *ms, *vs)
    return outs[:n], outs[n:2 * n], outs[2 * n:]


SMALL = ("norm_mix", "sg_ln_g", "sg_ln_b", "sg_w", "sg_b", "cv_w", "cv_b", "cv_ln_g", "cv_ln_b", "attn_sinks", "sc_w", "norm_ffn", "norm_final")
ORDER = ("norm_mix", "w_in", "sg_ln_g", "sg_ln_b", "sg_w", "sg_b", "cv_w", "cv_b", "cv_ln_g", "cv_ln_b", "attn_sinks", "sc_w",
         "w_branch", "w_out", "norm_ffn", "w_gate_up", "w_down", "norm_final")


def kernel(x, norm_mix, w_in, sg_ln_g, sg_ln_b, sg_w, sg_b, cv_w, cv_b, cv_ln_g, cv_ln_b, attn_sinks, sc_w, w_branch, w_out, norm_ffn, w_gate_up, w_down, norm_final, loss_target, m_norm_mix, m_w_in, m_sg_ln_g, m_sg_ln_b, m_sg_w, m_sg_b, m_cv_w, m_cv_b, m_cv_ln_g, m_cv_ln_b, m_attn_sinks, m_sc_w, m_w_branch, m_w_out, m_norm_ffn, m_w_gate_up, m_w_down, m_norm_final, v_norm_mix, v_w_in, v_sg_ln_g, v_sg_ln_b, v_sg_w, v_sg_b, v_cv_w, v_cv_b, v_cv_ln_g, v_cv_ln_b, v_attn_sinks, v_sc_w, v_w_branch, v_w_out, v_norm_ffn, v_w_gate_up, v_w_down, v_norm_final):
    W = dict(norm_mix=norm_mix, w_in=w_in, sg_ln_g=sg_ln_g, sg_ln_b=sg_ln_b, sg_w=sg_w, sg_b=sg_b, cv_w=cv_w, cv_b=cv_b, cv_ln_g=cv_ln_g,
             cv_ln_b=cv_ln_b, attn_sinks=attn_sinks, sc_w=sc_w, w_branch=w_branch, w_out=w_out, norm_ffn=norm_ffn, w_gate_up=w_gate_up,
             w_down=w_down, norm_final=norm_final)
    M = dict(norm_mix=m_norm_mix, w_in=m_w_in, sg_ln_g=m_sg_ln_g, sg_ln_b=m_sg_ln_b, sg_w=m_sg_w, sg_b=m_sg_b, cv_w=m_cv_w, cv_b=m_cv_b,
             cv_ln_g=m_cv_ln_g, cv_ln_b=m_cv_ln_b, attn_sinks=m_attn_sinks, sc_w=m_sc_w, w_branch=m_w_branch, w_out=m_w_out,
             norm_ffn=m_norm_ffn, w_gate_up=m_w_gate_up, w_down=m_w_down, norm_final=m_norm_final)
    V = dict(norm_mix=v_norm_mix, w_in=v_w_in, sg_ln_g=v_sg_ln_g, sg_ln_b=v_sg_ln_b, sg_w=v_sg_w, sg_b=v_sg_b, cv_w=v_cv_w, cv_b=v_cv_b,
             cv_ln_g=v_cv_ln_g, cv_ln_b=v_cv_ln_b, attn_sinks=v_attn_sinks, sc_w=v_sc_w, w_branch=v_w_branch, w_out=v_w_out,
             norm_ffn=v_norm_ffn, w_gate_up=v_w_gate_up, w_down=v_w_down, norm_final=v_norm_final)
    mx, my, mc = lax.axis_index("x"), lax.axis_index("y"), lax.axis_index("c")
    chip = 2 * mx + my

    place = jnp.stack([chip, mc]).astype(jnp.int32)
    tables = _rope_tables(x.shape[1])
    land_shapes = [(N_CHIPS,) + HALF_SHAPE[n] for n in BIG]
    part_shapes = {n: (N_CHIPS,) + HALF_SHAPE[n][1:] for n in BIG}

    T_ = lambda t: jnp.swapaxes(t, 1, 2)
    Wt, Mt, Vt = ({**t, "w_in": T_(t["w_in"])} for t in (W, M, V))

    def shards_of(l, tok):
        return [(Wt[n][l] + tok[0, 0]).astype(BF16).reshape(HALF_SHAPE[n]) for n in BIG]

    def finish_gather(tag, handle, after):
        srcs, lands = _ici_wait(handle, after, f"ag_wait{tag}")
        return _ag_pair(srcs, lands, f"ag_pair{tag}")[0]

    def mix_weights(l, g_in):
        return dict(w_in=_w_in_layout(g_in[0].reshape(N_CHIPS, W_IN_SHARD, D_MODEL)), norm_mix=norm_mix[l][None], norm_ffn=norm_ffn[l][None],
                    mixer=_mixer_params(l, sg_ln_g, sg_ln_b, sg_w, sg_b, cvw_full, cv_b, cv_ln_g, cv_ln_b, attn_sinks, scw_full))

    def rest_weights(lw, g_rest):
        G = dict(zip(BIG[1:], g_rest))
        lw.update(w_branch=G["w_branch"].reshape(N_CHIPS, N_BRANCH, HALF, 256), w_out=G["w_out"].reshape(D_MODEL, D_MODEL),
                  w_gate_up=G["w_gate_up"].reshape(N_CHIPS, D_MODEL, GU_SHARD), w_down=G["w_down"].reshape(D_FF, D_MODEL))

    def shard_major(g):
        t = dict(g)
        if "w_in" in t:
            t["w_in"] = _w_in_unlayout(t["w_in"])
        return {n: t[n].reshape((N_CHIPS,) + HALF_SHAPE[n]) for n in BIG if n in t}

    zero_tok = jnp.zeros((8, 128), F32)
    south = (mc == 0).astype(F32)
    cvw_z = lax.dynamic_update_slice(jnp.zeros((DEPTH, CV_KERNEL, HALF), F32), cv_w * south, (0, 0, chip * 128))
    scw_z = lax.dynamic_update_slice(jnp.zeros((DEPTH, SC_KERNEL, HALF), F32), sc_w * south, (0, 0, chip * 128))
    cvw_full, scw_full, tok = _gather_small_weights(cvw_z, scw_z)
    handles = []
    for l in range(DEPTH):
        for tag, sl in (("in", slice(0, 1)), ("rest", slice(1, NB))):
            h, tok = _ici_start("gather", shards_of(l, tok)[sl], land_shapes[sl], f"ag_start{l}{tag}")
            handles.append(h)
    pending = {}

    def behind(l, key):
        def order(lw, token):
            if key == "mixer":
                lw["mixer"] = [lw["mixer"][0] + token[0, 0]] + lw["mixer"][1:]
            else:
                lw[key] = lw[key] + token[0, 0]
        return order

    def early_pair(tag, handle, order):
        def between(after, lw):
            srcs, lands = _ici_wait(handle, after, f"ag_wait{tag}")
            pending[tag], token = _d2d_start(srcs + lands, [], _forward_plan(len(srcs)), f"ag_pair_start{tag}")
            order(lw, token)
        return between

    def finish_pair(tag, after):
        arrays = _d2d_wait(pending.pop(tag), after, f"ag_pair_wait{tag}")
        return arrays[len(arrays) // 2:]

    lw0 = mix_weights(0, finish_gather("0in", handles[0], tok))
    mixed = _fwd_layer_mix(0, x[0], lw0, tables)
    rest_weights(lw0, finish_gather("0rest", handles[1], mixed[2]))
    x1, sv0 = _fwd_layer_rest(0, x[0], mixed, lw0, early_pair("1in", handles[2], behind(0, "norm_ffn")))
    lw1 = mix_weights(1, finish_pair("1in", x1))
    mixed = _fwd_layer_mix(1, x1, lw1, tables, early_pair("1rest", handles[3], behind(1, "mixer")))
    rest_weights(lw1, finish_pair("1rest", mixed[2]))
    x2, sv1 = _fwd_layer_rest(1, x1, mixed, lw1)
    dx, d_nfinal, loss = _final_loss(x2, norm_final[None], loss_target[0], 256, "final_loss")

    lw1["after"] = zero_tok
    carry, g_ffn1 = _bwd_layer_ffn(1, dx, lw1, sv1)
    g1 = shard_major(g_ffn1)
    names_f = list(g1)
    h_swap, tok = _d2d_start([g1[n] for n in names_f], [part_shapes[n] for n in names_f], _swap_plan(len(names_f)), "rs_pair_start1")
    behind(1, "mixer")(lw1, tok)
    def early_swap(tag):
        def between(d_win, lw):
            g = shard_major({"w_in": d_win})["w_in"]
            pending[tag], token = _d2d_start([g], [part_shapes["w_in"]], _swap_plan(1), f"rs_pair_start{tag}")
            behind(None, "norm_mix")(lw, token)
        return between

    dx, g_mix1 = _bwd_layer_mix(1, carry, lw1, sv1, tables, early_swap("1in"))
    swapped = _d2d_wait(h_swap, dx, "rs_pair_wait1")
    own_in, got_in = _d2d_wait(pending.pop("1in"), dx, "rs_pair_wait1in")
    names1 = ["w_in"] + names_f
    own1 = [own_in] + swapped[:len(names_f)]
    got1 = [got_in] + swapped[len(names_f):]
    part1 = [_add_pairs(own1[k], got1[k], place, f"rs_add1_{n}") for k, n in enumerate(names1)]
    hr1, tok = _ici_start("scatter", part1, [part_shapes[n] for n in names1], "rs_start1")

    def early_ffn_swap(grads, lw):
        g = shard_major(grads)
        pending["0ffn"], token = _d2d_start([g[n] for n in g], [part_shapes[n] for n in g], _swap_plan(len(g)), "rs_pair_start0ffn")
        behind(None, "norm_ffn")(lw, token)

    lw0["after"] = tok
    carry, g_ffn0 = _bwd_layer_ffn(0, dx, lw0, sv0, early_ffn_swap)
    g0 = shard_major({n: g_ffn0[n] for n in ("w_branch", "w_out")})
    names_a = list(g0) + ["w_gate_up", "w_down"]
    swapped = _d2d_wait(pending.pop("0ffn"), g_ffn0["w_branch"], "rs_pair_wait0ffn")
    own_a = [g0[n] for n in g0] + swapped[:2]
    got_a = _rs_pair([g0[n] for n in g0], "rs_pair0a") + swapped[2:]
    part_a = [_add_pairs(own_a[k], got_a[k], place, f"rs_add0a_{n}") for k, n in enumerate(names_a)]
    _, recv1 = _ici_wait(hr1, part_a[0], "rs_wait1")
    hra, tok = _ici_start("scatter", part_a, [part_shapes[n] for n in names_a], "rs_start0a")

    lw0["mixer"] = [lw0["mixer"][0] + tok[0, 0]] + lw0["mixer"][1:]
    dx, g_mix0 = _bwd_layer_mix(0, carry, lw0, sv0, tables, early_swap("0in"))
    _, recv_a = _ici_wait(hra, dx, "rs_wait0a")

    small_red, nf_red, loss_red, tok = _all_reduce_small_grads([{**g_ffn0, **g_mix0}, {**g_ffn1, **g_mix1}], d_nfinal, loss)
    small_red["norm_final"] = nf_red
    loss_out = loss_red[0, 0]
    for n in ("cv_w", "sc_w"):
        small_red[n] = lax.dynamic_slice_in_dim(small_red[n], chip * 128, 128, axis=2)

    own_in, got_in = _d2d_wait(pending.pop("0in"), tok, "rs_pair_wait0in")
    names_b, part_b = ["w_in"], [_add_pairs(own_in, got_in, place, "rs_add0b_w_in")]
    hrb, tok = _ici_start("scatter", part_b, [part_shapes[n] for n in names_b], "rs_start0b")
    bufs = {n: _sum_chips(part1[k], recv1[k], place, 1, None, f"rs_sum1_{n}", tok) for k, n in enumerate(names1)}
    for k, n in enumerate(names_a):
        bufs[n] = _sum_chips(part_a[k], recv_a[k], place, 0, bufs[n], f"rs_sum0_{n}", tok)
    shared = dict(zip(names_a, _rs_share([bufs[n] for n in names_a], "rs_share_a")))
    upd = {}
    for n in names_a:
        red = shared[n].reshape(W[n].shape)
        upd[n] = [red] + _adamw(W[n], red, M[n], V[n], f"adamw_{n}")
    two_d = lambda t: t[None] if t.ndim == 1 else t
    small_upd = _adamw_small(*([two_d(t[n]) for n in SMALL] for t in (W, small_red, M, V)), "adamw_small")
    for n, d, mo, vo in zip(SMALL, *small_upd):
        upd[n] = [t.reshape(W[n].shape) for t in (small_red[n], d, mo, vo)]

    _, recv_b = _ici_wait(hrb, upd[names_a[-1]][1], "rs_wait0b")
    for k, n in enumerate(names_b):
        bufs[n] = _sum_chips(part_b[k], recv_b[k], place, 0, bufs[n], f"rs_sum0_{n}", tok)
    shared = dict(zip(names_b, _rs_share([bufs[n] for n in names_b], "rs_share_b")))
    for n in names_b:
        red = shared[n].reshape(Wt[n].shape)
        upd[n] = [T_(t) for t in [red] + _adamw(Wt[n], red, Mt[n], Vt[n], f"adamw_{n}")]

    out = [loss_out, dx[None]]
    for k in range(4):
        out += [upd[n][k] for n in ORDER]
    return tuple(out)
```

```python
import functools
import math

import jax
import jax.numpy as jnp
from jax import lax
from jax.experimental import pallas as pl
from jax.experimental.pallas import tpu as pltpu

F32 = jnp.float32
BF16 = jnp.bfloat16

D_MODEL = 1024
DEPTH = 2
HALF = 512
SG_CHUNK = 128
SG_GROUPS = 4
CV_KERNEL = 31
HEAD_DIM = 64
N_Q_HEADS = 8
N_KV_HEADS = 2
Q_PER_KV = N_Q_HEADS // N_KV_HEADS
WINDOW = 128
ROPE_THETA = 10000.0
SC_KERNEL = 3
N_BRANCH = 4
D_FF = 2816
EPS = 1e-6
N_CHIPS = 4
N_DEV = 8

MIX_W = 4352
GATE_W = N_BRANCH * D_MODEL
PROJ_PAD = 2 * MIX_W
W_IN_SHARD = 2112
GU_SHARD = 1408
HALO = 128
CV_PAD = 32

ADAM_LR = 0.001
ADAM_B1 = 0.9
ADAM_B2 = 0.999
ADAM_EPS = 1e-08
ADAM_WD = 0.01
ADAM_STEP = 10

VMEM_LIMIT = 56 * 1024 * 1024
INV_SQRT2 = 1.0 / math.sqrt(2.0)
INV_SQRT_2PI = 1.0 / math.sqrt(2.0 * math.pi)
NEG_BIG = -1e30
MESH = pl.DeviceIdType.MESH

C_ZA, C_ZB, C_Q, C_K, C_V, C_ZD = 0, 1024, 2048, 2560, 2688, 2816


def _params(*sem):
    return pltpu.CompilerParams(dimension_semantics=sem, vmem_limit_bytes=VMEM_LIMIT)


def _sig(v):
    return 1.0 / (1.0 + jnp.exp(-v))


def _dot(a, b):
    return jnp.dot(a, b, preferred_element_type=F32)


def _dot_nt(a, b):
    return lax.dot_general(a, b, (((1,), (1,)), ((), ())), preferred_element_type=F32)


def _dot_tn(a, b):
    return lax.dot_general(a, b, (((0,), (0,)), ((), ())), preferred_element_type=F32)


def _full(shape):
    nd = len(shape)
    return pl.BlockSpec(shape, lambda *_: (0,) * nd)


def _rms_mm(x, g, w, tm, tn, name):
    T = x.shape[0]
    transposed = w.ndim == 2
    if transposed:
        N = w.shape[0]
        wspec = pl.BlockSpec((tn, D_MODEL), lambda i, j: (j, 0))
    else:
        tn = w.shape[2]
        N = w.shape[0] * tn
        wspec = pl.BlockSpec((None, D_MODEL, tn), lambda i, j: (j, 0, 0))

    def body(x_ref, g_ref, w_ref, o_ref, xn_ref):
        @pl.when(pl.program_id(1) == 0)
        def _():
            xv = x_ref[...]
            r = lax.rsqrt(jnp.mean(xv * xv, axis=-1, keepdims=True) + EPS)
            xn_ref[...] = (xv * r * g_ref[...]).astype(BF16)

        o_ref[...] = (_dot_nt if transposed else _dot)(xn_ref[...], w_ref[...]).astype(BF16)

    return pl.pallas_call(
        body, name=name, grid=(T // tm, N // tn),
        in_specs=[pl.BlockSpec((tm, D_MODEL), lambda i, j: (i, 0)), _full((1, D_MODEL)), wspec],
        out_specs=[pl.BlockSpec((tm, tn), lambda i, j: (i, j)), pl.BlockSpec((tm, D_MODEL), lambda i, j: (i, 0))],
        out_shape=[jax.ShapeDtypeStruct((T, N), BF16), jax.ShapeDtypeStruct((T, D_MODEL), BF16)],
        compiler_params=_params("parallel", "arbitrary"),
    )(x, g, w)


def _merge_fwd(x, ys, proj, wb, wo, tm, name):
    T = x.shape[0]

    def body(x_ref, ys_ref, zg_ref, wb_ref, wo_ref, xo_ref, mg_ref):
        merged = None
        for n in range(N_BRANCH):
            yn = ys_ref[:, n * HALF:(n + 1) * HALF]
            br = jnp.concatenate([_dot(yn, wb_ref[s, n]) for s in range(N_CHIPS)], axis=1)
            t = _sig(zg_ref[:, n * D_MODEL:(n + 1) * D_MODEL].astype(F32)) * br
            merged = t if merged is None else merged + t
        mb = merged.astype(BF16)
        mg_ref[...] = mb
        xo_ref[...] = x_ref[...] + _dot(mb, wo_ref[...])

    return pl.pallas_call(
        body, name=name, grid=(T // tm,),
        in_specs=[pl.BlockSpec((tm, D_MODEL), lambda i: (i, 0)), pl.BlockSpec((tm, N_BRANCH * HALF), lambda i: (i, 0)),
                  pl.BlockSpec((tm, GATE_W), lambda i: (i, 0)), _full(wb.shape), _full(wo.shape)],
        out_specs=[pl.BlockSpec((tm, D_MODEL), lambda i: (i, 0)), pl.BlockSpec((tm, D_MODEL), lambda i: (i, 0))],
        out_shape=[jax.ShapeDtypeStruct((T, D_MODEL), F32), jax.ShapeDtypeStruct((T, D_MODEL), BF16)],
        compiler_params=_params("parallel"),
    )(x, ys, proj, wb, wo)


def _ffn_down(xm, gu, wd, tm, name):
    T = xm.shape[0]

    def body(x_ref, gu_ref, wd_ref, o_ref):
        g = gu_ref[:, :D_FF].astype(F32)
        u = gu_ref[:, D_FF:].astype(F32)
        act = (g * _sig(g) * u).astype(BF16)
        o_ref[...] = x_ref[...] + _dot(act, wd_ref[...])

    return pl.pallas_call(
        body, name=name, grid=(T // tm,),
        in_specs=[pl.BlockSpec((tm, D_MODEL), lambda i: (i, 0)), pl.BlockSpec((tm, 2 * D_FF), lambda i: (i, 0)), _full(wd.shape)],
        out_specs=pl.BlockSpec((tm, D_MODEL), lambda i: (i, 0)),
        out_shape=jax.ShapeDtypeStruct((T, D_MODEL), F32),
        compiler_params=_params("parallel"),
    )(xm, gu, wd)


def _final_loss(x, g, tgt, tm, name):
    T = x.shape[0]

    def body(x_ref, g_ref, t_ref, dx_ref, dg_ref, ls_ref):
        @pl.when(pl.program_id(0) == 0)
        def _():
            dg_ref[...] = jnp.zeros_like(dg_ref)
            ls_ref[...] = jnp.zeros_like(ls_ref)

        xv = x_ref[...]
        gv = g_ref[...]
        r = lax.rsqrt(jnp.mean(xv * xv, axis=-1, keepdims=True) + EPS)
        xh = xv * r
        diff = xh * gv - t_ref[...]
        ls_ref[...] += jnp.full(ls_ref.shape, 0.5 / D_MODEL, F32) * jnp.sum(diff * diff)
        dy = diff * (1.0 / D_MODEL)
        dxh = dy * gv
        dx_ref[...] = r * (dxh - xh * jnp.mean(dxh * xh, axis=-1, keepdims=True))
        dg_ref[...] += jnp.sum(dy * xh, axis=0, keepdims=True)

    return pl.pallas_call(
        body, name=name, grid=(T // tm,),
        in_specs=[pl.BlockSpec((tm, D_MODEL), lambda i: (i, 0)), _full((1, D_MODEL)), pl.BlockSpec((tm, D_MODEL), lambda i: (i, 0))],
        out_specs=[pl.BlockSpec((tm, D_MODEL), lambda i: (i, 0)), _full((1, D_MODEL)), _full((1, 128))],
        out_shape=[jax.ShapeDtypeStruct((T, D_MODEL), F32), jax.ShapeDtypeStruct((1, D_MODEL), F32), jax.ShapeDtypeStruct((1, 128), F32)],
        compiler_params=_params("arbitrary"),
    )(x, g, tgt)


def _swiglu_bwd(dx, gu, wd, tm, name, after):
    T = dx.shape[0]

    def body(dx_ref, gu_ref, wd_ref, after_ref, dgu_ref, act_ref):
        del after_ref
        dact = _dot_nt(dx_ref[...].astype(BF16), wd_ref[...])
        g = gu_ref[:, :D_FF].astype(F32)
        u = gu_ref[:, D_FF:].astype(F32)
        s = _sig(g)
        silu = g * s
        act_ref[...] = (silu * u).astype(BF16)
        dgu_ref[:, :D_FF] = (dact * u * (s + silu * (1.0 - s))).astype(BF16)
        dgu_ref[:, D_FF:] = (dact * silu).astype(BF16)

    return pl.pallas_call(
        body, name=name, grid=(T // tm,),
        in_specs=[pl.BlockSpec((tm, D_MODEL), lambda i: (i, 0)), pl.BlockSpec((tm, 2 * D_FF), lambda i: (i, 0)), _full(wd.shape),
                  pl.BlockSpec(memory_space=pl.ANY)],
        out_specs=[pl.BlockSpec((tm, 2 * D_FF), lambda i: (i, 0)), pl.BlockSpec((tm, D_FF), lambda i: (i, 0))],
        out_shape=[jax.ShapeDtypeStruct((T, 2 * D_FF), BF16), jax.ShapeDtypeStruct((T, D_FF), BF16)],
        compiler_params=_params("parallel"),
    )(dx, gu, wd, after)


def _mm_tn(a, b, grid, a_block, a_map, b_block, b_map, o_shape, o_block, o_map, name, col_split=1):
    gk = grid[2]
    tm = [d for d in a_block if d is not None][-1]
    tn = [d for d in b_block if d is not None][-1]

    def body(a_ref, b_ref, o_ref, acc_ref):
        k = pl.program_id(2)
        p = _dot_tn(a_ref[...].astype(BF16), b_ref[...].astype(BF16))

        @pl.when(k == 0)
        def _():
            acc_ref[...] = p

        @pl.when(k > 0)
        def _():
            acc_ref[...] += p

        @pl.when(k == gk - 1)
        def _():
            if col_split == 1:
                o_ref[...] = acc_ref[...].astype(o_ref.dtype)
            else:
                w = tn // col_split
                for s in range(col_split):
                    o_ref[s] = acc_ref[:, s * w:(s + 1) * w].astype(o_ref.dtype)

    return pl.pallas_call(
        body, name=name, grid=grid,
        in_specs=[pl.BlockSpec(a_block, a_map), pl.BlockSpec(b_block, b_map)],
        out_specs=pl.BlockSpec(o_block, o_map),
        out_shape=jax.ShapeDtypeStruct(o_shape, BF16),
        scratch_shapes=[pltpu.VMEM((tm, tn), F32)],
        compiler_params=_params("parallel", "parallel", "arbitrary"),
    )(a, b)


def _mm_nt_rmsbwd(a, w, x, g, dres, tm, tk, name):
    T = x.shape[0]
    transposed = w.ndim == 2
    if transposed:
        gk = w.shape[0] // tk
        wspec = pl.BlockSpec((tk, D_MODEL), lambda i, k: (k, 0))
    else:
        tk = w.shape[2]
        gk = w.shape[0]
        wspec = pl.BlockSpec((None, D_MODEL, tk), lambda i, k: (k, 0, 0))

    def body(a_ref, w_ref, x_ref, g_ref, r_ref, dx_ref, dg_ref, acc_ref):
        i, k = pl.program_id(0), pl.program_id(1)
        p = (_dot if transposed else _dot_nt)(a_ref[...], w_ref[...])

        @pl.when(k == 0)
        def _():
            acc_ref[...] = p

        @pl.when(k > 0)
        def _():
            acc_ref[...] += p

        @pl.when(jnp.logical_and(i == 0, k == 0))
        def _():
            dg_ref[...] = jnp.zeros_like(dg_ref)

        @pl.when(k == gk - 1)
        def _():
            dh = acc_ref[...]
            xv = x_ref[...]
            r = lax.rsqrt(jnp.mean(xv * xv, axis=-1, keepdims=True) + EPS)
            xh = xv * r
            dxh = dh * g_ref[...]
            dx_ref[...] = r_ref[...] + r * (dxh - xh * jnp.mean(dxh * xh, axis=-1, keepdims=True))
            dg_ref[...] += jnp.sum(dh * xh, axis=0, keepdims=True)

    return pl.pallas_call(
        body, name=name, grid=(T // tm, gk),
        in_specs=[pl.BlockSpec((tm, tk), lambda i, k: (i, k)), wspec, pl.BlockSpec((tm, D_MODEL), lambda i, k: (i, 0)),
                  _full((1, D_MODEL)), pl.BlockSpec((tm, D_MODEL), lambda i, k: (i, 0))],
        out_specs=[pl.BlockSpec((tm, D_MODEL), lambda i, k: (i, 0)), _full((1, D_MODEL))],
        out_shape=[jax.ShapeDtypeStruct((T, D_MODEL), F32), jax.ShapeDtypeStruct((1, D_MODEL), F32)],
        scratch_shapes=[pltpu.VMEM((tm, D_MODEL), F32)],
        compiler_params=_params("arbitrary", "arbitrary"),
    )(a, w, x, g, dres)


def _merge_bwd(dxm, ys, proj, wb, wo, tm, name):
    T = dxm.shape[0]

    def body(dx_ref, ys_ref, zg_ref, wb_ref, wo_ref, dys_ref, dbr_ref, dp_ref):
        dmerged = _dot_nt(dx_ref[...].astype(BF16), wo_ref[...])
        for n in range(N_BRANCH):
            yn = ys_ref[:, n * HALF:(n + 1) * HALF]
            br = jnp.concatenate([_dot(yn, wb_ref[s, n]) for s in range(N_CHIPS)], axis=1)
            gt = _sig(zg_ref[:, n * D_MODEL:(n + 1) * D_MODEL].astype(F32))
            dbr = (gt * dmerged).astype(BF16)
            dbr_ref[:, n * D_MODEL:(n + 1) * D_MODEL] = dbr
            dp_ref[:, n * D_MODEL:(n + 1) * D_MODEL] = (dmerged * br * gt * (1.0 - gt)).astype(BF16)
            dy = None
            for s in range(N_CHIPS):
                t = _dot_nt(dbr[:, s * 256:(s + 1) * 256], wb_ref[s, n])
                dy = t if dy is None else dy + t
            dys_ref[:, n * HALF:(n + 1) * HALF] = dy.astype(BF16)
        dp_ref[:, GATE_W:] = jnp.zeros((tm, MIX_W - GATE_W), BF16)

    return pl.pallas_call(
        body, name=name, grid=(T // tm,),
        in_specs=[pl.BlockSpec((tm, D_MODEL), lambda i: (i, 0)), pl.BlockSpec((tm, N_BRANCH * HALF), lambda i: (i, 0)),
                  pl.BlockSpec((tm, GATE_W), lambda i: (i, 0)), _full(wb.shape), _full(wo.shape)],
        out_specs=[pl.BlockSpec((tm, N_BRANCH * HALF), lambda i: (i, 0)), pl.BlockSpec((tm, GATE_W), lambda i: (i, 0)),
                   pl.BlockSpec((tm, MIX_W), lambda i: (i, 0))],
        out_shape=[jax.ShapeDtypeStruct((T, N_BRANCH * HALF), BF16), jax.ShapeDtypeStruct((T, GATE_W), BF16),
                   jax.ShapeDtypeStruct((T, PROJ_PAD), BF16)],
        compiler_params=_params("parallel"),
    )(dxm, ys, proj, wb, wo)


def _gelu(v):
    return 0.5 * v * (1.0 + lax.erf(v * INV_SQRT2))


def _gelu_grad(v):
    return 0.5 * (1.0 + lax.erf(v * INV_SQRT2)) + v * jnp.exp(-0.5 * v * v) * INV_SQRT_2PI


def _rot_half(t):
    w = t.shape[1]
    lane = lax.broadcasted_iota(jnp.int32, t.shape, 1)
    return jnp.where((lane % HEAD_DIM) < HEAD_DIM // 2, pltpu.roll(t, w - HEAD_DIM // 2, 1), pltpu.roll(t, HEAD_DIM // 2, 1))


def _rope(t, cos, sin_signed):
    return t * cos + _rot_half(t) * sin_signed


def _rope_t(d, cos, sin_signed):
    return d * cos + _rot_half(d * sin_signed)


def _ln_fwd(v, g, b):
    mu = jnp.mean(v, axis=-1, keepdims=True)
    vc = v - mu
    r = lax.rsqrt(jnp.mean(vc * vc, axis=-1, keepdims=True) + EPS)
    vh = vc * r
    return vh * g + b, vh, r


def _ln_bwd(dn, vh, r, g):
    dvh = dn * g
    return r * (dvh - jnp.mean(dvh, axis=-1, keepdims=True) - vh * jnp.mean(dvh * vh, axis=-1, keepdims=True))


def _sublane_shifts(sh_ref, rows):
    for b in range(1, 8):
        sh_ref[b, 0:rows - 8, :] = sh_ref[0, pl.ds(b, rows - 8), :]


def _tap(sh_ref, off, n):
    return sh_ref[off % 8, pl.ds(off - off % 8, n), :]


def _tril_mask():
    return lax.broadcasted_iota(jnp.int32, (SG_CHUNK, SG_CHUNK), 0) >= lax.broadcasted_iota(jnp.int32, (SG_CHUNK, SG_CHUNK), 1)


def _band_masks():
    shape = (Q_PER_KV * WINDOW, 2 * WINDOW)
    row = lax.broadcasted_iota(jnp.int32, shape, 0) % WINDOW
    col = lax.broadcasted_iota(jnp.int32, shape, 1)
    band = (col > row) & (col <= row + WINDOW)
    return band, band & (col >= WINDOW)


def _attn_probs(qs, kh, sink_col, valid):
    s = jnp.where(valid, _dot_nt(qs, kh) * (HEAD_DIM ** -0.5), NEG_BIG)
    m = jnp.maximum(jnp.max(s, axis=-1, keepdims=True), sink_col)
    p = jnp.exp(s - m)
    es = jnp.exp(sink_col - m)
    inv = 1.0 / (jnp.sum(p, axis=-1, keepdims=True) + es)
    return p * inv, es * inv


def _sink_col(sinks_ref, h):
    return jnp.concatenate([jnp.broadcast_to(sinks_ref[:, h * Q_PER_KV + g:h * Q_PER_KV + g + 1], (WINDOW, 1))
                            for g in range(Q_PER_KV)], axis=0)


def _mixer_in_specs(TB, nb):
    r = TB // HALO
    last = nb * r - 1
    cur = pl.BlockSpec((TB, MIX_W), lambda i: (i, 1))
    prev = pl.BlockSpec((HALO, MIX_W), lambda i: (jnp.maximum(i * r - 1, 0), 1))
    nxt = pl.BlockSpec((HALO, MIX_W), lambda i: (jnp.minimum((i + 1) * r, last), 1))
    tcur = pl.BlockSpec((TB, 128), lambda i: (i, 0))
    tprev = pl.BlockSpec((HALO, 128), lambda i: (jnp.maximum(i * r - 1, 0), 0))
    tnxt = pl.BlockSpec((HALO, 128), lambda i: (jnp.minimum((i + 1) * r, last), 0))
    return cur, prev, nxt, tcur, tprev, tnxt


def _mixer_param_specs():
    return [_full((1, HALF)), _full((1, HALF)), _full((SG_GROUPS, SG_CHUNK, SG_CHUNK)), _full((SG_CHUNK, 128)),
            _full((32, HALF)), _full((1, HALF)), _full((1, HALF)), _full((1, HALF)), _full((1, 128)), _full((8, HALF))]


def _mixers_fwd(proj, cos_t, sin_t, mp, TB, name):
    T = proj.shape[0]
    nb = T // TB
    r = TB // HALO
    cur, prev, _, tcur, tprev, _ = _mixer_in_specs(TB, nb)

    def body(zc_ref, zp_ref, cc_ref, sc_ref, cp_ref, sp_ref,
             lg_ref, lb_ref, sgw_ref, sgb_ref, cvw_ref, cvb_ref, cvg_ref, cvbb_ref, sinks_ref, scw_ref,
             ys_ref, scr_ref, k_ref, v_ref, sh_ref):
        i = pl.program_id(0)
        pm = (i > 0).astype(F32)

        def colsE(c0, c1):
            return jnp.concatenate([zp_ref[:, c0:c1].astype(F32) * pm, zc_ref[:, c0:c1].astype(F32)], axis=0)

        a = _gelu(zc_ref[:, C_ZA:C_ZA + 2 * HALF].astype(F32))
        u = a[:, :HALF]
        vn, _, _ = _ln_fwd(a[:, HALF:], lg_ref[...], lb_ref[...])
        vnb = vn.astype(BF16)
        tril = _tril_mask()
        chunks = [slice(ci * SG_CHUNK, (ci + 1) * SG_CHUNK) for ci in range(r)]
        for g in range(SG_GROUPS):
            cols = slice(g * 128, (g + 1) * 128)
            wt = jnp.where(tril, sgw_ref[g], 0.0).astype(BF16)
            mixed = _dot(wt, jnp.concatenate([vnb[rows, cols] for rows in chunks], axis=1)) + sgb_ref[:, g:g + 1]
            for ci, rows in enumerate(chunks):
                ys_ref[rows, cols] = (u[rows, cols] * mixed[:, ci * 128:(ci + 1) * 128]).astype(BF16)

        def colsB(c0, c1):
            return jnp.concatenate([zp_ref[HALO - CV_PAD:, c0:c1].astype(F32) * pm, zc_ref[:, c0:c1].astype(F32)], axis=0)

        sh_ref[0] = colsB(C_ZB, C_ZB + HALF) * _sig(colsB(C_ZB + HALF, C_ZB + 2 * HALF))
        _sublane_shifts(sh_ref, TB + CV_PAD)
        c = jnp.broadcast_to(cvb_ref[...], (TB, HALF))
        for k in range(CV_KERNEL):
            c = c + cvw_ref[k:k + 1, :] * _tap(sh_ref, CV_PAD - (CV_KERNEL - 1) + k, TB)
        n, _, _ = _ln_fwd(c, cvg_ref[...], cvbb_ref[...])
        ys_ref[:, HALF:2 * HALF] = (n * _sig(n)).astype(BF16)

        zd = colsE(C_ZD + HALF, C_ZD + 3 * HALF)
        scr_ref[...] = zd[:, :HALF] * zd[:, HALF:]
        cv = None
        for k in range(SC_KERNEL):
            t = scw_ref[k:k + 1, :] * scr_ref[pl.ds(HALO - (SC_KERNEL - 1) + k, TB), :]
            cv = t if cv is None else cv + t
        ys_ref[:, 3 * HALF:4 * HALF] = (zc_ref[:, C_ZD:C_ZD + HALF].astype(F32) * cv).astype(BF16)

        cosE = jnp.concatenate([cp_ref[...], cc_ref[...]], axis=0)
        sinE = jnp.concatenate([sp_ref[...], sc_ref[...]], axis=0)
        k_ref[...] = _rope(colsE(C_K, C_K + 128), cosE, sinE).astype(BF16)
        v_ref[...] = colsE(C_V, C_V + 128).astype(BF16)
        cosC, sinC = cc_ref[...], sc_ref[...]
        q = jnp.concatenate([_rope(zc_ref[:, C_Q + 128 * j:C_Q + 128 * (j + 1)].astype(F32), cosC, sinC)
                             for j in range(4)], axis=1).astype(BF16)
        in_band, in_band_cur = _band_masks()
        sink_cols = [_sink_col(sinks_ref, h) for h in range(N_KV_HEADS)]
        for qb in range(r):
            valid = in_band if qb else in_band_cur | (in_band & (i > 0))
            for h in range(N_KV_HEADS):
                hc = slice(h * HEAD_DIM, (h + 1) * HEAD_DIM)
                kh = k_ref[qb * WINDOW:qb * WINDOW + 2 * WINDOW, hc]
                vh = v_ref[qb * WINDOW:qb * WINDOW + 2 * WINDOW, hc]
                qs = jnp.concatenate([q[qb * WINDOW:(qb + 1) * WINDOW, (h * Q_PER_KV + g) * HEAD_DIM:(h * Q_PER_KV + g + 1) * HEAD_DIM]
                                      for g in range(Q_PER_KV)], axis=0)
                probs, _ = _attn_probs(qs, kh, sink_cols[h], valid)
                o = _dot(probs.astype(BF16), vh)
                for g in range(Q_PER_KV):
                    c0 = 2 * HALF + (h * Q_PER_KV + g) * HEAD_DIM
                    ys_ref[qb * WINDOW:(qb + 1) * WINDOW, c0:c0 + HEAD_DIM] = o[g * WINDOW:(g + 1) * WINDOW].astype(BF16)

    return pl.pallas_call(
        body, name=name, grid=(nb,),
        in_specs=[cur, prev, tcur, tcur, tprev, tprev] + _mixer_param_specs(),
        out_specs=pl.BlockSpec((TB, 4 * HALF), lambda i: (i, 0)),
        out_shape=jax.ShapeDtypeStruct((T, 4 * HALF), BF16),
        scratch_shapes=[pltpu.VMEM((TB + HALO, HALF), F32), pltpu.VMEM((TB + HALO, 128), BF16), pltpu.VMEM((TB + HALO, 128), BF16),
                        pltpu.VMEM((8, TB + CV_PAD, HALF), F32)],
        compiler_params=_params("parallel"),
    )(proj, proj, cos_t, sin_t, cos_t, sin_t, *mp)


def _mixers_bwd(proj, dys, dproj, cos_t, sin_t, mp, TB, name):
    T = proj.shape[0]
    nb = T // TB
    r = TB // HALO
    RE = TB + 2 * HALO
    RC = TB + HALO
    cur, prev, nxt, tcur, tprev, tnxt = _mixer_in_specs(TB, nb)
    dcur = pl.BlockSpec((TB, 4 * HALF), lambda i: (i, 0))
    dnxt = pl.BlockSpec((HALO, 4 * HALF), lambda i: (jnp.minimum((i + 1) * r, nb * r - 1), 0))

    def body(zc_ref, zp_ref, zn_ref, dyc_ref, dyn_ref, cc_ref, sc_ref, cp_ref, sp_ref, cn_ref, sn_ref,
             lg_ref, lb_ref, sgw_ref, sgb_ref, cvw_ref, cvb_ref, cvg_ref, cvbb_ref, sinks_ref, scw_ref, dp_in_ref,
             dz_ref, dlg_ref, dlb_ref, dsgw_ref, dsgb_ref, dcvw_ref, dcvb_ref, dcvg_ref, dcvbb_ref, dsink_ref, dscw_ref,
             scr_ref, scr2_ref, k_ref, v_ref, dk_ref, dv_ref, dq_ref, sh_ref, sh2_ref):
        del dp_in_ref
        i = pl.program_id(0)
        pm = (i > 0).astype(F32)
        nm = (i < nb - 1).astype(F32)

        @pl.when(i == 0)
        def _():
            for ref in (dlg_ref, dlb_ref, dsgw_ref, dsgb_ref, dcvw_ref, dcvb_ref, dcvg_ref, dcvbb_ref, dsink_ref, dscw_ref):
                ref[...] = jnp.zeros_like(ref)

        def colsE(c0, c1):
            return jnp.concatenate([zp_ref[:, c0:c1].astype(F32) * pm, zc_ref[:, c0:c1].astype(F32),
                                    zn_ref[:, c0:c1].astype(F32)], axis=0)

        def colsC(c0, c1):
            return jnp.concatenate([zc_ref[:, c0:c1].astype(F32), zn_ref[:, c0:c1].astype(F32)], axis=0)

        def dyC(c0, c1):
            return jnp.concatenate([dyc_ref[:, c0:c1].astype(F32), dyn_ref[:, c0:c1].astype(F32) * nm], axis=0)

        za = zc_ref[:, C_ZA:C_ZA + 2 * HALF].astype(F32)
        a = _gelu(za)
        u = a[:, :HALF]
        lg = lg_ref[...]
        vn, vh, rs = _ln_fwd(a[:, HALF:], lg, lb_ref[...])
        vnb = vn.astype(BF16)
        dya = dyc_ref[:, 0:HALF].astype(F32)
        tril = _tril_mask()
        lane128 = lax.broadcasted_iota(jnp.int32, (SG_CHUNK, 128), 1)
        chunks = [slice(ci * SG_CHUNK, (ci + 1) * SG_CHUNK) for ci in range(r)]
        side = lambda t, cols: jnp.concatenate([t[rows, cols] for rows in chunks], axis=1)
        for g in range(SG_GROUPS):
            cols = slice(g * 128, (g + 1) * 128)
            wt = jnp.where(tril, sgw_ref[g], 0.0).astype(BF16)
            vb = side(vnb, cols)
            dy_blk = side(dya, cols)
            du_g = dy_blk * (_dot(wt, vb) + sgb_ref[:, g:g + 1])
            dmix = dy_blk * side(u, cols)
            dmb = dmix.astype(BF16)
            dvn_g = _dot_tn(wt, dmb)
            dsgw_ref[g] += jnp.where(tril, _dot_nt(dmb, vb), 0.0)
            dsgb_ref[...] += jnp.where(lane128 == g, jnp.sum(dmix, axis=1, keepdims=True), 0.0)
            for ci, rows in enumerate(chunks):
                scr_ref[rows, cols] = du_g[:, ci * 128:(ci + 1) * 128]
                scr2_ref[rows, cols] = dvn_g[:, ci * 128:(ci + 1) * 128]
        du, dvn = scr_ref[0:TB, :], scr2_ref[0:TB, :]
        dlg_ref[...] += jnp.sum(dvn * vh, axis=0, keepdims=True)
        dlb_ref[...] += jnp.sum(dvn, axis=0, keepdims=True)
        dvv = _ln_bwd(dvn, vh, rs, lg)
        gg = _gelu_grad(za)
        dz_ref[:, C_ZA:C_ZA + HALF] = (du * gg[:, :HALF]).astype(BF16)
        dz_ref[:, C_ZA + HALF:C_ZA + 2 * HALF] = (dvv * gg[:, HALF:]).astype(BF16)

        RB = TB + CV_PAD

        def colsB(c0, c1):
            return jnp.concatenate([zp_ref[HALO - CV_PAD:, c0:c1].astype(F32) * pm, zc_ref[:, c0:c1].astype(F32),
                                    zn_ref[:CV_PAD, c0:c1].astype(F32)], axis=0)

        sh_ref[0] = colsB(C_ZB, C_ZB + HALF) * _sig(colsB(C_ZB + HALF, C_ZB + 2 * HALF))
        _sublane_shifts(sh_ref, RB + CV_PAD)
        c = jnp.broadcast_to(cvb_ref[...], (RB, HALF))
        for k in range(CV_KERNEL):
            c = c + cvw_ref[k:k + 1, :] * _tap(sh_ref, CV_PAD - (CV_KERNEL - 1) + k, RB)
        cvg = cvg_ref[...]
        n, ch, rc = _ln_fwd(c, cvg, cvbb_ref[...])
        sn = _sig(n)
        dyb = jnp.concatenate([dyc_ref[:, HALF:2 * HALF].astype(F32), dyn_ref[:CV_PAD, HALF:2 * HALF].astype(F32) * nm], axis=0)
        dn = dyb * (sn + n * sn * (1.0 - sn))
        dno = dn[:TB]
        dcvg_ref[...] += jnp.sum(dno * ch[:TB], axis=0, keepdims=True)
        dcvbb_ref[...] += jnp.sum(dno, axis=0, keepdims=True)
        dc = _ln_bwd(dn, ch, rc, cvg)
        sh2_ref[0] = dc
        _sublane_shifts(sh2_ref, RB)
        dcvb_ref[...] += jnp.sum(dc[:TB], axis=0, keepdims=True)
        dy0 = None
        for k in range(CV_KERNEL):
            wk = cvw_ref[k:k + 1, :]
            t = wk * _tap(sh2_ref, CV_KERNEL - 1 - k, TB)
            dy0 = t if dy0 is None else dy0 + t
            dcvw_ref[k:k + 1, :] += jnp.sum(dc[:TB] * _tap(sh_ref, CV_PAD - (CV_KERNEL - 1) + k, TB), axis=0, keepdims=True)
        ab = zc_ref[:, C_ZB:C_ZB + HALF].astype(F32)
        sg = _sig(zc_ref[:, C_ZB + HALF:C_ZB + 2 * HALF].astype(F32))
        dz_ref[:, C_ZB:C_ZB + HALF] = (dy0 * sg).astype(BF16)
        dz_ref[:, C_ZB + HALF:C_ZB + 2 * HALF] = (dy0 * ab * sg * (1.0 - sg)).astype(BF16)

        zd = colsE(C_ZD + HALF, C_ZD + 3 * HALF)
        scr_ref[...] = zd[:, :HALF] * zd[:, HALF:]
        dcv = dyC(3 * HALF, 4 * HALF) * colsC(C_ZD, C_ZD + HALF)
        scr2_ref[...] = dcv
        cv = None
        dud = None
        for k in range(SC_KERNEL):
            wk = scw_ref[k:k + 1, :]
            us = scr_ref[pl.ds(HALO - (SC_KERNEL - 1) + k, TB), :]
            t = wk * us
            cv = t if cv is None else cv + t
            t2 = wk * scr2_ref[pl.ds(SC_KERNEL - 1 - k, TB), :]
            dud = t2 if dud is None else dud + t2
            dscw_ref[k:k + 1, :] += jnp.sum(dcv[:TB] * us, axis=0, keepdims=True)
        dz_ref[:, C_ZD:C_ZD + HALF] = (dyc_ref[:, 3 * HALF:4 * HALF].astype(F32) * cv).astype(BF16)
        dz_ref[:, C_ZD + HALF:C_ZD + 2 * HALF] = (dud * zc_ref[:, C_ZD + 2 * HALF:C_ZD + 3 * HALF].astype(F32)).astype(BF16)
        dz_ref[:, C_ZD + 2 * HALF:C_ZD + 3 * HALF] = (dud * zc_ref[:, C_ZD + HALF:C_ZD + 2 * HALF].astype(F32)).astype(BF16)

        cosE = jnp.concatenate([cp_ref[...], cc_ref[...], cn_ref[...]], axis=0)
        sinE = jnp.concatenate([sp_ref[...], sc_ref[...], sn_ref[...]], axis=0)
        k_ref[...] = _rope(colsE(C_K, C_K + 128), cosE, sinE).astype(BF16)
        v_ref[...] = colsE(C_V, C_V + 128).astype(BF16)
        dk_ref[...] = jnp.zeros_like(dk_ref)
        dv_ref[...] = jnp.zeros_like(dv_ref)
        q = jnp.concatenate([_rope(colsC(C_Q + 128 * j, C_Q + 128 * (j + 1)), cosE[HALO:], sinE[HALO:])
                             for j in range(4)], axis=1).astype(BF16)
        dO = dyC(2 * HALF, 3 * HALF).astype(BF16)
        lane_s = lax.broadcasted_iota(jnp.int32, (1, 128), 1)
        in_band, in_band_cur = _band_masks()
        sink_cols = [_sink_col(sinks_ref, h) for h in range(N_KV_HEADS)]
        for qb in range(r + 1):
            valid = in_band if qb else in_band_cur | (in_band & (i > 0))
            rows = slice(qb * WINDOW, (qb + 1) * WINDOW)
            band = slice(qb * WINDOW, qb * WINDOW + 2 * WINDOW)
            for h in range(N_KV_HEADS):
                hc = slice(h * HEAD_DIM, (h + 1) * HEAD_DIM)
                kh = k_ref[band, hc]
                vh_ = v_ref[band, hc]
                heads = [slice((h * Q_PER_KV + g) * HEAD_DIM, (h * Q_PER_KV + g + 1) * HEAD_DIM) for g in range(Q_PER_KV)]
                qs = jnp.concatenate([q[rows, hs] for hs in heads], axis=0)
                dos = jnp.concatenate([dO[rows, hs] for hs in heads], axis=0)
                probs, p_sink = _attn_probs(qs, kh, sink_cols[h], valid)
                dP = _dot_nt(dos, vh_)
                rsum = jnp.sum(probs * dP, axis=-1, keepdims=True)
                dS = (probs * (dP - rsum) * (HEAD_DIM ** -0.5)).astype(BF16)
                dk_ref[band, hc] += _dot_tn(dS, qs)
                dv_ref[band, hc] += _dot_tn(probs.astype(BF16), dos)
                if qb < r:
                    dqs = _dot(dS, kh)
                    dsk = -p_sink * rsum
                    for g in range(Q_PER_KV):
                        dq_ref[rows, heads[g]] = dqs[g * WINDOW:(g + 1) * WINDOW]
                        dsink_ref[...] += jnp.where(lane_s == h * Q_PER_KV + g, jnp.sum(dsk[g * WINDOW:(g + 1) * WINDOW]), 0.0)
        cosC, sinC = cc_ref[...], sc_ref[...]
        for j in range(4):
            dz_ref[:, C_Q + 128 * j:C_Q + 128 * (j + 1)] = _rope_t(dq_ref[:, 128 * j:128 * (j + 1)], cosC, sinC).astype(BF16)
        dz_ref[:, C_K:C_K + 128] = _rope_t(dk_ref[HALO:HALO + TB, :], cosC, sinC).astype(BF16)
        dz_ref[:, C_V:C_V + 128] = dv_ref[HALO:HALO + TB, :].astype(BF16)

    small = [((1, HALF), F32), ((1, HALF), F32), ((SG_GROUPS, SG_CHUNK, SG_CHUNK), F32), ((SG_CHUNK, 128), F32),
             ((32, HALF), F32), ((1, HALF), F32), ((1, HALF), F32), ((1, HALF), F32), ((1, 128), F32), ((8, HALF), F32)]
    outs = pl.pallas_call(
        body, name=name, grid=(nb,),
        in_specs=[cur, prev, nxt, dcur, dnxt, tcur, tcur, tprev, tprev, tnxt, tnxt] + _mixer_param_specs()
                 + [pl.BlockSpec(memory_space=pl.ANY)],
        out_specs=[pl.BlockSpec((TB, MIX_W), lambda i: (i, 1))] + [_full(s) for s, _ in small],
        out_shape=[jax.ShapeDtypeStruct((T, PROJ_PAD), BF16)] + [jax.ShapeDtypeStruct(s, d) for s, d in small],
        scratch_shapes=[pltpu.VMEM((RE, HALF), F32), pltpu.VMEM((RC, HALF), F32), pltpu.VMEM((RE, 128), BF16), pltpu.VMEM((RE, 128), BF16),
                        pltpu.VMEM((RE, 128), F32), pltpu.VMEM((RE, 128), F32), pltpu.VMEM((TB, HALF), F32),
                        pltpu.VMEM((8, TB + 2 * CV_PAD, HALF), F32), pltpu.VMEM((8, TB + CV_PAD, HALF), F32)],
        input_output_aliases={21: 0},
        compiler_params=_params("arbitrary"),
    )(proj, proj, proj, dys, dys, cos_t, sin_t, cos_t, sin_t, cos_t, sin_t, *mp, dproj)
    return outs


def _rope_tables(T):
    pos = jnp.arange(T, dtype=F32)
    inv_freq = 1.0 / (ROPE_THETA ** (jnp.arange(0, HEAD_DIM, 2, dtype=F32) / HEAD_DIM))
    ang = pos[:, None] * inv_freq[None, :]
    cos, sin = jnp.cos(ang), jnp.sin(ang)
    cos_t = jnp.concatenate([cos, cos, cos, cos], axis=1)
    sin_t = jnp.concatenate([-sin, sin, -sin, sin], axis=1)
    return cos_t, sin_t


def _mixer_params(l, sg_ln_g, sg_ln_b, sg_w, sg_b, cv_w, cv_b, cv_ln_g, cv_ln_b, attn_sinks, sc_w):
    sgb_t = jnp.zeros((SG_CHUNK, 128), F32).at[:, :SG_GROUPS].set(sg_b[l].T)
    cvw = jnp.zeros((32, HALF), F32).at[:CV_KERNEL].set(cv_w[l])
    scw = jnp.zeros((8, HALF), F32).at[:SC_KERNEL].set(sc_w[l])
    sinks = jnp.zeros((1, 128), F32).at[0, :N_Q_HEADS].set(attn_sinks[l])
    return [sg_ln_g[l][None], sg_ln_b[l][None], sg_w[l], sgb_t, cvw, cv_b[l][None], cv_ln_g[l][None], cv_ln_b[l][None], sinks, scw]


def _w_in_layout(w_in_g):
    cut = MIX_W - 2 * W_IN_SHARD
    return jnp.concatenate([w_in_g[2][cut:], w_in_g[3], jnp.zeros((MIX_W - GATE_W, D_MODEL), w_in_g.dtype),
                            w_in_g[0], w_in_g[1], w_in_g[2][:cut]], axis=0)


def _w_in_unlayout(dw):
    cut = MIX_W - 2 * W_IN_SHARD
    return jnp.stack([dw[MIX_W:MIX_W + W_IN_SHARD], dw[MIX_W + W_IN_SHARD:MIX_W + 2 * W_IN_SHARD],
                      jnp.concatenate([dw[MIX_W + 2 * W_IN_SHARD:], dw[:W_IN_SHARD - cut]], axis=0),
                      dw[W_IN_SHARD - cut:GATE_W]], axis=0)


def _device_step(x, tgt, norm_mix, norm_ffn, norm_final, mixer_params, w_in_p, wb_g, wo_g, wgu_g, wd_g):
    T = x.shape[0]
    tables = _rope_tables(T)
    saved = []
    for l in range(DEPTH):
        lw = dict(w_in=w_in_p[l], w_branch=wb_g[l], w_out=wo_g[l], w_gate_up=wgu_g[l], w_down=wd_g[l],
                  norm_mix=norm_mix[l][None], norm_ffn=norm_ffn[l][None], mixer=mixer_params[l], after=jnp.zeros((8, 128), F32))
        x, sv = _fwd_layer(l, x, lw, tables)
        saved.append((lw, sv))
    dx, dnf, loss = _final_loss(x, norm_final[None], tgt, 256, "final_loss")
    grads = [None] * DEPTH
    for l in reversed(range(DEPTH)):
        lw, sv = saved[l]
        dxm, g_ffn = _bwd_layer_ffn(l, dx, lw, sv)
        dx, g_mix = _bwd_layer_mix(l, dxm, lw, sv, tables)
        raw = {**g_ffn, **g_mix}
        grads[l] = {**raw, **_small_views(raw)}
    return loss, dx, dnf[0], grads


MIX_BLOCK = 256


def _fwd_layer(l, x, lw, tables):
    return _fwd_layer_rest(l, x, _fwd_layer_mix(l, x, lw, tables), lw)


def _fwd_layer_mix(l, x, lw, tables, between=None):
    proj, xn = _rms_mm(x, lw["norm_mix"], lw["w_in"], min(x.shape[0], 1024), 2176, f"proj{l}")
    if between is not None:
        between(proj, lw)
    return proj, xn, _mixers_fwd(proj, *tables, lw["mixer"], MIX_BLOCK, f"mixers_fwd{l}")


def _fwd_layer_rest(l, x, mixed, lw, between=None):
    proj, xn, ys = mixed
    TM = min(x.shape[0], 1024)
    xm, merged = _merge_fwd(x, ys, proj, lw["w_branch"], lw["w_out"], min(x.shape[0], 512), f"merge_fwd{l}")
    if between is not None:
        between(xm, lw)
    gu, hn = _rms_mm(xm, lw["norm_ffn"], lw["w_gate_up"], TM, GU_SHARD, f"ffn_up{l}")
    x_out = _ffn_down(xm, gu, lw["w_down"], 256, f"ffn_down{l}")
    return x_out, (x, proj, xn, ys, xm, merged, gu, hn)


def _bwd_layer_ffn(l, dx, lw, sv, between=None):
    x_in, proj, xn, ys, xm, merged, gu, hn = sv
    T = dx.shape[0]
    tkk = min(T, 1024)
    gk = T // tkk
    dgu, act = _swiglu_bwd(dx, gu, lw["w_down"], 256, f"swiglu_bwd{l}", lw["after"])
    d_wd = _mm_tn(act, dx, (2, 1, gk), (tkk, D_FF // 2), lambda i, j, k: (k, i), (tkk, D_MODEL), lambda i, j, k: (k, 0),
                  (D_FF, D_MODEL), (D_FF // 2, D_MODEL), lambda i, j, k: (i, 0), f"dw_down{l}")
    d_wgu = _mm_tn(hn, dgu, (1, N_CHIPS, gk), (tkk, D_MODEL), lambda i, j, k: (k, 0), (tkk, GU_SHARD), lambda i, j, k: (k, j),
                   (N_CHIPS, D_MODEL, GU_SHARD), (None, D_MODEL, GU_SHARD), lambda i, j, k: (j, 0, 0), f"dw_gate_up{l}")
    if between is not None:
        between(dict(w_gate_up=d_wgu, w_down=d_wd), lw)
    dxm, d_nffn = _mm_nt_rmsbwd(dgu, lw["w_gate_up"], xm, lw["norm_ffn"], dx, min(T, 1024), GU_SHARD, f"ffn_up_bwd{l}")
    dys, dbr, dproj = _merge_bwd(dxm, ys, proj, lw["w_branch"], lw["w_out"], 256, f"merge_bwd{l}")
    d_wo = _mm_tn(merged, dxm, (2, 1, gk), (tkk, 512), lambda i, j, k: (k, i), (tkk, D_MODEL), lambda i, j, k: (k, 0),
                  (D_MODEL, D_MODEL), (512, D_MODEL), lambda i, j, k: (i, 0), f"dw_out{l}")
    d_wb = _mm_tn(ys, dbr, (N_BRANCH, 1, gk), (tkk, HALF), lambda i, j, k: (k, i), (tkk, D_MODEL), lambda i, j, k: (k, i),
                  (N_CHIPS, N_BRANCH, HALF, 256), (N_CHIPS, None, HALF, 256), lambda i, j, k: (0, i, 0, 0), f"dw_branch{l}", col_split=N_CHIPS)
    return (dxm, dys, dproj), dict(w_branch=d_wb, w_out=d_wo, w_gate_up=d_wgu, w_down=d_wd, norm_ffn=d_nffn)


def _bwd_layer_mix(l, carry, lw, sv, tables, between=None):
    dxm, dys, dproj = carry
    x_in, proj, xn, ys, xm, merged, gu, hn = sv
    T = dxm.shape[0]
    tkk = min(T, 1024)
    gk = T // tkk
    mb = _mixers_bwd(proj, dys, dproj, *tables, lw["mixer"], MIX_BLOCK, f"mixers_bwd{l}")
    dproj = mb[0]
    d_win = _mm_tn(dproj, xn, (PROJ_PAD // 2176, 1, gk), (tkk, 2176), lambda i, j, k: (k, i), (tkk, D_MODEL), lambda i, j, k: (k, 0),
                   (PROJ_PAD, D_MODEL), (2176, D_MODEL), lambda i, j, k: (i, 0), f"dw_in{l}")
    if between is not None:
        between(d_win, lw)
    dx, d_nmix = _mm_nt_rmsbwd(dproj, lw["w_in"], x_in, lw["norm_mix"], dxm, min(T, 1024), 2176, f"proj_bwd{l}")
    return dx, dict(w_in=d_win, norm_mix=d_nmix, sg_ln_g=mb[1], sg_ln_b=mb[2], sg_w=mb[3], sg_b=mb[4], cv_w=mb[5], cv_b=mb[6],
                    cv_ln_g=mb[7], cv_ln_b=mb[8], attn_sinks=mb[9], sc_w=mb[10])


ANY = pl.BlockSpec(memory_space=pl.ANY)
BIG = ("w_in", "w_branch", "w_out", "w_gate_up", "w_down")
HALF_SHAPE = {"w_in": (2, W_IN_SHARD // 2, D_MODEL), "w_branch": (2, 1024, 256), "w_out": (2, 128, D_MODEL),
              "w_gate_up": (2, 512, GU_SHARD), "w_down": (2, 352, D_MODEL)}
NB = len(BIG)


def _place():
    x, y, c = lax.axis_index("x"), lax.axis_index("y"), lax.axis_index("c")
    chips = [(1 - x, y), (x, 1 - y), (1 - x, 1 - y)]
    return x, y, c, 2 * x + y, chips, [2 * px + py for px, py in chips]


def _remote(src, dst, ssem, rsem, dev):
    return pltpu.make_async_remote_copy(src_ref=src, dst_ref=dst, send_sem=ssem, recv_sem=rsem, device_id=dev, device_id_type=MESH)


HBM_SPEC = pl.BlockSpec(memory_space=pltpu.HBM)
SEM_SPEC = pl.BlockSpec(memory_space=pltpu.SEMAPHORE)
DATAFLOW = pltpu.SideEffectType.DATAFLOW_SIDE_EFFECTING


def _ici_ends(kind, src, land, j, c, chip, chip_ids):
    if kind == "gather":
        return src.at[c], land.at[chip, c], land.at[chip_ids[j], c]
    return src.at[chip_ids[j]], land.at[chip], land.at[chip_ids[j]]


def _ici_start(kind, srcs, land_shapes, name):
    n = len(srcs)

    def body(*refs):
        src, land = refs[:n], refs[n:2 * n]
        ssem, rsem, token = refs[2 * n], refs[2 * n + 1], refs[-1]
        x, y, c, chip, chips, chip_ids = _place()
        for k in range(n):
            for j in range(3):
                s, d, _ = _ici_ends(kind, src[k], land[k], j, c, chip, chip_ids)
                _remote(s, d, ssem.at[3 * k + j], rsem.at[3 * k + j], (*chips[j], c)).start()
        token[...] = jnp.zeros_like(token)

    sem = pltpu.SemaphoreType.DMA((3 * n,))
    outs = pl.pallas_call(
        body, name=name,
        out_shape=(sem, sem, *[pltpu.HBM(s.shape, s.dtype) for s in srcs], *[pltpu.HBM(sh, BF16) for sh in land_shapes],
                   jax.ShapeDtypeStruct((8, 128), F32)),
        in_specs=[HBM_SPEC] * (2 * n),
        out_specs=(SEM_SPEC, SEM_SPEC, *[HBM_SPEC] * (2 * n), pl.BlockSpec(memory_space=pltpu.VMEM)),
        input_output_aliases={i: 2 + i for i in range(2 * n)},
        compiler_params=pltpu.CompilerParams(has_side_effects=DATAFLOW),
    )(*[pltpu.with_memory_space_constraint(s, pltpu.HBM) for s in srcs],
      *[pltpu.with_memory_space_constraint(lax.empty(sh, BF16), pltpu.HBM) for sh in land_shapes])
    return (kind, outs[0], outs[1], list(outs[2:2 + n]), list(outs[2 + n:2 + 2 * n])), outs[-1]


def _ici_wait(handle, after, name):
    kind, ssem_in, rsem_in, srcs, lands = handle
    n = len(srcs)

    def body(*refs):
        src, land = refs[:n], refs[n:2 * n]
        ssem, rsem = refs[2 * n], refs[2 * n + 1]
        x, y, c, chip, chips, chip_ids = _place()
        for k in range(n):
            for j in range(3):
                s, _, mine = _ici_ends(kind, src[k], land[k], j, c, chip, chip_ids)
                cp = _remote(s, mine, ssem.at[3 * k + j], rsem.at[3 * k + j], (*chips[j], c))
                cp.wait_send()
                cp.wait_recv()

    outs = pl.pallas_call(
        body, name=name, out_shape=[pltpu.HBM(t.shape, t.dtype) for t in srcs + lands],
        in_specs=[HBM_SPEC] * (2 * n) + [SEM_SPEC, SEM_SPEC, ANY], out_specs=[HBM_SPEC] * (2 * n),
        input_output_aliases={i: i for i in range(2 * n)},
        compiler_params=pltpu.CompilerParams(has_side_effects=DATAFLOW),
    )(*srcs, *lands, ssem_in, rsem_in, after)
    return list(outs[:n]), list(outs[n:])


def _ag_pair(shards, lands, name):
    n = len(shards)

    def body(*refs):
        ins, outs = refs[:n], refs[2 * n:3 * n]
        token = refs[3 * n]
        s_fwd, r_fwd, s_own, r_own = refs[3 * n + 1:]
        x, y, c, chip, chips, chip_ids = _place()
        sib = (x, y, 1 - c)
        cps = []
        for k in range(n):
            cp = _remote(ins[k], outs[k].at[chip], s_own.at[k], r_own.at[k], sib)
            cp.start()
            cps.append(cp)
            for j in range(3):
                got = outs[k].at[chip_ids[j], c]
                cp = _remote(got, got, s_fwd.at[k, j], r_fwd.at[k, j], sib)
                cp.start()
                cps.append(cp)
        for k in range(n):
            _remote(ins[k], outs[k].at[chip], s_own.at[k], r_own.at[k], sib).wait_recv()
            for j in range(3):
                got = outs[k].at[chip_ids[j], 1 - c]
                _remote(got, got, s_fwd.at[k, j], r_fwd.at[k, j], sib).wait_recv()
        for cp in cps:
            cp.wait_send()
        token[...] = jnp.zeros_like(token)

    sem, sem1 = pltpu.SemaphoreType.DMA((n, 3)), pltpu.SemaphoreType.DMA((n,))
    outs = pl.pallas_call(
        body, name=name, out_shape=[jax.ShapeDtypeStruct(t.shape, t.dtype) for t in lands] + [jax.ShapeDtypeStruct((8, 128), F32)],
        in_specs=[ANY] * (2 * n), out_specs=[ANY] * n + [pl.BlockSpec(memory_space=pltpu.VMEM)],
        input_output_aliases={n + k: k for k in range(n)},
        scratch_shapes=[sem, sem, sem1, sem1], compiler_params=pltpu.CompilerParams(has_side_effects=True),
    )(*shards, *lands)
    return list(outs[:n]), outs[n]


def _forward_plan(n):
    def plan(refs, c, chip, chip_ids):
        out = []
        for k in range(n):
            shard, land = refs[k], refs[n + k]
            out.append((shard, land.at[chip], land.at[chip]))
            out += [(land.at[q, c], land.at[q, c], land.at[q, 1 - c]) for q in chip_ids]
        return out
    return plan, 4 * n


def _swap_plan(n):
    def plan(refs, c, chip, chip_ids):
        return [(refs[k].at[q, 1 - c], refs[n + k].at[q], refs[n + k].at[q]) for k in range(n) for q in range(N_CHIPS)]
    return plan, N_CHIPS * n


def _d2d_start(arrays, new_shapes, plan_n, name):
    plan, n_copies = plan_n
    n = len(arrays) + len(new_shapes)

    def body(*refs):
        ssem, rsem, token = refs[n], refs[n + 1], refs[-1]
        x, y, c, chip, _, chip_ids = _place()
        for i, (s, d, _) in enumerate(plan(refs[:n], c, chip, chip_ids)):
            _remote(s, d, ssem.at[i], rsem.at[i], (x, y, 1 - c)).start()
        token[...] = jnp.zeros_like(token)

    sem = pltpu.SemaphoreType.DMA((n_copies,))
    args = [pltpu.with_memory_space_constraint(t, pltpu.HBM) for t in arrays] + \
           [pltpu.with_memory_space_constraint(lax.empty(sh, BF16), pltpu.HBM) for sh in new_shapes]
    outs = pl.pallas_call(
        body, name=name,
        out_shape=(sem, sem, *[pltpu.HBM(t.shape, t.dtype) for t in args], jax.ShapeDtypeStruct((8, 128), F32)),
        in_specs=[HBM_SPEC] * n, out_specs=(SEM_SPEC, SEM_SPEC, *[HBM_SPEC] * n, pl.BlockSpec(memory_space=pltpu.VMEM)),
        input_output_aliases={i: 2 + i for i in range(n)},
        compiler_params=pltpu.CompilerParams(has_side_effects=DATAFLOW),
    )(*args)
    return (plan, outs[0], outs[1], list(outs[2:2 + n])), outs[-1]


def _d2d_wait(handle, after, name):
    plan, ssem_in, rsem_in, arrays = handle
    n = len(arrays)

    def body(*refs):
        ssem, rsem = refs[n], refs[n + 1]
        x, y, c, chip, _, chip_ids = _place()
        for i, (s, _, mine) in enumerate(plan(refs[:n], c, chip, chip_ids)):
            cp = _remote(s, mine, ssem.at[i], rsem.at[i], (x, y, 1 - c))
            cp.wait_send()
            cp.wait_recv()

    outs = pl.pallas_call(
        body, name=name, out_shape=[pltpu.HBM(t.shape, t.dtype) for t in arrays],
        in_specs=[HBM_SPEC] * n + [SEM_SPEC, SEM_SPEC, ANY], out_specs=[HBM_SPEC] * n,
        input_output_aliases={i: i for i in range(n)},
        compiler_params=pltpu.CompilerParams(has_side_effects=DATAFLOW),
    )(*arrays, ssem_in, rsem_in, after)
    return list(outs)


def _rs_pair(grads, name):
    n_arr = len(grads)

    def body(*refs):
        ins, got = refs[:n_arr], refs[n_arr:2 * n_arr]
        ssem, rsem = refs[2 * n_arr:]
        x, y, c, _, _, _ = _place()
        sib = (x, y, 1 - c)
        sends = []
        for k in reversed(range(n_arr)):
            for q in range(N_CHIPS):
                cp = _remote(ins[k].at[q, 1 - c], got[k].at[q], ssem.at[k, q], rsem.at[k, q], sib)
                cp.start()
                sends.append(cp)
        for k in range(n_arr):
            for q in range(N_CHIPS):
                _remote(got[k].at[q], got[k].at[q], ssem.at[k, q], rsem.at[k, q], sib).wait_recv()
        for cp in sends:
            cp.wait_send()

    shp = [jax.ShapeDtypeStruct((N_CHIPS,) + g.shape[2:], BF16) for g in grads]
    sem = pltpu.SemaphoreType.DMA((n_arr, N_CHIPS))
    outs = pl.pallas_call(
        body, name=name, out_shape=shp, in_specs=[ANY] * n_arr, out_specs=[ANY] * n_arr,
        scratch_shapes=[sem, sem], compiler_params=pltpu.CompilerParams(has_side_effects=True),
    )(*grads)
    return list(outs)


def _rs_share(bufs, name):
    n = len(bufs)

    def body(*refs):
        outs = refs[n:2 * n]
        ssem, rsem = refs[2 * n:]
        x, y, c, _, _, _ = _place()
        sib = (x, y, 1 - c)
        sends = []
        for k in range(n):
            for l in range(DEPTH):
                cp = _remote(outs[k].at[l, c], outs[k].at[l, c], ssem.at[k, l], rsem.at[k, l], sib)
                cp.start()
                sends.append(cp)
        for k in range(n):
            for l in range(DEPTH):
                dst = outs[k].at[l, 1 - c]
                _remote(dst, dst, ssem.at[k, l], rsem.at[k, l], sib).wait_recv()
        for cp in sends:
            cp.wait_send()

    sem = pltpu.SemaphoreType.DMA((n, DEPTH))
    outs = pl.pallas_call(
        body, name=name, out_shape=[jax.ShapeDtypeStruct(b.shape, b.dtype) for b in bufs], in_specs=[ANY] * n, out_specs=[ANY] * n,
        input_output_aliases={k: k for k in range(n)},
        scratch_shapes=[sem, sem], compiler_params=pltpu.CompilerParams(has_side_effects=True),
    )(*bufs)
    return list(outs)


def _piece(src, idx, rows, width=128, align=1, transposed=False):
    return dict(src=src, idx=idx, rows=rows, width=width, align=align, transposed=transposed)


def _all_reduce_pieces(inputs, pieces, out_shapes, writes, name):
    n_in, n_out = len(inputs), len(out_shapes)
    offs, R = [], 0
    for p in pieces:
        R = -(-R // p["align"]) * p["align"]
        offs.append(R)
        R += p["rows"]
    R = -(-R // 8) * 8

    def body(*refs):
        ins, outs, token_ref = refs[:n_in], refs[n_in:n_in + n_out], refs[n_in + n_out]
        pair_ref, chip_ref, sum_ref, ssem, rsem = refs[n_in + n_out + 1:]
        token_ref[...] = jnp.zeros_like(token_ref)
        x, y, c, chip, chips, chip_ids = _place()
        pair_ref[c] = jnp.zeros((R, 128), F32)
        for p, off in zip(pieces, offs):
            v = ins[p["src"]][...].T[p["idx"]] if p["transposed"] else ins[p["src"]][p["idx"]]
            pair_ref[c, off:off + p["rows"], 0:p["width"]] = v
        mine = _remote(pair_ref.at[c], pair_ref.at[c], ssem.at[3], rsem.at[3], (x, y, 1 - c))
        mine.start()
        _remote(pair_ref.at[1 - c], pair_ref.at[1 - c], ssem.at[3], rsem.at[3], (x, y, 1 - c)).wait_recv()
        chip_ref[chip] = pair_ref[0] + pair_ref[1]
        cps = [_remote(chip_ref.at[chip], chip_ref.at[chip], ssem.at[j], rsem.at[j], (*chips[j], c)) for j in range(3)]
        for cp in cps:
            cp.start()
        for j in range(3):
            slot = chip_ref.at[chip_ids[j]]
            _remote(slot, slot, ssem.at[j], rsem.at[j], (*chips[j], c)).wait_recv()
        acc = chip_ref[0]
        for s in range(1, N_CHIPS):
            acc = acc + chip_ref[s]
        sum_ref[...] = acc
        for o, idx, p in writes:
            outs[o][idx] = sum_ref[offs[p]:offs[p] + pieces[p]["rows"], 0:pieces[p]["width"]]
        for cp in cps + [mine]:
            cp.wait_send()

    vm = pl.BlockSpec(memory_space=pltpu.VMEM)
    outs = pl.pallas_call(
        body, name=name, out_shape=[jax.ShapeDtypeStruct(s, F32) for s in out_shapes] + [jax.ShapeDtypeStruct((8, 128), F32)],
        in_specs=[vm] * n_in, out_specs=[vm] * (n_out + 1),
        scratch_shapes=[pltpu.VMEM((2, R, 128), F32), pltpu.VMEM((N_CHIPS, R, 128), F32), pltpu.VMEM((R, 128), F32),
                        pltpu.SemaphoreType.DMA((4,)), pltpu.SemaphoreType.DMA((4,))],
        compiler_params=pltpu.CompilerParams(vmem_limit_bytes=VMEM_LIMIT),
    )(*inputs)
    return list(outs[:n_out]), outs[n_out]


def _lanes(width):
    return [slice(k, min(k + 128, width)) for k in range(0, width, 128)]


def _gather_small_weights(cvw_z, scw_z):
    pieces, writes = [], []
    for i, arr in enumerate((cvw_z, scw_z)):
        for l in range(DEPTH):
            for ln in _lanes(HALF):
                writes.append((i, (l, slice(None), ln), len(pieces)))
                pieces.append(_piece(i, (l, slice(None), ln), arr.shape[1], align=8))
    (cvw, scw), tok = _all_reduce_pieces([cvw_z, scw_z], pieces, [cvw_z.shape, scw_z.shape], writes, "ag_small")
    return cvw, scw, tok


SMALL_RAW = dict(norm_mix=(1, D_MODEL), norm_ffn=(1, D_MODEL), sg_ln_g=(1, HALF), sg_ln_b=(1, HALF), cv_b=(1, HALF), cv_ln_g=(1, HALF),
                 cv_ln_b=(1, HALF))


def _all_reduce_small_grads(raw, d_nfinal, loss):
    names = list(SMALL_RAW) + ["attn_sinks", "sg_b", "sc_w", "cv_w", "sg_w"]
    out_shape = dict(norm_mix=(DEPTH, D_MODEL), norm_ffn=(DEPTH, D_MODEL), sg_ln_g=(DEPTH, HALF), sg_ln_b=(DEPTH, HALF), cv_b=(DEPTH, HALF),
                     cv_ln_g=(DEPTH, HALF), cv_ln_b=(DEPTH, HALF), attn_sinks=(DEPTH, N_Q_HEADS), sg_b=(DEPTH, SG_GROUPS, SG_CHUNK),
                     sc_w=(DEPTH, SC_KERNEL, HALF), cv_w=(DEPTH, CV_KERNEL, HALF), sg_w=(DEPTH, SG_GROUPS, SG_CHUNK, SG_CHUNK))
    inputs, pieces, writes = [], [], []

    def add(src, idx, rows, out, out_idx, **kw):
        writes.append((names.index(out) if out in names else out, out_idx, len(pieces)))
        pieces.append(_piece(src, idx, rows, **kw))

    for l in range(DEPTH):
        row = slice(l, l + 1)
        for n, (_, width) in SMALL_RAW.items():
            inputs.append(raw[l][n])
            for ln in _lanes(width):
                add(len(inputs) - 1, (slice(0, 1), ln), 1, n, (row, ln))
        inputs.append(raw[l]["attn_sinks"])
        add(len(inputs) - 1, (slice(0, 1), slice(0, N_Q_HEADS)), 1, "attn_sinks", (row, slice(None)), width=N_Q_HEADS)
    for l in range(DEPTH):
        inputs.append(raw[l]["sg_b"])
        add(len(inputs) - 1, (slice(0, SG_GROUPS), slice(None)), SG_GROUPS, "sg_b", (l,), align=8, transposed=True)
        inputs.append(raw[l]["sc_w"])
        for ln in _lanes(HALF):
            add(len(inputs) - 1, (slice(0, SC_KERNEL), ln), SC_KERNEL, "sc_w", (l, slice(None), ln), align=8)
        inputs.append(raw[l]["cv_w"])
        for ln in _lanes(HALF):
            add(len(inputs) - 1, (slice(0, CV_KERNEL), ln), CV_KERNEL, "cv_w", (l, slice(None), ln), align=8)
        inputs.append(raw[l]["sg_w"])
        for g in range(SG_GROUPS):
            add(len(inputs) - 1, (g,), SG_CHUNK, "sg_w", (l, g), align=8)
    n_names = len(names)
    inputs.append(d_nfinal)
    for ln in _lanes(D_MODEL):
        add(len(inputs) - 1, (slice(0, 1), ln), 1, n_names, (slice(0, 1), ln))
    inputs.append(loss)
    add(len(inputs) - 1, (slice(0, 1), slice(None)), 1, n_names + 1, (slice(0, 1), slice(None)))
    outs, tok = _all_reduce_pieces(inputs, pieces, [out_shape[n] for n in names] + [(1, D_MODEL), (1, 128)], writes, "ar_small")
    return dict(zip(names, outs[:n_names])), outs[n_names], outs[n_names + 1], tok


def _small_views(raw):
    v = {n: raw[n][0] for n in SMALL_RAW}
    v.update(sg_w=raw["sg_w"], sg_b=raw["sg_b"][:, :SG_GROUPS].T, cv_w=raw["cv_w"][:CV_KERNEL],
             attn_sinks=raw["attn_sinks"][0, :N_Q_HEADS], sc_w=raw["sc_w"][:SC_KERNEL])
    return v


def _row_tile(rows, cols, n_arrays):
    budget = 20 * 1024 * 1024 // (n_arrays * 2 * cols * 4)
    tiles = [t for t in range(16, min(rows, budget) + 1, 16) if rows % t == 0]
    assert tiles, (rows, cols)
    return tiles[-1]


def _add_pairs(g, got, place, name):
    _, _, rows, cols = g.shape
    tr = _row_tile(rows, cols, 3)

    def body(place_ref, a_ref, b_ref, o_ref):
        del place_ref
        o_ref[...] = (a_ref[...].astype(F32) + b_ref[...].astype(F32)).astype(BF16)

    spec = pl.BlockSpec((None, tr, cols), lambda q, i, p: (q, i, 0))
    grid_spec = pltpu.PrefetchScalarGridSpec(
        num_scalar_prefetch=1, grid=(N_CHIPS, rows // tr),
        in_specs=[pl.BlockSpec((None, None, tr, cols), lambda q, i, p: (q, p[1], i, 0)), spec], out_specs=spec)
    return pl.pallas_call(body, name=name, grid_spec=grid_spec, out_shape=jax.ShapeDtypeStruct((N_CHIPS, rows, cols), BF16),
                          compiler_params=_params("parallel", "parallel"))(place, g, got)


def _sum_chips(own, recv, place, l, buf, name, after):
    _, rows, cols = own.shape
    tr = _row_tile(rows, cols, 4)

    def body(place_ref, own_ref, recv_ref, *rest):
        chip = place_ref[0]
        acc = own_ref[...].astype(F32)
        for j in range(1, N_CHIPS):
            acc = acc + recv_ref[lax.rem(chip + j, N_CHIPS)].astype(F32)
        rest[-1][...] = acc

    in_specs = [pl.BlockSpec((None, tr, cols), lambda i, p: (p[0], i, 0)), pl.BlockSpec((N_CHIPS, tr, cols), lambda i, p: (0, i, 0)), ANY]
    args = [place, own, recv, after]
    aliases = {}
    if buf is not None:
        in_specs.append(ANY)
        args.append(buf)
        aliases = {4: 0}
    grid_spec = pltpu.PrefetchScalarGridSpec(
        num_scalar_prefetch=1, grid=(rows // tr,), in_specs=in_specs,
        out_specs=pl.BlockSpec((None, None, tr, cols), lambda i, p: (l, p[1], i, 0)))
    return pl.pallas_call(body, name=name, grid_spec=grid_spec, out_shape=jax.ShapeDtypeStruct((DEPTH, 2, rows, cols), F32),
                          input_output_aliases=aliases, compiler_params=_params("parallel"))(*args)


def _adamw(w, g, m, v, name):
    shape = w.shape
    lead, (rows, cols) = shape[:-2], shape[-2:]
    tr = _row_tile(rows, cols, 8)

    def body(w_ref, g_ref, m_ref, v_ref, go_ref, d_ref, mo_ref, vo_ref):
        gv = g_ref[...]
        go_ref[...] = gv
        mn = ADAM_B1 * m_ref[...] + (1.0 - ADAM_B1) * gv
        vn = ADAM_B2 * v_ref[...] + (1.0 - ADAM_B2) * (gv * gv)
        m_hat = mn / (1.0 - ADAM_B1 ** ADAM_STEP)
        v_hat = vn / (1.0 - ADAM_B2 ** ADAM_STEP)
        d_ref[...] = -ADAM_LR * (m_hat / (jnp.sqrt(v_hat) + ADAM_EPS) + ADAM_WD * w_ref[...])
        mo_ref[...] = mn
        vo_ref[...] = vn

    spec = pl.BlockSpec((None,) * len(lead) + (tr, cols), lambda *idx: (*idx, 0))
    grid = lead + (rows // tr,)
    return list(pl.pallas_call(body, name=name, grid=grid, in_specs=[spec] * 4, out_specs=[spec] * 4,
                               out_shape=[jax.ShapeDtypeStruct(shape, F32)] * 4,
                               compiler_params=_params(*(["parallel"] * len(grid))))(w, g, m, v))


def _adamw_small(ws, gs, ms, vs, name):
    n = len(ws)

    def body(*refs):
        for i in range(n):
            gv = refs[n + i][...]
            mn = ADAM_B1 * refs[2 * n + i][...] + (1.0 - ADAM_B1) * gv
            vn = ADAM_B2 * refs[3 * n + i][...] + (1.0 - ADAM_B2) * (gv * gv)
            m_hat = mn / (1.0 - ADAM_B1 ** ADAM_STEP)
            v_hat = vn / (1.0 - ADAM_B2 ** ADAM_STEP)
            refs[4 * n + i][...] = -ADAM_LR * (m_hat / (jnp.sqrt(v_hat) + ADAM_EPS) + ADAM_WD * refs[i][...])
            refs[5 * n + i][...] = mn
            refs[6 * n + i][...] = vn

    vm = pl.BlockSpec(memory_space=pltpu.VMEM)
    outs = pl.pallas_call(body, name=name, out_shape=[jax.ShapeDtypeStruct(t.shape, F32) for t in ws] * 3,
                          in_specs=[vm] * (4 * n), out_specs=[vm] * (3 * n),
                          compiler_params=pltpu.CompilerParams(vmem_limit_bytes=VMEM_LIMIT))(*ws, *gs, *ms, *vs)
    return outs[:n], outs[n:2 * n], outs[2 * n:]


SMALL = ("norm_mix", "sg_ln_g", "sg_ln_b", "sg_w", "sg_b", "cv_w", "cv_b", "cv_ln_g", "cv_ln_b", "attn_sinks", "sc_w", "norm_ffn", "norm_final")
ORDER = ("norm_mix", "w_in", "sg_ln_g", "sg_ln_b", "sg_w", "sg_b", "cv_w", "cv_b", "cv_ln_g", "cv_ln_b", "attn_sinks", "sc_w",
         "w_branch", "w_out", "norm_ffn", "w_gate_up", "w_down", "norm_final")


def kernel(x, norm_mix, w_in, sg_ln_g, sg_ln_b, sg_w, sg_b, cv_w, cv_b, cv_ln_g, cv_ln_b, attn_sinks, sc_w, w_branch, w_out, norm_ffn, w_gate_up, w_down, norm_final, loss_target, m_norm_mix, m_w_in, m_sg_ln_g, m_sg_ln_b, m_sg_w, m_sg_b, m_cv_w, m_cv_b, m_cv_ln_g, m_cv_ln_b, m_attn_sinks, m_sc_w, m_w_branch, m_w_out, m_norm_ffn, m_w_gate_up, m_w_down, m_norm_final, v_norm_mix, v_w_in, v_sg_ln_g, v_sg_ln_b, v_sg_w, v_sg_b, v_cv_w, v_cv_b, v_cv_ln_g, v_cv_ln_b, v_attn_sinks, v_sc_w, v_w_branch, v_w_out, v_norm_ffn, v_w_gate_up, v_w_down, v_norm_final):
    W = dict(norm_mix=norm_mix, w_in=w_in, sg_ln_g=sg_ln_g, sg_ln_b=sg_ln_b, sg_w=sg_w, sg_b=sg_b, cv_w=cv_w, cv_b=cv_b, cv_ln_g=cv_ln_g,
             cv_ln_b=cv_ln_b, attn_sinks=attn_sinks, sc_w=sc_w, w_branch=w_branch, w_out=w_out, norm_ffn=norm_ffn, w_gate_up=w_gate_up,
             w_down=w_down, norm_final=norm_final)
    M = dict(norm_mix=m_norm_mix, w_in=m_w_in, sg_ln_g=m_sg_ln_g, sg_ln_b=m_sg_ln_b, sg_w=m_sg_w, sg_b=m_sg_b, cv_w=m_cv_w, cv_b=m_cv_b,
             cv_ln_g=m_cv_ln_g, cv_ln_b=m_cv_ln_b, attn_sinks=m_attn_sinks, sc_w=m_sc_w, w_branch=m_w_branch, w_out=m_w_out,
             norm_ffn=m_norm_ffn, w_gate_up=m_w_gate_up, w_down=m_w_down, norm_final=m_norm_final)
    V = dict(norm_mix=v_norm_mix, w_in=v_w_in, sg_ln_g=v_sg_ln_g, sg_ln_b=v_sg_ln_b, sg_w=v_sg_w, sg_b=v_sg_b, cv_w=v_cv_w, cv_b=v_cv_b,
             cv_ln_g=v_cv_ln_g, cv_ln_b=v_cv_ln_b, attn_sinks=v_attn_sinks, sc_w=v_sc_w, w_branch=v_w_branch, w_out=v_w_out,
             norm_ffn=v_norm_ffn, w_gate_up=v_w_gate_up, w_down=v_w_down, norm_final=v_norm_final)
    mx, my, mc = lax.axis_index("x"), lax.axis_index("y"), lax.axis_index("c")
    chip = 2 * mx + my

    place = jnp.stack([chip, mc]).astype(jnp.int32)
    tables = _rope_tables(x.shape[1])
    land_shapes = [(N_CHIPS,) + HALF_SHAPE[n] for n in BIG]
    part_shapes = {n: (N_CHIPS,) + HALF_SHAPE[n][1:] for n in BIG}

    T_ = lambda t: jnp.swapaxes(t, 1, 2)
    Wt, Mt, Vt = ({**t, "w_in": T_(t["w_in"])} for t in (W, M, V))

    def shards_of(l, tok):
        return [(Wt[n][l] + tok[0, 0]).astype(BF16).reshape(HALF_SHAPE[n]) for n in BIG]

    def finish_gather(tag, handle, after):
        srcs, lands = _ici_wait(handle, after, f"ag_wait{tag}")
        return _ag_pair(srcs, lands, f"ag_pair{tag}")[0]

    def mix_weights(l, g_in):
        return dict(w_in=_w_in_layout(g_in[0].reshape(N_CHIPS, W_IN_SHARD, D_MODEL)), norm_mix=norm_mix[l][None], norm_ffn=norm_ffn[l][None],
                    mixer=_mixer_params(l, sg_ln_g, sg_ln_b, sg_w, sg_b, cvw_full, cv_b, cv_ln_g, cv_ln_b, attn_sinks, scw_full))

    def rest_weights(lw, g_rest):
        G = dict(zip(BIG[1:], g_rest))
        lw.update(w_branch=G["w_branch"].reshape(N_CHIPS, N_BRANCH, HALF, 256), w_out=G["w_out"].reshape(D_MODEL, D_MODEL),
                  w_gate_up=G["w_gate_up"].reshape(N_CHIPS, D_MODEL, GU_SHARD), w_down=G["w_down"].reshape(D_FF, D_MODEL))

    def shard_major(g):
        t = dict(g)
        if "w_in" in t:
            t["w_in"] = _w_in_unlayout(t["w_in"])
        return {n: t[n].reshape((N_CHIPS,) + HALF_SHAPE[n]) for n in BIG if n in t}

    zero_tok = jnp.zeros((8, 128), F32)
    south = (mc == 0).astype(F32)
    cvw_z = lax.dynamic_update_slice(jnp.zeros((DEPTH, CV_KERNEL, HALF), F32), cv_w * south, (0, 0, chip * 128))
    scw_z = lax.dynamic_update_slice(jnp.zeros((DEPTH, SC_KERNEL, HALF), F32), sc_w * south, (0, 0, chip * 128))
    cvw_full, scw_full, tok = _gather_small_weights(cvw_z, scw_z)
    handles = []
    for l in range(DEPTH):
        for tag, sl in (("in", slice(0, 1)), ("rest", slice(1, NB))):
            h, tok = _ici_start("gather", shards_of(l, tok)[sl], land_shapes[sl], f"ag_start{l}{tag}")
            handles.append(h)
    pending = {}

    def behind(l, key):
        def order(lw, token):
            if key == "mixer":
                lw["mixer"] = [lw["mixer"][0] + token[0, 0]] + lw["mixer"][1:]
            else:
                lw[key] = lw[key] + token[0, 0]
        return order

    def early_pair(tag, handle, order):
        def between(after, lw):
            srcs, lands = _ici_wait(handle, after, f"ag_wait{tag}")
            pending[tag], token = _d2d_start(srcs + lands, [], _forward_plan(len(srcs)), f"ag_pair_start{tag}")
            order(lw, token)
        return between

    def finish_pair(tag, after):
        arrays = _d2d_wait(pending.pop(tag), after, f"ag_pair_wait{tag}")
        return arrays[len(arrays) // 2:]

    lw0 = mix_weights(0, finish_gather("0in", handles[0], tok))
    mixed = _fwd_layer_mix(0, x[0], lw0, tables)
    rest_weights(lw0, finish_gather("0rest", handles[1], mixed[2]))
    x1, sv0 = _fwd_layer_rest(0, x[0], mixed, lw0, early_pair("1in", handles[2], behind(0, "norm_ffn")))
    lw1 = mix_weights(1, finish_pair("1in", x1))
    mixed = _fwd_layer_mix(1, x1, lw1, tables, early_pair("1rest", handles[3], behind(1, "mixer")))
    rest_weights(lw1, finish_pair("1rest", mixed[2]))
    x2, sv1 = _fwd_layer_rest(1, x1, mixed, lw1)
    dx, d_nfinal, loss = _final_loss(x2, norm_final[None], loss_target[0], 256, "final_loss")

    lw1["after"] = zero_tok
    carry, g_ffn1 = _bwd_layer_ffn(1, dx, lw1, sv1)
    g1 = shard_major(g_ffn1)
    names_f = list(g1)
    h_swap, tok = _d2d_start([g1[n] for n in names_f], [part_shapes[n] for n in names_f], _swap_plan(len(names_f)), "rs_pair_start1")
    behind(1, "mixer")(lw1, tok)
    def early_swap(tag):
        def between(d_win, lw):
            g = shard_major({"w_in": d_win})["w_in"]
            pending[tag], token = _d2d_start([g], [part_shapes["w_in"]], _swap_plan(1), f"rs_pair_start{tag}")
            behind(None, "norm_mix")(lw, token)
        return between

    dx, g_mix1 = _bwd_layer_mix(1, carry, lw1, sv1, tables, early_swap("1in"))
    swapped = _d2d_wait(h_swap, dx, "rs_pair_wait1")
    own_in, got_in = _d2d_wait(pending.pop("1in"), dx, "rs_pair_wait1in")
    names1 = ["w_in"] + names_f
    own1 = [own_in] + swapped[:len(names_f)]
    got1 = [got_in] + swapped[len(names_f):]
    part1 = [_add_pairs(own1[k], got1[k], place, f"rs_add1_{n}") for k, n in enumerate(names1)]
    hr1, tok = _ici_start("scatter", part1, [part_shapes[n] for n in names1], "rs_start1")

    def early_ffn_swap(grads, lw):
        g = shard_major(grads)
        pending["0ffn"], token = _d2d_start([g[n] for n in g], [part_shapes[n] for n in g], _swap_plan(len(g)), "rs_pair_start0ffn")
        behind(None, "norm_ffn")(lw, token)

    lw0["after"] = tok
    carry, g_ffn0 = _bwd_layer_ffn(0, dx, lw0, sv0, early_ffn_swap)
    g0 = shard_major({n: g_ffn0[n] for n in ("w_branch", "w_out")})
    names_a = list(g0) + ["w_gate_up", "w_down"]
    swapped = _d2d_wait(pending.pop("0ffn"), g_ffn0["w_branch"], "rs_pair_wait0ffn")
    own_a = [g0[n] for n in g0] + swapped[:2]
    got_a = _rs_pair([g0[n] for n in g0], "rs_pair0a") + swapped[2:]
    part_a = [_add_pairs(own_a[k], got_a[k], place, f"rs_add0a_{n}") for k, n in enumerate(names_a)]
    _, recv1 = _ici_wait(hr1, part_a[0], "rs_wait1")
    hra, tok = _ici_start("scatter", part_a, [part_shapes[n] for n in names_a], "rs_start0a")

    lw0["mixer"] = [lw0["mixer"][0] + tok[0, 0]] + lw0["mixer"][1:]
    dx, g_mix0 = _bwd_layer_mix(0, carry, lw0, sv0, tables, early_swap("0in"))
    _, recv_a = _ici_wait(hra, dx, "rs_wait0a")

    small_red, nf_red, loss_red, tok = _all_reduce_small_grads([{**g_ffn0, **g_mix0}, {**g_ffn1, **g_mix1}], d_nfinal, loss)
    small_red["norm_final"] = nf_red
    loss_out = loss_red[0, 0]
    for n in ("cv_w", "sc_w"):
        small_red[n] = lax.dynamic_slice_in_dim(small_red[n], chip * 128, 128, axis=2)

    own_in, got_in = _d2d_wait(pending.pop("0in"), tok, "rs_pair_wait0in")
    names_b, part_b = ["w_in"], [_add_pairs(own_in, got_in, place, "rs_add0b_w_in")]
    hrb, tok = _ici_start("scatter", part_b, [part_shapes[n] for n in names_b], "rs_start0b")
    bufs = {n: _sum_chips(part1[k], recv1[k], place, 1, None, f"rs_sum1_{n}", tok) for k, n in enumerate(names1)}
    for k, n in enumerate(names_a):
        bufs[n] = _sum_chips(part_a[k], recv_a[k], place, 0, bufs[n], f"rs_sum0_{n}", tok)
    shared = dict(zip(names_a, _rs_share([bufs[n] for n in names_a], "rs_share_a")))
    upd = {}
    for n in names_a:
        red = shared[n].reshape(W[n].shape)
        upd[n] = _adamw(W[n], red, M[n], V[n], f"adamw_{n}")
    two_d = lambda t: t[None] if t.ndim == 1 else t
    small_upd = _adamw_small(*([two_d(t[n]) for n in SMALL] for t in (W, small_red, M, V)), "adamw_small")
    for n, d, mo, vo in zip(SMALL, *small_upd):
        upd[n] = [t.reshape(W[n].shape) for t in (small_red[n], d, mo, vo)]

    _, recv_b = _ici_wait(hrb, upd[names_a[-1]][1], "rs_wait0b")
    for k, n in enumerate(names_b):
        bufs[n] = _sum_chips(part_b[k], recv_b[k], place, 0, bufs[n], f"rs_sum0_{n}", tok)
    shared = dict(zip(names_b, _rs_share([bufs[n] for n in names_b], "rs_share_b")))
    for n in names_b:
        red = shared[n].reshape(Wt[n].shape)
        upd[n] = [T_(t) for t in _adamw(Wt[n], red, Mt[n], Vt[n], f"adamw_{n}")]

    out = [loss_out, dx[None]]
    for k in range(4):
        out += [upd[n][k] for n in ORDER]
    return tuple(out)
```

```python
import functools
import math

import jax
import jax.numpy as jnp
from jax import lax
from jax.experimental import pallas as pl
from jax.experimental.pallas import tpu as pltpu

F32 = jnp.float32
BF16 = jnp.bfloat16

D_MODEL = 1024
DEPTH = 2
HALF = 512
SG_CHUNK = 128
SG_GROUPS = 4
CV_KERNEL = 31
HEAD_DIM = 64
N_Q_HEADS = 8
N_KV_HEADS = 2
Q_PER_KV = N_Q_HEADS // N_KV_HEADS
WINDOW = 128
ROPE_THETA = 10000.0
SC_KERNEL = 3
N_BRANCH = 4
D_FF = 2816
EPS = 1e-6
N_CHIPS = 4
N_DEV = 8

MIX_W = 4352
GATE_W = N_BRANCH * D_MODEL
PROJ_PAD = 2 * MIX_W
W_IN_SHARD = 2112
GU_SHARD = 1408
HALO = 128
CV_PAD = 32

ADAM_LR = 0.001
ADAM_B1 = 0.9
ADAM_B2 = 0.999
ADAM_EPS = 1e-08
ADAM_WD = 0.01
ADAM_STEP = 10

VMEM_LIMIT = 56 * 1024 * 1024
INV_SQRT2 = 1.0 / math.sqrt(2.0)
INV_SQRT_2PI = 1.0 / math.sqrt(2.0 * math.pi)
NEG_BIG = -1e30
MESH = pl.DeviceIdType.MESH

C_ZA, C_ZB, C_Q, C_K, C_V, C_ZD = 0, 1024, 2048, 2560, 2688, 2816


def _params(*sem):
    return pltpu.CompilerParams(dimension_semantics=sem, vmem_limit_bytes=VMEM_LIMIT)


def _sig(v):
    return 1.0 / (1.0 + jnp.exp(-v))


def _dot(a, b):
    return jnp.dot(a, b, preferred_element_type=F32)


def _dot_nt(a, b):
    return lax.dot_general(a, b, (((1,), (1,)), ((), ())), preferred_element_type=F32)


def _dot_tn(a, b):
    return lax.dot_general(a, b, (((0,), (0,)), ((), ())), preferred_element_type=F32)


def _full(shape):
    nd = len(shape)
    return pl.BlockSpec(shape, lambda *_: (0,) * nd)


def _rms_mm(x, g, w, tm, tn, name):
    T = x.shape[0]
    transposed = w.ndim == 2
    if transposed:
        N = w.shape[0]
        wspec = pl.BlockSpec((tn, D_MODEL), lambda i, j: (j, 0))
    else:
        tn = w.shape[2]
        N = w.shape[0] * tn
        wspec = pl.BlockSpec((None, D_MODEL, tn), lambda i, j: (j, 0, 0))

    def body(x_ref, g_ref, w_ref, o_ref, xn_ref):
        @pl.when(pl.program_id(1) == 0)
        def _():
            xv = x_ref[...]
            r = lax.rsqrt(jnp.mean(xv * xv, axis=-1, keepdims=True) + EPS)
            xn_ref[...] = (xv * r * g_ref[...]).astype(BF16)

        o_ref[...] = (_dot_nt if transposed else _dot)(xn_ref[...], w_ref[...]).astype(BF16)

    return pl.pallas_call(
        body, name=name, grid=(T // tm, N // tn),
        in_specs=[pl.BlockSpec((tm, D_MODEL), lambda i, j: (i, 0)), _full((1, D_MODEL)), wspec],
        out_specs=[pl.BlockSpec((tm, tn), lambda i, j: (i, j)), pl.BlockSpec((tm, D_MODEL), lambda i, j: (i, 0))],
        out_shape=[jax.ShapeDtypeStruct((T, N), BF16), jax.ShapeDtypeStruct((T, D_MODEL), BF16)],
        compiler_params=_params("parallel", "arbitrary"),
    )(x, g, w)


def _merge_fwd(x, ys, proj, wb, wo, tm, name):
    T = x.shape[0]

    def body(x_ref, ys_ref, zg_ref, wb_ref, wo_ref, xo_ref, mg_ref):
        merged = None
        for n in range(N_BRANCH):
            yn = ys_ref[:, n * HALF:(n + 1) * HALF]
            br = jnp.concatenate([_dot(yn, wb_ref[s, n]) for s in range(N_CHIPS)], axis=1)
            t = _sig(zg_ref[:, n * D_MODEL:(n + 1) * D_MODEL].astype(F32)) * br
            merged = t if merged is None else merged + t
        mb = merged.astype(BF16)
        mg_ref[...] = mb
        xo_ref[...] = x_ref[...] + _dot(mb, wo_ref[...])

    return pl.pallas_call(
        body, name=name, grid=(T // tm,),
        in_specs=[pl.BlockSpec((tm, D_MODEL), lambda i: (i, 0)), pl.BlockSpec((tm, N_BRANCH * HALF), lambda i: (i, 0)),
                  pl.BlockSpec((tm, GATE_W), lambda i: (i, 0)), _full(wb.shape), _full(wo.shape)],
        out_specs=[pl.BlockSpec((tm, D_MODEL), lambda i: (i, 0)), pl.BlockSpec((tm, D_MODEL), lambda i: (i, 0))],
        out_shape=[jax.ShapeDtypeStruct((T, D_MODEL), F32), jax.ShapeDtypeStruct((T, D_MODEL), BF16)],
        compiler_params=_params("parallel"),
    )(x, ys, proj, wb, wo)


def _ffn_down(xm, gu, wd, tm, name):
    T = xm.shape[0]

    def body(x_ref, gu_ref, wd_ref, o_ref):
        g = gu_ref[:, :D_FF].astype(F32)
        u = gu_ref[:, D_FF:].astype(F32)
        act = (g * _sig(g) * u).astype(BF16)
        o_ref[...] = x_ref[...] + _dot(act, wd_ref[...])

    return pl.pallas_call(
        body, name=name, grid=(T // tm,),
        in_specs=[pl.BlockSpec((tm, D_MODEL), lambda i: (i, 0)), pl.BlockSpec((tm, 2 * D_FF), lambda i: (i, 0)), _full(wd.shape)],
        out_specs=pl.BlockSpec((tm, D_MODEL), lambda i: (i, 0)),
        out_shape=jax.ShapeDtypeStruct((T, D_MODEL), F32),
        compiler_params=_params("parallel"),
    )(xm, gu, wd)


def _final_loss(x, g, tgt, tm, name):
    T = x.shape[0]

    def body(x_ref, g_ref, t_ref, dx_ref, dg_ref, ls_ref):
        @pl.when(pl.program_id(0) == 0)
        def _():
            dg_ref[...] = jnp.zeros_like(dg_ref)
            ls_ref[...] = jnp.zeros_like(ls_ref)

        xv = x_ref[...]
        gv = g_ref[...]
        r = lax.rsqrt(jnp.mean(xv * xv, axis=-1, keepdims=True) + EPS)
        xh = xv * r
        diff = xh * gv - t_ref[...]
        ls_ref[...] += jnp.full(ls_ref.shape, 0.5 / D_MODEL, F32) * jnp.sum(diff * diff)
        dy = diff * (1.0 / D_MODEL)
        dxh = dy * gv
        dx_ref[...] = r * (dxh - xh * jnp.mean(dxh * xh, axis=-1, keepdims=True))
        dg_ref[...] += jnp.sum(dy * xh, axis=0, keepdims=True)

    return pl.pallas_call(
        body, name=name, grid=(T // tm,),
        in_specs=[pl.BlockSpec((tm, D_MODEL), lambda i: (i, 0)), _full((1, D_MODEL)), pl.BlockSpec((tm, D_MODEL), lambda i: (i, 0))],
        out_specs=[pl.BlockSpec((tm, D_MODEL), lambda i: (i, 0)), _full((1, D_MODEL)), _full((1, 128))],
        out_shape=[jax.ShapeDtypeStruct((T, D_MODEL), F32), jax.ShapeDtypeStruct((1, D_MODEL), F32), jax.ShapeDtypeStruct((1, 128), F32)],
        compiler_params=_params("arbitrary"),
    )(x, g, tgt)


def _swiglu_bwd(dx, gu, wd, tm, name, after):
    T = dx.shape[0]

    def body(dx_ref, gu_ref, wd_ref, after_ref, dgu_ref, act_ref):
        del after_ref
        dact = _dot_nt(dx_ref[...].astype(BF16), wd_ref[...])
        g = gu_ref[:, :D_FF].astype(F32)
        u = gu_ref[:, D_FF:].astype(F32)
        s = _sig(g)
        silu = g * s
        act_ref[...] = (silu * u).astype(BF16)
        dgu_ref[:, :D_FF] = (dact * u * (s + silu * (1.0 - s))).astype(BF16)
        dgu_ref[:, D_FF:] = (dact * silu).astype(BF16)

    return pl.pallas_call(
        body, name=name, grid=(T // tm,),
        in_specs=[pl.BlockSpec((tm, D_MODEL), lambda i: (i, 0)), pl.BlockSpec((tm, 2 * D_FF), lambda i: (i, 0)), _full(wd.shape),
                  pl.BlockSpec(memory_space=pl.ANY)],
        out_specs=[pl.BlockSpec((tm, 2 * D_FF), lambda i: (i, 0)), pl.BlockSpec((tm, D_FF), lambda i: (i, 0))],
        out_shape=[jax.ShapeDtypeStruct((T, 2 * D_FF), BF16), jax.ShapeDtypeStruct((T, D_FF), BF16)],
        compiler_params=_params("parallel"),
    )(dx, gu, wd, after)


def _mm_tn(a, b, grid, a_block, a_map, b_block, b_map, o_shape, o_block, o_map, name, col_split=1):
    gk = grid[2]
    tm = [d for d in a_block if d is not None][-1]
    tn = [d for d in b_block if d is not None][-1]

    def body(a_ref, b_ref, o_ref, acc_ref):
        k = pl.program_id(2)
        p = _dot_tn(a_ref[...].astype(BF16), b_ref[...].astype(BF16))

        @pl.when(k == 0)
        def _():
            acc_ref[...] = p

        @pl.when(k > 0)
        def _():
            acc_ref[...] += p

        @pl.when(k == gk - 1)
        def _():
            if col_split == 1:
                o_ref[...] = acc_ref[...].astype(o_ref.dtype)
            else:
                w = tn // col_split
                for s in range(col_split):
                    o_ref[s] = acc_ref[:, s * w:(s + 1) * w].astype(o_ref.dtype)

    return pl.pallas_call(
        body, name=name, grid=grid,
        in_specs=[pl.BlockSpec(a_block, a_map), pl.BlockSpec(b_block, b_map)],
        out_specs=pl.BlockSpec(o_block, o_map),
        out_shape=jax.ShapeDtypeStruct(o_shape, BF16),
        scratch_shapes=[pltpu.VMEM((tm, tn), F32)],
        compiler_params=_params("parallel", "parallel", "arbitrary"),
    )(a, b)


def _mm_nt_rmsbwd(a, w, x, g, dres, tm, tk, name):
    T = x.shape[0]
    transposed = w.ndim == 2
    if transposed:
        gk = w.shape[0] // tk
        wspec = pl.BlockSpec((tk, D_MODEL), lambda i, k: (k, 0))
    else:
        tk = w.shape[2]
        gk = w.shape[0]
        wspec = pl.BlockSpec((None, D_MODEL, tk), lambda i, k: (k, 0, 0))

    def body(a_ref, w_ref, x_ref, g_ref, r_ref, dx_ref, dg_ref, acc_ref):
        i, k = pl.program_id(0), pl.program_id(1)
        p = (_dot if transposed else _dot_nt)(a_ref[...], w_ref[...])

        @pl.when(k == 0)
        def _():
            acc_ref[...] = p

        @pl.when(k > 0)
        def _():
            acc_ref[...] += p

        @pl.when(jnp.logical_and(i == 0, k == 0))
        def _():
            dg_ref[...] = jnp.zeros_like(dg_ref)

        @pl.when(k == gk - 1)
        def _():
            dh = acc_ref[...]
            xv = x_ref[...]
            r = lax.rsqrt(jnp.mean(xv * xv, axis=-1, keepdims=True) + EPS)
            xh = xv * r
            dxh = dh * g_ref[...]
            dx_ref[...] = r_ref[...] + r * (dxh - xh * jnp.mean(dxh * xh, axis=-1, keepdims=True))
            dg_ref[...] += jnp.sum(dh * xh, axis=0, keepdims=True)

    return pl.pallas_call(
        body, name=name, grid=(T // tm, gk),
        in_specs=[pl.BlockSpec((tm, tk), lambda i, k: (i, k)), wspec, pl.BlockSpec((tm, D_MODEL), lambda i, k: (i, 0)),
                  _full((1, D_MODEL)), pl.BlockSpec((tm, D_MODEL), lambda i, k: (i, 0))],
        out_specs=[pl.BlockSpec((tm, D_MODEL), lambda i, k: (i, 0)), _full((1, D_MODEL))],
        out_shape=[jax.ShapeDtypeStruct((T, D_MODEL), F32), jax.ShapeDtypeStruct((1, D_MODEL), F32)],
        scratch_shapes=[pltpu.VMEM((tm, D_MODEL), F32)],
        compiler_params=_params("arbitrary", "arbitrary"),
    )(a, w, x, g, dres)


def _merge_bwd(dxm, ys, proj, wb, wo, tm, name):
    T = dxm.shape[0]

    def body(dx_ref, ys_ref, zg_ref, wb_ref, wo_ref, dys_ref, dbr_ref, dp_ref):
        dmerged = _dot_nt(dx_ref[...].astype(BF16), wo_ref[...])
        for n in range(N_BRANCH):
            yn = ys_ref[:, n * HALF:(n + 1) * HALF]
            br = jnp.concatenate([_dot(yn, wb_ref[s, n]) for s in range(N_CHIPS)], axis=1)
            gt = _sig(zg_ref[:, n * D_MODEL:(n + 1) * D_MODEL].astype(F32))
            dbr = (gt * dmerged).astype(BF16)
            dbr_ref[:, n * D_MODEL:(n + 1) * D_MODEL] = dbr
            dp_ref[:, n * D_MODEL:(n + 1) * D_MODEL] = (dmerged * br * gt * (1.0 - gt)).astype(BF16)
            dy = None
            for s in range(N_CHIPS):
                t = _dot_nt(dbr[:, s * 256:(s + 1) * 256], wb_ref[s, n])
                dy = t if dy is None else dy + t
            dys_ref[:, n * HALF:(n + 1) * HALF] = dy.astype(BF16)
        dp_ref[:, GATE_W:] = jnp.zeros((tm, MIX_W - GATE_W), BF16)

    return pl.pallas_call(
        body, name=name, grid=(T // tm,),
        in_specs=[pl.BlockSpec((tm, D_MODEL), lambda i: (i, 0)), pl.BlockSpec((tm, N_BRANCH * HALF), lambda i: (i, 0)),
                  pl.BlockSpec((tm, GATE_W), lambda i: (i, 0)), _full(wb.shape), _full(wo.shape)],
        out_specs=[pl.BlockSpec((tm, N_BRANCH * HALF), lambda i: (i, 0)), pl.BlockSpec((tm, GATE_W), lambda i: (i, 0)),
                   pl.BlockSpec((tm, MIX_W), lambda i: (i, 0))],
        out_shape=[jax.ShapeDtypeStruct((T, N_BRANCH * HALF), BF16), jax.ShapeDtypeStruct((T, GATE_W), BF16),
                   jax.ShapeDtypeStruct((T, PROJ_PAD), BF16)],
        compiler_params=_params("parallel"),
    )(dxm, ys, proj, wb, wo)


def _gelu(v):
    return 0.5 * v * (1.0 + lax.erf(v * INV_SQRT2))


def _gelu_grad(v):
    return 0.5 * (1.0 + lax.erf(v * INV_SQRT2)) + v * jnp.exp(-0.5 * v * v) * INV_SQRT_2PI


def _rot_half(t):
    w = t.shape[1]
    lane = lax.broadcasted_iota(jnp.int32, t.shape, 1)
    return jnp.where((lane % HEAD_DIM) < HEAD_DIM // 2, pltpu.roll(t, w - HEAD_DIM // 2, 1), pltpu.roll(t, HEAD_DIM // 2, 1))


def _rope(t, cos, sin_signed):
    return t * cos + _rot_half(t) * sin_signed


def _rope_t(d, cos, sin_signed):
    return d * cos + _rot_half(d * sin_signed)


def _ln_fwd(v, g, b):
    mu = jnp.mean(v, axis=-1, keepdims=True)
    vc = v - mu
    r = lax.rsqrt(jnp.mean(vc * vc, axis=-1, keepdims=True) + EPS)
    vh = vc * r
    return vh * g + b, vh, r


def _ln_bwd(dn, vh, r, g):
    dvh = dn * g
    return r * (dvh - jnp.mean(dvh, axis=-1, keepdims=True) - vh * jnp.mean(dvh * vh, axis=-1, keepdims=True))


def _sublane_shifts(sh_ref, rows):
    for b in range(1, 8):
        sh_ref[b, 0:rows - 8, :] = sh_ref[0, pl.ds(b, rows - 8), :]


def _tap(sh_ref, off, n):
    return sh_ref[off % 8, pl.ds(off - off % 8, n), :]


def _tril_mask():
    return lax.broadcasted_iota(jnp.int32, (SG_CHUNK, SG_CHUNK), 0) >= lax.broadcasted_iota(jnp.int32, (SG_CHUNK, SG_CHUNK), 1)


def _band_masks():
    shape = (Q_PER_KV * WINDOW, 2 * WINDOW)
    row = lax.broadcasted_iota(jnp.int32, shape, 0) % WINDOW
    col = lax.broadcasted_iota(jnp.int32, shape, 1)
    band = (col > row) & (col <= row + WINDOW)
    return band, band & (col >= WINDOW)


def _attn_probs(qs, kh, sink_col, valid):
    s = jnp.where(valid, _dot_nt(qs, kh) * (HEAD_DIM ** -0.5), NEG_BIG)
    m = jnp.maximum(jnp.max(s, axis=-1, keepdims=True), sink_col)
    p = jnp.exp(s - m)
    es = jnp.exp(sink_col - m)
    inv = 1.0 / (jnp.sum(p, axis=-1, keepdims=True) + es)
    return p * inv, es * inv


def _sink_col(sinks_ref, h):
    return jnp.concatenate([jnp.broadcast_to(sinks_ref[:, h * Q_PER_KV + g:h * Q_PER_KV + g + 1], (WINDOW, 1))
                            for g in range(Q_PER_KV)], axis=0)


def _mixer_in_specs(TB, nb):
    r = TB // HALO
    last = nb * r - 1
    cur = pl.BlockSpec((TB, MIX_W), lambda i: (i, 1))
    prev = pl.BlockSpec((HALO, MIX_W), lambda i: (jnp.maximum(i * r - 1, 0), 1))
    nxt = pl.BlockSpec((HALO, MIX_W), lambda i: (jnp.minimum((i + 1) * r, last), 1))
    tcur = pl.BlockSpec((TB, 128), lambda i: (i, 0))
    tprev = pl.BlockSpec((HALO, 128), lambda i: (jnp.maximum(i * r - 1, 0), 0))
    tnxt = pl.BlockSpec((HALO, 128), lambda i: (jnp.minimum((i + 1) * r, last), 0))
    return cur, prev, nxt, tcur, tprev, tnxt


def _mixer_param_specs():
    return [_full((1, HALF)), _full((1, HALF)), _full((SG_GROUPS, SG_CHUNK, SG_CHUNK)), _full((SG_CHUNK, 128)),
            _full((32, HALF)), _full((1, HALF)), _full((1, HALF)), _full((1, HALF)), _full((1, 128)), _full((8, HALF))]


def _mixers_fwd(proj, cos_t, sin_t, mp, TB, name):
    T = proj.shape[0]
    nb = T // TB
    r = TB // HALO
    cur, prev, _, tcur, tprev, _ = _mixer_in_specs(TB, nb)

    def body(zc_ref, zp_ref, cc_ref, sc_ref, cp_ref, sp_ref,
             lg_ref, lb_ref, sgw_ref, sgb_ref, cvw_ref, cvb_ref, cvg_ref, cvbb_ref, sinks_ref, scw_ref,
             ys_ref, scr_ref, k_ref, v_ref, sh_ref):
        i = pl.program_id(0)
        pm = (i > 0).astype(F32)

        def colsE(c0, c1):
            return jnp.concatenate([zp_ref[:, c0:c1].astype(F32) * pm, zc_ref[:, c0:c1].astype(F32)], axis=0)

        a = _gelu(zc_ref[:, C_ZA:C_ZA + 2 * HALF].astype(F32))
        u = a[:, :HALF]
        vn, _, _ = _ln_fwd(a[:, HALF:], lg_ref[...], lb_ref[...])
        vnb = vn.astype(BF16)
        tril = _tril_mask()
        chunks = [slice(ci * SG_CHUNK, (ci + 1) * SG_CHUNK) for ci in range(r)]
        for g in range(SG_GROUPS):
            cols = slice(g * 128, (g + 1) * 128)
            wt = jnp.where(tril, sgw_ref[g], 0.0).astype(BF16)
            mixed = _dot(wt, jnp.concatenate([vnb[rows, cols] for rows in chunks], axis=1)) + sgb_ref[:, g:g + 1]
            for ci, rows in enumerate(chunks):
                ys_ref[rows, cols] = (u[rows, cols] * mixed[:, ci * 128:(ci + 1) * 128]).astype(BF16)

        def colsB(c0, c1):
            return jnp.concatenate([zp_ref[HALO - CV_PAD:, c0:c1].astype(F32) * pm, zc_ref[:, c0:c1].astype(F32)], axis=0)

        sh_ref[0] = colsB(C_ZB, C_ZB + HALF) * _sig(colsB(C_ZB + HALF, C_ZB + 2 * HALF))
        _sublane_shifts(sh_ref, TB + CV_PAD)
        c = jnp.broadcast_to(cvb_ref[...], (TB, HALF))
        for k in range(CV_KERNEL):
            c = c + cvw_ref[k:k + 1, :] * _tap(sh_ref, CV_PAD - (CV_KERNEL - 1) + k, TB)
        n, _, _ = _ln_fwd(c, cvg_ref[...], cvbb_ref[...])
        ys_ref[:, HALF:2 * HALF] = (n * _sig(n)).astype(BF16)

        zd = colsE(C_ZD + HALF, C_ZD + 3 * HALF)
        scr_ref[...] = zd[:, :HALF] * zd[:, HALF:]
        cv = None
        for k in range(SC_KERNEL):
            t = scw_ref[k:k + 1, :] * scr_ref[pl.ds(HALO - (SC_KERNEL - 1) + k, TB), :]
            cv = t if cv is None else cv + t
        ys_ref[:, 3 * HALF:4 * HALF] = (zc_ref[:, C_ZD:C_ZD + HALF].astype(F32) * cv).astype(BF16)

        cosE = jnp.concatenate([cp_ref[...], cc_ref[...]], axis=0)
        sinE = jnp.concatenate([sp_ref[...], sc_ref[...]], axis=0)
        k_ref[...] = _rope(colsE(C_K, C_K + 128), cosE, sinE).astype(BF16)
        v_ref[...] = colsE(C_V, C_V + 128).astype(BF16)
        cosC, sinC = cc_ref[...], sc_ref[...]
        q = jnp.concatenate([_rope(zc_ref[:, C_Q + 128 * j:C_Q + 128 * (j + 1)].astype(F32), cosC, sinC)
                             for j in range(4)], axis=1).astype(BF16)
        in_band, in_band_cur = _band_masks()
        sink_cols = [_sink_col(sinks_ref, h) for h in range(N_KV_HEADS)]
        for qb in range(r):
            valid = in_band if qb else in_band_cur | (in_band & (i > 0))
            for h in range(N_KV_HEADS):
                hc = slice(h * HEAD_DIM, (h + 1) * HEAD_DIM)
                kh = k_ref[qb * WINDOW:qb * WINDOW + 2 * WINDOW, hc]
                vh = v_ref[qb * WINDOW:qb * WINDOW + 2 * WINDOW, hc]
                qs = jnp.concatenate([q[qb * WINDOW:(qb + 1) * WINDOW, (h * Q_PER_KV + g) * HEAD_DIM:(h * Q_PER_KV + g + 1) * HEAD_DIM]
                                      for g in range(Q_PER_KV)], axis=0)
                probs, _ = _attn_probs(qs, kh, sink_cols[h], valid)
                o = _dot(probs.astype(BF16), vh)
                for g in range(Q_PER_KV):
                    c0 = 2 * HALF + (h * Q_PER_KV + g) * HEAD_DIM
                    ys_ref[qb * WINDOW:(qb + 1) * WINDOW, c0:c0 + HEAD_DIM] = o[g * WINDOW:(g + 1) * WINDOW].astype(BF16)

    return pl.pallas_call(
        body, name=name, grid=(nb,),
        in_specs=[cur, prev, tcur, tcur, tprev, tprev] + _mixer_param_specs(),
        out_specs=pl.BlockSpec((TB, 4 * HALF), lambda i: (i, 0)),
        out_shape=jax.ShapeDtypeStruct((T, 4 * HALF), BF16),
        scratch_shapes=[pltpu.VMEM((TB + HALO, HALF), F32), pltpu.VMEM((TB + HALO, 128), BF16), pltpu.VMEM((TB + HALO, 128), BF16),
                        pltpu.VMEM((8, TB + CV_PAD, HALF), F32)],
        compiler_params=_params("parallel"),
    )(proj, proj, cos_t, sin_t, cos_t, sin_t, *mp)


def _mixers_bwd(proj, dys, dproj, cos_t, sin_t, mp, TB, name):
    T = proj.shape[0]
    nb = T // TB
    r = TB // HALO
    RE = TB + 2 * HALO
    RC = TB + HALO
    cur, prev, nxt, tcur, tprev, tnxt = _mixer_in_specs(TB, nb)
    dcur = pl.BlockSpec((TB, 4 * HALF), lambda i: (i, 0))
    dnxt = pl.BlockSpec((HALO, 4 * HALF), lambda i: (jnp.minimum((i + 1) * r, nb * r - 1), 0))

    def body(zc_ref, zp_ref, zn_ref, dyc_ref, dyn_ref, cc_ref, sc_ref, cp_ref, sp_ref, cn_ref, sn_ref,
             lg_ref, lb_ref, sgw_ref, sgb_ref, cvw_ref, cvb_ref, cvg_ref, cvbb_ref, sinks_ref, scw_ref, dp_in_ref,
             dz_ref, dlg_ref, dlb_ref, dsgw_ref, dsgb_ref, dcvw_ref, dcvb_ref, dcvg_ref, dcvbb_ref, dsink_ref, dscw_ref,
             scr_ref, scr2_ref, k_ref, v_ref, dk_ref, dv_ref, dq_ref, sh_ref, sh2_ref):
        del dp_in_ref
        i = pl.program_id(0)
        pm = (i > 0).astype(F32)
        nm = (i < nb - 1).astype(F32)

        @pl.when(i == 0)
        def _():
            for ref in (dlg_ref, dlb_ref, dsgw_ref, dsgb_ref, dcvw_ref, dcvb_ref, dcvg_ref, dcvbb_ref, dsink_ref, dscw_ref):
                ref[...] = jnp.zeros_like(ref)

        def colsE(c0, c1):
            return jnp.concatenate([zp_ref[:, c0:c1].astype(F32) * pm, zc_ref[:, c0:c1].astype(F32),
                                    zn_ref[:, c0:c1].astype(F32)], axis=0)

        def colsC(c0, c1):
            return jnp.concatenate([zc_ref[:, c0:c1].astype(F32), zn_ref[:, c0:c1].astype(F32)], axis=0)

        def dyC(c0, c1):
            return jnp.concatenate([dyc_ref[:, c0:c1].astype(F32), dyn_ref[:, c0:c1].astype(F32) * nm], axis=0)

        za = zc_ref[:, C_ZA:C_ZA + 2 * HALF].astype(F32)
        a = _gelu(za)
        u = a[:, :HALF]
        lg = lg_ref[...]
        vn, vh, rs = _ln_fwd(a[:, HALF:], lg, lb_ref[...])
        vnb = vn.astype(BF16)
        dya = dyc_ref[:, 0:HALF].astype(F32)
        tril = _tril_mask()
        lane128 = lax.broadcasted_iota(jnp.int32, (SG_CHUNK, 128), 1)
        chunks = [slice(ci * SG_CHUNK, (ci + 1) * SG_CHUNK) for ci in range(r)]
        side = lambda t, cols: jnp.concatenate([t[rows, cols] for rows in chunks], axis=1)
        for g in range(SG_GROUPS):
            cols = slice(g * 128, (g + 1) * 128)
            wt = jnp.where(tril, sgw_ref[g], 0.0).astype(BF16)
            vb = side(vnb, cols)
            dy_blk = side(dya, cols)
            du_g = dy_blk * (_dot(wt, vb) + sgb_ref[:, g:g + 1])
            dmix = dy_blk * side(u, cols)
            dmb = dmix.astype(BF16)
            dvn_g = _dot_tn(wt, dmb)
            dsgw_ref[g] += jnp.where(tril, _dot_nt(dmb, vb), 0.0)
            dsgb_ref[...] += jnp.where(lane128 == g, jnp.sum(dmix, axis=1, keepdims=True), 0.0)
            for ci, rows in enumerate(chunks):
                scr_ref[rows, cols] = du_g[:, ci * 128:(ci + 1) * 128]
                scr2_ref[rows, cols] = dvn_g[:, ci * 128:(ci + 1) * 128]
        du, dvn = scr_ref[0:TB, :], scr2_ref[0:TB, :]
        dlg_ref[...] += jnp.sum(dvn * vh, axis=0, keepdims=True)
        dlb_ref[...] += jnp.sum(dvn, axis=0, keepdims=True)
        dvv = _ln_bwd(dvn, vh, rs, lg)
        gg = _gelu_grad(za)
        dz_ref[:, C_ZA:C_ZA + HALF] = (du * gg[:, :HALF]).astype(BF16)
        dz_ref[:, C_ZA + HALF:C_ZA + 2 * HALF] = (dvv * gg[:, HALF:]).astype(BF16)

        RB = TB + CV_PAD

        def colsB(c0, c1):
            return jnp.concatenate([zp_ref[HALO - CV_PAD:, c0:c1].astype(F32) * pm, zc_ref[:, c0:c1].astype(F32),
                                    zn_ref[:CV_PAD, c0:c1].astype(F32)], axis=0)

        sh_ref[0] = colsB(C_ZB, C_ZB + HALF) * _sig(colsB(C_ZB + HALF, C_ZB + 2 * HALF))
        _sublane_shifts(sh_ref, RB + CV_PAD)
        c = jnp.broadcast_to(cvb_ref[...], (RB, HALF))
        for k in range(CV_KERNEL):
            c = c + cvw_ref[k:k + 1, :] * _tap(sh_ref, CV_PAD - (CV_KERNEL - 1) + k, RB)
        cvg = cvg_ref[...]
        n, ch, rc = _ln_fwd(c, cvg, cvbb_ref[...])
        sn = _sig(n)
        dyb = jnp.concatenate([dyc_ref[:, HALF:2 * HALF].astype(F32), dyn_ref[:CV_PAD, HALF:2 * HALF].astype(F32) * nm], axis=0)
        dn = dyb * (sn + n * sn * (1.0 - sn))
        dno = dn[:TB]
        dcvg_ref[...] += jnp.sum(dno * ch[:TB], axis=0, keepdims=True)
        dcvbb_ref[...] += jnp.sum(dno, axis=0, keepdims=True)
        dc = _ln_bwd(dn, ch, rc, cvg)
        sh2_ref[0] = dc
        _sublane_shifts(sh2_ref, RB)
        dcvb_ref[...] += jnp.sum(dc[:TB], axis=0, keepdims=True)
        dy0 = None
        for k in range(CV_KERNEL):
            wk = cvw_ref[k:k + 1, :]
            t = wk * _tap(sh2_ref, CV_KERNEL - 1 - k, TB)
            dy0 = t if dy0 is None else dy0 + t
            dcvw_ref[k:k + 1, :] += jnp.sum(dc[:TB] * _tap(sh_ref, CV_PAD - (CV_KERNEL - 1) + k, TB), axis=0, keepdims=True)
        ab = zc_ref[:, C_ZB:C_ZB + HALF].astype(F32)
        sg = _sig(zc_ref[:, C_ZB + HALF:C_ZB + 2 * HALF].astype(F32))
        dz_ref[:, C_ZB:C_ZB + HALF] = (dy0 * sg).astype(BF16)
        dz_ref[:, C_ZB + HALF:C_ZB + 2 * HALF] = (dy0 * ab * sg * (1.0 - sg)).astype(BF16)

        zd = colsE(C_ZD + HALF, C_ZD + 3 * HALF)
        scr_ref[...] = zd[:, :HALF] * zd[:, HALF:]
        dcv = dyC(3 * HALF, 4 * HALF) * colsC(C_ZD, C_ZD + HALF)
        scr2_ref[...] = dcv
        cv = None
        dud = None
        for k in range(SC_KERNEL):
            wk = scw_ref[k:k + 1, :]
            us = scr_ref[pl.ds(HALO - (SC_KERNEL - 1) + k, TB), :]
            t = wk * us
            cv = t if cv is None else cv + t
            t2 = wk * scr2_ref[pl.ds(SC_KERNEL - 1 - k, TB), :]
            dud = t2 if dud is None else dud + t2
            dscw_ref[k:k + 1, :] += jnp.sum(dcv[:TB] * us, axis=0, keepdims=True)
        dz_ref[:, C_ZD:C_ZD + HALF] = (dyc_ref[:, 3 * HALF:4 * HALF].astype(F32) * cv).astype(BF16)
        dz_ref[:, C_ZD + HALF:C_ZD + 2 * HALF] = (dud * zc_ref[:, C_ZD + 2 * HALF:C_ZD + 3 * HALF].astype(F32)).astype(BF16)
        dz_ref[:, C_ZD + 2 * HALF:C_ZD + 3 * HALF] = (dud * zc_ref[:, C_ZD + HALF:C_ZD + 2 * HALF].astype(F32)).astype(BF16)

        cosE = jnp.concatenate([cp_ref[...], cc_ref[...], cn_ref[...]], axis=0)
        sinE = jnp.concatenate([sp_ref[...], sc_ref[...], sn_ref[...]], axis=0)
        k_ref[...] = _rope(colsE(C_K, C_K + 128), cosE, sinE).astype(BF16)
        v_ref[...] = colsE(C_V, C_V + 128).astype(BF16)
        dk_ref[...] = jnp.zeros_like(dk_ref)
        dv_ref[...] = jnp.zeros_like(dv_ref)
        q = jnp.concatenate([_rope(colsC(C_Q + 128 * j, C_Q + 128 * (j + 1)), cosE[HALO:], sinE[HALO:])
                             for j in range(4)], axis=1).astype(BF16)
        dO = dyC(2 * HALF, 3 * HALF).astype(BF16)
        lane_s = lax.broadcasted_iota(jnp.int32, (1, 128), 1)
        in_band, in_band_cur = _band_masks()
        sink_cols = [_sink_col(sinks_ref, h) for h in range(N_KV_HEADS)]
        for qb in range(r + 1):
            valid = in_band if qb else in_band_cur | (in_band & (i > 0))
            rows = slice(qb * WINDOW, (qb + 1) * WINDOW)
            band = slice(qb * WINDOW, qb * WINDOW + 2 * WINDOW)
            for h in range(N_KV_HEADS):
                hc = slice(h * HEAD_DIM, (h + 1) * HEAD_DIM)
                kh = k_ref[band, hc]
                vh_ = v_ref[band, hc]
                heads = [slice((h * Q_PER_KV + g) * HEAD_DIM, (h * Q_PER_KV + g + 1) * HEAD_DIM) for g in range(Q_PER_KV)]
                qs = jnp.concatenate([q[rows, hs] for hs in heads], axis=0)
                dos = jnp.concatenate([dO[rows, hs] for hs in heads], axis=0)
                probs, p_sink = _attn_probs(qs, kh, sink_cols[h], valid)
                dP = _dot_nt(dos, vh_)
                rsum = jnp.sum(probs * dP, axis=-1, keepdims=True)
                dS = (probs * (dP - rsum) * (HEAD_DIM ** -0.5)).astype(BF16)
                dk_ref[band, hc] += _dot_tn(dS, qs)
                dv_ref[band, hc] += _dot_tn(probs.astype(BF16), dos)
                if qb < r:
                    dqs = _dot(dS, kh)
                    dsk = -p_sink * rsum
                    for g in range(Q_PER_KV):
                        dq_ref[rows, heads[g]] = dqs[g * WINDOW:(g + 1) * WINDOW]
                        dsink_ref[...] += jnp.where(lane_s == h * Q_PER_KV + g, jnp.sum(dsk[g * WINDOW:(g + 1) * WINDOW]), 0.0)
        cosC, sinC = cc_ref[...], sc_ref[...]
        for j in range(4):
            dz_ref[:, C_Q + 128 * j:C_Q + 128 * (j + 1)] = _rope_t(dq_ref[:, 128 * j:128 * (j + 1)], cosC, sinC).astype(BF16)
        dz_ref[:, C_K:C_K + 128] = _rope_t(dk_ref[HALO:HALO + TB, :], cosC, sinC).astype(BF16)
        dz_ref[:, C_V:C_V + 128] = dv_ref[HALO:HALO + TB, :].astype(BF16)

    small = [((1, HALF), F32), ((1, HALF), F32), ((SG_GROUPS, SG_CHUNK, SG_CHUNK), F32), ((SG_CHUNK, 128), F32),
             ((32, HALF), F32), ((1, HALF), F32), ((1, HALF), F32), ((1, HALF), F32), ((1, 128), F32), ((8, HALF), F32)]
    outs = pl.pallas_call(
        body, name=name, grid=(nb,),
        in_specs=[cur, prev, nxt, dcur, dnxt, tcur, tcur, tprev, tprev, tnxt, tnxt] + _mixer_param_specs()
                 + [pl.BlockSpec(memory_space=pl.ANY)],
        out_specs=[pl.BlockSpec((TB, MIX_W), lambda i: (i, 1))] + [_full(s) for s, _ in small],
        out_shape=[jax.ShapeDtypeStruct((T, PROJ_PAD), BF16)] + [jax.ShapeDtypeStruct(s, d) for s, d in small],
        scratch_shapes=[pltpu.VMEM((RE, HALF), F32), pltpu.VMEM((RC, HALF), F32), pltpu.VMEM((RE, 128), BF16), pltpu.VMEM((RE, 128), BF16),
                        pltpu.VMEM((RE, 128), F32), pltpu.VMEM((RE, 128), F32), pltpu.VMEM((TB, HALF), F32),
                        pltpu.VMEM((8, TB + 2 * CV_PAD, HALF), F32), pltpu.VMEM((8, TB + CV_PAD, HALF), F32)],
        input_output_aliases={21: 0},
        compiler_params=_params("arbitrary"),
    )(proj, proj, proj, dys, dys, cos_t, sin_t, cos_t, sin_t, cos_t, sin_t, *mp, dproj)
    return outs


def _rope_tables(T):
    pos = jnp.arange(T, dtype=F32)
    inv_freq = 1.0 / (ROPE_THETA ** (jnp.arange(0, HEAD_DIM, 2, dtype=F32) / HEAD_DIM))
    ang = pos[:, None] * inv_freq[None, :]
    cos, sin = jnp.cos(ang), jnp.sin(ang)
    cos_t = jnp.concatenate([cos, cos, cos, cos], axis=1)
    sin_t = jnp.concatenate([-sin, sin, -sin, sin], axis=1)
    return cos_t, sin_t


def _mixer_params(l, sg_ln_g, sg_ln_b, sg_w, sg_b, cv_w, cv_b, cv_ln_g, cv_ln_b, attn_sinks, sc_w):
    sgb_t = jnp.zeros((SG_CHUNK, 128), F32).at[:, :SG_GROUPS].set(sg_b[l].T)
    cvw = jnp.zeros((32, HALF), F32).at[:CV_KERNEL].set(cv_w[l])
    scw = jnp.zeros((8, HALF), F32).at[:SC_KERNEL].set(sc_w[l])
    sinks = jnp.zeros((1, 128), F32).at[0, :N_Q_HEADS].set(attn_sinks[l])
    return [sg_ln_g[l][None], sg_ln_b[l][None], sg_w[l], sgb_t, cvw, cv_b[l][None], cv_ln_g[l][None], cv_ln_b[l][None], sinks, scw]


def _w_in_layout(w_in_g):
    cut = MIX_W - 2 * W_IN_SHARD
    return jnp.concatenate([w_in_g[2][cut:], w_in_g[3], jnp.zeros((MIX_W - GATE_W, D_MODEL), w_in_g.dtype),
                            w_in_g[0], w_in_g[1], w_in_g[2][:cut]], axis=0)


def _w_in_unlayout(dw):
    cut = MIX_W - 2 * W_IN_SHARD
    return jnp.stack([dw[MIX_W:MIX_W + W_IN_SHARD], dw[MIX_W + W_IN_SHARD:MIX_W + 2 * W_IN_SHARD],
                      jnp.concatenate([dw[MIX_W + 2 * W_IN_SHARD:], dw[:W_IN_SHARD - cut]], axis=0),
                      dw[W_IN_SHARD - cut:GATE_W]], axis=0)


def _device_step(x, tgt, norm_mix, norm_ffn, norm_final, mixer_params, w_in_p, wb_g, wo_g, wgu_g, wd_g):
    T = x.shape[0]
    tables = _rope_tables(T)
    saved = []
    for l in range(DEPTH):
        lw = dict(w_in=w_in_p[l], w_branch=wb_g[l], w_out=wo_g[l], w_gate_up=wgu_g[l], w_down=wd_g[l],
                  norm_mix=norm_mix[l][None], norm_ffn=norm_ffn[l][None], mixer=mixer_params[l], after=jnp.zeros((8, 128), F32))
        x, sv = _fwd_layer(l, x, lw, tables)
        saved.append((lw, sv))
    dx, dnf, loss = _final_loss(x, norm_final[None], tgt, 256, "final_loss")
    grads = [None] * DEPTH
    for l in reversed(range(DEPTH)):
        lw, sv = saved[l]
        dxm, g_ffn = _bwd_layer_ffn(l, dx, lw, sv)
        dx, g_mix = _bwd_layer_mix(l, dxm, lw, sv, tables)
        raw = {**g_ffn, **g_mix}
        grads[l] = {**raw, **_small_views(raw)}
    return loss, dx, dnf[0], grads


MIX_BLOCK = 256


def _fwd_layer(l, x, lw, tables):
    return _fwd_layer_rest(l, x, _fwd_layer_mix(l, x, lw, tables), lw)


def _fwd_layer_mix(l, x, lw, tables, between=None):
    proj, xn = _rms_mm(x, lw["norm_mix"], lw["w_in"], min(x.shape[0], 1024), 2176, f"proj{l}")
    if between is not None:
        between(proj, lw)
    return proj, xn, _mixers_fwd(proj, *tables, lw["mixer"], MIX_BLOCK, f"mixers_fwd{l}")


def _fwd_layer_rest(l, x, mixed, lw, between=None):
    proj, xn, ys = mixed
    TM = min(x.shape[0], 1024)
    xm, merged = _merge_fwd(x, ys, proj, lw["w_branch"], lw["w_out"], min(x.shape[0], 512), f"merge_fwd{l}")
    if between is not None:
        between(xm, lw)
    gu, hn = _rms_mm(xm, lw["norm_ffn"], lw["w_gate_up"], TM, GU_SHARD, f"ffn_up{l}")
    x_out = _ffn_down(xm, gu, lw["w_down"], min(x.shape[0], 512), f"ffn_down{l}")
    return x_out, (x, proj, xn, ys, xm, merged, gu, hn)


def _bwd_layer_ffn(l, dx, lw, sv, between=None):
    x_in, proj, xn, ys, xm, merged, gu, hn = sv
    T = dx.shape[0]
    tkk = min(T, 1024)
    gk = T // tkk
    dgu, act = _swiglu_bwd(dx, gu, lw["w_down"], 256, f"swiglu_bwd{l}", lw["after"])
    d_wd = _mm_tn(act, dx, (2, 1, gk), (tkk, D_FF // 2), lambda i, j, k: (k, i), (tkk, D_MODEL), lambda i, j, k: (k, 0),
                  (D_FF, D_MODEL), (D_FF // 2, D_MODEL), lambda i, j, k: (i, 0), f"dw_down{l}")
    d_wgu = _mm_tn(hn, dgu, (1, N_CHIPS, gk), (tkk, D_MODEL), lambda i, j, k: (k, 0), (tkk, GU_SHARD), lambda i, j, k: (k, j),
                   (N_CHIPS, D_MODEL, GU_SHARD), (None, D_MODEL, GU_SHARD), lambda i, j, k: (j, 0, 0), f"dw_gate_up{l}")
    if between is not None:
        between(dict(w_gate_up=d_wgu, w_down=d_wd), lw)
    dxm, d_nffn = _mm_nt_rmsbwd(dgu, lw["w_gate_up"], xm, lw["norm_ffn"], dx, min(T, 1024), GU_SHARD, f"ffn_up_bwd{l}")
    dys, dbr, dproj = _merge_bwd(dxm, ys, proj, lw["w_branch"], lw["w_out"], 256, f"merge_bwd{l}")
    d_wo = _mm_tn(merged, dxm, (2, 1, gk), (tkk, 512), lambda i, j, k: (k, i), (tkk, D_MODEL), lambda i, j, k: (k, 0),
                  (D_MODEL, D_MODEL), (512, D_MODEL), lambda i, j, k: (i, 0), f"dw_out{l}")
    d_wb = _mm_tn(ys, dbr, (N_BRANCH, 1, gk), (tkk, HALF), lambda i, j, k: (k, i), (tkk, D_MODEL), lambda i, j, k: (k, i),
                  (N_CHIPS, N_BRANCH, HALF, 256), (N_CHIPS, None, HALF, 256), lambda i, j, k: (0, i, 0, 0), f"dw_branch{l}", col_split=N_CHIPS)
    return (dxm, dys, dproj), dict(w_branch=d_wb, w_out=d_wo, w_gate_up=d_wgu, w_down=d_wd, norm_ffn=d_nffn)


def _bwd_layer_mix(l, carry, lw, sv, tables, between=None):
    dxm, dys, dproj = carry
    x_in, proj, xn, ys, xm, merged, gu, hn = sv
    T = dxm.shape[0]
    tkk = min(T, 1024)
    gk = T // tkk
    mb = _mixers_bwd(proj, dys, dproj, *tables, lw["mixer"], MIX_BLOCK, f"mixers_bwd{l}")
    dproj = mb[0]
    d_win = _mm_tn(dproj, xn, (PROJ_PAD // 2176, 1, gk), (tkk, 2176), lambda i, j, k: (k, i), (tkk, D_MODEL), lambda i, j, k: (k, 0),
                   (PROJ_PAD, D_MODEL), (2176, D_MODEL), lambda i, j, k: (i, 0), f"dw_in{l}")
    if between is not None:
        between(d_win, lw)
    dx, d_nmix = _mm_nt_rmsbwd(dproj, lw["w_in"], x_in, lw["norm_mix"], dxm, min(T, 1024), 2176, f"proj_bwd{l}")
    return dx, dict(w_in=d_win, norm_mix=d_nmix, sg_ln_g=mb[1], sg_ln_b=mb[2], sg_w=mb[3], sg_b=mb[4], cv_w=mb[5], cv_b=mb[6],
                    cv_ln_g=mb[7], cv_ln_b=mb[8], attn_sinks=mb[9], sc_w=mb[10])


ANY = pl.BlockSpec(memory_space=pl.ANY)
BIG = ("w_in", "w_branch", "w_out", "w_gate_up", "w_down")
HALF_SHAPE = {"w_in": (2, W_IN_SHARD // 2, D_MODEL), "w_branch": (2, 1024, 256), "w_out": (2, 128, D_MODEL),
              "w_gate_up": (2, 512, GU_SHARD), "w_down": (2, 352, D_MODEL)}
NB = len(BIG)


def _place():
    x, y, c = lax.axis_index("x"), lax.axis_index("y"), lax.axis_index("c")
    chips = [(1 - x, y), (x, 1 - y), (1 - x, 1 - y)]
    return x, y, c, 2 * x + y, chips, [2 * px + py for px, py in chips]


def _remote(src, dst, ssem, rsem, dev):
    return pltpu.make_async_remote_copy(src_ref=src, dst_ref=dst, send_sem=ssem, recv_sem=rsem, device_id=dev, device_id_type=MESH)


HBM_SPEC = pl.BlockSpec(memory_space=pltpu.HBM)
SEM_SPEC = pl.BlockSpec(memory_space=pltpu.SEMAPHORE)
DATAFLOW = pltpu.SideEffectType.DATAFLOW_SIDE_EFFECTING


def _ici_ends(kind, src, land, j, c, chip, chip_ids):
    if kind == "gather":
        return src.at[c], land.at[chip, c], land.at[chip_ids[j], c]
    return src.at[chip_ids[j]], land.at[chip], land.at[chip_ids[j]]


def _ici_start(kind, srcs, land_shapes, name):
    n = len(srcs)

    def body(*refs):
        src, land = refs[:n], refs[n:2 * n]
        ssem, rsem, token = refs[2 * n], refs[2 * n + 1], refs[-1]
        x, y, c, chip, chips, chip_ids = _place()
        for k in range(n):
            for j in range(3):
                s, d, _ = _ici_ends(kind, src[k], land[k], j, c, chip, chip_ids)
                _remote(s, d, ssem.at[3 * k + j], rsem.at[3 * k + j], (*chips[j], c)).start()
        token[...] = jnp.zeros_like(token)

    sem = pltpu.SemaphoreType.DMA((3 * n,))
    outs = pl.pallas_call(
        body, name=name,
        out_shape=(sem, sem, *[pltpu.HBM(s.shape, s.dtype) for s in srcs], *[pltpu.HBM(sh, BF16) for sh in land_shapes],
                   jax.ShapeDtypeStruct((8, 128), F32)),
        in_specs=[HBM_SPEC] * (2 * n),
        out_specs=(SEM_SPEC, SEM_SPEC, *[HBM_SPEC] * (2 * n), pl.BlockSpec(memory_space=pltpu.VMEM)),
        input_output_aliases={i: 2 + i for i in range(2 * n)},
        compiler_params=pltpu.CompilerParams(has_side_effects=DATAFLOW),
    )(*[pltpu.with_memory_space_constraint(s, pltpu.HBM) for s in srcs],
      *[pltpu.with_memory_space_constraint(lax.empty(sh, BF16), pltpu.HBM) for sh in land_shapes])
    return (kind, outs[0], outs[1], list(outs[2:2 + n]), list(outs[2 + n:2 + 2 * n])), outs[-1]


def _ici_wait(handle, after, name):
    kind, ssem_in, rsem_in, srcs, lands = handle
    n = len(srcs)

    def body(*refs):
        src, land = refs[:n], refs[n:2 * n]
        ssem, rsem = refs[2 * n], refs[2 * n + 1]
        x, y, c, chip, chips, chip_ids = _place()
        for k in range(n):
            for j in range(3):
                s, _, mine = _ici_ends(kind, src[k], land[k], j, c, chip, chip_ids)
                cp = _remote(s, mine, ssem.at[3 * k + j], rsem.at[3 * k + j], (*chips[j], c))
                cp.wait_send()
                cp.wait_recv()

    outs = pl.pallas_call(
        body, name=name, out_shape=[pltpu.HBM(t.shape, t.dtype) for t in srcs + lands],
        in_specs=[HBM_SPEC] * (2 * n) + [SEM_SPEC, SEM_SPEC, ANY], out_specs=[HBM_SPEC] * (2 * n),
        input_output_aliases={i: i for i in range(2 * n)},
        compiler_params=pltpu.CompilerParams(has_side_effects=DATAFLOW),
    )(*srcs, *lands, ssem_in, rsem_in, after)
    return list(outs[:n]), list(outs[n:])


def _ag_pair(shards, lands, name):
    n = len(shards)

    def body(*refs):
        ins, outs = refs[:n], refs[2 * n:3 * n]
        token = refs[3 * n]
        s_fwd, r_fwd, s_own, r_own = refs[3 * n + 1:]
        x, y, c, chip, chips, chip_ids = _place()
        sib = (x, y, 1 - c)
        cps = []
        for k in range(n):
            cp = _remote(ins[k], outs[k].at[chip], s_own.at[k], r_own.at[k], sib)
            cp.start()
            cps.append(cp)
            for j in range(3):
                got = outs[k].at[chip_ids[j], c]
                cp = _remote(got, got, s_fwd.at[k, j], r_fwd.at[k, j], sib)
                cp.start()
                cps.append(cp)
        for k in range(n):
            _remote(ins[k], outs[k].at[chip], s_own.at[k], r_own.at[k], sib).wait_recv()
            for j in range(3):
                got = outs[k].at[chip_ids[j], 1 - c]
                _remote(got, got, s_fwd.at[k, j], r_fwd.at[k, j], sib).wait_recv()
        for cp in cps:
            cp.wait_send()
        token[...] = jnp.zeros_like(token)

    sem, sem1 = pltpu.SemaphoreType.DMA((n, 3)), pltpu.SemaphoreType.DMA((n,))
    outs = pl.pallas_call(
        body, name=name, out_shape=[jax.ShapeDtypeStruct(t.shape, t.dtype) for t in lands] + [jax.ShapeDtypeStruct((8, 128), F32)],
        in_specs=[ANY] * (2 * n), out_specs=[ANY] * n + [pl.BlockSpec(memory_space=pltpu.VMEM)],
        input_output_aliases={n + k: k for k in range(n)},
        scratch_shapes=[sem, sem, sem1, sem1], compiler_params=pltpu.CompilerParams(has_side_effects=True),
    )(*shards, *lands)
    return list(outs[:n]), outs[n]


def _forward_plan(n):
    def plan(refs, c, chip, chip_ids):
        out = []
        for k in range(n):
            shard, land = refs[k], refs[n + k]
            out.append((shard, land.at[chip], land.at[chip]))
            out += [(land.at[q, c], land.at[q, c], land.at[q, 1 - c]) for q in chip_ids]
        return out
    return plan, 4 * n


def _swap_plan(n):
    def plan(refs, c, chip, chip_ids):
        return [(refs[k].at[q, 1 - c], refs[n + k].at[q], refs[n + k].at[q]) for k in range(n) for q in range(N_CHIPS)]
    return plan, N_CHIPS * n


def _d2d_start(arrays, new_shapes, plan_n, name):
    plan, n_copies = plan_n
    n = len(arrays) + len(new_shapes)

    def body(*refs):
        ssem, rsem, token = refs[n], refs[n + 1], refs[-1]
        x, y, c, chip, _, chip_ids = _place()
        for i, (s, d, _) in enumerate(plan(refs[:n], c, chip, chip_ids)):
            _remote(s, d, ssem.at[i], rsem.at[i], (x, y, 1 - c)).start()
        token[...] = jnp.zeros_like(token)

    sem = pltpu.SemaphoreType.DMA((n_copies,))
    args = [pltpu.with_memory_space_constraint(t, pltpu.HBM) for t in arrays] + \
           [pltpu.with_memory_space_constraint(lax.empty(sh, BF16), pltpu.HBM) for sh in new_shapes]
    outs = pl.pallas_call(
        body, name=name,
        out_shape=(sem, sem, *[pltpu.HBM(t.shape, t.dtype) for t in args], jax.ShapeDtypeStruct((8, 128), F32)),
        in_specs=[HBM_SPEC] * n, out_specs=(SEM_SPEC, SEM_SPEC, *[HBM_SPEC] * n, pl.BlockSpec(memory_space=pltpu.VMEM)),
        input_output_aliases={i: 2 + i for i in range(n)},
        compiler_params=pltpu.CompilerParams(has_side_effects=DATAFLOW),
    )(*args)
    return (plan, outs[0], outs[1], list(outs[2:2 + n])), outs[-1]


def _d2d_wait(handle, after, name):
    plan, ssem_in, rsem_in, arrays = handle
    n = len(arrays)

    def body(*refs):
        ssem, rsem = refs[n], refs[n + 1]
        x, y, c, chip, _, chip_ids = _place()
        for i, (s, _, mine) in enumerate(plan(refs[:n], c, chip, chip_ids)):
            cp = _remote(s, mine, ssem.at[i], rsem.at[i], (x, y, 1 - c))
            cp.wait_send()
            cp.wait_recv()

    outs = pl.pallas_call(
        body, name=name, out_shape=[pltpu.HBM(t.shape, t.dtype) for t in arrays],
        in_specs=[HBM_SPEC] * n + [SEM_SPEC, SEM_SPEC, ANY], out_specs=[HBM_SPEC] * n,
        input_output_aliases={i: i for i in range(n)},
        compiler_params=pltpu.CompilerParams(has_side_effects=DATAFLOW),
    )(*arrays, ssem_in, rsem_in, after)
    return list(outs)


def _rs_pair(grads, name):
    n_arr = len(grads)

    def body(*refs):
        ins, got = refs[:n_arr], refs[n_arr:2 * n_arr]
        ssem, rsem = refs[2 * n_arr:]
        x, y, c, _, _, _ = _place()
        sib = (x, y, 1 - c)
        sends = []
        for k in reversed(range(n_arr)):
            for q in range(N_CHIPS):
                cp = _remote(ins[k].at[q, 1 - c], got[k].at[q], ssem.at[k, q], rsem.at[k, q], sib)
                cp.start()
                sends.append(cp)
        for k in range(n_arr):
            for q in range(N_CHIPS):
                _remote(got[k].at[q], got[k].at[q], ssem.at[k, q], rsem.at[k, q], sib).wait_recv()
        for cp in sends:
            cp.wait_send()

    shp = [jax.ShapeDtypeStruct((N_CHIPS,) + g.shape[2:], BF16) for g in grads]
    sem = pltpu.SemaphoreType.DMA((n_arr, N_CHIPS))
    outs = pl.pallas_call(
        body, name=name, out_shape=shp, in_specs=[ANY] * n_arr, out_specs=[ANY] * n_arr,
        scratch_shapes=[sem, sem], compiler_params=pltpu.CompilerParams(has_side_effects=True),
    )(*grads)
    return list(outs)


def _rs_share(bufs, name):
    n = len(bufs)

    def body(*refs):
        outs = refs[n:2 * n]
        ssem, rsem = refs[2 * n:]
        x, y, c, _, _, _ = _place()
        sib = (x, y, 1 - c)
        sends = []
        for k in range(n):
            for l in range(DEPTH):
                cp = _remote(outs[k].at[l, c], outs[k].at[l, c], ssem.at[k, l], rsem.at[k, l], sib)
                cp.start()
                sends.append(cp)
        for k in range(n):
            for l in range(DEPTH):
                dst = outs[k].at[l, 1 - c]
                _remote(dst, dst, ssem.at[k, l], rsem.at[k, l], sib).wait_recv()
        for cp in sends:
            cp.wait_send()

    sem = pltpu.SemaphoreType.DMA((n, DEPTH))
    outs = pl.pallas_call(
        body, name=name, out_shape=[jax.ShapeDtypeStruct(b.shape, b.dtype) for b in bufs], in_specs=[ANY] * n, out_specs=[ANY] * n,
        input_output_aliases={k: k for k in range(n)},
        scratch_shapes=[sem, sem], compiler_params=pltpu.CompilerParams(has_side_effects=True),
    )(*bufs)
    return list(outs)


def _piece(src, idx, rows, width=128, align=1, transposed=False):
    return dict(src=src, idx=idx, rows=rows, width=width, align=align, transposed=transposed)


def _all_reduce_pieces(inputs, pieces, out_shapes, writes, name):
    n_in, n_out = len(inputs), len(out_shapes)
    offs, R = [], 0
    for p in pieces:
        R = -(-R // p["align"]) * p["align"]
        offs.append(R)
        R += p["rows"]
    R = -(-R // 8) * 8

    def body(*refs):
        ins, outs, token_ref = refs[:n_in], refs[n_in:n_in + n_out], refs[n_in + n_out]
        pair_ref, chip_ref, sum_ref, ssem, rsem = refs[n_in + n_out + 1:]
        token_ref[...] = jnp.zeros_like(token_ref)
        x, y, c, chip, chips, chip_ids = _place()
        pair_ref[c] = jnp.zeros((R, 128), F32)
        for p, off in zip(pieces, offs):
            v = ins[p["src"]][...].T[p["idx"]] if p["transposed"] else ins[p["src"]][p["idx"]]
            pair_ref[c, off:off + p["rows"], 0:p["width"]] = v
        mine = _remote(pair_ref.at[c], pair_ref.at[c], ssem.at[3], rsem.at[3], (x, y, 1 - c))
        mine.start()
        _remote(pair_ref.at[1 - c], pair_ref.at[1 - c], ssem.at[3], rsem.at[3], (x, y, 1 - c)).wait_recv()
        chip_ref[chip] = pair_ref[0] + pair_ref[1]
        cps = [_remote(chip_ref.at[chip], chip_ref.at[chip], ssem.at[j], rsem.at[j], (*chips[j], c)) for j in range(3)]
        for cp in cps:
            cp.start()
        for j in range(3):
            slot = chip_ref.at[chip_ids[j]]
            _remote(slot, slot, ssem.at[j], rsem.at[j], (*chips[j], c)).wait_recv()
        acc = chip_ref[0]
        for s in range(1, N_CHIPS):
            acc = acc + chip_ref[s]
        sum_ref[...] = acc
        for o, idx, p in writes:
            outs[o][idx] = sum_ref[offs[p]:offs[p] + pieces[p]["rows"], 0:pieces[p]["width"]]
        for cp in cps + [mine]:
            cp.wait_send()

    vm = pl.BlockSpec(memory_space=pltpu.VMEM)
    outs = pl.pallas_call(
        body, name=name, out_shape=[jax.ShapeDtypeStruct(s, F32) for s in out_shapes] + [jax.ShapeDtypeStruct((8, 128), F32)],
        in_specs=[vm] * n_in, out_specs=[vm] * (n_out + 1),
        scratch_shapes=[pltpu.VMEM((2, R, 128), F32), pltpu.VMEM((N_CHIPS, R, 128), F32), pltpu.VMEM((R, 128), F32),
                        pltpu.SemaphoreType.DMA((4,)), pltpu.SemaphoreType.DMA((4,))],
        compiler_params=pltpu.CompilerParams(vmem_limit_bytes=VMEM_LIMIT),
    )(*inputs)
    return list(outs[:n_out]), outs[n_out]


def _lanes(width):
    return [slice(k, min(k + 128, width)) for k in range(0, width, 128)]


def _gather_small_weights(cvw_z, scw_z):
    pieces, writes = [], []
    for i, arr in enumerate((cvw_z, scw_z)):
        for l in range(DEPTH):
            for ln in _lanes(HALF):
                writes.append((i, (l, slice(None), ln), len(pieces)))
                pieces.append(_piece(i, (l, slice(None), ln), arr.shape[1], align=8))
    (cvw, scw), tok = _all_reduce_pieces([cvw_z, scw_z], pieces, [cvw_z.shape, scw_z.shape], writes, "ag_small")
    return cvw, scw, tok


SMALL_RAW = dict(norm_mix=(1, D_MODEL), norm_ffn=(1, D_MODEL), sg_ln_g=(1, HALF), sg_ln_b=(1, HALF), cv_b=(1, HALF), cv_ln_g=(1, HALF),
                 cv_ln_b=(1, HALF))


def _all_reduce_small_grads(raw, d_nfinal, loss):
    names = list(SMALL_RAW) + ["attn_sinks", "sg_b", "sc_w", "cv_w", "sg_w"]
    out_shape = dict(norm_mix=(DEPTH, D_MODEL), norm_ffn=(DEPTH, D_MODEL), sg_ln_g=(DEPTH, HALF), sg_ln_b=(DEPTH, HALF), cv_b=(DEPTH, HALF),
                     cv_ln_g=(DEPTH, HALF), cv_ln_b=(DEPTH, HALF), attn_sinks=(DEPTH, N_Q_HEADS), sg_b=(DEPTH, SG_GROUPS, SG_CHUNK),
                     sc_w=(DEPTH, SC_KERNEL, HALF), cv_w=(DEPTH, CV_KERNEL, HALF), sg_w=(DEPTH, SG_GROUPS, SG_CHUNK, SG_CHUNK))
    inputs, pieces, writes = [], [], []

    def add(src, idx, rows, out, out_idx, **kw):
        writes.append((names.index(out) if out in names else out, out_idx, len(pieces)))
        pieces.append(_piece(src, idx, rows, **kw))

    for l in range(DEPTH):
        row = slice(l, l + 1)
        for n, (_, width) in SMALL_RAW.items():
            inputs.append(raw[l][n])
            for ln in _lanes(width):
                add(len(inputs) - 1, (slice(0, 1), ln), 1, n, (row, ln))
        inputs.append(raw[l]["attn_sinks"])
        add(len(inputs) - 1, (slice(0, 1), slice(0, N_Q_HEADS)), 1, "attn_sinks", (row, slice(None)), width=N_Q_HEADS)
    for l in range(DEPTH):
        inputs.append(raw[l]["sg_b"])
        add(len(inputs) - 1, (slice(0, SG_GROUPS), slice(None)), SG_GROUPS, "sg_b", (l,), align=8, transposed=True)
        inputs.append(raw[l]["sc_w"])
        for ln in _lanes(HALF):
            add(len(inputs) - 1, (slice(0, SC_KERNEL), ln), SC_KERNEL, "sc_w", (l, slice(None), ln), align=8)
        inputs.append(raw[l]["cv_w"])
        for ln in _lanes(HALF):
            add(len(inputs) - 1, (slice(0, CV_KERNEL), ln), CV_KERNEL, "cv_w", (l, slice(None), ln), align=8)
        inputs.append(raw[l]["sg_w"])
        for g in range(SG_GROUPS):
            add(len(inputs) - 1, (g,), SG_CHUNK, "sg_w", (l, g), align=8)
    n_names = len(names)
    inputs.append(d_nfinal)
    for ln in _lanes(D_MODEL):
        add(len(inputs) - 1, (slice(0, 1), ln), 1, n_names, (slice(0, 1), ln))
    inputs.append(loss)
    add(len(inputs) - 1, (slice(0, 1), slice(None)), 1, n_names + 1, (slice(0, 1), slice(None)))
    outs, tok = _all_reduce_pieces(inputs, pieces, [out_shape[n] for n in names] + [(1, D_MODEL), (1, 128)], writes, "ar_small")
    return dict(zip(names, outs[:n_names])), outs[n_names], outs[n_names + 1], tok


def _small_views(raw):
    v = {n: raw[n][0] for n in SMALL_RAW}
    v.update(sg_w=raw["sg_w"], sg_b=raw["sg_b"][:, :SG_GROUPS].T, cv_w=raw["cv_w"][:CV_KERNEL],
             attn_sinks=raw["attn_sinks"][0, :N_Q_HEADS], sc_w=raw["sc_w"][:SC_KERNEL])
    return v


def _row_tile(rows, cols, n_arrays):
    budget = 20 * 1024 * 1024 // (n_arrays * 2 * cols * 4)
    tiles = [t for t in range(16, min(rows, budget) + 1, 16) if rows % t == 0]
    assert tiles, (rows, cols)
    return tiles[-1]


def _add_pairs(g, got, place, name):
    _, _, rows, cols = g.shape
    tr = _row_tile(rows, cols, 3)

    def body(place_ref, a_ref, b_ref, o_ref):
        del place_ref
        o_ref[...] = (a_ref[...].astype(F32) + b_ref[...].astype(F32)).astype(BF16)

    spec = pl.BlockSpec((None, tr, cols), lambda q, i, p: (q, i, 0))
    grid_spec = pltpu.PrefetchScalarGridSpec(
        num_scalar_prefetch=1, grid=(N_CHIPS, rows // tr),
        in_specs=[pl.BlockSpec((None, None, tr, cols), lambda q, i, p: (q, p[1], i, 0)), spec], out_specs=spec)
    return pl.pallas_call(body, name=name, grid_spec=grid_spec, out_shape=jax.ShapeDtypeStruct((N_CHIPS, rows, cols), BF16),
                          compiler_params=_params("parallel", "parallel"))(place, g, got)


def _sum_chips(own, recv, place, l, buf, name, after):
    _, rows, cols = own.shape
    tr = _row_tile(rows, cols, 4)

    def body(place_ref, own_ref, recv_ref, *rest):
        chip = place_ref[0]
        acc = own_ref[...].astype(F32)
        for j in range(1, N_CHIPS):
            acc = acc + recv_ref[lax.rem(chip + j, N_CHIPS)].astype(F32)
        rest[-1][...] = acc

    in_specs = [pl.BlockSpec((None, tr, cols), lambda i, p: (p[0], i, 0)), pl.BlockSpec((N_CHIPS, tr, cols), lambda i, p: (0, i, 0)), ANY]
    args = [place, own, recv, after]
    aliases = {}
    if buf is not None:
        in_specs.append(ANY)
        args.append(buf)
        aliases = {4: 0}
    grid_spec = pltpu.PrefetchScalarGridSpec(
        num_scalar_prefetch=1, grid=(rows // tr,), in_specs=in_specs,
        out_specs=pl.BlockSpec((None, None, tr, cols), lambda i, p: (l, p[1], i, 0)))
    return pl.pallas_call(body, name=name, grid_spec=grid_spec, out_shape=jax.ShapeDtypeStruct((DEPTH, 2, rows, cols), F32),
                          input_output_aliases=aliases, compiler_params=_params("parallel"))(*args)


def _adamw(w, g, m, v, name):
    shape = w.shape
    lead, (rows, cols) = shape[:-2], shape[-2:]
    tr = _row_tile(rows, cols, 8)

    def body(w_ref, g_ref, m_ref, v_ref, go_ref, d_ref, mo_ref, vo_ref):
        gv = g_ref[...]
        go_ref[...] = gv
        mn = ADAM_B1 * m_ref[...] + (1.0 - ADAM_B1) * gv
        vn = ADAM_B2 * v_ref[...] + (1.0 - ADAM_B2) * (gv * gv)
        m_hat = mn / (1.0 - ADAM_B1 ** ADAM_STEP)
        v_hat = vn / (1.0 - ADAM_B2 ** ADAM_STEP)
        d_ref[...] = -ADAM_LR * (m_hat / (jnp.sqrt(v_hat) + ADAM_EPS) + ADAM_WD * w_ref[...])
        mo_ref[...] = mn
        vo_ref[...] = vn

    spec = pl.BlockSpec((None,) * len(lead) + (tr, cols), lambda *idx: (*idx, 0))
    grid = lead + (rows // tr,)
    return list(pl.pallas_call(body, name=name, grid=grid, in_specs=[spec] * 4, out_specs=[spec] * 4,
                               out_shape=[jax.ShapeDtypeStruct(shape, F32)] * 4,
                               compiler_params=_params(*(["parallel"] * len(grid))))(w, g, m, v))


def _adamw_small(ws, gs, ms, vs, name):
    n = len(ws)

    def body(*refs):
        for i in range(n):
            gv = refs[n + i][...]
            mn = ADAM_B1 * refs[2 * n + i][...] + (1.0 - ADAM_B1) * gv
            vn = ADAM_B2 * refs[3 * n + i][...] + (1.0 - ADAM_B2) * (gv * gv)
            m_hat = mn / (1.0 - ADAM_B1 ** ADAM_STEP)
            v_hat = vn / (1.0 - ADAM_B2 ** ADAM_STEP)
            refs[4 * n + i][...] = -ADAM_LR * (m_hat / (jnp.sqrt(v_hat) + ADAM_EPS) + ADAM_WD * refs[i][...])
            refs[5 * n + i][...] = mn
            refs[6 * n + i][...] = vn

    vm = pl.BlockSpec(memory_space=pltpu.VMEM)
    outs = pl.pallas_call(body, name=name, out_shape=[jax.ShapeDtypeStruct(t.shape, F32) for t in ws] * 3,
                          in_specs=[vm] * (4 * n), out_specs=[vm] * (3 * n),
                          compiler_params=pltpu.CompilerParams(vmem_limit_bytes=VMEM_LIMIT))(*ws, *gs, *ms, *vs)
    return outs[:n], outs[n:2 * n], outs[2 * n:]


SMALL = ("norm_mix", "sg_ln_g", "sg_ln_b", "sg_w", "sg_b", "cv_w", "cv_b", "cv_ln_g", "cv_ln_b", "attn_sinks", "sc_w", "norm_ffn", "norm_final")
ORDER = ("norm_mix", "w_in", "sg_ln_g", "sg_ln_b", "sg_w", "sg_b", "cv_w", "cv_b", "cv_ln_g", "cv_ln_b", "attn_sinks", "sc_w",
         "w_branch", "w_out", "norm_ffn", "w_gate_up", "w_down", "norm_final")


def kernel(x, norm_mix, w_in, sg_ln_g, sg_ln_b, sg_w, sg_b, cv_w, cv_b, cv_ln_g, cv_ln_b, attn_sinks, sc_w, w_branch, w_out, norm_ffn, w_gate_up, w_down, norm_final, loss_target, m_norm_mix, m_w_in, m_sg_ln_g, m_sg_ln_b, m_sg_w, m_sg_b, m_cv_w, m_cv_b, m_cv_ln_g, m_cv_ln_b, m_attn_sinks, m_sc_w, m_w_branch, m_w_out, m_norm_ffn, m_w_gate_up, m_w_down, m_norm_final, v_norm_mix, v_w_in, v_sg_ln_g, v_sg_ln_b, v_sg_w, v_sg_b, v_cv_w, v_cv_b, v_cv_ln_g, v_cv_ln_b, v_attn_sinks, v_sc_w, v_w_branch, v_w_out, v_norm_ffn, v_w_gate_up, v_w_down, v_norm_final):
    W = dict(norm_mix=norm_mix, w_in=w_in, sg_ln_g=sg_ln_g, sg_ln_b=sg_ln_b, sg_w=sg_w, sg_b=sg_b, cv_w=cv_w, cv_b=cv_b, cv_ln_g=cv_ln_g,
             cv_ln_b=cv_ln_b, attn_sinks=attn_sinks, sc_w=sc_w, w_branch=w_branch, w_out=w_out, norm_ffn=norm_ffn, w_gate_up=w_gate_up,
             w_down=w_down, norm_final=norm_final)
    M = dict(norm_mix=m_norm_mix, w_in=m_w_in, sg_ln_g=m_sg_ln_g, sg_ln_b=m_sg_ln_b, sg_w=m_sg_w, sg_b=m_sg_b, cv_w=m_cv_w, cv_b=m_cv_b,
             cv_ln_g=m_cv_ln_g, cv_ln_b=m_cv_ln_b, attn_sinks=m_attn_sinks, sc_w=m_sc_w, w_branch=m_w_branch, w_out=m_w_out,
             norm_ffn=m_norm_ffn, w_gate_up=m_w_gate_up, w_down=m_w_down, norm_final=m_norm_final)
    V = dict(norm_mix=v_norm_mix, w_in=v_w_in, sg_ln_g=v_sg_ln_g, sg_ln_b=v_sg_ln_b, sg_w=v_sg_w, sg_b=v_sg_b, cv_w=v_cv_w, cv_b=v_cv_b,
             cv_ln_g=v_cv_ln_g, cv_ln_b=v_cv_ln_b, attn_sinks=v_attn_sinks, sc_w=v_sc_w, w_branch=v_w_branch, w_out=v_w_out,
             norm_ffn=v_norm_ffn, w_gate_up=v_w_gate_up, w_down=v_w_down, norm_final=v_norm_final)
    mx, my, mc = lax.axis_index("x"), lax.axis_index("y"), lax.axis_index("c")
    chip = 2 * mx + my

    place = jnp.stack([chip, mc]).astype(jnp.int32)
    tables = _rope_tables(x.shape[1])
    land_shapes = [(N_CHIPS,) + HALF_SHAPE[n] for n in BIG]
    part_shapes = {n: (N_CHIPS,) + HALF_SHAPE[n][1:] for n in BIG}

    T_ = lambda t: jnp.swapaxes(t, 1, 2)
    Wt, Mt, Vt = ({**t, "w_in": T_(t["w_in"])} for t in (W, M, V))

    def shards_of(l, tok):
        return [(Wt[n][l] + tok[0, 0]).astype(BF16).reshape(HALF_SHAPE[n]) for n in BIG]

    def finish_gather(tag, handle, after):
        srcs, lands = _ici_wait(handle, after, f"ag_wait{tag}")
        return _ag_pair(srcs, lands, f"ag_pair{tag}")[0]

    def mix_weights(l, g_in):
        return dict(w_in=_w_in_layout(g_in[0].reshape(N_CHIPS, W_IN_SHARD, D_MODEL)), norm_mix=norm_mix[l][None], norm_ffn=norm_ffn[l][None],
                    mixer=_mixer_params(l, sg_ln_g, sg_ln_b, sg_w, sg_b, cvw_full, cv_b, cv_ln_g, cv_ln_b, attn_sinks, scw_full))

    def rest_weights(lw, g_rest):
        G = dict(zip(BIG[1:], g_rest))
        lw.update(w_branch=G["w_branch"].reshape(N_CHIPS, N_BRANCH, HALF, 256), w_out=G["w_out"].reshape(D_MODEL, D_MODEL),
                  w_gate_up=G["w_gate_up"].reshape(N_CHIPS, D_MODEL, GU_SHARD), w_down=G["w_down"].reshape(D_FF, D_MODEL))

    def shard_major(g):
        t = dict(g)
        if "w_in" in t:
            t["w_in"] = _w_in_unlayout(t["w_in"])
        return {n: t[n].reshape((N_CHIPS,) + HALF_SHAPE[n]) for n in BIG if n in t}

    zero_tok = jnp.zeros((8, 128), F32)
    south = (mc == 0).astype(F32)
    cvw_z = lax.dynamic_update_slice(jnp.zeros((DEPTH, CV_KERNEL, HALF), F32), cv_w * south, (0, 0, chip * 128))
    scw_z = lax.dynamic_update_slice(jnp.zeros((DEPTH, SC_KERNEL, HALF), F32), sc_w * south, (0, 0, chip * 128))
    cvw_full, scw_full, tok = _gather_small_weights(cvw_z, scw_z)
    handles = []
    for l in range(DEPTH):
        for tag, sl in (("in", slice(0, 1)), ("rest", slice(1, NB))):
            h, tok = _ici_start("gather", shards_of(l, tok)[sl], land_shapes[sl], f"ag_start{l}{tag}")
            handles.append(h)
    pending = {}

    def behind(l, key):
        def order(lw, token):
            if key == "mixer":
                lw["mixer"] = [lw["mixer"][0] + token[0, 0]] + lw["mixer"][1:]
            else:
                lw[key] = lw[key] + token[0, 0]
        return order

    def early_pair(tag, handle, order):
        def between(after, lw):
            srcs, lands = _ici_wait(handle, after, f"ag_wait{tag}")
            pending[tag], token = _d2d_start(srcs + lands, [], _forward_plan(len(srcs)), f"ag_pair_start{tag}")
            order(lw, token)
        return between

    def finish_pair(tag, after):
        arrays = _d2d_wait(pending.pop(tag), after, f"ag_pair_wait{tag}")
        return arrays[len(arrays) // 2:]

    lw0 = mix_weights(0, finish_gather("0in", handles[0], tok))
    mixed = _fwd_layer_mix(0, x[0], lw0, tables)
    rest_weights(lw0, finish_gather("0rest", handles[1], mixed[2]))
    x1, sv0 = _fwd_layer_rest(0, x[0], mixed, lw0, early_pair("1in", handles[2], behind(0, "norm_ffn")))
    lw1 = mix_weights(1, finish_pair("1in", x1))
    mixed = _fwd_layer_mix(1, x1, lw1, tables, early_pair("1rest", handles[3], behind(1, "mixer")))
    rest_weights(lw1, finish_pair("1rest", mixed[2]))
    x2, sv1 = _fwd_layer_rest(1, x1, mixed, lw1)
    dx, d_nfinal, loss = _final_loss(x2, norm_final[None], loss_target[0], 256, "final_loss")

    lw1["after"] = zero_tok
    carry, g_ffn1 = _bwd_layer_ffn(1, dx, lw1, sv1)
    g1 = shard_major(g_ffn1)
    names_f = list(g1)
    h_swap, tok = _d2d_start([g1[n] for n in names_f], [part_shapes[n] for n in names_f], _swap_plan(len(names_f)), "rs_pair_start1")
    behind(1, "mixer")(lw1, tok)
    def early_swap(tag):
        def between(d_win, lw):
            g = shard_major({"w_in": d_win})["w_in"]
            pending[tag], token = _d2d_start([g], [part_shapes["w_in"]], _swap_plan(1), f"rs_pair_start{tag}")
            behind(None, "norm_mix")(lw, token)
        return between

    dx, g_mix1 = _bwd_layer_mix(1, carry, lw1, sv1, tables, early_swap("1in"))
    swapped = _d2d_wait(h_swap, dx, "rs_pair_wait1")
    own_in, got_in = _d2d_wait(pending.pop("1in"), dx, "rs_pair_wait1in")
    names1 = ["w_in"] + names_f
    own1 = [own_in] + swapped[:len(names_f)]
    got1 = [got_in] + swapped[len(names_f):]
    part1 = [_add_pairs(own1[k], got1[k], place, f"rs_add1_{n}") for k, n in enumerate(names1)]
    hr1, tok = _ici_start("scatter", part1, [part_shapes[n] for n in names1], "rs_start1")

    def early_ffn_swap(grads, lw):
        g = shard_major(grads)
        pending["0ffn"], token = _d2d_start([g[n] for n in g], [part_shapes[n] for n in g], _swap_plan(len(g)), "rs_pair_start0ffn")
        behind(None, "norm_ffn")(lw, token)

    lw0["after"] = tok
    carry, g_ffn0 = _bwd_layer_ffn(0, dx, lw0, sv0, early_ffn_swap)
    g0 = shard_major({n: g_ffn0[n] for n in ("w_branch", "w_out")})
    names_a = list(g0) + ["w_gate_up", "w_down"]
    swapped = _d2d_wait(pending.pop("0ffn"), g_ffn0["w_branch"], "rs_pair_wait0ffn")
    own_a = [g0[n] for n in g0] + swapped[:2]
    got_a = _rs_pair([g0[n] for n in g0], "rs_pair0a") + swapped[2:]
    part_a = [_add_pairs(own_a[k], got_a[k], place, f"rs_add0a_{n}") for k, n in enumerate(names_a)]
    _, recv1 = _ici_wait(hr1, part_a[0], "rs_wait1")
    hra, tok = _ici_start("scatter", part_a, [part_shapes[n] for n in names_a], "rs_start0a")

    lw0["mixer"] = [lw0["mixer"][0] + tok[0, 0]] + lw0["mixer"][1:]
    dx, g_mix0 = _bwd_layer_mix(0, carry, lw0, sv0, tables, early_swap("0in"))
    _, recv_a = _ici_wait(hra, dx, "rs_wait0a")

    small_red, nf_red, loss_red, tok = _all_reduce_small_grads([{**g_ffn0, **g_mix0}, {**g_ffn1, **g_mix1}], d_nfinal, loss)
    small_red["norm_final"] = nf_red
    loss_out = loss_red[0, 0]
    for n in ("cv_w", "sc_w"):
        small_red[n] = lax.dynamic_slice_in_dim(small_red[n], chip * 128, 128, axis=2)

    own_in, got_in = _d2d_wait(pending.pop("0in"), tok, "rs_pair_wait0in")
    names_b, part_b = ["w_in"], [_add_pairs(own_in, got_in, place, "rs_add0b_w_in")]
    hrb, tok = _ici_start("scatter", part_b, [part_shapes[n] for n in names_b], "rs_start0b")
    bufs = {n: _sum_chips(part1[k], recv1[k], place, 1, None, f"rs_sum1_{n}", tok) for k, n in enumerate(names1)}
    for k, n in enumerate(names_a):
        bufs[n] = _sum_chips(part_a[k], recv_a[k], place, 0, bufs[n], f"rs_sum0_{n}", tok)
    shared = dict(zip(names_a, _rs_share([bufs[n] for n in names_a], "rs_share_a")))
    upd = {}
    for n in names_a:
        red = shared[n].reshape(W[n].shape)
        upd[n] = _adamw(W[n], red, M[n], V[n], f"adamw_{n}")
    two_d = lambda t: t[None] if t.ndim == 1 else t
    small_upd = _adamw_small(*([two_d(t[n]) for n in SMALL] for t in (W, small_red, M, V)), "adamw_small")
    for n, d, mo, vo in zip(SMALL, *small_upd):
        upd[n] = [t.reshape(W[n].shape) for t in (small_red[n], d, mo, vo)]

    _, recv_b = _ici_wait(hrb, upd[names_a[-1]][1], "rs_wait0b")
    for k, n in enumerate(names_b):
        bufs[n] = _sum_chips(part_b[k], recv_b[k], place, 0, bufs[n], f"rs_sum0_{n}", tok)
    shared = dict(zip(names_b, _rs_share([bufs[n] for n in names_b], "rs_share_b")))
    for n in names_b:
        red = shared[n].reshape(Wt[n].shape)
        upd[n] = [T_(t) for t in _adamw(Wt[n], red, Mt[n], Vt[n], f"adamw_{n}")]

    out = [loss_out, dx[None]]
    for k in range(4):
        out += [upd[n][k] for n in ORDER]
    return tuple(out)
```

```python
import functools
import math

import jax
import jax.numpy as jnp
from jax import lax
from jax.experimental import pallas as pl
from jax.experimental.pallas import tpu as pltpu

F32 = jnp.float32
BF16 = jnp.bfloat16

D_MODEL = 1024
DEPTH = 2
HALF = 512
SG_CHUNK = 128
SG_GROUPS = 4
CV_KERNEL = 31
HEAD_DIM = 64
N_Q_HEADS = 8
N_KV_HEADS = 2
Q_PER_KV = N_Q_HEADS // N_KV_HEADS
WINDOW = 128
ROPE_THETA = 10000.0
SC_KERNEL = 3
N_BRANCH = 4
D_FF = 2816
EPS = 1e-6
N_CHIPS = 4
N_DEV = 8

MIX_W = 4352
GATE_W = N_BRANCH * D_MODEL
PROJ_PAD = 2 * MIX_W
W_IN_SHARD = 2112
GU_SHARD = 1408
HALO = 128
CV_PAD = 32

ADAM_LR = 0.001
ADAM_B1 = 0.9
ADAM_B2 = 0.999
ADAM_EPS = 1e-08
ADAM_WD = 0.01
ADAM_STEP = 10

VMEM_LIMIT = 56 * 1024 * 1024
INV_SQRT2 = 1.0 / math.sqrt(2.0)
INV_SQRT_2PI = 1.0 / math.sqrt(2.0 * math.pi)
NEG_BIG = -1e30
MESH = pl.DeviceIdType.MESH

C_ZA, C_ZB, C_Q, C_K, C_V, C_ZD = 0, 1024, 2048, 2560, 2688, 2816


def _params(*sem):
    return pltpu.CompilerParams(dimension_semantics=sem, vmem_limit_bytes=VMEM_LIMIT)


def _sig(v):
    return 1.0 / (1.0 + jnp.exp(-v))


def _dot(a, b):
    return jnp.dot(a, b, preferred_element_type=F32)


def _dot_nt(a, b):
    return lax.dot_general(a, b, (((1,), (1,)), ((), ())), preferred_element_type=F32)


def _dot_tn(a, b):
    return lax.dot_general(a, b, (((0,), (0,)), ((), ())), preferred_element_type=F32)


def _full(shape):
    nd = len(shape)
    return pl.BlockSpec(shape, lambda *_: (0,) * nd)


def _rms_mm(x, g, w, tm, tn, name):
    T = x.shape[0]
    transposed = w.ndim == 2
    if transposed:
        N = w.shape[0]
        wspec = pl.BlockSpec((tn, D_MODEL), lambda i, j: (j, 0))
    else:
        tn = w.shape[2]
        N = w.shape[0] * tn
        wspec = pl.BlockSpec((None, D_MODEL, tn), lambda i, j: (j, 0, 0))

    def body(x_ref, g_ref, w_ref, o_ref, xn_ref):
        @pl.when(pl.program_id(1) == 0)
        def _():
            xv = x_ref[...]
            r = lax.rsqrt(jnp.mean(xv * xv, axis=-1, keepdims=True) + EPS)
            xn_ref[...] = (xv * r * g_ref[...]).astype(BF16)

        o_ref[...] = (_dot_nt if transposed else _dot)(xn_ref[...], w_ref[...]).astype(BF16)

    return pl.pallas_call(
        body, name=name, grid=(T // tm, N // tn),
        in_specs=[pl.BlockSpec((tm, D_MODEL), lambda i, j: (i, 0)), _full((1, D_MODEL)), wspec],
        out_specs=[pl.BlockSpec((tm, tn), lambda i, j: (i, j)), pl.BlockSpec((tm, D_MODEL), lambda i, j: (i, 0))],
        out_shape=[jax.ShapeDtypeStruct((T, N), BF16), jax.ShapeDtypeStruct((T, D_MODEL), BF16)],
        compiler_params=_params("parallel", "arbitrary"),
    )(x, g, w)


def _merge_fwd(x, ys, proj, wb, wo, tm, name):
    T = x.shape[0]

    def body(x_ref, ys_ref, zg_ref, wb_ref, wo_ref, xo_ref, mg_ref):
        merged = None
        for n in range(N_BRANCH):
            yn = ys_ref[:, n * HALF:(n + 1) * HALF]
            br = jnp.concatenate([_dot(yn, wb_ref[s, n]) for s in range(N_CHIPS)], axis=1)
            t = _sig(zg_ref[:, n * D_MODEL:(n + 1) * D_MODEL].astype(F32)) * br
            merged = t if merged is None else merged + t
        mb = merged.astype(BF16)
        mg_ref[...] = mb
        xo_ref[...] = x_ref[...] + _dot(mb, wo_ref[...])

    return pl.pallas_call(
        body, name=name, grid=(T // tm,),
        in_specs=[pl.BlockSpec((tm, D_MODEL), lambda i: (i, 0)), pl.BlockSpec((tm, N_BRANCH * HALF), lambda i: (i, 0)),
                  pl.BlockSpec((tm, GATE_W), lambda i: (i, 0)), _full(wb.shape), _full(wo.shape)],
        out_specs=[pl.BlockSpec((tm, D_MODEL), lambda i: (i, 0)), pl.BlockSpec((tm, D_MODEL), lambda i: (i, 0))],
        out_shape=[jax.ShapeDtypeStruct((T, D_MODEL), F32), jax.ShapeDtypeStruct((T, D_MODEL), BF16)],
        compiler_params=_params("parallel"),
    )(x, ys, proj, wb, wo)


def _ffn_down(xm, gu, wd, tm, name):
    T = xm.shape[0]

    def body(x_ref, gu_ref, wd_ref, o_ref):
        g = gu_ref[:, :D_FF].astype(F32)
        u = gu_ref[:, D_FF:].astype(F32)
        act = (g * _sig(g) * u).astype(BF16)
        o_ref[...] = x_ref[...] + _dot(act, wd_ref[...])

    return pl.pallas_call(
        body, name=name, grid=(T // tm,),
        in_specs=[pl.BlockSpec((tm, D_MODEL), lambda i: (i, 0)), pl.BlockSpec((tm, 2 * D_FF), lambda i: (i, 0)), _full(wd.shape)],
        out_specs=pl.BlockSpec((tm, D_MODEL), lambda i: (i, 0)),
        out_shape=jax.ShapeDtypeStruct((T, D_MODEL), F32),
        compiler_params=_params("parallel"),
    )(xm, gu, wd)


def _final_loss(x, g, tgt, tm, name):
    T = x.shape[0]

    def body(x_ref, g_ref, t_ref, dx_ref, dg_ref, ls_ref):
        @pl.when(pl.program_id(0) == 0)
        def _():
            dg_ref[...] = jnp.zeros_like(dg_ref)
            ls_ref[...] = jnp.zeros_like(ls_ref)

        xv = x_ref[...]
        gv = g_ref[...]
        r = lax.rsqrt(jnp.mean(xv * xv, axis=-1, keepdims=True) + EPS)
        xh = xv * r
        diff = xh * gv - t_ref[...]
        ls_ref[...] += jnp.full(ls_ref.shape, 0.5 / D_MODEL, F32) * jnp.sum(diff * diff)
        dy = diff * (1.0 / D_MODEL)
        dxh = dy * gv
        dx_ref[...] = r * (dxh - xh * jnp.mean(dxh * xh, axis=-1, keepdims=True))
        dg_ref[...] += jnp.sum(dy * xh, axis=0, keepdims=True)

    return pl.pallas_call(
        body, name=name, grid=(T // tm,),
        in_specs=[pl.BlockSpec((tm, D_MODEL), lambda i: (i, 0)), _full((1, D_MODEL)), pl.BlockSpec((tm, D_MODEL), lambda i: (i, 0))],
        out_specs=[pl.BlockSpec((tm, D_MODEL), lambda i: (i, 0)), _full((1, D_MODEL)), _full((1, 128))],
        out_shape=[jax.ShapeDtypeStruct((T, D_MODEL), F32), jax.ShapeDtypeStruct((1, D_MODEL), F32), jax.ShapeDtypeStruct((1, 128), F32)],
        compiler_params=_params("arbitrary"),
    )(x, g, tgt)


def _swiglu_bwd(dx, gu, wd, tm, name, after):
    T = dx.shape[0]

    def body(dx_ref, gu_ref, wd_ref, after_ref, dgu_ref, act_ref):
        del after_ref
        dact = _dot_nt(dx_ref[...].astype(BF16), wd_ref[...])
        g = gu_ref[:, :D_FF].astype(F32)
        u = gu_ref[:, D_FF:].astype(F32)
        s = _sig(g)
        silu = g * s
        act_ref[...] = (silu * u).astype(BF16)
        dgu_ref[:, :D_FF] = (dact * u * (s + silu * (1.0 - s))).astype(BF16)
        dgu_ref[:, D_FF:] = (dact * silu).astype(BF16)

    return pl.pallas_call(
        body, name=name, grid=(T // tm,),
        in_specs=[pl.BlockSpec((tm, D_MODEL), lambda i: (i, 0)), pl.BlockSpec((tm, 2 * D_FF), lambda i: (i, 0)), _full(wd.shape),
                  pl.BlockSpec(memory_space=pl.ANY)],
        out_specs=[pl.BlockSpec((tm, 2 * D_FF), lambda i: (i, 0)), pl.BlockSpec((tm, D_FF), lambda i: (i, 0))],
        out_shape=[jax.ShapeDtypeStruct((T, 2 * D_FF), BF16), jax.ShapeDtypeStruct((T, D_FF), BF16)],
        compiler_params=_params("parallel"),
    )(dx, gu, wd, after)


def _mm_tn(a, b, grid, a_block, a_map, b_block, b_map, o_shape, o_block, o_map, name, col_split=1):
    gk = grid[2]
    tm = [d for d in a_block if d is not None][-1]
    tn = [d for d in b_block if d is not None][-1]

    def body(a_ref, b_ref, o_ref, acc_ref):
        k = pl.program_id(2)
        p = _dot_tn(a_ref[...].astype(BF16), b_ref[...].astype(BF16))

        @pl.when(k == 0)
        def _():
            acc_ref[...] = p

        @pl.when(k > 0)
        def _():
            acc_ref[...] += p

        @pl.when(k == gk - 1)
        def _():
            if col_split == 1:
                o_ref[...] = acc_ref[...].astype(o_ref.dtype)
            else:
                w = tn // col_split
                for s in range(col_split):
                    o_ref[s] = acc_ref[:, s * w:(s + 1) * w].astype(o_ref.dtype)

    return pl.pallas_call(
        body, name=name, grid=grid,
        in_specs=[pl.BlockSpec(a_block, a_map), pl.BlockSpec(b_block, b_map)],
        out_specs=pl.BlockSpec(o_block, o_map),
        out_shape=jax.ShapeDtypeStruct(o_shape, BF16),
        scratch_shapes=[pltpu.VMEM((tm, tn), F32)],
        compiler_params=_params("parallel", "parallel", "arbitrary"),
    )(a, b)


def _mm_nt_rmsbwd(a, w, x, g, dres, tm, tk, name):
    T = x.shape[0]
    transposed = w.ndim == 2
    if transposed:
        gk = w.shape[0] // tk
        wspec = pl.BlockSpec((tk, D_MODEL), lambda i, k: (k, 0))
    else:
        tk = w.shape[2]
        gk = w.shape[0]
        wspec = pl.BlockSpec((None, D_MODEL, tk), lambda i, k: (k, 0, 0))

    def body(a_ref, w_ref, x_ref, g_ref, r_ref, dx_ref, dg_ref, acc_ref):
        i, k = pl.program_id(0), pl.program_id(1)
        p = (_dot if transposed else _dot_nt)(a_ref[...], w_ref[...])

        @pl.when(k == 0)
        def _():
            acc_ref[...] = p

        @pl.when(k > 0)
        def _():
            acc_ref[...] += p

        @pl.when(jnp.logical_and(i == 0, k == 0))
        def _():
            dg_ref[...] = jnp.zeros_like(dg_ref)

        @pl.when(k == gk - 1)
        def _():
            dh = acc_ref[...]
            xv = x_ref[...]
            r = lax.rsqrt(jnp.mean(xv * xv, axis=-1, keepdims=True) + EPS)
            xh = xv * r
            dxh = dh * g_ref[...]
            dx_ref[...] = r_ref[...] + r * (dxh - xh * jnp.mean(dxh * xh, axis=-1, keepdims=True))
            dg_ref[...] += jnp.sum(dh * xh, axis=0, keepdims=True)

    return pl.pallas_call(
        body, name=name, grid=(T // tm, gk),
        in_specs=[pl.BlockSpec((tm, tk), lambda i, k: (i, k)), wspec, pl.BlockSpec((tm, D_MODEL), lambda i, k: (i, 0)),
                  _full((1, D_MODEL)), pl.BlockSpec((tm, D_MODEL), lambda i, k: (i, 0))],
        out_specs=[pl.BlockSpec((tm, D_MODEL), lambda i, k: (i, 0)), _full((1, D_MODEL))],
        out_shape=[jax.ShapeDtypeStruct((T, D_MODEL), F32), jax.ShapeDtypeStruct((1, D_MODEL), F32)],
        scratch_shapes=[pltpu.VMEM((tm, D_MODEL), F32)],
        compiler_params=_params("arbitrary", "arbitrary"),
    )(a, w, x, g, dres)


def _merge_bwd(dxm, ys, proj, wb, wo, tm, name):
    T = dxm.shape[0]

    def body(dx_ref, ys_ref, zg_ref, wb_ref, wo_ref, dys_ref, dbr_ref, dp_ref):
        dmerged = _dot_nt(dx_ref[...].astype(BF16), wo_ref[...])
        for n in range(N_BRANCH):
            yn = ys_ref[:, n * HALF:(n + 1) * HALF]
            br = jnp.concatenate([_dot(yn, wb_ref[s, n]) for s in range(N_CHIPS)], axis=1)
            gt = _sig(zg_ref[:, n * D_MODEL:(n + 1) * D_MODEL].astype(F32))
            dbr = (gt * dmerged).astype(BF16)
            dbr_ref[:, n * D_MODEL:(n + 1) * D_MODEL] = dbr
            dp_ref[:, n * D_MODEL:(n + 1) * D_MODEL] = (dmerged * br * gt * (1.0 - gt)).astype(BF16)
            dy = None
            for s in range(N_CHIPS):
                t = _dot_nt(dbr[:, s * 256:(s + 1) * 256], wb_ref[s, n])
                dy = t if dy is None else dy + t
            dys_ref[:, n * HALF:(n + 1) * HALF] = dy.astype(BF16)
        dp_ref[:, GATE_W:] = jnp.zeros((tm, MIX_W - GATE_W), BF16)

    return pl.pallas_call(
        body, name=name, grid=(T // tm,),
        in_specs=[pl.BlockSpec((tm, D_MODEL), lambda i: (i, 0)), pl.BlockSpec((tm, N_BRANCH * HALF), lambda i: (i, 0)),
                  pl.BlockSpec((tm, GATE_W), lambda i: (i, 0)), _full(wb.shape), _full(wo.shape)],
        out_specs=[pl.BlockSpec((tm, N_BRANCH * HALF), lambda i: (i, 0)), pl.BlockSpec((tm, GATE_W), lambda i: (i, 0)),
                   pl.BlockSpec((tm, MIX_W), lambda i: (i, 0))],
        out_shape=[jax.ShapeDtypeStruct((T, N_BRANCH * HALF), BF16), jax.ShapeDtypeStruct((T, GATE_W), BF16),
                   jax.ShapeDtypeStruct((T, PROJ_PAD), BF16)],
        compiler_params=_params("parallel"),
    )(dxm, ys, proj, wb, wo)


def _gelu(v):
    return 0.5 * v * (1.0 + lax.erf(v * INV_SQRT2))


def _gelu_grad(v):
    return 0.5 * (1.0 + lax.erf(v * INV_SQRT2)) + v * jnp.exp(-0.5 * v * v) * INV_SQRT_2PI


def _rot_half(t):
    w = t.shape[1]
    lane = lax.broadcasted_iota(jnp.int32, t.shape, 1)
    return jnp.where((lane % HEAD_DIM) < HEAD_DIM // 2, pltpu.roll(t, w - HEAD_DIM // 2, 1), pltpu.roll(t, HEAD_DIM // 2, 1))


def _rope(t, cos, sin_signed):
    return t * cos + _rot_half(t) * sin_signed


def _rope_t(d, cos, sin_signed):
    return d * cos + _rot_half(d * sin_signed)


def _ln_fwd(v, g, b):
    mu = jnp.mean(v, axis=-1, keepdims=True)
    vc = v - mu
    r = lax.rsqrt(jnp.mean(vc * vc, axis=-1, keepdims=True) + EPS)
    vh = vc * r
    return vh * g + b, vh, r


def _ln_bwd(dn, vh, r, g):
    dvh = dn * g
    return r * (dvh - jnp.mean(dvh, axis=-1, keepdims=True) - vh * jnp.mean(dvh * vh, axis=-1, keepdims=True))


def _sublane_shifts(sh_ref, rows):
    for b in range(1, 8):
        sh_ref[b, 0:rows - 8, :] = sh_ref[0, pl.ds(b, rows - 8), :]


def _tap(sh_ref, off, n):
    return sh_ref[off % 8, pl.ds(off - off % 8, n), :]


def _tril_mask():
    return lax.broadcasted_iota(jnp.int32, (SG_CHUNK, SG_CHUNK), 0) >= lax.broadcasted_iota(jnp.int32, (SG_CHUNK, SG_CHUNK), 1)


def _band_masks():
    shape = (Q_PER_KV * WINDOW, 2 * WINDOW)
    row = lax.broadcasted_iota(jnp.int32, shape, 0) % WINDOW
    col = lax.broadcasted_iota(jnp.int32, shape, 1)
    band = (col > row) & (col <= row + WINDOW)
    return band, band & (col >= WINDOW)


def _attn_probs(qs, kh, sink_col, valid):
    s = jnp.where(valid, _dot_nt(qs, kh) * (HEAD_DIM ** -0.5), NEG_BIG)
    m = jnp.maximum(jnp.max(s, axis=-1, keepdims=True), sink_col)
    p = jnp.exp(s - m)
    es = jnp.exp(sink_col - m)
    inv = 1.0 / (jnp.sum(p, axis=-1, keepdims=True) + es)
    return p * inv, es * inv


def _sink_col(sinks_ref, h):
    return jnp.concatenate([jnp.broadcast_to(sinks_ref[:, h * Q_PER_KV + g:h * Q_PER_KV + g + 1], (WINDOW, 1))
                            for g in range(Q_PER_KV)], axis=0)


def _mixer_in_specs(TB, nb):
    r = TB // HALO
    last = nb * r - 1
    cur = pl.BlockSpec((TB, MIX_W), lambda i: (i, 1))
    prev = pl.BlockSpec((HALO, MIX_W), lambda i: (jnp.maximum(i * r - 1, 0), 1))
    nxt = pl.BlockSpec((HALO, MIX_W), lambda i: (jnp.minimum((i + 1) * r, last), 1))
    tcur = pl.BlockSpec((TB, 128), lambda i: (i, 0))
    tprev = pl.BlockSpec((HALO, 128), lambda i: (jnp.maximum(i * r - 1, 0), 0))
    tnxt = pl.BlockSpec((HALO, 128), lambda i: (jnp.minimum((i + 1) * r, last), 0))
    return cur, prev, nxt, tcur, tprev, tnxt


def _mixer_param_specs():
    return [_full((1, HALF)), _full((1, HALF)), _full((SG_GROUPS, SG_CHUNK, SG_CHUNK)), _full((SG_CHUNK, 128)),
            _full((32, HALF)), _full((1, HALF)), _full((1, HALF)), _full((1, HALF)), _full((1, 128)), _full((8, HALF))]


def _mixers_fwd(proj, cos_t, sin_t, mp, TB, name):
    T = proj.shape[0]
    nb = T // TB
    r = TB // HALO
    cur, prev, _, tcur, tprev, _ = _mixer_in_specs(TB, nb)

    def body(zc_ref, zp_ref, cc_ref, sc_ref, cp_ref, sp_ref,
             lg_ref, lb_ref, sgw_ref, sgb_ref, cvw_ref, cvb_ref, cvg_ref, cvbb_ref, sinks_ref, scw_ref,
             ys_ref, scr_ref, k_ref, v_ref, sh_ref):
        i = pl.program_id(0)
        pm = (i > 0).astype(F32)

        def colsE(c0, c1):
            return jnp.concatenate([zp_ref[:, c0:c1].astype(F32) * pm, zc_ref[:, c0:c1].astype(F32)], axis=0)

        a = _gelu(zc_ref[:, C_ZA:C_ZA + 2 * HALF].astype(F32))
        u = a[:, :HALF]
        vn, _, _ = _ln_fwd(a[:, HALF:], lg_ref[...], lb_ref[...])
        vnb = vn.astype(BF16)
        tril = _tril_mask()
        chunks = [slice(ci * SG_CHUNK, (ci + 1) * SG_CHUNK) for ci in range(r)]
        for g in range(SG_GROUPS):
            cols = slice(g * 128, (g + 1) * 128)
            wt = jnp.where(tril, sgw_ref[g], 0.0).astype(BF16)
            mixed = _dot(wt, jnp.concatenate([vnb[rows, cols] for rows in chunks], axis=1)) + sgb_ref[:, g:g + 1]
            for ci, rows in enumerate(chunks):
                ys_ref[rows, cols] = (u[rows, cols] * mixed[:, ci * 128:(ci + 1) * 128]).astype(BF16)

        def colsB(c0, c1):
            return jnp.concatenate([zp_ref[HALO - CV_PAD:, c0:c1].astype(F32) * pm, zc_ref[:, c0:c1].astype(F32)], axis=0)

        sh_ref[0] = colsB(C_ZB, C_ZB + HALF) * _sig(colsB(C_ZB + HALF, C_ZB + 2 * HALF))
        _sublane_shifts(sh_ref, TB + CV_PAD)
        c = jnp.broadcast_to(cvb_ref[...], (TB, HALF))
        for k in range(CV_KERNEL):
            c = c + cvw_ref[k:k + 1, :] * _tap(sh_ref, CV_PAD - (CV_KERNEL - 1) + k, TB)
        n, _, _ = _ln_fwd(c, cvg_ref[...], cvbb_ref[...])
        ys_ref[:, HALF:2 * HALF] = (n * _sig(n)).astype(BF16)

        zd = colsE(C_ZD + HALF, C_ZD + 3 * HALF)
        scr_ref[...] = zd[:, :HALF] * zd[:, HALF:]
        cv = None
        for k in range(SC_KERNEL):
            t = scw_ref[k:k + 1, :] * scr_ref[pl.ds(HALO - (SC_KERNEL - 1) + k, TB), :]
            cv = t if cv is None else cv + t
        ys_ref[:, 3 * HALF:4 * HALF] = (zc_ref[:, C_ZD:C_ZD + HALF].astype(F32) * cv).astype(BF16)

        cosE = jnp.concatenate([cp_ref[...], cc_ref[...]], axis=0)
        sinE = jnp.concatenate([sp_ref[...], sc_ref[...]], axis=0)
        k_ref[...] = _rope(colsE(C_K, C_K + 128), cosE, sinE).astype(BF16)
        v_ref[...] = colsE(C_V, C_V + 128).astype(BF16)
        cosC, sinC = cc_ref[...], sc_ref[...]
        q = jnp.concatenate([_rope(zc_ref[:, C_Q + 128 * j:C_Q + 128 * (j + 1)].astype(F32), cosC, sinC)
                             for j in range(4)], axis=1).astype(BF16)
        in_band, in_band_cur = _band_masks()
        sink_cols = [_sink_col(sinks_ref, h) for h in range(N_KV_HEADS)]
        for qb in range(r):
            valid = in_band if qb else in_band_cur | (in_band & (i > 0))
            for h in range(N_KV_HEADS):
                hc = slice(h * HEAD_DIM, (h + 1) * HEAD_DIM)
                kh = k_ref[qb * WINDOW:qb * WINDOW + 2 * WINDOW, hc]
                vh = v_ref[qb * WINDOW:qb * WINDOW + 2 * WINDOW, hc]
                qs = jnp.concatenate([q[qb * WINDOW:(qb + 1) * WINDOW, (h * Q_PER_KV + g) * HEAD_DIM:(h * Q_PER_KV + g + 1) * HEAD_DIM]
                                      for g in range(Q_PER_KV)], axis=0)
                probs, _ = _attn_probs(qs, kh, sink_cols[h], valid)
                o = _dot(probs.astype(BF16), vh)
                for g in range(Q_PER_KV):
                    c0 = 2 * HALF + (h * Q_PER_KV + g) * HEAD_DIM
                    ys_ref[qb * WINDOW:(qb + 1) * WINDOW, c0:c0 + HEAD_DIM] = o[g * WINDOW:(g + 1) * WINDOW].astype(BF16)

    return pl.pallas_call(
        body, name=name, grid=(nb,),
        in_specs=[cur, prev, tcur, tcur, tprev, tprev] + _mixer_param_specs(),
        out_specs=pl.BlockSpec((TB, 4 * HALF), lambda i: (i, 0)),
        out_shape=jax.ShapeDtypeStruct((T, 4 * HALF), BF16),
        scratch_shapes=[pltpu.VMEM((TB + HALO, HALF), F32), pltpu.VMEM((TB + HALO, 128), BF16), pltpu.VMEM((TB + HALO, 128), BF16),
                        pltpu.VMEM((8, TB + CV_PAD, HALF), F32)],
        compiler_params=_params("parallel"),
    )(proj, proj, cos_t, sin_t, cos_t, sin_t, *mp)


def _mixers_bwd(proj, dys, dproj, cos_t, sin_t, mp, TB, name):
    T = proj.shape[0]
    nb = T // TB
    r = TB // HALO
    RE = TB + 2 * HALO
    RC = TB + HALO
    cur, prev, nxt, tcur, tprev, tnxt = _mixer_in_specs(TB, nb)
    dcur = pl.BlockSpec((TB, 4 * HALF), lambda i: (i, 0))
    dnxt = pl.BlockSpec((HALO, 4 * HALF), lambda i: (jnp.minimum((i + 1) * r, nb * r - 1), 0))

    def body(zc_ref, zp_ref, zn_ref, dyc_ref, dyn_ref, cc_ref, sc_ref, cp_ref, sp_ref, cn_ref, sn_ref,
             lg_ref, lb_ref, sgw_ref, sgb_ref, cvw_ref, cvb_ref, cvg_ref, cvbb_ref, sinks_ref, scw_ref, dp_in_ref,
             dz_ref, dlg_ref, dlb_ref, dsgw_ref, dsgb_ref, dcvw_ref, dcvb_ref, dcvg_ref, dcvbb_ref, dsink_ref, dscw_ref,
             scr_ref, scr2_ref, k_ref, v_ref, dk_ref, dv_ref, dq_ref, sh_ref, sh2_ref):
        del dp_in_ref
        i = pl.program_id(0)
        pm = (i > 0).astype(F32)
        nm = (i < nb - 1).astype(F32)

        @pl.when(i == 0)
        def _():
            for ref in (dlg_ref, dlb_ref, dsgw_ref, dsgb_ref, dcvw_ref, dcvb_ref, dcvg_ref, dcvbb_ref, dsink_ref, dscw_ref):
                ref[...] = jnp.zeros_like(ref)

        def colsE(c0, c1):
            return jnp.concatenate([zp_ref[:, c0:c1].astype(F32) * pm, zc_ref[:, c0:c1].astype(F32),
                                    zn_ref[:, c0:c1].astype(F32)], axis=0)

        def colsC(c0, c1):
            return jnp.concatenate([zc_ref[:, c0:c1].astype(F32), zn_ref[:, c0:c1].astype(F32)], axis=0)

        def dyC(c0, c1):
            return jnp.concatenate([dyc_ref[:, c0:c1].astype(F32), dyn_ref[:, c0:c1].astype(F32) * nm], axis=0)

        za = zc_ref[:, C_ZA:C_ZA + 2 * HALF].astype(F32)
        a = _gelu(za)
        u = a[:, :HALF]
        lg = lg_ref[...]
        vn, vh, rs = _ln_fwd(a[:, HALF:], lg, lb_ref[...])
        vnb = vn.astype(BF16)
        dya = dyc_ref[:, 0:HALF].astype(F32)
        tril = _tril_mask()
        lane128 = lax.broadcasted_iota(jnp.int32, (SG_CHUNK, 128), 1)
        chunks = [slice(ci * SG_CHUNK, (ci + 1) * SG_CHUNK) for ci in range(r)]
        side = lambda t, cols: jnp.concatenate([t[rows, cols] for rows in chunks], axis=1)
        for g in range(SG_GROUPS):
            cols = slice(g * 128, (g + 1) * 128)
            wt = jnp.where(tril, sgw_ref[g], 0.0).astype(BF16)
            vb = side(vnb, cols)
            dy_blk = side(dya, cols)
            du_g = dy_blk * (_dot(wt, vb) + sgb_ref[:, g:g + 1])
            dmix = dy_blk * side(u, cols)
            dmb = dmix.astype(BF16)
            dvn_g = _dot_tn(wt, dmb)
            dsgw_ref[g] += jnp.where(tril, _dot_nt(dmb, vb), 0.0)
            dsgb_ref[...] += jnp.where(lane128 == g, jnp.sum(dmix, axis=1, keepdims=True), 0.0)
            for ci, rows in enumerate(chunks):
                scr_ref[rows, cols] = du_g[:, ci * 128:(ci + 1) * 128]
                scr2_ref[rows, cols] = dvn_g[:, ci * 128:(ci + 1) * 128]
        du, dvn = scr_ref[0:TB, :], scr2_ref[0:TB, :]
        dlg_ref[...] += jnp.sum(dvn * vh, axis=0, keepdims=True)
        dlb_ref[...] += jnp.sum(dvn, axis=0, keepdims=True)
        dvv = _ln_bwd(dvn, vh, rs, lg)
        gg = _gelu_grad(za)
        dz_ref[:, C_ZA:C_ZA + HALF] = (du * gg[:, :HALF]).astype(BF16)
        dz_ref[:, C_ZA + HALF:C_ZA + 2 * HALF] = (dvv * gg[:, HALF:]).astype(BF16)

        RB = TB + CV_PAD

        def colsB(c0, c1):
            return jnp.concatenate([zp_ref[HALO - CV_PAD:, c0:c1].astype(F32) * pm, zc_ref[:, c0:c1].astype(F32),
                                    zn_ref[:CV_PAD, c0:c1].astype(F32)], axis=0)

        sh_ref[0] = colsB(C_ZB, C_ZB + HALF) * _sig(colsB(C_ZB + HALF, C_ZB + 2 * HALF))
        _sublane_shifts(sh_ref, RB + CV_PAD)
        c = jnp.broadcast_to(cvb_ref[...], (RB, HALF))
        for k in range(CV_KERNEL):
            c = c + cvw_ref[k:k + 1, :] * _tap(sh_ref, CV_PAD - (CV_KERNEL - 1) + k, RB)
        cvg = cvg_ref[...]
        n, ch, rc = _ln_fwd(c, cvg, cvbb_ref[...])
        sn = _sig(n)
        dyb = jnp.concatenate([dyc_ref[:, HALF:2 * HALF].astype(F32), dyn_ref[:CV_PAD, HALF:2 * HALF].astype(F32) * nm], axis=0)
        dn = dyb * (sn + n * sn * (1.0 - sn))
        dno = dn[:TB]
        dcvg_ref[...] += jnp.sum(dno * ch[:TB], axis=0, keepdims=True)
        dcvbb_ref[...] += jnp.sum(dno, axis=0, keepdims=True)
        dc = _ln_bwd(dn, ch, rc, cvg)
        sh2_ref[0] = dc
        _sublane_shifts(sh2_ref, RB)
        dcvb_ref[...] += jnp.sum(dc[:TB], axis=0, keepdims=True)
        dy0 = None
        for k in range(CV_KERNEL):
            wk = cvw_ref[k:k + 1, :]
            t = wk * _tap(sh2_ref, CV_KERNEL - 1 - k, TB)
            dy0 = t if dy0 is None else dy0 + t
            dcvw_ref[k:k + 1, :] += jnp.sum(dc[:TB] * _tap(sh_ref, CV_PAD - (CV_KERNEL - 1) + k, TB), axis=0, keepdims=True)
        ab = zc_ref[:, C_ZB:C_ZB + HALF].astype(F32)
        sg = _sig(zc_ref[:, C_ZB + HALF:C_ZB + 2 * HALF].astype(F32))
        dz_ref[:, C_ZB:C_ZB + HALF] = (dy0 * sg).astype(BF16)
        dz_ref[:, C_ZB + HALF:C_ZB + 2 * HALF] = (dy0 * ab * sg * (1.0 - sg)).astype(BF16)

        zd = colsE(C_ZD + HALF, C_ZD + 3 * HALF)
        scr_ref[...] = zd[:, :HALF] * zd[:, HALF:]
        dcv = dyC(3 * HALF, 4 * HALF) * colsC(C_ZD, C_ZD + HALF)
        scr2_ref[...] = dcv
        cv = None
        dud = None
        for k in range(SC_KERNEL):
            wk = scw_ref[k:k + 1, :]
            us = scr_ref[pl.ds(HALO - (SC_KERNEL - 1) + k, TB), :]
            t = wk * us
            cv = t if cv is None else cv + t
            t2 = wk * scr2_ref[pl.ds(SC_KERNEL - 1 - k, TB), :]
            dud = t2 if dud is None else dud + t2
            dscw_ref[k:k + 1, :] += jnp.sum(dcv[:TB] * us, axis=0, keepdims=True)
        dz_ref[:, C_ZD:C_ZD + HALF] = (dyc_ref[:, 3 * HALF:4 * HALF].astype(F32) * cv).astype(BF16)
        dz_ref[:, C_ZD + HALF:C_ZD + 2 * HALF] = (dud * zc_ref[:, C_ZD + 2 * HALF:C_ZD + 3 * HALF].astype(F32)).astype(BF16)
        dz_ref[:, C_ZD + 2 * HALF:C_ZD + 3 * HALF] = (dud * zc_ref[:, C_ZD + HALF:C_ZD + 2 * HALF].astype(F32)).astype(BF16)

        cosE = jnp.concatenate([cp_ref[...], cc_ref[...], cn_ref[...]], axis=0)
        sinE = jnp.concatenate([sp_ref[...], sc_ref[...], sn_ref[...]], axis=0)
        k_ref[...] = _rope(colsE(C_K, C_K + 128), cosE, sinE).astype(BF16)
        v_ref[...] = colsE(C_V, C_V + 128).astype(BF16)
        dk_ref[...] = jnp.zeros_like(dk_ref)
        dv_ref[...] = jnp.zeros_like(dv_ref)
        q = jnp.concatenate([_rope(colsC(C_Q + 128 * j, C_Q + 128 * (j + 1)), cosE[HALO:], sinE[HALO:])
                             for j in range(4)], axis=1).astype(BF16)
        dO = dyC(2 * HALF, 3 * HALF).astype(BF16)
        lane_s = lax.broadcasted_iota(jnp.int32, (1, 128), 1)
        in_band, in_band_cur = _band_masks()
        sink_cols = [_sink_col(sinks_ref, h) for h in range(N_KV_HEADS)]
        for qb in range(r + 1):
            valid = in_band if qb else in_band_cur | (in_band & (i > 0))
            rows = slice(qb * WINDOW, (qb + 1) * WINDOW)
            band = slice(qb * WINDOW, qb * WINDOW + 2 * WINDOW)
            for h in range(N_KV_HEADS):
                hc = slice(h * HEAD_DIM, (h + 1) * HEAD_DIM)
                kh = k_ref[band, hc]
                vh_ = v_ref[band, hc]
                heads = [slice((h * Q_PER_KV + g) * HEAD_DIM, (h * Q_PER_KV + g + 1) * HEAD_DIM) for g in range(Q_PER_KV)]
                qs = jnp.concatenate([q[rows, hs] for hs in heads], axis=0)
                dos = jnp.concatenate([dO[rows, hs] for hs in heads], axis=0)
                probs, p_sink = _attn_probs(qs, kh, sink_cols[h], valid)
                dP = _dot_nt(dos, vh_)
                rsum = jnp.sum(probs * dP, axis=-1, keepdims=True)
                dS = (probs * (dP - rsum) * (HEAD_DIM ** -0.5)).astype(BF16)
                dk_ref[band, hc] += _dot_tn(dS, qs)
                dv_ref[band, hc] += _dot_tn(probs.astype(BF16), dos)
                if qb < r:
                    dqs = _dot(dS, kh)
                    dsk = -p_sink * rsum
                    for g in range(Q_PER_KV):
                        dq_ref[rows, heads[g]] = dqs[g * WINDOW:(g + 1) * WINDOW]
                        dsink_ref[...] += jnp.where(lane_s == h * Q_PER_KV + g, jnp.sum(dsk[g * WINDOW:(g + 1) * WINDOW]), 0.0)
        cosC, sinC = cc_ref[...], sc_ref[...]
        for j in range(4):
            dz_ref[:, C_Q + 128 * j:C_Q + 128 * (j + 1)] = _rope_t(dq_ref[:, 128 * j:128 * (j + 1)], cosC, sinC).astype(BF16)
        dz_ref[:, C_K:C_K + 128] = _rope_t(dk_ref[HALO:HALO + TB, :], cosC, sinC).astype(BF16)
        dz_ref[:, C_V:C_V + 128] = dv_ref[HALO:HALO + TB, :].astype(BF16)

    small = [((1, HALF), F32), ((1, HALF), F32), ((SG_GROUPS, SG_CHUNK, SG_CHUNK), F32), ((SG_CHUNK, 128), F32),
             ((32, HALF), F32), ((1, HALF), F32), ((1, HALF), F32), ((1, HALF), F32), ((1, 128), F32), ((8, HALF), F32)]
    outs = pl.pallas_call(
        body, name=name, grid=(nb,),
        in_specs=[cur, prev, nxt, dcur, dnxt, tcur, tcur, tprev, tprev, tnxt, tnxt] + _mixer_param_specs()
                 + [pl.BlockSpec(memory_space=pl.ANY)],
        out_specs=[pl.BlockSpec((TB, MIX_W), lambda i: (i, 1))] + [_full(s) for s, _ in small],
        out_shape=[jax.ShapeDtypeStruct((T, PROJ_PAD), BF16)] + [jax.ShapeDtypeStruct(s, d) for s, d in small],
        scratch_shapes=[pltpu.VMEM((RE, HALF), F32), pltpu.VMEM((RC, HALF), F32), pltpu.VMEM((RE, 128), BF16), pltpu.VMEM((RE, 128), BF16),
                        pltpu.VMEM((RE, 128), F32), pltpu.VMEM((RE, 128), F32), pltpu.VMEM((TB, HALF), F32),
                        pltpu.VMEM((8, TB + 2 * CV_PAD, HALF), F32), pltpu.VMEM((8, TB + CV_PAD, HALF), F32)],
        input_output_aliases={21: 0},
        compiler_params=_params("arbitrary"),
    )(proj, proj, proj, dys, dys, cos_t, sin_t, cos_t, sin_t, cos_t, sin_t, *mp, dproj)
    return outs


def _rope_tables(T):
    pos = jnp.arange(T, dtype=F32)
    inv_freq = 1.0 / (ROPE_THETA ** (jnp.arange(0, HEAD_DIM, 2, dtype=F32) / HEAD_DIM))
    ang = pos[:, None] * inv_freq[None, :]
    cos, sin = jnp.cos(ang), jnp.sin(ang)
    cos_t = jnp.concatenate([cos, cos, cos, cos], axis=1)
    sin_t = jnp.concatenate([-sin, sin, -sin, sin], axis=1)
    return cos_t, sin_t


def _mixer_params(l, sg_ln_g, sg_ln_b, sg_w, sg_b, cv_w, cv_b, cv_ln_g, cv_ln_b, attn_sinks, sc_w):
    sgb_t = jnp.zeros((SG_CHUNK, 128), F32).at[:, :SG_GROUPS].set(sg_b[l].T)
    cvw = jnp.zeros((32, HALF), F32).at[:CV_KERNEL].set(cv_w[l])
    scw = jnp.zeros((8, HALF), F32).at[:SC_KERNEL].set(sc_w[l])
    sinks = jnp.zeros((1, 128), F32).at[0, :N_Q_HEADS].set(attn_sinks[l])
    return [sg_ln_g[l][None], sg_ln_b[l][None], sg_w[l], sgb_t, cvw, cv_b[l][None], cv_ln_g[l][None], cv_ln_b[l][None], sinks, scw]


def _w_in_layout(w_in_g):
    cut = MIX_W - 2 * W_IN_SHARD
    return jnp.concatenate([w_in_g[2][cut:], w_in_g[3], jnp.zeros((MIX_W - GATE_W, D_MODEL), w_in_g.dtype),
                            w_in_g[0], w_in_g[1], w_in_g[2][:cut]], axis=0)


def _w_in_unlayout(dw):
    cut = MIX_W - 2 * W_IN_SHARD
    return jnp.stack([dw[MIX_W:MIX_W + W_IN_SHARD], dw[MIX_W + W_IN_SHARD:MIX_W + 2 * W_IN_SHARD],
                      jnp.concatenate([dw[MIX_W + 2 * W_IN_SHARD:], dw[:W_IN_SHARD - cut]], axis=0),
                      dw[W_IN_SHARD - cut:GATE_W]], axis=0)


def _device_step(x, tgt, norm_mix, norm_ffn, norm_final, mixer_params, w_in_p, wb_g, wo_g, wgu_g, wd_g):
    T = x.shape[0]
    tables = _rope_tables(T)
    saved = []
    for l in range(DEPTH):
        lw = dict(w_in=w_in_p[l], w_branch=wb_g[l], w_out=wo_g[l], w_gate_up=wgu_g[l], w_down=wd_g[l],
                  norm_mix=norm_mix[l][None], norm_ffn=norm_ffn[l][None], mixer=mixer_params[l], after=jnp.zeros((8, 128), F32))
        x, sv = _fwd_layer(l, x, lw, tables)
        saved.append((lw, sv))
    dx, dnf, loss = _final_loss(x, norm_final[None], tgt, 256, "final_loss")
    grads = [None] * DEPTH
    for l in reversed(range(DEPTH)):
        lw, sv = saved[l]
        dxm, g_ffn = _bwd_layer_ffn(l, dx, lw, sv)
        dx, g_mix = _bwd_layer_mix(l, dxm, lw, sv, tables)
        raw = {**g_ffn, **g_mix}
        grads[l] = {**raw, **_small_views(raw)}
    return loss, dx, dnf[0], grads


MIX_BLOCK = 256


def _fwd_layer(l, x, lw, tables):
    return _fwd_layer_rest(l, x, _fwd_layer_mix(l, x, lw, tables), lw)


def _fwd_layer_mix(l, x, lw, tables, between=None):
    proj, xn = _rms_mm(x, lw["norm_mix"], lw["w_in"], min(x.shape[0], 1024), 2176, f"proj{l}")
    if between is not None:
        between(proj, lw)
    return proj, xn, _mixers_fwd(proj, *tables, lw["mixer"], MIX_BLOCK, f"mixers_fwd{l}")


def _fwd_layer_rest(l, x, mixed, lw, between=None):
    proj, xn, ys = mixed
    TM = min(x.shape[0], 1024)
    xm, merged = _merge_fwd(x, ys, proj, lw["w_branch"], lw["w_out"], min(x.shape[0], 512), f"merge_fwd{l}")
    if between is not None:
        between(xm, lw)
    gu, hn = _rms_mm(xm, lw["norm_ffn"], lw["w_gate_up"], TM, GU_SHARD, f"ffn_up{l}")
    x_out = _ffn_down(xm, gu, lw["w_down"], min(x.shape[0], 512), f"ffn_down{l}")
    return x_out, (x, proj, xn, ys, xm, merged, gu, hn)


def _bwd_layer_ffn(l, dx, lw, sv, between=None):
    x_in, proj, xn, ys, xm, merged, gu, hn = sv
    T = dx.shape[0]
    tkk = min(T, 1024)
    gk = T // tkk
    dgu, act = _swiglu_bwd(dx, gu, lw["w_down"], 256, f"swiglu_bwd{l}", lw["after"])
    d_wd = _mm_tn(act, dx, (2, 1, gk), (tkk, D_FF // 2), lambda i, j, k: (k, i), (tkk, D_MODEL), lambda i, j, k: (k, 0),
                  (D_FF, D_MODEL), (D_FF // 2, D_MODEL), lambda i, j, k: (i, 0), f"dw_down{l}")
    d_wgu = _mm_tn(hn, dgu, (1, N_CHIPS, gk), (tkk, D_MODEL), lambda i, j, k: (k, 0), (tkk, GU_SHARD), lambda i, j, k: (k, j),
                   (N_CHIPS, D_MODEL, GU_SHARD), (None, D_MODEL, GU_SHARD), lambda i, j, k: (j, 0, 0), f"dw_gate_up{l}")
    if between is not None:
        between(dict(w_gate_up=d_wgu, w_down=d_wd), lw)
    dxm, d_nffn = _mm_nt_rmsbwd(dgu, lw["w_gate_up"], xm, lw["norm_ffn"], dx, min(T, 1024), GU_SHARD, f"ffn_up_bwd{l}")
    dys, dbr, dproj = _merge_bwd(dxm, ys, proj, lw["w_branch"], lw["w_out"], min(T, 512), f"merge_bwd{l}")
    d_wo = _mm_tn(merged, dxm, (2, 1, gk), (tkk, 512), lambda i, j, k: (k, i), (tkk, D_MODEL), lambda i, j, k: (k, 0),
                  (D_MODEL, D_MODEL), (512, D_MODEL), lambda i, j, k: (i, 0), f"dw_out{l}")
    d_wb = _mm_tn(ys, dbr, (N_BRANCH, 1, gk), (tkk, HALF), lambda i, j, k: (k, i), (tkk, D_MODEL), lambda i, j, k: (k, i),
                  (N_CHIPS, N_BRANCH, HALF, 256), (N_CHIPS, None, HALF, 256), lambda i, j, k: (0, i, 0, 0), f"dw_branch{l}", col_split=N_CHIPS)
    return (dxm, dys, dproj), dict(w_branch=d_wb, w_out=d_wo, w_gate_up=d_wgu, w_down=d_wd, norm_ffn=d_nffn)


def _bwd_layer_mix(l, carry, lw, sv, tables, between=None):
    dxm, dys, dproj = carry
    x_in, proj, xn, ys, xm, merged, gu, hn = sv
    T = dxm.shape[0]
    tkk = min(T, 1024)
    gk = T // tkk
    mb = _mixers_bwd(proj, dys, dproj, *tables, lw["mixer"], MIX_BLOCK, f"mixers_bwd{l}")
    dproj = mb[0]
    d_win = _mm_tn(dproj, xn, (PROJ_PAD // 2176, 1, gk), (tkk, 2176), lambda i, j, k: (k, i), (tkk, D_MODEL), lambda i, j, k: (k, 0),
                   (PROJ_PAD, D_MODEL), (2176, D_MODEL), lambda i, j, k: (i, 0), f"dw_in{l}")
    if between is not None:
        between(d_win, lw)
    dx, d_nmix = _mm_nt_rmsbwd(dproj, lw["w_in"], x_in, lw["norm_mix"], dxm, min(T, 1024), 2176, f"proj_bwd{l}")
    return dx, dict(w_in=d_win, norm_mix=d_nmix, sg_ln_g=mb[1], sg_ln_b=mb[2], sg_w=mb[3], sg_b=mb[4], cv_w=mb[5], cv_b=mb[6],
                    cv_ln_g=mb[7], cv_ln_b=mb[8], attn_sinks=mb[9], sc_w=mb[10])


ANY = pl.BlockSpec(memory_space=pl.ANY)
BIG = ("w_in", "w_branch", "w_out", "w_gate_up", "w_down")
HALF_SHAPE = {"w_in": (2, W_IN_SHARD // 2, D_MODEL), "w_branch": (2, 1024, 256), "w_out": (2, 128, D_MODEL),
              "w_gate_up": (2, 512, GU_SHARD), "w_down": (2, 352, D_MODEL)}
NB = len(BIG)


def _place():
    x, y, c = lax.axis_index("x"), lax.axis_index("y"), lax.axis_index("c")
    chips = [(1 - x, y), (x, 1 - y), (1 - x, 1 - y)]
    return x, y, c, 2 * x + y, chips, [2 * px + py for px, py in chips]


def _remote(src, dst, ssem, rsem, dev):
    return pltpu.make_async_remote_copy(src_ref=src, dst_ref=dst, send_sem=ssem, recv_sem=rsem, device_id=dev, device_id_type=MESH)


HBM_SPEC = pl.BlockSpec(memory_space=pltpu.HBM)
SEM_SPEC = pl.BlockSpec(memory_space=pltpu.SEMAPHORE)
DATAFLOW = pltpu.SideEffectType.DATAFLOW_SIDE_EFFECTING


def _ici_ends(kind, src, land, j, c, chip, chip_ids):
    if kind == "gather":
        return src.at[c], land.at[chip, c], land.at[chip_ids[j], c]
    return src.at[chip_ids[j]], land.at[chip], land.at[chip_ids[j]]


def _ici_start(kind, srcs, land_shapes, name):
    n = len(srcs)

    def body(*refs):
        src, land = refs[:n], refs[n:2 * n]
        ssem, rsem, token = refs[2 * n], refs[2 * n + 1], refs[-1]
        x, y, c, chip, chips, chip_ids = _place()
        for k in range(n):
            for j in range(3):
                s, d, _ = _ici_ends(kind, src[k], land[k], j, c, chip, chip_ids)
                _remote(s, d, ssem.at[3 * k + j], rsem.at[3 * k + j], (*chips[j], c)).start()
        token[...] = jnp.zeros_like(token)

    sem = pltpu.SemaphoreType.DMA((3 * n,))
    outs = pl.pallas_call(
        body, name=name,
        out_shape=(sem, sem, *[pltpu.HBM(s.shape, s.dtype) for s in srcs], *[pltpu.HBM(sh, BF16) for sh in land_shapes],
                   jax.ShapeDtypeStruct((8, 128), F32)),
        in_specs=[HBM_SPEC] * (2 * n),
        out_specs=(SEM_SPEC, SEM_SPEC, *[HBM_SPEC] * (2 * n), pl.BlockSpec(memory_space=pltpu.VMEM)),
        input_output_aliases={i: 2 + i for i in range(2 * n)},
        compiler_params=pltpu.CompilerParams(has_side_effects=DATAFLOW),
    )(*[pltpu.with_memory_space_constraint(s, pltpu.HBM) for s in srcs],
      *[pltpu.with_memory_space_constraint(lax.empty(sh, BF16), pltpu.HBM) for sh in land_shapes])
    return (kind, outs[0], outs[1], list(outs[2:2 + n]), list(outs[2 + n:2 + 2 * n])), outs[-1]


def _ici_wait(handle, after, name):
    kind, ssem_in, rsem_in, srcs, lands = handle
    n = len(srcs)

    def body(*refs):
        src, land = refs[:n], refs[n:2 * n]
        ssem, rsem = refs[2 * n], refs[2 * n + 1]
        x, y, c, chip, chips, chip_ids = _place()
        for k in range(n):
            for j in range(3):
                s, _, mine = _ici_ends(kind, src[k], land[k], j, c, chip, chip_ids)
                cp = _remote(s, mine, ssem.at[3 * k + j], rsem.at[3 * k + j], (*chips[j], c))
                cp.wait_send()
                cp.wait_recv()

    outs = pl.pallas_call(
        body, name=name, out_shape=[pltpu.HBM(t.shape, t.dtype) for t in srcs + lands],
        in_specs=[HBM_SPEC] * (2 * n) + [SEM_SPEC, SEM_SPEC, ANY], out_specs=[HBM_SPEC] * (2 * n),
        input_output_aliases={i: i for i in range(2 * n)},
        compiler_params=pltpu.CompilerParams(has_side_effects=DATAFLOW),
    )(*srcs, *lands, ssem_in, rsem_in, after)
    return list(outs[:n]), list(outs[n:])


def _ag_pair(shards, lands, name):
    n = len(shards)

    def body(*refs):
        ins, outs = refs[:n], refs[2 * n:3 * n]
        token = refs[3 * n]
        s_fwd, r_fwd, s_own, r_own = refs[3 * n + 1:]
        x, y, c, chip, chips, chip_ids = _place()
        sib = (x, y, 1 - c)
        cps = []
        for k in range(n):
            cp = _remote(ins[k], outs[k].at[chip], s_own.at[k], r_own.at[k], sib)
            cp.start()
            cps.append(cp)
            for j in range(3):
                got = outs[k].at[chip_ids[j], c]
                cp = _remote(got, got, s_fwd.at[k, j], r_fwd.at[k, j], sib)
                cp.start()
                cps.append(cp)
        for k in range(n):
            _remote(ins[k], outs[k].at[chip], s_own.at[k], r_own.at[k], sib).wait_recv()
            for j in range(3):
                got = outs[k].at[chip_ids[j], 1 - c]
                _remote(got, got, s_fwd.at[k, j], r_fwd.at[k, j], sib).wait_recv()
        for cp in cps:
            cp.wait_send()
        token[...] = jnp.zeros_like(token)

    sem, sem1 = pltpu.SemaphoreType.DMA((n, 3)), pltpu.SemaphoreType.DMA((n,))
    outs = pl.pallas_call(
        body, name=name, out_shape=[jax.ShapeDtypeStruct(t.shape, t.dtype) for t in lands] + [jax.ShapeDtypeStruct((8, 128), F32)],
        in_specs=[ANY] * (2 * n), out_specs=[ANY] * n + [pl.BlockSpec(memory_space=pltpu.VMEM)],
        input_output_aliases={n + k: k for k in range(n)},
        scratch_shapes=[sem, sem, sem1, sem1], compiler_params=pltpu.CompilerParams(has_side_effects=True),
    )(*shards, *lands)
    return list(outs[:n]), outs[n]


def _forward_plan(n):
    def plan(refs, c, chip, chip_ids):
        out = []
        for k in range(n):
            shard, land = refs[k], refs[n + k]
            out.append((shard, land.at[chip], land.at[chip]))
            out += [(land.at[q, c], land.at[q, c], land.at[q, 1 - c]) for q in chip_ids]
        return out
    return plan, 4 * n


def _swap_plan(n):
    def plan(refs, c, chip, chip_ids):
        return [(refs[k].at[q, 1 - c], refs[n + k].at[q], refs[n + k].at[q]) for k in range(n) for q in range(N_CHIPS)]
    return plan, N_CHIPS * n


def _d2d_start(arrays, new_shapes, plan_n, name):
    plan, n_copies = plan_n
    n = len(arrays) + len(new_shapes)

    def body(*refs):
        ssem, rsem, token = refs[n], refs[n + 1], refs[-1]
        x, y, c, chip, _, chip_ids = _place()
        for i, (s, d, _) in enumerate(plan(refs[:n], c, chip, chip_ids)):
            _remote(s, d, ssem.at[i], rsem.at[i], (x, y, 1 - c)).start()
        token[...] = jnp.zeros_like(token)

    sem = pltpu.SemaphoreType.DMA((n_copies,))
    args = [pltpu.with_memory_space_constraint(t, pltpu.HBM) for t in arrays] + \
           [pltpu.with_memory_space_constraint(lax.empty(sh, BF16), pltpu.HBM) for sh in new_shapes]
    outs = pl.pallas_call(
        body, name=name,
        out_shape=(sem, sem, *[pltpu.HBM(t.shape, t.dtype) for t in args], jax.ShapeDtypeStruct((8, 128), F32)),
        in_specs=[HBM_SPEC] * n, out_specs=(SEM_SPEC, SEM_SPEC, *[HBM_SPEC] * n, pl.BlockSpec(memory_space=pltpu.VMEM)),
        input_output_aliases={i: 2 + i for i in range(n)},
        compiler_params=pltpu.CompilerParams(has_side_effects=DATAFLOW),
    )(*args)
    return (plan, outs[0], outs[1], list(outs[2:2 + n])), outs[-1]


def _d2d_wait(handle, after, name):
    plan, ssem_in, rsem_in, arrays = handle
    n = len(arrays)

    def body(*refs):
        ssem, rsem = refs[n], refs[n + 1]
        x, y, c, chip, _, chip_ids = _place()
        for i, (s, _, mine) in enumerate(plan(refs[:n], c, chip, chip_ids)):
            cp = _remote(s, mine, ssem.at[i], rsem.at[i], (x, y, 1 - c))
            cp.wait_send()
            cp.wait_recv()

    outs = pl.pallas_call(
        body, name=name, out_shape=[pltpu.HBM(t.shape, t.dtype) for t in arrays],
        in_specs=[HBM_SPEC] * n + [SEM_SPEC, SEM_SPEC, ANY], out_specs=[HBM_SPEC] * n,
        input_output_aliases={i: i for i in range(n)},
        compiler_params=pltpu.CompilerParams(has_side_effects=DATAFLOW),
    )(*arrays, ssem_in, rsem_in, after)
    return list(outs)


def _rs_pair(grads, name):
    n_arr = len(grads)

    def body(*refs):
        ins, got = refs[:n_arr], refs[n_arr:2 * n_arr]
        ssem, rsem = refs[2 * n_arr:]
        x, y, c, _, _, _ = _place()
        sib = (x, y, 1 - c)
        sends = []
        for k in reversed(range(n_arr)):
            for q in range(N_CHIPS):
                cp = _remote(ins[k].at[q, 1 - c], got[k].at[q], ssem.at[k, q], rsem.at[k, q], sib)
                cp.start()
                sends.append(cp)
        for k in range(n_arr):
            for q in range(N_CHIPS):
                _remote(got[k].at[q], got[k].at[q], ssem.at[k, q], rsem.at[k, q], sib).wait_recv()
        for cp in sends:
            cp.wait_send()

    shp = [jax.ShapeDtypeStruct((N_CHIPS,) + g.shape[2:], BF16) for g in grads]
    sem = pltpu.SemaphoreType.DMA((n_arr, N_CHIPS))
    outs = pl.pallas_call(
        body, name=name, out_shape=shp, in_specs=[ANY] * n_arr, out_specs=[ANY] * n_arr,
        scratch_shapes=[sem, sem], compiler_params=pltpu.CompilerParams(has_side_effects=True),
    )(*grads)
    return list(outs)


def _rs_share(bufs, name):
    n = len(bufs)

    def body(*refs):
        outs = refs[n:2 * n]
        ssem, rsem = refs[2 * n:]
        x, y, c, _, _, _ = _place()
        sib = (x, y, 1 - c)
        sends = []
        for k in range(n):
            for l in range(DEPTH):
                cp = _remote(outs[k].at[l, c], outs[k].at[l, c], ssem.at[k, l], rsem.at[k, l], sib)
                cp.start()
                sends.append(cp)
        for k in range(n):
            for l in range(DEPTH):
                dst = outs[k].at[l, 1 - c]
                _remote(dst, dst, ssem.at[k, l], rsem.at[k, l], sib).wait_recv()
        for cp in sends:
            cp.wait_send()

    sem = pltpu.SemaphoreType.DMA((n, DEPTH))
    outs = pl.pallas_call(
        body, name=name, out_shape=[jax.ShapeDtypeStruct(b.shape, b.dtype) for b in bufs], in_specs=[ANY] * n, out_specs=[ANY] * n,
        input_output_aliases={k: k for k in range(n)},
        scratch_shapes=[sem, sem], compiler_params=pltpu.CompilerParams(has_side_effects=True),
    )(*bufs)
    return list(outs)


def _piece(src, idx, rows, width=128, align=1, transposed=False):
    return dict(src=src, idx=idx, rows=rows, width=width, align=align, transposed=transposed)


def _all_reduce_pieces(inputs, pieces, out_shapes, writes, name):
    n_in, n_out = len(inputs), len(out_shapes)
    offs, R = [], 0
    for p in pieces:
        R = -(-R // p["align"]) * p["align"]
        offs.append(R)
        R += p["rows"]
    R = -(-R // 8) * 8

    def body(*refs):
        ins, outs, token_ref = refs[:n_in], refs[n_in:n_in + n_out], refs[n_in + n_out]
        pair_ref, chip_ref, sum_ref, ssem, rsem = refs[n_in + n_out + 1:]
        token_ref[...] = jnp.zeros_like(token_ref)
        x, y, c, chip, chips, chip_ids = _place()
        pair_ref[c] = jnp.zeros((R, 128), F32)
        for p, off in zip(pieces, offs):
            v = ins[p["src"]][...].T[p["idx"]] if p["transposed"] else ins[p["src"]][p["idx"]]
            pair_ref[c, off:off + p["rows"], 0:p["width"]] = v
        mine = _remote(pair_ref.at[c], pair_ref.at[c], ssem.at[3], rsem.at[3], (x, y, 1 - c))
        mine.start()
        _remote(pair_ref.at[1 - c], pair_ref.at[1 - c], ssem.at[3], rsem.at[3], (x, y, 1 - c)).wait_recv()
        chip_ref[chip] = pair_ref[0] + pair_ref[1]
        cps = [_remote(chip_ref.at[chip], chip_ref.at[chip], ssem.at[j], rsem.at[j], (*chips[j], c)) for j in range(3)]
        for cp in cps:
            cp.start()
        for j in range(3):
            slot = chip_ref.at[chip_ids[j]]
            _remote(slot, slot, ssem.at[j], rsem.at[j], (*chips[j], c)).wait_recv()
        acc = chip_ref[0]
        for s in range(1, N_CHIPS):
            acc = acc + chip_ref[s]
        sum_ref[...] = acc
        for o, idx, p in writes:
            outs[o][idx] = sum_ref[offs[p]:offs[p] + pieces[p]["rows"], 0:pieces[p]["width"]]
        for cp in cps + [mine]:
            cp.wait_send()

    vm = pl.BlockSpec(memory_space=pltpu.VMEM)
    outs = pl.pallas_call(
        body, name=name, out_shape=[jax.ShapeDtypeStruct(s, F32) for s in out_shapes] + [jax.ShapeDtypeStruct((8, 128), F32)],
        in_specs=[vm] * n_in, out_specs=[vm] * (n_out + 1),
        scratch_shapes=[pltpu.VMEM((2, R, 128), F32), pltpu.VMEM((N_CHIPS, R, 128), F32), pltpu.VMEM((R, 128), F32),
                        pltpu.SemaphoreType.DMA((4,)), pltpu.SemaphoreType.DMA((4,))],
        compiler_params=pltpu.CompilerParams(vmem_limit_bytes=VMEM_LIMIT),
    )(*inputs)
    return list(outs[:n_out]), outs[n_out]


def _lanes(width):
    return [slice(k, min(k + 128, width)) for k in range(0, width, 128)]


def _gather_small_weights(cvw_z, scw_z):
    pieces, writes = [], []
    for i, arr in enumerate((cvw_z, scw_z)):
        for l in range(DEPTH):
            for ln in _lanes(HALF):
                writes.append((i, (l, slice(None), ln), len(pieces)))
                pieces.append(_piece(i, (l, slice(None), ln), arr.shape[1], align=8))
    (cvw, scw), tok = _all_reduce_pieces([cvw_z, scw_z], pieces, [cvw_z.shape, scw_z.shape], writes, "ag_small")
    return cvw, scw, tok


SMALL_RAW = dict(norm_mix=(1, D_MODEL), norm_ffn=(1, D_MODEL), sg_ln_g=(1, HALF), sg_ln_b=(1, HALF), cv_b=(1, HALF), cv_ln_g=(1, HALF),
                 cv_ln_b=(1, HALF))


def _all_reduce_small_grads(raw, d_nfinal, loss):
    names = list(SMALL_RAW) + ["attn_sinks", "sg_b", "sc_w", "cv_w", "sg_w"]
    out_shape = dict(norm_mix=(DEPTH, D_MODEL), norm_ffn=(DEPTH, D_MODEL), sg_ln_g=(DEPTH, HALF), sg_ln_b=(DEPTH, HALF), cv_b=(DEPTH, HALF),
                     cv_ln_g=(DEPTH, HALF), cv_ln_b=(DEPTH, HALF), attn_sinks=(DEPTH, N_Q_HEADS), sg_b=(DEPTH, SG_GROUPS, SG_CHUNK),
                     sc_w=(DEPTH, SC_KERNEL, HALF), cv_w=(DEPTH, CV_KERNEL, HALF), sg_w=(DEPTH, SG_GROUPS, SG_CHUNK, SG_CHUNK))
    inputs, pieces, writes = [], [], []

    def add(src, idx, rows, out, out_idx, **kw):
        writes.append((names.index(out) if out in names else out, out_idx, len(pieces)))
        pieces.append(_piece(src, idx, rows, **kw))

    for l in range(DEPTH):
        row = slice(l, l + 1)
        for n, (_, width) in SMALL_RAW.items():
            inputs.append(raw[l][n])
            for ln in _lanes(width):
                add(len(inputs) - 1, (slice(0, 1), ln), 1, n, (row, ln))
        inputs.append(raw[l]["attn_sinks"])
        add(len(inputs) - 1, (slice(0, 1), slice(0, N_Q_HEADS)), 1, "attn_sinks", (row, slice(None)), width=N_Q_HEADS)
    for l in range(DEPTH):
        inputs.append(raw[l]["sg_b"])
        add(len(inputs) - 1, (slice(0, SG_GROUPS), slice(None)), SG_GROUPS, "sg_b", (l,), align=8, transposed=True)
        inputs.append(raw[l]["sc_w"])
        for ln in _lanes(HALF):
            add(len(inputs) - 1, (slice(0, SC_KERNEL), ln), SC_KERNEL, "sc_w", (l, slice(None), ln), align=8)
        inputs.append(raw[l]["cv_w"])
        for ln in _lanes(HALF):
            add(len(inputs) - 1, (slice(0, CV_KERNEL), ln), CV_KERNEL, "cv_w", (l, slice(None), ln), align=8)
        inputs.append(raw[l]["sg_w"])
        for g in range(SG_GROUPS):
            add(len(inputs) - 1, (g,), SG_CHUNK, "sg_w", (l, g), align=8)
    n_names = len(names)
    inputs.append(d_nfinal)
    for ln in _lanes(D_MODEL):
        add(len(inputs) - 1, (slice(0, 1), ln), 1, n_names, (slice(0, 1), ln))
    inputs.append(loss)
    add(len(inputs) - 1, (slice(0, 1), slice(None)), 1, n_names + 1, (slice(0, 1), slice(None)))
    outs, tok = _all_reduce_pieces(inputs, pieces, [out_shape[n] for n in names] + [(1, D_MODEL), (1, 128)], writes, "ar_small")
    return dict(zip(names, outs[:n_names])), outs[n_names], outs[n_names + 1], tok


def _small_views(raw):
    v = {n: raw[n][0] for n in SMALL_RAW}
    v.update(sg_w=raw["sg_w"], sg_b=raw["sg_b"][:, :SG_GROUPS].T, cv_w=raw["cv_w"][:CV_KERNEL],
             attn_sinks=raw["attn_sinks"][0, :N_Q_HEADS], sc_w=raw["sc_w"][:SC_KERNEL])
    return v


def _row_tile(rows, cols, n_arrays):
    budget = 20 * 1024 * 1024 // (n_arrays * 2 * cols * 4)
    tiles = [t for t in range(16, min(rows, budget) + 1, 16) if rows % t == 0]
    assert tiles, (rows, cols)
    return tiles[-1]


def _add_pairs(g, got, place, name):
    _, _, rows, cols = g.shape
    tr = _row_tile(rows, cols, 3)

    def body(place_ref, a_ref, b_ref, o_ref):
        del place_ref
        o_ref[...] = (a_ref[...].astype(F32) + b_ref[...].astype(F32)).astype(BF16)

    spec = pl.BlockSpec((None, tr, cols), lambda q, i, p: (q, i, 0))
    grid_spec = pltpu.PrefetchScalarGridSpec(
        num_scalar_prefetch=1, grid=(N_CHIPS, rows // tr),
        in_specs=[pl.BlockSpec((None, None, tr, cols), lambda q, i, p: (q, p[1], i, 0)), spec], out_specs=spec)
    return pl.pallas_call(body, name=name, grid_spec=grid_spec, out_shape=jax.ShapeDtypeStruct((N_CHIPS, rows, cols), BF16),
                          compiler_params=_params("parallel", "parallel"))(place, g, got)


def _sum_chips(own, recv, place, l, buf, name, after):
    _, rows, cols = own.shape
    tr = _row_tile(rows, cols, 4)

    def body(place_ref, own_ref, recv_ref, *rest):
        chip = place_ref[0]
        acc = own_ref[...].astype(F32)
        for j in range(1, N_CHIPS):
            acc = acc + recv_ref[lax.rem(chip + j, N_CHIPS)].astype(F32)
        rest[-1][...] = acc

    in_specs = [pl.BlockSpec((None, tr, cols), lambda i, p: (p[0], i, 0)), pl.BlockSpec((N_CHIPS, tr, cols), lambda i, p: (0, i, 0)), ANY]
    args = [place, own, recv, after]
    aliases = {}
    if buf is not None:
        in_specs.append(ANY)
        args.append(buf)
        aliases = {4: 0}
    grid_spec = pltpu.PrefetchScalarGridSpec(
        num_scalar_prefetch=1, grid=(rows // tr,), in_specs=in_specs,
        out_specs=pl.BlockSpec((None, None, tr, cols), lambda i, p: (l, p[1], i, 0)))
    return pl.pallas_call(body, name=name, grid_spec=grid_spec, out_shape=jax.ShapeDtypeStruct((DEPTH, 2, rows, cols), F32),
                          input_output_aliases=aliases, compiler_params=_params("parallel"))(*args)


def _adamw(w, g, m, v, name):
    shape = w.shape
    lead, (rows, cols) = shape[:-2], shape[-2:]
    tr = _row_tile(rows, cols, 8)

    def body(w_ref, g_ref, m_ref, v_ref, go_ref, d_ref, mo_ref, vo_ref):
        gv = g_ref[...]
        go_ref[...] = gv
        mn = ADAM_B1 * m_ref[...] + (1.0 - ADAM_B1) * gv
        vn = ADAM_B2 * v_ref[...] + (1.0 - ADAM_B2) * (gv * gv)
        m_hat = mn / (1.0 - ADAM_B1 ** ADAM_STEP)
        v_hat = vn / (1.0 - ADAM_B2 ** ADAM_STEP)
        d_ref[...] = -ADAM_LR * (m_hat / (jnp.sqrt(v_hat) + ADAM_EPS) + ADAM_WD * w_ref[...])
        mo_ref[...] = mn
        vo_ref[...] = vn

    spec = pl.BlockSpec((None,) * len(lead) + (tr, cols), lambda *idx: (*idx, 0))
    grid = lead + (rows // tr,)
    return list(pl.pallas_call(body, name=name, grid=grid, in_specs=[spec] * 4, out_specs=[spec] * 4,
                               out_shape=[jax.ShapeDtypeStruct(shape, F32)] * 4,
                               compiler_params=_params(*(["parallel"] * len(grid))))(w, g, m, v))


def _adamw_small(ws, gs, ms, vs, name):
    n = len(ws)

    def body(*refs):
        for i in range(n):
            gv = refs[n + i][...]
            mn = ADAM_B1 * refs[2 * n + i][...] + (1.0 - ADAM_B1) * gv
            vn = ADAM_B2 * refs[3 * n + i][...] + (1.0 - ADAM_B2) * (gv * gv)
            m_hat = mn / (1.0 - ADAM_B1 ** ADAM_STEP)
            v_hat = vn / (1.0 - ADAM_B2 ** ADAM_STEP)
            refs[4 * n + i][...] = -ADAM_LR * (m_hat / (jnp.sqrt(v_hat) + ADAM_EPS) + ADAM_WD * refs[i][...])
            refs[5 * n + i][...] = mn
            refs[6 * n + i][...] = vn

    vm = pl.BlockSpec(memory_space=pltpu.VMEM)
    outs = pl.pallas_call(body, name=name, out_shape=[jax.ShapeDtypeStruct(t.shape, F32) for t in ws] * 3,
                          in_specs=[vm] * (4 * n), out_specs=[vm] * (3 * n),
                          compiler_params=pltpu.CompilerParams(vmem_limit_bytes=VMEM_LIMIT))(*ws, *gs, *ms, *vs)
    return outs[:n], outs[n:2 * n], outs[2 * n:]


SMALL = ("norm_mix", "sg_ln_g", "sg_ln_b", "sg_w", "sg_b", "cv_w", "cv_b", "cv_ln_g", "cv_ln_b", "attn_sinks", "sc_w", "norm_ffn", "norm_final")
ORDER = ("norm_mix", "w_in", "sg_ln_g", "sg_ln_b", "sg_w", "sg_b", "cv_w", "cv_b", "cv_ln_g", "cv_ln_b", "attn_sinks", "sc_w",
         "w_branch", "w_out", "norm_ffn", "w_gate_up", "w_down", "norm_final")


def kernel(x, norm_mix, w_in, sg_ln_g, sg_ln_b, sg_w, sg_b, cv_w, cv_b, cv_ln_g, cv_ln_b, attn_sinks, sc_w, w_branch, w_out, norm_ffn, w_gate_up, w_down, norm_final, loss_target, m_norm_mix, m_w_in, m_sg_ln_g, m_sg_ln_b, m_sg_w, m_sg_b, m_cv_w, m_cv_b, m_cv_ln_g, m_cv_ln_b, m_attn_sinks, m_sc_w, m_w_branch, m_w_out, m_norm_ffn, m_w_gate_up, m_w_down, m_norm_final, v_norm_mix, v_w_in, v_sg_ln_g, v_sg_ln_b, v_sg_w, v_sg_b, v_cv_w, v_cv_b, v_cv_ln_g, v_cv_ln_b, v_attn_sinks, v_sc_w, v_w_branch, v_w_out, v_norm_ffn, v_w_gate_up, v_w_down, v_norm_final):
    W = dict(norm_mix=norm_mix, w_in=w_in, sg_ln_g=sg_ln_g, sg_ln_b=sg_ln_b, sg_w=sg_w, sg_b=sg_b, cv_w=cv_w, cv_b=cv_b, cv_ln_g=cv_ln_g,
             cv_ln_b=cv_ln_b, attn_sinks=attn_sinks, sc_w=sc_w, w_branch=w_branch, w_out=w_out, norm_ffn=norm_ffn, w_gate_up=w_gate_up,
             w_down=w_down, norm_final=norm_final)
    M = dict(norm_mix=m_norm_mix, w_in=m_w_in, sg_ln_g=m_sg_ln_g, sg_ln_b=m_sg_ln_b, sg_w=m_sg_w, sg_b=m_sg_b, cv_w=m_cv_w, cv_b=m_cv_b,
             cv_ln_g=m_cv_ln_g, cv_ln_b=m_cv_ln_b, attn_sinks=m_attn_sinks, sc_w=m_sc_w, w_branch=m_w_branch, w_out=m_w_out,
             norm_ffn=m_norm_ffn, w_gate_up=m_w_gate_up, w_down=m_w_down, norm_final=m_norm_final)
    V = dict(norm_mix=v_norm_mix, w_in=v_w_in, sg_ln_g=v_sg_ln_g, sg_ln_b=v_sg_ln_b, sg_w=v_sg_w, sg_b=v_sg_b, cv_w=v_cv_w, cv_b=v_cv_b,
             cv_ln_g=v_cv_ln_g, cv_ln_b=v_cv_ln_b, attn_sinks=v_attn_sinks, sc_w=v_sc_w, w_branch=v_w_branch, w_out=v_w_out,
             norm_ffn=v_norm_ffn, w_gate_up=v_w_gate_up, w_down=v_w_down, norm_final=v_norm_final)
    mx, my, mc = lax.axis_index("x"), lax.axis_index("y"), lax.axis_index("c")
    chip = 2 * mx + my

    place = jnp.stack([chip, mc]).astype(jnp.int32)
    tables = _rope_tables(x.shape[1])
    land_shapes = [(N_CHIPS,) + HALF_SHAPE[n] for n in BIG]
    part_shapes = {n: (N_CHIPS,) + HALF_SHAPE[n][1:] for n in BIG}

    T_ = lambda t: jnp.swapaxes(t, 1, 2)
    Wt, Mt, Vt = ({**t, "w_in": T_(t["w_in"])} for t in (W, M, V))

    def shards_of(l, tok):
        return [(Wt[n][l] + tok[0, 0]).astype(BF16).reshape(HALF_SHAPE[n]) for n in BIG]

    def finish_gather(tag, handle, after):
        srcs, lands = _ici_wait(handle, after, f"ag_wait{tag}")
        return _ag_pair(srcs, lands, f"ag_pair{tag}")[0]

    def mix_weights(l, g_in):
        return dict(w_in=_w_in_layout(g_in[0].reshape(N_CHIPS, W_IN_SHARD, D_MODEL)), norm_mix=norm_mix[l][None], norm_ffn=norm_ffn[l][None],
                    mixer=_mixer_params(l, sg_ln_g, sg_ln_b, sg_w, sg_b, cvw_full, cv_b, cv_ln_g, cv_ln_b, attn_sinks, scw_full))

    def rest_weights(lw, g_rest):
        G = dict(zip(BIG[1:], g_rest))
        lw.update(w_branch=G["w_branch"].reshape(N_CHIPS, N_BRANCH, HALF, 256), w_out=G["w_out"].reshape(D_MODEL, D_MODEL),
                  w_gate_up=G["w_gate_up"].reshape(N_CHIPS, D_MODEL, GU_SHARD), w_down=G["w_down"].reshape(D_FF, D_MODEL))

    def shard_major(g):
        t = dict(g)
        if "w_in" in t:
            t["w_in"] = _w_in_unlayout(t["w_in"])
        return {n: t[n].reshape((N_CHIPS,) + HALF_SHAPE[n]) for n in BIG if n in t}

    zero_tok = jnp.zeros((8, 128), F32)
    south = (mc == 0).astype(F32)
    cvw_z = lax.dynamic_update_slice(jnp.zeros((DEPTH, CV_KERNEL, HALF), F32), cv_w * south, (0, 0, chip * 128))
    scw_z = lax.dynamic_update_slice(jnp.zeros((DEPTH, SC_KERNEL, HALF), F32), sc_w * south, (0, 0, chip * 128))
    cvw_full, scw_full, tok = _gather_small_weights(cvw_z, scw_z)
    handles = []
    for l in range(DEPTH):
        for tag, sl in (("in", slice(0, 1)), ("rest", slice(1, NB))):
            h, tok = _ici_start("gather", shards_of(l, tok)[sl], land_shapes[sl], f"ag_start{l}{tag}")
            handles.append(h)
    pending = {}

    def behind(l, key):
        def order(lw, token):
            if key == "mixer":
                lw["mixer"] = [lw["mixer"][0] + token[0, 0]] + lw["mixer"][1:]
            else:
                lw[key] = lw[key] + token[0, 0]
        return order

    def early_pair(tag, handle, order):
        def between(after, lw):
            srcs, lands = _ici_wait(handle, after, f"ag_wait{tag}")
            pending[tag], token = _d2d_start(srcs + lands, [], _forward_plan(len(srcs)), f"ag_pair_start{tag}")
            order(lw, token)
        return between

    def finish_pair(tag, after):
        arrays = _d2d_wait(pending.pop(tag), after, f"ag_pair_wait{tag}")
        return arrays[len(arrays) // 2:]

    lw0 = mix_weights(0, finish_gather("0in", handles[0], tok))
    mixed = _fwd_layer_mix(0, x[0], lw0, tables)
    rest_weights(lw0, finish_gather("0rest", handles[1], mixed[2]))
    x1, sv0 = _fwd_layer_rest(0, x[0], mixed, lw0, early_pair("1in", handles[2], behind(0, "norm_ffn")))
    lw1 = mix_weights(1, finish_pair("1in", x1))
    mixed = _fwd_layer_mix(1, x1, lw1, tables, early_pair("1rest", handles[3], behind(1, "mixer")))
    rest_weights(lw1, finish_pair("1rest", mixed[2]))
    x2, sv1 = _fwd_layer_rest(1, x1, mixed, lw1)
    dx, d_nfinal, loss = _final_loss(x2, norm_final[None], loss_target[0], 256, "final_loss")

    lw1["after"] = zero_tok
    carry, g_ffn1 = _bwd_layer_ffn(1, dx, lw1, sv1)
    g1 = shard_major(g_ffn1)
    names_f = list(g1)
    h_swap, tok = _d2d_start([g1[n] for n in names_f], [part_shapes[n] for n in names_f], _swap_plan(len(names_f)), "rs_pair_start1")
    behind(1, "mixer")(lw1, tok)
    def early_swap(tag):
        def between(d_win, lw):
            g = shard_major({"w_in": d_win})["w_in"]
            pending[tag], token = _d2d_start([g], [part_shapes["w_in"]], _swap_plan(1), f"rs_pair_start{tag}")
            behind(None, "norm_mix")(lw, token)
        return between

    dx, g_mix1 = _bwd_layer_mix(1, carry, lw1, sv1, tables, early_swap("1in"))
    swapped = _d2d_wait(h_swap, dx, "rs_pair_wait1")
    own_in, got_in = _d2d_wait(pending.pop("1in"), dx, "rs_pair_wait1in")
    names1 = ["w_in"] + names_f
    own1 = [own_in] + swapped[:len(names_f)]
    got1 = [got_in] + swapped[len(names_f):]
    part1 = [_add_pairs(own1[k], got1[k], place, f"rs_add1_{n}") for k, n in enumerate(names1)]
    hr1, tok = _ici_start("scatter", part1, [part_shapes[n] for n in names1], "rs_start1")

    def early_ffn_swap(grads, lw):
        g = shard_major(grads)
        pending["0ffn"], token = _d2d_start([g[n] for n in g], [part_shapes[n] for n in g], _swap_plan(len(g)), "rs_pair_start0ffn")
        behind(None, "norm_ffn")(lw, token)

    lw0["after"] = tok
    carry, g_ffn0 = _bwd_layer_ffn(0, dx, lw0, sv0, early_ffn_swap)
    g0 = shard_major({n: g_ffn0[n] for n in ("w_branch", "w_out")})
    names_a = list(g0) + ["w_gate_up", "w_down"]
    swapped = _d2d_wait(pending.pop("0ffn"), g_ffn0["w_branch"], "rs_pair_wait0ffn")
    own_a = [g0[n] for n in g0] + swapped[:2]
    got_a = _rs_pair([g0[n] for n in g0], "rs_pair0a") + swapped[2:]
    part_a = [_add_pairs(own_a[k], got_a[k], place, f"rs_add0a_{n}") for k, n in enumerate(names_a)]
    _, recv1 = _ici_wait(hr1, part_a[0], "rs_wait1")
    hra, tok = _ici_start("scatter", part_a, [part_shapes[n] for n in names_a], "rs_start0a")

    lw0["mixer"] = [lw0["mixer"][0] + tok[0, 0]] + lw0["mixer"][1:]
    dx, g_mix0 = _bwd_layer_mix(0, carry, lw0, sv0, tables, early_swap("0in"))
    _, recv_a = _ici_wait(hra, dx, "rs_wait0a")

    small_red, nf_red, loss_red, tok = _all_reduce_small_grads([{**g_ffn0, **g_mix0}, {**g_ffn1, **g_mix1}], d_nfinal, loss)
    small_red["norm_final"] = nf_red
    loss_out = loss_red[0, 0]
    for n in ("cv_w", "sc_w"):
        small_red[n] = lax.dynamic_slice_in_dim(small_red[n], chip * 128, 128, axis=2)

    own_in, got_in = _d2d_wait(pending.pop("0in"), tok, "rs_pair_wait0in")
    names_b, part_b = ["w_in"], [_add_pairs(own_in, got_in, place, "rs_add0b_w_in")]
    hrb, tok = _ici_start("scatter", part_b, [part_shapes[n] for n in names_b], "rs_start0b")
    bufs = {n: _sum_chips(part1[k], recv1[k], place, 1, None, f"rs_sum1_{n}", tok) for k, n in enumerate(names1)}
    for k, n in enumerate(names_a):
        bufs[n] = _sum_chips(part_a[k], recv_a[k], place, 0, bufs[n], f"rs_sum0_{n}", tok)
    shared = dict(zip(names_a, _rs_share([bufs[n] for n in names_a], "rs_share_a")))
    upd = {}
    for n in names_a:
        red = shared[n].reshape(W[n].shape)
        upd[n] = _adamw(W[n], red, M[n], V[n], f"adamw_{n}")
    two_d = lambda t: t[None] if t.ndim == 1 else t
    small_upd = _adamw_small(*([two_d(t[n]) for n in SMALL] for t in (W, small_red, M, V)), "adamw_small")
    for n, d, mo, vo in zip(SMALL, *small_upd):
        upd[n] = [t.reshape(W[n].shape) for t in (small_red[n], d, mo, vo)]

    _, recv_b = _ici_wait(hrb, upd[names_a[-1]][1], "rs_wait0b")
    for k, n in enumerate(names_b):
        bufs[n] = _sum_chips(part_b[k], recv_b[k], place, 0, bufs[n], f"rs_sum0_{n}", tok)
    shared = dict(zip(names_b, _rs_share([bufs[n] for n in names_b], "rs_share_b")))
    for n in names_b:
        red = shared[n].reshape(Wt[n].shape)
        upd[n] = [T_(t) for t in _adamw(Wt[n], red, Mt[n], Vt[n], f"adamw_{n}")]

    out = [loss_out, dx[None]]
    for k in range(4):
        out += [upd[n][k] for n in ORDER]
    return tuple(out)
```

```python
import functools
import math

import jax
import jax.numpy as jnp
from jax import lax
from jax.experimental import pallas as pl
from jax.experimental.pallas import tpu as pltpu

F32 = jnp.float32
BF16 = jnp.bfloat16

D_MODEL = 1024
DEPTH = 2
HALF = 512
SG_CHUNK = 128
SG_GROUPS = 4
CV_KERNEL = 31
HEAD_DIM = 64
N_Q_HEADS = 8
N_KV_HEADS = 2
Q_PER_KV = N_Q_HEADS // N_KV_HEADS
WINDOW = 128
ROPE_THETA = 10000.0
SC_KERNEL = 3
N_BRANCH = 4
D_FF = 2816
EPS = 1e-6
N_CHIPS = 4
N_DEV = 8

MIX_W = 4352
GATE_W = N_BRANCH * D_MODEL
PROJ_PAD = 2 * MIX_W
W_IN_SHARD = 2112
GU_SHARD = 1408
HALO = 128
CV_PAD = 32

ADAM_LR = 0.001
ADAM_B1 = 0.9
ADAM_B2 = 0.999
ADAM_EPS = 1e-08
ADAM_WD = 0.01
ADAM_STEP = 10

VMEM_LIMIT = 56 * 1024 * 1024
INV_SQRT2 = 1.0 / math.sqrt(2.0)
INV_SQRT_2PI = 1.0 / math.sqrt(2.0 * math.pi)
NEG_BIG = -1e30
MESH = pl.DeviceIdType.MESH

C_ZA, C_ZB, C_Q, C_K, C_V, C_ZD = 0, 1024, 2048, 2560, 2688, 2816


def _params(*sem):
    return pltpu.CompilerParams(dimension_semantics=sem, vmem_limit_bytes=VMEM_LIMIT)


def _sig(v):
    return 1.0 / (1.0 + jnp.exp(-v))


def _dot(a, b):
    return jnp.dot(a, b, preferred_element_type=F32)


def _dot_nt(a, b):
    return lax.dot_general(a, b, (((1,), (1,)), ((), ())), preferred_element_type=F32)


def _dot_tn(a, b):
    return lax.dot_general(a, b, (((0,), (0,)), ((), ())), preferred_element_type=F32)


def _full(shape):
    nd = len(shape)
    return pl.BlockSpec(shape, lambda *_: (0,) * nd)


def _rms_mm(x, g, w, tm, tn, name):
    T = x.shape[0]
    transposed = w.ndim == 2
    if transposed:
        N = w.shape[0]
        wspec = pl.BlockSpec((tn, D_MODEL), lambda i, j: (j, 0))
    else:
        tn = w.shape[2]
        N = w.shape[0] * tn
        wspec = pl.BlockSpec((None, D_MODEL, tn), lambda i, j: (j, 0, 0))

    def body(x_ref, g_ref, w_ref, o_ref, xn_ref):
        @pl.when(pl.program_id(1) == 0)
        def _():
            xv = x_ref[...]
            r = lax.rsqrt(jnp.mean(xv * xv, axis=-1, keepdims=True) + EPS)
            xn_ref[...] = (xv * r * g_ref[...]).astype(BF16)

        o_ref[...] = (_dot_nt if transposed else _dot)(xn_ref[...], w_ref[...]).astype(BF16)

    return pl.pallas_call(
        body, name=name, grid=(T // tm, N // tn),
        in_specs=[pl.BlockSpec((tm, D_MODEL), lambda i, j: (i, 0)), _full((1, D_MODEL)), wspec],
        out_specs=[pl.BlockSpec((tm, tn), lambda i, j: (i, j)), pl.BlockSpec((tm, D_MODEL), lambda i, j: (i, 0))],
        out_shape=[jax.ShapeDtypeStruct((T, N), BF16), jax.ShapeDtypeStruct((T, D_MODEL), BF16)],
        compiler_params=_params("parallel", "arbitrary"),
    )(x, g, w)


def _merge_fwd(x, ys, proj, wb, wo, tm, name):
    T = x.shape[0]

    def body(x_ref, ys_ref, zg_ref, wb_ref, wo_ref, xo_ref, mg_ref):
        merged = None
        for n in range(N_BRANCH):
            yn = ys_ref[:, n * HALF:(n + 1) * HALF]
            br = jnp.concatenate([_dot(yn, wb_ref[s, n]) for s in range(N_CHIPS)], axis=1)
            t = _sig(zg_ref[:, n * D_MODEL:(n + 1) * D_MODEL].astype(F32)) * br
            merged = t if merged is None else merged + t
        mb = merged.astype(BF16)
        mg_ref[...] = mb
        xo_ref[...] = x_ref[...] + _dot(mb, wo_ref[...])

    return pl.pallas_call(
        body, name=name, grid=(T // tm,),
        in_specs=[pl.BlockSpec((tm, D_MODEL), lambda i: (i, 0)), pl.BlockSpec((tm, N_BRANCH * HALF), lambda i: (i, 0)),
                  pl.BlockSpec((tm, GATE_W), lambda i: (i, 0)), _full(wb.shape), _full(wo.shape)],
        out_specs=[pl.BlockSpec((tm, D_MODEL), lambda i: (i, 0)), pl.BlockSpec((tm, D_MODEL), lambda i: (i, 0))],
        out_shape=[jax.ShapeDtypeStruct((T, D_MODEL), F32), jax.ShapeDtypeStruct((T, D_MODEL), BF16)],
        compiler_params=_params("parallel"),
    )(x, ys, proj, wb, wo)


def _ffn_down(xm, gu, wd, tm, name):
    T = xm.shape[0]

    def body(x_ref, gu_ref, wd_ref, o_ref):
        g = gu_ref[:, :D_FF].astype(F32)
        u = gu_ref[:, D_FF:].astype(F32)
        act = (g * _sig(g) * u).astype(BF16)
        o_ref[...] = x_ref[...] + _dot(act, wd_ref[...])

    return pl.pallas_call(
        body, name=name, grid=(T // tm,),
        in_specs=[pl.BlockSpec((tm, D_MODEL), lambda i: (i, 0)), pl.BlockSpec((tm, 2 * D_FF), lambda i: (i, 0)), _full(wd.shape)],
        out_specs=pl.BlockSpec((tm, D_MODEL), lambda i: (i, 0)),
        out_shape=jax.ShapeDtypeStruct((T, D_MODEL), F32),
        compiler_params=_params("parallel"),
    )(xm, gu, wd)


def _final_loss(x, g, tgt, tm, name):
    T = x.shape[0]

    def body(x_ref, g_ref, t_ref, dx_ref, dg_ref, ls_ref):
        @pl.when(pl.program_id(0) == 0)
        def _():
            dg_ref[...] = jnp.zeros_like(dg_ref)
            ls_ref[...] = jnp.zeros_like(ls_ref)

        xv = x_ref[...]
        gv = g_ref[...]
        r = lax.rsqrt(jnp.mean(xv * xv, axis=-1, keepdims=True) + EPS)
        xh = xv * r
        diff = xh * gv - t_ref[...]
        ls_ref[...] += jnp.full(ls_ref.shape, 0.5 / D_MODEL, F32) * jnp.sum(diff * diff)
        dy = diff * (1.0 / D_MODEL)
        dxh = dy * gv
        dx_ref[...] = r * (dxh - xh * jnp.mean(dxh * xh, axis=-1, keepdims=True))
        dg_ref[...] += jnp.sum(dy * xh, axis=0, keepdims=True)

    return pl.pallas_call(
        body, name=name, grid=(T // tm,),
        in_specs=[pl.BlockSpec((tm, D_MODEL), lambda i: (i, 0)), _full((1, D_MODEL)), pl.BlockSpec((tm, D_MODEL), lambda i: (i, 0))],
        out_specs=[pl.BlockSpec((tm, D_MODEL), lambda i: (i, 0)), _full((1, D_MODEL)), _full((1, 128))],
        out_shape=[jax.ShapeDtypeStruct((T, D_MODEL), F32), jax.ShapeDtypeStruct((1, D_MODEL), F32), jax.ShapeDtypeStruct((1, 128), F32)],
        compiler_params=_params("arbitrary"),
    )(x, g, tgt)


def _swiglu_bwd(dx, gu, wd, tm, name, after):
    T = dx.shape[0]

    def body(dx_ref, gu_ref, wd_ref, after_ref, dgu_ref, act_ref):
        del after_ref
        dact = _dot_nt(dx_ref[...].astype(BF16), wd_ref[...])
        g = gu_ref[:, :D_FF].astype(F32)
        u = gu_ref[:, D_FF:].astype(F32)
        s = _sig(g)
        silu = g * s
        act_ref[...] = (silu * u).astype(BF16)
        dgu_ref[:, :D_FF] = (dact * u * (s + silu * (1.0 - s))).astype(BF16)
        dgu_ref[:, D_FF:] = (dact * silu).astype(BF16)

    return pl.pallas_call(
        body, name=name, grid=(T // tm,),
        in_specs=[pl.BlockSpec((tm, D_MODEL), lambda i: (i, 0)), pl.BlockSpec((tm, 2 * D_FF), lambda i: (i, 0)), _full(wd.shape),
                  pl.BlockSpec(memory_space=pl.ANY)],
        out_specs=[pl.BlockSpec((tm, 2 * D_FF), lambda i: (i, 0)), pl.BlockSpec((tm, D_FF), lambda i: (i, 0))],
        out_shape=[jax.ShapeDtypeStruct((T, 2 * D_FF), BF16), jax.ShapeDtypeStruct((T, D_FF), BF16)],
        compiler_params=_params("parallel"),
    )(dx, gu, wd, after)


def _mm_tn(a, b, grid, a_block, a_map, b_block, b_map, o_shape, o_block, o_map, name, col_split=1):
    gk = grid[2]
    tm = [d for d in a_block if d is not None][-1]
    tn = [d for d in b_block if d is not None][-1]

    def body(a_ref, b_ref, o_ref, acc_ref):
        k = pl.program_id(2)
        p = _dot_tn(a_ref[...].astype(BF16), b_ref[...].astype(BF16))

        @pl.when(k == 0)
        def _():
            acc_ref[...] = p

        @pl.when(k > 0)
        def _():
            acc_ref[...] += p

        @pl.when(k == gk - 1)
        def _():
            if col_split == 1:
                o_ref[...] = acc_ref[...].astype(o_ref.dtype)
            else:
                w = tn // col_split
                for s in range(col_split):
                    o_ref[s] = acc_ref[:, s * w:(s + 1) * w].astype(o_ref.dtype)

    return pl.pallas_call(
        body, name=name, grid=grid,
        in_specs=[pl.BlockSpec(a_block, a_map), pl.BlockSpec(b_block, b_map)],
        out_specs=pl.BlockSpec(o_block, o_map),
        out_shape=jax.ShapeDtypeStruct(o_shape, BF16),
        scratch_shapes=[pltpu.VMEM((tm, tn), F32)],
        compiler_params=_params("parallel", "parallel", "arbitrary"),
    )(a, b)


def _mm_nt_rmsbwd(a, w, x, g, dres, tm, tk, name):
    T = x.shape[0]
    transposed = w.ndim == 2
    if transposed:
        gk = w.shape[0] // tk
        wspec = pl.BlockSpec((tk, D_MODEL), lambda i, k: (k, 0))
    else:
        tk = w.shape[2]
        gk = w.shape[0]
        wspec = pl.BlockSpec((None, D_MODEL, tk), lambda i, k: (k, 0, 0))

    def body(a_ref, w_ref, x_ref, g_ref, r_ref, dx_ref, dg_ref, acc_ref):
        i, k = pl.program_id(0), pl.program_id(1)
        p = (_dot if transposed else _dot_nt)(a_ref[...], w_ref[...])

        @pl.when(k == 0)
        def _():
            acc_ref[...] = p

        @pl.when(k > 0)
        def _():
            acc_ref[...] += p

        @pl.when(jnp.logical_and(i == 0, k == 0))
        def _():
            dg_ref[...] = jnp.zeros_like(dg_ref)

        @pl.when(k == gk - 1)
        def _():
            dh = acc_ref[...]
            xv = x_ref[...]
            r = lax.rsqrt(jnp.mean(xv * xv, axis=-1, keepdims=True) + EPS)
            xh = xv * r
            dxh = dh * g_ref[...]
            dx_ref[...] = r_ref[...] + r * (dxh - xh * jnp.mean(dxh * xh, axis=-1, keepdims=True))
            dg_ref[...] += jnp.sum(dh * xh, axis=0, keepdims=True)

    return pl.pallas_call(
        body, name=name, grid=(T // tm, gk),
        in_specs=[pl.BlockSpec((tm, tk), lambda i, k: (i, k)), wspec, pl.BlockSpec((tm, D_MODEL), lambda i, k: (i, 0)),
                  _full((1, D_MODEL)), pl.BlockSpec((tm, D_MODEL), lambda i, k: (i, 0))],
        out_specs=[pl.BlockSpec((tm, D_MODEL), lambda i, k: (i, 0)), _full((1, D_MODEL))],
        out_shape=[jax.ShapeDtypeStruct((T, D_MODEL), F32), jax.ShapeDtypeStruct((1, D_MODEL), F32)],
        scratch_shapes=[pltpu.VMEM((tm, D_MODEL), F32)],
        compiler_params=_params("arbitrary", "arbitrary"),
    )(a, w, x, g, dres)


def _merge_bwd(dxm, ys, proj, wb, wo, tm, name):
    T = dxm.shape[0]

    def body(dx_ref, ys_ref, zg_ref, wb_ref, wo_ref, dys_ref, dbr_ref, dp_ref):
        dmerged = _dot_nt(dx_ref[...].astype(BF16), wo_ref[...])
        for n in range(N_BRANCH):
            yn = ys_ref[:, n * HALF:(n + 1) * HALF]
            br = jnp.concatenate([_dot(yn, wb_ref[s, n]) for s in range(N_CHIPS)], axis=1)
            gt = _sig(zg_ref[:, n * D_MODEL:(n + 1) * D_MODEL].astype(F32))
            dbr = (gt * dmerged).astype(BF16)
            dbr_ref[:, n * D_MODEL:(n + 1) * D_MODEL] = dbr
            dp_ref[:, n * D_MODEL:(n + 1) * D_MODEL] = (dmerged * br * gt * (1.0 - gt)).astype(BF16)
            dy = None
            for s in range(N_CHIPS):
                t = _dot_nt(dbr[:, s * 256:(s + 1) * 256], wb_ref[s, n])
                dy = t if dy is None else dy + t
            dys_ref[:, n * HALF:(n + 1) * HALF] = dy.astype(BF16)
        dp_ref[:, GATE_W:] = jnp.zeros((tm, MIX_W - GATE_W), BF16)

    return pl.pallas_call(
        body, name=name, grid=(T // tm,),
        in_specs=[pl.BlockSpec((tm, D_MODEL), lambda i: (i, 0)), pl.BlockSpec((tm, N_BRANCH * HALF), lambda i: (i, 0)),
                  pl.BlockSpec((tm, GATE_W), lambda i: (i, 0)), _full(wb.shape), _full(wo.shape)],
        out_specs=[pl.BlockSpec((tm, N_BRANCH * HALF), lambda i: (i, 0)), pl.BlockSpec((tm, GATE_W), lambda i: (i, 0)),
                   pl.BlockSpec((tm, MIX_W), lambda i: (i, 0))],
        out_shape=[jax.ShapeDtypeStruct((T, N_BRANCH * HALF), BF16), jax.ShapeDtypeStruct((T, GATE_W), BF16),
                   jax.ShapeDtypeStruct((T, PROJ_PAD), BF16)],
        compiler_params=_params("parallel"),
    )(dxm, ys, proj, wb, wo)


def _gelu(v):
    return 0.5 * v * (1.0 + lax.erf(v * INV_SQRT2))


def _gelu_grad(v):
    return 0.5 * (1.0 + lax.erf(v * INV_SQRT2)) + v * jnp.exp(-0.5 * v * v) * INV_SQRT_2PI


def _rot_half(t):
    w = t.shape[1]
    lane = lax.broadcasted_iota(jnp.int32, t.shape, 1)
    return jnp.where((lane % HEAD_DIM) < HEAD_DIM // 2, pltpu.roll(t, w - HEAD_DIM // 2, 1), pltpu.roll(t, HEAD_DIM // 2, 1))


def _rope(t, cos, sin_signed):
    return t * cos + _rot_half(t) * sin_signed


def _rope_t(d, cos, sin_signed):
    return d * cos + _rot_half(d * sin_signed)


def _ln_fwd(v, g, b):
    mu = jnp.mean(v, axis=-1, keepdims=True)
    vc = v - mu
    r = lax.rsqrt(jnp.mean(vc * vc, axis=-1, keepdims=True) + EPS)
    vh = vc * r
    return vh * g + b, vh, r


def _ln_bwd(dn, vh, r, g):
    dvh = dn * g
    return r * (dvh - jnp.mean(dvh, axis=-1, keepdims=True) - vh * jnp.mean(dvh * vh, axis=-1, keepdims=True))


def _sublane_shifts(sh_ref, rows):
    for b in range(1, 8):
        sh_ref[b, 0:rows - 8, :] = sh_ref[0, pl.ds(b, rows - 8), :]


def _tap(sh_ref, off, n):
    return sh_ref[off % 8, pl.ds(off - off % 8, n), :]


def _tril_mask():
    return lax.broadcasted_iota(jnp.int32, (SG_CHUNK, SG_CHUNK), 0) >= lax.broadcasted_iota(jnp.int32, (SG_CHUNK, SG_CHUNK), 1)


def _band_masks():
    shape = (Q_PER_KV * WINDOW, 2 * WINDOW)
    row = lax.broadcasted_iota(jnp.int32, shape, 0) % WINDOW
    col = lax.broadcasted_iota(jnp.int32, shape, 1)
    band = (col > row) & (col <= row + WINDOW)
    return band, band & (col >= WINDOW)


def _attn_probs(qs, kh, sink_col, valid):
    s = jnp.where(valid, _dot_nt(qs, kh) * (HEAD_DIM ** -0.5), NEG_BIG)
    m = jnp.maximum(jnp.max(s, axis=-1, keepdims=True), sink_col)
    p = jnp.exp(s - m)
    es = jnp.exp(sink_col - m)
    inv = 1.0 / (jnp.sum(p, axis=-1, keepdims=True) + es)
    return p * inv, es * inv


def _sink_col(sinks_ref, h):
    return jnp.concatenate([jnp.broadcast_to(sinks_ref[:, h * Q_PER_KV + g:h * Q_PER_KV + g + 1], (WINDOW, 1))
                            for g in range(Q_PER_KV)], axis=0)


def _mixer_in_specs(TB, nb):
    r = TB // HALO
    last = nb * r - 1
    cur = pl.BlockSpec((TB, MIX_W), lambda i: (i, 1))
    prev = pl.BlockSpec((HALO, MIX_W), lambda i: (jnp.maximum(i * r - 1, 0), 1))
    nxt = pl.BlockSpec((HALO, MIX_W), lambda i: (jnp.minimum((i + 1) * r, last), 1))
    tcur = pl.BlockSpec((TB, 128), lambda i: (i, 0))
    tprev = pl.BlockSpec((HALO, 128), lambda i: (jnp.maximum(i * r - 1, 0), 0))
    tnxt = pl.BlockSpec((HALO, 128), lambda i: (jnp.minimum((i + 1) * r, last), 0))
    return cur, prev, nxt, tcur, tprev, tnxt


def _mixer_param_specs():
    return [_full((1, HALF)), _full((1, HALF)), _full((SG_GROUPS, SG_CHUNK, SG_CHUNK)), _full((SG_CHUNK, 128)),
            _full((32, HALF)), _full((1, HALF)), _full((1, HALF)), _full((1, HALF)), _full((1, 128)), _full((8, HALF))]


def _mixers_fwd(proj, cos_t, sin_t, mp, TB, name):
    T = proj.shape[0]
    nb = T // TB
    r = TB // HALO
    cur, prev, _, tcur, tprev, _ = _mixer_in_specs(TB, nb)

    def body(zc_ref, zp_ref, cc_ref, sc_ref, cp_ref, sp_ref,
             lg_ref, lb_ref, sgw_ref, sgb_ref, cvw_ref, cvb_ref, cvg_ref, cvbb_ref, sinks_ref, scw_ref,
             ys_ref, scr_ref, k_ref, v_ref, sh_ref):
        i = pl.program_id(0)
        pm = (i > 0).astype(F32)

        def colsE(c0, c1):
            return jnp.concatenate([zp_ref[:, c0:c1].astype(F32) * pm, zc_ref[:, c0:c1].astype(F32)], axis=0)

        a = _gelu(zc_ref[:, C_ZA:C_ZA + 2 * HALF].astype(F32))
        u = a[:, :HALF]
        vn, _, _ = _ln_fwd(a[:, HALF:], lg_ref[...], lb_ref[...])
        vnb = vn.astype(BF16)
        tril = _tril_mask()
        chunks = [slice(ci * SG_CHUNK, (ci + 1) * SG_CHUNK) for ci in range(r)]
        for g in range(SG_GROUPS):
            cols = slice(g * 128, (g + 1) * 128)
            wt = jnp.where(tril, sgw_ref[g], 0.0).astype(BF16)
            mixed = _dot(wt, jnp.concatenate([vnb[rows, cols] for rows in chunks], axis=1)) + sgb_ref[:, g:g + 1]
            for ci, rows in enumerate(chunks):
                ys_ref[rows, cols] = (u[rows, cols] * mixed[:, ci * 128:(ci + 1) * 128]).astype(BF16)

        def colsB(c0, c1):
            return jnp.concatenate([zp_ref[HALO - CV_PAD:, c0:c1].astype(F32) * pm, zc_ref[:, c0:c1].astype(F32)], axis=0)

        sh_ref[0] = colsB(C_ZB, C_ZB + HALF) * _sig(colsB(C_ZB + HALF, C_ZB + 2 * HALF))
        _sublane_shifts(sh_ref, TB + CV_PAD)
        c = jnp.broadcast_to(cvb_ref[...], (TB, HALF))
        for k in range(CV_KERNEL):
            c = c + cvw_ref[k:k + 1, :] * _tap(sh_ref, CV_PAD - (CV_KERNEL - 1) + k, TB)
        n, _, _ = _ln_fwd(c, cvg_ref[...], cvbb_ref[...])
        ys_ref[:, HALF:2 * HALF] = (n * _sig(n)).astype(BF16)

        zd = colsE(C_ZD + HALF, C_ZD + 3 * HALF)
        scr_ref[...] = zd[:, :HALF] * zd[:, HALF:]
        cv = None
        for k in range(SC_KERNEL):
            t = scw_ref[k:k + 1, :] * scr_ref[pl.ds(HALO - (SC_KERNEL - 1) + k, TB), :]
            cv = t if cv is None else cv + t
        ys_ref[:, 3 * HALF:4 * HALF] = (zc_ref[:, C_ZD:C_ZD + HALF].astype(F32) * cv).astype(BF16)

        cosE = jnp.concatenate([cp_ref[...], cc_ref[...]], axis=0)
        sinE = jnp.concatenate([sp_ref[...], sc_ref[...]], axis=0)
        k_ref[...] = _rope(colsE(C_K, C_K + 128), cosE, sinE).astype(BF16)
        v_ref[...] = colsE(C_V, C_V + 128).astype(BF16)
        cosC, sinC = cc_ref[...], sc_ref[...]
        q = jnp.concatenate([_rope(zc_ref[:, C_Q + 128 * j:C_Q + 128 * (j + 1)].astype(F32), cosC, sinC)
                             for j in range(4)], axis=1).astype(BF16)
        in_band, in_band_cur = _band_masks()
        sink_cols = [_sink_col(sinks_ref, h) for h in range(N_KV_HEADS)]
        for qb in range(r):
            valid = in_band if qb else in_band_cur | (in_band & (i > 0))
            for h in range(N_KV_HEADS):
                hc = slice(h * HEAD_DIM, (h + 1) * HEAD_DIM)
                kh = k_ref[qb * WINDOW:qb * WINDOW + 2 * WINDOW, hc]
                vh = v_ref[qb * WINDOW:qb * WINDOW + 2 * WINDOW, hc]
                qs = jnp.concatenate([q[qb * WINDOW:(qb + 1) * WINDOW, (h * Q_PER_KV + g) * HEAD_DIM:(h * Q_PER_KV + g + 1) * HEAD_DIM]
                                      for g in range(Q_PER_KV)], axis=0)
                probs, _ = _attn_probs(qs, kh, sink_cols[h], valid)
                o = _dot(probs.astype(BF16), vh)
                for g in range(Q_PER_KV):
                    c0 = 2 * HALF + (h * Q_PER_KV + g) * HEAD_DIM
                    ys_ref[qb * WINDOW:(qb + 1) * WINDOW, c0:c0 + HEAD_DIM] = o[g * WINDOW:(g + 1) * WINDOW].astype(BF16)

    return pl.pallas_call(
        body, name=name, grid=(nb,),
        in_specs=[cur, prev, tcur, tcur, tprev, tprev] + _mixer_param_specs(),
        out_specs=pl.BlockSpec((TB, 4 * HALF), lambda i: (i, 0)),
        out_shape=jax.ShapeDtypeStruct((T, 4 * HALF), BF16),
        scratch_shapes=[pltpu.VMEM((TB + HALO, HALF), F32), pltpu.VMEM((TB + HALO, 128), BF16), pltpu.VMEM((TB + HALO, 128), BF16),
                        pltpu.VMEM((8, TB + CV_PAD, HALF), F32)],
        compiler_params=_params("parallel"),
    )(proj, proj, cos_t, sin_t, cos_t, sin_t, *mp)


def _mixers_bwd(proj, dys, dproj, cos_t, sin_t, mp, TB, name):
    T = proj.shape[0]
    nb = T // TB
    r = TB // HALO
    RE = TB + 2 * HALO
    RC = TB + HALO
    cur, prev, nxt, tcur, tprev, tnxt = _mixer_in_specs(TB, nb)
    dcur = pl.BlockSpec((TB, 4 * HALF), lambda i: (i, 0))
    dnxt = pl.BlockSpec((HALO, 4 * HALF), lambda i: (jnp.minimum((i + 1) * r, nb * r - 1), 0))

    def body(zc_ref, zp_ref, zn_ref, dyc_ref, dyn_ref, cc_ref, sc_ref, cp_ref, sp_ref, cn_ref, sn_ref,
             lg_ref, lb_ref, sgw_ref, sgb_ref, cvw_ref, cvb_ref, cvg_ref, cvbb_ref, sinks_ref, scw_ref, dp_in_ref,
             dz_ref, dlg_ref, dlb_ref, dsgw_ref, dsgb_ref, dcvw_ref, dcvb_ref, dcvg_ref, dcvbb_ref, dsink_ref, dscw_ref,
             scr_ref, scr2_ref, k_ref, v_ref, dk_ref, dv_ref, dq_ref, sh_ref, sh2_ref):
        del dp_in_ref
        i = pl.program_id(0)
        pm = (i > 0).astype(F32)
        nm = (i < nb - 1).astype(F32)

        @pl.when(i == 0)
        def _():
            for ref in (dlg_ref, dlb_ref, dsgw_ref, dsgb_ref, dcvw_ref, dcvb_ref, dcvg_ref, dcvbb_ref, dsink_ref, dscw_ref):
                ref[...] = jnp.zeros_like(ref)

        def colsE(c0, c1):
            return jnp.concatenate([zp_ref[:, c0:c1].astype(F32) * pm, zc_ref[:, c0:c1].astype(F32),
                                    zn_ref[:, c0:c1].astype(F32)], axis=0)

        def colsC(c0, c1):
            return jnp.concatenate([zc_ref[:, c0:c1].astype(F32), zn_ref[:, c0:c1].astype(F32)], axis=0)

        def dyC(c0, c1):
            return jnp.concatenate([dyc_ref[:, c0:c1].astype(F32), dyn_ref[:, c0:c1].astype(F32) * nm], axis=0)

        za = zc_ref[:, C_ZA:C_ZA + 2 * HALF].astype(F32)
        a = _gelu(za)
        u = a[:, :HALF]
        lg = lg_ref[...]
        vn, vh, rs = _ln_fwd(a[:, HALF:], lg, lb_ref[...])
        vnb = vn.astype(BF16)
        dya = dyc_ref[:, 0:HALF].astype(F32)
        tril = _tril_mask()
        lane128 = lax.broadcasted_iota(jnp.int32, (SG_CHUNK, 128), 1)
        chunks = [slice(ci * SG_CHUNK, (ci + 1) * SG_CHUNK) for ci in range(r)]
        side = lambda t, cols: jnp.concatenate([t[rows, cols] for rows in chunks], axis=1)
        for g in range(SG_GROUPS):
            cols = slice(g * 128, (g + 1) * 128)
            wt = jnp.where(tril, sgw_ref[g], 0.0).astype(BF16)
            vb = side(vnb, cols)
            dy_blk = side(dya, cols)
            du_g = dy_blk * (_dot(wt, vb) + sgb_ref[:, g:g + 1])
            dmix = dy_blk * side(u, cols)
            dmb = dmix.astype(BF16)
            dvn_g = _dot_tn(wt, dmb)
            dsgw_ref[g] += jnp.where(tril, _dot_nt(dmb, vb), 0.0)
            dsgb_ref[...] += jnp.where(lane128 == g, jnp.sum(dmix, axis=1, keepdims=True), 0.0)
            for ci, rows in enumerate(chunks):
                scr_ref[rows, cols] = du_g[:, ci * 128:(ci + 1) * 128]
                scr2_ref[rows, cols] = dvn_g[:, ci * 128:(ci + 1) * 128]
        du, dvn = scr_ref[0:TB, :], scr2_ref[0:TB, :]
        dlg_ref[...] += jnp.sum(dvn * vh, axis=0, keepdims=True)
        dlb_ref[...] += jnp.sum(dvn, axis=0, keepdims=True)
        dvv = _ln_bwd(dvn, vh, rs, lg)
        gg = _gelu_grad(za)
        dz_ref[:, C_ZA:C_ZA + HALF] = (du * gg[:, :HALF]).astype(BF16)
        dz_ref[:, C_ZA + HALF:C_ZA + 2 * HALF] = (dvv * gg[:, HALF:]).astype(BF16)

        RB = TB + CV_PAD

        def colsB(c0, c1):
            return jnp.concatenate([zp_ref[HALO - CV_PAD:, c0:c1].astype(F32) * pm, zc_ref[:, c0:c1].astype(F32),
                                    zn_ref[:CV_PAD, c0:c1].astype(F32)], axis=0)

        sh_ref[0] = colsB(C_ZB, C_ZB + HALF) * _sig(colsB(C_ZB + HALF, C_ZB + 2 * HALF))
        _sublane_shifts(sh_ref, RB + CV_PAD)
        c = jnp.broadcast_to(cvb_ref[...], (RB, HALF))
        for k in range(CV_KERNEL):
            c = c + cvw_ref[k:k + 1, :] * _tap(sh_ref, CV_PAD - (CV_KERNEL - 1) + k, RB)
        cvg = cvg_ref[...]
        n, ch, rc = _ln_fwd(c, cvg, cvbb_ref[...])
        sn = _sig(n)
        dyb = jnp.concatenate([dyc_ref[:, HALF:2 * HALF].astype(F32), dyn_ref[:CV_PAD, HALF:2 * HALF].astype(F32) * nm], axis=0)
        dn = dyb * (sn + n * sn * (1.0 - sn))
        dno = dn[:TB]
        dcvg_ref[...] += jnp.sum(dno * ch[:TB], axis=0, keepdims=True)
        dcvbb_ref[...] += jnp.sum(dno, axis=0, keepdims=True)
        dc = _ln_bwd(dn, ch, rc, cvg)
        sh2_ref[0] = dc
        _sublane_shifts(sh2_ref, RB)
        dcvb_ref[...] += jnp.sum(dc[:TB], axis=0, keepdims=True)
        dy0 = None
        for k in range(CV_KERNEL):
            wk = cvw_ref[k:k + 1, :]
            t = wk * _tap(sh2_ref, CV_KERNEL - 1 - k, TB)
            dy0 = t if dy0 is None else dy0 + t
            dcvw_ref[k:k + 1, :] += jnp.sum(dc[:TB] * _tap(sh_ref, CV_PAD - (CV_KERNEL - 1) + k, TB), axis=0, keepdims=True)
        ab = zc_ref[:, C_ZB:C_ZB + HALF].astype(F32)
        sg = _sig(zc_ref[:, C_ZB + HALF:C_ZB + 2 * HALF].astype(F32))
        dz_ref[:, C_ZB:C_ZB + HALF] = (dy0 * sg).astype(BF16)
        dz_ref[:, C_ZB + HALF:C_ZB + 2 * HALF] = (dy0 * ab * sg * (1.0 - sg)).astype(BF16)

        zd = colsE(C_ZD + HALF, C_ZD + 3 * HALF)
        scr_ref[...] = zd[:, :HALF] * zd[:, HALF:]
        dcv = dyC(3 * HALF, 4 * HALF) * colsC(C_ZD, C_ZD + HALF)
        scr2_ref[...] = dcv
        cv = None
        dud = None
        for k in range(SC_KERNEL):
            wk = scw_ref[k:k + 1, :]
            us = scr_ref[pl.ds(HALO - (SC_KERNEL - 1) + k, TB), :]
            t = wk * us
            cv = t if cv is None else cv + t
            t2 = wk * scr2_ref[pl.ds(SC_KERNEL - 1 - k, TB), :]
            dud = t2 if dud is None else dud + t2
            dscw_ref[k:k + 1, :] += jnp.sum(dcv[:TB] * us, axis=0, keepdims=True)
        dz_ref[:, C_ZD:C_ZD + HALF] = (dyc_ref[:, 3 * HALF:4 * HALF].astype(F32) * cv).astype(BF16)
        dz_ref[:, C_ZD + HALF:C_ZD + 2 * HALF] = (dud * zc_ref[:, C_ZD + 2 * HALF:C_ZD + 3 * HALF].astype(F32)).astype(BF16)
        dz_ref[:, C_ZD + 2 * HALF:C_ZD + 3 * HALF] = (dud * zc_ref[:, C_ZD + HALF:C_ZD + 2 * HALF].astype(F32)).astype(BF16)

        cosE = jnp.concatenate([cp_ref[...], cc_ref[...], cn_ref[...]], axis=0)
        sinE = jnp.concatenate([sp_ref[...], sc_ref[...], sn_ref[...]], axis=0)
        k_ref[...] = _rope(colsE(C_K, C_K + 128), cosE, sinE).astype(BF16)
        v_ref[...] = colsE(C_V, C_V + 128).astype(BF16)
        dk_ref[...] = jnp.zeros_like(dk_ref)
        dv_ref[...] = jnp.zeros_like(dv_ref)
        q = jnp.concatenate([_rope(colsC(C_Q + 128 * j, C_Q + 128 * (j + 1)), cosE[HALO:], sinE[HALO:])
                             for j in range(4)], axis=1).astype(BF16)
        dO = dyC(2 * HALF, 3 * HALF).astype(BF16)
        lane_s = lax.broadcasted_iota(jnp.int32, (1, 128), 1)
        in_band, in_band_cur = _band_masks()
        sink_cols = [_sink_col(sinks_ref, h) for h in range(N_KV_HEADS)]
        for qb in range(r + 1):
            valid = in_band if qb else in_band_cur | (in_band & (i > 0))
            rows = slice(qb * WINDOW, (qb + 1) * WINDOW)
            band = slice(qb * WINDOW, qb * WINDOW + 2 * WINDOW)
            for h in range(N_KV_HEADS):
                hc = slice(h * HEAD_DIM, (h + 1) * HEAD_DIM)
                kh = k_ref[band, hc]
                vh_ = v_ref[band, hc]
                heads = [slice((h * Q_PER_KV + g) * HEAD_DIM, (h * Q_PER_KV + g + 1) * HEAD_DIM) for g in range(Q_PER_KV)]
                qs = jnp.concatenate([q[rows, hs] for hs in heads], axis=0)
                dos = jnp.concatenate([dO[rows, hs] for hs in heads], axis=0)
                probs, p_sink = _attn_probs(qs, kh, sink_cols[h], valid)
                dP = _dot_nt(dos, vh_)
                rsum = jnp.sum(probs * dP, axis=-1, keepdims=True)
                dS = (probs * (dP - rsum) * (HEAD_DIM ** -0.5)).astype(BF16)
                dk_ref[band, hc] += _dot_tn(dS, qs)
                dv_ref[band, hc] += _dot_tn(probs.astype(BF16), dos)
                if qb < r:
                    dqs = _dot(dS, kh)
                    dsk = -p_sink * rsum
                    for g in range(Q_PER_KV):
                        dq_ref[rows, heads[g]] = dqs[g * WINDOW:(g + 1) * WINDOW]
                        dsink_ref[...] += jnp.where(lane_s == h * Q_PER_KV + g, jnp.sum(dsk[g * WINDOW:(g + 1) * WINDOW]), 0.0)
        cosC, sinC = cc_ref[...], sc_ref[...]
        for j in range(4):
            dz_ref[:, C_Q + 128 * j:C_Q + 128 * (j + 1)] = _rope_t(dq_ref[:, 128 * j:128 * (j + 1)], cosC, sinC).astype(BF16)
        dz_ref[:, C_K:C_K + 128] = _rope_t(dk_ref[HALO:HALO + TB, :], cosC, sinC).astype(BF16)
        dz_ref[:, C_V:C_V + 128] = dv_ref[HALO:HALO + TB, :].astype(BF16)

    small = [((1, HALF), F32), ((1, HALF), F32), ((SG_GROUPS, SG_CHUNK, SG_CHUNK), F32), ((SG_CHUNK, 128), F32),
             ((32, HALF), F32), ((1, HALF), F32), ((1, HALF), F32), ((1, HALF), F32), ((1, 128), F32), ((8, HALF), F32)]
    outs = pl.pallas_call(
        body, name=name, grid=(nb,),
        in_specs=[cur, prev, nxt, dcur, dnxt, tcur, tcur, tprev, tprev, tnxt, tnxt] + _mixer_param_specs()
                 + [pl.BlockSpec(memory_space=pl.ANY)],
        out_specs=[pl.BlockSpec((TB, MIX_W), lambda i: (i, 1))] + [_full(s) for s, _ in small],
        out_shape=[jax.ShapeDtypeStruct((T, PROJ_PAD), BF16)] + [jax.ShapeDtypeStruct(s, d) for s, d in small],
        scratch_shapes=[pltpu.VMEM((RE, HALF), F32), pltpu.VMEM((RC, HALF), F32), pltpu.VMEM((RE, 128), BF16), pltpu.VMEM((RE, 128), BF16),
                        pltpu.VMEM((RE, 128), F32), pltpu.VMEM((RE, 128), F32), pltpu.VMEM((TB, HALF), F32),
                        pltpu.VMEM((8, TB + 2 * CV_PAD, HALF), F32), pltpu.VMEM((8, TB + CV_PAD, HALF), F32)],
        input_output_aliases={21: 0},
        compiler_params=_params("arbitrary"),
    )(proj, proj, proj, dys, dys, cos_t, sin_t, cos_t, sin_t, cos_t, sin_t, *mp, dproj)
    return outs


def _rope_tables(T):
    pos = jnp.arange(T, dtype=F32)
    inv_freq = 1.0 / (ROPE_THETA ** (jnp.arange(0, HEAD_DIM, 2, dtype=F32) / HEAD_DIM))
    ang = pos[:, None] * inv_freq[None, :]
    cos, sin = jnp.cos(ang), jnp.sin(ang)
    cos_t = jnp.concatenate([cos, cos, cos, cos], axis=1)
    sin_t = jnp.concatenate([-sin, sin, -sin, sin], axis=1)
    return cos_t, sin_t


def _mixer_params(l, sg_ln_g, sg_ln_b, sg_w, sg_b, cv_w, cv_b, cv_ln_g, cv_ln_b, attn_sinks, sc_w):
    sgb_t = jnp.zeros((SG_CHUNK, 128), F32).at[:, :SG_GROUPS].set(sg_b[l].T)
    cvw = jnp.zeros((32, HALF), F32).at[:CV_KERNEL].set(cv_w[l])
    scw = jnp.zeros((8, HALF), F32).at[:SC_KERNEL].set(sc_w[l])
    sinks = jnp.zeros((1, 128), F32).at[0, :N_Q_HEADS].set(attn_sinks[l])
    return [sg_ln_g[l][None], sg_ln_b[l][None], sg_w[l], sgb_t, cvw, cv_b[l][None], cv_ln_g[l][None], cv_ln_b[l][None], sinks, scw]


def _w_in_layout(w_in_g):
    cut = MIX_W - 2 * W_IN_SHARD
    return jnp.concatenate([w_in_g[2][cut:], w_in_g[3], jnp.zeros((MIX_W - GATE_W, D_MODEL), w_in_g.dtype),
                            w_in_g[0], w_in_g[1], w_in_g[2][:cut]], axis=0)


def _w_in_unlayout(dw):
    cut = MIX_W - 2 * W_IN_SHARD
    return jnp.stack([dw[MIX_W:MIX_W + W_IN_SHARD], dw[MIX_W + W_IN_SHARD:MIX_W + 2 * W_IN_SHARD],
                      jnp.concatenate([dw[MIX_W + 2 * W_IN_SHARD:], dw[:W_IN_SHARD - cut]], axis=0),
                      dw[W_IN_SHARD - cut:GATE_W]], axis=0)


def _device_step(x, tgt, norm_mix, norm_ffn, norm_final, mixer_params, w_in_p, wb_g, wo_g, wgu_g, wd_g):
    T = x.shape[0]
    tables = _rope_tables(T)
    saved = []
    for l in range(DEPTH):
        lw = dict(w_in=w_in_p[l], w_branch=wb_g[l], w_out=wo_g[l], w_gate_up=wgu_g[l], w_down=wd_g[l],
                  norm_mix=norm_mix[l][None], norm_ffn=norm_ffn[l][None], mixer=mixer_params[l], after=jnp.zeros((8, 128), F32))
        x, sv = _fwd_layer(l, x, lw, tables)
        saved.append((lw, sv))
    dx, dnf, loss = _final_loss(x, norm_final[None], tgt, 256, "final_loss")
    grads = [None] * DEPTH
    for l in reversed(range(DEPTH)):
        lw, sv = saved[l]
        dxm, g_ffn = _bwd_layer_ffn(l, dx, lw, sv)
        dx, g_mix = _bwd_layer_mix(l, dxm, lw, sv, tables)
        raw = {**g_ffn, **g_mix}
        grads[l] = {**raw, **_small_views(raw)}
    return loss, dx, dnf[0], grads


MIX_BLOCK = 256


def _fwd_layer(l, x, lw, tables):
    return _fwd_layer_rest(l, x, _fwd_layer_mix(l, x, lw, tables), lw)


def _fwd_layer_mix(l, x, lw, tables, between=None):
    proj, xn = _rms_mm(x, lw["norm_mix"], lw["w_in"], min(x.shape[0], 1024), 2176, f"proj{l}")
    if between is not None:
        between(proj, lw)
    return proj, xn, _mixers_fwd(proj, *tables, lw["mixer"], MIX_BLOCK, f"mixers_fwd{l}")


def _fwd_layer_rest(l, x, mixed, lw, between=None):
    proj, xn, ys = mixed
    TM = min(x.shape[0], 1024)
    xm, merged = _merge_fwd(x, ys, proj, lw["w_branch"], lw["w_out"], min(x.shape[0], 512), f"merge_fwd{l}")
    if between is not None:
        between(xm, lw)
    gu, hn = _rms_mm(xm, lw["norm_ffn"], lw["w_gate_up"], TM, GU_SHARD, f"ffn_up{l}")
    x_out = _ffn_down(xm, gu, lw["w_down"], min(x.shape[0], 512), f"ffn_down{l}")
    return x_out, (x, proj, xn, ys, xm, merged, gu, hn)


def _bwd_layer_ffn(l, dx, lw, sv, between=None):
    x_in, proj, xn, ys, xm, merged, gu, hn = sv
    T = dx.shape[0]
    tkk = min(T, 1024)
    gk = T // tkk
    dgu, act = _swiglu_bwd(dx, gu, lw["w_down"], 256, f"swiglu_bwd{l}", lw["after"])
    d_wd = _mm_tn(act, dx, (2, 1, gk), (tkk, D_FF // 2), lambda i, j, k: (k, i), (tkk, D_MODEL), lambda i, j, k: (k, 0),
                  (D_FF, D_MODEL), (D_FF // 2, D_MODEL), lambda i, j, k: (i, 0), f"dw_down{l}")
    d_wgu = _mm_tn(hn, dgu, (1, N_CHIPS, gk), (tkk, D_MODEL), lambda i, j, k: (k, 0), (tkk, GU_SHARD), lambda i, j, k: (k, j),
                   (N_CHIPS, D_MODEL, GU_SHARD), (None, D_MODEL, GU_SHARD), lambda i, j, k: (j, 0, 0), f"dw_gate_up{l}")
    if between is not None:
        between(dict(w_gate_up=d_wgu, w_down=d_wd), lw)
    dxm, d_nffn = _mm_nt_rmsbwd(dgu, lw["w_gate_up"], xm, lw["norm_ffn"], dx, min(T, 1024), GU_SHARD, f"ffn_up_bwd{l}")
    dys, dbr, dproj = _merge_bwd(dxm, ys, proj, lw["w_branch"], lw["w_out"], 256, f"merge_bwd{l}")
    d_wo = _mm_tn(merged, dxm, (2, 1, gk), (tkk, 512), lambda i, j, k: (k, i), (tkk, D_MODEL), lambda i, j, k: (k, 0),
                  (D_MODEL, D_MODEL), (512, D_MODEL), lambda i, j, k: (i, 0), f"dw_out{l}")
    d_wb = _mm_tn(ys, dbr, (N_BRANCH, 1, gk), (tkk, HALF), lambda i, j, k: (k, i), (tkk, D_MODEL), lambda i, j, k: (k, i),
                  (N_CHIPS, N_BRANCH, HALF, 256), (N_CHIPS, None, HALF, 256), lambda i, j, k: (0, i, 0, 0), f"dw_branch{l}", col_split=N_CHIPS)
    return (dxm, dys, dproj), dict(w_branch=d_wb, w_out=d_wo, w_gate_up=d_wgu, w_down=d_wd, norm_ffn=d_nffn)


def _bwd_layer_mix(l, carry, lw, sv, tables, between=None):
    dxm, dys, dproj = carry
    x_in, proj, xn, ys, xm, merged, gu, hn = sv
    T = dxm.shape[0]
    tkk = min(T, 1024)
    gk = T // tkk
    mb = _mixers_bwd(proj, dys, dproj, *tables, lw["mixer"], MIX_BLOCK, f"mixers_bwd{l}")
    dproj = mb[0]
    d_win = _mm_tn(dproj, xn, (PROJ_PAD // 2176, 1, gk), (tkk, 2176), lambda i, j, k: (k, i), (tkk, D_MODEL), lambda i, j, k: (k, 0),
                   (PROJ_PAD, D_MODEL), (2176, D_MODEL), lambda i, j, k: (i, 0), f"dw_in{l}")
    if between is not None:
        between(d_win, lw)
    dx, d_nmix = _mm_nt_rmsbwd(dproj, lw["w_in"], x_in, lw["norm_mix"], dxm, min(T, 1024), 2176, f"proj_bwd{l}")
    return dx, dict(w_in=d_win, norm_mix=d_nmix, sg_ln_g=mb[1], sg_ln_b=mb[2], sg_w=mb[3], sg_b=mb[4], cv_w=mb[5], cv_b=mb[6],
                    cv_ln_g=mb[7], cv_ln_b=mb[8], attn_sinks=mb[9], sc_w=mb[10])


ANY = pl.BlockSpec(memory_space=pl.ANY)
BIG = ("w_in", "w_branch", "w_out", "w_gate_up", "w_down")
HALF_SHAPE = {"w_in": (2, W_IN_SHARD // 2, D_MODEL), "w_branch": (2, 1024, 256), "w_out": (2, 128, D_MODEL),
              "w_gate_up": (2, 512, GU_SHARD), "w_down": (2, 352, D_MODEL)}
NB = len(BIG)


def _place():
    x, y, c = lax.axis_index("x"), lax.axis_index("y"), lax.axis_index("c")
    chips = [(1 - x, y), (x, 1 - y), (1 - x, 1 - y)]
    return x, y, c, 2 * x + y, chips, [2 * px + py for px, py in chips]


def _remote(src, dst, ssem, rsem, dev):
    return pltpu.make_async_remote_copy(src_ref=src, dst_ref=dst, send_sem=ssem, recv_sem=rsem, device_id=dev, device_id_type=MESH)


HBM_SPEC = pl.BlockSpec(memory_space=pltpu.HBM)
SEM_SPEC = pl.BlockSpec(memory_space=pltpu.SEMAPHORE)
DATAFLOW = pltpu.SideEffectType.DATAFLOW_SIDE_EFFECTING


def _ici_ends(kind, src, land, j, c, chip, chip_ids):
    if kind == "gather":
        return src.at[c], land.at[chip, c], land.at[chip_ids[j], c]
    return src.at[chip_ids[j]], land.at[chip], land.at[chip_ids[j]]


def _ici_start(kind, srcs, land_shapes, name):
    n = len(srcs)

    def body(*refs):
        src, land = refs[:n], refs[n:2 * n]
        ssem, rsem, token = refs[2 * n], refs[2 * n + 1], refs[-1]
        x, y, c, chip, chips, chip_ids = _place()
        for k in range(n):
            for j in range(3):
                s, d, _ = _ici_ends(kind, src[k], land[k], j, c, chip, chip_ids)
                _remote(s, d, ssem.at[3 * k + j], rsem.at[3 * k + j], (*chips[j], c)).start()
        token[...] = jnp.zeros_like(token)

    sem = pltpu.SemaphoreType.DMA((3 * n,))
    outs = pl.pallas_call(
        body, name=name,
        out_shape=(sem, sem, *[pltpu.HBM(s.shape, s.dtype) for s in srcs], *[pltpu.HBM(sh, BF16) for sh in land_shapes],
                   jax.ShapeDtypeStruct((8, 128), F32)),
        in_specs=[HBM_SPEC] * (2 * n),
        out_specs=(SEM_SPEC, SEM_SPEC, *[HBM_SPEC] * (2 * n), pl.BlockSpec(memory_space=pltpu.VMEM)),
        input_output_aliases={i: 2 + i for i in range(2 * n)},
        compiler_params=pltpu.CompilerParams(has_side_effects=DATAFLOW),
    )(*[pltpu.with_memory_space_constraint(s, pltpu.HBM) for s in srcs],
      *[pltpu.with_memory_space_constraint(lax.empty(sh, BF16), pltpu.HBM) for sh in land_shapes])
    return (kind, outs[0], outs[1], list(outs[2:2 + n]), list(outs[2 + n:2 + 2 * n])), outs[-1]


def _ici_wait(handle, after, name):
    kind, ssem_in, rsem_in, srcs, lands = handle
    n = len(srcs)

    def body(*refs):
        src, land = refs[:n], refs[n:2 * n]
        ssem, rsem = refs[2 * n], refs[2 * n + 1]
        x, y, c, chip, chips, chip_ids = _place()
        for k in range(n):
            for j in range(3):
                s, _, mine = _ici_ends(kind, src[k], land[k], j, c, chip, chip_ids)
                cp = _remote(s, mine, ssem.at[3 * k + j], rsem.at[3 * k + j], (*chips[j], c))
                cp.wait_send()
                cp.wait_recv()

    outs = pl.pallas_call(
        body, name=name, out_shape=[pltpu.HBM(t.shape, t.dtype) for t in srcs + lands],
        in_specs=[HBM_SPEC] * (2 * n) + [SEM_SPEC, SEM_SPEC, ANY], out_specs=[HBM_SPEC] * (2 * n),
        input_output_aliases={i: i for i in range(2 * n)},
        compiler_params=pltpu.CompilerParams(has_side_effects=DATAFLOW),
    )(*srcs, *lands, ssem_in, rsem_in, after)
    return list(outs[:n]), list(outs[n:])


def _ag_pair(shards, lands, name):
    n = len(shards)

    def body(*refs):
        ins, outs = refs[:n], refs[2 * n:3 * n]
        token = refs[3 * n]
        s_fwd, r_fwd, s_own, r_own = refs[3 * n + 1:]
        x, y, c, chip, chips, chip_ids = _place()
        sib = (x, y, 1 - c)
        cps = []
        for k in range(n):
            cp = _remote(ins[k], outs[k].at[chip], s_own.at[k], r_own.at[k], sib)
            cp.start()
            cps.append(cp)
            for j in range(3):
                got = outs[k].at[chip_ids[j], c]
                cp = _remote(got, got, s_fwd.at[k, j], r_fwd.at[k, j], sib)
                cp.start()
                cps.append(cp)
        for k in range(n):
            _remote(ins[k], outs[k].at[chip], s_own.at[k], r_own.at[k], sib).wait_recv()
            for j in range(3):
                got = outs[k].at[chip_ids[j], 1 - c]
                _remote(got, got, s_fwd.at[k, j], r_fwd.at[k, j], sib).wait_recv()
        for cp in cps:
            cp.wait_send()
        token[...] = jnp.zeros_like(token)

    sem, sem1 = pltpu.SemaphoreType.DMA((n, 3)), pltpu.SemaphoreType.DMA((n,))
    outs = pl.pallas_call(
        body, name=name, out_shape=[jax.ShapeDtypeStruct(t.shape, t.dtype) for t in lands] + [jax.ShapeDtypeStruct((8, 128), F32)],
        in_specs=[ANY] * (2 * n), out_specs=[ANY] * n + [pl.BlockSpec(memory_space=pltpu.VMEM)],
        input_output_aliases={n + k: k for k in range(n)},
        scratch_shapes=[sem, sem, sem1, sem1], compiler_params=pltpu.CompilerParams(has_side_effects=True),
    )(*shards, *lands)
    return list(outs[:n]), outs[n]


def _forward_plan(n):
    def plan(refs, c, chip, chip_ids):
        out = []
        for k in range(n):
            shard, land = refs[k], refs[n + k]
            out.append((shard, land.at[chip], land.at[chip]))
            out += [(land.at[q, c], land.at[q, c], land.at[q, 1 - c]) for q in chip_ids]
        return out
    return plan, 4 * n


def _share_plan(n):
    def plan(refs, c, chip, chip_ids):
        return [(refs[k].at[l, c], refs[k].at[l, c], refs[k].at[l, 1 - c]) for k in range(n) for l in range(DEPTH)]
    return plan, DEPTH * n


def _swap_plan(n):
    def plan(refs, c, chip, chip_ids):
        return [(refs[k].at[q, 1 - c], refs[n + k].at[q], refs[n + k].at[q]) for k in range(n) for q in range(N_CHIPS)]
    return plan, N_CHIPS * n


def _d2d_start(arrays, new_shapes, plan_n, name):
    plan, n_copies = plan_n
    n = len(arrays) + len(new_shapes)

    def body(*refs):
        ssem, rsem, token = refs[n], refs[n + 1], refs[-1]
        x, y, c, chip, _, chip_ids = _place()
        for i, (s, d, _) in enumerate(plan(refs[:n], c, chip, chip_ids)):
            _remote(s, d, ssem.at[i], rsem.at[i], (x, y, 1 - c)).start()
        token[...] = jnp.zeros_like(token)

    sem = pltpu.SemaphoreType.DMA((n_copies,))
    args = [pltpu.with_memory_space_constraint(t, pltpu.HBM) for t in arrays] + \
           [pltpu.with_memory_space_constraint(lax.empty(sh, BF16), pltpu.HBM) for sh in new_shapes]
    outs = pl.pallas_call(
        body, name=name,
        out_shape=(sem, sem, *[pltpu.HBM(t.shape, t.dtype) for t in args], jax.ShapeDtypeStruct((8, 128), F32)),
        in_specs=[HBM_SPEC] * n, out_specs=(SEM_SPEC, SEM_SPEC, *[HBM_SPEC] * n, pl.BlockSpec(memory_space=pltpu.VMEM)),
        input_output_aliases={i: 2 + i for i in range(n)},
        compiler_params=pltpu.CompilerParams(has_side_effects=DATAFLOW),
    )(*args)
    return (plan, outs[0], outs[1], list(outs[2:2 + n])), outs[-1]


def _d2d_wait(handle, after, name):
    plan, ssem_in, rsem_in, arrays = handle
    n = len(arrays)

    def body(*refs):
        ssem, rsem = refs[n], refs[n + 1]
        x, y, c, chip, _, chip_ids = _place()
        for i, (s, _, mine) in enumerate(plan(refs[:n], c, chip, chip_ids)):
            cp = _remote(s, mine, ssem.at[i], rsem.at[i], (x, y, 1 - c))
            cp.wait_send()
            cp.wait_recv()

    outs = pl.pallas_call(
        body, name=name, out_shape=[pltpu.HBM(t.shape, t.dtype) for t in arrays],
        in_specs=[HBM_SPEC] * n + [SEM_SPEC, SEM_SPEC, ANY], out_specs=[HBM_SPEC] * n,
        input_output_aliases={i: i for i in range(n)},
        compiler_params=pltpu.CompilerParams(has_side_effects=DATAFLOW),
    )(*arrays, ssem_in, rsem_in, after)
    return list(outs)


def _rs_pair(grads, name):
    n_arr = len(grads)

    def body(*refs):
        ins, got = refs[:n_arr], refs[n_arr:2 * n_arr]
        ssem, rsem = refs[2 * n_arr:]
        x, y, c, _, _, _ = _place()
        sib = (x, y, 1 - c)
        sends = []
        for k in reversed(range(n_arr)):
            for q in range(N_CHIPS):
                cp = _remote(ins[k].at[q, 1 - c], got[k].at[q], ssem.at[k, q], rsem.at[k, q], sib)
                cp.start()
                sends.append(cp)
        for k in range(n_arr):
            for q in range(N_CHIPS):
                _remote(got[k].at[q], got[k].at[q], ssem.at[k, q], rsem.at[k, q], sib).wait_recv()
        for cp in sends:
            cp.wait_send()

    shp = [jax.ShapeDtypeStruct((N_CHIPS,) + g.shape[2:], BF16) for g in grads]
    sem = pltpu.SemaphoreType.DMA((n_arr, N_CHIPS))
    outs = pl.pallas_call(
        body, name=name, out_shape=shp, in_specs=[ANY] * n_arr, out_specs=[ANY] * n_arr,
        scratch_shapes=[sem, sem], compiler_params=pltpu.CompilerParams(has_side_effects=True),
    )(*grads)
    return list(outs)


def _rs_share(bufs, name):
    n = len(bufs)

    def body(*refs):
        outs = refs[n:2 * n]
        ssem, rsem = refs[2 * n:]
        x, y, c, _, _, _ = _place()
        sib = (x, y, 1 - c)
        sends = []
        for k in range(n):
            for l in range(DEPTH):
                cp = _remote(outs[k].at[l, c], outs[k].at[l, c], ssem.at[k, l], rsem.at[k, l], sib)
                cp.start()
                sends.append(cp)
        for k in range(n):
            for l in range(DEPTH):
                dst = outs[k].at[l, 1 - c]
                _remote(dst, dst, ssem.at[k, l], rsem.at[k, l], sib).wait_recv()
        for cp in sends:
            cp.wait_send()

    sem = pltpu.SemaphoreType.DMA((n, DEPTH))
    outs = pl.pallas_call(
        body, name=name, out_shape=[jax.ShapeDtypeStruct(b.shape, b.dtype) for b in bufs], in_specs=[ANY] * n, out_specs=[ANY] * n,
        input_output_aliases={k: k for k in range(n)},
        scratch_shapes=[sem, sem], compiler_params=pltpu.CompilerParams(has_side_effects=True),
    )(*bufs)
    return list(outs)


def _piece(src, idx, rows, width=128, align=1, transposed=False):
    return dict(src=src, idx=idx, rows=rows, width=width, align=align, transposed=transposed)


def _all_reduce_pieces(inputs, pieces, out_shapes, writes, name):
    n_in, n_out = len(inputs), len(out_shapes)
    offs, R = [], 0
    for p in pieces:
        R = -(-R // p["align"]) * p["align"]
        offs.append(R)
        R += p["rows"]
    R = -(-R // 8) * 8

    def body(*refs):
        ins, outs, token_ref = refs[:n_in], refs[n_in:n_in + n_out], refs[n_in + n_out]
        pair_ref, chip_ref, sum_ref, ssem, rsem = refs[n_in + n_out + 1:]
        token_ref[...] = jnp.zeros_like(token_ref)
        x, y, c, chip, chips, chip_ids = _place()
        pair_ref[c] = jnp.zeros((R, 128), F32)
        for p, off in zip(pieces, offs):
            v = ins[p["src"]][...].T[p["idx"]] if p["transposed"] else ins[p["src"]][p["idx"]]
            pair_ref[c, off:off + p["rows"], 0:p["width"]] = v
        mine = _remote(pair_ref.at[c], pair_ref.at[c], ssem.at[3], rsem.at[3], (x, y, 1 - c))
        mine.start()
        _remote(pair_ref.at[1 - c], pair_ref.at[1 - c], ssem.at[3], rsem.at[3], (x, y, 1 - c)).wait_recv()
        chip_ref[chip] = pair_ref[0] + pair_ref[1]
        cps = [_remote(chip_ref.at[chip], chip_ref.at[chip], ssem.at[j], rsem.at[j], (*chips[j], c)) for j in range(3)]
        for cp in cps:
            cp.start()
        for j in range(3):
            slot = chip_ref.at[chip_ids[j]]
            _remote(slot, slot, ssem.at[j], rsem.at[j], (*chips[j], c)).wait_recv()
        acc = chip_ref[0]
        for s in range(1, N_CHIPS):
            acc = acc + chip_ref[s]
        sum_ref[...] = acc
        for o, idx, p in writes:
            outs[o][idx] = sum_ref[offs[p]:offs[p] + pieces[p]["rows"], 0:pieces[p]["width"]]
        for cp in cps + [mine]:
            cp.wait_send()

    vm = pl.BlockSpec(memory_space=pltpu.VMEM)
    outs = pl.pallas_call(
        body, name=name, out_shape=[jax.ShapeDtypeStruct(s, F32) for s in out_shapes] + [jax.ShapeDtypeStruct((8, 128), F32)],
        in_specs=[vm] * n_in, out_specs=[vm] * (n_out + 1),
        scratch_shapes=[pltpu.VMEM((2, R, 128), F32), pltpu.VMEM((N_CHIPS, R, 128), F32), pltpu.VMEM((R, 128), F32),
                        pltpu.SemaphoreType.DMA((4,)), pltpu.SemaphoreType.DMA((4,))],
        compiler_params=pltpu.CompilerParams(vmem_limit_bytes=VMEM_LIMIT),
    )(*inputs)
    return list(outs[:n_out]), outs[n_out]


def _lanes(width):
    return [slice(k, min(k + 128, width)) for k in range(0, width, 128)]


def _gather_small_weights(cvw_z, scw_z):
    pieces, writes = [], []
    for i, arr in enumerate((cvw_z, scw_z)):
        for l in range(DEPTH):
            for ln in _lanes(HALF):
                writes.append((i, (l, slice(None), ln), len(pieces)))
                pieces.append(_piece(i, (l, slice(None), ln), arr.shape[1], align=8))
    (cvw, scw), tok = _all_reduce_pieces([cvw_z, scw_z], pieces, [cvw_z.shape, scw_z.shape], writes, "ag_small")
    return cvw, scw, tok


SMALL_RAW = dict(norm_mix=(1, D_MODEL), norm_ffn=(1, D_MODEL), sg_ln_g=(1, HALF), sg_ln_b=(1, HALF), cv_b=(1, HALF), cv_ln_g=(1, HALF),
                 cv_ln_b=(1, HALF))


def _all_reduce_small_grads(raw, d_nfinal, loss):
    names = list(SMALL_RAW) + ["attn_sinks", "sg_b", "sc_w", "cv_w", "sg_w"]
    out_shape = dict(norm_mix=(DEPTH, D_MODEL), norm_ffn=(DEPTH, D_MODEL), sg_ln_g=(DEPTH, HALF), sg_ln_b=(DEPTH, HALF), cv_b=(DEPTH, HALF),
                     cv_ln_g=(DEPTH, HALF), cv_ln_b=(DEPTH, HALF), attn_sinks=(DEPTH, N_Q_HEADS), sg_b=(DEPTH, SG_GROUPS, SG_CHUNK),
                     sc_w=(DEPTH, SC_KERNEL, HALF), cv_w=(DEPTH, CV_KERNEL, HALF), sg_w=(DEPTH, SG_GROUPS, SG_CHUNK, SG_CHUNK))
    inputs, pieces, writes = [], [], []

    def add(src, idx, rows, out, out_idx, **kw):
        writes.append((names.index(out) if out in names else out, out_idx, len(pieces)))
        pieces.append(_piece(src, idx, rows, **kw))

    for l in range(DEPTH):
        row = slice(l, l + 1)
        for n, (_, width) in SMALL_RAW.items():
            inputs.append(raw[l][n])
            for ln in _lanes(width):
                add(len(inputs) - 1, (slice(0, 1), ln), 1, n, (row, ln))
        inputs.append(raw[l]["attn_sinks"])
        add(len(inputs) - 1, (slice(0, 1), slice(0, N_Q_HEADS)), 1, "attn_sinks", (row, slice(None)), width=N_Q_HEADS)
    for l in range(DEPTH):
        inputs.append(raw[l]["sg_b"])
        add(len(inputs) - 1, (slice(0, SG_GROUPS), slice(None)), SG_GROUPS, "sg_b", (l,), align=8, transposed=True)
        inputs.append(raw[l]["sc_w"])
        for ln in _lanes(HALF):
            add(len(inputs) - 1, (slice(0, SC_KERNEL), ln), SC_KERNEL, "sc_w", (l, slice(None), ln), align=8)
        inputs.append(raw[l]["cv_w"])
        for ln in _lanes(HALF):
            add(len(inputs) - 1, (slice(0, CV_KERNEL), ln), CV_KERNEL, "cv_w", (l, slice(None), ln), align=8)
        inputs.append(raw[l]["sg_w"])
        for g in range(SG_GROUPS):
            add(len(inputs) - 1, (g,), SG_CHUNK, "sg_w", (l, g), align=8)
    n_names = len(names)
    inputs.append(d_nfinal)
    for ln in _lanes(D_MODEL):
        add(len(inputs) - 1, (slice(0, 1), ln), 1, n_names, (slice(0, 1), ln))
    inputs.append(loss)
    add(len(inputs) - 1, (slice(0, 1), slice(None)), 1, n_names + 1, (slice(0, 1), slice(None)))
    outs, tok = _all_reduce_pieces(inputs, pieces, [out_shape[n] for n in names] + [(1, D_MODEL), (1, 128)], writes, "ar_small")
    return dict(zip(names, outs[:n_names])), outs[n_names], outs[n_names + 1], tok


def _small_views(raw):
    v = {n: raw[n][0] for n in SMALL_RAW}
    v.update(sg_w=raw["sg_w"], sg_b=raw["sg_b"][:, :SG_GROUPS].T, cv_w=raw["cv_w"][:CV_KERNEL],
             attn_sinks=raw["attn_sinks"][0, :N_Q_HEADS], sc_w=raw["sc_w"][:SC_KERNEL])
    return v


def _row_tile(rows, cols, n_arrays):
    budget = 20 * 1024 * 1024 // (n_arrays * 2 * cols * 4)
    tiles = [t for t in range(16, min(rows, budget) + 1, 16) if rows % t == 0]
    assert tiles, (rows, cols)
    return tiles[-1]


def _add_pairs(g, got, place, name):
    _, _, rows, cols = g.shape
    tr = _row_tile(rows, cols, 3)

    def body(place_ref, a_ref, b_ref, o_ref):
        del place_ref
        o_ref[...] = (a_ref[...].astype(F32) + b_ref[...].astype(F32)).astype(BF16)

    spec = pl.BlockSpec((None, tr, cols), lambda q, i, p: (q, i, 0))
    grid_spec = pltpu.PrefetchScalarGridSpec(
        num_scalar_prefetch=1, grid=(N_CHIPS, rows // tr),
        in_specs=[pl.BlockSpec((None, None, tr, cols), lambda q, i, p: (q, p[1], i, 0)), spec], out_specs=spec)
    return pl.pallas_call(body, name=name, grid_spec=grid_spec, out_shape=jax.ShapeDtypeStruct((N_CHIPS, rows, cols), BF16),
                          compiler_params=_params("parallel", "parallel"))(place, g, got)


def _sum_chips(own, recv, place, l, buf, name, after):
    _, rows, cols = own.shape
    tr = _row_tile(rows, cols, 4)

    def body(place_ref, own_ref, recv_ref, *rest):
        chip = place_ref[0]
        acc = own_ref[...].astype(F32)
        for j in range(1, N_CHIPS):
            acc = acc + recv_ref[lax.rem(chip + j, N_CHIPS)].astype(F32)
        rest[-1][...] = acc

    in_specs = [pl.BlockSpec((None, tr, cols), lambda i, p: (p[0], i, 0)), pl.BlockSpec((N_CHIPS, tr, cols), lambda i, p: (0, i, 0)), ANY]
    args = [place, own, recv, after]
    aliases = {}
    if buf is not None:
        in_specs.append(ANY)
        args.append(buf)
        aliases = {4: 0}
    grid_spec = pltpu.PrefetchScalarGridSpec(
        num_scalar_prefetch=1, grid=(rows // tr,), in_specs=in_specs,
        out_specs=pl.BlockSpec((None, None, tr, cols), lambda i, p: (l, p[1], i, 0)))
    return pl.pallas_call(body, name=name, grid_spec=grid_spec, out_shape=jax.ShapeDtypeStruct((DEPTH, 2, rows, cols), F32),
                          input_output_aliases=aliases, compiler_params=_params("parallel"))(*args)


def _adamw(w, g, m, v, name):
    shape = w.shape
    lead, (rows, cols) = shape[:-2], shape[-2:]
    tr = _row_tile(rows, cols, 8)

    def body(w_ref, g_ref, m_ref, v_ref, go_ref, d_ref, mo_ref, vo_ref):
        gv = g_ref[...]
        go_ref[...] = gv
        mn = ADAM_B1 * m_ref[...] + (1.0 - ADAM_B1) * gv
        vn = ADAM_B2 * v_ref[...] + (1.0 - ADAM_B2) * (gv * gv)
        m_hat = mn / (1.0 - ADAM_B1 ** ADAM_STEP)
        v_hat = vn / (1.0 - ADAM_B2 ** ADAM_STEP)
        d_ref[...] = -ADAM_LR * (m_hat / (jnp.sqrt(v_hat) + ADAM_EPS) + ADAM_WD * w_ref[...])
        mo_ref[...] = mn
        vo_ref[...] = vn

    spec = pl.BlockSpec((None,) * len(lead) + (tr, cols), lambda *idx: (*idx, 0))
    grid = lead + (rows // tr,)
    return list(pl.pallas_call(body, name=name, grid=grid, in_specs=[spec] * 4, out_specs=[spec] * 4,
                               out_shape=[jax.ShapeDtypeStruct(shape, F32)] * 4,
                               compiler_params=_params(*(["parallel"] * len(grid))))(w, g, m, v))


def _adamw_small(ws, gs, ms, vs, name):
    n = len(ws)

    def body(*refs):
        for i in range(n):
            gv = refs[n + i][...]
            mn = ADAM_B1 * refs[2 * n + i][...] + (1.0 - ADAM_B1) * gv
            vn = ADAM_B2 * refs[3 * n + i][...] + (1.0 - ADAM_B2) * (gv * gv)
            m_hat = mn / (1.0 - ADAM_B1 ** ADAM_STEP)
            v_hat = vn / (1.0 - ADAM_B2 ** ADAM_STEP)
            refs[4 * n + i][...] = -ADAM_LR * (m_hat / (jnp.sqrt(v_hat) + ADAM_EPS) + ADAM_WD * refs[i][...])
            refs[5 * n + i][...] = mn
            refs[6 * n + i][...] = vn

    vm = pl.BlockSpec(memory_space=pltpu.VMEM)
    outs = pl.pallas_call(body, name=name, out_shape=[jax.ShapeDtypeStruct(t.shape, F32) for t in ws] * 3,
                          in_specs=[vm] * (4 * n), out_specs=[vm] * (3 * n),
                          compiler_params=pltpu.CompilerParams(vmem_limit_bytes=VMEM_LIMIT))(*ws, *gs, *ms, *vs)
    return outs[:n], outs[n:2 * n], outs[2 * n:]


SMALL = ("norm_mix", "sg_ln_g", "sg_ln_b", "sg_w", "sg_b", "cv_w", "cv_b", "cv_ln_g", "cv_ln_b", "attn_sinks", "sc_w", "norm_ffn", "norm_final")
ORDER = ("norm_mix", "w_in", "sg_ln_g", "sg_ln_b", "sg_w", "sg_b", "cv_w", "cv_b", "cv_ln_g", "cv_ln_b", "attn_sinks", "sc_w",
         "w_branch", "w_out", "norm_ffn", "w_gate_up", "w_down", "norm_final")


def kernel(x, norm_mix, w_in, sg_ln_g, sg_ln_b, sg_w, sg_b, cv_w, cv_b, cv_ln_g, cv_ln_b, attn_sinks, sc_w, w_branch, w_out, norm_ffn, w_gate_up, w_down, norm_final, loss_target, m_norm_mix, m_w_in, m_sg_ln_g, m_sg_ln_b, m_sg_w, m_sg_b, m_cv_w, m_cv_b, m_cv_ln_g, m_cv_ln_b, m_attn_sinks, m_sc_w, m_w_branch, m_w_out, m_norm_ffn, m_w_gate_up, m_w_down, m_norm_final, v_norm_mix, v_w_in, v_sg_ln_g, v_sg_ln_b, v_sg_w, v_sg_b, v_cv_w, v_cv_b, v_cv_ln_g, v_cv_ln_b, v_attn_sinks, v_sc_w, v_w_branch, v_w_out, v_norm_ffn, v_w_gate_up, v_w_down, v_norm_final):
    W = dict(norm_mix=norm_mix, w_in=w_in, sg_ln_g=sg_ln_g, sg_ln_b=sg_ln_b, sg_w=sg_w, sg_b=sg_b, cv_w=cv_w, cv_b=cv_b, cv_ln_g=cv_ln_g,
             cv_ln_b=cv_ln_b, attn_sinks=attn_sinks, sc_w=sc_w, w_branch=w_branch, w_out=w_out, norm_ffn=norm_ffn, w_gate_up=w_gate_up,
             w_down=w_down, norm_final=norm_final)
    M = dict(norm_mix=m_norm_mix, w_in=m_w_in, sg_ln_g=m_sg_ln_g, sg_ln_b=m_sg_ln_b, sg_w=m_sg_w, sg_b=m_sg_b, cv_w=m_cv_w, cv_b=m_cv_b,
             cv_ln_g=m_cv_ln_g, cv_ln_b=m_cv_ln_b, attn_sinks=m_attn_sinks, sc_w=m_sc_w, w_branch=m_w_branch, w_out=m_w_out,
             norm_ffn=m_norm_ffn, w_gate_up=m_w_gate_up, w_down=m_w_down, norm_final=m_norm_final)
    V = dict(norm_mix=v_norm_mix, w_in=v_w_in, sg_ln_g=v_sg_ln_g, sg_ln_b=v_sg_ln_b, sg_w=v_sg_w, sg_b=v_sg_b, cv_w=v_cv_w, cv_b=v_cv_b,
             cv_ln_g=v_cv_ln_g, cv_ln_b=v_cv_ln_b, attn_sinks=v_attn_sinks, sc_w=v_sc_w, w_branch=v_w_branch, w_out=v_w_out,
             norm_ffn=v_norm_ffn, w_gate_up=v_w_gate_up, w_down=v_w_down, norm_final=v_norm_final)
    mx, my, mc = lax.axis_index("x"), lax.axis_index("y"), lax.axis_index("c")
    chip = 2 * mx + my

    place = jnp.stack([chip, mc]).astype(jnp.int32)
    tables = _rope_tables(x.shape[1])
    land_shapes = [(N_CHIPS,) + HALF_SHAPE[n] for n in BIG]
    part_shapes = {n: (N_CHIPS,) + HALF_SHAPE[n][1:] for n in BIG}

    T_ = lambda t: jnp.swapaxes(t, 1, 2)
    Wt, Mt, Vt = ({**t, "w_in": T_(t["w_in"])} for t in (W, M, V))

    def shards_of(l, tok):
        return [(Wt[n][l] + tok[0, 0]).astype(BF16).reshape(HALF_SHAPE[n]) for n in BIG]

    def finish_gather(tag, handle, after):
        srcs, lands = _ici_wait(handle, after, f"ag_wait{tag}")
        return _ag_pair(srcs, lands, f"ag_pair{tag}")[0]

    def mix_weights(l, g_in):
        return dict(w_in=_w_in_layout(g_in[0].reshape(N_CHIPS, W_IN_SHARD, D_MODEL)), norm_mix=norm_mix[l][None], norm_ffn=norm_ffn[l][None],
                    mixer=_mixer_params(l, sg_ln_g, sg_ln_b, sg_w, sg_b, cvw_full, cv_b, cv_ln_g, cv_ln_b, attn_sinks, scw_full))

    def rest_weights(lw, g_rest):
        G = dict(zip(BIG[1:], g_rest))
        lw.update(w_branch=G["w_branch"].reshape(N_CHIPS, N_BRANCH, HALF, 256), w_out=G["w_out"].reshape(D_MODEL, D_MODEL),
                  w_gate_up=G["w_gate_up"].reshape(N_CHIPS, D_MODEL, GU_SHARD), w_down=G["w_down"].reshape(D_FF, D_MODEL))

    def shard_major(g):
        t = dict(g)
        if "w_in" in t:
            t["w_in"] = _w_in_unlayout(t["w_in"])
        return {n: t[n].reshape((N_CHIPS,) + HALF_SHAPE[n]) for n in BIG if n in t}

    zero_tok = jnp.zeros((8, 128), F32)
    south = (mc == 0).astype(F32)
    cvw_z = lax.dynamic_update_slice(jnp.zeros((DEPTH, CV_KERNEL, HALF), F32), cv_w * south, (0, 0, chip * 128))
    scw_z = lax.dynamic_update_slice(jnp.zeros((DEPTH, SC_KERNEL, HALF), F32), sc_w * south, (0, 0, chip * 128))
    cvw_full, scw_full, tok = _gather_small_weights(cvw_z, scw_z)
    handles = []
    for l in range(DEPTH):
        for tag, sl in (("in", slice(0, 1)), ("rest", slice(1, NB))):
            h, tok = _ici_start("gather", shards_of(l, tok)[sl], land_shapes[sl], f"ag_start{l}{tag}")
            handles.append(h)
    pending = {}

    def behind(l, key):
        def order(lw, token):
            if key == "mixer":
                lw["mixer"] = [lw["mixer"][0] + token[0, 0]] + lw["mixer"][1:]
            else:
                lw[key] = lw[key] + token[0, 0]
        return order

    def early_pair(tag, handle, order):
        def between(after, lw):
            srcs, lands = _ici_wait(handle, after, f"ag_wait{tag}")
            pending[tag], token = _d2d_start(srcs + lands, [], _forward_plan(len(srcs)), f"ag_pair_start{tag}")
            order(lw, token)
        return between

    def finish_pair(tag, after):
        arrays = _d2d_wait(pending.pop(tag), after, f"ag_pair_wait{tag}")
        return arrays[len(arrays) // 2:]

    lw0 = mix_weights(0, finish_gather("0in", handles[0], tok))
    mixed = _fwd_layer_mix(0, x[0], lw0, tables)
    rest_weights(lw0, finish_gather("0rest", handles[1], mixed[2]))
    x1, sv0 = _fwd_layer_rest(0, x[0], mixed, lw0, early_pair("1in", handles[2], behind(0, "norm_ffn")))
    lw1 = mix_weights(1, finish_pair("1in", x1))
    mixed = _fwd_layer_mix(1, x1, lw1, tables, early_pair("1rest", handles[3], behind(1, "mixer")))
    rest_weights(lw1, finish_pair("1rest", mixed[2]))
    x2, sv1 = _fwd_layer_rest(1, x1, mixed, lw1)
    dx, d_nfinal, loss = _final_loss(x2, norm_final[None], loss_target[0], 256, "final_loss")

    lw1["after"] = zero_tok
    carry, g_ffn1 = _bwd_layer_ffn(1, dx, lw1, sv1)
    g1 = shard_major(g_ffn1)
    names_f = list(g1)
    h_swap, tok = _d2d_start([g1[n] for n in names_f], [part_shapes[n] for n in names_f], _swap_plan(len(names_f)), "rs_pair_start1")
    behind(1, "mixer")(lw1, tok)
    def early_swap(tag):
        def between(d_win, lw):
            g = shard_major({"w_in": d_win})["w_in"]
            pending[tag], token = _d2d_start([g], [part_shapes["w_in"]], _swap_plan(1), f"rs_pair_start{tag}")
            behind(None, "norm_mix")(lw, token)
        return between

    dx, g_mix1 = _bwd_layer_mix(1, carry, lw1, sv1, tables, early_swap("1in"))
    swapped = _d2d_wait(h_swap, dx, "rs_pair_wait1")
    own_in, got_in = _d2d_wait(pending.pop("1in"), dx, "rs_pair_wait1in")
    names1 = ["w_in"] + names_f
    own1 = [own_in] + swapped[:len(names_f)]
    got1 = [got_in] + swapped[len(names_f):]
    part1 = [_add_pairs(own1[k], got1[k], place, f"rs_add1_{n}") for k, n in enumerate(names1)]
    hr1, tok = _ici_start("scatter", part1, [part_shapes[n] for n in names1], "rs_start1")

    def early_ffn_swap(grads, lw):
        g = shard_major(grads)
        pending["0ffn"], token = _d2d_start([g[n] for n in g], [part_shapes[n] for n in g], _swap_plan(len(g)), "rs_pair_start0ffn")
        behind(None, "norm_ffn")(lw, token)

    lw0["after"] = tok
    carry, g_ffn0 = _bwd_layer_ffn(0, dx, lw0, sv0, early_ffn_swap)
    g0 = shard_major({n: g_ffn0[n] for n in ("w_branch", "w_out")})
    names_a = list(g0) + ["w_gate_up", "w_down"]
    swapped = _d2d_wait(pending.pop("0ffn"), g_ffn0["w_branch"], "rs_pair_wait0ffn")
    own_a = [g0[n] for n in g0] + swapped[:2]
    got_a = _rs_pair([g0[n] for n in g0], "rs_pair0a") + swapped[2:]
    part_a = [_add_pairs(own_a[k], got_a[k], place, f"rs_add0a_{n}") for k, n in enumerate(names_a)]
    _, recv1 = _ici_wait(hr1, part_a[0], "rs_wait1")
    hra, tok = _ici_start("scatter", part_a, [part_shapes[n] for n in names_a], "rs_start0a")

    lw0["mixer"] = [lw0["mixer"][0] + tok[0, 0]] + lw0["mixer"][1:]
    dx, g_mix0 = _bwd_layer_mix(0, carry, lw0, sv0, tables, early_swap("0in"))
    _, recv_a = _ici_wait(hra, dx, "rs_wait0a")

    small_red, nf_red, loss_red, tok = _all_reduce_small_grads([{**g_ffn0, **g_mix0}, {**g_ffn1, **g_mix1}], d_nfinal, loss)
    small_red["norm_final"] = nf_red
    loss_out = loss_red[0, 0]
    for n in ("cv_w", "sc_w"):
        small_red[n] = lax.dynamic_slice_in_dim(small_red[n], chip * 128, 128, axis=2)

    own_in, got_in = _d2d_wait(pending.pop("0in"), tok, "rs_pair_wait0in")
    names_b, part_b = ["w_in"], [_add_pairs(own_in, got_in, place, "rs_add0b_w_in")]
    hrb, tok = _ici_start("scatter", part_b, [part_shapes[n] for n in names_b], "rs_start0b")
    bufs = {n: _sum_chips(part1[k], recv1[k], place, 1, None, f"rs_sum1_{n}", tok) for k, n in enumerate(names1) if n != "w_in"}
    for k, n in enumerate(names_a):
        bufs[n] = _sum_chips(part_a[k], recv_a[k], place, 0, bufs[n], f"rs_sum0_{n}", tok)
    h_share, tok_share = _d2d_start([bufs[n] for n in names_a], [], _share_plan(len(names_a)), "rs_share_start_a")
    bufs["w_in"] = _sum_chips(part1[0], recv1[0], place, 1, None, "rs_sum1_w_in", tok_share)
    shared = dict(zip(names_a, _d2d_wait(h_share, bufs["w_in"], "rs_share_wait_a")))
    upd = {}
    for n in names_a:
        red = shared[n].reshape(W[n].shape)
        upd[n] = _adamw(W[n], red, M[n], V[n], f"adamw_{n}")
    two_d = lambda t: t[None] if t.ndim == 1 else t
    small_upd = _adamw_small(*([two_d(t[n]) for n in SMALL] for t in (W, small_red, M, V)), "adamw_small")
    for n, d, mo, vo in zip(SMALL, *small_upd):
        upd[n] = [t.reshape(W[n].shape) for t in (small_red[n], d, mo, vo)]

    _, recv_b = _ici_wait(hrb, upd[names_a[-1]][1], "rs_wait0b")
    for k, n in enumerate(names_b):
        bufs[n] = _sum_chips(part_b[k], recv_b[k], place, 0, bufs[n], f"rs_sum0_{n}", tok)
    shared = dict(zip(names_b, _rs_share([bufs[n] for n in names_b], "rs_share_b")))
    for n in names_b:
        red = shared[n].reshape(Wt[n].shape)
        upd[n] = [T_(t) for t in _adamw(Wt[n], red, Mt[n], Vt[n], f"adamw_{n}")]

    out = [loss_out, dx[None]]
    for k in range(4):
        out += [upd[n][k] for n in ORDER]
    return tuple(out)
```

```python
import functools
import math

import jax
import jax.numpy as jnp
from jax import lax
from jax.experimental import pallas as pl
from jax.experimental.pallas import tpu as pltpu

F32 = jnp.float32
BF16 = jnp.bfloat16

D_MODEL = 1024
DEPTH = 2
HALF = 512
SG_CHUNK = 128
SG_GROUPS = 4
CV_KERNEL = 31
HEAD_DIM = 64
N_Q_HEADS = 8
N_KV_HEADS = 2
Q_PER_KV = N_Q_HEADS // N_KV_HEADS
WINDOW = 128
ROPE_THETA = 10000.0
SC_KERNEL = 3
N_BRANCH = 4
D_FF = 2816
EPS = 1e-6
N_CHIPS = 4
N_DEV = 8

MIX_W = 4352
GATE_W = N_BRANCH * D_MODEL
PROJ_PAD = 2 * MIX_W
W_IN_SHARD = 2112
GU_SHARD = 1408
HALO = 128
CV_PAD = 32

ADAM_LR = 0.001
ADAM_B1 = 0.9
ADAM_B2 = 0.999
ADAM_EPS = 1e-08
ADAM_WD = 0.01
ADAM_STEP = 10

VMEM_LIMIT = 56 * 1024 * 1024
INV_SQRT2 = 1.0 / math.sqrt(2.0)
INV_SQRT_2PI = 1.0 / math.sqrt(2.0 * math.pi)
NEG_BIG = -1e30
MESH = pl.DeviceIdType.MESH

C_ZA, C_ZB, C_Q, C_K, C_V, C_ZD = 0, 1024, 2048, 2560, 2688, 2816


def _params(*sem):
    return pltpu.CompilerParams(dimension_semantics=sem, vmem_limit_bytes=VMEM_LIMIT)


def _sig(v):
    return 1.0 / (1.0 + jnp.exp(-v))


def _dot(a, b):
    return jnp.dot(a, b, preferred_element_type=F32)


def _dot_nt(a, b):
    return lax.dot_general(a, b, (((1,), (1,)), ((), ())), preferred_element_type=F32)


def _dot_tn(a, b):
    return lax.dot_general(a, b, (((0,), (0,)), ((), ())), preferred_element_type=F32)


def _full(shape):
    nd = len(shape)
    return pl.BlockSpec(shape, lambda *_: (0,) * nd)


def _rms_mm(x, g, w, tm, tn, name):
    T = x.shape[0]
    transposed = w.ndim == 2
    if transposed:
        N = w.shape[0]
        wspec = pl.BlockSpec((tn, D_MODEL), lambda i, j: (j, 0))
    else:
        tn = w.shape[2]
        N = w.shape[0] * tn
        wspec = pl.BlockSpec((None, D_MODEL, tn), lambda i, j: (j, 0, 0))

    def body(x_ref, g_ref, w_ref, o_ref, xn_ref):
        @pl.when(pl.program_id(1) == 0)
        def _():
            xv = x_ref[...]
            r = lax.rsqrt(jnp.mean(xv * xv, axis=-1, keepdims=True) + EPS)
            xn_ref[...] = (xv * r * g_ref[...]).astype(BF16)

        o_ref[...] = (_dot_nt if transposed else _dot)(xn_ref[...], w_ref[...]).astype(BF16)

    return pl.pallas_call(
        body, name=name, grid=(T // tm, N // tn),
        in_specs=[pl.BlockSpec((tm, D_MODEL), lambda i, j: (i, 0)), _full((1, D_MODEL)), wspec],
        out_specs=[pl.BlockSpec((tm, tn), lambda i, j: (i, j)), pl.BlockSpec((tm, D_MODEL), lambda i, j: (i, 0))],
        out_shape=[jax.ShapeDtypeStruct((T, N), BF16), jax.ShapeDtypeStruct((T, D_MODEL), BF16)],
        compiler_params=_params("parallel", "arbitrary"),
    )(x, g, w)


def _merge_fwd(x, ys, proj, wb, wo, tm, name):
    T = x.shape[0]

    def body(x_ref, ys_ref, zg_ref, wb_ref, wo_ref, xo_ref, mg_ref):
        merged = None
        for n in range(N_BRANCH):
            yn = ys_ref[:, n * HALF:(n + 1) * HALF]
            br = jnp.concatenate([_dot(yn, wb_ref[s, n]) for s in range(N_CHIPS)], axis=1)
            t = _sig(zg_ref[:, n * D_MODEL:(n + 1) * D_MODEL].astype(F32)) * br
            merged = t if merged is None else merged + t
        mb = merged.astype(BF16)
        mg_ref[...] = mb
        xo_ref[...] = x_ref[...] + _dot(mb, wo_ref[...])

    return pl.pallas_call(
        body, name=name, grid=(T // tm,),
        in_specs=[pl.BlockSpec((tm, D_MODEL), lambda i: (i, 0)), pl.BlockSpec((tm, N_BRANCH * HALF), lambda i: (i, 0)),
                  pl.BlockSpec((tm, GATE_W), lambda i: (i, 0)), _full(wb.shape), _full(wo.shape)],
        out_specs=[pl.BlockSpec((tm, D_MODEL), lambda i: (i, 0)), pl.BlockSpec((tm, D_MODEL), lambda i: (i, 0))],
        out_shape=[jax.ShapeDtypeStruct((T, D_MODEL), F32), jax.ShapeDtypeStruct((T, D_MODEL), BF16)],
        compiler_params=_params("parallel"),
    )(x, ys, proj, wb, wo)


def _ffn_down(xm, gu, wd, tm, name):
    T = xm.shape[0]

    def body(x_ref, gu_ref, wd_ref, o_ref):
        g = gu_ref[:, :D_FF].astype(F32)
        u = gu_ref[:, D_FF:].astype(F32)
        act = (g * _sig(g) * u).astype(BF16)
        o_ref[...] = x_ref[...] + _dot(act, wd_ref[...])

    return pl.pallas_call(
        body, name=name, grid=(T // tm,),
        in_specs=[pl.BlockSpec((tm, D_MODEL), lambda i: (i, 0)), pl.BlockSpec((tm, 2 * D_FF), lambda i: (i, 0)), _full(wd.shape)],
        out_specs=pl.BlockSpec((tm, D_MODEL), lambda i: (i, 0)),
        out_shape=jax.ShapeDtypeStruct((T, D_MODEL), F32),
        compiler_params=_params("parallel"),
    )(xm, gu, wd)


def _final_loss(x, g, tgt, tm, name):
    T = x.shape[0]

    def body(x_ref, g_ref, t_ref, dx_ref, dg_ref, ls_ref):
        @pl.when(pl.program_id(0) == 0)
        def _():
            dg_ref[...] = jnp.zeros_like(dg_ref)
            ls_ref[...] = jnp.zeros_like(ls_ref)

        xv = x_ref[...]
        gv = g_ref[...]
        r = lax.rsqrt(jnp.mean(xv * xv, axis=-1, keepdims=True) + EPS)
        xh = xv * r
        diff = xh * gv - t_ref[...]
        ls_ref[...] += jnp.full(ls_ref.shape, 0.5 / D_MODEL, F32) * jnp.sum(diff * diff)
        dy = diff * (1.0 / D_MODEL)
        dxh = dy * gv
        dx_ref[...] = r * (dxh - xh * jnp.mean(dxh * xh, axis=-1, keepdims=True))
        dg_ref[...] += jnp.sum(dy * xh, axis=0, keepdims=True)

    return pl.pallas_call(
        body, name=name, grid=(T // tm,),
        in_specs=[pl.BlockSpec((tm, D_MODEL), lambda i: (i, 0)), _full((1, D_MODEL)), pl.BlockSpec((tm, D_MODEL), lambda i: (i, 0))],
        out_specs=[pl.BlockSpec((tm, D_MODEL), lambda i: (i, 0)), _full((1, D_MODEL)), _full((1, 128))],
        out_shape=[jax.ShapeDtypeStruct((T, D_MODEL), F32), jax.ShapeDtypeStruct((1, D_MODEL), F32), jax.ShapeDtypeStruct((1, 128), F32)],
        compiler_params=_params("arbitrary"),
    )(x, g, tgt)


def _swiglu_bwd(dx, gu, wd, tm, name, after):
    T = dx.shape[0]

    def body(dx_ref, gu_ref, wd_ref, after_ref, dgu_ref, act_ref):
        del after_ref
        dact = _dot_nt(dx_ref[...].astype(BF16), wd_ref[...])
        g = gu_ref[:, :D_FF].astype(F32)
        u = gu_ref[:, D_FF:].astype(F32)
        s = _sig(g)
        silu = g * s
        act_ref[...] = (silu * u).astype(BF16)
        dgu_ref[:, :D_FF] = (dact * u * (s + silu * (1.0 - s))).astype(BF16)
        dgu_ref[:, D_FF:] = (dact * silu).astype(BF16)

    return pl.pallas_call(
        body, name=name, grid=(T // tm,),
        in_specs=[pl.BlockSpec((tm, D_MODEL), lambda i: (i, 0)), pl.BlockSpec((tm, 2 * D_FF), lambda i: (i, 0)), _full(wd.shape),
                  pl.BlockSpec(memory_space=pl.ANY)],
        out_specs=[pl.BlockSpec((tm, 2 * D_FF), lambda i: (i, 0)), pl.BlockSpec((tm, D_FF), lambda i: (i, 0))],
        out_shape=[jax.ShapeDtypeStruct((T, 2 * D_FF), BF16), jax.ShapeDtypeStruct((T, D_FF), BF16)],
        compiler_params=_params("parallel"),
    )(dx, gu, wd, after)


def _mm_tn(a, b, grid, a_block, a_map, b_block, b_map, o_shape, o_block, o_map, name, col_split=1):
    gk = grid[2]
    tm = [d for d in a_block if d is not None][-1]
    tn = [d for d in b_block if d is not None][-1]

    def body(a_ref, b_ref, o_ref, acc_ref):
        k = pl.program_id(2)
        p = _dot_tn(a_ref[...].astype(BF16), b_ref[...].astype(BF16))

        @pl.when(k == 0)
        def _():
            acc_ref[...] = p

        @pl.when(k > 0)
        def _():
            acc_ref[...] += p

        @pl.when(k == gk - 1)
        def _():
            if col_split == 1:
                o_ref[...] = acc_ref[...].astype(o_ref.dtype)
            else:
                w = tn // col_split
                for s in range(col_split):
                    o_ref[s] = acc_ref[:, s * w:(s + 1) * w].astype(o_ref.dtype)

    return pl.pallas_call(
        body, name=name, grid=grid,
        in_specs=[pl.BlockSpec(a_block, a_map), pl.BlockSpec(b_block, b_map)],
        out_specs=pl.BlockSpec(o_block, o_map),
        out_shape=jax.ShapeDtypeStruct(o_shape, BF16),
        scratch_shapes=[pltpu.VMEM((tm, tn), F32)],
        compiler_params=_params("parallel", "parallel", "arbitrary"),
    )(a, b)


def _mm_nt_rmsbwd(a, w, x, g, dres, tm, tk, name):
    T = x.shape[0]
    transposed = w.ndim == 2
    if transposed:
        gk = w.shape[0] // tk
        wspec = pl.BlockSpec((tk, D_MODEL), lambda i, k: (k, 0))
    else:
        tk = w.shape[2]
        gk = w.shape[0]
        wspec = pl.BlockSpec((None, D_MODEL, tk), lambda i, k: (k, 0, 0))

    def body(a_ref, w_ref, x_ref, g_ref, r_ref, dx_ref, dg_ref, acc_ref):
        i, k = pl.program_id(0), pl.program_id(1)
        p = (_dot if transposed else _dot_nt)(a_ref[...], w_ref[...])

        @pl.when(k == 0)
        def _():
            acc_ref[...] = p

        @pl.when(k > 0)
        def _():
            acc_ref[...] += p

        @pl.when(jnp.logical_and(i == 0, k == 0))
        def _():
            dg_ref[...] = jnp.zeros_like(dg_ref)

        @pl.when(k == gk - 1)
        def _():
            dh = acc_ref[...]
            xv = x_ref[...]
            r = lax.rsqrt(jnp.mean(xv * xv, axis=-1, keepdims=True) + EPS)
            xh = xv * r
            dxh = dh * g_ref[...]
            dx_ref[...] = r_ref[...] + r * (dxh - xh * jnp.mean(dxh * xh, axis=-1, keepdims=True))
            dg_ref[...] += jnp.sum(dh * xh, axis=0, keepdims=True)

    return pl.pallas_call(
        body, name=name, grid=(T // tm, gk),
        in_specs=[pl.BlockSpec((tm, tk), lambda i, k: (i, k)), wspec, pl.BlockSpec((tm, D_MODEL), lambda i, k: (i, 0)),
                  _full((1, D_MODEL)), pl.BlockSpec((tm, D_MODEL), lambda i, k: (i, 0))],
        out_specs=[pl.BlockSpec((tm, D_MODEL), lambda i, k: (i, 0)), _full((1, D_MODEL))],
        out_shape=[jax.ShapeDtypeStruct((T, D_MODEL), F32), jax.ShapeDtypeStruct((1, D_MODEL), F32)],
        scratch_shapes=[pltpu.VMEM((tm, D_MODEL), F32)],
        compiler_params=_params("arbitrary", "arbitrary"),
    )(a, w, x, g, dres)


def _merge_bwd(dxm, ys, proj, wb, wo, tm, name):
    T = dxm.shape[0]

    def body(dx_ref, ys_ref, zg_ref, wb_ref, wo_ref, dys_ref, dbr_ref, dp_ref):
        dmerged = _dot_nt(dx_ref[...].astype(BF16), wo_ref[...])
        for n in range(N_BRANCH):
            yn = ys_ref[:, n * HALF:(n + 1) * HALF]
            br = jnp.concatenate([_dot(yn, wb_ref[s, n]) for s in range(N_CHIPS)], axis=1)
            gt = _sig(zg_ref[:, n * D_MODEL:(n + 1) * D_MODEL].astype(F32))
            dbr = (gt * dmerged).astype(BF16)
            dbr_ref[:, n * D_MODEL:(n + 1) * D_MODEL] = dbr
            dp_ref[:, n * D_MODEL:(n + 1) * D_MODEL] = (dmerged * br * gt * (1.0 - gt)).astype(BF16)
            dy = None
            for s in range(N_CHIPS):
                t = _dot_nt(dbr[:, s * 256:(s + 1) * 256], wb_ref[s, n])
                dy = t if dy is None else dy + t
            dys_ref[:, n * HALF:(n + 1) * HALF] = dy.astype(BF16)
        dp_ref[:, GATE_W:] = jnp.zeros((tm, MIX_W - GATE_W), BF16)

    return pl.pallas_call(
        body, name=name, grid=(T // tm,),
        in_specs=[pl.BlockSpec((tm, D_MODEL), lambda i: (i, 0)), pl.BlockSpec((tm, N_BRANCH * HALF), lambda i: (i, 0)),
                  pl.BlockSpec((tm, GATE_W), lambda i: (i, 0)), _full(wb.shape), _full(wo.shape)],
        out_specs=[pl.BlockSpec((tm, N_BRANCH * HALF), lambda i: (i, 0)), pl.BlockSpec((tm, GATE_W), lambda i: (i, 0)),
                   pl.BlockSpec((tm, MIX_W), lambda i: (i, 0))],
        out_shape=[jax.ShapeDtypeStruct((T, N_BRANCH * HALF), BF16), jax.ShapeDtypeStruct((T, GATE_W), BF16),
                   jax.ShapeDtypeStruct((T, PROJ_PAD), BF16)],
        compiler_params=_params("parallel"),
    )(dxm, ys, proj, wb, wo)


def _gelu(v):
    return 0.5 * v * (1.0 + lax.erf(v * INV_SQRT2))


def _gelu_grad(v):
    return 0.5 * (1.0 + lax.erf(v * INV_SQRT2)) + v * jnp.exp(-0.5 * v * v) * INV_SQRT_2PI


def _rot_half(t):
    w = t.shape[1]
    lane = lax.broadcasted_iota(jnp.int32, t.shape, 1)
    return jnp.where((lane % HEAD_DIM) < HEAD_DIM // 2, pltpu.roll(t, w - HEAD_DIM // 2, 1), pltpu.roll(t, HEAD_DIM // 2, 1))


def _rope(t, cos, sin_signed):
    return t * cos + _rot_half(t) * sin_signed


def _rope_t(d, cos, sin_signed):
    return d * cos + _rot_half(d * sin_signed)


def _ln_fwd(v, g, b):
    mu = jnp.mean(v, axis=-1, keepdims=True)
    vc = v - mu
    r = lax.rsqrt(jnp.mean(vc * vc, axis=-1, keepdims=True) + EPS)
    vh = vc * r
    return vh * g + b, vh, r


def _ln_bwd(dn, vh, r, g):
    dvh = dn * g
    return r * (dvh - jnp.mean(dvh, axis=-1, keepdims=True) - vh * jnp.mean(dvh * vh, axis=-1, keepdims=True))


def _sublane_shifts(sh_ref, rows):
    for b in range(1, 8):
        sh_ref[b, 0:rows - 8, :] = sh_ref[0, pl.ds(b, rows - 8), :]


def _tap(sh_ref, off, n):
    return sh_ref[off % 8, pl.ds(off - off % 8, n), :]


def _tril_mask():
    return lax.broadcasted_iota(jnp.int32, (SG_CHUNK, SG_CHUNK), 0) >= lax.broadcasted_iota(jnp.int32, (SG_CHUNK, SG_CHUNK), 1)


def _band_masks():
    shape = (Q_PER_KV * WINDOW, 2 * WINDOW)
    row = lax.broadcasted_iota(jnp.int32, shape, 0) % WINDOW
    col = lax.broadcasted_iota(jnp.int32, shape, 1)
    band = (col > row) & (col <= row + WINDOW)
    return band, band & (col >= WINDOW)


def _attn_probs(qs, kh, sink_col, valid):
    s = jnp.where(valid, _dot_nt(qs, kh) * (HEAD_DIM ** -0.5), NEG_BIG)
    m = jnp.maximum(jnp.max(s, axis=-1, keepdims=True), sink_col)
    p = jnp.exp(s - m)
    es = jnp.exp(sink_col - m)
    inv = 1.0 / (jnp.sum(p, axis=-1, keepdims=True) + es)
    return p * inv, es * inv


def _sink_col(sinks_ref, h):
    return jnp.concatenate([jnp.broadcast_to(sinks_ref[:, h * Q_PER_KV + g:h * Q_PER_KV + g + 1], (WINDOW, 1))
                            for g in range(Q_PER_KV)], axis=0)


def _mixer_in_specs(TB, nb):
    r = TB // HALO
    last = nb * r - 1
    cur = pl.BlockSpec((TB, MIX_W), lambda i: (i, 1))
    prev = pl.BlockSpec((HALO, MIX_W), lambda i: (jnp.maximum(i * r - 1, 0), 1))
    nxt = pl.BlockSpec((HALO, MIX_W), lambda i: (jnp.minimum((i + 1) * r, last), 1))
    tcur = pl.BlockSpec((TB, 128), lambda i: (i, 0))
    tprev = pl.BlockSpec((HALO, 128), lambda i: (jnp.maximum(i * r - 1, 0), 0))
    tnxt = pl.BlockSpec((HALO, 128), lambda i: (jnp.minimum((i + 1) * r, last), 0))
    return cur, prev, nxt, tcur, tprev, tnxt


def _mixer_param_specs():
    return [_full((1, HALF)), _full((1, HALF)), _full((SG_GROUPS, SG_CHUNK, SG_CHUNK)), _full((SG_CHUNK, 128)),
            _full((32, HALF)), _full((1, HALF)), _full((1, HALF)), _full((1, HALF)), _full((1, 128)), _full((8, HALF))]


def _mixers_fwd(proj, cos_t, sin_t, mp, TB, name):
    T = proj.shape[0]
    nb = T // TB
    r = TB // HALO
    cur, prev, _, tcur, tprev, _ = _mixer_in_specs(TB, nb)

    def body(zc_ref, zp_ref, cc_ref, sc_ref, cp_ref, sp_ref,
             lg_ref, lb_ref, sgw_ref, sgb_ref, cvw_ref, cvb_ref, cvg_ref, cvbb_ref, sinks_ref, scw_ref,
             ys_ref, scr_ref, k_ref, v_ref, sh_ref):
        i = pl.program_id(0)
        pm = (i > 0).astype(F32)

        def colsE(c0, c1):
            return jnp.concatenate([zp_ref[:, c0:c1].astype(F32) * pm, zc_ref[:, c0:c1].astype(F32)], axis=0)

        a = _gelu(zc_ref[:, C_ZA:C_ZA + 2 * HALF].astype(F32))
        u = a[:, :HALF]
        vn, _, _ = _ln_fwd(a[:, HALF:], lg_ref[...], lb_ref[...])
        vnb = vn.astype(BF16)
        tril = _tril_mask()
        chunks = [slice(ci * SG_CHUNK, (ci + 1) * SG_CHUNK) for ci in range(r)]
        for g in range(SG_GROUPS):
            cols = slice(g * 128, (g + 1) * 128)
            wt = jnp.where(tril, sgw_ref[g], 0.0).astype(BF16)
            mixed = _dot(wt, jnp.concatenate([vnb[rows, cols] for rows in chunks], axis=1)) + sgb_ref[:, g:g + 1]
            for ci, rows in enumerate(chunks):
                ys_ref[rows, cols] = (u[rows, cols] * mixed[:, ci * 128:(ci + 1) * 128]).astype(BF16)

        def colsB(c0, c1):
            return jnp.concatenate([zp_ref[HALO - CV_PAD:, c0:c1].astype(F32) * pm, zc_ref[:, c0:c1].astype(F32)], axis=0)

        sh_ref[0] = colsB(C_ZB, C_ZB + HALF) * _sig(colsB(C_ZB + HALF, C_ZB + 2 * HALF))
        _sublane_shifts(sh_ref, TB + CV_PAD)
        c = jnp.broadcast_to(cvb_ref[...], (TB, HALF))
        for k in range(CV_KERNEL):
            c = c + cvw_ref[k:k + 1, :] * _tap(sh_ref, CV_PAD - (CV_KERNEL - 1) + k, TB)
        n, _, _ = _ln_fwd(c, cvg_ref[...], cvbb_ref[...])
        ys_ref[:, HALF:2 * HALF] = (n * _sig(n)).astype(BF16)

        zd = colsE(C_ZD + HALF, C_ZD + 3 * HALF)
        scr_ref[...] = zd[:, :HALF] * zd[:, HALF:]
        cv = None
        for k in range(SC_KERNEL):
            t = scw_ref[k:k + 1, :] * scr_ref[pl.ds(HALO - (SC_KERNEL - 1) + k, TB), :]
            cv = t if cv is None else cv + t
        ys_ref[:, 3 * HALF:4 * HALF] = (zc_ref[:, C_ZD:C_ZD + HALF].astype(F32) * cv).astype(BF16)

        cosE = jnp.concatenate([cp_ref[...], cc_ref[...]], axis=0)
        sinE = jnp.concatenate([sp_ref[...], sc_ref[...]], axis=0)
        k_ref[...] = _rope(colsE(C_K, C_K + 128), cosE, sinE).astype(BF16)
        v_ref[...] = colsE(C_V, C_V + 128).astype(BF16)
        cosC, sinC = cc_ref[...], sc_ref[...]
        q = jnp.concatenate([_rope(zc_ref[:, C_Q + 128 * j:C_Q + 128 * (j + 1)].astype(F32), cosC, sinC)
                             for j in range(4)], axis=1).astype(BF16)
        in_band, in_band_cur = _band_masks()
        sink_cols = [_sink_col(sinks_ref, h) for h in range(N_KV_HEADS)]
        for qb in range(r):
            valid = in_band if qb else in_band_cur | (in_band & (i > 0))
            for h in range(N_KV_HEADS):
                hc = slice(h * HEAD_DIM, (h + 1) * HEAD_DIM)
                kh = k_ref[qb * WINDOW:qb * WINDOW + 2 * WINDOW, hc]
                vh = v_ref[qb * WINDOW:qb * WINDOW + 2 * WINDOW, hc]
                qs = jnp.concatenate([q[qb * WINDOW:(qb + 1) * WINDOW, (h * Q_PER_KV + g) * HEAD_DIM:(h * Q_PER_KV + g + 1) * HEAD_DIM]
                                      for g in range(Q_PER_KV)], axis=0)
                probs, _ = _attn_probs(qs, kh, sink_cols[h], valid)
                o = _dot(probs.astype(BF16), vh)
                for g in range(Q_PER_KV):
                    c0 = 2 * HALF + (h * Q_PER_KV + g) * HEAD_DIM
                    ys_ref[qb * WINDOW:(qb + 1) * WINDOW, c0:c0 + HEAD_DIM] = o[g * WINDOW:(g + 1) * WINDOW].astype(BF16)

    return pl.pallas_call(
        body, name=name, grid=(nb,),
        in_specs=[cur, prev, tcur, tcur, tprev, tprev] + _mixer_param_specs(),
        out_specs=pl.BlockSpec((TB, 4 * HALF), lambda i: (i, 0)),
        out_shape=jax.ShapeDtypeStruct((T, 4 * HALF), BF16),
        scratch_shapes=[pltpu.VMEM((TB + HALO, HALF), F32), pltpu.VMEM((TB + HALO, 128), BF16), pltpu.VMEM((TB + HALO, 128), BF16),
                        pltpu.VMEM((8, TB + CV_PAD, HALF), F32)],
        compiler_params=_params("parallel"),
    )(proj, proj, cos_t, sin_t, cos_t, sin_t, *mp)


def _mixers_bwd(proj, dys, dproj, cos_t, sin_t, mp, TB, name):
    T = proj.shape[0]
    nb = T // TB
    r = TB // HALO
    RE = TB + 2 * HALO
    RC = TB + HALO
    cur, prev, nxt, tcur, tprev, tnxt = _mixer_in_specs(TB, nb)
    dcur = pl.BlockSpec((TB, 4 * HALF), lambda i: (i, 0))
    dnxt = pl.BlockSpec((HALO, 4 * HALF), lambda i: (jnp.minimum((i + 1) * r, nb * r - 1), 0))

    def body(zc_ref, zp_ref, zn_ref, dyc_ref, dyn_ref, cc_ref, sc_ref, cp_ref, sp_ref, cn_ref, sn_ref,
             lg_ref, lb_ref, sgw_ref, sgb_ref, cvw_ref, cvb_ref, cvg_ref, cvbb_ref, sinks_ref, scw_ref, dp_in_ref,
             dz_ref, dlg_ref, dlb_ref, dsgw_ref, dsgb_ref, dcvw_ref, dcvb_ref, dcvg_ref, dcvbb_ref, dsink_ref, dscw_ref,
             scr_ref, scr2_ref, k_ref, v_ref, dk_ref, dv_ref, dq_ref, sh_ref, sh2_ref):
        del dp_in_ref
        i = pl.program_id(0)
        pm = (i > 0).astype(F32)
        nm = (i < nb - 1).astype(F32)

        @pl.when(i == 0)
        def _():
            for ref in (dlg_ref, dlb_ref, dsgw_ref, dsgb_ref, dcvw_ref, dcvb_ref, dcvg_ref, dcvbb_ref, dsink_ref, dscw_ref):
                ref[...] = jnp.zeros_like(ref)

        def colsE(c0, c1):
            return jnp.concatenate([zp_ref[:, c0:c1].astype(F32) * pm, zc_ref[:, c0:c1].astype(F32),
                                    zn_ref[:, c0:c1].astype(F32)], axis=0)

        def colsC(c0, c1):
            return jnp.concatenate([zc_ref[:, c0:c1].astype(F32), zn_ref[:, c0:c1].astype(F32)], axis=0)

        def dyC(c0, c1):
            return jnp.concatenate([dyc_ref[:, c0:c1].astype(F32), dyn_ref[:, c0:c1].astype(F32) * nm], axis=0)

        za = zc_ref[:, C_ZA:C_ZA + 2 * HALF].astype(F32)
        a = _gelu(za)
        u = a[:, :HALF]
        lg = lg_ref[...]
        vn, vh, rs = _ln_fwd(a[:, HALF:], lg, lb_ref[...])
        vnb = vn.astype(BF16)
        dya = dyc_ref[:, 0:HALF].astype(F32)
        tril = _tril_mask()
        lane128 = lax.broadcasted_iota(jnp.int32, (SG_CHUNK, 128), 1)
        chunks = [slice(ci * SG_CHUNK, (ci + 1) * SG_CHUNK) for ci in range(r)]
        side = lambda t, cols: jnp.concatenate([t[rows, cols] for rows in chunks], axis=1)
        for g in range(SG_GROUPS):
            cols = slice(g * 128, (g + 1) * 128)
            wt = jnp.where(tril, sgw_ref[g], 0.0).astype(BF16)
            vb = side(vnb, cols)
            dy_blk = side(dya, cols)
            du_g = dy_blk * (_dot(wt, vb) + sgb_ref[:, g:g + 1])
            dmix = dy_blk * side(u, cols)
            dmb = dmix.astype(BF16)
            dvn_g = _dot_tn(wt, dmb)
            dsgw_ref[g] += jnp.where(tril, _dot_nt(dmb, vb), 0.0)
            dsgb_ref[...] += jnp.where(lane128 == g, jnp.sum(dmix, axis=1, keepdims=True), 0.0)
            for ci, rows in enumerate(chunks):
                scr_ref[rows, cols] = du_g[:, ci * 128:(ci + 1) * 128]
                scr2_ref[rows, cols] = dvn_g[:, ci * 128:(ci + 1) * 128]
        du, dvn = scr_ref[0:TB, :], scr2_ref[0:TB, :]
        dlg_ref[...] += jnp.sum(dvn * vh, axis=0, keepdims=True)
        dlb_ref[...] += jnp.sum(dvn, axis=0, keepdims=True)
        dvv = _ln_bwd(dvn, vh, rs, lg)
        gg = _gelu_grad(za)
        dz_ref[:, C_ZA:C_ZA + HALF] = (du * gg[:, :HALF]).astype(BF16)
        dz_ref[:, C_ZA + HALF:C_ZA + 2 * HALF] = (dvv * gg[:, HALF:]).astype(BF16)

        RB = TB + CV_PAD

        def colsB(c0, c1):
            return jnp.concatenate([zp_ref[HALO - CV_PAD:, c0:c1].astype(F32) * pm, zc_ref[:, c0:c1].astype(F32),
                                    zn_ref[:CV_PAD, c0:c1].astype(F32)], axis=0)

        sh_ref[0] = colsB(C_ZB, C_ZB + HALF) * _sig(colsB(C_ZB + HALF, C_ZB + 2 * HALF))
        _sublane_shifts(sh_ref, RB + CV_PAD)
        c = jnp.broadcast_to(cvb_ref[...], (RB, HALF))
        for k in range(CV_KERNEL):
            c = c + cvw_ref[k:k + 1, :] * _tap(sh_ref, CV_PAD - (CV_KERNEL - 1) + k, RB)
        cvg = cvg_ref[...]
        n, ch, rc = _ln_fwd(c, cvg, cvbb_ref[...])
        sn = _sig(n)
        dyb = jnp.concatenate([dyc_ref[:, HALF:2 * HALF].astype(F32), dyn_ref[:CV_PAD, HALF:2 * HALF].astype(F32) * nm], axis=0)
        dn = dyb * (sn + n * sn * (1.0 - sn))
        dno = dn[:TB]
        dcvg_ref[...] += jnp.sum(dno * ch[:TB], axis=0, keepdims=True)
        dcvbb_ref[...] += jnp.sum(dno, axis=0, keepdims=True)
        dc = _ln_bwd(dn, ch, rc, cvg)
        sh2_ref[0] = dc
        _sublane_shifts(sh2_ref, RB)
        dcvb_ref[...] += jnp.sum(dc[:TB], axis=0, keepdims=True)
        dy0 = None
        for k in range(CV_KERNEL):
            wk = cvw_ref[k:k + 1, :]
            t = wk * _tap(sh2_ref, CV_KERNEL - 1 - k, TB)
            dy0 = t if dy0 is None else dy0 + t
            dcvw_ref[k:k + 1, :] += jnp.sum(dc[:TB] * _tap(sh_ref, CV_PAD - (CV_KERNEL - 1) + k, TB), axis=0, keepdims=True)
        ab = zc_ref[:, C_ZB:C_ZB + HALF].astype(F32)
        sg = _sig(zc_ref[:, C_ZB + HALF:C_ZB + 2 * HALF].astype(F32))
        dz_ref[:, C_ZB:C_ZB + HALF] = (dy0 * sg).astype(BF16)
        dz_ref[:, C_ZB + HALF:C_ZB + 2 * HALF] = (dy0 * ab * sg * (1.0 - sg)).astype(BF16)

        zd = colsE(C_ZD + HALF, C_ZD + 3 * HALF)
        scr_ref[...] = zd[:, :HALF] * zd[:, HALF:]
        dcv = dyC(3 * HALF, 4 * HALF) * colsC(C_ZD, C_ZD + HALF)
        scr2_ref[...] = dcv
        cv = None
        dud = None
        for k in range(SC_KERNEL):
            wk = scw_ref[k:k + 1, :]
            us = scr_ref[pl.ds(HALO - (SC_KERNEL - 1) + k, TB), :]
            t = wk * us
            cv = t if cv is None else cv + t
            t2 = wk * scr2_ref[pl.ds(SC_KERNEL - 1 - k, TB), :]
            dud = t2 if dud is None else dud + t2
            dscw_ref[k:k + 1, :] += jnp.sum(dcv[:TB] * us, axis=0, keepdims=True)
        dz_ref[:, C_ZD:C_ZD + HALF] = (dyc_ref[:, 3 * HALF:4 * HALF].astype(F32) * cv).astype(BF16)
        dz_ref[:, C_ZD + HALF:C_ZD + 2 * HALF] = (dud * zc_ref[:, C_ZD + 2 * HALF:C_ZD + 3 * HALF].astype(F32)).astype(BF16)
        dz_ref[:, C_ZD + 2 * HALF:C_ZD + 3 * HALF] = (dud * zc_ref[:, C_ZD + HALF:C_ZD + 2 * HALF].astype(F32)).astype(BF16)

        cosE = jnp.concatenate([cp_ref[...], cc_ref[...], cn_ref[...]], axis=0)
        sinE = jnp.concatenate([sp_ref[...], sc_ref[...], sn_ref[...]], axis=0)
        k_ref[...] = _rope(colsE(C_K, C_K + 128), cosE, sinE).astype(BF16)
        v_ref[...] = colsE(C_V, C_V + 128).astype(BF16)
        dk_ref[...] = jnp.zeros_like(dk_ref)
        dv_ref[...] = jnp.zeros_like(dv_ref)
        q = jnp.concatenate([_rope(colsC(C_Q + 128 * j, C_Q + 128 * (j + 1)), cosE[HALO:], sinE[HALO:])
                             for j in range(4)], axis=1).astype(BF16)
        dO = dyC(2 * HALF, 3 * HALF).astype(BF16)
        lane_s = lax.broadcasted_iota(jnp.int32, (1, 128), 1)
        in_band, in_band_cur = _band_masks()
        sink_cols = [_sink_col(sinks_ref, h) for h in range(N_KV_HEADS)]
        for qb in range(r + 1):
            valid = in_band if qb else in_band_cur | (in_band & (i > 0))
            rows = slice(qb * WINDOW, (qb + 1) * WINDOW)
            band = slice(qb * WINDOW, qb * WINDOW + 2 * WINDOW)
            for h in range(N_KV_HEADS):
                hc = slice(h * HEAD_DIM, (h + 1) * HEAD_DIM)
                kh = k_ref[band, hc]
                vh_ = v_ref[band, hc]
                heads = [slice((h * Q_PER_KV + g) * HEAD_DIM, (h * Q_PER_KV + g + 1) * HEAD_DIM) for g in range(Q_PER_KV)]
                qs = jnp.concatenate([q[rows, hs] for hs in heads], axis=0)
                dos = jnp.concatenate([dO[rows, hs] for hs in heads], axis=0)
                probs, p_sink = _attn_probs(qs, kh, sink_cols[h], valid)
                dP = _dot_nt(dos, vh_)
                rsum = jnp.sum(probs * dP, axis=-1, keepdims=True)
                dS = (probs * (dP - rsum) * (HEAD_DIM ** -0.5)).astype(BF16)
                dk_ref[band, hc] += _dot_tn(dS, qs)
                dv_ref[band, hc] += _dot_tn(probs.astype(BF16), dos)
                if qb < r:
                    dqs = _dot(dS, kh)
                    dsk = -p_sink * rsum
                    for g in range(Q_PER_KV):
                        dq_ref[rows, heads[g]] = dqs[g * WINDOW:(g + 1) * WINDOW]
                        dsink_ref[...] += jnp.where(lane_s == h * Q_PER_KV + g, jnp.sum(dsk[g * WINDOW:(g + 1) * WINDOW]), 0.0)
        cosC, sinC = cc_ref[...], sc_ref[...]
        for j in range(4):
            dz_ref[:, C_Q + 128 * j:C_Q + 128 * (j + 1)] = _rope_t(dq_ref[:, 128 * j:128 * (j + 1)], cosC, sinC).astype(BF16)
        dz_ref[:, C_K:C_K + 128] = _rope_t(dk_ref[HALO:HALO + TB, :], cosC, sinC).astype(BF16)
        dz_ref[:, C_V:C_V + 128] = dv_ref[HALO:HALO + TB, :].astype(BF16)

    small = [((1, HALF), F32), ((1, HALF), F32), ((SG_GROUPS, SG_CHUNK, SG_CHUNK), F32), ((SG_CHUNK, 128), F32),
             ((32, HALF), F32), ((1, HALF), F32), ((1, HALF), F32), ((1, HALF), F32), ((1, 128), F32), ((8, HALF), F32)]
    outs = pl.pallas_call(
        body, name=name, grid=(nb,),
        in_specs=[cur, prev, nxt, dcur, dnxt, tcur, tcur, tprev, tprev, tnxt, tnxt] + _mixer_param_specs()
                 + [pl.BlockSpec(memory_space=pl.ANY)],
        out_specs=[pl.BlockSpec((TB, MIX_W), lambda i: (i, 1))] + [_full(s) for s, _ in small],
        out_shape=[jax.ShapeDtypeStruct((T, PROJ_PAD), BF16)] + [jax.ShapeDtypeStruct(s, d) for s, d in small],
        scratch_shapes=[pltpu.VMEM((RE, HALF), F32), pltpu.VMEM((RC, HALF), F32), pltpu.VMEM((RE, 128), BF16), pltpu.VMEM((RE, 128), BF16),
                        pltpu.VMEM((RE, 128), F32), pltpu.VMEM((RE, 128), F32), pltpu.VMEM((TB, HALF), F32),
                        pltpu.VMEM((8, TB + 2 * CV_PAD, HALF), F32), pltpu.VMEM((8, TB + CV_PAD, HALF), F32)],
        input_output_aliases={21: 0},
        compiler_params=_params("arbitrary"),
    )(proj, proj, proj, dys, dys, cos_t, sin_t, cos_t, sin_t, cos_t, sin_t, *mp, dproj)
    return outs


def _rope_tables(T):
    pos = jnp.arange(T, dtype=F32)
    inv_freq = 1.0 / (ROPE_THETA ** (jnp.arange(0, HEAD_DIM, 2, dtype=F32) / HEAD_DIM))
    ang = pos[:, None] * inv_freq[None, :]
    cos, sin = jnp.cos(ang), jnp.sin(ang)
    cos_t = jnp.concatenate([cos, cos, cos, cos], axis=1)
    sin_t = jnp.concatenate([-sin, sin, -sin, sin], axis=1)
    return cos_t, sin_t


def _mixer_params(l, sg_ln_g, sg_ln_b, sg_w, sg_b, cv_w, cv_b, cv_ln_g, cv_ln_b, attn_sinks, sc_w):
    sgb_t = jnp.zeros((SG_CHUNK, 128), F32).at[:, :SG_GROUPS].set(sg_b[l].T)
    cvw = jnp.zeros((32, HALF), F32).at[:CV_KERNEL].set(cv_w[l])
    scw = jnp.zeros((8, HALF), F32).at[:SC_KERNEL].set(sc_w[l])
    sinks = jnp.zeros((1, 128), F32).at[0, :N_Q_HEADS].set(attn_sinks[l])
    return [sg_ln_g[l][None], sg_ln_b[l][None], sg_w[l], sgb_t, cvw, cv_b[l][None], cv_ln_g[l][None], cv_ln_b[l][None], sinks, scw]


def _w_in_layout(w_in_g):
    cut = MIX_W - 2 * W_IN_SHARD
    return jnp.concatenate([w_in_g[2][cut:], w_in_g[3], jnp.zeros((MIX_W - GATE_W, D_MODEL), w_in_g.dtype),
                            w_in_g[0], w_in_g[1], w_in_g[2][:cut]], axis=0)


def _w_in_unlayout(dw):
    cut = MIX_W - 2 * W_IN_SHARD
    return jnp.stack([dw[MIX_W:MIX_W + W_IN_SHARD], dw[MIX_W + W_IN_SHARD:MIX_W + 2 * W_IN_SHARD],
                      jnp.concatenate([dw[MIX_W + 2 * W_IN_SHARD:], dw[:W_IN_SHARD - cut]], axis=0),
                      dw[W_IN_SHARD - cut:GATE_W]], axis=0)


def _device_step(x, tgt, norm_mix, norm_ffn, norm_final, mixer_params, w_in_p, wb_g, wo_g, wgu_g, wd_g):
    T = x.shape[0]
    tables = _rope_tables(T)
    saved = []
    for l in range(DEPTH):
        lw = dict(w_in=w_in_p[l], w_branch=wb_g[l], w_out=wo_g[l], w_gate_up=wgu_g[l], w_down=wd_g[l],
                  norm_mix=norm_mix[l][None], norm_ffn=norm_ffn[l][None], mixer=mixer_params[l], after=jnp.zeros((8, 128), F32))
        x, sv = _fwd_layer(l, x, lw, tables)
        saved.append((lw, sv))
    dx, dnf, loss = _final_loss(x, norm_final[None], tgt, 256, "final_loss")
    grads = [None] * DEPTH
    for l in reversed(range(DEPTH)):
        lw, sv = saved[l]
        dxm, g_ffn = _bwd_layer_ffn(l, dx, lw, sv)
        dx, g_mix = _bwd_layer_mix(l, dxm, lw, sv, tables)
        raw = {**g_ffn, **g_mix}
        grads[l] = {**raw, **_small_views(raw)}
    return loss, dx, dnf[0], grads


MIX_BLOCK = 256


def _fwd_layer(l, x, lw, tables):
    return _fwd_layer_rest(l, x, _fwd_layer_mix(l, x, lw, tables), lw)


def _fwd_layer_mix(l, x, lw, tables, between=None):
    proj, xn = _rms_mm(x, lw["norm_mix"], lw["w_in"], min(x.shape[0], 1024), 2176, f"proj{l}")
    if between is not None:
        between(proj, lw)
    return proj, xn, _mixers_fwd(proj, *tables, lw["mixer"], MIX_BLOCK, f"mixers_fwd{l}")


def _fwd_layer_rest(l, x, mixed, lw, between=None):
    proj, xn, ys = mixed
    TM = min(x.shape[0], 1024)
    xm, merged = _merge_fwd(x, ys, proj, lw["w_branch"], lw["w_out"], min(x.shape[0], 512), f"merge_fwd{l}")
    if between is not None:
        between(xm, lw)
    gu, hn = _rms_mm(xm, lw["norm_ffn"], lw["w_gate_up"], TM, GU_SHARD, f"ffn_up{l}")
    x_out = _ffn_down(xm, gu, lw["w_down"], min(x.shape[0], 512), f"ffn_down{l}")
    return x_out, (x, proj, xn, ys, xm, merged, gu, hn)


def _bwd_layer_ffn(l, dx, lw, sv, between=None):
    x_in, proj, xn, ys, xm, merged, gu, hn = sv
    T = dx.shape[0]
    tkk = min(T, 1024)
    gk = T // tkk
    dgu, act = _swiglu_bwd(dx, gu, lw["w_down"], 256, f"swiglu_bwd{l}", lw["after"])
    d_wd = _mm_tn(act, dx, (2, 1, gk), (tkk, D_FF // 2), lambda i, j, k: (k, i), (tkk, D_MODEL), lambda i, j, k: (k, 0),
                  (D_FF, D_MODEL), (D_FF // 2, D_MODEL), lambda i, j, k: (i, 0), f"dw_down{l}")
    d_wgu = _mm_tn(hn, dgu, (1, N_CHIPS, gk), (tkk, D_MODEL), lambda i, j, k: (k, 0), (tkk, GU_SHARD), lambda i, j, k: (k, j),
                   (N_CHIPS, D_MODEL, GU_SHARD), (None, D_MODEL, GU_SHARD), lambda i, j, k: (j, 0, 0), f"dw_gate_up{l}")
    if between is not None:
        between(dict(w_gate_up=d_wgu, w_down=d_wd), lw)
    dxm, d_nffn = _mm_nt_rmsbwd(dgu, lw["w_gate_up"], xm, lw["norm_ffn"], dx, min(T, 1024), GU_SHARD, f"ffn_up_bwd{l}")
    dys, dbr, dproj = _merge_bwd(dxm, ys, proj, lw["w_branch"], lw["w_out"], 256, f"merge_bwd{l}")
    d_wo = _mm_tn(merged, dxm, (2, 1, gk), (tkk, 512), lambda i, j, k: (k, i), (tkk, D_MODEL), lambda i, j, k: (k, 0),
                  (D_MODEL, D_MODEL), (512, D_MODEL), lambda i, j, k: (i, 0), f"dw_out{l}")
    d_wb = _mm_tn(ys, dbr, (N_BRANCH, 1, gk), (tkk, HALF), lambda i, j, k: (k, i), (tkk, D_MODEL), lambda i, j, k: (k, i),
                  (N_CHIPS, N_BRANCH, HALF, 256), (N_CHIPS, None, HALF, 256), lambda i, j, k: (0, i, 0, 0), f"dw_branch{l}", col_split=N_CHIPS)
    return (dxm, dys, dproj), dict(w_branch=d_wb, w_out=d_wo, w_gate_up=d_wgu, w_down=d_wd, norm_ffn=d_nffn)


def _bwd_layer_mix(l, carry, lw, sv, tables, between=None):
    dxm, dys, dproj = carry
    x_in, proj, xn, ys, xm, merged, gu, hn = sv
    T = dxm.shape[0]
    tkk = min(T, 1024)
    gk = T // tkk
    mb = _mixers_bwd(proj, dys, dproj, *tables, lw["mixer"], MIX_BLOCK, f"mixers_bwd{l}")
    dproj = mb[0]
    d_win = _mm_tn(dproj, xn, (PROJ_PAD // 2176, 1, gk), (tkk, 2176), lambda i, j, k: (k, i), (tkk, D_MODEL), lambda i, j, k: (k, 0),
                   (PROJ_PAD, D_MODEL), (2176, D_MODEL), lambda i, j, k: (i, 0), f"dw_in{l}")
    if between is not None:
        between(d_win, lw)
    dx, d_nmix = _mm_nt_rmsbwd(dproj, lw["w_in"], x_in, lw["norm_mix"], dxm, min(T, 1024), 2176, f"proj_bwd{l}")
    return dx, dict(w_in=d_win, norm_mix=d_nmix, sg_ln_g=mb[1], sg_ln_b=mb[2], sg_w=mb[3], sg_b=mb[4], cv_w=mb[5], cv_b=mb[6],
                    cv_ln_g=mb[7], cv_ln_b=mb[8], attn_sinks=mb[9], sc_w=mb[10])


ANY = pl.BlockSpec(memory_space=pl.ANY)
BIG = ("w_in", "w_branch", "w_out", "w_gate_up", "w_down")
HALF_SHAPE = {"w_in": (2, W_IN_SHARD // 2, D_MODEL), "w_branch": (2, 1024, 256), "w_out": (2, 128, D_MODEL),
              "w_gate_up": (2, 512, GU_SHARD), "w_down": (2, 352, D_MODEL)}
NB = len(BIG)


def _place():
    x, y, c = lax.axis_index("x"), lax.axis_index("y"), lax.axis_index("c")
    chips = [(1 - x, y), (x, 1 - y), (1 - x, 1 - y)]
    return x, y, c, 2 * x + y, chips, [2 * px + py for px, py in chips]


def _remote(src, dst, ssem, rsem, dev):
    return pltpu.make_async_remote_copy(src_ref=src, dst_ref=dst, send_sem=ssem, recv_sem=rsem, device_id=dev, device_id_type=MESH)


HBM_SPEC = pl.BlockSpec(memory_space=pltpu.HBM)
SEM_SPEC = pl.BlockSpec(memory_space=pltpu.SEMAPHORE)
DATAFLOW = pltpu.SideEffectType.DATAFLOW_SIDE_EFFECTING


def _ici_ends(kind, src, land, j, c, chip, chip_ids):
    if kind == "gather":
        return src.at[c], land.at[chip, c], land.at[chip_ids[j], c]
    return src.at[chip_ids[j]], land.at[chip], land.at[chip_ids[j]]


def _ici_start(kind, srcs, land_shapes, name):
    n = len(srcs)

    def body(*refs):
        src, land = refs[:n], refs[n:2 * n]
        ssem, rsem, token = refs[2 * n], refs[2 * n + 1], refs[-1]
        x, y, c, chip, chips, chip_ids = _place()
        for k in range(n):
            for j in range(3):
                s, d, _ = _ici_ends(kind, src[k], land[k], j, c, chip, chip_ids)
                _remote(s, d, ssem.at[3 * k + j], rsem.at[3 * k + j], (*chips[j], c)).start()
        token[...] = jnp.zeros_like(token)

    sem = pltpu.SemaphoreType.DMA((3 * n,))
    outs = pl.pallas_call(
        body, name=name,
        out_shape=(sem, sem, *[pltpu.HBM(s.shape, s.dtype) for s in srcs], *[pltpu.HBM(sh, BF16) for sh in land_shapes],
                   jax.ShapeDtypeStruct((8, 128), F32)),
        in_specs=[HBM_SPEC] * (2 * n),
        out_specs=(SEM_SPEC, SEM_SPEC, *[HBM_SPEC] * (2 * n), pl.BlockSpec(memory_space=pltpu.VMEM)),
        input_output_aliases={i: 2 + i for i in range(2 * n)},
        compiler_params=pltpu.CompilerParams(has_side_effects=DATAFLOW),
    )(*[pltpu.with_memory_space_constraint(s, pltpu.HBM) for s in srcs],
      *[pltpu.with_memory_space_constraint(lax.empty(sh, BF16), pltpu.HBM) for sh in land_shapes])
    return (kind, outs[0], outs[1], list(outs[2:2 + n]), list(outs[2 + n:2 + 2 * n])), outs[-1]


def _ici_wait(handle, after, name):
    kind, ssem_in, rsem_in, srcs, lands = handle
    n = len(srcs)

    def body(*refs):
        src, land = refs[:n], refs[n:2 * n]
        ssem, rsem = refs[2 * n], refs[2 * n + 1]
        x, y, c, chip, chips, chip_ids = _place()
        for k in range(n):
            for j in range(3):
                s, _, mine = _ici_ends(kind, src[k], land[k], j, c, chip, chip_ids)
                cp = _remote(s, mine, ssem.at[3 * k + j], rsem.at[3 * k + j], (*chips[j], c))
                cp.wait_send()
                cp.wait_recv()

    outs = pl.pallas_call(
        body, name=name, out_shape=[pltpu.HBM(t.shape, t.dtype) for t in srcs + lands],
        in_specs=[HBM_SPEC] * (2 * n) + [SEM_SPEC, SEM_SPEC, ANY], out_specs=[HBM_SPEC] * (2 * n),
        input_output_aliases={i: i for i in range(2 * n)},
        compiler_params=pltpu.CompilerParams(has_side_effects=DATAFLOW),
    )(*srcs, *lands, ssem_in, rsem_in, after)
    return list(outs[:n]), list(outs[n:])


def _ag_pair(shards, lands, name):
    n = len(shards)

    def body(*refs):
        ins, outs = refs[:n], refs[2 * n:3 * n]
        token = refs[3 * n]
        s_fwd, r_fwd, s_own, r_own = refs[3 * n + 1:]
        x, y, c, chip, chips, chip_ids = _place()
        sib = (x, y, 1 - c)
        cps = []
        for k in range(n):
            cp = _remote(ins[k], outs[k].at[chip], s_own.at[k], r_own.at[k], sib)
            cp.start()
            cps.append(cp)
            for j in range(3):
                got = outs[k].at[chip_ids[j], c]
                cp = _remote(got, got, s_fwd.at[k, j], r_fwd.at[k, j], sib)
                cp.start()
                cps.append(cp)
        for k in range(n):
            _remote(ins[k], outs[k].at[chip], s_own.at[k], r_own.at[k], sib).wait_recv()
            for j in range(3):
                got = outs[k].at[chip_ids[j], 1 - c]
                _remote(got, got, s_fwd.at[k, j], r_fwd.at[k, j], sib).wait_recv()
        for cp in cps:
            cp.wait_send()
        token[...] = jnp.zeros_like(token)

    sem, sem1 = pltpu.SemaphoreType.DMA((n, 3)), pltpu.SemaphoreType.DMA((n,))
    outs = pl.pallas_call(
        body, name=name, out_shape=[jax.ShapeDtypeStruct(t.shape, t.dtype) for t in lands] + [jax.ShapeDtypeStruct((8, 128), F32)],
        in_specs=[ANY] * (2 * n), out_specs=[ANY] * n + [pl.BlockSpec(memory_space=pltpu.VMEM)],
        input_output_aliases={n + k: k for k in range(n)},
        scratch_shapes=[sem, sem, sem1, sem1], compiler_params=pltpu.CompilerParams(has_side_effects=True),
    )(*shards, *lands)
    return list(outs[:n]), outs[n]


def _forward_plan(n):
    def plan(refs, c, chip, chip_ids):
        out = []
        for k in range(n):
            shard, land = refs[k], refs[n + k]
            out.append((shard, land.at[chip], land.at[chip]))
            out += [(land.at[q, c], land.at[q, c], land.at[q, 1 - c]) for q in chip_ids]
        return out
    return plan, 4 * n


def _swap_plan(n):
    def plan(refs, c, chip, chip_ids):
        return [(refs[k].at[q, 1 - c], refs[n + k].at[q], refs[n + k].at[q]) for k in range(n) for q in range(N_CHIPS)]
    return plan, N_CHIPS * n


def _d2d_start(arrays, new_shapes, plan_n, name):
    plan, n_copies = plan_n
    n = len(arrays) + len(new_shapes)

    def body(*refs):
        ssem, rsem, token = refs[n], refs[n + 1], refs[-1]
        x, y, c, chip, _, chip_ids = _place()
        for i, (s, d, _) in enumerate(plan(refs[:n], c, chip, chip_ids)):
            _remote(s, d, ssem.at[i], rsem.at[i], (x, y, 1 - c)).start()
        token[...] = jnp.zeros_like(token)

    sem = pltpu.SemaphoreType.DMA((n_copies,))
    args = [pltpu.with_memory_space_constraint(t, pltpu.HBM) for t in arrays] + \
           [pltpu.with_memory_space_constraint(lax.empty(sh, BF16), pltpu.HBM) for sh in new_shapes]
    outs = pl.pallas_call(
        body, name=name,
        out_shape=(sem, sem, *[pltpu.HBM(t.shape, t.dtype) for t in args], jax.ShapeDtypeStruct((8, 128), F32)),
        in_specs=[HBM_SPEC] * n, out_specs=(SEM_SPEC, SEM_SPEC, *[HBM_SPEC] * n, pl.BlockSpec(memory_space=pltpu.VMEM)),
        input_output_aliases={i: 2 + i for i in range(n)},
        compiler_params=pltpu.CompilerParams(has_side_effects=DATAFLOW),
    )(*args)
    return (plan, outs[0], outs[1], list(outs[2:2 + n])), outs[-1]


def _d2d_wait(handle, after, name):
    plan, ssem_in, rsem_in, arrays = handle
    n = len(arrays)

    def body(*refs):
        ssem, rsem = refs[n], refs[n + 1]
        x, y, c, chip, _, chip_ids = _place()
        for i, (s, _, mine) in enumerate(plan(refs[:n], c, chip, chip_ids)):
            cp = _remote(s, mine, ssem.at[i], rsem.at[i], (x, y, 1 - c))
            cp.wait_send()
            cp.wait_recv()

    outs = pl.pallas_call(
        body, name=name, out_shape=[pltpu.HBM(t.shape, t.dtype) for t in arrays],
        in_specs=[HBM_SPEC] * n + [SEM_SPEC, SEM_SPEC, ANY], out_specs=[HBM_SPEC] * n,
        input_output_aliases={i: i for i in range(n)},
        compiler_params=pltpu.CompilerParams(has_side_effects=DATAFLOW),
    )(*arrays, ssem_in, rsem_in, after)
    return list(outs)


def _rs_pair(grads, name):
    n_arr = len(grads)

    def body(*refs):
        ins, got = refs[:n_arr], refs[n_arr:2 * n_arr]
        ssem, rsem = refs[2 * n_arr:]
        x, y, c, _, _, _ = _place()
        sib = (x, y, 1 - c)
        sends = []
        for k in reversed(range(n_arr)):
            for q in range(N_CHIPS):
                cp = _remote(ins[k].at[q, 1 - c], got[k].at[q], ssem.at[k, q], rsem.at[k, q], sib)
                cp.start()
                sends.append(cp)
        for k in range(n_arr):
            for q in range(N_CHIPS):
                _remote(got[k].at[q], got[k].at[q], ssem.at[k, q], rsem.at[k, q], sib).wait_recv()
        for cp in sends:
            cp.wait_send()

    shp = [jax.ShapeDtypeStruct((N_CHIPS,) + g.shape[2:], BF16) for g in grads]
    sem = pltpu.SemaphoreType.DMA((n_arr, N_CHIPS))
    outs = pl.pallas_call(
        body, name=name, out_shape=shp, in_specs=[ANY] * n_arr, out_specs=[ANY] * n_arr,
        scratch_shapes=[sem, sem], compiler_params=pltpu.CompilerParams(has_side_effects=True),
    )(*grads)
    return list(outs)


def _rs_share(bufs, name):
    n = len(bufs)

    def body(*refs):
        outs = refs[n:2 * n]
        ssem, rsem = refs[2 * n:]
        x, y, c, _, _, _ = _place()
        sib = (x, y, 1 - c)
        sends = []
        for k in range(n):
            for l in range(DEPTH):
                cp = _remote(outs[k].at[l, c], outs[k].at[l, c], ssem.at[k, l], rsem.at[k, l], sib)
                cp.start()
                sends.append(cp)
        for k in range(n):
            for l in range(DEPTH):
                dst = outs[k].at[l, 1 - c]
                _remote(dst, dst, ssem.at[k, l], rsem.at[k, l], sib).wait_recv()
        for cp in sends:
            cp.wait_send()

    sem = pltpu.SemaphoreType.DMA((n, DEPTH))
    outs = pl.pallas_call(
        body, name=name, out_shape=[jax.ShapeDtypeStruct(b.shape, b.dtype) for b in bufs], in_specs=[ANY] * n, out_specs=[ANY] * n,
        input_output_aliases={k: k for k in range(n)},
        scratch_shapes=[sem, sem], compiler_params=pltpu.CompilerParams(has_side_effects=True),
    )(*bufs)
    return list(outs)


def _piece(src, idx, rows, width=128, align=1, transposed=False):
    return dict(src=src, idx=idx, rows=rows, width=width, align=align, transposed=transposed)


def _all_reduce_pieces(inputs, pieces, out_shapes, writes, name):
    n_in, n_out = len(inputs), len(out_shapes)
    offs, R = [], 0
    for p in pieces:
        R = -(-R // p["align"]) * p["align"]
        offs.append(R)
        R += p["rows"]
    R = -(-R // 8) * 8

    def body(*refs):
        ins, outs, token_ref = refs[:n_in], refs[n_in:n_in + n_out], refs[n_in + n_out]
        pair_ref, chip_ref, sum_ref, ssem, rsem = refs[n_in + n_out + 1:]
        token_ref[...] = jnp.zeros_like(token_ref)
        x, y, c, chip, chips, chip_ids = _place()
        pair_ref[c] = jnp.zeros((R, 128), F32)
        for p, off in zip(pieces, offs):
            v = ins[p["src"]][...].T[p["idx"]] if p["transposed"] else ins[p["src"]][p["idx"]]
            pair_ref[c, off:off + p["rows"], 0:p["width"]] = v
        mine = _remote(pair_ref.at[c], pair_ref.at[c], ssem.at[3], rsem.at[3], (x, y, 1 - c))
        mine.start()
        _remote(pair_ref.at[1 - c], pair_ref.at[1 - c], ssem.at[3], rsem.at[3], (x, y, 1 - c)).wait_recv()
        chip_ref[chip] = pair_ref[0] + pair_ref[1]
        cps = [_remote(chip_ref.at[chip], chip_ref.at[chip], ssem.at[j], rsem.at[j], (*chips[j], c)) for j in range(3)]
        for cp in cps:
            cp.start()
        for j in range(3):
            slot = chip_ref.at[chip_ids[j]]
            _remote(slot, slot, ssem.at[j], rsem.at[j], (*chips[j], c)).wait_recv()
        acc = chip_ref[0]
        for s in range(1, N_CHIPS):
            acc = acc + chip_ref[s]
        sum_ref[...] = acc
        for o, idx, p in writes:
            outs[o][idx] = sum_ref[offs[p]:offs[p] + pieces[p]["rows"], 0:pieces[p]["width"]]
        for cp in cps + [mine]:
            cp.wait_send()

    vm = pl.BlockSpec(memory_space=pltpu.VMEM)
    outs = pl.pallas_call(
        body, name=name, out_shape=[jax.ShapeDtypeStruct(s, F32) for s in out_shapes] + [jax.ShapeDtypeStruct((8, 128), F32)],
        in_specs=[vm] * n_in, out_specs=[vm] * (n_out + 1),
        scratch_shapes=[pltpu.VMEM((2, R, 128), F32), pltpu.VMEM((N_CHIPS, R, 128), F32), pltpu.VMEM((R, 128), F32),
                        pltpu.SemaphoreType.DMA((4,)), pltpu.SemaphoreType.DMA((4,))],
        compiler_params=pltpu.CompilerParams(vmem_limit_bytes=VMEM_LIMIT),
    )(*inputs)
    return list(outs[:n_out]), outs[n_out]


def _lanes(width):
    return [slice(k, min(k + 128, width)) for k in range(0, width, 128)]


def _gather_small_weights(cvw_z, scw_z):
    pieces, writes = [], []
    for i, arr in enumerate((cvw_z, scw_z)):
        for l in range(DEPTH):
            for ln in _lanes(HALF):
                writes.append((i, (l, slice(None), ln), len(pieces)))
                pieces.append(_piece(i, (l, slice(None), ln), arr.shape[1], align=8))
    (cvw, scw), tok = _all_reduce_pieces([cvw_z, scw_z], pieces, [cvw_z.shape, scw_z.shape], writes, "ag_small")
    return cvw, scw, tok


SMALL_RAW = dict(norm_mix=(1, D_MODEL), norm_ffn=(1, D_MODEL), sg_ln_g=(1, HALF), sg_ln_b=(1, HALF), cv_b=(1, HALF), cv_ln_g=(1, HALF),
                 cv_ln_b=(1, HALF))


def _all_reduce_small_grads(raw, d_nfinal, loss):
    names = list(SMALL_RAW) + ["attn_sinks", "sg_b", "sc_w", "cv_w", "sg_w"]
    out_shape = dict(norm_mix=(DEPTH, D_MODEL), norm_ffn=(DEPTH, D_MODEL), sg_ln_g=(DEPTH, HALF), sg_ln_b=(DEPTH, HALF), cv_b=(DEPTH, HALF),
                     cv_ln_g=(DEPTH, HALF), cv_ln_b=(DEPTH, HALF), attn_sinks=(DEPTH, N_Q_HEADS), sg_b=(DEPTH, SG_GROUPS, SG_CHUNK),
                     sc_w=(DEPTH, SC_KERNEL, HALF), cv_w=(DEPTH, CV_KERNEL, HALF), sg_w=(DEPTH, SG_GROUPS, SG_CHUNK, SG_CHUNK))
    inputs, pieces, writes = [], [], []

    def add(src, idx, rows, out, out_idx, **kw):
        writes.append((names.index(out) if out in names else out, out_idx, len(pieces)))
        pieces.append(_piece(src, idx, rows, **kw))

    for l in range(DEPTH):
        row = slice(l, l + 1)
        for n, (_, width) in SMALL_RAW.items():
            inputs.append(raw[l][n])
            for ln in _lanes(width):
                add(len(inputs) - 1, (slice(0, 1), ln), 1, n, (row, ln))
        inputs.append(raw[l]["attn_sinks"])
        add(len(inputs) - 1, (slice(0, 1), slice(0, N_Q_HEADS)), 1, "attn_sinks", (row, slice(None)), width=N_Q_HEADS)
    for l in range(DEPTH):
        inputs.append(raw[l]["sg_b"])
        add(len(inputs) - 1, (slice(0, SG_GROUPS), slice(None)), SG_GROUPS, "sg_b", (l,), align=8, transposed=True)
        inputs.append(raw[l]["sc_w"])
        for ln in _lanes(HALF):
            add(len(inputs) - 1, (slice(0, SC_KERNEL), ln), SC_KERNEL, "sc_w", (l, slice(None), ln), align=8)
        inputs.append(raw[l]["cv_w"])
        for ln in _lanes(HALF):
            add(len(inputs) - 1, (slice(0, CV_KERNEL), ln), CV_KERNEL, "cv_w", (l, slice(None), ln), align=8)
        inputs.append(raw[l]["sg_w"])
        for g in range(SG_GROUPS):
            add(len(inputs) - 1, (g,), SG_CHUNK, "sg_w", (l, g), align=8)
    n_names = len(names)
    inputs.append(d_nfinal)
    for ln in _lanes(D_MODEL):
        add(len(inputs) - 1, (slice(0, 1), ln), 1, n_names, (slice(0, 1), ln))
    inputs.append(loss)
    add(len(inputs) - 1, (slice(0, 1), slice(None)), 1, n_names + 1, (slice(0, 1), slice(None)))
    outs, tok = _all_reduce_pieces(inputs, pieces, [out_shape[n] for n in names] + [(1, D_MODEL), (1, 128)], writes, "ar_small")
    return dict(zip(names, outs[:n_names])), outs[n_names], outs[n_names + 1], tok


def _small_views(raw):
    v = {n: raw[n][0] for n in SMALL_RAW}
    v.update(sg_w=raw["sg_w"], sg_b=raw["sg_b"][:, :SG_GROUPS].T, cv_w=raw["cv_w"][:CV_KERNEL],
             attn_sinks=raw["attn_sinks"][0, :N_Q_HEADS], sc_w=raw["sc_w"][:SC_KERNEL])
    return v


def _row_tile(rows, cols, n_arrays):
    budget = 28 * 1024 * 1024 // (n_arrays * 2 * cols * 4)
    tiles = [t for t in range(16, min(rows, budget) + 1, 16) if rows % t == 0]
    assert tiles, (rows, cols)
    return tiles[-1]


def _add_pairs(g, got, place, name):
    _, _, rows, cols = g.shape
    tr = _row_tile(rows, cols, 3)

    def body(place_ref, a_ref, b_ref, o_ref):
        del place_ref
        o_ref[...] = (a_ref[...].astype(F32) + b_ref[...].astype(F32)).astype(BF16)

    spec = pl.BlockSpec((None, tr, cols), lambda q, i, p: (q, i, 0))
    grid_spec = pltpu.PrefetchScalarGridSpec(
        num_scalar_prefetch=1, grid=(N_CHIPS, rows // tr),
        in_specs=[pl.BlockSpec((None, None, tr, cols), lambda q, i, p: (q, p[1], i, 0)), spec], out_specs=spec)
    return pl.pallas_call(body, name=name, grid_spec=grid_spec, out_shape=jax.ShapeDtypeStruct((N_CHIPS, rows, cols), BF16),
                          compiler_params=_params("parallel", "parallel"))(place, g, got)


def _sum_chips(own, recv, place, l, buf, name, after):
    _, rows, cols = own.shape
    tr = _row_tile(rows, cols, 4)

    def body(place_ref, own_ref, recv_ref, *rest):
        chip = place_ref[0]
        acc = own_ref[...].astype(F32)
        for j in range(1, N_CHIPS):
            acc = acc + recv_ref[lax.rem(chip + j, N_CHIPS)].astype(F32)
        rest[-1][...] = acc

    in_specs = [pl.BlockSpec((None, tr, cols), lambda i, p: (p[0], i, 0)), pl.BlockSpec((N_CHIPS, tr, cols), lambda i, p: (0, i, 0)), ANY]
    args = [place, own, recv, after]
    aliases = {}
    if buf is not None:
        in_specs.append(ANY)
        args.append(buf)
        aliases = {4: 0}
    grid_spec = pltpu.PrefetchScalarGridSpec(
        num_scalar_prefetch=1, grid=(rows // tr,), in_specs=in_specs,
        out_specs=pl.BlockSpec((None, None, tr, cols), lambda i, p: (l, p[1], i, 0)))
    return pl.pallas_call(body, name=name, grid_spec=grid_spec, out_shape=jax.ShapeDtypeStruct((DEPTH, 2, rows, cols), F32),
                          input_output_aliases=aliases, compiler_params=_params("parallel"))(*args)


def _adamw(w, g, m, v, name):
    shape = w.shape
    lead, (rows, cols) = shape[:-2], shape[-2:]
    tr = _row_tile(rows, cols, 8)

    def body(w_ref, g_ref, m_ref, v_ref, go_ref, d_ref, mo_ref, vo_ref):
        gv = g_ref[...]
        go_ref[...] = gv
        mn = ADAM_B1 * m_ref[...] + (1.0 - ADAM_B1) * gv
        vn = ADAM_B2 * v_ref[...] + (1.0 - ADAM_B2) * (gv * gv)
        m_hat = mn / (1.0 - ADAM_B1 ** ADAM_STEP)
        v_hat = vn / (1.0 - ADAM_B2 ** ADAM_STEP)
        d_ref[...] = -ADAM_LR * (m_hat / (jnp.sqrt(v_hat) + ADAM_EPS) + ADAM_WD * w_ref[...])
        mo_ref[...] = mn
        vo_ref[...] = vn

    spec = pl.BlockSpec((None,) * len(lead) + (tr, cols), lambda *idx: (*idx, 0))
    grid = lead + (rows // tr,)
    return list(pl.pallas_call(body, name=name, grid=grid, in_specs=[spec] * 4, out_specs=[spec] * 4,
                               out_shape=[jax.ShapeDtypeStruct(shape, F32)] * 4,
                               compiler_params=_params(*(["parallel"] * len(grid))))(w, g, m, v))


def _adamw_small(ws, gs, ms, vs, name):
    n = len(ws)

    def body(*refs):
        for i in range(n):
            gv = refs[n + i][...]
            mn = ADAM_B1 * refs[2 * n + i][...] + (1.0 - ADAM_B1) * gv
            vn = ADAM_B2 * refs[3 * n + i][...] + (1.0 - ADAM_B2) * (gv * gv)
            m_hat = mn / (1.0 - ADAM_B1 ** ADAM_STEP)
            v_hat = vn / (1.0 - ADAM_B2 ** ADAM_STEP)
            refs[4 * n + i][...] = -ADAM_LR * (m_hat / (jnp.sqrt(v_hat) + ADAM_EPS) + ADAM_WD * refs[i][...])
            refs[5 * n + i][...] = mn
            refs[6 * n + i][...] = vn

    vm = pl.BlockSpec(memory_space=pltpu.VMEM)
    outs = pl.pallas_call(body, name=name, out_shape=[jax.ShapeDtypeStruct(t.shape, F32) for t in ws] * 3,
                          in_specs=[vm] * (4 * n), out_specs=[vm] * (3 * n),
                          compiler_params=pltpu.CompilerParams(vmem_limit_bytes=VMEM_LIMIT))(*ws, *gs, *ms, *vs)
    return outs[:n], outs[n:2 * n], outs[2 * n:]


SMALL = ("norm_mix", "sg_ln_g", "sg_ln_b", "sg_w", "sg_b", "cv_w", "cv_b", "cv_ln_g", "cv_ln_b", "attn_sinks", "sc_w", "norm_ffn", "norm_final")
ORDER = ("norm_mix", "w_in", "sg_ln_g", "sg_ln_b", "sg_w", "sg_b", "cv_w", "cv_b", "cv_ln_g", "cv_ln_b", "attn_sinks", "sc_w",
         "w_branch", "w_out", "norm_ffn", "w_gate_up", "w_down", "norm_final")


def kernel(x, norm_mix, w_in, sg_ln_g, sg_ln_b, sg_w, sg_b, cv_w, cv_b, cv_ln_g, cv_ln_b, attn_sinks, sc_w, w_branch, w_out, norm_ffn, w_gate_up, w_down, norm_final, loss_target, m_norm_mix, m_w_in, m_sg_ln_g, m_sg_ln_b, m_sg_w, m_sg_b, m_cv_w, m_cv_b, m_cv_ln_g, m_cv_ln_b, m_attn_sinks, m_sc_w, m_w_branch, m_w_out, m_norm_ffn, m_w_gate_up, m_w_down, m_norm_final, v_norm_mix, v_w_in, v_sg_ln_g, v_sg_ln_b, v_sg_w, v_sg_b, v_cv_w, v_cv_b, v_cv_ln_g, v_cv_ln_b, v_attn_sinks, v_sc_w, v_w_branch, v_w_out, v_norm_ffn, v_w_gate_up, v_w_down, v_norm_final):
    W = dict(norm_mix=norm_mix, w_in=w_in, sg_ln_g=sg_ln_g, sg_ln_b=sg_ln_b, sg_w=sg_w, sg_b=sg_b, cv_w=cv_w, cv_b=cv_b, cv_ln_g=cv_ln_g,
             cv_ln_b=cv_ln_b, attn_sinks=attn_sinks, sc_w=sc_w, w_branch=w_branch, w_out=w_out, norm_ffn=norm_ffn, w_gate_up=w_gate_up,
             w_down=w_down, norm_final=norm_final)
    M = dict(norm_mix=m_norm_mix, w_in=m_w_in, sg_ln_g=m_sg_ln_g, sg_ln_b=m_sg_ln_b, sg_w=m_sg_w, sg_b=m_sg_b, cv_w=m_cv_w, cv_b=m_cv_b,
             cv_ln_g=m_cv_ln_g, cv_ln_b=m_cv_ln_b, attn_sinks=m_attn_sinks, sc_w=m_sc_w, w_branch=m_w_branch, w_out=m_w_out,
             norm_ffn=m_norm_ffn, w_gate_up=m_w_gate_up, w_down=m_w_down, norm_final=m_norm_final)
    V = dict(norm_mix=v_norm_mix, w_in=v_w_in, sg_ln_g=v_sg_ln_g, sg_ln_b=v_sg_ln_b, sg_w=v_sg_w, sg_b=v_sg_b, cv_w=v_cv_w, cv_b=v_cv_b,
             cv_ln_g=v_cv_ln_g, cv_ln_b=v_cv_ln_b, attn_sinks=v_attn_sinks, sc_w=v_sc_w, w_branch=v_w_branch, w_out=v_w_out,
             norm_ffn=v_norm_ffn, w_gate_up=v_w_gate_up, w_down=v_w_down, norm_final=v_norm_final)
    mx, my, mc = lax.axis_index("x"), lax.axis_index("y"), lax.axis_index("c")
    chip = 2 * mx + my

    place = jnp.stack([chip, mc]).astype(jnp.int32)
    tables = _rope_tables(x.shape[1])
    land_shapes = [(N_CHIPS,) + HALF_SHAPE[n] for n in BIG]
    part_shapes = {n: (N_CHIPS,) + HALF_SHAPE[n][1:] for n in BIG}

    T_ = lambda t: jnp.swapaxes(t, 1, 2)
    Wt, Mt, Vt = ({**t, "w_in": T_(t["w_in"])} for t in (W, M, V))

    def shards_of(l, tok):
        return [(Wt[n][l] + tok[0, 0]).astype(BF16).reshape(HALF_SHAPE[n]) for n in BIG]

    def finish_gather(tag, handle, after):
        srcs, lands = _ici_wait(handle, after, f"ag_wait{tag}")
        return _ag_pair(srcs, lands, f"ag_pair{tag}")[0]

    def mix_weights(l, g_in):
        return dict(w_in=_w_in_layout(g_in[0].reshape(N_CHIPS, W_IN_SHARD, D_MODEL)), norm_mix=norm_mix[l][None], norm_ffn=norm_ffn[l][None],
                    mixer=_mixer_params(l, sg_ln_g, sg_ln_b, sg_w, sg_b, cvw_full, cv_b, cv_ln_g, cv_ln_b, attn_sinks, scw_full))

    def rest_weights(lw, g_rest):
        G = dict(zip(BIG[1:], g_rest))
        lw.update(w_branch=G["w_branch"].reshape(N_CHIPS, N_BRANCH, HALF, 256), w_out=G["w_out"].reshape(D_MODEL, D_MODEL),
                  w_gate_up=G["w_gate_up"].reshape(N_CHIPS, D_MODEL, GU_SHARD), w_down=G["w_down"].reshape(D_FF, D_MODEL))

    def shard_major(g):
        t = dict(g)
        if "w_in" in t:
            t["w_in"] = _w_in_unlayout(t["w_in"])
        return {n: t[n].reshape((N_CHIPS,) + HALF_SHAPE[n]) for n in BIG if n in t}

    zero_tok = jnp.zeros((8, 128), F32)
    south = (mc == 0).astype(F32)
    cvw_z = lax.dynamic_update_slice(jnp.zeros((DEPTH, CV_KERNEL, HALF), F32), cv_w * south, (0, 0, chip * 128))
    scw_z = lax.dynamic_update_slice(jnp.zeros((DEPTH, SC_KERNEL, HALF), F32), sc_w * south, (0, 0, chip * 128))
    cvw_full, scw_full, tok = _gather_small_weights(cvw_z, scw_z)
    handles = []
    for l in range(DEPTH):
        for tag, sl in (("in", slice(0, 1)), ("rest", slice(1, NB))):
            h, tok = _ici_start("gather", shards_of(l, tok)[sl], land_shapes[sl], f"ag_start{l}{tag}")
            handles.append(h)
    pending = {}

    def behind(l, key):
        def order(lw, token):
            if key == "mixer":
                lw["mixer"] = [lw["mixer"][0] + token[0, 0]] + lw["mixer"][1:]
            else:
                lw[key] = lw[key] + token[0, 0]
        return order

    def early_pair(tag, handle, order):
        def between(after, lw):
            srcs, lands = _ici_wait(handle, after, f"ag_wait{tag}")
            pending[tag], token = _d2d_start(srcs + lands, [], _forward_plan(len(srcs)), f"ag_pair_start{tag}")
            order(lw, token)
        return between

    def finish_pair(tag, after):
        arrays = _d2d_wait(pending.pop(tag), after, f"ag_pair_wait{tag}")
        return arrays[len(arrays) // 2:]

    lw0 = mix_weights(0, finish_gather("0in", handles[0], tok))
    mixed = _fwd_layer_mix(0, x[0], lw0, tables)
    rest_weights(lw0, finish_gather("0rest", handles[1], mixed[2]))
    x1, sv0 = _fwd_layer_rest(0, x[0], mixed, lw0, early_pair("1in", handles[2], behind(0, "norm_ffn")))
    lw1 = mix_weights(1, finish_pair("1in", x1))
    mixed = _fwd_layer_mix(1, x1, lw1, tables, early_pair("1rest", handles[3], behind(1, "mixer")))
    rest_weights(lw1, finish_pair("1rest", mixed[2]))
    x2, sv1 = _fwd_layer_rest(1, x1, mixed, lw1)
    dx, d_nfinal, loss = _final_loss(x2, norm_final[None], loss_target[0], 256, "final_loss")

    lw1["after"] = zero_tok
    carry, g_ffn1 = _bwd_layer_ffn(1, dx, lw1, sv1)
    g1 = shard_major(g_ffn1)
    names_f = list(g1)
    h_swap, tok = _d2d_start([g1[n] for n in names_f], [part_shapes[n] for n in names_f], _swap_plan(len(names_f)), "rs_pair_start1")
    behind(1, "mixer")(lw1, tok)
    def early_swap(tag):
        def between(d_win, lw):
            g = shard_major({"w_in": d_win})["w_in"]
            pending[tag], token = _d2d_start([g], [part_shapes["w_in"]], _swap_plan(1), f"rs_pair_start{tag}")
            behind(None, "norm_mix")(lw, token)
        return between

    dx, g_mix1 = _bwd_layer_mix(1, carry, lw1, sv1, tables, early_swap("1in"))
    swapped = _d2d_wait(h_swap, dx, "rs_pair_wait1")
    own_in, got_in = _d2d_wait(pending.pop("1in"), dx, "rs_pair_wait1in")
    names1 = ["w_in"] + names_f
    own1 = [own_in] + swapped[:len(names_f)]
    got1 = [got_in] + swapped[len(names_f):]
    part1 = [_add_pairs(own1[k], got1[k], place, f"rs_add1_{n}") for k, n in enumerate(names1)]
    hr1, tok = _ici_start("scatter", part1, [part_shapes[n] for n in names1], "rs_start1")

    def early_ffn_swap(grads, lw):
        g = shard_major(grads)
        pending["0ffn"], token = _d2d_start([g[n] for n in g], [part_shapes[n] for n in g], _swap_plan(len(g)), "rs_pair_start0ffn")
        behind(None, "norm_ffn")(lw, token)

    lw0["after"] = tok
    carry, g_ffn0 = _bwd_layer_ffn(0, dx, lw0, sv0, early_ffn_swap)
    g0 = shard_major({n: g_ffn0[n] for n in ("w_branch", "w_out")})
    names_a = list(g0) + ["w_gate_up", "w_down"]
    swapped = _d2d_wait(pending.pop("0ffn"), g_ffn0["w_branch"], "rs_pair_wait0ffn")
    own_a = [g0[n] for n in g0] + swapped[:2]
    got_a = _rs_pair([g0[n] for n in g0], "rs_pair0a") + swapped[2:]
    part_a = [_add_pairs(own_a[k], got_a[k], place, f"rs_add0a_{n}") for k, n in enumerate(names_a)]
    _, recv1 = _ici_wait(hr1, part_a[0], "rs_wait1")
    hra, tok = _ici_start("scatter", part_a, [part_shapes[n] for n in names_a], "rs_start0a")

    lw0["mixer"] = [lw0["mixer"][0] + tok[0, 0]] + lw0["mixer"][1:]
    dx, g_mix0 = _bwd_layer_mix(0, carry, lw0, sv0, tables, early_swap("0in"))
    _, recv_a = _ici_wait(hra, dx, "rs_wait0a")

    small_red, nf_red, loss_red, tok = _all_reduce_small_grads([{**g_ffn0, **g_mix0}, {**g_ffn1, **g_mix1}], d_nfinal, loss)
    small_red["norm_final"] = nf_red
    loss_out = loss_red[0, 0]
    for n in ("cv_w", "sc_w"):
        small_red[n] = lax.dynamic_slice_in_dim(small_red[n], chip * 128, 128, axis=2)

    own_in, got_in = _d2d_wait(pending.pop("0in"), tok, "rs_pair_wait0in")
    names_b, part_b = ["w_in"], [_add_pairs(own_in, got_in, place, "rs_add0b_w_in")]
    hrb, tok = _ici_start("scatter", part_b, [part_shapes[n] for n in names_b], "rs_start0b")
    bufs = {n: _sum_chips(part1[k], recv1[k], place, 1, None, f"rs_sum1_{n}", tok) for k, n in enumerate(names1)}
    for k, n in enumerate(names_a):
        bufs[n] = _sum_chips(part_a[k], recv_a[k], place, 0, bufs[n], f"rs_sum0_{n}", tok)
    shared = dict(zip(names_a, _rs_share([bufs[n] for n in names_a], "rs_share_a")))
    upd = {}
    for n in names_a:
        red = shared[n].reshape(W[n].shape)
        upd[n] = _adamw(W[n], red, M[n], V[n], f"adamw_{n}")
    two_d = lambda t: t[None] if t.ndim == 1 else t
    small_upd = _adamw_small(*([two_d(t[n]) for n in SMALL] for t in (W, small_red, M, V)), "adamw_small")
    for n, d, mo, vo in zip(SMALL, *small_upd):
        upd[n] = [t.reshape(W[n].shape) for t in (small_red[n], d, mo, vo)]

    _, recv_b = _ici_wait(hrb, upd[names_a[-1]][1], "rs_wait0b")
    for k, n in enumerate(names_b):
        bufs[n] = _sum_chips(part_b[k], recv_b[k], place, 0, bufs[n], f"rs_sum0_{n}", tok)
    shared = dict(zip(names_b, _rs_share([bufs[n] for n in names_b], "rs_share_b")))
    for n in names_b:
        red = shared[n].reshape(Wt[n].shape)
        upd[n] = [T_(t) for t in _adamw(Wt[n], red, Mt[n], Vt[n], f"adamw_{n}")]

    out = [loss_out, dx[None]]
    for k in range(4):
        out += [upd[n][k] for n in ORDER]
    return tuple(out)
```

```python
import functools
import math

import jax
import jax.numpy as jnp
from jax import lax
from jax.experimental import pallas as pl
from jax.experimental.pallas import tpu as pltpu

F32 = jnp.float32
BF16 = jnp.bfloat16

D_MODEL = 1024
DEPTH = 2
HALF = 512
SG_CHUNK = 128
SG_GROUPS = 4
CV_KERNEL = 31
HEAD_DIM = 64
N_Q_HEADS = 8
N_KV_HEADS = 2
Q_PER_KV = N_Q_HEADS // N_KV_HEADS
WINDOW = 128
ROPE_THETA = 10000.0
SC_KERNEL = 3
N_BRANCH = 4
D_FF = 2816
EPS = 1e-6
N_CHIPS = 4
N_DEV = 8

MIX_W = 4352
GATE_W = N_BRANCH * D_MODEL
PROJ_PAD = 2 * MIX_W
W_IN_SHARD = 2112
GU_SHARD = 1408
HALO = 128
CV_PAD = 32

ADAM_LR = 0.001
ADAM_B1 = 0.9
ADAM_B2 = 0.999
ADAM_EPS = 1e-08
ADAM_WD = 0.01
ADAM_STEP = 10

VMEM_LIMIT = 56 * 1024 * 1024
INV_SQRT2 = 1.0 / math.sqrt(2.0)
INV_SQRT_2PI = 1.0 / math.sqrt(2.0 * math.pi)
NEG_BIG = -1e30
MESH = pl.DeviceIdType.MESH

C_ZA, C_ZB, C_Q, C_K, C_V, C_ZD = 0, 1024, 2048, 2560, 2688, 2816


def _params(*sem):
    return pltpu.CompilerParams(dimension_semantics=sem, vmem_limit_bytes=VMEM_LIMIT)


def _sig(v):
    return 1.0 / (1.0 + jnp.exp(-v))


def _dot(a, b):
    return jnp.dot(a, b, preferred_element_type=F32)


def _dot_nt(a, b):
    return lax.dot_general(a, b, (((1,), (1,)), ((), ())), preferred_element_type=F32)


def _dot_tn(a, b):
    return lax.dot_general(a, b, (((0,), (0,)), ((), ())), preferred_element_type=F32)


def _full(shape):
    nd = len(shape)
    return pl.BlockSpec(shape, lambda *_: (0,) * nd)


def _rms_mm(x, g, w, tm, tn, name):
    T = x.shape[0]
    transposed = w.ndim == 2
    if transposed:
        N = w.shape[0]
        wspec = pl.BlockSpec((tn, D_MODEL), lambda i, j: (j, 0))
    else:
        tn = w.shape[2]
        N = w.shape[0] * tn
        wspec = pl.BlockSpec((None, D_MODEL, tn), lambda i, j: (j, 0, 0))

    def body(x_ref, g_ref, w_ref, o_ref, xn_ref):
        @pl.when(pl.program_id(1) == 0)
        def _():
            xv = x_ref[...]
            r = lax.rsqrt(jnp.mean(xv * xv, axis=-1, keepdims=True) + EPS)
            xn_ref[...] = (xv * r * g_ref[...]).astype(BF16)

        o_ref[...] = (_dot_nt if transposed else _dot)(xn_ref[...], w_ref[...]).astype(BF16)

    return pl.pallas_call(
        body, name=name, grid=(T // tm, N // tn),
        in_specs=[pl.BlockSpec((tm, D_MODEL), lambda i, j: (i, 0)), _full((1, D_MODEL)), wspec],
        out_specs=[pl.BlockSpec((tm, tn), lambda i, j: (i, j)), pl.BlockSpec((tm, D_MODEL), lambda i, j: (i, 0))],
        out_shape=[jax.ShapeDtypeStruct((T, N), BF16), jax.ShapeDtypeStruct((T, D_MODEL), BF16)],
        compiler_params=_params("parallel", "arbitrary"),
    )(x, g, w)


def _merge_fwd(x, ys, proj, wb, wo, tm, name):
    T = x.shape[0]

    def body(x_ref, ys_ref, zg_ref, wb_ref, wo_ref, xo_ref, mg_ref):
        merged = None
        for n in range(N_BRANCH):
            yn = ys_ref[:, n * HALF:(n + 1) * HALF]
            br = jnp.concatenate([_dot(yn, wb_ref[s, n]) for s in range(N_CHIPS)], axis=1)
            t = _sig(zg_ref[:, n * D_MODEL:(n + 1) * D_MODEL].astype(F32)) * br
            merged = t if merged is None else merged + t
        mb = merged.astype(BF16)
        mg_ref[...] = mb
        xo_ref[...] = x_ref[...] + _dot(mb, wo_ref[...])

    return pl.pallas_call(
        body, name=name, grid=(T // tm,),
        in_specs=[pl.BlockSpec((tm, D_MODEL), lambda i: (i, 0)), pl.BlockSpec((tm, N_BRANCH * HALF), lambda i: (i, 0)),
                  pl.BlockSpec((tm, GATE_W), lambda i: (i, 0)), _full(wb.shape), _full(wo.shape)],
        out_specs=[pl.BlockSpec((tm, D_MODEL), lambda i: (i, 0)), pl.BlockSpec((tm, D_MODEL), lambda i: (i, 0))],
        out_shape=[jax.ShapeDtypeStruct((T, D_MODEL), F32), jax.ShapeDtypeStruct((T, D_MODEL), BF16)],
        compiler_params=_params("parallel"),
    )(x, ys, proj, wb, wo)


def _ffn_down(xm, gu, wd, tm, name):
    T = xm.shape[0]

    def body(x_ref, gu_ref, wd_ref, o_ref):
        g = gu_ref[:, :D_FF].astype(F32)
        u = gu_ref[:, D_FF:].astype(F32)
        act = (g * _sig(g) * u).astype(BF16)
        o_ref[...] = x_ref[...] + _dot(act, wd_ref[...])

    return pl.pallas_call(
        body, name=name, grid=(T // tm,),
        in_specs=[pl.BlockSpec((tm, D_MODEL), lambda i: (i, 0)), pl.BlockSpec((tm, 2 * D_FF), lambda i: (i, 0)), _full(wd.shape)],
        out_specs=pl.BlockSpec((tm, D_MODEL), lambda i: (i, 0)),
        out_shape=jax.ShapeDtypeStruct((T, D_MODEL), F32),
        compiler_params=_params("parallel"),
    )(xm, gu, wd)


def _final_loss(x, g, tgt, tm, name):
    T = x.shape[0]

    def body(x_ref, g_ref, t_ref, dx_ref, dg_ref, ls_ref):
        @pl.when(pl.program_id(0) == 0)
        def _():
            dg_ref[...] = jnp.zeros_like(dg_ref)
            ls_ref[...] = jnp.zeros_like(ls_ref)

        xv = x_ref[...]
        gv = g_ref[...]
        r = lax.rsqrt(jnp.mean(xv * xv, axis=-1, keepdims=True) + EPS)
        xh = xv * r
        diff = xh * gv - t_ref[...]
        ls_ref[...] += jnp.full(ls_ref.shape, 0.5 / D_MODEL, F32) * jnp.sum(diff * diff)
        dy = diff * (1.0 / D_MODEL)
        dxh = dy * gv
        dx_ref[...] = r * (dxh - xh * jnp.mean(dxh * xh, axis=-1, keepdims=True))
        dg_ref[...] += jnp.sum(dy * xh, axis=0, keepdims=True)

    return pl.pallas_call(
        body, name=name, grid=(T // tm,),
        in_specs=[pl.BlockSpec((tm, D_MODEL), lambda i: (i, 0)), _full((1, D_MODEL)), pl.BlockSpec((tm, D_MODEL), lambda i: (i, 0))],
        out_specs=[pl.BlockSpec((tm, D_MODEL), lambda i: (i, 0)), _full((1, D_MODEL)), _full((1, 128))],
        out_shape=[jax.ShapeDtypeStruct((T, D_MODEL), F32), jax.ShapeDtypeStruct((1, D_MODEL), F32), jax.ShapeDtypeStruct((1, 128), F32)],
        compiler_params=_params("arbitrary"),
    )(x, g, tgt)


def _swiglu_bwd(dx, gu, wd, tm, name, after):
    T = dx.shape[0]

    def body(dx_ref, gu_ref, wd_ref, after_ref, dgu_ref, act_ref):
        del after_ref
        dact = _dot_nt(dx_ref[...].astype(BF16), wd_ref[...])
        g = gu_ref[:, :D_FF].astype(F32)
        u = gu_ref[:, D_FF:].astype(F32)
        s = _sig(g)
        silu = g * s
        act_ref[...] = (silu * u).astype(BF16)
        dgu_ref[:, :D_FF] = (dact * u * (s + silu * (1.0 - s))).astype(BF16)
        dgu_ref[:, D_FF:] = (dact * silu).astype(BF16)

    return pl.pallas_call(
        body, name=name, grid=(T // tm,),
        in_specs=[pl.BlockSpec((tm, D_MODEL), lambda i: (i, 0)), pl.BlockSpec((tm, 2 * D_FF), lambda i: (i, 0)), _full(wd.shape),
                  pl.BlockSpec(memory_space=pl.ANY)],
        out_specs=[pl.BlockSpec((tm, 2 * D_FF), lambda i: (i, 0)), pl.BlockSpec((tm, D_FF), lambda i: (i, 0))],
        out_shape=[jax.ShapeDtypeStruct((T, 2 * D_FF), BF16), jax.ShapeDtypeStruct((T, D_FF), BF16)],
        compiler_params=_params("parallel"),
    )(dx, gu, wd, after)


def _mm_tn(a, b, grid, a_block, a_map, b_block, b_map, o_shape, o_block, o_map, name, col_split=1):
    gk = grid[2]
    tm = [d for d in a_block if d is not None][-1]
    tn = [d for d in b_block if d is not None][-1]

    def body(a_ref, b_ref, o_ref, acc_ref):
        k = pl.program_id(2)
        p = _dot_tn(a_ref[...].astype(BF16), b_ref[...].astype(BF16))

        @pl.when(k == 0)
        def _():
            acc_ref[...] = p

        @pl.when(k > 0)
        def _():
            acc_ref[...] += p

        @pl.when(k == gk - 1)
        def _():
            if col_split == 1:
                o_ref[...] = acc_ref[...].astype(o_ref.dtype)
            else:
                w = tn // col_split
                for s in range(col_split):
                    o_ref[s] = acc_ref[:, s * w:(s + 1) * w].astype(o_ref.dtype)

    return pl.pallas_call(
        body, name=name, grid=grid,
        in_specs=[pl.BlockSpec(a_block, a_map), pl.BlockSpec(b_block, b_map)],
        out_specs=pl.BlockSpec(o_block, o_map),
        out_shape=jax.ShapeDtypeStruct(o_shape, BF16),
        scratch_shapes=[pltpu.VMEM((tm, tn), F32)],
        compiler_params=_params("parallel", "parallel", "arbitrary"),
    )(a, b)


def _mm_nt_rmsbwd(a, w, x, g, dres, tm, tk, name):
    T = x.shape[0]
    transposed = w.ndim == 2
    if transposed:
        gk = w.shape[0] // tk
        wspec = pl.BlockSpec((tk, D_MODEL), lambda i, k: (k, 0))
    else:
        tk = w.shape[2]
        gk = w.shape[0]
        wspec = pl.BlockSpec((None, D_MODEL, tk), lambda i, k: (k, 0, 0))

    def body(a_ref, w_ref, x_ref, g_ref, r_ref, dx_ref, dg_ref, acc_ref):
        i, k = pl.program_id(0), pl.program_id(1)
        p = (_dot if transposed else _dot_nt)(a_ref[...], w_ref[...])

        @pl.when(k == 0)
        def _():
            acc_ref[...] = p

        @pl.when(k > 0)
        def _():
            acc_ref[...] += p

        @pl.when(jnp.logical_and(i == 0, k == 0))
        def _():
            dg_ref[...] = jnp.zeros_like(dg_ref)

        @pl.when(k == gk - 1)
        def _():
            dh = acc_ref[...]
            xv = x_ref[...]
            r = lax.rsqrt(jnp.mean(xv * xv, axis=-1, keepdims=True) + EPS)
            xh = xv * r
            dxh = dh * g_ref[...]
            dx_ref[...] = r_ref[...] + r * (dxh - xh * jnp.mean(dxh * xh, axis=-1, keepdims=True))
            dg_ref[...] += jnp.sum(dh * xh, axis=0, keepdims=True)

    return pl.pallas_call(
        body, name=name, grid=(T // tm, gk),
        in_specs=[pl.BlockSpec((tm, tk), lambda i, k: (i, k)), wspec, pl.BlockSpec((tm, D_MODEL), lambda i, k: (i, 0)),
                  _full((1, D_MODEL)), pl.BlockSpec((tm, D_MODEL), lambda i, k: (i, 0))],
        out_specs=[pl.BlockSpec((tm, D_MODEL), lambda i, k: (i, 0)), _full((1, D_MODEL))],
        out_shape=[jax.ShapeDtypeStruct((T, D_MODEL), F32), jax.ShapeDtypeStruct((1, D_MODEL), F32)],
        scratch_shapes=[pltpu.VMEM((tm, D_MODEL), F32)],
        compiler_params=_params("arbitrary", "arbitrary"),
    )(a, w, x, g, dres)


def _merge_bwd(dxm, ys, proj, wb, wo, tm, name):
    T = dxm.shape[0]

    def body(dx_ref, ys_ref, zg_ref, wb_ref, wo_ref, dys_ref, dbr_ref, dp_ref):
        dmerged = _dot_nt(dx_ref[...].astype(BF16), wo_ref[...])
        for n in range(N_BRANCH):
            yn = ys_ref[:, n * HALF:(n + 1) * HALF]
            br = jnp.concatenate([_dot(yn, wb_ref[s, n]) for s in range(N_CHIPS)], axis=1)
            gt = _sig(zg_ref[:, n * D_MODEL:(n + 1) * D_MODEL].astype(F32))
            dbr = (gt * dmerged).astype(BF16)
            dbr_ref[:, n * D_MODEL:(n + 1) * D_MODEL] = dbr
            dp_ref[:, n * D_MODEL:(n + 1) * D_MODEL] = (dmerged * br * gt * (1.0 - gt)).astype(BF16)
            dy = None
            for s in range(N_CHIPS):
                t = _dot_nt(dbr[:, s * 256:(s + 1) * 256], wb_ref[s, n])
                dy = t if dy is None else dy + t
            dys_ref[:, n * HALF:(n + 1) * HALF] = dy.astype(BF16)
        dp_ref[:, GATE_W:] = jnp.zeros((tm, MIX_W - GATE_W), BF16)

    return pl.pallas_call(
        body, name=name, grid=(T // tm,),
        in_specs=[pl.BlockSpec((tm, D_MODEL), lambda i: (i, 0)), pl.BlockSpec((tm, N_BRANCH * HALF), lambda i: (i, 0)),
                  pl.BlockSpec((tm, GATE_W), lambda i: (i, 0)), _full(wb.shape), _full(wo.shape)],
        out_specs=[pl.BlockSpec((tm, N_BRANCH * HALF), lambda i: (i, 0)), pl.BlockSpec((tm, GATE_W), lambda i: (i, 0)),
                   pl.BlockSpec((tm, MIX_W), lambda i: (i, 0))],
        out_shape=[jax.ShapeDtypeStruct((T, N_BRANCH * HALF), BF16), jax.ShapeDtypeStruct((T, GATE_W), BF16),
                   jax.ShapeDtypeStruct((T, PROJ_PAD), BF16)],
        compiler_params=_params("parallel"),
    )(dxm, ys, proj, wb, wo)


def _gelu(v):
    return 0.5 * v * (1.0 + lax.erf(v * INV_SQRT2))


def _gelu_grad(v):
    return 0.5 * (1.0 + lax.erf(v * INV_SQRT2)) + v * jnp.exp(-0.5 * v * v) * INV_SQRT_2PI


def _rot_half(t):
    w = t.shape[1]
    lane = lax.broadcasted_iota(jnp.int32, t.shape, 1)
    return jnp.where((lane % HEAD_DIM) < HEAD_DIM // 2, pltpu.roll(t, w - HEAD_DIM // 2, 1), pltpu.roll(t, HEAD_DIM // 2, 1))


def _rope(t, cos, sin_signed):
    return t * cos + _rot_half(t) * sin_signed


def _rope_t(d, cos, sin_signed):
    return d * cos + _rot_half(d * sin_signed)


def _ln_fwd(v, g, b):
    mu = jnp.mean(v, axis=-1, keepdims=True)
    vc = v - mu
    r = lax.rsqrt(jnp.mean(vc * vc, axis=-1, keepdims=True) + EPS)
    vh = vc * r
    return vh * g + b, vh, r


def _ln_bwd(dn, vh, r, g):
    dvh = dn * g
    return r * (dvh - jnp.mean(dvh, axis=-1, keepdims=True) - vh * jnp.mean(dvh * vh, axis=-1, keepdims=True))


def _sublane_shifts(sh_ref, rows):
    for b in range(1, 8):
        sh_ref[b, 0:rows - 8, :] = sh_ref[0, pl.ds(b, rows - 8), :]


def _tap(sh_ref, off, n):
    return sh_ref[off % 8, pl.ds(off - off % 8, n), :]


def _tril_mask():
    return lax.broadcasted_iota(jnp.int32, (SG_CHUNK, SG_CHUNK), 0) >= lax.broadcasted_iota(jnp.int32, (SG_CHUNK, SG_CHUNK), 1)


def _band_masks():
    shape = (Q_PER_KV * WINDOW, 2 * WINDOW)
    row = lax.broadcasted_iota(jnp.int32, shape, 0) % WINDOW
    col = lax.broadcasted_iota(jnp.int32, shape, 1)
    band = (col > row) & (col <= row + WINDOW)
    return band, band & (col >= WINDOW)


def _attn_probs(qs, kh, sink_col, valid):
    s = jnp.where(valid, _dot_nt(qs, kh) * (HEAD_DIM ** -0.5), NEG_BIG)
    m = jnp.maximum(jnp.max(s, axis=-1, keepdims=True), sink_col)
    p = jnp.exp(s - m)
    es = jnp.exp(sink_col - m)
    inv = 1.0 / (jnp.sum(p, axis=-1, keepdims=True) + es)
    return p * inv, es * inv


def _sink_col(sinks_ref, h):
    return jnp.concatenate([jnp.broadcast_to(sinks_ref[:, h * Q_PER_KV + g:h * Q_PER_KV + g + 1], (WINDOW, 1))
                            for g in range(Q_PER_KV)], axis=0)


def _mixer_in_specs(TB, nb):
    r = TB // HALO
    last = nb * r - 1
    cur = pl.BlockSpec((TB, MIX_W), lambda i: (i, 1))
    prev = pl.BlockSpec((HALO, MIX_W), lambda i: (jnp.maximum(i * r - 1, 0), 1))
    nxt = pl.BlockSpec((HALO, MIX_W), lambda i: (jnp.minimum((i + 1) * r, last), 1))
    tcur = pl.BlockSpec((TB, 128), lambda i: (i, 0))
    tprev = pl.BlockSpec((HALO, 128), lambda i: (jnp.maximum(i * r - 1, 0), 0))
    tnxt = pl.BlockSpec((HALO, 128), lambda i: (jnp.minimum((i + 1) * r, last), 0))
    return cur, prev, nxt, tcur, tprev, tnxt


def _mixer_param_specs():
    return [_full((1, HALF)), _full((1, HALF)), _full((SG_GROUPS, SG_CHUNK, SG_CHUNK)), _full((SG_CHUNK, 128)),
            _full((32, HALF)), _full((1, HALF)), _full((1, HALF)), _full((1, HALF)), _full((1, 128)), _full((8, HALF))]


def _mixers_fwd(proj, cos_t, sin_t, mp, TB, name):
    T = proj.shape[0]
    nb = T // TB
    r = TB // HALO
    cur, prev, _, tcur, tprev, _ = _mixer_in_specs(TB, nb)

    def body(zc_ref, zp_ref, cc_ref, sc_ref, cp_ref, sp_ref,
             lg_ref, lb_ref, sgw_ref, sgb_ref, cvw_ref, cvb_ref, cvg_ref, cvbb_ref, sinks_ref, scw_ref,
             ys_ref, scr_ref, k_ref, v_ref, sh_ref):
        i = pl.program_id(0)
        pm = (i > 0).astype(F32)

        def colsE(c0, c1):
            return jnp.concatenate([zp_ref[:, c0:c1].astype(F32) * pm, zc_ref[:, c0:c1].astype(F32)], axis=0)

        a = _gelu(zc_ref[:, C_ZA:C_ZA + 2 * HALF].astype(F32))
        u = a[:, :HALF]
        vn, _, _ = _ln_fwd(a[:, HALF:], lg_ref[...], lb_ref[...])
        vnb = vn.astype(BF16)
        tril = _tril_mask()
        chunks = [slice(ci * SG_CHUNK, (ci + 1) * SG_CHUNK) for ci in range(r)]
        for g in range(SG_GROUPS):
            cols = slice(g * 128, (g + 1) * 128)
            wt = jnp.where(tril, sgw_ref[g], 0.0).astype(BF16)
            mixed = _dot(wt, jnp.concatenate([vnb[rows, cols] for rows in chunks], axis=1)) + sgb_ref[:, g:g + 1]
            for ci, rows in enumerate(chunks):
                ys_ref[rows, cols] = (u[rows, cols] * mixed[:, ci * 128:(ci + 1) * 128]).astype(BF16)

        def colsB(c0, c1):
            return jnp.concatenate([zp_ref[HALO - CV_PAD:, c0:c1].astype(F32) * pm, zc_ref[:, c0:c1].astype(F32)], axis=0)

        sh_ref[0] = colsB(C_ZB, C_ZB + HALF) * _sig(colsB(C_ZB + HALF, C_ZB + 2 * HALF))
        _sublane_shifts(sh_ref, TB + CV_PAD)
        c = jnp.broadcast_to(cvb_ref[...], (TB, HALF))
        for k in range(CV_KERNEL):
            c = c + cvw_ref[k:k + 1, :] * _tap(sh_ref, CV_PAD - (CV_KERNEL - 1) + k, TB)
        n, _, _ = _ln_fwd(c, cvg_ref[...], cvbb_ref[...])
        ys_ref[:, HALF:2 * HALF] = (n * _sig(n)).astype(BF16)

        zd = colsE(C_ZD + HALF, C_ZD + 3 * HALF)
        scr_ref[...] = zd[:, :HALF] * zd[:, HALF:]
        cv = None
        for k in range(SC_KERNEL):
            t = scw_ref[k:k + 1, :] * scr_ref[pl.ds(HALO - (SC_KERNEL - 1) + k, TB), :]
            cv = t if cv is None else cv + t
        ys_ref[:, 3 * HALF:4 * HALF] = (zc_ref[:, C_ZD:C_ZD + HALF].astype(F32) * cv).astype(BF16)

        cosE = jnp.concatenate([cp_ref[...], cc_ref[...]], axis=0)
        sinE = jnp.concatenate([sp_ref[...], sc_ref[...]], axis=0)
        k_ref[...] = _rope(colsE(C_K, C_K + 128), cosE, sinE).astype(BF16)
        v_ref[...] = colsE(C_V, C_V + 128).astype(BF16)
        cosC, sinC = cc_ref[...], sc_ref[...]
        q = jnp.concatenate([_rope(zc_ref[:, C_Q + 128 * j:C_Q + 128 * (j + 1)].astype(F32), cosC, sinC)
                             for j in range(4)], axis=1).astype(BF16)
        in_band, in_band_cur = _band_masks()
        sink_cols = [_sink_col(sinks_ref, h) for h in range(N_KV_HEADS)]
        for qb in range(r):
            valid = in_band if qb else in_band_cur | (in_band & (i > 0))
            for h in range(N_KV_HEADS):
                hc = slice(h * HEAD_DIM, (h + 1) * HEAD_DIM)
                kh = k_ref[qb * WINDOW:qb * WINDOW + 2 * WINDOW, hc]
                vh = v_ref[qb * WINDOW:qb * WINDOW + 2 * WINDOW, hc]
                qs = jnp.concatenate([q[qb * WINDOW:(qb + 1) * WINDOW, (h * Q_PER_KV + g) * HEAD_DIM:(h * Q_PER_KV + g + 1) * HEAD_DIM]
                                      for g in range(Q_PER_KV)], axis=0)
                probs, _ = _attn_probs(qs, kh, sink_cols[h], valid)
                o = _dot(probs.astype(BF16), vh)
                for g in range(Q_PER_KV):
                    c0 = 2 * HALF + (h * Q_PER_KV + g) * HEAD_DIM
                    ys_ref[qb * WINDOW:(qb + 1) * WINDOW, c0:c0 + HEAD_DIM] = o[g * WINDOW:(g + 1) * WINDOW].astype(BF16)

    return pl.pallas_call(
        body, name=name, grid=(nb,),
        in_specs=[cur, prev, tcur, tcur, tprev, tprev] + _mixer_param_specs(),
        out_specs=pl.BlockSpec((TB, 4 * HALF), lambda i: (i, 0)),
        out_shape=jax.ShapeDtypeStruct((T, 4 * HALF), BF16),
        scratch_shapes=[pltpu.VMEM((TB + HALO, HALF), F32), pltpu.VMEM((TB + HALO, 128), BF16), pltpu.VMEM((TB + HALO, 128), BF16),
                        pltpu.VMEM((8, TB + CV_PAD, HALF), F32)],
        compiler_params=_params("parallel"),
    )(proj, proj, cos_t, sin_t, cos_t, sin_t, *mp)


def _mixers_bwd(proj, dys, dproj, cos_t, sin_t, mp, TB, name):
    T = proj.shape[0]
    nb = T // TB
    r = TB // HALO
    RE = TB + 2 * HALO
    RC = TB + HALO
    cur, prev, nxt, tcur, tprev, tnxt = _mixer_in_specs(TB, nb)
    dcur = pl.BlockSpec((TB, 4 * HALF), lambda i: (i, 0))
    dnxt = pl.BlockSpec((HALO, 4 * HALF), lambda i: (jnp.minimum((i + 1) * r, nb * r - 1), 0))

    def body(zc_ref, zp_ref, zn_ref, dyc_ref, dyn_ref, cc_ref, sc_ref, cp_ref, sp_ref, cn_ref, sn_ref,
             lg_ref, lb_ref, sgw_ref, sgb_ref, cvw_ref, cvb_ref, cvg_ref, cvbb_ref, sinks_ref, scw_ref, dp_in_ref,
             dz_ref, dlg_ref, dlb_ref, dsgw_ref, dsgb_ref, dcvw_ref, dcvb_ref, dcvg_ref, dcvbb_ref, dsink_ref, dscw_ref,
             scr_ref, scr2_ref, k_ref, v_ref, dk_ref, dv_ref, dq_ref, sh_ref, sh2_ref):
        del dp_in_ref
        i = pl.program_id(0)
        pm = (i > 0).astype(F32)
        nm = (i < nb - 1).astype(F32)

        @pl.when(i == 0)
        def _():
            for ref in (dlg_ref, dlb_ref, dsgw_ref, dsgb_ref, dcvw_ref, dcvb_ref, dcvg_ref, dcvbb_ref, dsink_ref, dscw_ref):
                ref[...] = jnp.zeros_like(ref)

        def colsE(c0, c1):
            return jnp.concatenate([zp_ref[:, c0:c1].astype(F32) * pm, zc_ref[:, c0:c1].astype(F32),
                                    zn_ref[:, c0:c1].astype(F32)], axis=0)

        def colsC(c0, c1):
            return jnp.concatenate([zc_ref[:, c0:c1].astype(F32), zn_ref[:, c0:c1].astype(F32)], axis=0)

        def dyC(c0, c1):
            return jnp.concatenate([dyc_ref[:, c0:c1].astype(F32), dyn_ref[:, c0:c1].astype(F32) * nm], axis=0)

        za = zc_ref[:, C_ZA:C_ZA + 2 * HALF].astype(F32)
        a = _gelu(za)
        u = a[:, :HALF]
        lg = lg_ref[...]
        vn, vh, rs = _ln_fwd(a[:, HALF:], lg, lb_ref[...])
        vnb = vn.astype(BF16)
        dya = dyc_ref[:, 0:HALF].astype(F32)
        tril = _tril_mask()
        lane128 = lax.broadcasted_iota(jnp.int32, (SG_CHUNK, 128), 1)
        chunks = [slice(ci * SG_CHUNK, (ci + 1) * SG_CHUNK) for ci in range(r)]
        side = lambda t, cols: jnp.concatenate([t[rows, cols] for rows in chunks], axis=1)
        for g in range(SG_GROUPS):
            cols = slice(g * 128, (g + 1) * 128)
            wt = jnp.where(tril, sgw_ref[g], 0.0).astype(BF16)
            vb = side(vnb, cols)
            dy_blk = side(dya, cols)
            du_g = dy_blk * (_dot(wt, vb) + sgb_ref[:, g:g + 1])
            dmix = dy_blk * side(u, cols)
            dmb = dmix.astype(BF16)
            dvn_g = _dot_tn(wt, dmb)
            dsgw_ref[g] += jnp.where(tril, _dot_nt(dmb, vb), 0.0)
            dsgb_ref[...] += jnp.where(lane128 == g, jnp.sum(dmix, axis=1, keepdims=True), 0.0)
            for ci, rows in enumerate(chunks):
                scr_ref[rows, cols] = du_g[:, ci * 128:(ci + 1) * 128]
                scr2_ref[rows, cols] = dvn_g[:, ci * 128:(ci + 1) * 128]
        du, dvn = scr_ref[0:TB, :], scr2_ref[0:TB, :]
        dlg_ref[...] += jnp.sum(dvn * vh, axis=0, keepdims=True)
        dlb_ref[...] += jnp.sum(dvn, axis=0, keepdims=True)
        dvv = _ln_bwd(dvn, vh, rs, lg)
        gg = _gelu_grad(za)
        dz_ref[:, C_ZA:C_ZA + HALF] = (du * gg[:, :HALF]).astype(BF16)
        dz_ref[:, C_ZA + HALF:C_ZA + 2 * HALF] = (dvv * gg[:, HALF:]).astype(BF16)

        RB = TB + CV_PAD

        def colsB(c0, c1):
            return jnp.concatenate([zp_ref[HALO - CV_PAD:, c0:c1].astype(F32) * pm, zc_ref[:, c0:c1].astype(F32),
                                    zn_ref[:CV_PAD, c0:c1].astype(F32)], axis=0)

        sh_ref[0] = colsB(C_ZB, C_ZB + HALF) * _sig(colsB(C_ZB + HALF, C_ZB + 2 * HALF))
        _sublane_shifts(sh_ref, RB + CV_PAD)
        c = jnp.broadcast_to(cvb_ref[...], (RB, HALF))
        for k in range(CV_KERNEL):
            c = c + cvw_ref[k:k + 1, :] * _tap(sh_ref, CV_PAD - (CV_KERNEL - 1) + k, RB)
        cvg = cvg_ref[...]
        n, ch, rc = _ln_fwd(c, cvg, cvbb_ref[...])
        sn = _sig(n)
        dyb = jnp.concatenate([dyc_ref[:, HALF:2 * HALF].astype(F32), dyn_ref[:CV_PAD, HALF:2 * HALF].astype(F32) * nm], axis=0)
        dn = dyb * (sn + n * sn * (1.0 - sn))
        dno = dn[:TB]
        dcvg_ref[...] += jnp.sum(dno * ch[:TB], axis=0, keepdims=True)
        dcvbb_ref[...] += jnp.sum(dno, axis=0, keepdims=True)
        dc = _ln_bwd(dn, ch, rc, cvg)
        sh2_ref[0] = dc
        _sublane_shifts(sh2_ref, RB)
        dcvb_ref[...] += jnp.sum(dc[:TB], axis=0, keepdims=True)
        dy0 = None
        for k in range(CV_KERNEL):
            wk = cvw_ref[k:k + 1, :]
            t = wk * _tap(sh2_ref, CV_KERNEL - 1 - k, TB)
            dy0 = t if dy0 is None else dy0 + t
            dcvw_ref[k:k + 1, :] += jnp.sum(dc[:TB] * _tap(sh_ref, CV_PAD - (CV_KERNEL - 1) + k, TB), axis=0, keepdims=True)
        ab = zc_ref[:, C_ZB:C_ZB + HALF].astype(F32)
        sg = _sig(zc_ref[:, C_ZB + HALF:C_ZB + 2 * HALF].astype(F32))
        dz_ref[:, C_ZB:C_ZB + HALF] = (dy0 * sg).astype(BF16)
        dz_ref[:, C_ZB + HALF:C_ZB + 2 * HALF] = (dy0 * ab * sg * (1.0 - sg)).astype(BF16)

        zd = colsE(C_ZD + HALF, C_ZD + 3 * HALF)
        scr_ref[...] = zd[:, :HALF] * zd[:, HALF:]
        dcv = dyC(3 * HALF, 4 * HALF) * colsC(C_ZD, C_ZD + HALF)
        scr2_ref[...] = dcv
        cv = None
        dud = None
        for k in range(SC_KERNEL):
            wk = scw_ref[k:k + 1, :]
            us = scr_ref[pl.ds(HALO - (SC_KERNEL - 1) + k, TB), :]
            t = wk * us
            cv = t if cv is None else cv + t
            t2 = wk * scr2_ref[pl.ds(SC_KERNEL - 1 - k, TB), :]
            dud = t2 if dud is None else dud + t2
            dscw_ref[k:k + 1, :] += jnp.sum(dcv[:TB] * us, axis=0, keepdims=True)
        dz_ref[:, C_ZD:C_ZD + HALF] = (dyc_ref[:, 3 * HALF:4 * HALF].astype(F32) * cv).astype(BF16)
        dz_ref[:, C_ZD + HALF:C_ZD + 2 * HALF] = (dud * zc_ref[:, C_ZD + 2 * HALF:C_ZD + 3 * HALF].astype(F32)).astype(BF16)
        dz_ref[:, C_ZD + 2 * HALF:C_ZD + 3 * HALF] = (dud * zc_ref[:, C_ZD + HALF:C_ZD + 2 * HALF].astype(F32)).astype(BF16)

        cosE = jnp.concatenate([cp_ref[...], cc_ref[...], cn_ref[...]], axis=0)
        sinE = jnp.concatenate([sp_ref[...], sc_ref[...], sn_ref[...]], axis=0)
        k_ref[...] = _rope(colsE(C_K, C_K + 128), cosE, sinE).astype(BF16)
        v_ref[...] = colsE(C_V, C_V + 128).astype(BF16)
        dk_ref[...] = jnp.zeros_like(dk_ref)
        dv_ref[...] = jnp.zeros_like(dv_ref)
        q = jnp.concatenate([_rope(colsC(C_Q + 128 * j, C_Q + 128 * (j + 1)), cosE[HALO:], sinE[HALO:])
                             for j in range(4)], axis=1).astype(BF16)
        dO = dyC(2 * HALF, 3 * HALF).astype(BF16)
        lane_s = lax.broadcasted_iota(jnp.int32, (1, 128), 1)
        in_band, in_band_cur = _band_masks()
        sink_cols = [_sink_col(sinks_ref, h) for h in range(N_KV_HEADS)]
        for qb in range(r + 1):
            valid = in_band if qb else in_band_cur | (in_band & (i > 0))
            rows = slice(qb * WINDOW, (qb + 1) * WINDOW)
            band = slice(qb * WINDOW, qb * WINDOW + 2 * WINDOW)
            for h in range(N_KV_HEADS):
                hc = slice(h * HEAD_DIM, (h + 1) * HEAD_DIM)
                kh = k_ref[band, hc]
                vh_ = v_ref[band, hc]
                heads = [slice((h * Q_PER_KV + g) * HEAD_DIM, (h * Q_PER_KV + g + 1) * HEAD_DIM) for g in range(Q_PER_KV)]
                qs = jnp.concatenate([q[rows, hs] for hs in heads], axis=0)
                dos = jnp.concatenate([dO[rows, hs] for hs in heads], axis=0)
                probs, p_sink = _attn_probs(qs, kh, sink_cols[h], valid)
                dP = _dot_nt(dos, vh_)
                rsum = jnp.sum(probs * dP, axis=-1, keepdims=True)
                dS = (probs * (dP - rsum) * (HEAD_DIM ** -0.5)).astype(BF16)
                dk_ref[band, hc] += _dot_tn(dS, qs)
                dv_ref[band, hc] += _dot_tn(probs.astype(BF16), dos)
                if qb < r:
                    dqs = _dot(dS, kh)
                    dsk = -p_sink * rsum
                    for g in range(Q_PER_KV):
                        dq_ref[rows, heads[g]] = dqs[g * WINDOW:(g + 1) * WINDOW]
                        dsink_ref[...] += jnp.where(lane_s == h * Q_PER_KV + g, jnp.sum(dsk[g * WINDOW:(g + 1) * WINDOW]), 0.0)
        cosC, sinC = cc_ref[...], sc_ref[...]
        for j in range(4):
            dz_ref[:, C_Q + 128 * j:C_Q + 128 * (j + 1)] = _rope_t(dq_ref[:, 128 * j:128 * (j + 1)], cosC, sinC).astype(BF16)
        dz_ref[:, C_K:C_K + 128] = _rope_t(dk_ref[HALO:HALO + TB, :], cosC, sinC).astype(BF16)
        dz_ref[:, C_V:C_V + 128] = dv_ref[HALO:HALO + TB, :].astype(BF16)

    small = [((1, HALF), F32), ((1, HALF), F32), ((SG_GROUPS, SG_CHUNK, SG_CHUNK), F32), ((SG_CHUNK, 128), F32),
             ((32, HALF), F32), ((1, HALF), F32), ((1, HALF), F32), ((1, HALF), F32), ((1, 128), F32), ((8, HALF), F32)]
    outs = pl.pallas_call(
        body, name=name, grid=(nb,),
        in_specs=[cur, prev, nxt, dcur, dnxt, tcur, tcur, tprev, tprev, tnxt, tnxt] + _mixer_param_specs()
                 + [pl.BlockSpec(memory_space=pl.ANY)],
        out_specs=[pl.BlockSpec((TB, MIX_W), lambda i: (i, 1))] + [_full(s) for s, _ in small],
        out_shape=[jax.ShapeDtypeStruct((T, PROJ_PAD), BF16)] + [jax.ShapeDtypeStruct(s, d) for s, d in small],
        scratch_shapes=[pltpu.VMEM((RE, HALF), F32), pltpu.VMEM((RC, HALF), F32), pltpu.VMEM((RE, 128), BF16), pltpu.VMEM((RE, 128), BF16),
                        pltpu.VMEM((RE, 128), F32), pltpu.VMEM((RE, 128), F32), pltpu.VMEM((TB, HALF), F32),
                        pltpu.VMEM((8, TB + 2 * CV_PAD, HALF), F32), pltpu.VMEM((8, TB + CV_PAD, HALF), F32)],
        input_output_aliases={21: 0},
        compiler_params=_params("arbitrary"),
    )(proj, proj, proj, dys, dys, cos_t, sin_t, cos_t, sin_t, cos_t, sin_t, *mp, dproj)
    return outs


def _rope_tables(T):
    pos = jnp.arange(T, dtype=F32)
    inv_freq = 1.0 / (ROPE_THETA ** (jnp.arange(0, HEAD_DIM, 2, dtype=F32) / HEAD_DIM))
    ang = pos[:, None] * inv_freq[None, :]
    cos, sin = jnp.cos(ang), jnp.sin(ang)
    cos_t = jnp.concatenate([cos, cos, cos, cos], axis=1)
    sin_t = jnp.concatenate([-sin, sin, -sin, sin], axis=1)
    return cos_t, sin_t


def _mixer_params(l, sg_ln_g, sg_ln_b, sg_w, sg_b, cv_w, cv_b, cv_ln_g, cv_ln_b, attn_sinks, sc_w):
    sgb_t = jnp.zeros((SG_CHUNK, 128), F32).at[:, :SG_GROUPS].set(sg_b[l].T)
    cvw = jnp.zeros((32, HALF), F32).at[:CV_KERNEL].set(cv_w[l])
    scw = jnp.zeros((8, HALF), F32).at[:SC_KERNEL].set(sc_w[l])
    sinks = jnp.zeros((1, 128), F32).at[0, :N_Q_HEADS].set(attn_sinks[l])
    return [sg_ln_g[l][None], sg_ln_b[l][None], sg_w[l], sgb_t, cvw, cv_b[l][None], cv_ln_g[l][None], cv_ln_b[l][None], sinks, scw]


def _w_in_layout(w_in_g):
    cut = MIX_W - 2 * W_IN_SHARD
    return jnp.concatenate([w_in_g[2][cut:], w_in_g[3], jnp.zeros((MIX_W - GATE_W, D_MODEL), w_in_g.dtype),
                            w_in_g[0], w_in_g[1], w_in_g[2][:cut]], axis=0)


def _w_in_unlayout(dw):
    cut = MIX_W - 2 * W_IN_SHARD
    return jnp.stack([dw[MIX_W:MIX_W + W_IN_SHARD], dw[MIX_W + W_IN_SHARD:MIX_W + 2 * W_IN_SHARD],
                      jnp.concatenate([dw[MIX_W + 2 * W_IN_SHARD:], dw[:W_IN_SHARD - cut]], axis=0),
                      dw[W_IN_SHARD - cut:GATE_W]], axis=0)


def _device_step(x, tgt, norm_mix, norm_ffn, norm_final, mixer_params, w_in_p, wb_g, wo_g, wgu_g, wd_g):
    T = x.shape[0]
    tables = _rope_tables(T)
    saved = []
    for l in range(DEPTH):
        lw = dict(w_in=w_in_p[l], w_branch=wb_g[l], w_out=wo_g[l], w_gate_up=wgu_g[l], w_down=wd_g[l],
                  norm_mix=norm_mix[l][None], norm_ffn=norm_ffn[l][None], mixer=mixer_params[l], after=jnp.zeros((8, 128), F32))
        x, sv = _fwd_layer(l, x, lw, tables)
        saved.append((lw, sv))
    dx, dnf, loss = _final_loss(x, norm_final[None], tgt, 256, "final_loss")
    grads = [None] * DEPTH
    for l in reversed(range(DEPTH)):
        lw, sv = saved[l]
        dxm, g_ffn = _bwd_layer_ffn(l, dx, lw, sv)
        dx, g_mix = _bwd_layer_mix(l, dxm, lw, sv, tables)
        raw = {**g_ffn, **g_mix}
        grads[l] = {**raw, **_small_views(raw)}
    return loss, dx, dnf[0], grads


MIX_BLOCK = 256


def _fwd_layer(l, x, lw, tables):
    return _fwd_layer_rest(l, x, _fwd_layer_mix(l, x, lw, tables), lw)


def _fwd_layer_mix(l, x, lw, tables, between=None):
    proj, xn = _rms_mm(x, lw["norm_mix"], lw["w_in"], min(x.shape[0], 1024), 2176, f"proj{l}")
    if between is not None:
        between(proj, lw)
    return proj, xn, _mixers_fwd(proj, *tables, lw["mixer"], MIX_BLOCK, f"mixers_fwd{l}")


def _fwd_layer_rest(l, x, mixed, lw, between=None):
    proj, xn, ys = mixed
    TM = min(x.shape[0], 1024)
    xm, merged = _merge_fwd(x, ys, proj, lw["w_branch"], lw["w_out"], min(x.shape[0], 512), f"merge_fwd{l}")
    if between is not None:
        between(xm, lw)
    gu, hn = _rms_mm(xm, lw["norm_ffn"], lw["w_gate_up"], TM, GU_SHARD, f"ffn_up{l}")
    x_out = _ffn_down(xm, gu, lw["w_down"], min(x.shape[0], 512), f"ffn_down{l}")
    return x_out, (x, proj, xn, ys, xm, merged, gu, hn)


def _bwd_layer_ffn(l, dx, lw, sv, between=None):
    x_in, proj, xn, ys, xm, merged, gu, hn = sv
    T = dx.shape[0]
    tkk = min(T, 1024)
    gk = T // tkk
    dgu, act = _swiglu_bwd(dx, gu, lw["w_down"], 256, f"swiglu_bwd{l}", lw["after"])
    d_wd = _mm_tn(act, dx, (2, 1, gk), (tkk, D_FF // 2), lambda i, j, k: (k, i), (tkk, D_MODEL), lambda i, j, k: (k, 0),
                  (D_FF, D_MODEL), (D_FF // 2, D_MODEL), lambda i, j, k: (i, 0), f"dw_down{l}")
    d_wgu = _mm_tn(hn, dgu, (1, N_CHIPS, gk), (tkk, D_MODEL), lambda i, j, k: (k, 0), (tkk, GU_SHARD), lambda i, j, k: (k, j),
                   (N_CHIPS, D_MODEL, GU_SHARD), (None, D_MODEL, GU_SHARD), lambda i, j, k: (j, 0, 0), f"dw_gate_up{l}")
    if between is not None:
        between(dict(w_gate_up=d_wgu, w_down=d_wd), lw)
    dxm, d_nffn = _mm_nt_rmsbwd(dgu, lw["w_gate_up"], xm, lw["norm_ffn"], dx, min(T, 1024), GU_SHARD, f"ffn_up_bwd{l}")
    dys, dbr, dproj = _merge_bwd(dxm, ys, proj, lw["w_branch"], lw["w_out"], 256, f"merge_bwd{l}")
    d_wo = _mm_tn(merged, dxm, (2, 1, gk), (tkk, 512), lambda i, j, k: (k, i), (tkk, D_MODEL), lambda i, j, k: (k, 0),
                  (D_MODEL, D_MODEL), (512, D_MODEL), lambda i, j, k: (i, 0), f"dw_out{l}")
    d_wb = _mm_tn(ys, dbr, (N_BRANCH, 1, gk), (tkk, HALF), lambda i, j, k: (k, i), (tkk, D_MODEL), lambda i, j, k: (k, i),
                  (N_CHIPS, N_BRANCH, HALF, 256), (N_CHIPS, None, HALF, 256), lambda i, j, k: (0, i, 0, 0), f"dw_branch{l}", col_split=N_CHIPS)
    return (dxm, dys, dproj), dict(w_branch=d_wb, w_out=d_wo, w_gate_up=d_wgu, w_down=d_wd, norm_ffn=d_nffn)


def _bwd_layer_mix(l, carry, lw, sv, tables, between=None):
    dxm, dys, dproj = carry
    x_in, proj, xn, ys, xm, merged, gu, hn = sv
    T = dxm.shape[0]
    tkk = min(T, 1024)
    gk = T // tkk
    mb = _mixers_bwd(proj, dys, dproj, *tables, lw["mixer"], MIX_BLOCK, f"mixers_bwd{l}")
    dproj = mb[0]
    d_win = _mm_tn(dproj, xn, (PROJ_PAD // 2176, 1, gk), (tkk, 2176), lambda i, j, k: (k, i), (tkk, D_MODEL), lambda i, j, k: (k, 0),
                   (PROJ_PAD, D_MODEL), (2176, D_MODEL), lambda i, j, k: (i, 0), f"dw_in{l}")
    if between is not None:
        between(d_win, lw)
    dx, d_nmix = _mm_nt_rmsbwd(dproj, lw["w_in"], x_in, lw["norm_mix"], dxm, min(T, 1024), 2176, f"proj_bwd{l}")
    return dx, dict(w_in=d_win, norm_mix=d_nmix, sg_ln_g=mb[1], sg_ln_b=mb[2], sg_w=mb[3], sg_b=mb[4], cv_w=mb[5], cv_b=mb[6],
                    cv_ln_g=mb[7], cv_ln_b=mb[8], attn_sinks=mb[9], sc_w=mb[10])


ANY = pl.BlockSpec(memory_space=pl.ANY)
BIG = ("w_in", "w_branch", "w_out", "w_gate_up", "w_down")
HALF_SHAPE = {"w_in": (2, W_IN_SHARD // 2, D_MODEL), "w_branch": (2, 1024, 256), "w_out": (2, 128, D_MODEL),
              "w_gate_up": (2, 512, GU_SHARD), "w_down": (2, 352, D_MODEL)}
NB = len(BIG)


def _place():
    x, y, c = lax.axis_index("x"), lax.axis_index("y"), lax.axis_index("c")
    chips = [(1 - x, y), (x, 1 - y), (1 - x, 1 - y)]
    return x, y, c, 2 * x + y, chips, [2 * px + py for px, py in chips]


def _remote(src, dst, ssem, rsem, dev):
    return pltpu.make_async_remote_copy(src_ref=src, dst_ref=dst, send_sem=ssem, recv_sem=rsem, device_id=dev, device_id_type=MESH)


HBM_SPEC = pl.BlockSpec(memory_space=pltpu.HBM)
SEM_SPEC = pl.BlockSpec(memory_space=pltpu.SEMAPHORE)
DATAFLOW = pltpu.SideEffectType.DATAFLOW_SIDE_EFFECTING


def _ici_ends(kind, src, land, j, c, chip, chip_ids):
    if kind == "gather":
        return src.at[c], land.at[chip, c], land.at[chip_ids[j], c]
    return src.at[chip_ids[j]], land.at[chip], land.at[chip_ids[j]]


def _ici_start(kind, srcs, land_shapes, name):
    n = len(srcs)

    def body(*refs):
        src, land = refs[:n], refs[n:2 * n]
        ssem, rsem, token = refs[2 * n], refs[2 * n + 1], refs[-1]
        x, y, c, chip, chips, chip_ids = _place()
        for k in range(n):
            for j in range(3):
                s, d, _ = _ici_ends(kind, src[k], land[k], j, c, chip, chip_ids)
                _remote(s, d, ssem.at[3 * k + j], rsem.at[3 * k + j], (*chips[j], c)).start()
        token[...] = jnp.zeros_like(token)

    sem = pltpu.SemaphoreType.DMA((3 * n,))
    outs = pl.pallas_call(
        body, name=name,
        out_shape=(sem, sem, *[pltpu.HBM(s.shape, s.dtype) for s in srcs], *[pltpu.HBM(sh, BF16) for sh in land_shapes],
                   jax.ShapeDtypeStruct((8, 128), F32)),
        in_specs=[HBM_SPEC] * (2 * n),
        out_specs=(SEM_SPEC, SEM_SPEC, *[HBM_SPEC] * (2 * n), pl.BlockSpec(memory_space=pltpu.VMEM)),
        input_output_aliases={i: 2 + i for i in range(2 * n)},
        compiler_params=pltpu.CompilerParams(has_side_effects=DATAFLOW),
    )(*[pltpu.with_memory_space_constraint(s, pltpu.HBM) for s in srcs],
      *[pltpu.with_memory_space_constraint(lax.empty(sh, BF16), pltpu.HBM) for sh in land_shapes])
    return (kind, outs[0], outs[1], list(outs[2:2 + n]), list(outs[2 + n:2 + 2 * n])), outs[-1]


def _ici_wait(handle, after, name):
    kind, ssem_in, rsem_in, srcs, lands = handle
    n = len(srcs)

    def body(*refs):
        src, land = refs[:n], refs[n:2 * n]
        ssem, rsem = refs[2 * n], refs[2 * n + 1]
        x, y, c, chip, chips, chip_ids = _place()
        for k in range(n):
            for j in range(3):
                s, _, mine = _ici_ends(kind, src[k], land[k], j, c, chip, chip_ids)
                cp = _remote(s, mine, ssem.at[3 * k + j], rsem.at[3 * k + j], (*chips[j], c))
                cp.wait_send()
                cp.wait_recv()

    outs = pl.pallas_call(
        body, name=name, out_shape=[pltpu.HBM(t.shape, t.dtype) for t in srcs + lands],
        in_specs=[HBM_SPEC] * (2 * n) + [SEM_SPEC, SEM_SPEC, ANY], out_specs=[HBM_SPEC] * (2 * n),
        input_output_aliases={i: i for i in range(2 * n)},
        compiler_params=pltpu.CompilerParams(has_side_effects=DATAFLOW),
    )(*srcs, *lands, ssem_in, rsem_in, after)
    return list(outs[:n]), list(outs[n:])


def _ag_pair(shards, lands, name):
    n = len(shards)

    def body(*refs):
        ins, outs = refs[:n], refs[2 * n:3 * n]
        token = refs[3 * n]
        s_fwd, r_fwd, s_own, r_own = refs[3 * n + 1:]
        x, y, c, chip, chips, chip_ids = _place()
        sib = (x, y, 1 - c)
        cps = []
        for k in range(n):
            cp = _remote(ins[k], outs[k].at[chip], s_own.at[k], r_own.at[k], sib)
            cp.start()
            cps.append(cp)
            for j in range(3):
                got = outs[k].at[chip_ids[j], c]
                cp = _remote(got, got, s_fwd.at[k, j], r_fwd.at[k, j], sib)
                cp.start()
                cps.append(cp)
        for k in range(n):
            _remote(ins[k], outs[k].at[chip], s_own.at[k], r_own.at[k], sib).wait_recv()
            for j in range(3):
                got = outs[k].at[chip_ids[j], 1 - c]
                _remote(got, got, s_fwd.at[k, j], r_fwd.at[k, j], sib).wait_recv()
        for cp in cps:
            cp.wait_send()
        token[...] = jnp.zeros_like(token)

    sem, sem1 = pltpu.SemaphoreType.DMA((n, 3)), pltpu.SemaphoreType.DMA((n,))
    outs = pl.pallas_call(
        body, name=name, out_shape=[jax.ShapeDtypeStruct(t.shape, t.dtype) for t in lands] + [jax.ShapeDtypeStruct((8, 128), F32)],
        in_specs=[ANY] * (2 * n), out_specs=[ANY] * n + [pl.BlockSpec(memory_space=pltpu.VMEM)],
        input_output_aliases={n + k: k for k in range(n)},
        scratch_shapes=[sem, sem, sem1, sem1], compiler_params=pltpu.CompilerParams(has_side_effects=True),
    )(*shards, *lands)
    return list(outs[:n]), outs[n]


def _forward_plan(n):
    def plan(refs, c, chip, chip_ids):
        out = []
        for k in range(n):
            shard, land = refs[k], refs[n + k]
            out.append((shard, land.at[chip], land.at[chip]))
            out += [(land.at[q, c], land.at[q, c], land.at[q, 1 - c]) for q in chip_ids]
        return out
    return plan, 4 * n


def _swap_plan(n):
    def plan(refs, c, chip, chip_ids):
        return [(refs[k].at[q, 1 - c], refs[n + k].at[q], refs[n + k].at[q]) for k in range(n) for q in range(N_CHIPS)]
    return plan, N_CHIPS * n


def _d2d_start(arrays, new_shapes, plan_n, name):
    plan, n_copies = plan_n
    n = len(arrays) + len(new_shapes)

    def body(*refs):
        ssem, rsem, token = refs[n], refs[n + 1], refs[-1]
        x, y, c, chip, _, chip_ids = _place()
        for i, (s, d, _) in enumerate(plan(refs[:n], c, chip, chip_ids)):
            _remote(s, d, ssem.at[i], rsem.at[i], (x, y, 1 - c)).start()
        token[...] = jnp.zeros_like(token)

    sem = pltpu.SemaphoreType.DMA((n_copies,))
    args = [pltpu.with_memory_space_constraint(t, pltpu.HBM) for t in arrays] + \
           [pltpu.with_memory_space_constraint(lax.empty(sh, BF16), pltpu.HBM) for sh in new_shapes]
    outs = pl.pallas_call(
        body, name=name,
        out_shape=(sem, sem, *[pltpu.HBM(t.shape, t.dtype) for t in args], jax.ShapeDtypeStruct((8, 128), F32)),
        in_specs=[HBM_SPEC] * n, out_specs=(SEM_SPEC, SEM_SPEC, *[HBM_SPEC] * n, pl.BlockSpec(memory_space=pltpu.VMEM)),
        input_output_aliases={i: 2 + i for i in range(n)},
        compiler_params=pltpu.CompilerParams(has_side_effects=DATAFLOW),
    )(*args)
    return (plan, outs[0], outs[1], list(outs[2:2 + n])), outs[-1]


def _d2d_wait(handle, after, name):
    plan, ssem_in, rsem_in, arrays = handle
    n = len(arrays)

    def body(*refs):
        ssem, rsem = refs[n], refs[n + 1]
        x, y, c, chip, _, chip_ids = _place()
        for i, (s, _, mine) in enumerate(plan(refs[:n], c, chip, chip_ids)):
            cp = _remote(s, mine, ssem.at[i], rsem.at[i], (x, y, 1 - c))
            cp.wait_send()
            cp.wait_recv()

    outs = pl.pallas_call(
        body, name=name, out_shape=[pltpu.HBM(t.shape, t.dtype) for t in arrays],
        in_specs=[HBM_SPEC] * n + [SEM_SPEC, SEM_SPEC, ANY], out_specs=[HBM_SPEC] * n,
        input_output_aliases={i: i for i in range(n)},
        compiler_params=pltpu.CompilerParams(has_side_effects=DATAFLOW),
    )(*arrays, ssem_in, rsem_in, after)
    return list(outs)


def _rs_pair(grads, name):
    n_arr = len(grads)

    def body(*refs):
        ins, got = refs[:n_arr], refs[n_arr:2 * n_arr]
        ssem, rsem = refs[2 * n_arr:]
        x, y, c, _, _, _ = _place()
        sib = (x, y, 1 - c)
        sends = []
        for k in reversed(range(n_arr)):
            for q in range(N_CHIPS):
                cp = _remote(ins[k].at[q, 1 - c], got[k].at[q], ssem.at[k, q], rsem.at[k, q], sib)
                cp.start()
                sends.append(cp)
        for k in range(n_arr):
            for q in range(N_CHIPS):
                _remote(got[k].at[q], got[k].at[q], ssem.at[k, q], rsem.at[k, q], sib).wait_recv()
        for cp in sends:
            cp.wait_send()

    shp = [jax.ShapeDtypeStruct((N_CHIPS,) + g.shape[2:], BF16) for g in grads]
    sem = pltpu.SemaphoreType.DMA((n_arr, N_CHIPS))
    outs = pl.pallas_call(
        body, name=name, out_shape=shp, in_specs=[ANY] * n_arr, out_specs=[ANY] * n_arr,
        scratch_shapes=[sem, sem], compiler_params=pltpu.CompilerParams(has_side_effects=True),
    )(*grads)
    return list(outs)


def _rs_share(bufs, name):
    n = len(bufs)

    def body(*refs):
        outs = refs[n:2 * n]
        ssem, rsem = refs[2 * n:]
        x, y, c, _, _, _ = _place()
        sib = (x, y, 1 - c)
        sends = []
        for k in range(n):
            for l in range(DEPTH):
                cp = _remote(outs[k].at[l, c], outs[k].at[l, c], ssem.at[k, l], rsem.at[k, l], sib)
                cp.start()
                sends.append(cp)
        for k in range(n):
            for l in range(DEPTH):
                dst = outs[k].at[l, 1 - c]
                _remote(dst, dst, ssem.at[k, l], rsem.at[k, l], sib).wait_recv()
        for cp in sends:
            cp.wait_send()

    sem = pltpu.SemaphoreType.DMA((n, DEPTH))
    outs = pl.pallas_call(
        body, name=name, out_shape=[jax.ShapeDtypeStruct(b.shape, b.dtype) for b in bufs], in_specs=[ANY] * n, out_specs=[ANY] * n,
        input_output_aliases={k: k for k in range(n)},
        scratch_shapes=[sem, sem], compiler_params=pltpu.CompilerParams(has_side_effects=True),
    )(*bufs)
    return list(outs)


def _piece(src, idx, rows, width=128, align=1, transposed=False):
    return dict(src=src, idx=idx, rows=rows, width=width, align=align, transposed=transposed)


def _all_reduce_pieces(inputs, pieces, out_shapes, writes, name):
    n_in, n_out = len(inputs), len(out_shapes)
    offs, R = [], 0
    for p in pieces:
        R = -(-R // p["align"]) * p["align"]
        offs.append(R)
        R += p["rows"]
    R = -(-R // 8) * 8

    def body(*refs):
        ins, outs, token_ref = refs[:n_in], refs[n_in:n_in + n_out], refs[n_in + n_out]
        pair_ref, chip_ref, sum_ref, ssem, rsem = refs[n_in + n_out + 1:]
        token_ref[...] = jnp.zeros_like(token_ref)
        x, y, c, chip, chips, chip_ids = _place()
        pair_ref[c] = jnp.zeros((R, 128), F32)
        for p, off in zip(pieces, offs):
            v = ins[p["src"]][...].T[p["idx"]] if p["transposed"] else ins[p["src"]][p["idx"]]
            pair_ref[c, off:off + p["rows"], 0:p["width"]] = v
        mine = _remote(pair_ref.at[c], pair_ref.at[c], ssem.at[3], rsem.at[3], (x, y, 1 - c))
        mine.start()
        _remote(pair_ref.at[1 - c], pair_ref.at[1 - c], ssem.at[3], rsem.at[3], (x, y, 1 - c)).wait_recv()
        chip_ref[chip] = pair_ref[0] + pair_ref[1]
        cps = [_remote(chip_ref.at[chip], chip_ref.at[chip], ssem.at[j], rsem.at[j], (*chips[j], c)) for j in range(3)]
        for cp in cps:
            cp.start()
        for j in range(3):
            slot = chip_ref.at[chip_ids[j]]
            _remote(slot, slot, ssem.at[j], rsem.at[j], (*chips[j], c)).wait_recv()
        acc = chip_ref[0]
        for s in range(1, N_CHIPS):
            acc = acc + chip_ref[s]
        sum_ref[...] = acc
        for o, idx, p in writes:
            outs[o][idx] = sum_ref[offs[p]:offs[p] + pieces[p]["rows"], 0:pieces[p]["width"]]
        for cp in cps + [mine]:
            cp.wait_send()

    vm = pl.BlockSpec(memory_space=pltpu.VMEM)
    outs = pl.pallas_call(
        body, name=name, out_shape=[jax.ShapeDtypeStruct(s, F32) for s in out_shapes] + [jax.ShapeDtypeStruct((8, 128), F32)],
        in_specs=[vm] * n_in, out_specs=[vm] * (n_out + 1),
        scratch_shapes=[pltpu.VMEM((2, R, 128), F32), pltpu.VMEM((N_CHIPS, R, 128), F32), pltpu.VMEM((R, 128), F32),
                        pltpu.SemaphoreType.DMA((4,)), pltpu.SemaphoreType.DMA((4,))],
        compiler_params=pltpu.CompilerParams(vmem_limit_bytes=VMEM_LIMIT),
    )(*inputs)
    return list(outs[:n_out]), outs[n_out]


def _lanes(width):
    return [slice(k, min(k + 128, width)) for k in range(0, width, 128)]


def _gather_small_weights(cvw_z, scw_z):
    pieces, writes = [], []
    for i, arr in enumerate((cvw_z, scw_z)):
        for l in range(DEPTH):
            for ln in _lanes(HALF):
                writes.append((i, (l, slice(None), ln), len(pieces)))
                pieces.append(_piece(i, (l, slice(None), ln), arr.shape[1], align=8))
    (cvw, scw), tok = _all_reduce_pieces([cvw_z, scw_z], pieces, [cvw_z.shape, scw_z.shape], writes, "ag_small")
    return cvw, scw, tok


SMALL_RAW = dict(norm_mix=(1, D_MODEL), norm_ffn=(1, D_MODEL), sg_ln_g=(1, HALF), sg_ln_b=(1, HALF), cv_b=(1, HALF), cv_ln_g=(1, HALF),
                 cv_ln_b=(1, HALF))


def _all_reduce_small_grads(raw, d_nfinal, loss):
    names = list(SMALL_RAW) + ["attn_sinks", "sg_b", "sc_w", "cv_w", "sg_w"]
    out_shape = dict(norm_mix=(DEPTH, D_MODEL), norm_ffn=(DEPTH, D_MODEL), sg_ln_g=(DEPTH, HALF), sg_ln_b=(DEPTH, HALF), cv_b=(DEPTH, HALF),
                     cv_ln_g=(DEPTH, HALF), cv_ln_b=(DEPTH, HALF), attn_sinks=(DEPTH, N_Q_HEADS), sg_b=(DEPTH, SG_GROUPS, SG_CHUNK),
                     sc_w=(DEPTH, SC_KERNEL, HALF), cv_w=(DEPTH, CV_KERNEL, HALF), sg_w=(DEPTH, SG_GROUPS, SG_CHUNK, SG_CHUNK))
    inputs, pieces, writes = [], [], []

    def add(src, idx, rows, out, out_idx, **kw):
        writes.append((names.index(out) if out in names else out, out_idx, len(pieces)))
        pieces.append(_piece(src, idx, rows, **kw))

    for l in range(DEPTH):
        row = slice(l, l + 1)
        for n, (_, width) in SMALL_RAW.items():
            inputs.append(raw[l][n])
            for ln in _lanes(width):
                add(len(inputs) - 1, (slice(0, 1), ln), 1, n, (row, ln))
        inputs.append(raw[l]["attn_sinks"])
        add(len(inputs) - 1, (slice(0, 1), slice(0, N_Q_HEADS)), 1, "attn_sinks", (row, slice(None)), width=N_Q_HEADS)
    for l in range(DEPTH):
        inputs.append(raw[l]["sg_b"])
        add(len(inputs) - 1, (slice(0, SG_GROUPS), slice(None)), SG_GROUPS, "sg_b", (l,), align=8, transposed=True)
        inputs.append(raw[l]["sc_w"])
        for ln in _lanes(HALF):
            add(len(inputs) - 1, (slice(0, SC_KERNEL), ln), SC_KERNEL, "sc_w", (l, slice(None), ln), align=8)
        inputs.append(raw[l]["cv_w"])
        for ln in _lanes(HALF):
            add(len(inputs) - 1, (slice(0, CV_KERNEL), ln), CV_KERNEL, "cv_w", (l, slice(None), ln), align=8)
        inputs.append(raw[l]["sg_w"])
        for g in range(SG_GROUPS):
            add(len(inputs) - 1, (g,), SG_CHUNK, "sg_w", (l, g), align=8)
    n_names = len(names)
    inputs.append(d_nfinal)
    for ln in _lanes(D_MODEL):
        add(len(inputs) - 1, (slice(0, 1), ln), 1, n_names, (slice(0, 1), ln))
    inputs.append(loss)
    add(len(inputs) - 1, (slice(0, 1), slice(None)), 1, n_names + 1, (slice(0, 1), slice(None)))
    outs, tok = _all_reduce_pieces(inputs, pieces, [out_shape[n] for n in names] + [(1, D_MODEL), (1, 128)], writes, "ar_small")
    return dict(zip(names, outs[:n_names])), outs[n_names], outs[n_names + 1], tok


def _small_views(raw):
    v = {n: raw[n][0] for n in SMALL_RAW}
    v.update(sg_w=raw["sg_w"], sg_b=raw["sg_b"][:, :SG_GROUPS].T, cv_w=raw["cv_w"][:CV_KERNEL],
             attn_sinks=raw["attn_sinks"][0, :N_Q_HEADS], sc_w=raw["sc_w"][:SC_KERNEL])
    return v


def _row_tile(rows, cols, n_arrays):
    budget = 36 * 1024 * 1024 // (n_arrays * 2 * cols * 4)
    tiles = [t for t in range(16, min(rows, budget) + 1, 16) if rows % t == 0]
    assert tiles, (rows, cols)
    return tiles[-1]


def _add_pairs(g, got, place, name):
    _, _, rows, cols = g.shape
    tr = _row_tile(rows, cols, 3)

    def body(place_ref, a_ref, b_ref, o_ref):
        del place_ref
        o_ref[...] = (a_ref[...].astype(F32) + b_ref[...].astype(F32)).astype(BF16)

    spec = pl.BlockSpec((None, tr, cols), lambda q, i, p: (q, i, 0))
    grid_spec = pltpu.PrefetchScalarGridSpec(
        num_scalar_prefetch=1, grid=(N_CHIPS, rows // tr),
        in_specs=[pl.BlockSpec((None, None, tr, cols), lambda q, i, p: (q, p[1], i, 0)), spec], out_specs=spec)
    return pl.pallas_call(body, name=name, grid_spec=grid_spec, out_shape=jax.ShapeDtypeStruct((N_CHIPS, rows, cols), BF16),
                          compiler_params=_params("parallel", "parallel"))(place, g, got)


def _sum_chips(own, recv, place, l, buf, name, after):
    _, rows, cols = own.shape
    tr = _row_tile(rows, cols, 4)

    def body(place_ref, own_ref, recv_ref, *rest):
        chip = place_ref[0]
        acc = own_ref[...].astype(F32)
        for j in range(1, N_CHIPS):
            acc = acc + recv_ref[lax.rem(chip + j, N_CHIPS)].astype(F32)
        rest[-1][...] = acc

    in_specs = [pl.BlockSpec((None, tr, cols), lambda i, p: (p[0], i, 0)), pl.BlockSpec((N_CHIPS, tr, cols), lambda i, p: (0, i, 0)), ANY]
    args = [place, own, recv, after]
    aliases = {}
    if buf is not None:
        in_specs.append(ANY)
        args.append(buf)
        aliases = {4: 0}
    grid_spec = pltpu.PrefetchScalarGridSpec(
        num_scalar_prefetch=1, grid=(rows // tr,), in_specs=in_specs,
        out_specs=pl.BlockSpec((None, None, tr, cols), lambda i, p: (l, p[1], i, 0)))
    return pl.pallas_call(body, name=name, grid_spec=grid_spec, out_shape=jax.ShapeDtypeStruct((DEPTH, 2, rows, cols), F32),
                          input_output_aliases=aliases, compiler_params=_params("parallel"))(*args)


def _adamw(w, g, m, v, name):
    shape = w.shape
    lead, (rows, cols) = shape[:-2], shape[-2:]
    tr = _row_tile(rows, cols, 8)

    def body(w_ref, g_ref, m_ref, v_ref, go_ref, d_ref, mo_ref, vo_ref):
        gv = g_ref[...]
        go_ref[...] = gv
        mn = ADAM_B1 * m_ref[...] + (1.0 - ADAM_B1) * gv
        vn = ADAM_B2 * v_ref[...] + (1.0 - ADAM_B2) * (gv * gv)
        m_hat = mn / (1.0 - ADAM_B1 ** ADAM_STEP)
        v_hat = vn / (1.0 - ADAM_B2 ** ADAM_STEP)
        d_ref[...] = -ADAM_LR * (m_hat / (jnp.sqrt(v_hat) + ADAM_EPS) + ADAM_WD * w_ref[...])
        mo_ref[...] = mn
        vo_ref[...] = vn

    spec = pl.BlockSpec((None,) * len(lead) + (tr, cols), lambda *idx: (*idx, 0))
    grid = lead + (rows // tr,)
    return list(pl.pallas_call(body, name=name, grid=grid, in_specs=[spec] * 4, out_specs=[spec] * 4,
                               out_shape=[jax.ShapeDtypeStruct(shape, F32)] * 4,
                               compiler_params=_params(*(["parallel"] * len(grid))))(w, g, m, v))


def _adamw_small(ws, gs, ms, vs, name):
    n = len(ws)

    def body(*refs):
        for i in range(n):
            gv = refs[n + i][...]
            mn = ADAM_B1 * refs[2 * n + i][...] + (1.0 - ADAM_B1) * gv
            vn = ADAM_B2 * refs[3 * n + i][...] + (1.0 - ADAM_B2) * (gv * gv)
            m_hat = mn / (1.0 - ADAM_B1 ** ADAM_STEP)
            v_hat = vn / (1.0 - ADAM_B2 ** ADAM_STEP)
            refs[4 * n + i][...] = -ADAM_LR * (m_hat / (jnp.sqrt(v_hat) + ADAM_EPS) + ADAM_WD * refs[i][...])
            refs[5 * n + i][...] = mn
            refs[6 * n + i][...] = vn

    vm = pl.BlockSpec(memory_space=pltpu.VMEM)
    outs = pl.pallas_call(body, name=name, out_shape=[jax.ShapeDtypeStruct(t.shape, F32) for t in ws] * 3,
                          in_specs=[vm] * (4 * n), out_specs=[vm] * (3 * n),
                          compiler_params=pltpu.CompilerParams(vmem_limit_bytes=VMEM_LIMIT))(*ws, *gs, *ms, *vs)
    return outs[:n], outs[n:2 * n], outs[2 * n:]


SMALL = ("norm_mix", "sg_ln_g", "sg_ln_b", "sg_w", "sg_b", "cv_w", "cv_b", "cv_ln_g", "cv_ln_b", "attn_sinks", "sc_w", "norm_ffn", "norm_final")
ORDER = ("norm_mix", "w_in", "sg_ln_g", "sg_ln_b", "sg_w", "sg_b", "cv_w", "cv_b", "cv_ln_g", "cv_ln_b", "attn_sinks", "sc_w",
         "w_branch", "w_out", "norm_ffn", "w_gate_up", "w_down", "norm_final")


def kernel(x, norm_mix, w_in, sg_ln_g, sg_ln_b, sg_w, sg_b, cv_w, cv_b, cv_ln_g, cv_ln_b, attn_sinks, sc_w, w_branch, w_out, norm_ffn, w_gate_up, w_down, norm_final, loss_target, m_norm_mix, m_w_in, m_sg_ln_g, m_sg_ln_b, m_sg_w, m_sg_b, m_cv_w, m_cv_b, m_cv_ln_g, m_cv_ln_b, m_attn_sinks, m_sc_w, m_w_branch, m_w_out, m_norm_ffn, m_w_gate_up, m_w_down, m_norm_final, v_norm_mix, v_w_in, v_sg_ln_g, v_sg_ln_b, v_sg_w, v_sg_b, v_cv_w, v_cv_b, v_cv_ln_g, v_cv_ln_b, v_attn_sinks, v_sc_w, v_w_branch, v_w_out, v_norm_ffn, v_w_gate_up, v_w_down, v_norm_final):
    W = dict(norm_mix=norm_mix, w_in=w_in, sg_ln_g=sg_ln_g, sg_ln_b=sg_ln_b, sg_w=sg_w, sg_b=sg_b, cv_w=cv_w, cv_b=cv_b, cv_ln_g=cv_ln_g,
             cv_ln_b=cv_ln_b, attn_sinks=attn_sinks, sc_w=sc_w, w_branch=w_branch, w_out=w_out, norm_ffn=norm_ffn, w_gate_up=w_gate_up,
             w_down=w_down, norm_final=norm_final)
    M = dict(norm_mix=m_norm_mix, w_in=m_w_in, sg_ln_g=m_sg_ln_g, sg_ln_b=m_sg_ln_b, sg_w=m_sg_w, sg_b=m_sg_b, cv_w=m_cv_w, cv_b=m_cv_b,
             cv_ln_g=m_cv_ln_g, cv_ln_b=m_cv_ln_b, attn_sinks=m_attn_sinks, sc_w=m_sc_w, w_branch=m_w_branch, w_out=m_w_out,
             norm_ffn=m_norm_ffn, w_gate_up=m_w_gate_up, w_down=m_w_down, norm_final=m_norm_final)
    V = dict(norm_mix=v_norm_mix, w_in=v_w_in, sg_ln_g=v_sg_ln_g, sg_ln_b=v_sg_ln_b, sg_w=v_sg_w, sg_b=v_sg_b, cv_w=v_cv_w, cv_b=v_cv_b,
             cv_ln_g=v_cv_ln_g, cv_ln_b=v_cv_ln_b, attn_sinks=v_attn_sinks, sc_w=v_sc_w, w_branch=v_w_branch, w_out=v_w_out,
             norm_ffn=v_norm_ffn, w_gate_up=v_w_gate_up, w_down=v_w_down, norm_final=v_norm_final)
    mx, my, mc = lax.axis_index("x"), lax.axis_index("y"), lax.axis_index("c")
    chip = 2 * mx + my

    place = jnp.stack([chip, mc]).astype(jnp.int32)
    tables = _rope_tables(x.shape[1])
    land_shapes = [(N_CHIPS,) + HALF_SHAPE[n] for n in BIG]
    part_shapes = {n: (N_CHIPS,) + HALF_SHAPE[n][1:] for n in BIG}

    T_ = lambda t: jnp.swapaxes(t, 1, 2)
    Wt, Mt, Vt = ({**t, "w_in": T_(t["w_in"])} for t in (W, M, V))

    def shards_of(l, tok):
        return [(Wt[n][l] + tok[0, 0]).astype(BF16).reshape(HALF_SHAPE[n]) for n in BIG]

    def finish_gather(tag, handle, after):
        srcs, lands = _ici_wait(handle, after, f"ag_wait{tag}")
        return _ag_pair(srcs, lands, f"ag_pair{tag}")[0]

    def mix_weights(l, g_in):
        return dict(w_in=_w_in_layout(g_in[0].reshape(N_CHIPS, W_IN_SHARD, D_MODEL)), norm_mix=norm_mix[l][None], norm_ffn=norm_ffn[l][None],
                    mixer=_mixer_params(l, sg_ln_g, sg_ln_b, sg_w, sg_b, cvw_full, cv_b, cv_ln_g, cv_ln_b, attn_sinks, scw_full))

    def rest_weights(lw, g_rest):
        G = dict(zip(BIG[1:], g_rest))
        lw.update(w_branch=G["w_branch"].reshape(N_CHIPS, N_BRANCH, HALF, 256), w_out=G["w_out"].reshape(D_MODEL, D_MODEL),
                  w_gate_up=G["w_gate_up"].reshape(N_CHIPS, D_MODEL, GU_SHARD), w_down=G["w_down"].reshape(D_FF, D_MODEL))

    def shard_major(g):
        t = dict(g)
        if "w_in" in t:
            t["w_in"] = _w_in_unlayout(t["w_in"])
        return {n: t[n].reshape((N_CHIPS,) + HALF_SHAPE[n]) for n in BIG if n in t}

    zero_tok = jnp.zeros((8, 128), F32)
    south = (mc == 0).astype(F32)
    cvw_z = lax.dynamic_update_slice(jnp.zeros((DEPTH, CV_KERNEL, HALF), F32), cv_w * south, (0, 0, chip * 128))
    scw_z = lax.dynamic_update_slice(jnp.zeros((DEPTH, SC_KERNEL, HALF), F32), sc_w * south, (0, 0, chip * 128))
    cvw_full, scw_full, tok = _gather_small_weights(cvw_z, scw_z)
    handles = []
    for l in range(DEPTH):
        for tag, sl in (("in", slice(0, 1)), ("rest", slice(1, NB))):
            h, tok = _ici_start("gather", shards_of(l, tok)[sl], land_shapes[sl], f"ag_start{l}{tag}")
            handles.append(h)
    pending = {}

    def behind(l, key):
        def order(lw, token):
            if key == "mixer":
                lw["mixer"] = [lw["mixer"][0] + token[0, 0]] + lw["mixer"][1:]
            else:
                lw[key] = lw[key] + token[0, 0]
        return order

    def early_pair(tag, handle, order):
        def between(after, lw):
            srcs, lands = _ici_wait(handle, after, f"ag_wait{tag}")
            pending[tag], token = _d2d_start(srcs + lands, [], _forward_plan(len(srcs)), f"ag_pair_start{tag}")
            order(lw, token)
        return between

    def finish_pair(tag, after):
        arrays = _d2d_wait(pending.pop(tag), after, f"ag_pair_wait{tag}")
        return arrays[len(arrays) // 2:]

    lw0 = mix_weights(0, finish_gather("0in", handles[0], tok))
    mixed = _fwd_layer_mix(0, x[0], lw0, tables)
    rest_weights(lw0, finish_gather("0rest", handles[1], mixed[2]))
    x1, sv0 = _fwd_layer_rest(0, x[0], mixed, lw0, early_pair("1in", handles[2], behind(0, "norm_ffn")))
    lw1 = mix_weights(1, finish_pair("1in", x1))
    mixed = _fwd_layer_mix(1, x1, lw1, tables, early_pair("1rest", handles[3], behind(1, "mixer")))
    rest_weights(lw1, finish_pair("1rest", mixed[2]))
    x2, sv1 = _fwd_layer_rest(1, x1, mixed, lw1)
    dx, d_nfinal, loss = _final_loss(x2, norm_final[None], loss_target[0], 256, "final_loss")

    lw1["after"] = zero_tok
    carry, g_ffn1 = _bwd_layer_ffn(1, dx, lw1, sv1)
    g1 = shard_major(g_ffn1)
    names_f = list(g1)
    h_swap, tok = _d2d_start([g1[n] for n in names_f], [part_shapes[n] for n in names_f], _swap_plan(len(names_f)), "rs_pair_start1")
    behind(1, "mixer")(lw1, tok)
    def early_swap(tag):
        def between(d_win, lw):
            g = shard_major({"w_in": d_win})["w_in"]
            pending[tag], token = _d2d_start([g], [part_shapes["w_in"]], _swap_plan(1), f"rs_pair_start{tag}")
            behind(None, "norm_mix")(lw, token)
        return between

    dx, g_mix1 = _bwd_layer_mix(1, carry, lw1, sv1, tables, early_swap("1in"))
    swapped = _d2d_wait(h_swap, dx, "rs_pair_wait1")
    own_in, got_in = _d2d_wait(pending.pop("1in"), dx, "rs_pair_wait1in")
    names1 = ["w_in"] + names_f
    own1 = [own_in] + swapped[:len(names_f)]
    got1 = [got_in] + swapped[len(names_f):]
    part1 = [_add_pairs(own1[k], got1[k], place, f"rs_add1_{n}") for k, n in enumerate(names1)]
    hr1, tok = _ici_start("scatter", part1, [part_shapes[n] for n in names1], "rs_start1")

    def early_ffn_swap(grads, lw):
        g = shard_major(grads)
        pending["0ffn"], token = _d2d_start([g[n] for n in g], [part_shapes[n] for n in g], _swap_plan(len(g)), "rs_pair_start0ffn")
        behind(None, "norm_ffn")(lw, token)

    lw0["after"] = tok
    carry, g_ffn0 = _bwd_layer_ffn(0, dx, lw0, sv0, early_ffn_swap)
    g0 = shard_major({n: g_ffn0[n] for n in ("w_branch", "w_out")})
    names_a = list(g0) + ["w_gate_up", "w_down"]
    swapped = _d2d_wait(pending.pop("0ffn"), g_ffn0["w_branch"], "rs_pair_wait0ffn")
    own_a = [g0[n] for n in g0] + swapped[:2]
    got_a = _rs_pair([g0[n] for n in g0], "rs_pair0a") + swapped[2:]
    part_a = [_add_pairs(own_a[k], got_a[k], place, f"rs_add0a_{n}") for k, n in enumerate(names_a)]
    _, recv1 = _ici_wait(hr1, part_a[0], "rs_wait1")
    hra, tok = _ici_start("scatter", part_a, [part_shapes[n] for n in names_a], "rs_start0a")

    lw0["mixer"] = [lw0["mixer"][0] + tok[0, 0]] + lw0["mixer"][1:]
    dx, g_mix0 = _bwd_layer_mix(0, carry, lw0, sv0, tables, early_swap("0in"))
    _, recv_a = _ici_wait(hra, dx, "rs_wait0a")

    small_red, nf_red, loss_red, tok = _all_reduce_small_grads([{**g_ffn0, **g_mix0}, {**g_ffn1, **g_mix1}], d_nfinal, loss)
    small_red["norm_final"] = nf_red
    loss_out = loss_red[0, 0]
    for n in ("cv_w", "sc_w"):
        small_red[n] = lax.dynamic_slice_in_dim(small_red[n], chip * 128, 128, axis=2)

    own_in, got_in = _d2d_wait(pending.pop("0in"), tok, "rs_pair_wait0in")
    names_b, part_b = ["w_in"], [_add_pairs(own_in, got_in, place, "rs_add0b_w_in")]
    hrb, tok = _ici_start("scatter", part_b, [part_shapes[n] for n in names_b], "rs_start0b")
    bufs = {n: _sum_chips(part1[k], recv1[k], place, 1, None, f"rs_sum1_{n}", tok) for k, n in enumerate(names1)}
    for k, n in enumerate(names_a):
        bufs[n] = _sum_chips(part_a[k], recv_a[k], place, 0, bufs[n], f"rs_sum0_{n}", tok)
    shared = dict(zip(names_a, _rs_share([bufs[n] for n in names_a], "rs_share_a")))
    upd = {}
    for n in names_a:
        red = shared[n].reshape(W[n].shape)
        upd[n] = _adamw(W[n], red, M[n], V[n], f"adamw_{n}")
    two_d = lambda t: t[None] if t.ndim == 1 else t
    small_upd = _adamw_small(*([two_d(t[n]) for n in SMALL] for t in (W, small_red, M, V)), "adamw_small")
    for n, d, mo, vo in zip(SMALL, *small_upd):
        upd[n] = [t.reshape(W[n].shape) for t in (small_red[n], d, mo, vo)]

    _, recv_b = _ici_wait(hrb, upd[names_a[-1]][1], "rs_wait0b")
    for k, n in enumerate(names_b):
        bufs[n] = _sum_chips(part_b[k], recv_b[k], place, 0, bufs[n], f"rs_sum0_{n}", tok)
    shared = dict(zip(names_b, _rs_share([bufs[n] for n in names_b], "rs_share_b")))
    for n in names_b:
        red = shared[n].reshape(Wt[n].shape)
        upd[n] = [T_(t) for t in _adamw(Wt[n], red, Mt[n], Vt[n], f"adamw_{n}")]

    out = [loss_out, dx[None]]
    for k in range(4):
        out += [upd[n][k] for n in ORDER]
    return tuple(out)
```
